```python
import math
import jax, jax.numpy as jnp
from jax import lax
import numpy as np

D_MODEL = 2048
BATCH = 8
SEQ = 2048
DEPTH = 4

CHUNK = 64
D_MIX = D_MODEL
D_S5 = D_MIX // 2
D_CONV = D_MIX - D_S5
S5_P = 16
S5_G = D_S5 // S5_P
S5_N = 64
CONV_HEADS = 8
CONV_W = 3
D_IN = D_S5 + 3 * D_CONV
D_FF = 5632
LN_EPS = 1e-5
RMS_EPS = 1e-6
DEEPNORM_ALPHA = (2.0 * DEPTH) ** 0.25
DEEPNORM_BETA = (8.0 * DEPTH) ** -0.25

kernel_name = "hybrid_s5_shortconv_macaron_deepnorm"


def layer_norm(x, g, b):
    xf = x.astype(jnp.float32)
    mu = jnp.mean(xf, axis=-1, keepdims=True)
    xc = xf - mu
    var = jnp.mean(xc * xc, axis=-1, keepdims=True)
    y = xc * lax.rsqrt(var + LN_EPS)
    return (y * g.astype(jnp.float32) + b.astype(jnp.float32)).astype(x.dtype)


def rms_norm(x, g):
    xf = x.astype(jnp.float32)
    y = xf * lax.rsqrt(jnp.mean(xf * xf, axis=-1, keepdims=True) + RMS_EPS)
    return (y * g.astype(jnp.float32)).astype(x.dtype)


def swiglu(x, w_gate, w_up, w_down):
    return (jax.nn.silu(x @ w_gate) * (x @ w_up)) @ w_down


def s5_mixer(u, lam_re, lam_im, log_dt, b_re, b_im, c_re, c_im, d):
    bsz, seq, _ = u.shape
    f32 = jnp.float32
    uf = u.astype(f32).reshape(bsz, seq, S5_G, S5_P)
    lre = lam_re.astype(f32)
    lim = lam_im.astype(f32)
    dt = jnp.exp(log_dt.astype(f32))[:, None]
    mag = jnp.exp(lre * dt)
    ang = lim * dt
    ab_re = mag * jnp.cos(ang)
    ab_im = mag * jnp.sin(ang)
    den = lre * lre + lim * lim
    nr = ab_re - 1.0
    ni = ab_im
    q_re = (nr * lre + ni * lim) / den
    q_im = (ni * lre - nr * lim) / den
    br = b_re.astype(f32)
    bi = b_im.astype(f32)
    bb_re = q_re[..., None] * br - q_im[..., None] * bi
    bb_im = q_re[..., None] * bi + q_im[..., None] * br
    bu_re = jnp.einsum('blgp,gnp->blgn', uf, bb_re)
    bu_im = jnp.einsum('blgp,gnp->blgn', uf, bb_im)
    a_re = jnp.broadcast_to(ab_re, bu_re.shape)
    a_im = jnp.broadcast_to(ab_im, bu_im.shape)

    def combine(e1, e2):
        a1r, a1i, b1r, b1i = e1
        a2r, a2i, b2r, b2i = e2
        return (a1r * a2r - a1i * a2i,
                a1r * a2i + a1i * a2r,
                a2r * b1r - a2i * b1i + b2r,
                a2r * b1i + a2i * b1r + b2i)

    _, _, s_re, s_im = lax.associative_scan(combine, (a_re, a_im, bu_re, bu_im), axis=1)
    y = (jnp.einsum('blgn,gpn->blgp', s_re, c_re.astype(f32))
         - jnp.einsum('blgn,gpn->blgp', s_im, c_im.astype(f32))
         + d.astype(f32) * uf)
    return y.reshape(bsz, seq, S5_G * S5_P).astype(u.dtype)


def causal_depthwise_conv(z, w, b):
    seq = z.shape[1]
    zp = jnp.pad(z, ((0, 0), (CONV_W - 1, 0), (0, 0)))
    out = b
    for k in range(CONV_W):
        out = out + w[k] * zp[:, k:k + seq]
    return out


def hybrid_mixer(x, w_in, lam_re, lam_im, log_dt, b_re, b_im, c_re, c_im, d,
                 w_glu, conv_w, conv_b, g_s5, g_conv, w_out):
    proj = x @ w_in
    u = proj[..., :D_S5]
    gate_b = proj[..., D_S5:D_S5 + D_CONV]
    gate_c = proj[..., D_S5 + D_CONV:D_S5 + 2 * D_CONV]
    h = proj[..., D_S5 + 2 * D_CONV:]
    y = jax.nn.gelu(s5_mixer(u, lam_re, lam_im, log_dt, b_re, b_im, c_re, c_im, d))
    y = y * jax.nn.sigmoid(y @ w_glu)
    y = rms_norm(y, g_s5)
    z = gate_b * causal_depthwise_conv(gate_c * h, conv_w, conv_b)
    z = rms_norm(z, g_conv)
    return jnp.concatenate([y, z], axis=-1) @ w_out


def _fwd_setup_inputs(seed: int = 0) -> dict:
    key = jax.random.key(seed)
    ks = jax.random.split(key, 32)
    f32 = jnp.float32

    def nrm(k, shape, scale):
        return jax.random.normal(k, shape, f32) * scale

    def gain(k, shape):
        return 1.0 + 0.02 * jax.random.normal(k, shape, f32)

    L = DEPTH
    n_idx = jnp.arange(S5_N, dtype=f32)
    inp = {}
    inp["x"] = jax.random.normal(ks[0], (BATCH, SEQ, D_MODEL), f32)
    inp["ffn1_gate"] = nrm(ks[1], (L, D_MODEL, D_FF), D_MODEL ** -0.5)
    inp["ffn1_up"] = nrm(ks[2], (L, D_MODEL, D_FF), D_MODEL ** -0.5)
    inp["ffn1_down"] = nrm(ks[3], (L, D_FF, D_MODEL), D_FF ** -0.5 * DEEPNORM_BETA)
    inp["ln1_g"] = gain(ks[4], (L, D_MODEL))
    inp["ln1_b"] = nrm(ks[5], (L, D_MODEL), 0.02)
    inp["w_in"] = nrm(ks[6], (L, D_MODEL, D_IN), D_MODEL ** -0.5)
    inp["s5_lam_re"] = -0.5 * jnp.exp(0.05 * jax.random.normal(ks[7], (L, S5_G, S5_N), f32))
    inp["s5_lam_im"] = math.pi * n_idx + 0.01 * jax.random.normal(ks[8], (L, S5_G, S5_N), f32)
    inp["s5_log_dt"] = jax.random.uniform(ks[9], (L, S5_G), f32, math.log(1e-3), math.log(1e-1))
    inp["s5_b_re"] = nrm(ks[10], (L, S5_G, S5_N, S5_P), (2.0 * S5_P) ** -0.5)
    inp["s5_b_im"] = nrm(ks[11], (L, S5_G, S5_N, S5_P), (2.0 * S5_P) ** -0.5)
    inp["s5_c_re"] = nrm(ks[12], (L, S5_G, S5_P, S5_N), (2.0 * S5_N) ** -0.5)
    inp["s5_c_im"] = nrm(ks[13], (L, S5_G, S5_P, S5_N), (2.0 * S5_N) ** -0.5)
    inp["s5_d"] = nrm(ks[14], (L, S5_G, S5_P), 1.0)
    inp["s5_w_glu"] = nrm(ks[15], (L, D_S5, D_S5), D_S5 ** -0.5)
    inp["conv_w"] = nrm(ks[16], (L, CONV_W, D_CONV), CONV_W ** -0.5)
    inp["conv_b"] = nrm(ks[17], (L, D_CONV), 0.02)
    inp["g_s5"] = gain(ks[18], (L, D_S5))
    inp["g_conv"] = gain(ks[19], (L, D_CONV))
    inp["w_out"] = nrm(ks[20], (L, D_MIX, D_MODEL), D_MIX ** -0.5 * DEEPNORM_BETA)
    inp["ln2_g"] = gain(ks[21], (L, D_MODEL))
    inp["ln2_b"] = nrm(ks[22], (L, D_MODEL), 0.02)
    inp["ffn2_gate"] = nrm(ks[23], (L, D_MODEL, D_FF), D_MODEL ** -0.5)
    inp["ffn2_up"] = nrm(ks[24], (L, D_MODEL, D_FF), D_MODEL ** -0.5)
    inp["ffn2_down"] = nrm(ks[25], (L, D_FF, D_MODEL), D_FF ** -0.5 * DEEPNORM_BETA)
    inp["ln3_g"] = gain(ks[26], (L, D_MODEL))
    inp["ln3_b"] = nrm(ks[27], (L, D_MODEL), 0.02)
    return inp


def _fwd_reference(x, ffn1_gate, ffn1_up, ffn1_down, ln1_g, ln1_b, w_in, s5_lam_re, s5_lam_im,
              s5_log_dt, s5_b_re, s5_b_im, s5_c_re, s5_c_im, s5_d, s5_w_glu, conv_w, conv_b,
              g_s5, g_conv, w_out, ln2_g, ln2_b, ffn2_gate, ffn2_up, ffn2_down, ln3_g, ln3_b):
    for l in range(DEPTH):
        x = layer_norm(DEEPNORM_ALPHA * x + 0.5 * swiglu(x, ffn1_gate[l], ffn1_up[l], ffn1_down[l]),
                       ln1_g[l], ln1_b[l])
        m = hybrid_mixer(x, w_in[l], s5_lam_re[l], s5_lam_im[l], s5_log_dt[l], s5_b_re[l],
                         s5_b_im[l], s5_c_re[l], s5_c_im[l], s5_d[l], s5_w_glu[l], conv_w[l],
                         conv_b[l], g_s5[l], g_conv[l], w_out[l])
        x = layer_norm(DEEPNORM_ALPHA * x + m, ln2_g[l], ln2_b[l])
        x = layer_norm(DEEPNORM_ALPHA * x + 0.5 * swiglu(x, ffn2_gate[l], ffn2_up[l], ffn2_down[l]),
                       ln3_g[l], ln3_b[l])
    return x


import jax as _jax
import jax.numpy as _jnp

TWIN_FORMAT = 'train_step'
FWD_PARAMS = ['x', 'ffn1_gate', 'ffn1_up', 'ffn1_down', 'ln1_g', 'ln1_b', 'w_in', 's5_lam_re', 's5_lam_im', 's5_log_dt', 's5_b_re', 's5_b_im', 's5_c_re', 's5_c_im', 's5_d', 's5_w_glu', 'conv_w', 'conv_b', 'g_s5', 'g_conv', 'w_out', 'ln2_g', 'ln2_b', 'ffn2_gate', 'ffn2_up', 'ffn2_down', 'ln3_g', 'ln3_b']
TWIN_WEIGHTS = ['ffn1_gate', 'ffn1_up', 'ffn1_down', 'ln1_g', 'ln1_b', 'w_in', 's5_lam_re', 's5_lam_im', 's5_log_dt', 's5_b_re', 's5_b_im', 's5_c_re', 's5_c_im', 's5_d', 's5_w_glu', 'conv_w', 'conv_b', 'g_s5', 'g_conv', 'w_out', 'ln2_g', 'ln2_b', 'ffn2_gate', 'ffn2_up', 'ffn2_down', 'ln3_g', 'ln3_b']
TWIN_DIFF_INPUT = 'x'
TWIN_INPUTS = ['x', 'ffn1_gate', 'ffn1_up', 'ffn1_down', 'ln1_g', 'ln1_b', 'w_in', 's5_lam_re', 's5_lam_im', 's5_log_dt', 's5_b_re', 's5_b_im', 's5_c_re', 's5_c_im', 's5_d', 's5_w_glu', 'conv_w', 'conv_b', 'g_s5', 'g_conv', 'w_out', 'ln2_g', 'ln2_b', 'ffn2_gate', 'ffn2_up', 'ffn2_down', 'ln3_g', 'ln3_b', 'loss_target', 'm_ffn1_gate', 'm_ffn1_up', 'm_ffn1_down', 'm_ln1_g', 'm_ln1_b', 'm_w_in', 'm_s5_lam_re', 'm_s5_lam_im', 'm_s5_log_dt', 'm_s5_b_re', 'm_s5_b_im', 'm_s5_c_re', 'm_s5_c_im', 'm_s5_d', 'm_s5_w_glu', 'm_conv_w', 'm_conv_b', 'm_g_s5', 'm_g_conv', 'm_w_out', 'm_ln2_g', 'm_ln2_b', 'm_ffn2_gate', 'm_ffn2_up', 'm_ffn2_down', 'm_ln3_g', 'm_ln3_b', 'v_ffn1_gate', 'v_ffn1_up', 'v_ffn1_down', 'v_ln1_g', 'v_ln1_b', 'v_w_in', 'v_s5_lam_re', 'v_s5_lam_im', 'v_s5_log_dt', 'v_s5_b_re', 'v_s5_b_im', 'v_s5_c_re', 'v_s5_c_im', 'v_s5_d', 'v_s5_w_glu', 'v_conv_w', 'v_conv_b', 'v_g_s5', 'v_g_conv', 'v_w_out', 'v_ln2_g', 'v_ln2_b', 'v_ffn2_gate', 'v_ffn2_up', 'v_ffn2_down', 'v_ln3_g', 'v_ln3_b']
TWIN_OUTPUTS = ['loss', 'grad_x', 'grad_ffn1_gate', 'grad_ffn1_up', 'grad_ffn1_down', 'grad_ln1_g', 'grad_ln1_b', 'grad_w_in', 'grad_s5_lam_re', 'grad_s5_lam_im', 'grad_s5_log_dt', 'grad_s5_b_re', 'grad_s5_b_im', 'grad_s5_c_re', 'grad_s5_c_im', 'grad_s5_d', 'grad_s5_w_glu', 'grad_conv_w', 'grad_conv_b', 'grad_g_s5', 'grad_g_conv', 'grad_w_out', 'grad_ln2_g', 'grad_ln2_b', 'grad_ffn2_gate', 'grad_ffn2_up', 'grad_ffn2_down', 'grad_ln3_g', 'grad_ln3_b', 'delta_ffn1_gate', 'delta_ffn1_up', 'delta_ffn1_down', 'delta_ln1_g', 'delta_ln1_b', 'delta_w_in', 'delta_s5_lam_re', 'delta_s5_lam_im', 'delta_s5_log_dt', 'delta_s5_b_re', 'delta_s5_b_im', 'delta_s5_c_re', 'delta_s5_c_im', 'delta_s5_d', 'delta_s5_w_glu', 'delta_conv_w', 'delta_conv_b', 'delta_g_s5', 'delta_g_conv', 'delta_w_out', 'delta_ln2_g', 'delta_ln2_b', 'delta_ffn2_gate', 'delta_ffn2_up', 'delta_ffn2_down', 'delta_ln3_g', 'delta_ln3_b', 'new_m_ffn1_gate', 'new_m_ffn1_up', 'new_m_ffn1_down', 'new_m_ln1_g', 'new_m_ln1_b', 'new_m_w_in', 'new_m_s5_lam_re', 'new_m_s5_lam_im', 'new_m_s5_log_dt', 'new_m_s5_b_re', 'new_m_s5_b_im', 'new_m_s5_c_re', 'new_m_s5_c_im', 'new_m_s5_d', 'new_m_s5_w_glu', 'new_m_conv_w', 'new_m_conv_b', 'new_m_g_s5', 'new_m_g_conv', 'new_m_w_out', 'new_m_ln2_g', 'new_m_ln2_b', 'new_m_ffn2_gate', 'new_m_ffn2_up', 'new_m_ffn2_down', 'new_m_ln3_g', 'new_m_ln3_b', 'new_v_ffn1_gate', 'new_v_ffn1_up', 'new_v_ffn1_down', 'new_v_ln1_g', 'new_v_ln1_b', 'new_v_w_in', 'new_v_s5_lam_re', 'new_v_s5_lam_im', 'new_v_s5_log_dt', 'new_v_s5_b_re', 'new_v_s5_b_im', 'new_v_s5_c_re', 'new_v_s5_c_im', 'new_v_s5_d', 'new_v_s5_w_glu', 'new_v_conv_w', 'new_v_conv_b', 'new_v_g_s5', 'new_v_g_conv', 'new_v_w_out', 'new_v_ln2_g', 'new_v_ln2_b', 'new_v_ffn2_gate', 'new_v_ffn2_up', 'new_v_ffn2_down', 'new_v_ln3_g', 'new_v_ln3_b']
TWIN_LEAF_KINDS = {'loss': 'loss', 'grad_x': 'grad_x', 'grad_ffn1_gate': 'grad_w', 'grad_ffn1_up': 'grad_w', 'grad_ffn1_down': 'grad_w', 'grad_ln1_g': 'grad_w', 'grad_ln1_b': 'grad_w', 'grad_w_in': 'grad_w', 'grad_s5_lam_re': 'grad_w', 'grad_s5_lam_im': 'grad_w', 'grad_s5_log_dt': 'grad_w', 'grad_s5_b_re': 'grad_w', 'grad_s5_b_im': 'grad_w', 'grad_s5_c_re': 'grad_w', 'grad_s5_c_im': 'grad_w', 'grad_s5_d': 'grad_w', 'grad_s5_w_glu': 'grad_w', 'grad_conv_w': 'grad_w', 'grad_conv_b': 'grad_w', 'grad_g_s5': 'grad_w', 'grad_g_conv': 'grad_w', 'grad_w_out': 'grad_w', 'grad_ln2_g': 'grad_w', 'grad_ln2_b': 'grad_w', 'grad_ffn2_gate': 'grad_w', 'grad_ffn2_up': 'grad_w', 'grad_ffn2_down': 'grad_w', 'grad_ln3_g': 'grad_w', 'grad_ln3_b': 'grad_w', 'delta_ffn1_gate': 'delta_w', 'delta_ffn1_up': 'delta_w', 'delta_ffn1_down': 'delta_w', 'delta_ln1_g': 'delta_w', 'delta_ln1_b': 'delta_w', 'delta_w_in': 'delta_w', 'delta_s5_lam_re': 'delta_w', 'delta_s5_lam_im': 'delta_w', 'delta_s5_log_dt': 'delta_w', 'delta_s5_b_re': 'delta_w', 'delta_s5_b_im': 'delta_w', 'delta_s5_c_re': 'delta_w', 'delta_s5_c_im': 'delta_w', 'delta_s5_d': 'delta_w', 'delta_s5_w_glu': 'delta_w', 'delta_conv_w': 'delta_w', 'delta_conv_b': 'delta_w', 'delta_g_s5': 'delta_w', 'delta_g_conv': 'delta_w', 'delta_w_out': 'delta_w', 'delta_ln2_g': 'delta_w', 'delta_ln2_b': 'delta_w', 'delta_ffn2_gate': 'delta_w', 'delta_ffn2_up': 'delta_w', 'delta_ffn2_down': 'delta_w', 'delta_ln3_g': 'delta_w', 'delta_ln3_b': 'delta_w', 'new_m_ffn1_gate': 'new_m', 'new_m_ffn1_up': 'new_m', 'new_m_ffn1_down': 'new_m', 'new_m_ln1_g': 'new_m', 'new_m_ln1_b': 'new_m', 'new_m_w_in': 'new_m', 'new_m_s5_lam_re': 'new_m', 'new_m_s5_lam_im': 'new_m', 'new_m_s5_log_dt': 'new_m', 'new_m_s5_b_re': 'new_m', 'new_m_s5_b_im': 'new_m', 'new_m_s5_c_re': 'new_m', 'new_m_s5_c_im': 'new_m', 'new_m_s5_d': 'new_m', 'new_m_s5_w_glu': 'new_m', 'new_m_conv_w': 'new_m', 'new_m_conv_b': 'new_m', 'new_m_g_s5': 'new_m', 'new_m_g_conv': 'new_m', 'new_m_w_out': 'new_m', 'new_m_ln2_g': 'new_m', 'new_m_ln2_b': 'new_m', 'new_m_ffn2_gate': 'new_m', 'new_m_ffn2_up': 'new_m', 'new_m_ffn2_down': 'new_m', 'new_m_ln3_g': 'new_m', 'new_m_ln3_b': 'new_m', 'new_v_ffn1_gate': 'new_v', 'new_v_ffn1_up': 'new_v', 'new_v_ffn1_down': 'new_v', 'new_v_ln1_g': 'new_v', 'new_v_ln1_b': 'new_v', 'new_v_w_in': 'new_v', 'new_v_s5_lam_re': 'new_v', 'new_v_s5_lam_im': 'new_v', 'new_v_s5_log_dt': 'new_v', 'new_v_s5_b_re': 'new_v', 'new_v_s5_b_im': 'new_v', 'new_v_s5_c_re': 'new_v', 'new_v_s5_c_im': 'new_v', 'new_v_s5_d': 'new_v', 'new_v_s5_w_glu': 'new_v', 'new_v_conv_w': 'new_v', 'new_v_conv_b': 'new_v', 'new_v_g_s5': 'new_v', 'new_v_g_conv': 'new_v', 'new_v_w_out': 'new_v', 'new_v_ln2_g': 'new_v', 'new_v_ln2_b': 'new_v', 'new_v_ffn2_gate': 'new_v', 'new_v_ffn2_up': 'new_v', 'new_v_ffn2_down': 'new_v', 'new_v_ln3_g': 'new_v', 'new_v_ln3_b': 'new_v'}


def _forward(args):
    return _fwd_reference(*[args[k] for k in FWD_PARAMS])


def _output_shape():
    out = _jax.eval_shape(lambda: _forward(_fwd_setup_inputs(0)))
    return out.shape, out.dtype

N_MICROBATCH = 1
ADAM_LR = 0.001
ADAM_B1 = 0.9
ADAM_B2 = 0.999
ADAM_EPS = 1e-08
ADAM_WD = 0.01
ADAM_STEP = 10
PER_EXAMPLE_BATCH_AXIS = {'x': 0, 'loss_target': 0}
SHARED_INPUTS = []
_WEIGHT_DTYPES = {'ffn1_gate': _jnp.float32, 'ffn1_up': _jnp.float32, 'ffn1_down': _jnp.float32, 'ln1_g': _jnp.float32, 'ln1_b': _jnp.float32, 'w_in': _jnp.float32, 's5_lam_re': _jnp.float32, 's5_lam_im': _jnp.float32, 's5_log_dt': _jnp.float32, 's5_b_re': _jnp.float32, 's5_b_im': _jnp.float32, 's5_c_re': _jnp.float32, 's5_c_im': _jnp.float32, 's5_d': _jnp.float32, 's5_w_glu': _jnp.float32, 'conv_w': _jnp.float32, 'conv_b': _jnp.float32, 'g_s5': _jnp.float32, 'g_conv': _jnp.float32, 'w_out': _jnp.float32, 'ln2_g': _jnp.float32, 'ln2_b': _jnp.float32, 'ffn2_gate': _jnp.float32, 'ffn2_up': _jnp.float32, 'ffn2_down': _jnp.float32, 'ln3_g': _jnp.float32, 'ln3_b': _jnp.float32}
MOMENT_SCALE = {'ffn1_gate': 3.171530e-03, 'ffn1_up': 3.076594e-03, 'ffn1_down': 1.212432e-02, 'ln1_g': 2.444859e-01, 'ln1_b': 1.393887e-01, 'w_in': 1.650860e-02, 's5_lam_re': 9.692830e-04, 's5_lam_im': 9.058576e-04, 's5_log_dt': 4.607758e-01, 's5_b_re': 5.865861e-04, 's5_b_im': 5.851448e-04, 's5_c_re': 1.173615e-03, 's5_c_im': 1.169883e-03, 's5_d': 2.197533e-02, 's5_w_glu': 5.025378e-03, 'conv_w': 1.639117e-02, 'conv_b': 1.643826e-02, 'g_s5': 2.032162e-02, 'g_conv': 1.619108e-02, 'w_out': 4.361368e-02, 'ln2_g': 2.620039e-01, 'ln2_b': 1.412773e-01, 'ffn2_gate': 3.066568e-03, 'ffn2_up': 2.980696e-03, 'ffn2_down': 1.175563e-02, 'ln3_g': 4.030561e+00, 'ln3_b': 6.655907e-01}


def _to_microbatches(a, axis):
    t = _jnp.moveaxis(a, axis, 0)
    t = t.reshape((N_MICROBATCH, t.shape[0] // N_MICROBATCH) + t.shape[1:])
    return _jnp.moveaxis(t, 1, axis + 1)


def setup_inputs(seed: int = 0) -> dict:
    inp = _fwd_setup_inputs(seed)
    key = _jax.random.fold_in(_jax.random.key(seed), 7919)
    shape, _ = _output_shape()
    out = dict(inp)
    out["loss_target"] = _jax.random.normal(_jax.random.fold_in(key, 0), shape, _jnp.float32)
    for i, name in enumerate(TWIN_WEIGHTS):
        w = inp[name].astype(_jnp.float32)
        if MOMENT_SCALE is None:
            s = _jnp.sqrt(_jnp.mean(_jnp.square(w)) + 1e-30)
        else:
            s = MOMENT_SCALE[name]
        km, kv = _jax.random.split(_jax.random.fold_in(key, i + 1))
        out[name] = w
        out["m_" + name] = s * _jax.random.normal(km, w.shape, _jnp.float32)
        out["v_" + name] = (s * s) * _jax.random.uniform(kv, w.shape, _jnp.float32, 0.5, 1.5)
    if N_MICROBATCH > 1:
        for name, axis in PER_EXAMPLE_BATCH_AXIS.items():
            out[name] = _to_microbatches(out[name], axis)
    return {'x': out['x'], 'ffn1_gate': out['ffn1_gate'], 'ffn1_up': out['ffn1_up'], 'ffn1_down': out['ffn1_down'], 'ln1_g': out['ln1_g'], 'ln1_b': out['ln1_b'], 'w_in': out['w_in'], 's5_lam_re': out['s5_lam_re'], 's5_lam_im': out['s5_lam_im'], 's5_log_dt': out['s5_log_dt'], 's5_b_re': out['s5_b_re'], 's5_b_im': out['s5_b_im'], 's5_c_re': out['s5_c_re'], 's5_c_im': out['s5_c_im'], 's5_d': out['s5_d'], 's5_w_glu': out['s5_w_glu'], 'conv_w': out['conv_w'], 'conv_b': out['conv_b'], 'g_s5': out['g_s5'], 'g_conv': out['g_conv'], 'w_out': out['w_out'], 'ln2_g': out['ln2_g'], 'ln2_b': out['ln2_b'], 'ffn2_gate': out['ffn2_gate'], 'ffn2_up': out['ffn2_up'], 'ffn2_down': out['ffn2_down'], 'ln3_g': out['ln3_g'], 'ln3_b': out['ln3_b'], 'loss_target': out['loss_target'], 'm_ffn1_gate': out['m_ffn1_gate'], 'm_ffn1_up': out['m_ffn1_up'], 'm_ffn1_down': out['m_ffn1_down'], 'm_ln1_g': out['m_ln1_g'], 'm_ln1_b': out['m_ln1_b'], 'm_w_in': out['m_w_in'], 'm_s5_lam_re': out['m_s5_lam_re'], 'm_s5_lam_im': out['m_s5_lam_im'], 'm_s5_log_dt': out['m_s5_log_dt'], 'm_s5_b_re': out['m_s5_b_re'], 'm_s5_b_im': out['m_s5_b_im'], 'm_s5_c_re': out['m_s5_c_re'], 'm_s5_c_im': out['m_s5_c_im'], 'm_s5_d': out['m_s5_d'], 'm_s5_w_glu': out['m_s5_w_glu'], 'm_conv_w': out['m_conv_w'], 'm_conv_b': out['m_conv_b'], 'm_g_s5': out['m_g_s5'], 'm_g_conv': out['m_g_conv'], 'm_w_out': out['m_w_out'], 'm_ln2_g': out['m_ln2_g'], 'm_ln2_b': out['m_ln2_b'], 'm_ffn2_gate': out['m_ffn2_gate'], 'm_ffn2_up': out['m_ffn2_up'], 'm_ffn2_down': out['m_ffn2_down'], 'm_ln3_g': out['m_ln3_g'], 'm_ln3_b': out['m_ln3_b'], 'v_ffn1_gate': out['v_ffn1_gate'], 'v_ffn1_up': out['v_ffn1_up'], 'v_ffn1_down': out['v_ffn1_down'], 'v_ln1_g': out['v_ln1_g'], 'v_ln1_b': out['v_ln1_b'], 'v_w_in': out['v_w_in'], 'v_s5_lam_re': out['v_s5_lam_re'], 'v_s5_lam_im': out['v_s5_lam_im'], 'v_s5_log_dt': out['v_s5_log_dt'], 'v_s5_b_re': out['v_s5_b_re'], 'v_s5_b_im': out['v_s5_b_im'], 'v_s5_c_re': out['v_s5_c_re'], 'v_s5_c_im': out['v_s5_c_im'], 'v_s5_d': out['v_s5_d'], 'v_s5_w_glu': out['v_s5_w_glu'], 'v_conv_w': out['v_conv_w'], 'v_conv_b': out['v_conv_b'], 'v_g_s5': out['v_g_s5'], 'v_g_conv': out['v_g_conv'], 'v_w_out': out['v_w_out'], 'v_ln2_g': out['v_ln2_g'], 'v_ln2_b': out['v_ln2_b'], 'v_ffn2_gate': out['v_ffn2_gate'], 'v_ffn2_up': out['v_ffn2_up'], 'v_ffn2_down': out['v_ffn2_down'], 'v_ln3_g': out['v_ln3_g'], 'v_ln3_b': out['v_ln3_b']}


def _loss(weights, diff, rest, loss_target):
    with _jax.named_scope("forward"):
        args = {**rest, TWIN_DIFF_INPUT: diff, **{k: w.astype(_WEIGHT_DTYPES[k]) for k, w in weights.items()}}
        y = _forward(args)
    with _jax.named_scope("loss_head"):
        err = _jnp.square(y.astype(_jnp.float32) - loss_target)
        return 0.5 * _jnp.sum(_jnp.mean(err, axis=-1)) if err.ndim else 0.5 * err


def _adamw(w, g, m, v):
    m = ADAM_B1 * m + (1.0 - ADAM_B1) * g
    v = ADAM_B2 * v + (1.0 - ADAM_B2) * _jnp.square(g)
    m_hat = m / (1.0 - ADAM_B1 ** ADAM_STEP)
    v_hat = v / (1.0 - ADAM_B2 ** ADAM_STEP)
    delta = -ADAM_LR * (m_hat / (_jnp.sqrt(v_hat) + ADAM_EPS) + ADAM_WD * w)
    return delta, m, v


def reference(x, ffn1_gate, ffn1_up, ffn1_down, ln1_g, ln1_b, w_in, s5_lam_re, s5_lam_im, s5_log_dt, s5_b_re, s5_b_im, s5_c_re, s5_c_im, s5_d, s5_w_glu, conv_w, conv_b, g_s5, g_conv, w_out, ln2_g, ln2_b, ffn2_gate, ffn2_up, ffn2_down, ln3_g, ln3_b, loss_target, m_ffn1_gate, m_ffn1_up, m_ffn1_down, m_ln1_g, m_ln1_b, m_w_in, m_s5_lam_re, m_s5_lam_im, m_s5_log_dt, m_s5_b_re, m_s5_b_im, m_s5_c_re, m_s5_c_im, m_s5_d, m_s5_w_glu, m_conv_w, m_conv_b, m_g_s5, m_g_conv, m_w_out, m_ln2_g, m_ln2_b, m_ffn2_gate, m_ffn2_up, m_ffn2_down, m_ln3_g, m_ln3_b, v_ffn1_gate, v_ffn1_up, v_ffn1_down, v_ln1_g, v_ln1_b, v_w_in, v_s5_lam_re, v_s5_lam_im, v_s5_log_dt, v_s5_b_re, v_s5_b_im, v_s5_c_re, v_s5_c_im, v_s5_d, v_s5_w_glu, v_conv_w, v_conv_b, v_g_s5, v_g_conv, v_w_out, v_ln2_g, v_ln2_b, v_ffn2_gate, v_ffn2_up, v_ffn2_down, v_ln3_g, v_ln3_b):
    given = dict(x=x, ffn1_gate=ffn1_gate, ffn1_up=ffn1_up, ffn1_down=ffn1_down, ln1_g=ln1_g, ln1_b=ln1_b, w_in=w_in, s5_lam_re=s5_lam_re, s5_lam_im=s5_lam_im, s5_log_dt=s5_log_dt, s5_b_re=s5_b_re, s5_b_im=s5_b_im, s5_c_re=s5_c_re, s5_c_im=s5_c_im, s5_d=s5_d, s5_w_glu=s5_w_glu, conv_w=conv_w, conv_b=conv_b, g_s5=g_s5, g_conv=g_conv, w_out=w_out, ln2_g=ln2_g, ln2_b=ln2_b, ffn2_gate=ffn2_gate, ffn2_up=ffn2_up, ffn2_down=ffn2_down, ln3_g=ln3_g, ln3_b=ln3_b, loss_target=loss_target, m_ffn1_gate=m_ffn1_gate, m_ffn1_up=m_ffn1_up, m_ffn1_down=m_ffn1_down, m_ln1_g=m_ln1_g, m_ln1_b=m_ln1_b, m_w_in=m_w_in, m_s5_lam_re=m_s5_lam_re, m_s5_lam_im=m_s5_lam_im, m_s5_log_dt=m_s5_log_dt, m_s5_b_re=m_s5_b_re, m_s5_b_im=m_s5_b_im, m_s5_c_re=m_s5_c_re, m_s5_c_im=m_s5_c_im, m_s5_d=m_s5_d, m_s5_w_glu=m_s5_w_glu, m_conv_w=m_conv_w, m_conv_b=m_conv_b, m_g_s5=m_g_s5, m_g_conv=m_g_conv, m_w_out=m_w_out, m_ln2_g=m_ln2_g, m_ln2_b=m_ln2_b, m_ffn2_gate=m_ffn2_gate, m_ffn2_up=m_ffn2_up, m_ffn2_down=m_ffn2_down, m_ln3_g=m_ln3_g, m_ln3_b=m_ln3_b, v_ffn1_gate=v_ffn1_gate, v_ffn1_up=v_ffn1_up, v_ffn1_down=v_ffn1_down, v_ln1_g=v_ln1_g, v_ln1_b=v_ln1_b, v_w_in=v_w_in, v_s5_lam_re=v_s5_lam_re, v_s5_lam_im=v_s5_lam_im, v_s5_log_dt=v_s5_log_dt, v_s5_b_re=v_s5_b_re, v_s5_b_im=v_s5_b_im, v_s5_c_re=v_s5_c_re, v_s5_c_im=v_s5_c_im, v_s5_d=v_s5_d, v_s5_w_glu=v_s5_w_glu, v_conv_w=v_conv_w, v_conv_b=v_conv_b, v_g_s5=v_g_s5, v_g_conv=v_g_conv, v_w_out=v_w_out, v_ln2_g=v_ln2_g, v_ln2_b=v_ln2_b, v_ffn2_gate=v_ffn2_gate, v_ffn2_up=v_ffn2_up, v_ffn2_down=v_ffn2_down, v_ln3_g=v_ln3_g, v_ln3_b=v_ln3_b)
    weights = {n: given[n] for n in TWIN_WEIGHTS}
    shared = {n: given[n] for n in SHARED_INPUTS}
    per_example = {n: given[n] for n in ['x']}
    grad_fn = _jax.value_and_grad(_loss, argnums=(0, 1))

    def one_microbatch(ex, loss_target):
        ex = dict(ex)
        diff = ex.pop(TWIN_DIFF_INPUT)
        return grad_fn(weights, diff, {**shared, **ex}, loss_target)

    if N_MICROBATCH == 1:
        loss, (grad_w, grad_x) = one_microbatch(per_example, given["loss_target"])
    else:
        def body(carry, xs):
            loss_sum, grad_sum = carry
            l_k, (gw_k, gx_k) = one_microbatch(xs[0], xs[1])
            with _jax.named_scope("update"):
                return (loss_sum + l_k, _jax.tree.map(_jnp.add, grad_sum, gw_k)), gx_k

        init = (_jnp.zeros((), _jnp.float32), _jax.tree.map(_jnp.zeros_like, weights))
        (loss, grad_w), grad_x = _jax.lax.scan(body, init, (per_example, given["loss_target"]))
    with _jax.named_scope("update"):
        delta_w, new_m, new_v = {}, {}, {}
        for n in TWIN_WEIGHTS:
            delta_w[n], new_m[n], new_v[n] = _adamw(weights[n], grad_w[n], given["m_" + n], given["v_" + n])
    return (loss, grad_x, *[grad_w[n] for n in TWIN_WEIGHTS], *[delta_w[n] for n in TWIN_WEIGHTS],
            *[new_m[n] for n in TWIN_WEIGHTS], *[new_v[n] for n in TWIN_WEIGHTS])
```

```python
import functools
import math

import jax
import jax.numpy as jnp
from jax import lax
from jax.experimental import pallas as pl
from jax.experimental.pallas import tpu as pltpu

F32 = jnp.float32
BF16 = jnp.bfloat16
MESH = pl.DeviceIdType.MESH
AXES = ("x", "y", "c")
N_DEV = 8

S5_P = 16
S5_N = 64
CONV_W = 3
LN_EPS = 1e-5
RMS_EPS = 1e-6
ADAM_LR = 0.001
ADAM_B1 = 0.9
ADAM_B2 = 0.999
ADAM_EPS = 1e-08
ADAM_WD = 0.01
ADAM_STEP = 10

V7X_VMEM_BYTES = 64 * 1024 * 1024
VMEM_LIMIT = V7X_VMEM_BYTES * 7 // 8
LANES = 128
SUBLANES = 8
BF16_ROWS = 16
PACK_ROWS = 512
S5_GROUPS_PER_BLOCK = LANES // S5_P
S5_STATE_BLOCK = S5_GROUPS_PER_BLOCK * S5_N


def _tile(n, pref, align):
    best = None
    d = align
    while d <= min(n, pref):
        if n % d == 0:
            best = d
        d += align
    return best if best is not None else n


def _params(*sem):
    return pltpu.CompilerParams(dimension_semantics=sem, vmem_limit_bytes=VMEM_LIMIT)


def _dot_nn(a, b):
    return lax.dot_general(a, b, (((1,), (0,)), ((), ())), preferred_element_type=F32)


def _dot_nt(a, b):
    return lax.dot_general(a, b, (((1,), (1,)), ((), ())), preferred_element_type=F32)


def _dot_tn(a, b):
    return lax.dot_general(a, b, (((0,), (0,)), ((), ())), preferred_element_type=F32)


def _colsum(v):
    return jnp.sum(v, axis=0, keepdims=True)


def _rowmean(v):
    return jnp.mean(v, axis=-1, keepdims=True)


def _ffn_up(xb, wg, wu):
    T, D = xb.shape
    F = wg.shape[0]
    tm = _tile(T, 512, 16)
    tn = _tile(F, 512, LANES)

    def body(x_ref, wg_ref, wu_ref, g_ref, u_ref, h_ref):
        x = x_ref[...]
        g = _dot_nt(x, wg_ref[...])
        u = _dot_nt(x, wu_ref[...])
        g_ref[...] = g
        u_ref[...] = u
        h_ref[...] = (g * jax.nn.sigmoid(g) * u).astype(BF16)

    w_spec = pl.BlockSpec((tn, D), lambda j, i: (j, 0))
    o_spec = pl.BlockSpec((tm, tn), lambda j, i: (i, j))
    return pl.pallas_call(
        body, name="ffn_up", grid=(F // tn, T // tm),
        in_specs=[pl.BlockSpec((tm, D), lambda j, i: (i, 0)), w_spec, w_spec],
        out_specs=[o_spec, o_spec, o_spec],
        out_shape=[jax.ShapeDtypeStruct((T, F), F32), jax.ShapeDtypeStruct((T, F), F32),
                   jax.ShapeDtypeStruct((T, F), BF16)],
        compiler_params=_params("arbitrary", "arbitrary"),
    )(xb, wg, wu)


def _mm_res_ln(a, w, res, g, b, scale, alpha):
    T, K = a.shape
    D = w.shape[1]
    tm = _tile(T, 256, 16)
    tk = _tile(K, 512, LANES)
    nk = K // tk

    def body(a_ref, w_ref, res_ref, g_ref, b_ref, xo_ref, xb_ref, xh_ref, rstd_ref, acc_ref):
        k = pl.program_id(1)

        @pl.when(k == 0)
        def _():
            acc_ref[...] = jnp.zeros_like(acc_ref)

        acc_ref[...] += _dot_nn(a_ref[...], w_ref[...])

        @pl.when(k == nk - 1)
        def _():
            r = alpha * res_ref[...] + scale * acc_ref[...]
            xc = r - _rowmean(r)
            rstd = lax.rsqrt(_rowmean(xc * xc) + LN_EPS)
            xh = xc * rstd
            xo = xh * g_ref[...] + b_ref[...]
            xo_ref[...] = xo
            xb_ref[...] = xo.astype(BF16)
            xh_ref[...] = xh
            rstd_ref[...] = rstd

    row = pl.BlockSpec((tm, D), lambda i, k: (i, 0))
    vec = pl.BlockSpec((1, D), lambda i, k: (0, 0))
    return pl.pallas_call(
        body, name="mm_res_ln", grid=(T // tm, nk),
        in_specs=[pl.BlockSpec((tm, tk), lambda i, k: (i, k)), pl.BlockSpec((tk, D), lambda i, k: (k, 0)),
                  row, vec, vec],
        out_specs=[row, row, row, pl.BlockSpec((tm, 1), lambda i, k: (i, 0))],
        out_shape=[jax.ShapeDtypeStruct((T, D), F32), jax.ShapeDtypeStruct((T, D), BF16),
                   jax.ShapeDtypeStruct((T, D), F32), jax.ShapeDtypeStruct((T, 1), F32)],
        scratch_shapes=[pltpu.VMEM((tm, D), F32)],
        compiler_params=_params("arbitrary", "arbitrary"),
    )(a, w, res, g, b)


def _mm_nt(a, w):
    M, K = a.shape
    N = w.shape[0]
    tm = _tile(M, 512, 16)
    tn = _tile(N, 512, LANES)

    def body(a_ref, w_ref, o_ref):
        o_ref[...] = _dot_nt(a_ref[...], w_ref[...])

    return pl.pallas_call(
        body, name="mm_nt", grid=(N // tn, M // tm),
        in_specs=[pl.BlockSpec((tm, K), lambda j, i: (i, 0)), pl.BlockSpec((tn, K), lambda j, i: (j, 0))],
        out_specs=pl.BlockSpec((tm, tn), lambda j, i: (i, j)),
        out_shape=jax.ShapeDtypeStruct((M, N), F32),
        compiler_params=_params("arbitrary", "arbitrary"),
    )(a, w)


def _mm_tn(a, b, scale, out_dtype):
    T, M = a.shape
    N = b.shape[1]
    tm = _tile(M, 512, LANES)
    tn = _tile(N, 1024, LANES)
    tk = _tile(T, 512, 16)
    nk = T // tk

    def body(a_ref, b_ref, o_ref, acc_ref):
        k = pl.program_id(2)

        @pl.when(k == 0)
        def _():
            acc_ref[...] = jnp.zeros_like(acc_ref)

        acc_ref[...] += _dot_tn(a_ref[...], b_ref[...])

        @pl.when(k == nk - 1)
        def _():
            o_ref[...] = (scale * acc_ref[...]).astype(out_dtype)

    return pl.pallas_call(
        body, name="mm_tn", grid=(M // tm, N // tn, nk),
        in_specs=[pl.BlockSpec((tk, tm), lambda i, j, k: (k, i)), pl.BlockSpec((tk, tn), lambda i, j, k: (k, j))],
        out_specs=pl.BlockSpec((tm, tn), lambda i, j, k: (i, j)),
        out_shape=jax.ShapeDtypeStruct((M, N), out_dtype),
        scratch_shapes=[pltpu.VMEM((tm, tn), F32)],
        compiler_params=_params("arbitrary", "arbitrary", "arbitrary"),
    )(a, b)


def _ln_bwd(dy, xh, rstd, g):
    dxh = dy * g
    dr = rstd * (dxh - _rowmean(dxh) - xh * _rowmean(dxh * xh))
    return dr, _colsum(dy * xh), _colsum(dy)


def _loss_ln_bwd(y, target, xh, rstd, g):
    T, D = y.shape
    tm = _tile(T, 256, 16)

    def body(y_ref, t_ref, xh_ref, rstd_ref, g_ref, dr_ref, drb_ref, dg_ref, db_ref, loss_ref):
        i = pl.program_id(0)

        @pl.when(i == 0)
        def _():
            dg_ref[...] = jnp.zeros_like(dg_ref)
            db_ref[...] = jnp.zeros_like(db_ref)
            loss_ref[...] = jnp.zeros_like(loss_ref)

        err = y_ref[...] - t_ref[...]
        loss_ref[...] += (0.5 / D) * _colsum(jnp.sum(err * err, axis=1, keepdims=True))
        dr, dg, db = _ln_bwd(err * (1.0 / D), xh_ref[...], rstd_ref[...], g_ref[...])
        dr_ref[...] = dr
        drb_ref[...] = dr.astype(BF16)
        dg_ref[...] += dg
        db_ref[...] += db

    row = pl.BlockSpec((tm, D), lambda i: (i, 0))
    vec = pl.BlockSpec((1, D), lambda i: (0, 0))
    return pl.pallas_call(
        body, name="loss_ln_bwd", grid=(T // tm,),
        in_specs=[row, row, row, pl.BlockSpec((tm, 1), lambda i: (i, 0)), vec],
        out_specs=[row, row, vec, vec, pl.BlockSpec((1, 1), lambda i: (0, 0))],
        out_shape=[jax.ShapeDtypeStruct((T, D), F32), jax.ShapeDtypeStruct((T, D), BF16),
                   jax.ShapeDtypeStruct((1, D), F32), jax.ShapeDtypeStruct((1, D), F32),
                   jax.ShapeDtypeStruct((1, 1), F32)],
        compiler_params=_params("arbitrary"),
    )(y, target, xh, rstd, g)


def _ffn_down_bwd(drb, wd, gpre, upre):
    T, D = drb.shape
    F = wd.shape[0]
    tm = _tile(T, 512, 16)
    tn = _tile(F, 512, LANES)

    def body(dr_ref, wd_ref, g_ref, u_ref, dg_ref, du_ref):
        dh = 0.5 * _dot_nt(dr_ref[...], wd_ref[...])
        g = g_ref[...]
        u = u_ref[...]
        sg = jax.nn.sigmoid(g)
        du_ref[...] = (dh * (g * sg)).astype(BF16)
        dg_ref[...] = (dh * u * (sg * (1.0 + g * (1.0 - sg)))).astype(BF16)

    t_spec = pl.BlockSpec((tm, tn), lambda j, i: (i, j))
    return pl.pallas_call(
        body, name="ffn_down_bwd", grid=(F // tn, T // tm),
        in_specs=[pl.BlockSpec((tm, D), lambda j, i: (i, 0)), pl.BlockSpec((tn, D), lambda j, i: (j, 0)),
                  t_spec, t_spec],
        out_specs=[t_spec, t_spec],
        out_shape=[jax.ShapeDtypeStruct((T, F), BF16), jax.ShapeDtypeStruct((T, F), BF16)],
        compiler_params=_params("arbitrary", "arbitrary"),
    )(drb, wd, gpre, upre)


def _mm_dx(pairs, res, alpha, ln=None):
    T, K = pairs[0][0].shape
    D = res.shape[1]
    n = len(pairs)
    tm = _tile(T, 256, 16)
    tk = _tile(K, 512, LANES)
    nk = K // tk
    with_ln = ln is not None

    def body(*refs):
        a_refs = refs[0:2 * n:2]
        w_refs = refs[1:2 * n:2]
        refs = refs[2 * n:]
        res_ref = refs[0]
        acc_ref = refs[-1]
        i = pl.program_id(0)
        k = pl.program_id(1)

        @pl.when(k == 0)
        def _():
            acc_ref[...] = jnp.zeros_like(acc_ref)

        for a_ref, w_ref in zip(a_refs, w_refs):
            acc_ref[...] += _dot_nn(a_ref[...], w_ref[...])

        if with_ln:
            xh_ref, rstd_ref, g_ref, dr_ref, drb_ref, dg_ref, db_ref = refs[1:8]

            @pl.when((i == 0) & (k == 0))
            def _():
                dg_ref[...] = jnp.zeros_like(dg_ref)
                db_ref[...] = jnp.zeros_like(db_ref)

            @pl.when(k == nk - 1)
            def _():
                dx = alpha * res_ref[...] + acc_ref[...]
                dr, dg, db = _ln_bwd(dx, xh_ref[...], rstd_ref[...], g_ref[...])
                dr_ref[...] = dr
                drb_ref[...] = dr.astype(BF16)
                dg_ref[...] += dg
                db_ref[...] += db
        else:
            dx_ref = refs[1]

            @pl.when(k == nk - 1)
            def _():
                dx_ref[...] = alpha * res_ref[...] + acc_ref[...]

    row = pl.BlockSpec((tm, D), lambda i, k: (i, 0))
    vec = pl.BlockSpec((1, D), lambda i, k: (0, 0))
    in_specs, operands = [], []
    for a, w in pairs:
        in_specs += [pl.BlockSpec((tm, tk), lambda i, k: (i, k)), pl.BlockSpec((tk, D), lambda i, k: (k, 0))]
        operands += [a, w]
    in_specs.append(row)
    operands.append(res)
    if with_ln:
        in_specs += [row, pl.BlockSpec((tm, 1), lambda i, k: (i, 0)), vec]
        operands += list(ln)
        out_specs = [row, row, vec, vec]
        out_shape = [jax.ShapeDtypeStruct((T, D), F32), jax.ShapeDtypeStruct((T, D), BF16),
                     jax.ShapeDtypeStruct((1, D), F32), jax.ShapeDtypeStruct((1, D), F32)]
    else:
        out_specs = [row]
        out_shape = [jax.ShapeDtypeStruct((T, D), F32)]
    return pl.pallas_call(
        body, name="mm_dx_ln_bwd" if with_ln else "mm_dx", grid=(T // tm, nk),
        in_specs=in_specs, out_specs=out_specs, out_shape=out_shape,
        scratch_shapes=[pltpu.VMEM((tm, D), F32)],
        compiler_params=_params("arbitrary", "arbitrary"),
    )(*operands)


def _s5_discretize(lre, lim, ldt, br, bi):
    dt = jnp.exp(ldt)
    mag = jnp.exp(lre * dt)
    ang = lim * dt
    ar = mag * jnp.cos(ang)
    ai = mag * jnp.sin(ang)
    den = lre * lre + lim * lim
    nr = ar - 1.0
    qr = (nr * lre + ai * lim) / den
    qi = (ai * lre - nr * lim) / den
    bbr = qr[None] * br - qi[None] * bi
    bbi = qr[None] * bi + qi[None] * br
    return ar, ai, bbr, bbi


def _s5_params_fwd(lre, lim, ldt, br, bi):
    def body(lre_ref, lim_ref, ldt_ref, br_ref, bi_ref, ar_ref, ai_ref, bbr_ref, bbi_ref):
        ar, ai, bbr, bbi = _s5_discretize(lre_ref[...], lim_ref[...], ldt_ref[...], br_ref[...], bi_ref[...])
        ar_ref[...] = ar
        ai_ref[...] = ai
        bbr_ref[...] = bbr
        bbi_ref[...] = bbi

    sds = jax.ShapeDtypeStruct
    return pl.pallas_call(
        body, name="s5_params_fwd",
        out_shape=[sds(lre.shape, F32), sds(lre.shape, F32), sds(br.shape, F32), sds(br.shape, F32)],
        compiler_params=pltpu.CompilerParams(vmem_limit_bytes=VMEM_LIMIT),
    )(lre, lim, ldt, br, bi)


def _s5_params_bwd(lre, lim, ldt, br, bi, dar, dai, dbbr, dbbi):
    def body(lre_ref, lim_ref, ldt_ref, br_ref, bi_ref, dar_ref, dai_ref, dbbr_ref, dbbi_ref,
             o_lre, o_lim, o_ldt, o_br, o_bi):
        _, vjp = jax.vjp(_s5_discretize, lre_ref[...], lim_ref[...], ldt_ref[...], br_ref[...], bi_ref[...])
        g = vjp((dar_ref[...], dai_ref[...], dbbr_ref[...], dbbi_ref[...]))
        o_lre[...] = g[0]
        o_lim[...] = g[1]
        o_ldt[...] = g[2]
        o_br[...] = g[3]
        o_bi[...] = g[4]

    sds = jax.ShapeDtypeStruct
    return pl.pallas_call(
        body, name="s5_params_bwd",
        out_shape=[sds(lre.shape, F32), sds(lre.shape, F32), sds(ldt.shape, F32), sds(br.shape, F32),
                   sds(br.shape, F32)],
        compiler_params=pltpu.CompilerParams(vmem_limit_bytes=VMEM_LIMIT),
    )(lre, lim, ldt, br, bi, dar, dai, dbbr, dbbi)


def _s5_fwd(proj, bdr, bdi, cdr, cdi, ar, ai, dvec):
    T = proj.shape[0]
    GB, UB, SB = bdr.shape
    tc = _tile(T, 256, SUBLANES)

    def body(u_ref, bdr_ref, bdi_ref, cdr_ref, cdi_ref, ar_ref, ai_ref, d_ref, y_ref, sr_ref, si_ref,
             cr_ref, ci_ref):
        @pl.when(pl.program_id(1) == 0)
        def _():
            cr_ref[...] = jnp.zeros_like(cr_ref)
            ci_ref[...] = jnp.zeros_like(ci_ref)

        u = u_ref[...]
        ub = u.astype(BF16)
        sr_ref[...] = _dot_nn(ub, bdr_ref[...])
        si_ref[...] = _dot_nn(ub, bdi_ref[...])
        a_re = ar_ref[...]
        a_im = ai_ref[...]

        def step(t, carry):
            p_re, p_im = carry
            row = pl.ds(t, 1)
            n_re = a_re * p_re - a_im * p_im + sr_ref[row, :]
            n_im = a_re * p_im + a_im * p_re + si_ref[row, :]
            sr_ref[row, :] = n_re
            si_ref[row, :] = n_im
            return n_re, n_im

        p_re, p_im = lax.fori_loop(0, tc, step, (cr_ref[...], ci_ref[...]), unroll=8)
        cr_ref[...] = p_re
        ci_ref[...] = p_im
        y_ref[...] = (_dot_nn(sr_ref[...].astype(BF16), cdr_ref[...])
                      - _dot_nn(si_ref[...].astype(BF16), cdi_ref[...]) + d_ref[...] * u)

    return pl.pallas_call(
        body, name="s5_fwd", grid=(GB, T // tc),
        in_specs=[pl.BlockSpec((tc, UB), lambda j, t: (t, j)),
                  pl.BlockSpec((None, UB, SB), lambda j, t: (j, 0, 0)),
                  pl.BlockSpec((None, UB, SB), lambda j, t: (j, 0, 0)),
                  pl.BlockSpec((None, SB, UB), lambda j, t: (j, 0, 0)),
                  pl.BlockSpec((None, SB, UB), lambda j, t: (j, 0, 0)),
                  pl.BlockSpec((1, SB), lambda j, t: (0, j)),
                  pl.BlockSpec((1, SB), lambda j, t: (0, j)),
                  pl.BlockSpec((1, UB), lambda j, t: (0, j))],
        out_specs=[pl.BlockSpec((tc, UB), lambda j, t: (t, j)),
                   pl.BlockSpec((tc, SB), lambda j, t: (t, j)),
                   pl.BlockSpec((tc, SB), lambda j, t: (t, j))],
        out_shape=[jax.ShapeDtypeStruct((T, GB * UB), F32), jax.ShapeDtypeStruct((T, GB * SB), F32),
                   jax.ShapeDtypeStruct((T, GB * SB), F32)],
        scratch_shapes=[pltpu.VMEM((1, SB), F32), pltpu.VMEM((1, SB), F32)],
        compiler_params=_params("arbitrary", "arbitrary"),
    )(proj, bdr, bdi, cdr, cdi, ar, ai, dvec)


def _s5_bwd(dy, proj, sr, si, bdr, bdi, cdr, cdi, ar, ai, dvec):
    T = dy.shape[0]
    GB, UB, SB = bdr.shape
    tc = _tile(T, 256, SUBLANES)
    nt = T // tc
    halo_blocks = tc // SUBLANES

    def body(dy_ref, u_ref, sr_ref, si_ref, hr_ref, hi_ref, bdr_ref, bdi_ref, cdr_ref, cdi_ref, ar_ref, ai_ref,
             d_ref, du_ref, dbdr_ref, dbdi_ref, dcdr_ref, dcdi_ref, dar_ref, dai_ref, dd_ref,
             gr_ref, gi_ref, pr_ref, pi_ref, cr_ref, ci_ref):
        step_no = pl.program_id(1)
        first_chunk = step_no == nt - 1

        @pl.when(step_no == 0)
        def _():
            for ref in (cr_ref, ci_ref, dbdr_ref, dbdi_ref, dcdr_ref, dcdi_ref, dar_ref, dai_ref, dd_ref):
                ref[...] = jnp.zeros_like(ref)

        dy = dy_ref[...]
        dyb = dy.astype(BF16)
        u = u_ref[...]
        s_re = sr_ref[...]
        s_im = si_ref[...]
        gr_ref[...] = _dot_nt(dyb, cdr_ref[...])
        gi_ref[...] = -_dot_nt(dyb, cdi_ref[...])
        dcdr_ref[...] += _dot_tn(s_re.astype(BF16), dyb)
        dcdi_ref[...] -= _dot_tn(s_im.astype(BF16), dyb)
        keep = jnp.where(first_chunk, 0.0, 1.0)
        pr_ref[0:SUBLANES, :] = hr_ref[...] * keep
        pi_ref[0:SUBLANES, :] = hi_ref[...] * keep
        pr_ref[SUBLANES:, :] = s_re
        pi_ref[SUBLANES:, :] = s_im
        a_re = ar_ref[...]
        a_im = ai_ref[...]

        def step(n, carry):
            c_re, c_im, acc_re, acc_im = carry
            t = tc - 1 - n
            row = pl.ds(t, 1)
            g_re = gr_ref[row, :] + a_re * c_re + a_im * c_im
            g_im = gi_ref[row, :] + a_re * c_im - a_im * c_re
            gr_ref[row, :] = g_re
            gi_ref[row, :] = g_im
            prev = pl.ds(t + SUBLANES - 1, 1)
            p_re = pr_ref[prev, :]
            p_im = pi_ref[prev, :]
            acc_re = acc_re + p_re * g_re + p_im * g_im
            acc_im = acc_im + p_re * g_im - p_im * g_re
            return g_re, g_im, acc_re, acc_im

        zero = jnp.zeros((1, SB), F32)
        c_re, c_im, acc_re, acc_im = lax.fori_loop(0, tc, step, (cr_ref[...], ci_ref[...], zero, zero), unroll=8)
        cr_ref[...] = c_re
        ci_ref[...] = c_im
        dar_ref[...] += acc_re
        dai_ref[...] += acc_im
        gsr = gr_ref[...].astype(BF16)
        gsi = gi_ref[...].astype(BF16)
        ub = u.astype(BF16)
        dbdr_ref[...] += _dot_tn(ub, gsr)
        dbdi_ref[...] += _dot_tn(ub, gsi)
        du_ref[...] = (_dot_nt(gsr, bdr_ref[...]) + _dot_nt(gsi, bdi_ref[...]) + d_ref[...] * dy).astype(BF16)
        dd_ref[...] += _colsum(dy * u)

    def rev(t):
        return nt - 1 - t

    def halo(j, t):
        return (jnp.maximum(rev(t) * halo_blocks - 1, 0), j)

    ublk = pl.BlockSpec((tc, UB), lambda j, t: (rev(t), j))
    sblk = pl.BlockSpec((tc, SB), lambda j, t: (rev(t), j))
    bd_spec = pl.BlockSpec((None, UB, SB), lambda j, t: (j, 0, 0))
    cd_spec = pl.BlockSpec((None, SB, UB), lambda j, t: (j, 0, 0))
    svec = pl.BlockSpec((1, SB), lambda j, t: (0, j))
    uvec = pl.BlockSpec((1, UB), lambda j, t: (0, j))
    sds = jax.ShapeDtypeStruct
    return pl.pallas_call(
        body, name="s5_bwd", grid=(GB, nt),
        in_specs=[ublk, ublk, sblk, sblk, pl.BlockSpec((SUBLANES, SB), halo), pl.BlockSpec((SUBLANES, SB), halo),
                  bd_spec, bd_spec, cd_spec, cd_spec, svec, svec, uvec],
        out_specs=[ublk, bd_spec, bd_spec, cd_spec, cd_spec, svec, svec, uvec],
        out_shape=[sds((T, GB * UB), BF16), sds((GB, UB, SB), F32), sds((GB, UB, SB), F32),
                   sds((GB, SB, UB), F32), sds((GB, SB, UB), F32), sds((1, GB * SB), F32),
                   sds((1, GB * SB), F32), sds((1, GB * UB), F32)],
        scratch_shapes=[pltpu.VMEM((tc, SB), F32), pltpu.VMEM((tc, SB), F32),
                        pltpu.VMEM((tc + SUBLANES, SB), F32), pltpu.VMEM((tc + SUBLANES, SB), F32),
                        pltpu.VMEM((1, SB), F32), pltpu.VMEM((1, SB), F32)],
        compiler_params=_params("arbitrary", "arbitrary"),
    )(dy, proj, sr, si, sr, si, bdr, bdi, cdr, cdi, ar, ai, dvec)


def _shift_down(v, k):
    rows = lax.broadcasted_iota(jnp.int32, v.shape, 0)
    return jnp.where(rows >= k, pltpu.roll(v, k, 0), 0.0)


def _shift_up(v, k):
    n = v.shape[0]
    rows = lax.broadcasted_iota(jnp.int32, v.shape, 0)
    return jnp.where(rows < n - k, pltpu.roll(v, n - k, 0), 0.0)


def _conv_specs(T, cb, n_s5_blocks, n_conv_blocks):
    gb = pl.BlockSpec((T, cb), lambda j: (0, n_s5_blocks + j))
    gc = pl.BlockSpec((T, cb), lambda j: (0, n_s5_blocks + n_conv_blocks + j))
    hh = pl.BlockSpec((T, cb), lambda j: (0, n_s5_blocks + 2 * n_conv_blocks + j))
    return gb, gc, hh


def _conv_fwd(proj, cw, cbias, d_s5, d_conv):
    T = proj.shape[0]
    cb = _tile(d_conv, 256, LANES)

    def body(gb_ref, gc_ref, hh_ref, w_ref, b_ref, z_ref):
        v = gc_ref[...] * hh_ref[...]
        w = w_ref[...]
        cv = b_ref[...] + w[0:1, :] * _shift_down(v, 2) + w[1:2, :] * _shift_down(v, 1) + w[2:3, :] * v
        z_ref[...] = gb_ref[...] * cv

    gb, gc, hh = _conv_specs(T, cb, d_s5 // cb, d_conv // cb)
    col = pl.BlockSpec((T, cb), lambda j: (0, j))
    return pl.pallas_call(
        body, name="conv_fwd", grid=(d_conv // cb,),
        in_specs=[gb, gc, hh, pl.BlockSpec((CONV_W, cb), lambda j: (0, j)), pl.BlockSpec((1, cb), lambda j: (0, j))],
        out_specs=col, out_shape=jax.ShapeDtypeStruct((T, d_conv), F32),
        compiler_params=_params("arbitrary"),
    )(proj, proj, proj, cw, cbias)


def _conv_bwd(dz, proj, cw, cbias, d_s5, d_conv):
    T = proj.shape[0]
    cb = _tile(d_conv, 256, LANES)

    def body(dz_ref, gb_ref, gc_ref, hh_ref, w_ref, b_ref, dgb_ref, dgc_ref, dhh_ref, dw_ref, db_ref):
        gc = gc_ref[...]
        hh = hh_ref[...]
        dz = dz_ref[...]
        w = w_ref[...]
        v = gc * hh
        v1 = _shift_down(v, 1)
        v2 = _shift_down(v, 2)
        cv = b_ref[...] + w[0:1, :] * v2 + w[1:2, :] * v1 + w[2:3, :] * v
        dgb_ref[...] = (dz * cv).astype(BF16)
        dcv = dz * gb_ref[...]
        dv = w[2:3, :] * dcv + w[1:2, :] * _shift_up(dcv, 1) + w[0:1, :] * _shift_up(dcv, 2)
        dgc_ref[...] = (dv * hh).astype(BF16)
        dhh_ref[...] = (dv * gc).astype(BF16)
        dw_ref[0:1, :] = _colsum(dcv * v2)
        dw_ref[1:2, :] = _colsum(dcv * v1)
        dw_ref[2:3, :] = _colsum(dcv * v)
        db_ref[...] = _colsum(dcv)

    gb, gc, hh = _conv_specs(T, cb, d_s5 // cb, d_conv // cb)
    col = pl.BlockSpec((T, cb), lambda j: (0, j))
    wspec = pl.BlockSpec((CONV_W, cb), lambda j: (0, j))
    bspec = pl.BlockSpec((1, cb), lambda j: (0, j))
    sds = jax.ShapeDtypeStruct
    return pl.pallas_call(
        body, name="conv_bwd", grid=(d_conv // cb,),
        in_specs=[col, gb, gc, hh, wspec, bspec],
        out_specs=[col, col, col, wspec, bspec],
        out_shape=[sds((T, d_conv), BF16), sds((T, d_conv), BF16), sds((T, d_conv), BF16),
                   sds((CONV_W, d_conv), F32), sds((1, d_conv), F32)],
        compiler_params=_params("arbitrary"),
    )(dz, proj, proj, proj, cw, cbias)


def _rms(v, g):
    rstd = lax.rsqrt(_rowmean(v * v) + RMS_EPS)
    return v * rstd * g, rstd


def _rms_bwd(dyn, v, rstd, g):
    w = dyn * g
    return rstd * w - v * (rstd * rstd * rstd) * _rowmean(w * v), _colsum(dyn * v * rstd)


def _mix_post(y, z, wglu, g_s5, g_conv):
    T, C = y.shape
    tm = _tile(T, 256, 16)

    def body(y_ref, z_ref, w_ref, gs_ref, gc_ref, m_ref, gl_ref):
        ge = jax.nn.gelu(y_ref[...])
        gl = _dot_nn(ge.astype(BF16), w_ref[...])
        gl_ref[...] = gl
        yn, _ = _rms(ge * jax.nn.sigmoid(gl), gs_ref[...])
        zn, _ = _rms(z_ref[...], gc_ref[...])
        m_ref[:, 0:C] = yn.astype(BF16)
        m_ref[:, C:2 * C] = zn.astype(BF16)

    row = pl.BlockSpec((tm, C), lambda i: (i, 0))
    vec = pl.BlockSpec((1, C), lambda i: (0, 0))
    return pl.pallas_call(
        body, name="mix_post", grid=(T // tm,),
        in_specs=[row, row, pl.BlockSpec((C, C), lambda i: (0, 0)), vec, vec],
        out_specs=[pl.BlockSpec((tm, 2 * C), lambda i: (i, 0)), row],
        out_shape=[jax.ShapeDtypeStruct((T, 2 * C), BF16), jax.ShapeDtypeStruct((T, C), F32)],
        compiler_params=_params("arbitrary"),
    )(y, z, wglu, g_s5, g_conv)


def _mix_post_bwd(dm, y, gl, z, wglu, g_s5, g_conv):
    T, C = y.shape
    tm = _tile(T, 256, 16)

    def body(dm_ref, y_ref, gl_ref, z_ref, w_ref, gs_ref, gc_ref, dy_ref, dz_ref, dw_ref, dgs_ref, dgc_ref):
        @pl.when(pl.program_id(0) == 0)
        def _():
            dw_ref[...] = jnp.zeros_like(dw_ref)
            dgs_ref[...] = jnp.zeros_like(dgs_ref)
            dgc_ref[...] = jnp.zeros_like(dgc_ref)

        yv = y_ref[...]
        ge, gelu_vjp = jax.vjp(jax.nn.gelu, yv)
        gl = gl_ref[...]
        sg = jax.nn.sigmoid(gl)
        y2 = ge * sg
        _, rstd_y = _rms(y2, gs_ref[...])
        dy2, dgs = _rms_bwd(dm_ref[:, 0:C], y2, rstd_y, gs_ref[...])
        dgs_ref[...] += dgs
        dgl = (dy2 * ge * sg * (1.0 - sg)).astype(BF16)
        dge = dy2 * sg + _dot_nt(dgl, w_ref[...])
        dw_ref[...] += _dot_tn(ge.astype(BF16), dgl)
        dy_ref[...] = gelu_vjp(dge)[0]
        zv = z_ref[...]
        _, rstd_z = _rms(zv, gc_ref[...])
        dz, dgc = _rms_bwd(dm_ref[:, C:2 * C], zv, rstd_z, gc_ref[...])
        dz_ref[...] = dz
        dgc_ref[...] += dgc

    row = pl.BlockSpec((tm, C), lambda i: (i, 0))
    vec = pl.BlockSpec((1, C), lambda i: (0, 0))
    full = pl.BlockSpec((C, C), lambda i: (0, 0))
    sds = jax.ShapeDtypeStruct
    return pl.pallas_call(
        body, name="mix_post_bwd", grid=(T // tm,),
        in_specs=[pl.BlockSpec((tm, 2 * C), lambda i: (i, 0)), row, row, row, full, vec, vec],
        out_specs=[row, row, full, vec, vec],
        out_shape=[sds((T, C), F32), sds((T, C), F32), sds((C, C), F32), sds((1, C), F32), sds((1, C), F32)],
        compiler_params=_params("arbitrary"),
    )(dm, y, gl, z, wglu, g_s5, g_conv)


def _adamw(w, g, m, v):
    m = ADAM_B1 * m + (1.0 - ADAM_B1) * g
    v = ADAM_B2 * v + (1.0 - ADAM_B2) * (g * g)
    m_hat = m / (1.0 - ADAM_B1 ** ADAM_STEP)
    v_hat = v / (1.0 - ADAM_B2 ** ADAM_STEP)
    return -ADAM_LR * (m_hat / (jnp.sqrt(v_hat) + ADAM_EPS) + ADAM_WD * w), m, v


def _sum_parts(p_ref):
    total = p_ref[0].astype(F32)
    for d in range(1, N_DEV):
        total = total + p_ref[d].astype(F32)
    return total


def _row_tile(R, C, n_streams):
    budget = VMEM_LIMIT // 3 // (n_streams * C * 4)
    return _tile(R, max(BF16_ROWS, budget), BF16_ROWS)


def _reduce_parts(parts):
    L, _, R, C = parts.shape
    tr = _row_tile(R, C, N_DEV + 1)

    def body(p_ref, o_ref):
        o_ref[...] = _sum_parts(p_ref)

    return pl.pallas_call(
        body, name="reduce_parts", grid=(L, R // tr),
        in_specs=[pl.BlockSpec((None, N_DEV, tr, C), lambda l, i: (l, 0, i, 0))],
        out_specs=pl.BlockSpec((None, tr, C), lambda l, i: (l, i, 0)),
        out_shape=jax.ShapeDtypeStruct((L, R, C), F32),
        compiler_params=_params("arbitrary", "arbitrary"),
    )(parts)


def _adamw_update(w, m, v, grad=None, parts=None):
    L, R, C = w.shape
    from_parts = parts is not None
    tr = _row_tile(R, C, (N_DEV if from_parts else 1) + 7)

    def body(g_in_ref, w_ref, m_ref, v_ref, g_ref, d_ref, nm_ref, nv_ref):
        g = _sum_parts(g_in_ref) if from_parts else g_in_ref[...]
        delta, nm, nv = _adamw(w_ref[...], g, m_ref[...], v_ref[...])
        g_ref[...] = g
        d_ref[...] = delta
        nm_ref[...] = nm
        nv_ref[...] = nv

    blk = pl.BlockSpec((None, tr, C), lambda l, i: (l, i, 0))
    g_spec = pl.BlockSpec((None, N_DEV, tr, C), lambda l, i: (l, 0, i, 0)) if from_parts else blk
    out = jax.ShapeDtypeStruct((L, R, C), F32)
    return pl.pallas_call(
        body, name="adamw_parts" if from_parts else "adamw", grid=(L, R // tr),
        in_specs=[g_spec, blk, blk, blk], out_specs=[blk, blk, blk, blk], out_shape=[out, out, out, out],
        compiler_params=_params("arbitrary", "arbitrary"),
    )(parts if from_parts else grad, w, m, v)


def _me():
    x, y, c = (lax.axis_index(a) for a in AXES)
    return x, y, c, 4 * x + 2 * y + c


def _peer(rel):
    x, y, c, _ = _me()
    px = 1 - x if rel & 4 else x
    py = 1 - y if rel & 2 else y
    pc = 1 - c if rel & 1 else c
    return (px, py, pc), 4 * px + 2 * py + pc


HBM_SPEC = pl.BlockSpec(memory_space=pltpu.HBM)


def _exchange(body_copies, name, operands, out_shape, n_arrays, aliases=None):
    n_in = len(operands)

    def body(*refs):
        ins, outs = refs[:n_in], refs[n_in:n_in + len(out_shape)]
        send_sems, recv_sems, local_sems = refs[n_in + len(out_shape):]
        local, remote = body_copies(ins, outs, send_sems, recv_sems, local_sems)
        for cp in local:
            cp.start()
        for send, _ in remote:
            send.start()
        for send, landing in remote:
            send.wait_send()
            landing.wait_recv()
        for cp in local:
            cp.wait()

    return pl.pallas_call(
        body, name=name, in_specs=[HBM_SPEC] * n_in, out_specs=[HBM_SPEC] * len(out_shape), out_shape=out_shape,
        scratch_shapes=[pltpu.SemaphoreType.DMA((n_arrays, N_DEV - 1)), pltpu.SemaphoreType.DMA((n_arrays, N_DEV - 1)),
                        pltpu.SemaphoreType.DMA((n_arrays,))],
        input_output_aliases=aliases or {},
        compiler_params=pltpu.CompilerParams(has_side_effects=True),
    )(*operands)


def _all_gather(shards, layer, name):
    n = len(shards)

    def copies(ins, outs, send_sems, recv_sems, local_sems):
        me = _me()[3]
        local, remote = [], []
        for k in range(n):
            src = ins[k].at[layer]
            local.append(pltpu.make_async_copy(src, outs[k].at[me], local_sems.at[k]))
            for rel in range(1, N_DEV):
                dev, blk = _peer(rel)
                sems = dict(send_sem=send_sems.at[k, rel - 1], recv_sem=recv_sems.at[k, rel - 1],
                            device_id=dev, device_id_type=MESH)
                remote.append((pltpu.make_async_remote_copy(src_ref=src, dst_ref=outs[k].at[me], **sems),
                               pltpu.make_async_remote_copy(src_ref=src, dst_ref=outs[k].at[blk], **sems)))
        return local, remote

    out_shape = [jax.ShapeDtypeStruct((N_DEV,) + s.shape[1:], s.dtype) for s in shards]
    return _exchange(copies, name, list(shards), out_shape, n)


def _scatter_parts(fulls, bufs, layer, name):
    n = len(fulls)

    def copies(ins, outs, send_sems, recv_sems, local_sems):
        me = _me()[3]
        local, remote = [], []
        for k in range(n):
            local.append(pltpu.make_async_copy(ins[k].at[me], outs[k].at[layer, me], local_sems.at[k]))
            for rel in range(1, N_DEV):
                dev, blk = _peer(rel)
                sems = dict(send_sem=send_sems.at[k, rel - 1], recv_sem=recv_sems.at[k, rel - 1],
                            device_id=dev, device_id_type=MESH)
                remote.append((pltpu.make_async_remote_copy(src_ref=ins[k].at[blk], dst_ref=outs[k].at[layer, me], **sems),
                               pltpu.make_async_remote_copy(src_ref=ins[k].at[blk], dst_ref=outs[k].at[layer, blk], **sems)))
        return local, remote

    out_shape = [jax.ShapeDtypeStruct(b.shape, b.dtype) for b in bufs]
    aliases = {n + k: k for k in range(n)}
    return _exchange(copies, name, list(fulls) + list(bufs), out_shape, n, aliases)


def _block_diag(blocks, row_major):
    L, GB, g, P, N = blocks.shape
    eye = jnp.eye(g, dtype=blocks.dtype)
    if row_major:
        return jnp.einsum("lbgpn,gh->lbgphn", blocks, eye).reshape(L, GB, g * P, g * N)
    return jnp.einsum("lbgpn,gh->lbhngp", blocks, eye).reshape(L, GB, g * N, g * P)


def _diag_blocks(mat, g, P, N, row_major):
    GB = mat.shape[0]
    eye = jnp.eye(g, dtype=mat.dtype)
    if row_major:
        return jnp.einsum("bgphn,gh->bgpn", mat.reshape(GB, g, P, g, N), eye)
    return jnp.einsum("bhngp,gh->bgpn", mat.reshape(GB, g, N, g, P), eye)


def _pack(arrays, rows_multiple):
    flat = jnp.concatenate([a.reshape(-1).astype(F32) for a in arrays])
    pad = (-flat.shape[0]) % (rows_multiple * LANES)
    return jnp.pad(flat, (0, pad)).reshape(-1, LANES)


def _unpack(packed, shapes):
    flat = packed.reshape(-1)
    out, pos = [], 0
    for s in shapes:
        n = math.prod(s)
        out.append(flat[pos:pos + n].reshape(s))
        pos += n
    return out


SMALL = ["ln1_g", "ln1_b", "s5_lam_re", "s5_lam_im", "s5_log_dt", "s5_b_re", "s5_b_im", "s5_c_re", "s5_c_im", "s5_d",
         "conv_b", "g_s5", "g_conv", "ln2_g", "ln2_b", "ln3_g", "ln3_b"]
WEIGHTS = ["ffn1_gate", "ffn1_up", "ffn1_down", "ln1_g", "ln1_b", "w_in", "s5_lam_re", "s5_lam_im", "s5_log_dt",
           "s5_b_re", "s5_b_im", "s5_c_re", "s5_c_im", "s5_d", "s5_w_glu", "conv_w", "conv_b", "g_s5", "g_conv",
           "w_out", "ln2_g", "ln2_b", "ffn2_gate", "ffn2_up", "ffn2_down", "ln3_g", "ln3_b"]
TRANSPOSED = ["ffn1_gate", "ffn1_up", "w_in", "ffn2_gate", "ffn2_up"]
ROW_SHARDED = ["ffn1_down", "s5_w_glu", "w_out", "ffn2_down"]
BIG = ["ffn1_gate", "ffn1_up", "ffn1_down", "w_in", "s5_w_glu", "w_out", "ffn2_gate", "ffn2_up", "ffn2_down"]


def _train_step(x, target, w, m, v):
    T, D = x.shape
    L = w["ln1_g"].shape[0]
    alpha = (2.0 * L) ** 0.25
    G = w["s5_log_dt"].shape[1]
    d_s5 = G * S5_P
    d_conv = w["conv_b"].shape[1]
    GB = G // S5_GROUPS_PER_BLOCK
    me = _me()[3]

    shards = {n: jnp.swapaxes(w[n], 1, 2).astype(BF16) for n in TRANSPOSED}
    shards.update({n: w[n].astype(BF16) for n in ROW_SHARDED})
    conv_w_rows = jnp.pad(w["conv_w"], ((0, 0), (0, SUBLANES - CONV_W), (0, 0)))
    gathered = []
    for l in range(L):
        parts = _all_gather([shards[n] for n in BIG] + [conv_w_rows], l, f"gather_weights_{l}")
        full = {n: p.reshape(-1, p.shape[-1]) for n, p in zip(BIG, parts[:-1])}
        full["conv_w"] = jnp.swapaxes(parts[-1][:, :CONV_W, :], 0, 1).reshape(CONV_W, d_conv)
        gathered.append(full)

    lre = w["s5_lam_re"].reshape(L * G, S5_N)
    lim = w["s5_lam_im"].reshape(L * G, S5_N)
    ldt = w["s5_log_dt"].reshape(L * G, 1)
    b_re = jnp.transpose(w["s5_b_re"], (3, 0, 1, 2)).reshape(S5_P, L * G, S5_N)
    b_im = jnp.transpose(w["s5_b_im"], (3, 0, 1, 2)).reshape(S5_P, L * G, S5_N)
    ab_re, ab_im, bb_re, bb_im = _s5_params_fwd(lre, lim, ldt, b_re, b_im)

    def groups(bb):
        return jnp.transpose(bb.reshape(S5_P, L, GB, S5_GROUPS_PER_BLOCK, S5_N), (1, 2, 3, 0, 4))

    bd_re = _block_diag(groups(bb_re), True).astype(BF16)
    bd_im = _block_diag(groups(bb_im), True).astype(BF16)
    c_shape = (L, GB, S5_GROUPS_PER_BLOCK, S5_P, S5_N)
    cd_re = _block_diag(w["s5_c_re"].reshape(c_shape), False).astype(BF16)
    cd_im = _block_diag(w["s5_c_im"].reshape(c_shape), False).astype(BF16)
    a_re = ab_re.reshape(L, 1, G * S5_N)
    a_im = ab_im.reshape(L, 1, G * S5_N)
    d_vec = w["s5_d"].reshape(L, 1, d_s5)

    def vec(name, l):
        return w[name][l].reshape(1, -1)

    saved = []
    x_in, x_in_b = x, x.astype(BF16)
    for l in range(L):
        gw = gathered[l]
        s = {"x0b": x_in_b}
        s["g1"], s["u1"], s["h1"] = _ffn_up(x_in_b, gw["ffn1_gate"], gw["ffn1_up"])
        x1, s["x1b"], s["xh1"], s["rstd1"] = _mm_res_ln(s["h1"], gw["ffn1_down"], x_in, vec("ln1_g", l),
                                                         vec("ln1_b", l), 0.5, alpha)
        s["proj"] = _mm_nt(s["x1b"], gw["w_in"])
        s["y"], s["sr"], s["si"] = _s5_fwd(s["proj"], bd_re[l], bd_im[l], cd_re[l], cd_im[l], a_re[l], a_im[l],
                                           d_vec[l])
        s["z"] = _conv_fwd(s["proj"], gw["conv_w"], vec("conv_b", l), d_s5, d_conv)
        s["mcat"], s["gl"] = _mix_post(s["y"], s["z"], gw["s5_w_glu"], vec("g_s5", l), vec("g_conv", l))
        x2, s["x2b"], s["xh2"], s["rstd2"] = _mm_res_ln(s["mcat"], gw["w_out"], x1, vec("ln2_g", l),
                                                         vec("ln2_b", l), 1.0, alpha)
        s["g2"], s["u2"], s["h2"] = _ffn_up(s["x2b"], gw["ffn2_gate"], gw["ffn2_up"])
        x3, x3b, s["xh3"], s["rstd3"] = _mm_res_ln(s["h2"], gw["ffn2_down"], x2, vec("ln3_g", l), vec("ln3_b", l),
                                                   0.5, alpha)
        saved.append(s)
        x_in, x_in_b = x3, x3b

    last = saved[L - 1]
    dr, drb, dg, db, loss = _loss_ln_bwd(x_in, target, last["xh3"], last["rstd3"], vec("ln3_g", L - 1))
    small = [dict() for _ in range(L)]
    small[L - 1]["ln3_g"], small[L - 1]["ln3_b"] = dg, db
    bufs = {n: lax.empty((L, N_DEV) + shards[n].shape[1:], BF16) for n in BIG}
    grad_x = None
    for l in reversed(range(L)):
        gw, s, sm = gathered[l], saved[l], small[l]
        full = {}

        def ffn_bwd(dr, drb, tag, xb_in, ln):
            dgp, dup = _ffn_down_bwd(drb, gw[f"ffn{tag}_down"], s[f"g{tag}"], s[f"u{tag}"])
            full[f"ffn{tag}_down"] = _mm_tn(s[f"h{tag}"], drb, 0.5, BF16)
            full[f"ffn{tag}_gate"] = _mm_tn(dgp, xb_in, 1.0, BF16)
            full[f"ffn{tag}_up"] = _mm_tn(dup, xb_in, 1.0, BF16)
            return _mm_dx([(dgp, gw[f"ffn{tag}_gate"]), (dup, gw[f"ffn{tag}_up"])], dr, alpha, ln)

        dr, drb, sm["ln2_g"], sm["ln2_b"] = ffn_bwd(dr, drb, 2, s["x2b"], (s["xh2"], s["rstd2"], vec("ln2_g", l)))
        dm = _mm_nt(drb, gw["w_out"])
        full["w_out"] = _mm_tn(s["mcat"], drb, 1.0, BF16)
        dy, dz, dwglu, sm["g_s5"], sm["g_conv"] = _mix_post_bwd(dm, s["y"], s["gl"], s["z"], gw["s5_w_glu"],
                                                                vec("g_s5", l), vec("g_conv", l))
        full["s5_w_glu"] = dwglu.astype(BF16)
        du, dbd_re, dbd_im, dcd_re, dcd_im, sm["d_ab_re"], sm["d_ab_im"], sm["s5_d"] = _s5_bwd(
            dy, s["proj"], s["sr"], s["si"], bd_re[l], bd_im[l], cd_re[l], cd_im[l], a_re[l], a_im[l], d_vec[l])
        gsz = (S5_GROUPS_PER_BLOCK, S5_P, S5_N)
        sm["d_bb_re"] = _diag_blocks(dbd_re, *gsz, True)
        sm["d_bb_im"] = _diag_blocks(dbd_im, *gsz, True)
        sm["s5_c_re"] = _diag_blocks(dcd_re, *gsz, False).reshape(G, S5_P, S5_N)
        sm["s5_c_im"] = _diag_blocks(dcd_im, *gsz, False).reshape(G, S5_P, S5_N)
        dgb, dgc, dhh, sm["conv_w"], sm["conv_b"] = _conv_bwd(dz, s["proj"], gw["conv_w"], vec("conv_b", l),
                                                              d_s5, d_conv)
        dproj = jnp.concatenate([du, dgb, dgc, dhh], axis=1)
        full["w_in"] = _mm_tn(dproj, s["x1b"], 1.0, BF16)
        dr, drb, sm["ln1_g"], sm["ln1_b"] = _mm_dx([(dproj, gw["w_in"])], dr, alpha,
                                                   (s["xh1"], s["rstd1"], vec("ln1_g", l)))
        if l > 0:
            prev = saved[l - 1]
            dr, drb, small[l - 1]["ln3_g"], small[l - 1]["ln3_b"] = ffn_bwd(
                dr, drb, 1, s["x0b"], (prev["xh3"], prev["rstd3"], vec("ln3_g", l - 1)))
        else:
            (grad_x,) = ffn_bwd(dr, drb, 1, s["x0b"], None)
        new = _scatter_parts([full[n].reshape((N_DEV, -1) + full[n].shape[1:]) for n in BIG], [bufs[n] for n in BIG],
                             l, f"scatter_grads_{l}")
        bufs = dict(zip(BIG, new))

    def stack(key):
        return jnp.stack([small[l][key] for l in range(L)])

    d_bb_re = jnp.transpose(stack("d_bb_re").reshape(L * G, S5_P, S5_N), (1, 0, 2))
    d_bb_im = jnp.transpose(stack("d_bb_im").reshape(L * G, S5_P, S5_N), (1, 0, 2))
    g_lre, g_lim, g_ldt, g_bre, g_bim = _s5_params_bwd(
        lre, lim, ldt, b_re, b_im, stack("d_ab_re").reshape(L * G, S5_N), stack("d_ab_im").reshape(L * G, S5_N),
        d_bb_re, d_bb_im)
    part = {n: stack(n) for n in ["ln1_g", "ln1_b", "s5_c_re", "s5_c_im", "s5_d", "conv_b", "g_s5", "g_conv", "ln2_g",
                                  "ln2_b", "ln3_g", "ln3_b", "conv_w"]}
    part["s5_lam_re"], part["s5_lam_im"], part["s5_log_dt"] = g_lre, g_lim, g_ldt
    part["s5_b_re"] = jnp.transpose(g_bre, (1, 2, 0))
    part["s5_b_im"] = jnp.transpose(g_bim, (1, 2, 0))

    small_names = SMALL + ["conv_w"]
    small_shapes = [w[n].shape for n in SMALL] + [(L, CONV_W, d_conv)]
    packed = _pack([part[n] for n in small_names], N_DEV * PACK_ROWS)
    rows = packed.shape[0] // N_DEV
    (landed,) = _scatter_parts([packed.reshape(N_DEV, rows, LANES)], [lax.empty((1, N_DEV, rows, LANES), F32)], 0,
                               "scatter_small")
    mine = _reduce_parts(landed)
    (summed,) = _all_gather([mine], 0, "gather_small")
    small_grads = dict(zip(small_names, _unpack(summed, small_shapes)))

    out = {}

    def update(name, w3, m3, v3, shape, **grad):
        res = _adamw_update(w3, m3, v3, **grad)
        out[name] = [r.reshape(shape) for r in res]

    for n in ROW_SHARDED:
        update(n, w[n], m[n], v[n], w[n].shape, parts=bufs[n])
    for n in TRANSPOSED:
        g = jnp.swapaxes(_reduce_parts(bufs[n]), 1, 2)
        update(n, w[n], m[n], v[n], w[n].shape, grad=g)
    cw_shape = w["conv_w"].shape
    g_cw = lax.dynamic_slice_in_dim(small_grads["conv_w"], me * cw_shape[2], cw_shape[2], axis=2)
    update("conv_w", w["conv_w"], m["conv_w"], v["conv_w"], cw_shape, grad=g_cw)
    sizes = [w[n].shape for n in SMALL]
    pw, pm, pv, pg = (_pack([d[n] for n in SMALL], PACK_ROWS)[None] for d in (w, m, v, small_grads))
    for name, res in zip(SMALL, zip(*[_unpack(r, sizes) for r in _adamw_update(pw, pm, pv, grad=pg)])):
        out[name] = list(res)

    loss = lax.psum(loss[0, 0], AXES)
    return loss, grad_x, out


def kernel(x, ffn1_gate, ffn1_up, ffn1_down, ln1_g, ln1_b, w_in, s5_lam_re, s5_lam_im, s5_log_dt, s5_b_re, s5_b_im, s5_c_re, s5_c_im, s5_d, s5_w_glu, conv_w, conv_b, g_s5, g_conv, w_out, ln2_g, ln2_b, ffn2_gate, ffn2_up, ffn2_down, ln3_g, ln3_b, loss_target, m_ffn1_gate, m_ffn1_up, m_ffn1_down, m_ln1_g, m_ln1_b, m_w_in, m_s5_lam_re, m_s5_lam_im, m_s5_log_dt, m_s5_b_re, m_s5_b_im, m_s5_c_re, m_s5_c_im, m_s5_d, m_s5_w_glu, m_conv_w, m_conv_b, m_g_s5, m_g_conv, m_w_out, m_ln2_g, m_ln2_b, m_ffn2_gate, m_ffn2_up, m_ffn2_down, m_ln3_g, m_ln3_b, v_ffn1_gate, v_ffn1_up, v_ffn1_down, v_ln1_g, v_ln1_b, v_w_in, v_s5_lam_re, v_s5_lam_im, v_s5_log_dt, v_s5_b_re, v_s5_b_im, v_s5_c_re, v_s5_c_im, v_s5_d, v_s5_w_glu, v_conv_w, v_conv_b, v_g_s5, v_g_conv, v_w_out, v_ln2_g, v_ln2_b, v_ffn2_gate, v_ffn2_up, v_ffn2_down, v_ln3_g, v_ln3_b):
    given = dict(locals())
    w = {n: given[n] for n in WEIGHTS}
    m = {n: given["m_" + n] for n in WEIGHTS}
    v = {n: given["v_" + n] for n in WEIGHTS}
    T, D = x.shape[-2:]
    loss, grad_x, out = _train_step(x.reshape(T, D), loss_target.reshape(T, D), w, m, v)
    results = [loss, grad_x.reshape(x.shape)]
    for i in range(4):
        results += [out[n][i] for n in WEIGHTS]
    return tuple(results)
```

```python
import functools
import math

import jax
import jax.numpy as jnp
from jax import lax
from jax.experimental import pallas as pl
from jax.experimental.pallas import tpu as pltpu

F32 = jnp.float32
BF16 = jnp.bfloat16
MESH = pl.DeviceIdType.MESH
AXES = ("x", "y", "c")
N_DEV = 8

S5_P = 16
S5_N = 64
CONV_W = 3
LN_EPS = 1e-5
RMS_EPS = 1e-6
ADAM_LR = 0.001
ADAM_B1 = 0.9
ADAM_B2 = 0.999
ADAM_EPS = 1e-08
ADAM_WD = 0.01
ADAM_STEP = 10

V7X_VMEM_BYTES = 64 * 1024 * 1024
VMEM_LIMIT = V7X_VMEM_BYTES * 7 // 8
LANES = 128
SUBLANES = 8
BF16_ROWS = 16
PACK_ROWS = 512
S5_GROUPS_PER_BLOCK = LANES // S5_P
S5_STATE_BLOCK = S5_GROUPS_PER_BLOCK * S5_N

HBM_SPEC = pl.BlockSpec(memory_space=pltpu.HBM)
SEM_SPEC = pl.BlockSpec(memory_space=pltpu.SEMAPHORE)
ANY_SPEC = pl.BlockSpec(memory_space=pl.ANY)


def _tile(n, pref, align):
    best = None
    d = align
    while d <= min(n, pref):
        if n % d == 0:
            best = d
        d += align
    return best if best is not None else n


def _params(*sem):
    return pltpu.CompilerParams(dimension_semantics=sem, vmem_limit_bytes=VMEM_LIMIT)


def _dot_nn(a, b):
    return lax.dot_general(a, b, (((1,), (0,)), ((), ())), preferred_element_type=F32)


def _dot_nt(a, b):
    return lax.dot_general(a, b, (((1,), (1,)), ((), ())), preferred_element_type=F32)


def _dot_tn(a, b):
    return lax.dot_general(a, b, (((0,), (0,)), ((), ())), preferred_element_type=F32)


def _colsum(v):
    return jnp.sum(v, axis=0, keepdims=True)


def _rowmean(v):
    return jnp.mean(v, axis=-1, keepdims=True)


def _ffn_up(xb, wg, wu):
    T, D = xb.shape
    F = wg.shape[0]
    tm = _tile(T, 512, 16)
    tn = _tile(F, 512, LANES)

    def body(x_ref, wg_ref, wu_ref, g_ref, u_ref, h_ref):
        x = x_ref[...]
        g = _dot_nt(x, wg_ref[...])
        u = _dot_nt(x, wu_ref[...])
        g_ref[...] = g
        u_ref[...] = u
        h_ref[...] = (g * jax.nn.sigmoid(g) * u).astype(BF16)

    w_spec = pl.BlockSpec((tn, D), lambda j, i: (j, 0))
    o_spec = pl.BlockSpec((tm, tn), lambda j, i: (i, j))
    return pl.pallas_call(
        body, name="ffn_up", grid=(F // tn, T // tm),
        in_specs=[pl.BlockSpec((tm, D), lambda j, i: (i, 0)), w_spec, w_spec],
        out_specs=[o_spec, o_spec, o_spec],
        out_shape=[jax.ShapeDtypeStruct((T, F), F32), jax.ShapeDtypeStruct((T, F), F32),
                   jax.ShapeDtypeStruct((T, F), BF16)],
        compiler_params=_params("arbitrary", "arbitrary"),
    )(xb, wg, wu)


def _mm_res_ln(a, w, res, g, b, scale, alpha):
    T, K = a.shape
    D = w.shape[1]
    tm = _tile(T, 256, 16)
    tk = _tile(K, 512, LANES)
    nk = K // tk

    def body(a_ref, w_ref, res_ref, g_ref, b_ref, xo_ref, xb_ref, xh_ref, rstd_ref, acc_ref):
        k = pl.program_id(1)

        @pl.when(k == 0)
        def _():
            acc_ref[...] = jnp.zeros_like(acc_ref)

        acc_ref[...] += _dot_nn(a_ref[...], w_ref[...])

        @pl.when(k == nk - 1)
        def _():
            r = alpha * res_ref[...] + scale * acc_ref[...]
            xc = r - _rowmean(r)
            rstd = lax.rsqrt(_rowmean(xc * xc) + LN_EPS)
            xh = xc * rstd
            xo = xh * g_ref[...] + b_ref[...]
            xo_ref[...] = xo
            xb_ref[...] = xo.astype(BF16)
            xh_ref[...] = xh
            rstd_ref[...] = rstd

    row = pl.BlockSpec((tm, D), lambda i, k: (i, 0))
    vec = pl.BlockSpec((1, D), lambda i, k: (0, 0))
    return pl.pallas_call(
        body, name="mm_res_ln", grid=(T // tm, nk),
        in_specs=[pl.BlockSpec((tm, tk), lambda i, k: (i, k)), pl.BlockSpec((tk, D), lambda i, k: (k, 0)),
                  row, vec, vec],
        out_specs=[row, row, row, pl.BlockSpec((tm, 1), lambda i, k: (i, 0))],
        out_shape=[jax.ShapeDtypeStruct((T, D), F32), jax.ShapeDtypeStruct((T, D), BF16),
                   jax.ShapeDtypeStruct((T, D), F32), jax.ShapeDtypeStruct((T, 1), F32)],
        scratch_shapes=[pltpu.VMEM((tm, D), F32)],
        compiler_params=_params("arbitrary", "arbitrary"),
    )(a, w, res, g, b)


def _mm_nt(a, w):
    M, K = a.shape
    N = w.shape[0]
    tm = _tile(M, 512, 16)
    tn = _tile(N, 512, LANES)

    def body(a_ref, w_ref, o_ref):
        o_ref[...] = _dot_nt(a_ref[...], w_ref[...])

    return pl.pallas_call(
        body, name="mm_nt", grid=(N // tn, M // tm),
        in_specs=[pl.BlockSpec((tm, K), lambda j, i: (i, 0)), pl.BlockSpec((tn, K), lambda j, i: (j, 0))],
        out_specs=pl.BlockSpec((tm, tn), lambda j, i: (i, j)),
        out_shape=jax.ShapeDtypeStruct((M, N), F32),
        compiler_params=_params("arbitrary", "arbitrary"),
    )(a, w)


def _mm_tn(a, b, scale, out_dtype):
    T, M = a.shape
    N = b.shape[1]
    tm = _tile(M, 512, LANES)
    tn = _tile(N, 1024, LANES)
    tk = _tile(T, 512, 16)
    nk = T // tk

    def body(a_ref, b_ref, o_ref, acc_ref):
        k = pl.program_id(2)

        @pl.when(k == 0)
        def _():
            acc_ref[...] = jnp.zeros_like(acc_ref)

        acc_ref[...] += _dot_tn(a_ref[...], b_ref[...])

        @pl.when(k == nk - 1)
        def _():
            o_ref[...] = (scale * acc_ref[...]).astype(out_dtype)

    return pl.pallas_call(
        body, name="mm_tn", grid=(M // tm, N // tn, nk),
        in_specs=[pl.BlockSpec((tk, tm), lambda i, j, k: (k, i)), pl.BlockSpec((tk, tn), lambda i, j, k: (k, j))],
        out_specs=pl.BlockSpec((tm, tn), lambda i, j, k: (i, j)),
        out_shape=jax.ShapeDtypeStruct((M, N), out_dtype),
        scratch_shapes=[pltpu.VMEM((tm, tn), F32)],
        compiler_params=_params("arbitrary", "arbitrary", "arbitrary"),
    )(a, b)


def _ln_bwd(dy, xh, rstd, g):
    dxh = dy * g
    dr = rstd * (dxh - _rowmean(dxh) - xh * _rowmean(dxh * xh))
    return dr, _colsum(dy * xh), _colsum(dy)


def _loss_ln_bwd(y, target, xh, rstd, g):
    T, D = y.shape
    tm = _tile(T, 256, 16)

    def body(y_ref, t_ref, xh_ref, rstd_ref, g_ref, dr_ref, drb_ref, dg_ref, db_ref, loss_ref):
        i = pl.program_id(0)

        @pl.when(i == 0)
        def _():
            dg_ref[...] = jnp.zeros_like(dg_ref)
            db_ref[...] = jnp.zeros_like(db_ref)
            loss_ref[...] = jnp.zeros_like(loss_ref)

        err = y_ref[...] - t_ref[...]
        loss_ref[...] += (0.5 / D) * _colsum(jnp.sum(err * err, axis=1, keepdims=True))
        dr, dg, db = _ln_bwd(err * (1.0 / D), xh_ref[...], rstd_ref[...], g_ref[...])
        dr_ref[...] = dr
        drb_ref[...] = dr.astype(BF16)
        dg_ref[...] += dg
        db_ref[...] += db

    row = pl.BlockSpec((tm, D), lambda i: (i, 0))
    vec = pl.BlockSpec((1, D), lambda i: (0, 0))
    return pl.pallas_call(
        body, name="loss_ln_bwd", grid=(T // tm,),
        in_specs=[row, row, row, pl.BlockSpec((tm, 1), lambda i: (i, 0)), vec],
        out_specs=[row, row, vec, vec, pl.BlockSpec((1, 1), lambda i: (0, 0))],
        out_shape=[jax.ShapeDtypeStruct((T, D), F32), jax.ShapeDtypeStruct((T, D), BF16),
                   jax.ShapeDtypeStruct((1, D), F32), jax.ShapeDtypeStruct((1, D), F32),
                   jax.ShapeDtypeStruct((1, 1), F32)],
        compiler_params=_params("arbitrary"),
    )(y, target, xh, rstd, g)


def _ffn_down_bwd(drb, wd, gpre, upre):
    T, D = drb.shape
    F = wd.shape[0]
    tm = _tile(T, 512, 16)
    tn = _tile(F, 512, LANES)

    def body(dr_ref, wd_ref, g_ref, u_ref, dg_ref, du_ref):
        dh = 0.5 * _dot_nt(dr_ref[...], wd_ref[...])
        g = g_ref[...]
        u = u_ref[...]
        sg = jax.nn.sigmoid(g)
        du_ref[...] = (dh * (g * sg)).astype(BF16)
        dg_ref[...] = (dh * u * (sg * (1.0 + g * (1.0 - sg)))).astype(BF16)

    t_spec = pl.BlockSpec((tm, tn), lambda j, i: (i, j))
    return pl.pallas_call(
        body, name="ffn_down_bwd", grid=(F // tn, T // tm),
        in_specs=[pl.BlockSpec((tm, D), lambda j, i: (i, 0)), pl.BlockSpec((tn, D), lambda j, i: (j, 0)),
                  t_spec, t_spec],
        out_specs=[t_spec, t_spec],
        out_shape=[jax.ShapeDtypeStruct((T, F), BF16), jax.ShapeDtypeStruct((T, F), BF16)],
        compiler_params=_params("arbitrary", "arbitrary"),
    )(drb, wd, gpre, upre)


def _mm_dx(pairs, res, alpha, ln=None, after=()):
    T, K = pairs[0][0].shape
    D = res.shape[1]
    n = len(pairs)
    n_after = len(after)
    tm = _tile(T, 256, 16)
    tk = _tile(K, 512, LANES)
    nk = K // tk
    with_ln = ln is not None

    def body(*refs):
        a_refs = refs[0:2 * n:2]
        w_refs = refs[1:2 * n:2]
        refs = refs[2 * n + n_after:]
        res_ref = refs[0]
        acc_ref = refs[-1]
        i = pl.program_id(0)
        k = pl.program_id(1)

        @pl.when(k == 0)
        def _():
            acc_ref[...] = jnp.zeros_like(acc_ref)

        for a_ref, w_ref in zip(a_refs, w_refs):
            acc_ref[...] += _dot_nn(a_ref[...], w_ref[...])

        if with_ln:
            xh_ref, rstd_ref, g_ref, dr_ref, drb_ref, dg_ref, db_ref = refs[1:8]

            @pl.when((i == 0) & (k == 0))
            def _():
                dg_ref[...] = jnp.zeros_like(dg_ref)
                db_ref[...] = jnp.zeros_like(db_ref)

            @pl.when(k == nk - 1)
            def _():
                dx = alpha * res_ref[...] + acc_ref[...]
                dr, dg, db = _ln_bwd(dx, xh_ref[...], rstd_ref[...], g_ref[...])
                dr_ref[...] = dr
                drb_ref[...] = dr.astype(BF16)
                dg_ref[...] += dg
                db_ref[...] += db
        else:
            dx_ref = refs[1]

            @pl.when(k == nk - 1)
            def _():
                dx_ref[...] = alpha * res_ref[...] + acc_ref[...]

    row = pl.BlockSpec((tm, D), lambda i, k: (i, 0))
    vec = pl.BlockSpec((1, D), lambda i, k: (0, 0))
    in_specs, operands = [], []
    for a, w in pairs:
        in_specs += [pl.BlockSpec((tm, tk), lambda i, k: (i, k)), pl.BlockSpec((tk, D), lambda i, k: (k, 0))]
        operands += [a, w]
    in_specs += [ANY_SPEC] * n_after
    operands += list(after)
    in_specs.append(row)
    operands.append(res)
    if with_ln:
        in_specs += [row, pl.BlockSpec((tm, 1), lambda i, k: (i, 0)), vec]
        operands += list(ln)
        out_specs = [row, row, vec, vec]
        out_shape = [jax.ShapeDtypeStruct((T, D), F32), jax.ShapeDtypeStruct((T, D), BF16),
                     jax.ShapeDtypeStruct((1, D), F32), jax.ShapeDtypeStruct((1, D), F32)]
    else:
        out_specs = [row]
        out_shape = [jax.ShapeDtypeStruct((T, D), F32)]
    return pl.pallas_call(
        body, name="mm_dx_ln_bwd" if with_ln else "mm_dx", grid=(T // tm, nk),
        in_specs=in_specs, out_specs=out_specs, out_shape=out_shape,
        scratch_shapes=[pltpu.VMEM((tm, D), F32)],
        compiler_params=_params("arbitrary", "arbitrary"),
    )(*operands)


def _s5_discretize(lre, lim, ldt, br, bi):
    dt = jnp.exp(ldt)
    mag = jnp.exp(lre * dt)
    ang = lim * dt
    ar = mag * jnp.cos(ang)
    ai = mag * jnp.sin(ang)
    den = lre * lre + lim * lim
    nr = ar - 1.0
    qr = (nr * lre + ai * lim) / den
    qi = (ai * lre - nr * lim) / den
    bbr = qr[None] * br - qi[None] * bi
    bbi = qr[None] * bi + qi[None] * br
    return ar, ai, bbr, bbi


def _s5_params_fwd(lre, lim, ldt, br, bi):
    def body(lre_ref, lim_ref, ldt_ref, br_ref, bi_ref, ar_ref, ai_ref, bbr_ref, bbi_ref):
        ar, ai, bbr, bbi = _s5_discretize(lre_ref[...], lim_ref[...], ldt_ref[...], br_ref[...], bi_ref[...])
        ar_ref[...] = ar
        ai_ref[...] = ai
        bbr_ref[...] = bbr
        bbi_ref[...] = bbi

    sds = jax.ShapeDtypeStruct
    return pl.pallas_call(
        body, name="s5_params_fwd",
        out_shape=[sds(lre.shape, F32), sds(lre.shape, F32), sds(br.shape, F32), sds(br.shape, F32)],
        compiler_params=pltpu.CompilerParams(vmem_limit_bytes=VMEM_LIMIT),
    )(lre, lim, ldt, br, bi)


def _s5_params_bwd(lre, lim, ldt, br, bi, dar, dai, dbbr, dbbi):
    def body(lre_ref, lim_ref, ldt_ref, br_ref, bi_ref, dar_ref, dai_ref, dbbr_ref, dbbi_ref,
             o_lre, o_lim, o_ldt, o_br, o_bi):
        _, vjp = jax.vjp(_s5_discretize, lre_ref[...], lim_ref[...], ldt_ref[...], br_ref[...], bi_ref[...])
        g = vjp((dar_ref[...], dai_ref[...], dbbr_ref[...], dbbi_ref[...]))
        o_lre[...] = g[0]
        o_lim[...] = g[1]
        o_ldt[...] = g[2]
        o_br[...] = g[3]
        o_bi[...] = g[4]

    sds = jax.ShapeDtypeStruct
    return pl.pallas_call(
        body, name="s5_params_bwd",
        out_shape=[sds(lre.shape, F32), sds(lre.shape, F32), sds(ldt.shape, F32), sds(br.shape, F32),
                   sds(br.shape, F32)],
        compiler_params=pltpu.CompilerParams(vmem_limit_bytes=VMEM_LIMIT),
    )(lre, lim, ldt, br, bi, dar, dai, dbbr, dbbi)


def _s5_fwd(proj, bdr, bdi, cdr, cdi, ar, ai, dvec):
    T = proj.shape[0]
    GB, UB, SB = bdr.shape
    tc = _tile(T, 256, SUBLANES)

    def body(u_ref, bdr_ref, bdi_ref, cdr_ref, cdi_ref, ar_ref, ai_ref, d_ref, y_ref, sr_ref, si_ref,
             cr_ref, ci_ref):
        @pl.when(pl.program_id(1) == 0)
        def _():
            cr_ref[...] = jnp.zeros_like(cr_ref)
            ci_ref[...] = jnp.zeros_like(ci_ref)

        u = u_ref[...]
        ub = u.astype(BF16)
        sr_ref[...] = _dot_nn(ub, bdr_ref[...])
        si_ref[...] = _dot_nn(ub, bdi_ref[...])
        a_re = ar_ref[...]
        a_im = ai_ref[...]

        def step(t, carry):
            p_re, p_im = carry
            row = pl.ds(t, 1)
            n_re = a_re * p_re - a_im * p_im + sr_ref[row, :]
            n_im = a_re * p_im + a_im * p_re + si_ref[row, :]
            sr_ref[row, :] = n_re
            si_ref[row, :] = n_im
            return n_re, n_im

        p_re, p_im = lax.fori_loop(0, tc, step, (cr_ref[...], ci_ref[...]), unroll=8)
        cr_ref[...] = p_re
        ci_ref[...] = p_im
        y_ref[...] = (_dot_nn(sr_ref[...].astype(BF16), cdr_ref[...])
                      - _dot_nn(si_ref[...].astype(BF16), cdi_ref[...]) + d_ref[...] * u)

    return pl.pallas_call(
        body, name="s5_fwd", grid=(GB, T // tc),
        in_specs=[pl.BlockSpec((tc, UB), lambda j, t: (t, j)),
                  pl.BlockSpec((None, UB, SB), lambda j, t: (j, 0, 0)),
                  pl.BlockSpec((None, UB, SB), lambda j, t: (j, 0, 0)),
                  pl.BlockSpec((None, SB, UB), lambda j, t: (j, 0, 0)),
                  pl.BlockSpec((None, SB, UB), lambda j, t: (j, 0, 0)),
                  pl.BlockSpec((1, SB), lambda j, t: (0, j)),
                  pl.BlockSpec((1, SB), lambda j, t: (0, j)),
                  pl.BlockSpec((1, UB), lambda j, t: (0, j))],
        out_specs=[pl.BlockSpec((tc, UB), lambda j, t: (t, j)),
                   pl.BlockSpec((tc, SB), lambda j, t: (t, j)),
                   pl.BlockSpec((tc, SB), lambda j, t: (t, j))],
        out_shape=[jax.ShapeDtypeStruct((T, GB * UB), F32), jax.ShapeDtypeStruct((T, GB * SB), F32),
                   jax.ShapeDtypeStruct((T, GB * SB), F32)],
        scratch_shapes=[pltpu.VMEM((1, SB), F32), pltpu.VMEM((1, SB), F32)],
        compiler_params=_params("arbitrary", "arbitrary"),
    )(proj, bdr, bdi, cdr, cdi, ar, ai, dvec)


def _s5_bwd(dy, proj, sr, si, bdr, bdi, cdr, cdi, ar, ai, dvec):
    T = dy.shape[0]
    GB, UB, SB = bdr.shape
    tc = _tile(T, 256, SUBLANES)
    nt = T // tc
    halo_blocks = tc // SUBLANES

    def body(dy_ref, u_ref, sr_ref, si_ref, hr_ref, hi_ref, bdr_ref, bdi_ref, cdr_ref, cdi_ref, ar_ref, ai_ref,
             d_ref, du_ref, dbdr_ref, dbdi_ref, dcdr_ref, dcdi_ref, dar_ref, dai_ref, dd_ref,
             gr_ref, gi_ref, pr_ref, pi_ref, cr_ref, ci_ref):
        step_no = pl.program_id(1)
        first_chunk = step_no == nt - 1

        @pl.when(step_no == 0)
        def _():
            for ref in (cr_ref, ci_ref, dbdr_ref, dbdi_ref, dcdr_ref, dcdi_ref, dar_ref, dai_ref, dd_ref):
                ref[...] = jnp.zeros_like(ref)

        dy = dy_ref[...]
        dyb = dy.astype(BF16)
        u = u_ref[...]
        s_re = sr_ref[...]
        s_im = si_ref[...]
        gr_ref[...] = _dot_nt(dyb, cdr_ref[...])
        gi_ref[...] = -_dot_nt(dyb, cdi_ref[...])
        dcdr_ref[...] += _dot_tn(s_re.astype(BF16), dyb)
        dcdi_ref[...] -= _dot_tn(s_im.astype(BF16), dyb)
        keep = jnp.where(first_chunk, 0.0, 1.0)
        pr_ref[0:SUBLANES, :] = hr_ref[...] * keep
        pi_ref[0:SUBLANES, :] = hi_ref[...] * keep
        pr_ref[SUBLANES:, :] = s_re
        pi_ref[SUBLANES:, :] = s_im
        a_re = ar_ref[...]
        a_im = ai_ref[...]

        def step(n, carry):
            c_re, c_im, acc_re, acc_im = carry
            t = tc - 1 - n
            row = pl.ds(t, 1)
            g_re = gr_ref[row, :] + a_re * c_re + a_im * c_im
            g_im = gi_ref[row, :] + a_re * c_im - a_im * c_re
            gr_ref[row, :] = g_re
            gi_ref[row, :] = g_im
            prev = pl.ds(t + SUBLANES - 1, 1)
            p_re = pr_ref[prev, :]
            p_im = pi_ref[prev, :]
            acc_re = acc_re + p_re * g_re + p_im * g_im
            acc_im = acc_im + p_re * g_im - p_im * g_re
            return g_re, g_im, acc_re, acc_im

        zero = jnp.zeros((1, SB), F32)
        c_re, c_im, acc_re, acc_im = lax.fori_loop(0, tc, step, (cr_ref[...], ci_ref[...], zero, zero), unroll=8)
        cr_ref[...] = c_re
        ci_ref[...] = c_im
        dar_ref[...] += acc_re
        dai_ref[...] += acc_im
        gsr = gr_ref[...].astype(BF16)
        gsi = gi_ref[...].astype(BF16)
        ub = u.astype(BF16)
        dbdr_ref[...] += _dot_tn(ub, gsr)
        dbdi_ref[...] += _dot_tn(ub, gsi)
        du_ref[...] = (_dot_nt(gsr, bdr_ref[...]) + _dot_nt(gsi, bdi_ref[...]) + d_ref[...] * dy).astype(BF16)
        dd_ref[...] += _colsum(dy * u)

    def rev(t):
        return nt - 1 - t

    def halo(j, t):
        return (jnp.maximum(rev(t) * halo_blocks - 1, 0), j)

    ublk = pl.BlockSpec((tc, UB), lambda j, t: (rev(t), j))
    sblk = pl.BlockSpec((tc, SB), lambda j, t: (rev(t), j))
    bd_spec = pl.BlockSpec((None, UB, SB), lambda j, t: (j, 0, 0))
    cd_spec = pl.BlockSpec((None, SB, UB), lambda j, t: (j, 0, 0))
    svec = pl.BlockSpec((1, SB), lambda j, t: (0, j))
    uvec = pl.BlockSpec((1, UB), lambda j, t: (0, j))
    sds = jax.ShapeDtypeStruct
    return pl.pallas_call(
        body, name="s5_bwd", grid=(GB, nt),
        in_specs=[ublk, ublk, sblk, sblk, pl.BlockSpec((SUBLANES, SB), halo), pl.BlockSpec((SUBLANES, SB), halo),
                  bd_spec, bd_spec, cd_spec, cd_spec, svec, svec, uvec],
        out_specs=[ublk, bd_spec, bd_spec, cd_spec, cd_spec, svec, svec, uvec],
        out_shape=[sds((T, GB * UB), BF16), sds((GB, UB, SB), F32), sds((GB, UB, SB), F32),
                   sds((GB, SB, UB), F32), sds((GB, SB, UB), F32), sds((1, GB * SB), F32),
                   sds((1, GB * SB), F32), sds((1, GB * UB), F32)],
        scratch_shapes=[pltpu.VMEM((tc, SB), F32), pltpu.VMEM((tc, SB), F32),
                        pltpu.VMEM((tc + SUBLANES, SB), F32), pltpu.VMEM((tc + SUBLANES, SB), F32),
                        pltpu.VMEM((1, SB), F32), pltpu.VMEM((1, SB), F32)],
        compiler_params=_params("arbitrary", "arbitrary"),
    )(dy, proj, sr, si, sr, si, bdr, bdi, cdr, cdi, ar, ai, dvec)


def _shift_down(v, k):
    rows = lax.broadcasted_iota(jnp.int32, v.shape, 0)
    return jnp.where(rows >= k, pltpu.roll(v, k, 0), 0.0)


def _shift_up(v, k):
    n = v.shape[0]
    rows = lax.broadcasted_iota(jnp.int32, v.shape, 0)
    return jnp.where(rows < n - k, pltpu.roll(v, n - k, 0), 0.0)


def _conv_specs(T, cb, n_s5_blocks, n_conv_blocks):
    gb = pl.BlockSpec((T, cb), lambda j: (0, n_s5_blocks + j))
    gc = pl.BlockSpec((T, cb), lambda j: (0, n_s5_blocks + n_conv_blocks + j))
    hh = pl.BlockSpec((T, cb), lambda j: (0, n_s5_blocks + 2 * n_conv_blocks + j))
    return gb, gc, hh


def _conv_fwd(proj, cw, cbias, d_s5, d_conv):
    T = proj.shape[0]
    cb = _tile(d_conv, 256, LANES)

    def body(gb_ref, gc_ref, hh_ref, w_ref, b_ref, z_ref):
        v = gc_ref[...] * hh_ref[...]
        w = w_ref[...]
        cv = b_ref[...] + w[0:1, :] * _shift_down(v, 2) + w[1:2, :] * _shift_down(v, 1) + w[2:3, :] * v
        z_ref[...] = gb_ref[...] * cv

    gb, gc, hh = _conv_specs(T, cb, d_s5 // cb, d_conv // cb)
    col = pl.BlockSpec((T, cb), lambda j: (0, j))
    return pl.pallas_call(
        body, name="conv_fwd", grid=(d_conv // cb,),
        in_specs=[gb, gc, hh, pl.BlockSpec((CONV_W, cb), lambda j: (0, j)), pl.BlockSpec((1, cb), lambda j: (0, j))],
        out_specs=col, out_shape=jax.ShapeDtypeStruct((T, d_conv), F32),
        compiler_params=_params("arbitrary"),
    )(proj, proj, proj, cw, cbias)


def _conv_bwd(dz, proj, cw, cbias, d_s5, d_conv):
    T = proj.shape[0]
    cb = _tile(d_conv, 256, LANES)

    def body(dz_ref, gb_ref, gc_ref, hh_ref, w_ref, b_ref, dgb_ref, dgc_ref, dhh_ref, dw_ref, db_ref):
        gc = gc_ref[...]
        hh = hh_ref[...]
        dz = dz_ref[...]
        w = w_ref[...]
        v = gc * hh
        v1 = _shift_down(v, 1)
        v2 = _shift_down(v, 2)
        cv = b_ref[...] + w[0:1, :] * v2 + w[1:2, :] * v1 + w[2:3, :] * v
        dgb_ref[...] = (dz * cv).astype(BF16)
        dcv = dz * gb_ref[...]
        dv = w[2:3, :] * dcv + w[1:2, :] * _shift_up(dcv, 1) + w[0:1, :] * _shift_up(dcv, 2)
        dgc_ref[...] = (dv * hh).astype(BF16)
        dhh_ref[...] = (dv * gc).astype(BF16)
        dw_ref[0:1, :] = _colsum(dcv * v2)
        dw_ref[1:2, :] = _colsum(dcv * v1)
        dw_ref[2:3, :] = _colsum(dcv * v)
        db_ref[...] = _colsum(dcv)

    gb, gc, hh = _conv_specs(T, cb, d_s5 // cb, d_conv // cb)
    col = pl.BlockSpec((T, cb), lambda j: (0, j))
    wspec = pl.BlockSpec((CONV_W, cb), lambda j: (0, j))
    bspec = pl.BlockSpec((1, cb), lambda j: (0, j))
    sds = jax.ShapeDtypeStruct
    return pl.pallas_call(
        body, name="conv_bwd", grid=(d_conv // cb,),
        in_specs=[col, gb, gc, hh, wspec, bspec],
        out_specs=[col, col, col, wspec, bspec],
        out_shape=[sds((T, d_conv), BF16), sds((T, d_conv), BF16), sds((T, d_conv), BF16),
                   sds((CONV_W, d_conv), F32), sds((1, d_conv), F32)],
        compiler_params=_params("arbitrary"),
    )(dz, proj, proj, proj, cw, cbias)


def _rms(v, g):
    rstd = lax.rsqrt(_rowmean(v * v) + RMS_EPS)
    return v * rstd * g, rstd


def _rms_bwd(dyn, v, rstd, g):
    w = dyn * g
    return rstd * w - v * (rstd * rstd * rstd) * _rowmean(w * v), _colsum(dyn * v * rstd)


def _mix_post(y, z, wglu, g_s5, g_conv):
    T, C = y.shape
    tm = _tile(T, 256, 16)

    def body(y_ref, z_ref, w_ref, gs_ref, gc_ref, m_ref, gl_ref):
        ge = jax.nn.gelu(y_ref[...])
        gl = _dot_nn(ge.astype(BF16), w_ref[...])
        gl_ref[...] = gl
        yn, _ = _rms(ge * jax.nn.sigmoid(gl), gs_ref[...])
        zn, _ = _rms(z_ref[...], gc_ref[...])
        m_ref[:, 0:C] = yn.astype(BF16)
        m_ref[:, C:2 * C] = zn.astype(BF16)

    row = pl.BlockSpec((tm, C), lambda i: (i, 0))
    vec = pl.BlockSpec((1, C), lambda i: (0, 0))
    return pl.pallas_call(
        body, name="mix_post", grid=(T // tm,),
        in_specs=[row, row, pl.BlockSpec((C, C), lambda i: (0, 0)), vec, vec],
        out_specs=[pl.BlockSpec((tm, 2 * C), lambda i: (i, 0)), row],
        out_shape=[jax.ShapeDtypeStruct((T, 2 * C), BF16), jax.ShapeDtypeStruct((T, C), F32)],
        compiler_params=_params("arbitrary"),
    )(y, z, wglu, g_s5, g_conv)


def _mix_post_bwd(dm, y, gl, z, wglu, g_s5, g_conv):
    T, C = y.shape
    tm = _tile(T, 256, 16)

    def body(dm_ref, y_ref, gl_ref, z_ref, w_ref, gs_ref, gc_ref, dy_ref, dz_ref, dw_ref, dgs_ref, dgc_ref):
        @pl.when(pl.program_id(0) == 0)
        def _():
            dw_ref[...] = jnp.zeros_like(dw_ref)
            dgs_ref[...] = jnp.zeros_like(dgs_ref)
            dgc_ref[...] = jnp.zeros_like(dgc_ref)

        yv = y_ref[...]
        ge, gelu_vjp = jax.vjp(jax.nn.gelu, yv)
        gl = gl_ref[...]
        sg = jax.nn.sigmoid(gl)
        y2 = ge * sg
        _, rstd_y = _rms(y2, gs_ref[...])
        dy2, dgs = _rms_bwd(dm_ref[:, 0:C], y2, rstd_y, gs_ref[...])
        dgs_ref[...] += dgs
        dgl = (dy2 * ge * sg * (1.0 - sg)).astype(BF16)
        dge = dy2 * sg + _dot_nt(dgl, w_ref[...])
        dw_ref[...] += _dot_tn(ge.astype(BF16), dgl)
        dy_ref[...] = gelu_vjp(dge)[0]
        zv = z_ref[...]
        _, rstd_z = _rms(zv, gc_ref[...])
        dz, dgc = _rms_bwd(dm_ref[:, C:2 * C], zv, rstd_z, gc_ref[...])
        dz_ref[...] = dz
        dgc_ref[...] += dgc

    row = pl.BlockSpec((tm, C), lambda i: (i, 0))
    vec = pl.BlockSpec((1, C), lambda i: (0, 0))
    full = pl.BlockSpec((C, C), lambda i: (0, 0))
    sds = jax.ShapeDtypeStruct
    return pl.pallas_call(
        body, name="mix_post_bwd", grid=(T // tm,),
        in_specs=[pl.BlockSpec((tm, 2 * C), lambda i: (i, 0)), row, row, row, full, vec, vec],
        out_specs=[row, row, full, vec, vec],
        out_shape=[sds((T, C), F32), sds((T, C), F32), sds((C, C), F32), sds((1, C), F32), sds((1, C), F32)],
        compiler_params=_params("arbitrary"),
    )(dm, y, gl, z, wglu, g_s5, g_conv)


def _adamw(w, g, m, v):
    m = ADAM_B1 * m + (1.0 - ADAM_B1) * g
    v = ADAM_B2 * v + (1.0 - ADAM_B2) * (g * g)
    m_hat = m / (1.0 - ADAM_B1 ** ADAM_STEP)
    v_hat = v / (1.0 - ADAM_B2 ** ADAM_STEP)
    return -ADAM_LR * (m_hat / (jnp.sqrt(v_hat) + ADAM_EPS) + ADAM_WD * w), m, v


def _sum_parts(p_ref):
    total = p_ref[0].astype(F32)
    for d in range(1, N_DEV):
        total = total + p_ref[d].astype(F32)
    return total


def _row_tile(R, C, n_streams):
    budget = VMEM_LIMIT // 3 // (n_streams * C * 4)
    return _tile(R, max(BF16_ROWS, budget), BF16_ROWS)


def _reduce_parts(parts):
    L, _, R, C = parts.shape
    tr = _row_tile(R, C, N_DEV + 1)

    def body(p_ref, o_ref):
        o_ref[...] = _sum_parts(p_ref)

    return pl.pallas_call(
        body, name="reduce_parts", grid=(L, R // tr),
        in_specs=[pl.BlockSpec((None, N_DEV, tr, C), lambda l, i: (l, 0, i, 0))],
        out_specs=pl.BlockSpec((None, tr, C), lambda l, i: (l, i, 0)),
        out_shape=jax.ShapeDtypeStruct((L, R, C), F32),
        compiler_params=_params("arbitrary", "arbitrary"),
    )(parts)


def _adamw_update(w, m, v, grad=None, parts=None):
    L, R, C = w.shape
    from_parts = parts is not None
    tr = _row_tile(R, C, (N_DEV if from_parts else 1) + 7)

    def body(g_in_ref, w_ref, m_ref, v_ref, g_ref, d_ref, nm_ref, nv_ref):
        g = _sum_parts(g_in_ref) if from_parts else g_in_ref[...]
        delta, nm, nv = _adamw(w_ref[...], g, m_ref[...], v_ref[...])
        g_ref[...] = g
        d_ref[...] = delta
        nm_ref[...] = nm
        nv_ref[...] = nv

    blk = pl.BlockSpec((None, tr, C), lambda l, i: (l, i, 0))
    g_spec = pl.BlockSpec((None, N_DEV, tr, C), lambda l, i: (l, 0, i, 0)) if from_parts else blk
    out = jax.ShapeDtypeStruct((L, R, C), F32)
    return pl.pallas_call(
        body, name="adamw_parts" if from_parts else "adamw", grid=(L, R // tr),
        in_specs=[g_spec, blk, blk, blk], out_specs=[blk, blk, blk, blk], out_shape=[out, out, out, out],
        compiler_params=_params("arbitrary", "arbitrary"),
    )(parts if from_parts else grad, w, m, v)


def _me():
    x, y, c = (lax.axis_index(a) for a in AXES)
    return x, y, c, 4 * x + 2 * y + c


def _peer(rel):
    x, y, c, _ = _me()
    px = 1 - x if rel & 4 else x
    py = 1 - y if rel & 2 else y
    pc = 1 - c if rel & 1 else c
    return (px, py, pc), 4 * px + 2 * py + pc


DATAFLOW = pltpu.SideEffectType.DATAFLOW_SIDE_EFFECTING


def _remote_copy(src, dst, sems):
    return functools.partial(pltpu.make_async_remote_copy, src_ref=src, dst_ref=dst, **sems)


def _gather_copies(n):
    def copies(srcs, lands, send_sems, recv_sems, local_sems):
        me = _me()[3]
        local, remote = [], []
        for k in range(n):
            local.append(functools.partial(pltpu.make_async_copy, srcs[k], lands[k].at[me], local_sems.at[k]))
            for rel in range(1, N_DEV):
                dev, blk = _peer(rel)
                sems = dict(send_sem=send_sems.at[_sem_index(k, rel)], recv_sem=recv_sems.at[_sem_index(k, rel)],
                            device_id=dev, device_id_type=MESH)
                remote.append((_remote_copy(srcs[k], lands[k].at[me], sems), _remote_copy(srcs[k], lands[k].at[blk], sems)))
        return local, remote

    return copies


def _scatter_copies(n, layer):
    def copies(srcs, lands, send_sems, recv_sems, local_sems):
        me = _me()[3]
        local, remote = [], []
        for k in range(n):
            local.append(functools.partial(pltpu.make_async_copy, srcs[k].at[me], lands[k].at[layer, me],
                                           local_sems.at[k]))
            for rel in range(1, N_DEV):
                dev, blk = _peer(rel)
                sems = dict(send_sem=send_sems.at[_sem_index(k, rel)], recv_sem=recv_sems.at[_sem_index(k, rel)],
                            device_id=dev, device_id_type=MESH)
                remote.append((_remote_copy(srcs[k].at[blk], lands[k].at[layer, me], sems),
                               _remote_copy(srcs[k].at[blk], lands[k].at[layer, blk], sems)))
        return local, remote

    return copies


def _sem_shapes(n):
    return [pltpu.SemaphoreType.DMA((n * (N_DEV - 1),)), pltpu.SemaphoreType.DMA((n * (N_DEV - 1),)),
            pltpu.SemaphoreType.DMA((n,))]


def _sem_index(k, rel):
    return k * (N_DEV - 1) + rel - 1


def _exchange(copies, name, srcs, lands):
    n_src, n_land = len(srcs), len(lands)

    def body(*refs):
        src_refs = refs[:n_src]
        land_refs = refs[n_src + n_land:n_src + 2 * n_land]
        local, remote = copies(src_refs, land_refs, *refs[n_src + 2 * n_land:])
        local = [cp() for cp in local]
        sends = [send() for send, _ in remote]
        for cp in local + sends:
            cp.start()
        for send, (_, landing) in zip(sends, remote):
            send.wait_send()
            landing().wait_recv()
        for cp in local:
            cp.wait()

    return pl.pallas_call(
        body, name=name, in_specs=[HBM_SPEC] * (n_src + n_land), out_specs=[HBM_SPEC] * n_land,
        out_shape=[jax.ShapeDtypeStruct(b.shape, b.dtype) for b in lands],
        scratch_shapes=_sem_shapes(n_src),
        input_output_aliases={n_src + k: k for k in range(n_land)},
        compiler_params=pltpu.CompilerParams(has_side_effects=True),
    )(*srcs, *lands)


def _hbm(arrays):
    return [pltpu.with_memory_space_constraint(a, pltpu.HBM) for a in arrays]


def _exchange_start(copies, name, srcs, lands, after):
    n_src, n_land, n_after = len(srcs), len(lands), len(after)
    n_data = n_src + n_land

    def body(*refs):
        outs = refs[n_data + n_after:]
        local, remote = copies(refs[:n_src], refs[n_src:n_data], *outs[:3])
        for cp in local:
            cp().start()
        for send, _ in remote:
            send().start()
        outs[-1][...] = jnp.zeros_like(outs[-1])

    res = pl.pallas_call(
        body, name=name, in_specs=[HBM_SPEC] * n_data + [ANY_SPEC] * n_after,
        out_specs=[SEM_SPEC] * 3 + [HBM_SPEC] * n_data + [pl.BlockSpec(memory_space=pltpu.VMEM)],
        out_shape=_sem_shapes(n_src) + [pltpu.HBM(a.shape, a.dtype) for a in list(srcs) + list(lands)]
        + [jax.ShapeDtypeStruct((SUBLANES, LANES), F32)],
        input_output_aliases={k: 3 + k for k in range(n_data)},
        compiler_params=pltpu.CompilerParams(has_side_effects=DATAFLOW),
    )(*_hbm(list(srcs) + list(lands)), *after)
    return res[:3], res[3:3 + n_src], res[3 + n_src:3 + n_data], res[-1]


def _exchange_wait(copies, name, sems, srcs, lands, after):
    n_src, n_land, n_after = len(srcs), len(lands), len(after)
    n_data = n_src + n_land

    def body(*refs):
        local, remote = copies(refs[:n_src], refs[n_src:n_data], *refs[n_data:n_data + 3])
        for send, landing in remote:
            send().wait_send()
            landing().wait_recv()
        for cp in local:
            cp().wait()

    res = pl.pallas_call(
        body, name=name, in_specs=[HBM_SPEC] * n_data + [SEM_SPEC] * 3 + [ANY_SPEC] * n_after,
        out_specs=[HBM_SPEC] * n_data,
        out_shape=[pltpu.HBM(a.shape, a.dtype) for a in list(srcs) + list(lands)],
        input_output_aliases={k: k for k in range(n_data)},
        compiler_params=pltpu.CompilerParams(has_side_effects=DATAFLOW),
    )(*srcs, *lands, *sems, *after)
    return res[n_src:]


def _block_diag(blocks, row_major):
    L, GB, g, P, N = blocks.shape
    eye = jnp.eye(g, dtype=blocks.dtype)
    if row_major:
        return jnp.einsum("lbgpn,gh->lbgphn", blocks, eye).reshape(L, GB, g * P, g * N)
    return jnp.einsum("lbgpn,gh->lbhngp", blocks, eye).reshape(L, GB, g * N, g * P)


def _diag_blocks(mat, g, P, N, row_major):
    GB = mat.shape[0]
    eye = jnp.eye(g, dtype=mat.dtype)
    if row_major:
        return jnp.einsum("bgphn,gh->bgpn", mat.reshape(GB, g, P, g, N), eye)
    return jnp.einsum("bhngp,gh->bgpn", mat.reshape(GB, g, N, g, P), eye)


def _pack(arrays, rows_multiple):
    flat = jnp.concatenate([a.reshape(-1).astype(F32) for a in arrays])
    pad = (-flat.shape[0]) % (rows_multiple * LANES)
    return jnp.pad(flat, (0, pad)).reshape(-1, LANES)


def _unpack(packed, shapes):
    flat = packed.reshape(-1)
    out, pos = [], 0
    for s in shapes:
        n = math.prod(s)
        out.append(flat[pos:pos + n].reshape(s))
        pos += n
    return out


SMALL = ["ln1_g", "ln1_b", "s5_lam_re", "s5_lam_im", "s5_log_dt", "s5_b_re", "s5_b_im", "s5_c_re", "s5_c_im", "s5_d",
         "conv_b", "g_s5", "g_conv", "ln2_g", "ln2_b", "ln3_g", "ln3_b"]
WEIGHTS = ["ffn1_gate", "ffn1_up", "ffn1_down", "ln1_g", "ln1_b", "w_in", "s5_lam_re", "s5_lam_im", "s5_log_dt",
           "s5_b_re", "s5_b_im", "s5_c_re", "s5_c_im", "s5_d", "s5_w_glu", "conv_w", "conv_b", "g_s5", "g_conv",
           "w_out", "ln2_g", "ln2_b", "ffn2_gate", "ffn2_up", "ffn2_down", "ln3_g", "ln3_b"]
TRANSPOSED = ["ffn1_gate", "ffn1_up", "w_in", "ffn2_gate", "ffn2_up"]
GROUPS = {"a": ["ffn1_gate", "ffn1_up", "ffn1_down"], "b": ["w_in", "s5_w_glu", "w_out"],
          "c": ["ffn2_gate", "ffn2_up", "ffn2_down"]}


def _train_step(x, target, w, m, v):
    T, D = x.shape
    L = w["ln1_g"].shape[0]
    alpha = (2.0 * L) ** 0.25
    G = w["s5_log_dt"].shape[1]
    d_s5 = G * S5_P
    d_conv = w["conv_b"].shape[1]
    GB = G // S5_GROUPS_PER_BLOCK
    me = _me()[3]

    def shard(n, l):
        return (jnp.swapaxes(w[n][l], 0, 1) if n in TRANSPOSED else w[n][l]).astype(BF16)

    conv_w_rows = jnp.pad(w["conv_w"], ((0, 0), (0, SUBLANES - CONV_W), (0, 0)))
    gathers, token = {}, []
    for l in range(L):
        for grp, names in GROUPS.items():
            srcs = [shard(n, l) for n in names] + ([conv_w_rows[l]] if grp == "b" else [])
            lands = [lax.empty((N_DEV,) + a.shape, a.dtype) for a in srcs]
            sems, srcs, lands, tok = _exchange_start(_gather_copies(len(srcs)), f"gather_start_{l}{grp}", srcs, lands,
                                                     token)
            gathers[l, grp] = (sems, srcs, lands)
            token = [tok]

    def gathered(l, grp, after):
        sems, srcs, lands = gathers[l, grp]
        lands = _exchange_wait(_gather_copies(len(srcs)), f"gather_wait_{l}{grp}", sems, srcs, lands, after)
        full = {n: p.reshape(-1, p.shape[-1]) for n, p in zip(GROUPS[grp], lands)}
        if grp == "b":
            full["conv_w"] = jnp.swapaxes(lands[-1][:, :CONV_W, :], 0, 1).reshape(CONV_W, d_conv)
        return full

    lre = w["s5_lam_re"].reshape(L * G, S5_N)
    lim = w["s5_lam_im"].reshape(L * G, S5_N)
    ldt = w["s5_log_dt"].reshape(L * G, 1)
    b_re = jnp.transpose(w["s5_b_re"], (3, 0, 1, 2)).reshape(S5_P, L * G, S5_N)
    b_im = jnp.transpose(w["s5_b_im"], (3, 0, 1, 2)).reshape(S5_P, L * G, S5_N)
    ab_re, ab_im, bb_re, bb_im = _s5_params_fwd(lre, lim, ldt, b_re, b_im)

    def groups(bb):
        return jnp.transpose(bb.reshape(S5_P, L, GB, S5_GROUPS_PER_BLOCK, S5_N), (1, 2, 3, 0, 4))

    bd_re = _block_diag(groups(bb_re), True).astype(BF16)
    bd_im = _block_diag(groups(bb_im), True).astype(BF16)
    c_shape = (L, GB, S5_GROUPS_PER_BLOCK, S5_P, S5_N)
    cd_re = _block_diag(w["s5_c_re"].reshape(c_shape), False).astype(BF16)
    cd_im = _block_diag(w["s5_c_im"].reshape(c_shape), False).astype(BF16)
    a_re = ab_re.reshape(L, 1, G * S5_N)
    a_im = ab_im.reshape(L, 1, G * S5_N)
    d_vec = w["s5_d"].reshape(L, 1, d_s5)

    def vec(name, l):
        return w[name][l].reshape(1, -1)

    saved, weights = [], []
    x_in, x_in_b = x, x.astype(BF16)
    for l in range(L):
        gw = gathered(l, "a", [x_in] if l else token)
        s = {"x0b": x_in_b}
        s["g1"], s["u1"], s["h1"] = _ffn_up(x_in_b, gw["ffn1_gate"], gw["ffn1_up"])
        x1, s["x1b"], s["xh1"], s["rstd1"] = _mm_res_ln(s["h1"], gw["ffn1_down"], x_in, vec("ln1_g", l),
                                                         vec("ln1_b", l), 0.5, alpha)
        gw.update(gathered(l, "b", [s["x1b"]]))
        s["proj"] = _mm_nt(s["x1b"], gw["w_in"])
        s["y"], s["sr"], s["si"] = _s5_fwd(s["proj"], bd_re[l], bd_im[l], cd_re[l], cd_im[l], a_re[l], a_im[l],
                                           d_vec[l])
        s["z"] = _conv_fwd(s["proj"], gw["conv_w"], vec("conv_b", l), d_s5, d_conv)
        s["mcat"], s["gl"] = _mix_post(s["y"], s["z"], gw["s5_w_glu"], vec("g_s5", l), vec("g_conv", l))
        x2, s["x2b"], s["xh2"], s["rstd2"] = _mm_res_ln(s["mcat"], gw["w_out"], x1, vec("ln2_g", l),
                                                         vec("ln2_b", l), 1.0, alpha)
        gw.update(gathered(l, "c", [s["x2b"]]))
        s["g2"], s["u2"], s["h2"] = _ffn_up(s["x2b"], gw["ffn2_gate"], gw["ffn2_up"])
        x3, x3b, s["xh3"], s["rstd3"] = _mm_res_ln(s["h2"], gw["ffn2_down"], x2, vec("ln3_g", l), vec("ln3_b", l),
                                                   0.5, alpha)
        saved.append(s)
        weights.append(gw)
        x_in, x_in_b = x3, x3b

    last = saved[L - 1]
    dr, drb, dg, db, loss = _loss_ln_bwd(x_in, target, last["xh3"], last["rstd3"], vec("ln3_g", L - 1))
    small = [dict() for _ in range(L)]
    small[L - 1]["ln3_g"], small[L - 1]["ln3_b"] = dg, db
    bufs = {grp: [lax.empty((L, N_DEV) + shard(n, 0).shape, BF16) for n in names] for grp, names in GROUPS.items()}
    scatters = {grp: [] for grp in GROUPS}
    grad_x = None
    for l in reversed(range(L)):
        gw, s, sm = weights[l], saved[l], small[l]
        full = {}

        def scatter_start(grp):
            fulls = [full[n].reshape((N_DEV, -1) + full[n].shape[1:]) for n in GROUPS[grp]]
            sems, fulls, bufs[grp], tok = _exchange_start(_scatter_copies(len(fulls), l), f"scatter_start_{l}{grp}",
                                                          fulls, bufs[grp], [])
            scatters[grp].append((l, sems, fulls))
            return [tok]

        def ffn_bwd(dr, drb, tag, grp, xb_in, ln):
            dgp, dup = _ffn_down_bwd(drb, gw[f"ffn{tag}_down"], s[f"g{tag}"], s[f"u{tag}"])
            full[f"ffn{tag}_down"] = _mm_tn(s[f"h{tag}"], drb, 0.5, BF16)
            full[f"ffn{tag}_gate"] = _mm_tn(dgp, xb_in, 1.0, BF16)
            full[f"ffn{tag}_up"] = _mm_tn(dup, xb_in, 1.0, BF16)
            return _mm_dx([(dgp, gw[f"ffn{tag}_gate"]), (dup, gw[f"ffn{tag}_up"])], dr, alpha, ln, scatter_start(grp))

        dr, drb, sm["ln2_g"], sm["ln2_b"] = ffn_bwd(dr, drb, 2, "c", s["x2b"], (s["xh2"], s["rstd2"], vec("ln2_g", l)))
        dm = _mm_nt(drb, gw["w_out"])
        full["w_out"] = _mm_tn(s["mcat"], drb, 1.0, BF16)
        dy, dz, dwglu, sm["g_s5"], sm["g_conv"] = _mix_post_bwd(dm, s["y"], s["gl"], s["z"], gw["s5_w_glu"],
                                                                vec("g_s5", l), vec("g_conv", l))
        full["s5_w_glu"] = dwglu.astype(BF16)
        du, dbd_re, dbd_im, dcd_re, dcd_im, sm["d_ab_re"], sm["d_ab_im"], sm["s5_d"] = _s5_bwd(
            dy, s["proj"], s["sr"], s["si"], bd_re[l], bd_im[l], cd_re[l], cd_im[l], a_re[l], a_im[l], d_vec[l])
        gsz = (S5_GROUPS_PER_BLOCK, S5_P, S5_N)
        sm["d_bb_re"] = _diag_blocks(dbd_re, *gsz, True)
        sm["d_bb_im"] = _diag_blocks(dbd_im, *gsz, True)
        sm["s5_c_re"] = _diag_blocks(dcd_re, *gsz, False).reshape(G, S5_P, S5_N)
        sm["s5_c_im"] = _diag_blocks(dcd_im, *gsz, False).reshape(G, S5_P, S5_N)
        dgb, dgc, dhh, sm["conv_w"], sm["conv_b"] = _conv_bwd(dz, s["proj"], gw["conv_w"], vec("conv_b", l),
                                                              d_s5, d_conv)
        dproj = jnp.concatenate([du, dgb, dgc, dhh], axis=1)
        full["w_in"] = _mm_tn(dproj, s["x1b"], 1.0, BF16)
        dr, drb, sm["ln1_g"], sm["ln1_b"] = _mm_dx([(dproj, gw["w_in"])], dr, alpha,
                                                   (s["xh1"], s["rstd1"], vec("ln1_g", l)), scatter_start("b"))
        if l > 0:
            prev = saved[l - 1]
            dr, drb, small[l - 1]["ln3_g"], small[l - 1]["ln3_b"] = ffn_bwd(
                dr, drb, 1, "a", s["x0b"], (prev["xh3"], prev["rstd3"], vec("ln3_g", l - 1)))
        else:
            (grad_x,) = ffn_bwd(dr, drb, 1, "a", s["x0b"], None)

    def stack(key):
        return jnp.stack([small[l][key] for l in range(L)])

    d_bb_re = jnp.transpose(stack("d_bb_re").reshape(L * G, S5_P, S5_N), (1, 0, 2))
    d_bb_im = jnp.transpose(stack("d_bb_im").reshape(L * G, S5_P, S5_N), (1, 0, 2))
    g_lre, g_lim, g_ldt, g_bre, g_bim = _s5_params_bwd(
        lre, lim, ldt, b_re, b_im, stack("d_ab_re").reshape(L * G, S5_N), stack("d_ab_im").reshape(L * G, S5_N),
        d_bb_re, d_bb_im)
    part = {n: stack(n) for n in ["ln1_g", "ln1_b", "s5_c_re", "s5_c_im", "s5_d", "conv_b", "g_s5", "g_conv", "ln2_g",
                                  "ln2_b", "ln3_g", "ln3_b", "conv_w"]}
    part["s5_lam_re"], part["s5_lam_im"], part["s5_log_dt"] = g_lre, g_lim, g_ldt
    part["s5_b_re"] = jnp.transpose(g_bre, (1, 2, 0))
    part["s5_b_im"] = jnp.transpose(g_bim, (1, 2, 0))

    small_names = SMALL + ["conv_w"]
    small_shapes = [w[n].shape for n in SMALL] + [(L, CONV_W, d_conv)]
    packed = _pack([part[n] for n in small_names], N_DEV * PACK_ROWS)
    rows = packed.shape[0] // N_DEV
    (landed,) = _exchange(_scatter_copies(1, 0), "scatter_small", [packed.reshape(N_DEV, rows, LANES)],
                          [lax.empty((1, N_DEV, rows, LANES), F32)])
    mine = _reduce_parts(landed)
    (summed,) = _exchange(_gather_copies(1), "gather_small", [mine[0]], [lax.empty((N_DEV, rows, LANES), F32)])
    small_grads = dict(zip(small_names, _unpack(summed, small_shapes)))

    out = {}

    def update(name, w3, m3, v3, shape, **grad):
        res = _adamw_update(w3, m3, v3, **grad)
        out[name] = [r.reshape(shape) for r in res]

    for grp in ("c", "b", "a"):
        for l, sems, fulls in scatters[grp]:
            bufs[grp] = _exchange_wait(_scatter_copies(len(fulls), l), f"scatter_wait_{l}{grp}", sems, fulls, bufs[grp],
                                       [grad_x])
        for n, parts in zip(GROUPS[grp], bufs[grp]):
            if n in TRANSPOSED:
                update(n, w[n], m[n], v[n], w[n].shape, grad=jnp.swapaxes(_reduce_parts(parts), 1, 2))
            else:
                update(n, w[n], m[n], v[n], w[n].shape, parts=parts)
    cw_shape = w["conv_w"].shape
    g_cw = lax.dynamic_slice_in_dim(small_grads["conv_w"], me * cw_shape[2], cw_shape[2], axis=2)
    update("conv_w", w["conv_w"], m["conv_w"], v["conv_w"], cw_shape, grad=g_cw)
    sizes = [w[n].shape for n in SMALL]
    pw, pm, pv, pg = (_pack([d[n] for n in SMALL], PACK_ROWS)[None] for d in (w, m, v, small_grads))
    for name, res in zip(SMALL, zip(*[_unpack(r, sizes) for r in _adamw_update(pw, pm, pv, grad=pg)])):
        out[name] = list(res)

    loss = lax.psum(loss[0, 0], AXES)
    return loss, grad_x, out


def kernel(x, ffn1_gate, ffn1_up, ffn1_down, ln1_g, ln1_b, w_in, s5_lam_re, s5_lam_im, s5_log_dt, s5_b_re, s5_b_im, s5_c_re, s5_c_im, s5_d, s5_w_glu, conv_w, conv_b, g_s5, g_conv, w_out, ln2_g, ln2_b, ffn2_gate, ffn2_up, ffn2_down, ln3_g, ln3_b, loss_target, m_ffn1_gate, m_ffn1_up, m_ffn1_down, m_ln1_g, m_ln1_b, m_w_in, m_s5_lam_re, m_s5_lam_im, m_s5_log_dt, m_s5_b_re, m_s5_b_im, m_s5_c_re, m_s5_c_im, m_s5_d, m_s5_w_glu, m_conv_w, m_conv_b, m_g_s5, m_g_conv, m_w_out, m_ln2_g, m_ln2_b, m_ffn2_gate, m_ffn2_up, m_ffn2_down, m_ln3_g, m_ln3_b, v_ffn1_gate, v_ffn1_up, v_ffn1_down, v_ln1_g, v_ln1_b, v_w_in, v_s5_lam_re, v_s5_lam_im, v_s5_log_dt, v_s5_b_re, v_s5_b_im, v_s5_c_re, v_s5_c_im, v_s5_d, v_s5_w_glu, v_conv_w, v_conv_b, v_g_s5, v_g_conv, v_w_out, v_ln2_g, v_ln2_b, v_ffn2_gate, v_ffn2_up, v_ffn2_down, v_ln3_g, v_ln3_b):
    given = dict(locals())
    w = {n: given[n] for n in WEIGHTS}
    m = {n: given["m_" + n] for n in WEIGHTS}
    v = {n: given["v_" + n] for n in WEIGHTS}
    T, D = x.shape[-2:]
    loss, grad_x, out = _train_step(x.reshape(T, D), loss_target.reshape(T, D), w, m, v)
    results = [loss, grad_x.reshape(x.shape)]
    for i in range(4):
        results += [out[n][i] for n in WEIGHTS]
    return tuple(results)
```

```python
import functools
import math

import jax
import jax.numpy as jnp
from jax import lax
from jax.experimental import pallas as pl
from jax.experimental.pallas import tpu as pltpu

F32 = jnp.float32
BF16 = jnp.bfloat16
MESH = pl.DeviceIdType.MESH
AXES = ("x", "y", "c")
N_DEV = 8

S5_P = 16
S5_N = 64
CONV_W = 3
LN_EPS = 1e-5
RMS_EPS = 1e-6
ADAM_LR = 0.001
ADAM_B1 = 0.9
ADAM_B2 = 0.999
ADAM_EPS = 1e-08
ADAM_WD = 0.01
ADAM_STEP = 10

V7X_VMEM_BYTES = 64 * 1024 * 1024
VMEM_LIMIT = V7X_VMEM_BYTES * 7 // 8
LANES = 128
SUBLANES = 8
BF16_ROWS = 16
PACK_ROWS = 512
S5_GROUPS_PER_BLOCK = LANES // S5_P
S5_STATE_BLOCK = S5_GROUPS_PER_BLOCK * S5_N

HBM_SPEC = pl.BlockSpec(memory_space=pltpu.HBM)
SEM_SPEC = pl.BlockSpec(memory_space=pltpu.SEMAPHORE)
ANY_SPEC = pl.BlockSpec(memory_space=pl.ANY)


def _tile(n, pref, align):
    best = None
    d = align
    while d <= min(n, pref):
        if n % d == 0:
            best = d
        d += align
    return best if best is not None else n


def _k_rows(vmem_bytes, cols):
    return vmem_bytes // (2 * 2 * cols)


def _params(*sem):
    return pltpu.CompilerParams(dimension_semantics=sem, vmem_limit_bytes=VMEM_LIMIT)


def _dot_nn(a, b):
    return lax.dot_general(a, b, (((1,), (0,)), ((), ())), preferred_element_type=F32)


def _dot_nt(a, b):
    return lax.dot_general(a, b, (((1,), (1,)), ((), ())), preferred_element_type=F32)


def _dot_tn(a, b):
    return lax.dot_general(a, b, (((0,), (0,)), ((), ())), preferred_element_type=F32)


def _colsum(v):
    return jnp.sum(v, axis=0, keepdims=True)


def _rowmean(v):
    return jnp.mean(v, axis=-1, keepdims=True)


def _ffn_up(xb, wg, wu):
    T, D = xb.shape
    F = wg.shape[0]
    tm = _tile(T, 512, 16)
    tn = _tile(F, 512, LANES)

    def body(x_ref, wg_ref, wu_ref, g_ref, u_ref, h_ref):
        x = x_ref[...]
        g = _dot_nt(x, wg_ref[...])
        u = _dot_nt(x, wu_ref[...])
        g_ref[...] = g
        u_ref[...] = u
        h_ref[...] = (g * jax.nn.sigmoid(g) * u).astype(BF16)

    w_spec = pl.BlockSpec((tn, D), lambda j, i: (j, 0))
    o_spec = pl.BlockSpec((tm, tn), lambda j, i: (i, j))
    return pl.pallas_call(
        body, name="ffn_up", grid=(F // tn, T // tm),
        in_specs=[pl.BlockSpec((tm, D), lambda j, i: (i, 0)), w_spec, w_spec],
        out_specs=[o_spec, o_spec, o_spec],
        out_shape=[jax.ShapeDtypeStruct((T, F), F32), jax.ShapeDtypeStruct((T, F), F32),
                   jax.ShapeDtypeStruct((T, F), BF16)],
        compiler_params=_params("arbitrary", "arbitrary"),
    )(xb, wg, wu)


def _mm_res_ln(a, w, res, g, b, scale, alpha):
    T, K = a.shape
    D = w.shape[1]
    tm = _tile(T, 512, 16)
    tk = _tile(K, _k_rows(V7X_VMEM_BYTES * 3 // 16, D), LANES)
    nk = K // tk

    def body(a_ref, w_ref, res_ref, g_ref, b_ref, xo_ref, xb_ref, xh_ref, rstd_ref, acc_ref):
        k = pl.program_id(1)

        @pl.when(k == 0)
        def _():
            acc_ref[...] = jnp.zeros_like(acc_ref)

        acc_ref[...] += _dot_nn(a_ref[...], w_ref[...])

        @pl.when(k == nk - 1)
        def _():
            r = alpha * res_ref[...] + scale * acc_ref[...]
            xc = r - _rowmean(r)
            rstd = lax.rsqrt(_rowmean(xc * xc) + LN_EPS)
            xh = xc * rstd
            xo = xh * g_ref[...] + b_ref[...]
            xo_ref[...] = xo
            xb_ref[...] = xo.astype(BF16)
            xh_ref[...] = xh
            rstd_ref[...] = rstd

    row = pl.BlockSpec((tm, D), lambda i, k: (i, 0))
    vec = pl.BlockSpec((1, D), lambda i, k: (0, 0))
    return pl.pallas_call(
        body, name="mm_res_ln", grid=(T // tm, nk),
        in_specs=[pl.BlockSpec((tm, tk), lambda i, k: (i, k)), pl.BlockSpec((tk, D), lambda i, k: (k, 0)),
                  row, vec, vec],
        out_specs=[row, row, row, pl.BlockSpec((tm, 1), lambda i, k: (i, 0))],
        out_shape=[jax.ShapeDtypeStruct((T, D), F32), jax.ShapeDtypeStruct((T, D), BF16),
                   jax.ShapeDtypeStruct((T, D), F32), jax.ShapeDtypeStruct((T, 1), F32)],
        scratch_shapes=[pltpu.VMEM((tm, D), F32)],
        compiler_params=_params("arbitrary", "arbitrary"),
    )(a, w, res, g, b)


def _mm_nt(a, w):
    M, K = a.shape
    N = w.shape[0]
    tm = _tile(M, 512, 16)
    tn = _tile(N, 512, LANES)

    def body(a_ref, w_ref, o_ref):
        o_ref[...] = _dot_nt(a_ref[...], w_ref[...])

    return pl.pallas_call(
        body, name="mm_nt", grid=(N // tn, M // tm),
        in_specs=[pl.BlockSpec((tm, K), lambda j, i: (i, 0)), pl.BlockSpec((tn, K), lambda j, i: (j, 0))],
        out_specs=pl.BlockSpec((tm, tn), lambda j, i: (i, j)),
        out_shape=jax.ShapeDtypeStruct((M, N), F32),
        compiler_params=_params("arbitrary", "arbitrary"),
    )(a, w)


def _mm_tn(a, b, scale, out_dtype):
    T, M = a.shape
    N = b.shape[1]
    tm = _tile(M, 512, LANES)
    tn = _tile(N, 1024, LANES)

    def body(a_ref, b_ref, o_ref):
        o_ref[...] = (scale * _dot_tn(a_ref[...], b_ref[...])).astype(out_dtype)

    return pl.pallas_call(
        body, name="mm_tn", grid=(M // tm, N // tn),
        in_specs=[pl.BlockSpec((T, tm), lambda i, j: (0, i)), pl.BlockSpec((T, tn), lambda i, j: (0, j))],
        out_specs=pl.BlockSpec((tm, tn), lambda i, j: (i, j)),
        out_shape=jax.ShapeDtypeStruct((M, N), out_dtype),
        compiler_params=_params("arbitrary", "arbitrary"),
    )(a, b)


def _ln_bwd(dy, xh, rstd, g):
    dxh = dy * g
    dr = rstd * (dxh - _rowmean(dxh) - xh * _rowmean(dxh * xh))
    return dr, _colsum(dy * xh), _colsum(dy)


def _loss_ln_bwd(y, target, xh, rstd, g):
    T, D = y.shape
    tm = _tile(T, 256, 16)

    def body(y_ref, t_ref, xh_ref, rstd_ref, g_ref, dr_ref, drb_ref, dg_ref, db_ref, loss_ref):
        i = pl.program_id(0)

        @pl.when(i == 0)
        def _():
            dg_ref[...] = jnp.zeros_like(dg_ref)
            db_ref[...] = jnp.zeros_like(db_ref)
            loss_ref[...] = jnp.zeros_like(loss_ref)

        err = y_ref[...] - t_ref[...]
        loss_ref[...] += (0.5 / D) * _colsum(jnp.sum(err * err, axis=1, keepdims=True))
        dr, dg, db = _ln_bwd(err * (1.0 / D), xh_ref[...], rstd_ref[...], g_ref[...])
        dr_ref[...] = dr
        drb_ref[...] = dr.astype(BF16)
        dg_ref[...] += dg
        db_ref[...] += db

    row = pl.BlockSpec((tm, D), lambda i: (i, 0))
    vec = pl.BlockSpec((1, D), lambda i: (0, 0))
    return pl.pallas_call(
        body, name="loss_ln_bwd", grid=(T // tm,),
        in_specs=[row, row, row, pl.BlockSpec((tm, 1), lambda i: (i, 0)), vec],
        out_specs=[row, row, vec, vec, pl.BlockSpec((1, 1), lambda i: (0, 0))],
        out_shape=[jax.ShapeDtypeStruct((T, D), F32), jax.ShapeDtypeStruct((T, D), BF16),
                   jax.ShapeDtypeStruct((1, D), F32), jax.ShapeDtypeStruct((1, D), F32),
                   jax.ShapeDtypeStruct((1, 1), F32)],
        compiler_params=_params("arbitrary"),
    )(y, target, xh, rstd, g)


def _ffn_down_bwd(drb, wd, gpre, upre):
    T, D = drb.shape
    F = wd.shape[0]
    tm = _tile(T, 512, 16)
    tn = _tile(F, 512, LANES)

    def body(dr_ref, wd_ref, g_ref, u_ref, dg_ref, du_ref):
        dh = 0.5 * _dot_nt(dr_ref[...], wd_ref[...])
        g = g_ref[...]
        u = u_ref[...]
        sg = jax.nn.sigmoid(g)
        du_ref[...] = (dh * (g * sg)).astype(BF16)
        dg_ref[...] = (dh * u * (sg * (1.0 + g * (1.0 - sg)))).astype(BF16)

    t_spec = pl.BlockSpec((tm, tn), lambda j, i: (i, j))
    return pl.pallas_call(
        body, name="ffn_down_bwd", grid=(F // tn, T // tm),
        in_specs=[pl.BlockSpec((tm, D), lambda j, i: (i, 0)), pl.BlockSpec((tn, D), lambda j, i: (j, 0)),
                  t_spec, t_spec],
        out_specs=[t_spec, t_spec],
        out_shape=[jax.ShapeDtypeStruct((T, F), BF16), jax.ShapeDtypeStruct((T, F), BF16)],
        compiler_params=_params("arbitrary", "arbitrary"),
    )(drb, wd, gpre, upre)


def _mm_dx(pairs, res, alpha, ln=None, after=()):
    T, K = pairs[0][0].shape
    D = res.shape[1]
    n = len(pairs)
    n_after = len(after)
    tm = _tile(T, 256, 16)
    tk = _tile(K, _k_rows(V7X_VMEM_BYTES * 3 // 8, n * D), LANES)
    nk = K // tk
    with_ln = ln is not None

    def body(*refs):
        a_refs = refs[0:2 * n:2]
        w_refs = refs[1:2 * n:2]
        refs = refs[2 * n + n_after:]
        res_ref = refs[0]
        acc_ref = refs[-1]
        i = pl.program_id(0)
        k = pl.program_id(1)

        @pl.when(k == 0)
        def _():
            acc_ref[...] = jnp.zeros_like(acc_ref)

        for a_ref, w_ref in zip(a_refs, w_refs):
            acc_ref[...] += _dot_nn(a_ref[...], w_ref[...])

        if with_ln:
            xh_ref, rstd_ref, g_ref, dr_ref, drb_ref, dg_ref, db_ref = refs[1:8]

            @pl.when((i == 0) & (k == 0))
            def _():
                dg_ref[...] = jnp.zeros_like(dg_ref)
                db_ref[...] = jnp.zeros_like(db_ref)

            @pl.when(k == nk - 1)
            def _():
                dx = alpha * res_ref[...] + acc_ref[...]
                dr, dg, db = _ln_bwd(dx, xh_ref[...], rstd_ref[...], g_ref[...])
                dr_ref[...] = dr
                drb_ref[...] = dr.astype(BF16)
                dg_ref[...] += dg
                db_ref[...] += db
        else:
            dx_ref = refs[1]

            @pl.when(k == nk - 1)
            def _():
                dx_ref[...] = alpha * res_ref[...] + acc_ref[...]

    row = pl.BlockSpec((tm, D), lambda i, k: (i, 0))
    vec = pl.BlockSpec((1, D), lambda i, k: (0, 0))
    in_specs, operands = [], []
    for a, w in pairs:
        in_specs += [pl.BlockSpec((tm, tk), lambda i, k: (i, k)), pl.BlockSpec((tk, D), lambda i, k: (k, 0))]
        operands += [a, w]
    in_specs += [ANY_SPEC] * n_after
    operands += list(after)
    in_specs.append(row)
    operands.append(res)
    if with_ln:
        in_specs += [row, pl.BlockSpec((tm, 1), lambda i, k: (i, 0)), vec]
        operands += list(ln)
        out_specs = [row, row, vec, vec]
        out_shape = [jax.ShapeDtypeStruct((T, D), F32), jax.ShapeDtypeStruct((T, D), BF16),
                     jax.ShapeDtypeStruct((1, D), F32), jax.ShapeDtypeStruct((1, D), F32)]
    else:
        out_specs = [row]
        out_shape = [jax.ShapeDtypeStruct((T, D), F32)]
    return pl.pallas_call(
        body, name="mm_dx_ln_bwd" if with_ln else "mm_dx", grid=(T // tm, nk),
        in_specs=in_specs, out_specs=out_specs, out_shape=out_shape,
        scratch_shapes=[pltpu.VMEM((tm, D), F32)],
        compiler_params=_params("arbitrary", "arbitrary"),
    )(*operands)


def _s5_discretize(lre, lim, ldt, br, bi):
    dt = jnp.exp(ldt)
    mag = jnp.exp(lre * dt)
    ang = lim * dt
    ar = mag * jnp.cos(ang)
    ai = mag * jnp.sin(ang)
    den = lre * lre + lim * lim
    nr = ar - 1.0
    qr = (nr * lre + ai * lim) / den
    qi = (ai * lre - nr * lim) / den
    bbr = qr[None] * br - qi[None] * bi
    bbi = qr[None] * bi + qi[None] * br
    return ar, ai, bbr, bbi


def _s5_params_fwd(lre, lim, ldt, br, bi):
    def body(lre_ref, lim_ref, ldt_ref, br_ref, bi_ref, ar_ref, ai_ref, bbr_ref, bbi_ref):
        ar, ai, bbr, bbi = _s5_discretize(lre_ref[...], lim_ref[...], ldt_ref[...], br_ref[...], bi_ref[...])
        ar_ref[...] = ar
        ai_ref[...] = ai
        bbr_ref[...] = bbr
        bbi_ref[...] = bbi

    sds = jax.ShapeDtypeStruct
    return pl.pallas_call(
        body, name="s5_params_fwd",
        out_shape=[sds(lre.shape, F32), sds(lre.shape, F32), sds(br.shape, F32), sds(br.shape, F32)],
        compiler_params=pltpu.CompilerParams(vmem_limit_bytes=VMEM_LIMIT),
    )(lre, lim, ldt, br, bi)


def _s5_params_bwd(lre, lim, ldt, br, bi, dar, dai, dbbr, dbbi):
    def body(lre_ref, lim_ref, ldt_ref, br_ref, bi_ref, dar_ref, dai_ref, dbbr_ref, dbbi_ref,
             o_lre, o_lim, o_ldt, o_br, o_bi):
        _, vjp = jax.vjp(_s5_discretize, lre_ref[...], lim_ref[...], ldt_ref[...], br_ref[...], bi_ref[...])
        g = vjp((dar_ref[...], dai_ref[...], dbbr_ref[...], dbbi_ref[...]))
        o_lre[...] = g[0]
        o_lim[...] = g[1]
        o_ldt[...] = g[2]
        o_br[...] = g[3]
        o_bi[...] = g[4]

    sds = jax.ShapeDtypeStruct
    return pl.pallas_call(
        body, name="s5_params_bwd",
        out_shape=[sds(lre.shape, F32), sds(lre.shape, F32), sds(ldt.shape, F32), sds(br.shape, F32),
                   sds(br.shape, F32)],
        compiler_params=pltpu.CompilerParams(vmem_limit_bytes=VMEM_LIMIT),
    )(lre, lim, ldt, br, bi, dar, dai, dbbr, dbbi)


def _s5_fwd(proj, bdr, bdi, cdr, cdi, ar, ai, dvec):
    T = proj.shape[0]
    GB, UB, SB = bdr.shape
    tc = _tile(T, 256, SUBLANES)

    def body(u_ref, bdr_ref, bdi_ref, cdr_ref, cdi_ref, ar_ref, ai_ref, d_ref, y_ref, sr_ref, si_ref,
             cr_ref, ci_ref):
        @pl.when(pl.program_id(1) == 0)
        def _():
            cr_ref[...] = jnp.zeros_like(cr_ref)
            ci_ref[...] = jnp.zeros_like(ci_ref)

        u = u_ref[...]
        ub = u.astype(BF16)
        sr_ref[...] = _dot_nn(ub, bdr_ref[...])
        si_ref[...] = _dot_nn(ub, bdi_ref[...])
        a_re = ar_ref[...]
        a_im = ai_ref[...]

        def step(t, carry):
            p_re, p_im = carry
            row = pl.ds(t, 1)
            n_re = a_re * p_re - a_im * p_im + sr_ref[row, :]
            n_im = a_re * p_im + a_im * p_re + si_ref[row, :]
            sr_ref[row, :] = n_re
            si_ref[row, :] = n_im
            return n_re, n_im

        p_re, p_im = lax.fori_loop(0, tc, step, (cr_ref[...], ci_ref[...]), unroll=8)
        cr_ref[...] = p_re
        ci_ref[...] = p_im
        y_ref[...] = (_dot_nn(sr_ref[...].astype(BF16), cdr_ref[...])
                      - _dot_nn(si_ref[...].astype(BF16), cdi_ref[...]) + d_ref[...] * u)

    return pl.pallas_call(
        body, name="s5_fwd", grid=(GB, T // tc),
        in_specs=[pl.BlockSpec((tc, UB), lambda j, t: (t, j)),
                  pl.BlockSpec((None, UB, SB), lambda j, t: (j, 0, 0)),
                  pl.BlockSpec((None, UB, SB), lambda j, t: (j, 0, 0)),
                  pl.BlockSpec((None, SB, UB), lambda j, t: (j, 0, 0)),
                  pl.BlockSpec((None, SB, UB), lambda j, t: (j, 0, 0)),
                  pl.BlockSpec((1, SB), lambda j, t: (0, j)),
                  pl.BlockSpec((1, SB), lambda j, t: (0, j)),
                  pl.BlockSpec((1, UB), lambda j, t: (0, j))],
        out_specs=[pl.BlockSpec((tc, UB), lambda j, t: (t, j)),
                   pl.BlockSpec((tc, SB), lambda j, t: (t, j)),
                   pl.BlockSpec((tc, SB), lambda j, t: (t, j))],
        out_shape=[jax.ShapeDtypeStruct((T, GB * UB), F32), jax.ShapeDtypeStruct((T, GB * SB), F32),
                   jax.ShapeDtypeStruct((T, GB * SB), F32)],
        scratch_shapes=[pltpu.VMEM((1, SB), F32), pltpu.VMEM((1, SB), F32)],
        compiler_params=_params("arbitrary", "arbitrary"),
    )(proj, bdr, bdi, cdr, cdi, ar, ai, dvec)


def _s5_bwd(dy, proj, sr, si, bdr, bdi, cdr, cdi, ar, ai, dvec):
    T = dy.shape[0]
    GB, UB, SB = bdr.shape
    tc = _tile(T, 256, SUBLANES)
    nt = T // tc
    halo_blocks = tc // SUBLANES

    def body(dy_ref, u_ref, sr_ref, si_ref, hr_ref, hi_ref, bdr_ref, bdi_ref, cdr_ref, cdi_ref, ar_ref, ai_ref,
             d_ref, du_ref, dbdr_ref, dbdi_ref, dcdr_ref, dcdi_ref, dar_ref, dai_ref, dd_ref,
             gr_ref, gi_ref, pr_ref, pi_ref, cr_ref, ci_ref):
        step_no = pl.program_id(1)
        first_chunk = step_no == nt - 1

        @pl.when(step_no == 0)
        def _():
            for ref in (cr_ref, ci_ref, dbdr_ref, dbdi_ref, dcdr_ref, dcdi_ref, dar_ref, dai_ref, dd_ref):
                ref[...] = jnp.zeros_like(ref)

        dy = dy_ref[...]
        dyb = dy.astype(BF16)
        u = u_ref[...]
        s_re = sr_ref[...]
        s_im = si_ref[...]
        gr_ref[...] = _dot_nt(dyb, cdr_ref[...])
        gi_ref[...] = -_dot_nt(dyb, cdi_ref[...])
        dcdr_ref[...] += _dot_tn(s_re.astype(BF16), dyb)
        dcdi_ref[...] -= _dot_tn(s_im.astype(BF16), dyb)
        keep = jnp.where(first_chunk, 0.0, 1.0)
        pr_ref[0:SUBLANES, :] = hr_ref[...] * keep
        pi_ref[0:SUBLANES, :] = hi_ref[...] * keep
        pr_ref[SUBLANES:, :] = s_re
        pi_ref[SUBLANES:, :] = s_im
        a_re = ar_ref[...]
        a_im = ai_ref[...]

        def step(n, carry):
            c_re, c_im, acc_re, acc_im = carry
            t = tc - 1 - n
            row = pl.ds(t, 1)
            g_re = gr_ref[row, :] + a_re * c_re + a_im * c_im
            g_im = gi_ref[row, :] + a_re * c_im - a_im * c_re
            gr_ref[row, :] = g_re
            gi_ref[row, :] = g_im
            prev = pl.ds(t + SUBLANES - 1, 1)
            p_re = pr_ref[prev, :]
            p_im = pi_ref[prev, :]
            acc_re = acc_re + p_re * g_re + p_im * g_im
            acc_im = acc_im + p_re * g_im - p_im * g_re
            return g_re, g_im, acc_re, acc_im

        zero = jnp.zeros((1, SB), F32)
        c_re, c_im, acc_re, acc_im = lax.fori_loop(0, tc, step, (cr_ref[...], ci_ref[...], zero, zero), unroll=8)
        cr_ref[...] = c_re
        ci_ref[...] = c_im
        dar_ref[...] += acc_re
        dai_ref[...] += acc_im
        gsr = gr_ref[...].astype(BF16)
        gsi = gi_ref[...].astype(BF16)
        ub = u.astype(BF16)
        dbdr_ref[...] += _dot_tn(ub, gsr)
        dbdi_ref[...] += _dot_tn(ub, gsi)
        du_ref[...] = (_dot_nt(gsr, bdr_ref[...]) + _dot_nt(gsi, bdi_ref[...]) + d_ref[...] * dy).astype(BF16)
        dd_ref[...] += _colsum(dy * u)

    def rev(t):
        return nt - 1 - t

    def halo(j, t):
        return (jnp.maximum(rev(t) * halo_blocks - 1, 0), j)

    ublk = pl.BlockSpec((tc, UB), lambda j, t: (rev(t), j))
    sblk = pl.BlockSpec((tc, SB), lambda j, t: (rev(t), j))
    bd_spec = pl.BlockSpec((None, UB, SB), lambda j, t: (j, 0, 0))
    cd_spec = pl.BlockSpec((None, SB, UB), lambda j, t: (j, 0, 0))
    svec = pl.BlockSpec((1, SB), lambda j, t: (0, j))
    uvec = pl.BlockSpec((1, UB), lambda j, t: (0, j))
    sds = jax.ShapeDtypeStruct
    return pl.pallas_call(
        body, name="s5_bwd", grid=(GB, nt),
        in_specs=[ublk, ublk, sblk, sblk, pl.BlockSpec((SUBLANES, SB), halo), pl.BlockSpec((SUBLANES, SB), halo),
                  bd_spec, bd_spec, cd_spec, cd_spec, svec, svec, uvec],
        out_specs=[ublk, bd_spec, bd_spec, cd_spec, cd_spec, svec, svec, uvec],
        out_shape=[sds((T, GB * UB), BF16), sds((GB, UB, SB), F32), sds((GB, UB, SB), F32),
                   sds((GB, SB, UB), F32), sds((GB, SB, UB), F32), sds((1, GB * SB), F32),
                   sds((1, GB * SB), F32), sds((1, GB * UB), F32)],
        scratch_shapes=[pltpu.VMEM((tc, SB), F32), pltpu.VMEM((tc, SB), F32),
                        pltpu.VMEM((tc + SUBLANES, SB), F32), pltpu.VMEM((tc + SUBLANES, SB), F32),
                        pltpu.VMEM((1, SB), F32), pltpu.VMEM((1, SB), F32)],
        compiler_params=_params("arbitrary", "arbitrary"),
    )(dy, proj, sr, si, sr, si, bdr, bdi, cdr, cdi, ar, ai, dvec)


def _shift_down(v, k):
    rows = lax.broadcasted_iota(jnp.int32, v.shape, 0)
    return jnp.where(rows >= k, pltpu.roll(v, k, 0), 0.0)


def _shift_up(v, k):
    n = v.shape[0]
    rows = lax.broadcasted_iota(jnp.int32, v.shape, 0)
    return jnp.where(rows < n - k, pltpu.roll(v, n - k, 0), 0.0)


def _conv_specs(T, cb, n_s5_blocks, n_conv_blocks):
    gb = pl.BlockSpec((T, cb), lambda j: (0, n_s5_blocks + j))
    gc = pl.BlockSpec((T, cb), lambda j: (0, n_s5_blocks + n_conv_blocks + j))
    hh = pl.BlockSpec((T, cb), lambda j: (0, n_s5_blocks + 2 * n_conv_blocks + j))
    return gb, gc, hh


def _conv_fwd(proj, cw, cbias, d_s5, d_conv):
    T = proj.shape[0]
    cb = _tile(d_conv, 256, LANES)

    def body(gb_ref, gc_ref, hh_ref, w_ref, b_ref, z_ref):
        v = gc_ref[...] * hh_ref[...]
        w = w_ref[...]
        cv = b_ref[...] + w[0:1, :] * _shift_down(v, 2) + w[1:2, :] * _shift_down(v, 1) + w[2:3, :] * v
        z_ref[...] = gb_ref[...] * cv

    gb, gc, hh = _conv_specs(T, cb, d_s5 // cb, d_conv // cb)
    col = pl.BlockSpec((T, cb), lambda j: (0, j))
    return pl.pallas_call(
        body, name="conv_fwd", grid=(d_conv // cb,),
        in_specs=[gb, gc, hh, pl.BlockSpec((CONV_W, cb), lambda j: (0, j)), pl.BlockSpec((1, cb), lambda j: (0, j))],
        out_specs=col, out_shape=jax.ShapeDtypeStruct((T, d_conv), F32),
        compiler_params=_params("arbitrary"),
    )(proj, proj, proj, cw, cbias)


def _conv_bwd(dz, proj, cw, cbias, d_s5, d_conv):
    T = proj.shape[0]
    cb = _tile(d_conv, 256, LANES)

    def body(dz_ref, gb_ref, gc_ref, hh_ref, w_ref, b_ref, dgb_ref, dgc_ref, dhh_ref, dw_ref, db_ref):
        gc = gc_ref[...]
        hh = hh_ref[...]
        dz = dz_ref[...]
        w = w_ref[...]
        v = gc * hh
        v1 = _shift_down(v, 1)
        v2 = _shift_down(v, 2)
        cv = b_ref[...] + w[0:1, :] * v2 + w[1:2, :] * v1 + w[2:3, :] * v
        dgb_ref[...] = (dz * cv).astype(BF16)
        dcv = dz * gb_ref[...]
        dv = w[2:3, :] * dcv + w[1:2, :] * _shift_up(dcv, 1) + w[0:1, :] * _shift_up(dcv, 2)
        dgc_ref[...] = (dv * hh).astype(BF16)
        dhh_ref[...] = (dv * gc).astype(BF16)
        dw_ref[0:1, :] = _colsum(dcv * v2)
        dw_ref[1:2, :] = _colsum(dcv * v1)
        dw_ref[2:3, :] = _colsum(dcv * v)
        db_ref[...] = _colsum(dcv)

    gb, gc, hh = _conv_specs(T, cb, d_s5 // cb, d_conv // cb)
    col = pl.BlockSpec((T, cb), lambda j: (0, j))
    wspec = pl.BlockSpec((CONV_W, cb), lambda j: (0, j))
    bspec = pl.BlockSpec((1, cb), lambda j: (0, j))
    sds = jax.ShapeDtypeStruct
    return pl.pallas_call(
        body, name="conv_bwd", grid=(d_conv // cb,),
        in_specs=[col, gb, gc, hh, wspec, bspec],
        out_specs=[col, col, col, wspec, bspec],
        out_shape=[sds((T, d_conv), BF16), sds((T, d_conv), BF16), sds((T, d_conv), BF16),
                   sds((CONV_W, d_conv), F32), sds((1, d_conv), F32)],
        compiler_params=_params("arbitrary"),
    )(dz, proj, proj, proj, cw, cbias)


def _rms(v, g):
    rstd = lax.rsqrt(_rowmean(v * v) + RMS_EPS)
    return v * rstd * g, rstd


def _rms_bwd(dyn, v, rstd, g):
    w = dyn * g
    return rstd * w - v * (rstd * rstd * rstd) * _rowmean(w * v), _colsum(dyn * v * rstd)


def _mix_post(y, z, wglu, g_s5, g_conv):
    T, C = y.shape
    tm = _tile(T, 256, 16)

    def body(y_ref, z_ref, w_ref, gs_ref, gc_ref, m_ref, gl_ref):
        ge = jax.nn.gelu(y_ref[...])
        gl = _dot_nn(ge.astype(BF16), w_ref[...])
        gl_ref[...] = gl
        yn, _ = _rms(ge * jax.nn.sigmoid(gl), gs_ref[...])
        zn, _ = _rms(z_ref[...], gc_ref[...])
        m_ref[:, 0:C] = yn.astype(BF16)
        m_ref[:, C:2 * C] = zn.astype(BF16)

    row = pl.BlockSpec((tm, C), lambda i: (i, 0))
    vec = pl.BlockSpec((1, C), lambda i: (0, 0))
    return pl.pallas_call(
        body, name="mix_post", grid=(T // tm,),
        in_specs=[row, row, pl.BlockSpec((C, C), lambda i: (0, 0)), vec, vec],
        out_specs=[pl.BlockSpec((tm, 2 * C), lambda i: (i, 0)), row],
        out_shape=[jax.ShapeDtypeStruct((T, 2 * C), BF16), jax.ShapeDtypeStruct((T, C), F32)],
        compiler_params=_params("arbitrary"),
    )(y, z, wglu, g_s5, g_conv)


def _mix_post_bwd(dm, y, gl, z, wglu, g_s5, g_conv):
    T, C = y.shape
    tm = _tile(T, 256, 16)

    def body(dm_ref, y_ref, gl_ref, z_ref, w_ref, gs_ref, gc_ref, dy_ref, dz_ref, dw_ref, dgs_ref, dgc_ref):
        @pl.when(pl.program_id(0) == 0)
        def _():
            dw_ref[...] = jnp.zeros_like(dw_ref)
            dgs_ref[...] = jnp.zeros_like(dgs_ref)
            dgc_ref[...] = jnp.zeros_like(dgc_ref)

        yv = y_ref[...]
        ge, gelu_vjp = jax.vjp(jax.nn.gelu, yv)
        gl = gl_ref[...]
        sg = jax.nn.sigmoid(gl)
        y2 = ge * sg
        _, rstd_y = _rms(y2, gs_ref[...])
        dy2, dgs = _rms_bwd(dm_ref[:, 0:C], y2, rstd_y, gs_ref[...])
        dgs_ref[...] += dgs
        dgl = (dy2 * ge * sg * (1.0 - sg)).astype(BF16)
        dge = dy2 * sg + _dot_nt(dgl, w_ref[...])
        dw_ref[...] += _dot_tn(ge.astype(BF16), dgl)
        dy_ref[...] = gelu_vjp(dge)[0]
        zv = z_ref[...]
        _, rstd_z = _rms(zv, gc_ref[...])
        dz, dgc = _rms_bwd(dm_ref[:, C:2 * C], zv, rstd_z, gc_ref[...])
        dz_ref[...] = dz
        dgc_ref[...] += dgc

    row = pl.BlockSpec((tm, C), lambda i: (i, 0))
    vec = pl.BlockSpec((1, C), lambda i: (0, 0))
    full = pl.BlockSpec((C, C), lambda i: (0, 0))
    sds = jax.ShapeDtypeStruct
    return pl.pallas_call(
        body, name="mix_post_bwd", grid=(T // tm,),
        in_specs=[pl.BlockSpec((tm, 2 * C), lambda i: (i, 0)), row, row, row, full, vec, vec],
        out_specs=[row, row, full, vec, vec],
        out_shape=[sds((T, C), F32), sds((T, C), F32), sds((C, C), F32), sds((1, C), F32), sds((1, C), F32)],
        compiler_params=_params("arbitrary"),
    )(dm, y, gl, z, wglu, g_s5, g_conv)


def _adamw(w, g, m, v):
    m = ADAM_B1 * m + (1.0 - ADAM_B1) * g
    v = ADAM_B2 * v + (1.0 - ADAM_B2) * (g * g)
    m_hat = m / (1.0 - ADAM_B1 ** ADAM_STEP)
    v_hat = v / (1.0 - ADAM_B2 ** ADAM_STEP)
    return -ADAM_LR * (m_hat / (jnp.sqrt(v_hat) + ADAM_EPS) + ADAM_WD * w), m, v


def _sum_parts(p_ref):
    total = p_ref[0].astype(F32)
    for d in range(1, N_DEV):
        total = total + p_ref[d].astype(F32)
    return total


def _row_tile(R, C, n_streams):
    budget = VMEM_LIMIT // 3 // (n_streams * C * 4)
    return _tile(R, max(BF16_ROWS, budget), BF16_ROWS)


def _reduce_parts(parts):
    L, _, R, C = parts.shape
    tr = _row_tile(R, C, N_DEV + 1)

    def body(p_ref, o_ref):
        o_ref[...] = _sum_parts(p_ref)

    return pl.pallas_call(
        body, name="reduce_parts", grid=(L, R // tr),
        in_specs=[pl.BlockSpec((None, N_DEV, tr, C), lambda l, i: (l, 0, i, 0))],
        out_specs=pl.BlockSpec((None, tr, C), lambda l, i: (l, i, 0)),
        out_shape=jax.ShapeDtypeStruct((L, R, C), F32),
        compiler_params=_params("arbitrary", "arbitrary"),
    )(parts)


def _adamw_update(w, m, v, grad=None, parts=None):
    L, R, C = w.shape
    from_parts = parts is not None
    tr = _row_tile(R, C, (N_DEV if from_parts else 1) + 7)

    def body(g_in_ref, w_ref, m_ref, v_ref, g_ref, d_ref, nm_ref, nv_ref):
        g = _sum_parts(g_in_ref) if from_parts else g_in_ref[...]
        delta, nm, nv = _adamw(w_ref[...], g, m_ref[...], v_ref[...])
        g_ref[...] = g
        d_ref[...] = delta
        nm_ref[...] = nm
        nv_ref[...] = nv

    blk = pl.BlockSpec((None, tr, C), lambda l, i: (l, i, 0))
    g_spec = pl.BlockSpec((None, N_DEV, tr, C), lambda l, i: (l, 0, i, 0)) if from_parts else blk
    out = jax.ShapeDtypeStruct((L, R, C), F32)
    return pl.pallas_call(
        body, name="adamw_parts" if from_parts else "adamw", grid=(L, R // tr),
        in_specs=[g_spec, blk, blk, blk], out_specs=[blk, blk, blk, blk], out_shape=[out, out, out, out],
        compiler_params=_params("arbitrary", "arbitrary"),
    )(parts if from_parts else grad, w, m, v)


def _me():
    x, y, c = (lax.axis_index(a) for a in AXES)
    return x, y, c, 4 * x + 2 * y + c


def _peer(rel):
    x, y, c, _ = _me()
    px = 1 - x if rel & 4 else x
    py = 1 - y if rel & 2 else y
    pc = 1 - c if rel & 1 else c
    return (px, py, pc), 4 * px + 2 * py + pc


DATAFLOW = pltpu.SideEffectType.DATAFLOW_SIDE_EFFECTING


def _remote_copy(src, dst, sems):
    return functools.partial(pltpu.make_async_remote_copy, src_ref=src, dst_ref=dst, **sems)


def _gather_copies(n):
    def copies(srcs, lands, send_sems, recv_sems, local_sems):
        me = _me()[3]
        local, remote = [], []
        for k in range(n):
            local.append(functools.partial(pltpu.make_async_copy, srcs[k], lands[k].at[me], local_sems.at[k]))
            for rel in range(1, N_DEV):
                dev, blk = _peer(rel)
                sems = dict(send_sem=send_sems.at[_sem_index(k, rel)], recv_sem=recv_sems.at[_sem_index(k, rel)],
                            device_id=dev, device_id_type=MESH)
                remote.append((_remote_copy(srcs[k], lands[k].at[me], sems), _remote_copy(srcs[k], lands[k].at[blk], sems)))
        return local, remote

    return copies


def _scatter_copies(n, layer):
    def copies(srcs, lands, send_sems, recv_sems, local_sems):
        me = _me()[3]
        local, remote = [], []
        for k in range(n):
            local.append(functools.partial(pltpu.make_async_copy, srcs[k].at[me], lands[k].at[layer, me],
                                           local_sems.at[k]))
            for rel in range(1, N_DEV):
                dev, blk = _peer(rel)
                sems = dict(send_sem=send_sems.at[_sem_index(k, rel)], recv_sem=recv_sems.at[_sem_index(k, rel)],
                            device_id=dev, device_id_type=MESH)
                remote.append((_remote_copy(srcs[k].at[blk], lands[k].at[layer, me], sems),
                               _remote_copy(srcs[k].at[blk], lands[k].at[layer, blk], sems)))
        return local, remote

    return copies


def _sem_shapes(n):
    return [pltpu.SemaphoreType.DMA((n * (N_DEV - 1),)), pltpu.SemaphoreType.DMA((n * (N_DEV - 1),)),
            pltpu.SemaphoreType.DMA((n,))]


def _sem_index(k, rel):
    return k * (N_DEV - 1) + rel - 1


def _exchange(copies, name, srcs, lands):
    n_src, n_land = len(srcs), len(lands)

    def body(*refs):
        src_refs = refs[:n_src]
        land_refs = refs[n_src + n_land:n_src + 2 * n_land]
        local, remote = copies(src_refs, land_refs, *refs[n_src + 2 * n_land:])
        local = [cp() for cp in local]
        sends = [send() for send, _ in remote]
        for cp in local + sends:
            cp.start()
        for send, (_, landing) in zip(sends, remote):
            send.wait_send()
            landing().wait_recv()
        for cp in local:
            cp.wait()

    return pl.pallas_call(
        body, name=name, in_specs=[HBM_SPEC] * (n_src + n_land), out_specs=[HBM_SPEC] * n_land,
        out_shape=[jax.ShapeDtypeStruct(b.shape, b.dtype) for b in lands],
        scratch_shapes=_sem_shapes(n_src),
        input_output_aliases={n_src + k: k for k in range(n_land)},
        compiler_params=pltpu.CompilerParams(has_side_effects=True),
    )(*srcs, *lands)


def _hbm(arrays):
    return [pltpu.with_memory_space_constraint(a, pltpu.HBM) for a in arrays]


def _exchange_start(copies, name, srcs, lands, after):
    n_src, n_land, n_after = len(srcs), len(lands), len(after)
    n_data = n_src + n_land

    def body(*refs):
        outs = refs[n_data + n_after:]
        local, remote = copies(refs[:n_src], refs[n_src:n_data], *outs[:3])
        for cp in local:
            cp().start()
        for send, _ in remote:
            send().start()
        outs[-1][...] = jnp.zeros_like(outs[-1])

    res = pl.pallas_call(
        body, name=name, in_specs=[HBM_SPEC] * n_data + [ANY_SPEC] * n_after,
        out_specs=[SEM_SPEC] * 3 + [HBM_SPEC] * n_data + [pl.BlockSpec(memory_space=pltpu.VMEM)],
        out_shape=_sem_shapes(n_src) + [pltpu.HBM(a.shape, a.dtype) for a in list(srcs) + list(lands)]
        + [jax.ShapeDtypeStruct((SUBLANES, LANES), F32)],
        input_output_aliases={k: 3 + k for k in range(n_data)},
        compiler_params=pltpu.CompilerParams(has_side_effects=DATAFLOW),
    )(*_hbm(list(srcs) + list(lands)), *after)
    return res[:3], res[3:3 + n_src], res[3 + n_src:3 + n_data], res[-1]


def _exchange_wait(copies, name, sems, srcs, lands, after):
    n_src, n_land, n_after = len(srcs), len(lands), len(after)
    n_data = n_src + n_land

    def body(*refs):
        local, remote = copies(refs[:n_src], refs[n_src:n_data], *refs[n_data:n_data + 3])
        for send, landing in remote:
            send().wait_send()
            landing().wait_recv()
        for cp in local:
            cp().wait()

    res = pl.pallas_call(
        body, name=name, in_specs=[HBM_SPEC] * n_data + [SEM_SPEC] * 3 + [ANY_SPEC] * n_after,
        out_specs=[HBM_SPEC] * n_data,
        out_shape=[pltpu.HBM(a.shape, a.dtype) for a in list(srcs) + list(lands)],
        input_output_aliases={k: k for k in range(n_data)},
        compiler_params=pltpu.CompilerParams(has_side_effects=DATAFLOW),
    )(*srcs, *lands, *sems, *after)
    return res[n_src:]


def _block_diag(blocks, row_major):
    L, GB, g, P, N = blocks.shape
    eye = jnp.eye(g, dtype=blocks.dtype)
    if row_major:
        return jnp.einsum("lbgpn,gh->lbgphn", blocks, eye).reshape(L, GB, g * P, g * N)
    return jnp.einsum("lbgpn,gh->lbhngp", blocks, eye).reshape(L, GB, g * N, g * P)


def _diag_blocks(mat, g, P, N, row_major):
    GB = mat.shape[0]
    eye = jnp.eye(g, dtype=mat.dtype)
    if row_major:
        return jnp.einsum("bgphn,gh->bgpn", mat.reshape(GB, g, P, g, N), eye)
    return jnp.einsum("bhngp,gh->bgpn", mat.reshape(GB, g, N, g, P), eye)


def _pack(arrays, rows_multiple):
    flat = jnp.concatenate([a.reshape(-1).astype(F32) for a in arrays])
    pad = (-flat.shape[0]) % (rows_multiple * LANES)
    return jnp.pad(flat, (0, pad)).reshape(-1, LANES)


def _unpack(packed, shapes):
    flat = packed.reshape(-1)
    out, pos = [], 0
    for s in shapes:
        n = math.prod(s)
        out.append(flat[pos:pos + n].reshape(s))
        pos += n
    return out


SMALL = ["ln1_g", "ln1_b", "s5_lam_re", "s5_lam_im", "s5_log_dt", "s5_b_re", "s5_b_im", "s5_c_re", "s5_c_im", "s5_d",
         "conv_b", "g_s5", "g_conv", "ln2_g", "ln2_b", "ln3_g", "ln3_b"]
WEIGHTS = ["ffn1_gate", "ffn1_up", "ffn1_down", "ln1_g", "ln1_b", "w_in", "s5_lam_re", "s5_lam_im", "s5_log_dt",
           "s5_b_re", "s5_b_im", "s5_c_re", "s5_c_im", "s5_d", "s5_w_glu", "conv_w", "conv_b", "g_s5", "g_conv",
           "w_out", "ln2_g", "ln2_b", "ffn2_gate", "ffn2_up", "ffn2_down", "ln3_g", "ln3_b"]
TRANSPOSED = ["ffn1_gate", "ffn1_up", "w_in", "ffn2_gate", "ffn2_up"]
UPDATED_TRANSPOSED = ["ffn1_gate", "ffn1_up", "ffn2_gate", "ffn2_up"]
GROUPS = {"a": ["ffn1_gate", "ffn1_up", "ffn1_down"], "b": ["w_in", "s5_w_glu", "w_out"],
          "c": ["ffn2_gate", "ffn2_up", "ffn2_down"]}


def _train_step(x, target, w, m, v):
    T, D = x.shape
    L = w["ln1_g"].shape[0]
    alpha = (2.0 * L) ** 0.25
    G = w["s5_log_dt"].shape[1]
    d_s5 = G * S5_P
    d_conv = w["conv_b"].shape[1]
    GB = G // S5_GROUPS_PER_BLOCK
    me = _me()[3]

    def shard(n, l):
        return (jnp.swapaxes(w[n][l], 0, 1) if n in TRANSPOSED else w[n][l]).astype(BF16)

    conv_w_rows = jnp.pad(w["conv_w"], ((0, 0), (0, SUBLANES - CONV_W), (0, 0)))
    gathers, token = {}, []
    for l in range(L):
        for grp, names in GROUPS.items():
            srcs = [shard(n, l) for n in names] + ([conv_w_rows[l]] if grp == "b" else [])
            lands = [lax.empty((N_DEV,) + a.shape, a.dtype) for a in srcs]
            sems, srcs, lands, tok = _exchange_start(_gather_copies(len(srcs)), f"gather_start_{l}{grp}", srcs, lands,
                                                     token)
            gathers[l, grp] = (sems, srcs, lands)
            token = [tok]

    def gathered(l, grp, after):
        sems, srcs, lands = gathers[l, grp]
        lands = _exchange_wait(_gather_copies(len(srcs)), f"gather_wait_{l}{grp}", sems, srcs, lands, after)
        full = {n: p.reshape(-1, p.shape[-1]) for n, p in zip(GROUPS[grp], lands)}
        if grp == "b":
            full["conv_w"] = jnp.swapaxes(lands[-1][:, :CONV_W, :], 0, 1).reshape(CONV_W, d_conv)
        return full

    lre = w["s5_lam_re"].reshape(L * G, S5_N)
    lim = w["s5_lam_im"].reshape(L * G, S5_N)
    ldt = w["s5_log_dt"].reshape(L * G, 1)
    b_re = jnp.transpose(w["s5_b_re"], (3, 0, 1, 2)).reshape(S5_P, L * G, S5_N)
    b_im = jnp.transpose(w["s5_b_im"], (3, 0, 1, 2)).reshape(S5_P, L * G, S5_N)
    ab_re, ab_im, bb_re, bb_im = _s5_params_fwd(lre, lim, ldt, b_re, b_im)

    def groups(bb):
        return jnp.transpose(bb.reshape(S5_P, L, GB, S5_GROUPS_PER_BLOCK, S5_N), (1, 2, 3, 0, 4))

    bd_re = _block_diag(groups(bb_re), True).astype(BF16)
    bd_im = _block_diag(groups(bb_im), True).astype(BF16)
    c_shape = (L, GB, S5_GROUPS_PER_BLOCK, S5_P, S5_N)
    cd_re = _block_diag(w["s5_c_re"].reshape(c_shape), False).astype(BF16)
    cd_im = _block_diag(w["s5_c_im"].reshape(c_shape), False).astype(BF16)
    a_re = ab_re.reshape(L, 1, G * S5_N)
    a_im = ab_im.reshape(L, 1, G * S5_N)
    d_vec = w["s5_d"].reshape(L, 1, d_s5)

    def vec(name, l):
        return w[name][l].reshape(1, -1)

    saved, weights = [], []
    x_in, x_in_b = x, x.astype(BF16)
    for l in range(L):
        gw = gathered(l, "a", [x_in] if l else token)
        s = {"x0b": x_in_b}
        s["g1"], s["u1"], s["h1"] = _ffn_up(x_in_b, gw["ffn1_gate"], gw["ffn1_up"])
        x1, s["x1b"], s["xh1"], s["rstd1"] = _mm_res_ln(s["h1"], gw["ffn1_down"], x_in, vec("ln1_g", l),
                                                         vec("ln1_b", l), 0.5, alpha)
        gw.update(gathered(l, "b", [s["x1b"]]))
        s["proj"] = _mm_nt(s["x1b"], gw["w_in"])
        s["y"], s["sr"], s["si"] = _s5_fwd(s["proj"], bd_re[l], bd_im[l], cd_re[l], cd_im[l], a_re[l], a_im[l],
                                           d_vec[l])
        s["z"] = _conv_fwd(s["proj"], gw["conv_w"], vec("conv_b", l), d_s5, d_conv)
        s["mcat"], s["gl"] = _mix_post(s["y"], s["z"], gw["s5_w_glu"], vec("g_s5", l), vec("g_conv", l))
        x2, s["x2b"], s["xh2"], s["rstd2"] = _mm_res_ln(s["mcat"], gw["w_out"], x1, vec("ln2_g", l),
                                                         vec("ln2_b", l), 1.0, alpha)
        gw.update(gathered(l, "c", [s["x2b"]]))
        s["g2"], s["u2"], s["h2"] = _ffn_up(s["x2b"], gw["ffn2_gate"], gw["ffn2_up"])
        x3, x3b, s["xh3"], s["rstd3"] = _mm_res_ln(s["h2"], gw["ffn2_down"], x2, vec("ln3_g", l), vec("ln3_b", l),
                                                   0.5, alpha)
        saved.append(s)
        weights.append(gw)
        x_in, x_in_b = x3, x3b

    last = saved[L - 1]
    dr, drb, dg, db, loss = _loss_ln_bwd(x_in, target, last["xh3"], last["rstd3"], vec("ln3_g", L - 1))
    small = [dict() for _ in range(L)]
    small[L - 1]["ln3_g"], small[L - 1]["ln3_b"] = dg, db
    bufs = {grp: [lax.empty((L, N_DEV) + shard(n, 0).shape, BF16) for n in names] for grp, names in GROUPS.items()}
    scatters = {grp: [] for grp in GROUPS}
    grad_x = None
    for l in reversed(range(L)):
        gw, s, sm = weights[l], saved[l], small[l]
        full = {}

        def scatter_start(grp):
            fulls = [full[n].reshape((N_DEV, -1) + full[n].shape[1:]) for n in GROUPS[grp]]
            sems, fulls, bufs[grp], tok = _exchange_start(_scatter_copies(len(fulls), l), f"scatter_start_{l}{grp}",
                                                          fulls, bufs[grp], [])
            scatters[grp].append((l, sems, fulls))
            return [tok]

        def ffn_bwd(dr, drb, tag, grp, xb_in, ln):
            dgp, dup = _ffn_down_bwd(drb, gw[f"ffn{tag}_down"], s[f"g{tag}"], s[f"u{tag}"])
            full[f"ffn{tag}_down"] = _mm_tn(s[f"h{tag}"], drb, 0.5, BF16)
            full[f"ffn{tag}_gate"] = _mm_tn(dgp, xb_in, 1.0, BF16)
            full[f"ffn{tag}_up"] = _mm_tn(dup, xb_in, 1.0, BF16)
            return _mm_dx([(dgp, gw[f"ffn{tag}_gate"]), (dup, gw[f"ffn{tag}_up"])], dr, alpha, ln, scatter_start(grp))

        dr, drb, sm["ln2_g"], sm["ln2_b"] = ffn_bwd(dr, drb, 2, "c", s["x2b"], (s["xh2"], s["rstd2"], vec("ln2_g", l)))
        dm = _mm_nt(drb, gw["w_out"])
        full["w_out"] = _mm_tn(s["mcat"], drb, 1.0, BF16)
        dy, dz, dwglu, sm["g_s5"], sm["g_conv"] = _mix_post_bwd(dm, s["y"], s["gl"], s["z"], gw["s5_w_glu"],
                                                                vec("g_s5", l), vec("g_conv", l))
        full["s5_w_glu"] = dwglu.astype(BF16)
        du, dbd_re, dbd_im, dcd_re, dcd_im, sm["d_ab_re"], sm["d_ab_im"], sm["s5_d"] = _s5_bwd(
            dy, s["proj"], s["sr"], s["si"], bd_re[l], bd_im[l], cd_re[l], cd_im[l], a_re[l], a_im[l], d_vec[l])
        gsz = (S5_GROUPS_PER_BLOCK, S5_P, S5_N)
        sm["d_bb_re"] = _diag_blocks(dbd_re, *gsz, True)
        sm["d_bb_im"] = _diag_blocks(dbd_im, *gsz, True)
        sm["s5_c_re"] = _diag_blocks(dcd_re, *gsz, False).reshape(G, S5_P, S5_N)
        sm["s5_c_im"] = _diag_blocks(dcd_im, *gsz, False).reshape(G, S5_P, S5_N)
        dgb, dgc, dhh, sm["conv_w"], sm["conv_b"] = _conv_bwd(dz, s["proj"], gw["conv_w"], vec("conv_b", l),
                                                              d_s5, d_conv)
        dproj = jnp.concatenate([du, dgb, dgc, dhh], axis=1)
        full["w_in"] = _mm_tn(dproj, s["x1b"], 1.0, BF16)
        dr, drb, sm["ln1_g"], sm["ln1_b"] = _mm_dx([(dproj, gw["w_in"])], dr, alpha,
                                                   (s["xh1"], s["rstd1"], vec("ln1_g", l)), scatter_start("b"))
        if l > 0:
            prev = saved[l - 1]
            dr, drb, small[l - 1]["ln3_g"], small[l - 1]["ln3_b"] = ffn_bwd(
                dr, drb, 1, "a", s["x0b"], (prev["xh3"], prev["rstd3"], vec("ln3_g", l - 1)))
        else:
            (grad_x,) = ffn_bwd(dr, drb, 1, "a", s["x0b"], None)

    def stack(key):
        return jnp.stack([small[l][key] for l in range(L)])

    d_bb_re = jnp.transpose(stack("d_bb_re").reshape(L * G, S5_P, S5_N), (1, 0, 2))
    d_bb_im = jnp.transpose(stack("d_bb_im").reshape(L * G, S5_P, S5_N), (1, 0, 2))
    g_lre, g_lim, g_ldt, g_bre, g_bim = _s5_params_bwd(
        lre, lim, ldt, b_re, b_im, stack("d_ab_re").reshape(L * G, S5_N), stack("d_ab_im").reshape(L * G, S5_N),
        d_bb_re, d_bb_im)
    part = {n: stack(n) for n in ["ln1_g", "ln1_b", "s5_c_re", "s5_c_im", "s5_d", "conv_b", "g_s5", "g_conv", "ln2_g",
                                  "ln2_b", "ln3_g", "ln3_b", "conv_w"]}
    part["s5_lam_re"], part["s5_lam_im"], part["s5_log_dt"] = g_lre, g_lim, g_ldt
    part["s5_b_re"] = jnp.transpose(g_bre, (1, 2, 0))
    part["s5_b_im"] = jnp.transpose(g_bim, (1, 2, 0))

    small_names = SMALL + ["conv_w"]
    small_shapes = [w[n].shape for n in SMALL] + [(L, CONV_W, d_conv)]
    packed = _pack([part[n] for n in small_names], N_DEV * PACK_ROWS)
    rows = packed.shape[0] // N_DEV
    (landed,) = _exchange(_scatter_copies(1, 0), "scatter_small", [packed.reshape(N_DEV, rows, LANES)],
                          [lax.empty((1, N_DEV, rows, LANES), F32)])
    mine = _reduce_parts(landed)
    (summed,) = _exchange(_gather_copies(1), "gather_small", [mine[0]], [lax.empty((N_DEV, rows, LANES), F32)])
    small_grads = dict(zip(small_names, _unpack(summed, small_shapes)))

    out = {}

    def update(name, w3, m3, v3, shape, **grad):
        res = _adamw_update(w3, m3, v3, **grad)
        out[name] = [r.reshape(shape) for r in res]

    for grp in ("c", "b", "a"):
        for l, sems, fulls in scatters[grp]:
            bufs[grp] = _exchange_wait(_scatter_copies(len(fulls), l), f"scatter_wait_{l}{grp}", sems, fulls, bufs[grp],
                                       [grad_x])
        for n, parts in zip(GROUPS[grp], bufs[grp]):
            if n in UPDATED_TRANSPOSED:
                res = _adamw_update(*(jnp.swapaxes(a[n], 1, 2) for a in (w, m, v)), parts=parts)
                out[n] = [jnp.swapaxes(r, 1, 2) for r in res]
            elif n in TRANSPOSED:
                update(n, w[n], m[n], v[n], w[n].shape, grad=jnp.swapaxes(_reduce_parts(parts), 1, 2))
            else:
                update(n, w[n], m[n], v[n], w[n].shape, parts=parts)
    cw_shape = w["conv_w"].shape
    g_cw = lax.dynamic_slice_in_dim(small_grads["conv_w"], me * cw_shape[2], cw_shape[2], axis=2)
    update("conv_w", w["conv_w"], m["conv_w"], v["conv_w"], cw_shape, grad=g_cw)
    sizes = [w[n].shape for n in SMALL]
    pw, pm, pv, pg = (_pack([d[n] for n in SMALL], PACK_ROWS)[None] for d in (w, m, v, small_grads))
    for name, res in zip(SMALL, zip(*[_unpack(r, sizes) for r in _adamw_update(pw, pm, pv, grad=pg)])):
        out[name] = list(res)

    loss = lax.psum(loss[0, 0], AXES)
    return loss, grad_x, out


def kernel(x, ffn1_gate, ffn1_up, ffn1_down, ln1_g, ln1_b, w_in, s5_lam_re, s5_lam_im, s5_log_dt, s5_b_re, s5_b_im, s5_c_re, s5_c_im, s5_d, s5_w_glu, conv_w, conv_b, g_s5, g_conv, w_out, ln2_g, ln2_b, ffn2_gate, ffn2_up, ffn2_down, ln3_g, ln3_b, loss_target, m_ffn1_gate, m_ffn1_up, m_ffn1_down, m_ln1_g, m_ln1_b, m_w_in, m_s5_lam_re, m_s5_lam_im, m_s5_log_dt, m_s5_b_re, m_s5_b_im, m_s5_c_re, m_s5_c_im, m_s5_d, m_s5_w_glu, m_conv_w, m_conv_b, m_g_s5, m_g_conv, m_w_out, m_ln2_g, m_ln2_b, m_ffn2_gate, m_ffn2_up, m_ffn2_down, m_ln3_g, m_ln3_b, v_ffn1_gate, v_ffn1_up, v_ffn1_down, v_ln1_g, v_ln1_b, v_w_in, v_s5_lam_re, v_s5_lam_im, v_s5_log_dt, v_s5_b_re, v_s5_b_im, v_s5_c_re, v_s5_c_im, v_s5_d, v_s5_w_glu, v_conv_w, v_conv_b, v_g_s5, v_g_conv, v_w_out, v_ln2_g, v_ln2_b, v_ffn2_gate, v_ffn2_up, v_ffn2_down, v_ln3_g, v_ln3_b):
    given = dict(locals())
    w = {n: given[n] for n in WEIGHTS}
    m = {n: given["m_" + n] for n in WEIGHTS}
    v = {n: given["v_" + n] for n in WEIGHTS}
    T, D = x.shape[-2:]
    loss, grad_x, out = _train_step(x.reshape(T, D), loss_target.reshape(T, D), w, m, v)
    results = [loss, grad_x.reshape(x.shape)]
    for i in range(4):
        results += [out[n][i] for n in WEIGHTS]
    return tuple(results)
```

```python
import functools
import math

import jax
import jax.numpy as jnp
from jax import lax
from jax.experimental import pallas as pl
from jax.experimental.pallas import tpu as pltpu

F32 = jnp.float32
BF16 = jnp.bfloat16
MESH = pl.DeviceIdType.MESH
AXES = ("x", "y", "c")
N_DEV = 8

S5_P = 16
S5_N = 64
CONV_W = 3
LN_EPS = 1e-5
RMS_EPS = 1e-6
ADAM_LR = 0.001
ADAM_B1 = 0.9
ADAM_B2 = 0.999
ADAM_EPS = 1e-08
ADAM_WD = 0.01
ADAM_STEP = 10

V7X_VMEM_BYTES = 64 * 1024 * 1024
VMEM_LIMIT = V7X_VMEM_BYTES * 7 // 8
LANES = 128
SUBLANES = 8
BF16_ROWS = 16
PACK_ROWS = 512
S5_GROUPS_PER_BLOCK = LANES // S5_P
S5_STATE_BLOCK = S5_GROUPS_PER_BLOCK * S5_N

HBM_SPEC = pl.BlockSpec(memory_space=pltpu.HBM)
SEM_SPEC = pl.BlockSpec(memory_space=pltpu.SEMAPHORE)
ANY_SPEC = pl.BlockSpec(memory_space=pl.ANY)


def _tile(n, pref, align):
    best = None
    d = align
    while d <= min(n, pref):
        if n % d == 0:
            best = d
        d += align
    return best if best is not None else n


def _k_rows(vmem_bytes, cols):
    return vmem_bytes // (2 * 2 * cols)


def _params(*sem):
    return pltpu.CompilerParams(dimension_semantics=sem, vmem_limit_bytes=VMEM_LIMIT)


def _dot_nn(a, b):
    return lax.dot_general(a, b, (((1,), (0,)), ((), ())), preferred_element_type=F32)


def _dot_nt(a, b):
    return lax.dot_general(a, b, (((1,), (1,)), ((), ())), preferred_element_type=F32)


def _dot_tn(a, b):
    return lax.dot_general(a, b, (((0,), (0,)), ((), ())), preferred_element_type=F32)


def _colsum(v):
    return jnp.sum(v, axis=0, keepdims=True)


def _rowmean(v):
    return jnp.mean(v, axis=-1, keepdims=True)


def _ffn_up(xb, wg, wu):
    T, D = xb.shape
    F = wg.shape[0]
    tm = _tile(T, 512, 16)
    tn = _tile(F, 512, LANES)

    def body(x_ref, wg_ref, wu_ref, g_ref, u_ref, h_ref):
        x = x_ref[...]
        g = _dot_nt(x, wg_ref[...])
        u = _dot_nt(x, wu_ref[...])
        g_ref[...] = g
        u_ref[...] = u
        h_ref[...] = (g * jax.nn.sigmoid(g) * u).astype(BF16)

    w_spec = pl.BlockSpec((tn, D), lambda j, i: (j, 0))
    o_spec = pl.BlockSpec((tm, tn), lambda j, i: (i, j))
    return pl.pallas_call(
        body, name="ffn_up", grid=(F // tn, T // tm),
        in_specs=[pl.BlockSpec((tm, D), lambda j, i: (i, 0)), w_spec, w_spec],
        out_specs=[o_spec, o_spec, o_spec],
        out_shape=[jax.ShapeDtypeStruct((T, F), F32), jax.ShapeDtypeStruct((T, F), F32),
                   jax.ShapeDtypeStruct((T, F), BF16)],
        compiler_params=_params("arbitrary", "arbitrary"),
    )(xb, wg, wu)


def _mm_res_ln(a, w, res, g, b, scale, alpha, after=()):
    T, K = a.shape
    D = w.shape[1]
    tm = _tile(T, 512, 16)
    tk = _tile(K, _k_rows(V7X_VMEM_BYTES * 3 // 16, D), LANES)
    nk = K // tk

    def body(a_ref, w_ref, res_ref, g_ref, b_ref, *rest):
        xo_ref, xb_ref, xh_ref, rstd_ref, acc_ref = rest[len(after):]
        k = pl.program_id(1)

        @pl.when(k == 0)
        def _():
            acc_ref[...] = jnp.zeros_like(acc_ref)

        acc_ref[...] += _dot_nn(a_ref[...], w_ref[...])

        @pl.when(k == nk - 1)
        def _():
            r = alpha * res_ref[...] + scale * acc_ref[...]
            xc = r - _rowmean(r)
            rstd = lax.rsqrt(_rowmean(xc * xc) + LN_EPS)
            xh = xc * rstd
            xo = xh * g_ref[...] + b_ref[...]
            xo_ref[...] = xo
            xb_ref[...] = xo.astype(BF16)
            xh_ref[...] = xh
            rstd_ref[...] = rstd

    row = pl.BlockSpec((tm, D), lambda i, k: (i, 0))
    vec = pl.BlockSpec((1, D), lambda i, k: (0, 0))
    return pl.pallas_call(
        body, name="mm_res_ln", grid=(T // tm, nk),
        in_specs=[pl.BlockSpec((tm, tk), lambda i, k: (i, k)), pl.BlockSpec((tk, D), lambda i, k: (k, 0)),
                  row, vec, vec] + [ANY_SPEC] * len(after),
        out_specs=[row, row, row, pl.BlockSpec((tm, 1), lambda i, k: (i, 0))],
        out_shape=[jax.ShapeDtypeStruct((T, D), F32), jax.ShapeDtypeStruct((T, D), BF16),
                   jax.ShapeDtypeStruct((T, D), F32), jax.ShapeDtypeStruct((T, 1), F32)],
        scratch_shapes=[pltpu.VMEM((tm, D), F32)],
        compiler_params=_params("arbitrary", "arbitrary"),
    )(a, w, res, g, b, *after)


def _mm_nt(a, w):
    M, K = a.shape
    N = w.shape[0]
    tm = _tile(M, 512, 16)
    tn = _tile(N, 512, LANES)

    def body(a_ref, w_ref, o_ref):
        o_ref[...] = _dot_nt(a_ref[...], w_ref[...])

    return pl.pallas_call(
        body, name="mm_nt", grid=(N // tn, M // tm),
        in_specs=[pl.BlockSpec((tm, K), lambda j, i: (i, 0)), pl.BlockSpec((tn, K), lambda j, i: (j, 0))],
        out_specs=pl.BlockSpec((tm, tn), lambda j, i: (i, j)),
        out_shape=jax.ShapeDtypeStruct((M, N), F32),
        compiler_params=_params("arbitrary", "arbitrary"),
    )(a, w)


def _mm_tn(a, b, scale, out_dtype):
    T, M = a.shape
    N = b.shape[1]
    tm = _tile(M, 512, LANES)
    tn = _tile(N, 1024, LANES)

    def body(a_ref, b_ref, o_ref):
        o_ref[...] = (scale * _dot_tn(a_ref[...], b_ref[...])).astype(out_dtype)

    return pl.pallas_call(
        body, name="mm_tn", grid=(M // tm, N // tn),
        in_specs=[pl.BlockSpec((T, tm), lambda i, j: (0, i)), pl.BlockSpec((T, tn), lambda i, j: (0, j))],
        out_specs=pl.BlockSpec((tm, tn), lambda i, j: (i, j)),
        out_shape=jax.ShapeDtypeStruct((M, N), out_dtype),
        compiler_params=_params("arbitrary", "arbitrary"),
    )(a, b)


def _ln_bwd(dy, xh, rstd, g):
    dxh = dy * g
    dr = rstd * (dxh - _rowmean(dxh) - xh * _rowmean(dxh * xh))
    return dr, _colsum(dy * xh), _colsum(dy)


def _loss_ln_bwd(y, target, xh, rstd, g):
    T, D = y.shape
    tm = _tile(T, 256, 16)

    def body(y_ref, t_ref, xh_ref, rstd_ref, g_ref, dr_ref, drb_ref, dg_ref, db_ref, loss_ref):
        i = pl.program_id(0)

        @pl.when(i == 0)
        def _():
            dg_ref[...] = jnp.zeros_like(dg_ref)
            db_ref[...] = jnp.zeros_like(db_ref)
            loss_ref[...] = jnp.zeros_like(loss_ref)

        err = y_ref[...] - t_ref[...]
        loss_ref[...] += (0.5 / D) * _colsum(jnp.sum(err * err, axis=1, keepdims=True))
        dr, dg, db = _ln_bwd(err * (1.0 / D), xh_ref[...], rstd_ref[...], g_ref[...])
        dr_ref[...] = dr
        drb_ref[...] = dr.astype(BF16)
        dg_ref[...] += dg
        db_ref[...] += db

    row = pl.BlockSpec((tm, D), lambda i: (i, 0))
    vec = pl.BlockSpec((1, D), lambda i: (0, 0))
    return pl.pallas_call(
        body, name="loss_ln_bwd", grid=(T // tm,),
        in_specs=[row, row, row, pl.BlockSpec((tm, 1), lambda i: (i, 0)), vec],
        out_specs=[row, row, vec, vec, pl.BlockSpec((1, 1), lambda i: (0, 0))],
        out_shape=[jax.ShapeDtypeStruct((T, D), F32), jax.ShapeDtypeStruct((T, D), BF16),
                   jax.ShapeDtypeStruct((1, D), F32), jax.ShapeDtypeStruct((1, D), F32),
                   jax.ShapeDtypeStruct((1, 1), F32)],
        compiler_params=_params("arbitrary"),
    )(y, target, xh, rstd, g)


def _ffn_down_bwd(drb, wd, gpre, upre):
    T, D = drb.shape
    F = wd.shape[0]
    tm = _tile(T, 512, 16)
    tn = _tile(F, 512, LANES)

    def body(dr_ref, wd_ref, g_ref, u_ref, dg_ref, du_ref):
        dh = 0.5 * _dot_nt(dr_ref[...], wd_ref[...])
        g = g_ref[...]
        u = u_ref[...]
        sg = jax.nn.sigmoid(g)
        du_ref[...] = (dh * (g * sg)).astype(BF16)
        dg_ref[...] = (dh * u * (sg * (1.0 + g * (1.0 - sg)))).astype(BF16)

    t_spec = pl.BlockSpec((tm, tn), lambda j, i: (i, j))
    return pl.pallas_call(
        body, name="ffn_down_bwd", grid=(F // tn, T // tm),
        in_specs=[pl.BlockSpec((tm, D), lambda j, i: (i, 0)), pl.BlockSpec((tn, D), lambda j, i: (j, 0)),
                  t_spec, t_spec],
        out_specs=[t_spec, t_spec],
        out_shape=[jax.ShapeDtypeStruct((T, F), BF16), jax.ShapeDtypeStruct((T, F), BF16)],
        compiler_params=_params("arbitrary", "arbitrary"),
    )(drb, wd, gpre, upre)


def _mm_dx(pairs, res, alpha, ln=None, after=()):
    T, K = pairs[0][0].shape
    D = res.shape[1]
    n = len(pairs)
    n_after = len(after)
    tm = _tile(T, 256, 16)
    tk = _tile(K, _k_rows(V7X_VMEM_BYTES * 3 // 8, n * D), LANES)
    nk = K // tk
    with_ln = ln is not None

    def body(*refs):
        a_refs = refs[0:2 * n:2]
        w_refs = refs[1:2 * n:2]
        refs = refs[2 * n + n_after:]
        res_ref = refs[0]
        acc_ref = refs[-1]
        i = pl.program_id(0)
        k = pl.program_id(1)

        @pl.when(k == 0)
        def _():
            acc_ref[...] = jnp.zeros_like(acc_ref)

        for a_ref, w_ref in zip(a_refs, w_refs):
            acc_ref[...] += _dot_nn(a_ref[...], w_ref[...])

        if with_ln:
            xh_ref, rstd_ref, g_ref, dr_ref, drb_ref, dg_ref, db_ref = refs[1:8]

            @pl.when((i == 0) & (k == 0))
            def _():
                dg_ref[...] = jnp.zeros_like(dg_ref)
                db_ref[...] = jnp.zeros_like(db_ref)

            @pl.when(k == nk - 1)
            def _():
                dx = alpha * res_ref[...] + acc_ref[...]
                dr, dg, db = _ln_bwd(dx, xh_ref[...], rstd_ref[...], g_ref[...])
                dr_ref[...] = dr
                drb_ref[...] = dr.astype(BF16)
                dg_ref[...] += dg
                db_ref[...] += db
        else:
            dx_ref = refs[1]

            @pl.when(k == nk - 1)
            def _():
                dx_ref[...] = alpha * res_ref[...] + acc_ref[...]

    row = pl.BlockSpec((tm, D), lambda i, k: (i, 0))
    vec = pl.BlockSpec((1, D), lambda i, k: (0, 0))
    in_specs, operands = [], []
    for a, w in pairs:
        in_specs += [pl.BlockSpec((tm, tk), lambda i, k: (i, k)), pl.BlockSpec((tk, D), lambda i, k: (k, 0))]
        operands += [a, w]
    in_specs += [ANY_SPEC] * n_after
    operands += list(after)
    in_specs.append(row)
    operands.append(res)
    if with_ln:
        in_specs += [row, pl.BlockSpec((tm, 1), lambda i, k: (i, 0)), vec]
        operands += list(ln)
        out_specs = [row, row, vec, vec]
        out_shape = [jax.ShapeDtypeStruct((T, D), F32), jax.ShapeDtypeStruct((T, D), BF16),
                     jax.ShapeDtypeStruct((1, D), F32), jax.ShapeDtypeStruct((1, D), F32)]
    else:
        out_specs = [row]
        out_shape = [jax.ShapeDtypeStruct((T, D), F32)]
    return pl.pallas_call(
        body, name="mm_dx_ln_bwd" if with_ln else "mm_dx", grid=(T // tm, nk),
        in_specs=in_specs, out_specs=out_specs, out_shape=out_shape,
        scratch_shapes=[pltpu.VMEM((tm, D), F32)],
        compiler_params=_params("arbitrary", "arbitrary"),
    )(*operands)


def _s5_discretize(lre, lim, ldt, br, bi):
    dt = jnp.exp(ldt)
    mag = jnp.exp(lre * dt)
    ang = lim * dt
    ar = mag * jnp.cos(ang)
    ai = mag * jnp.sin(ang)
    den = lre * lre + lim * lim
    nr = ar - 1.0
    qr = (nr * lre + ai * lim) / den
    qi = (ai * lre - nr * lim) / den
    bbr = qr[None] * br - qi[None] * bi
    bbi = qr[None] * bi + qi[None] * br
    return ar, ai, bbr, bbi


def _s5_params_fwd(lre, lim, ldt, br, bi):
    def body(lre_ref, lim_ref, ldt_ref, br_ref, bi_ref, ar_ref, ai_ref, bbr_ref, bbi_ref):
        ar, ai, bbr, bbi = _s5_discretize(lre_ref[...], lim_ref[...], ldt_ref[...], br_ref[...], bi_ref[...])
        ar_ref[...] = ar
        ai_ref[...] = ai
        bbr_ref[...] = bbr
        bbi_ref[...] = bbi

    sds = jax.ShapeDtypeStruct
    return pl.pallas_call(
        body, name="s5_params_fwd",
        out_shape=[sds(lre.shape, F32), sds(lre.shape, F32), sds(br.shape, F32), sds(br.shape, F32)],
        compiler_params=pltpu.CompilerParams(vmem_limit_bytes=VMEM_LIMIT),
    )(lre, lim, ldt, br, bi)


def _s5_params_bwd(lre, lim, ldt, br, bi, dar, dai, dbbr, dbbi):
    def body(lre_ref, lim_ref, ldt_ref, br_ref, bi_ref, dar_ref, dai_ref, dbbr_ref, dbbi_ref,
             o_lre, o_lim, o_ldt, o_br, o_bi):
        _, vjp = jax.vjp(_s5_discretize, lre_ref[...], lim_ref[...], ldt_ref[...], br_ref[...], bi_ref[...])
        g = vjp((dar_ref[...], dai_ref[...], dbbr_ref[...], dbbi_ref[...]))
        o_lre[...] = g[0]
        o_lim[...] = g[1]
        o_ldt[...] = g[2]
        o_br[...] = g[3]
        o_bi[...] = g[4]

    sds = jax.ShapeDtypeStruct
    return pl.pallas_call(
        body, name="s5_params_bwd",
        out_shape=[sds(lre.shape, F32), sds(lre.shape, F32), sds(ldt.shape, F32), sds(br.shape, F32),
                   sds(br.shape, F32)],
        compiler_params=pltpu.CompilerParams(vmem_limit_bytes=VMEM_LIMIT),
    )(lre, lim, ldt, br, bi, dar, dai, dbbr, dbbi)


def _s5_fwd(proj, bdr, bdi, cdr, cdi, ar, ai, dvec):
    T = proj.shape[0]
    GB, UB, SB = bdr.shape
    tc = _tile(T, 256, SUBLANES)

    def body(u_ref, bdr_ref, bdi_ref, cdr_ref, cdi_ref, ar_ref, ai_ref, d_ref, y_ref, sr_ref, si_ref,
             cr_ref, ci_ref):
        @pl.when(pl.program_id(1) == 0)
        def _():
            cr_ref[...] = jnp.zeros_like(cr_ref)
            ci_ref[...] = jnp.zeros_like(ci_ref)

        u = u_ref[...]
        ub = u.astype(BF16)
        sr_ref[...] = _dot_nn(ub, bdr_ref[...])
        si_ref[...] = _dot_nn(ub, bdi_ref[...])
        a_re = ar_ref[...]
        a_im = ai_ref[...]

        def step(t, carry):
            p_re, p_im = carry
            row = pl.ds(t, 1)
            n_re = a_re * p_re - a_im * p_im + sr_ref[row, :]
            n_im = a_re * p_im + a_im * p_re + si_ref[row, :]
            sr_ref[row, :] = n_re
            si_ref[row, :] = n_im
            return n_re, n_im

        p_re, p_im = lax.fori_loop(0, tc, step, (cr_ref[...], ci_ref[...]), unroll=8)
        cr_ref[...] = p_re
        ci_ref[...] = p_im
        y_ref[...] = (_dot_nn(sr_ref[...].astype(BF16), cdr_ref[...])
                      - _dot_nn(si_ref[...].astype(BF16), cdi_ref[...]) + d_ref[...] * u)

    return pl.pallas_call(
        body, name="s5_fwd", grid=(GB, T // tc),
        in_specs=[pl.BlockSpec((tc, UB), lambda j, t: (t, j)),
                  pl.BlockSpec((None, UB, SB), lambda j, t: (j, 0, 0)),
                  pl.BlockSpec((None, UB, SB), lambda j, t: (j, 0, 0)),
                  pl.BlockSpec((None, SB, UB), lambda j, t: (j, 0, 0)),
                  pl.BlockSpec((None, SB, UB), lambda j, t: (j, 0, 0)),
                  pl.BlockSpec((1, SB), lambda j, t: (0, j)),
                  pl.BlockSpec((1, SB), lambda j, t: (0, j)),
                  pl.BlockSpec((1, UB), lambda j, t: (0, j))],
        out_specs=[pl.BlockSpec((tc, UB), lambda j, t: (t, j)),
                   pl.BlockSpec((tc, SB), lambda j, t: (t, j)),
                   pl.BlockSpec((tc, SB), lambda j, t: (t, j))],
        out_shape=[jax.ShapeDtypeStruct((T, GB * UB), F32), jax.ShapeDtypeStruct((T, GB * SB), F32),
                   jax.ShapeDtypeStruct((T, GB * SB), F32)],
        scratch_shapes=[pltpu.VMEM((1, SB), F32), pltpu.VMEM((1, SB), F32)],
        compiler_params=_params("arbitrary", "arbitrary"),
    )(proj, bdr, bdi, cdr, cdi, ar, ai, dvec)


def _s5_bwd(dy, proj, sr, si, bdr, bdi, cdr, cdi, ar, ai, dvec):
    T = dy.shape[0]
    GB, UB, SB = bdr.shape
    tc = _tile(T, 256, SUBLANES)
    nt = T // tc
    halo_blocks = tc // SUBLANES

    def body(dy_ref, u_ref, sr_ref, si_ref, hr_ref, hi_ref, bdr_ref, bdi_ref, cdr_ref, cdi_ref, ar_ref, ai_ref,
             d_ref, du_ref, dbdr_ref, dbdi_ref, dcdr_ref, dcdi_ref, dar_ref, dai_ref, dd_ref,
             gr_ref, gi_ref, pr_ref, pi_ref, cr_ref, ci_ref):
        step_no = pl.program_id(1)
        first_chunk = step_no == nt - 1

        @pl.when(step_no == 0)
        def _():
            for ref in (cr_ref, ci_ref, dbdr_ref, dbdi_ref, dcdr_ref, dcdi_ref, dar_ref, dai_ref, dd_ref):
                ref[...] = jnp.zeros_like(ref)

        dy = dy_ref[...]
        dyb = dy.astype(BF16)
        u = u_ref[...]
        s_re = sr_ref[...]
        s_im = si_ref[...]
        gr_ref[...] = _dot_nt(dyb, cdr_ref[...])
        gi_ref[...] = -_dot_nt(dyb, cdi_ref[...])
        dcdr_ref[...] += _dot_tn(s_re.astype(BF16), dyb)
        dcdi_ref[...] -= _dot_tn(s_im.astype(BF16), dyb)
        keep = jnp.where(first_chunk, 0.0, 1.0)
        pr_ref[0:SUBLANES, :] = hr_ref[...] * keep
        pi_ref[0:SUBLANES, :] = hi_ref[...] * keep
        pr_ref[SUBLANES:, :] = s_re
        pi_ref[SUBLANES:, :] = s_im
        a_re = ar_ref[...]
        a_im = ai_ref[...]

        def step(n, carry):
            c_re, c_im, acc_re, acc_im = carry
            t = tc - 1 - n
            row = pl.ds(t, 1)
            g_re = gr_ref[row, :] + a_re * c_re + a_im * c_im
            g_im = gi_ref[row, :] + a_re * c_im - a_im * c_re
            gr_ref[row, :] = g_re
            gi_ref[row, :] = g_im
            prev = pl.ds(t + SUBLANES - 1, 1)
            p_re = pr_ref[prev, :]
            p_im = pi_ref[prev, :]
            acc_re = acc_re + p_re * g_re + p_im * g_im
            acc_im = acc_im + p_re * g_im - p_im * g_re
            return g_re, g_im, acc_re, acc_im

        zero = jnp.zeros((1, SB), F32)
        c_re, c_im, acc_re, acc_im = lax.fori_loop(0, tc, step, (cr_ref[...], ci_ref[...], zero, zero), unroll=8)
        cr_ref[...] = c_re
        ci_ref[...] = c_im
        dar_ref[...] += acc_re
        dai_ref[...] += acc_im
        gsr = gr_ref[...].astype(BF16)
        gsi = gi_ref[...].astype(BF16)
        ub = u.astype(BF16)
        dbdr_ref[...] += _dot_tn(ub, gsr)
        dbdi_ref[...] += _dot_tn(ub, gsi)
        du_ref[...] = (_dot_nt(gsr, bdr_ref[...]) + _dot_nt(gsi, bdi_ref[...]) + d_ref[...] * dy).astype(BF16)
        dd_ref[...] += _colsum(dy * u)

    def rev(t):
        return nt - 1 - t

    def halo(j, t):
        return (jnp.maximum(rev(t) * halo_blocks - 1, 0), j)

    ublk = pl.BlockSpec((tc, UB), lambda j, t: (rev(t), j))
    sblk = pl.BlockSpec((tc, SB), lambda j, t: (rev(t), j))
    bd_spec = pl.BlockSpec((None, UB, SB), lambda j, t: (j, 0, 0))
    cd_spec = pl.BlockSpec((None, SB, UB), lambda j, t: (j, 0, 0))
    svec = pl.BlockSpec((1, SB), lambda j, t: (0, j))
    uvec = pl.BlockSpec((1, UB), lambda j, t: (0, j))
    sds = jax.ShapeDtypeStruct
    return pl.pallas_call(
        body, name="s5_bwd", grid=(GB, nt),
        in_specs=[ublk, ublk, sblk, sblk, pl.BlockSpec((SUBLANES, SB), halo), pl.BlockSpec((SUBLANES, SB), halo),
                  bd_spec, bd_spec, cd_spec, cd_spec, svec, svec, uvec],
        out_specs=[ublk, bd_spec, bd_spec, cd_spec, cd_spec, svec, svec, uvec],
        out_shape=[sds((T, GB * UB), BF16), sds((GB, UB, SB), F32), sds((GB, UB, SB), F32),
                   sds((GB, SB, UB), F32), sds((GB, SB, UB), F32), sds((1, GB * SB), F32),
                   sds((1, GB * SB), F32), sds((1, GB * UB), F32)],
        scratch_shapes=[pltpu.VMEM((tc, SB), F32), pltpu.VMEM((tc, SB), F32),
                        pltpu.VMEM((tc + SUBLANES, SB), F32), pltpu.VMEM((tc + SUBLANES, SB), F32),
                        pltpu.VMEM((1, SB), F32), pltpu.VMEM((1, SB), F32)],
        compiler_params=_params("arbitrary", "arbitrary"),
    )(dy, proj, sr, si, sr, si, bdr, bdi, cdr, cdi, ar, ai, dvec)


def _shift_down(v, k):
    rows = lax.broadcasted_iota(jnp.int32, v.shape, 0)
    return jnp.where(rows >= k, pltpu.roll(v, k, 0), 0.0)


def _shift_up(v, k):
    n = v.shape[0]
    rows = lax.broadcasted_iota(jnp.int32, v.shape, 0)
    return jnp.where(rows < n - k, pltpu.roll(v, n - k, 0), 0.0)


def _conv_specs(T, cb, n_s5_blocks, n_conv_blocks):
    gb = pl.BlockSpec((T, cb), lambda j: (0, n_s5_blocks + j))
    gc = pl.BlockSpec((T, cb), lambda j: (0, n_s5_blocks + n_conv_blocks + j))
    hh = pl.BlockSpec((T, cb), lambda j: (0, n_s5_blocks + 2 * n_conv_blocks + j))
    return gb, gc, hh


def _conv_fwd(proj, cw, cbias, d_s5, d_conv):
    T = proj.shape[0]
    cb = _tile(d_conv, 256, LANES)

    def body(gb_ref, gc_ref, hh_ref, w_ref, b_ref, z_ref):
        v = gc_ref[...] * hh_ref[...]
        w = w_ref[...]
        cv = b_ref[...] + w[0:1, :] * _shift_down(v, 2) + w[1:2, :] * _shift_down(v, 1) + w[2:3, :] * v
        z_ref[...] = gb_ref[...] * cv

    gb, gc, hh = _conv_specs(T, cb, d_s5 // cb, d_conv // cb)
    col = pl.BlockSpec((T, cb), lambda j: (0, j))
    return pl.pallas_call(
        body, name="conv_fwd", grid=(d_conv // cb,),
        in_specs=[gb, gc, hh, pl.BlockSpec((CONV_W, cb), lambda j: (0, j)), pl.BlockSpec((1, cb), lambda j: (0, j))],
        out_specs=col, out_shape=jax.ShapeDtypeStruct((T, d_conv), F32),
        compiler_params=_params("arbitrary"),
    )(proj, proj, proj, cw, cbias)


def _conv_bwd(dz, proj, cw, cbias, d_s5, d_conv):
    T = proj.shape[0]
    cb = _tile(d_conv, 256, LANES)

    def body(dz_ref, gb_ref, gc_ref, hh_ref, w_ref, b_ref, dgb_ref, dgc_ref, dhh_ref, dw_ref, db_ref):
        gc = gc_ref[...]
        hh = hh_ref[...]
        dz = dz_ref[...]
        w = w_ref[...]
        v = gc * hh
        v1 = _shift_down(v, 1)
        v2 = _shift_down(v, 2)
        cv = b_ref[...] + w[0:1, :] * v2 + w[1:2, :] * v1 + w[2:3, :] * v
        dgb_ref[...] = (dz * cv).astype(BF16)
        dcv = dz * gb_ref[...]
        dv = w[2:3, :] * dcv + w[1:2, :] * _shift_up(dcv, 1) + w[0:1, :] * _shift_up(dcv, 2)
        dgc_ref[...] = (dv * hh).astype(BF16)
        dhh_ref[...] = (dv * gc).astype(BF16)
        dw_ref[0:1, :] = _colsum(dcv * v2)
        dw_ref[1:2, :] = _colsum(dcv * v1)
        dw_ref[2:3, :] = _colsum(dcv * v)
        db_ref[...] = _colsum(dcv)

    gb, gc, hh = _conv_specs(T, cb, d_s5 // cb, d_conv // cb)
    col = pl.BlockSpec((T, cb), lambda j: (0, j))
    wspec = pl.BlockSpec((CONV_W, cb), lambda j: (0, j))
    bspec = pl.BlockSpec((1, cb), lambda j: (0, j))
    sds = jax.ShapeDtypeStruct
    return pl.pallas_call(
        body, name="conv_bwd", grid=(d_conv // cb,),
        in_specs=[col, gb, gc, hh, wspec, bspec],
        out_specs=[col, col, col, wspec, bspec],
        out_shape=[sds((T, d_conv), BF16), sds((T, d_conv), BF16), sds((T, d_conv), BF16),
                   sds((CONV_W, d_conv), F32), sds((1, d_conv), F32)],
        compiler_params=_params("arbitrary"),
    )(dz, proj, proj, proj, cw, cbias)


def _rms(v, g):
    rstd = lax.rsqrt(_rowmean(v * v) + RMS_EPS)
    return v * rstd * g, rstd


def _rms_bwd(dyn, v, rstd, g):
    w = dyn * g
    return rstd * w - v * (rstd * rstd * rstd) * _rowmean(w * v), _colsum(dyn * v * rstd)


def _mix_post(y, z, wglu, g_s5, g_conv):
    T, C = y.shape
    tm = _tile(T, 256, 16)

    def body(y_ref, z_ref, w_ref, gs_ref, gc_ref, m_ref, gl_ref):
        ge = jax.nn.gelu(y_ref[...])
        gl = _dot_nn(ge.astype(BF16), w_ref[...])
        gl_ref[...] = gl
        yn, _ = _rms(ge * jax.nn.sigmoid(gl), gs_ref[...])
        zn, _ = _rms(z_ref[...], gc_ref[...])
        m_ref[:, 0:C] = yn.astype(BF16)
        m_ref[:, C:2 * C] = zn.astype(BF16)

    row = pl.BlockSpec((tm, C), lambda i: (i, 0))
    vec = pl.BlockSpec((1, C), lambda i: (0, 0))
    return pl.pallas_call(
        body, name="mix_post", grid=(T // tm,),
        in_specs=[row, row, pl.BlockSpec((C, C), lambda i: (0, 0)), vec, vec],
        out_specs=[pl.BlockSpec((tm, 2 * C), lambda i: (i, 0)), row],
        out_shape=[jax.ShapeDtypeStruct((T, 2 * C), BF16), jax.ShapeDtypeStruct((T, C), F32)],
        compiler_params=_params("arbitrary"),
    )(y, z, wglu, g_s5, g_conv)


def _mix_post_bwd(dm, y, gl, z, wglu, g_s5, g_conv):
    T, C = y.shape
    tm = _tile(T, 256, 16)

    def body(dm_ref, y_ref, gl_ref, z_ref, w_ref, gs_ref, gc_ref, dy_ref, dz_ref, dw_ref, dgs_ref, dgc_ref):
        @pl.when(pl.program_id(0) == 0)
        def _():
            dw_ref[...] = jnp.zeros_like(dw_ref)
            dgs_ref[...] = jnp.zeros_like(dgs_ref)
            dgc_ref[...] = jnp.zeros_like(dgc_ref)

        yv = y_ref[...]
        ge, gelu_vjp = jax.vjp(jax.nn.gelu, yv)
        gl = gl_ref[...]
        sg = jax.nn.sigmoid(gl)
        y2 = ge * sg
        _, rstd_y = _rms(y2, gs_ref[...])
        dy2, dgs = _rms_bwd(dm_ref[:, 0:C], y2, rstd_y, gs_ref[...])
        dgs_ref[...] += dgs
        dgl = (dy2 * ge * sg * (1.0 - sg)).astype(BF16)
        dge = dy2 * sg + _dot_nt(dgl, w_ref[...])
        dw_ref[...] += _dot_tn(ge.astype(BF16), dgl)
        dy_ref[...] = gelu_vjp(dge)[0]
        zv = z_ref[...]
        _, rstd_z = _rms(zv, gc_ref[...])
        dz, dgc = _rms_bwd(dm_ref[:, C:2 * C], zv, rstd_z, gc_ref[...])
        dz_ref[...] = dz
        dgc_ref[...] += dgc

    row = pl.BlockSpec((tm, C), lambda i: (i, 0))
    vec = pl.BlockSpec((1, C), lambda i: (0, 0))
    full = pl.BlockSpec((C, C), lambda i: (0, 0))
    sds = jax.ShapeDtypeStruct
    return pl.pallas_call(
        body, name="mix_post_bwd", grid=(T // tm,),
        in_specs=[pl.BlockSpec((tm, 2 * C), lambda i: (i, 0)), row, row, row, full, vec, vec],
        out_specs=[row, row, full, vec, vec],
        out_shape=[sds((T, C), F32), sds((T, C), F32), sds((C, C), F32), sds((1, C), F32), sds((1, C), F32)],
        compiler_params=_params("arbitrary"),
    )(dm, y, gl, z, wglu, g_s5, g_conv)


def _adamw(w, g, m, v):
    m = ADAM_B1 * m + (1.0 - ADAM_B1) * g
    v = ADAM_B2 * v + (1.0 - ADAM_B2) * (g * g)
    m_hat = m / (1.0 - ADAM_B1 ** ADAM_STEP)
    v_hat = v / (1.0 - ADAM_B2 ** ADAM_STEP)
    return -ADAM_LR * (m_hat / (jnp.sqrt(v_hat) + ADAM_EPS) + ADAM_WD * w), m, v


def _sum_parts(p_ref):
    total = p_ref[0].astype(F32)
    for d in range(1, N_DEV):
        total = total + p_ref[d].astype(F32)
    return total


def _row_tile(R, C, n_streams):
    budget = VMEM_LIMIT // 3 // (n_streams * C * 4)
    return _tile(R, max(BF16_ROWS, budget), BF16_ROWS)


def _reduce_parts(parts):
    L, _, R, C = parts.shape
    tr = _row_tile(R, C, N_DEV + 1)

    def body(p_ref, o_ref):
        o_ref[...] = _sum_parts(p_ref)

    return pl.pallas_call(
        body, name="reduce_parts", grid=(L, R // tr),
        in_specs=[pl.BlockSpec((None, N_DEV, tr, C), lambda l, i: (l, 0, i, 0))],
        out_specs=pl.BlockSpec((None, tr, C), lambda l, i: (l, i, 0)),
        out_shape=jax.ShapeDtypeStruct((L, R, C), F32),
        compiler_params=_params("arbitrary", "arbitrary"),
    )(parts)


def _adamw_update(w, m, v, grad=None, parts=None):
    L, R, C = w.shape
    from_parts = parts is not None
    tr = _row_tile(R, C, (N_DEV if from_parts else 1) + 7)

    def body(g_in_ref, w_ref, m_ref, v_ref, g_ref, d_ref, nm_ref, nv_ref):
        g = _sum_parts(g_in_ref) if from_parts else g_in_ref[...]
        delta, nm, nv = _adamw(w_ref[...], g, m_ref[...], v_ref[...])
        g_ref[...] = g
        d_ref[...] = delta
        nm_ref[...] = nm
        nv_ref[...] = nv

    blk = pl.BlockSpec((None, tr, C), lambda l, i: (l, i, 0))
    g_spec = pl.BlockSpec((None, N_DEV, tr, C), lambda l, i: (l, 0, i, 0)) if from_parts else blk
    out = jax.ShapeDtypeStruct((L, R, C), F32)
    return pl.pallas_call(
        body, name="adamw_parts" if from_parts else "adamw", grid=(L, R // tr),
        in_specs=[g_spec, blk, blk, blk], out_specs=[blk, blk, blk, blk], out_shape=[out, out, out, out],
        compiler_params=_params("arbitrary", "arbitrary"),
    )(parts if from_parts else grad, w, m, v)


def _me():
    x, y, c = (lax.axis_index(a) for a in AXES)
    return x, y, c, 4 * x + 2 * y + c


def _peer(rel):
    x, y, c, _ = _me()
    px = 1 - x if rel & 4 else x
    py = 1 - y if rel & 2 else y
    pc = 1 - c if rel & 1 else c
    return (px, py, pc), 4 * px + 2 * py + pc


DATAFLOW = pltpu.SideEffectType.DATAFLOW_SIDE_EFFECTING


def _remote_copy(src, dst, sems):
    return functools.partial(pltpu.make_async_remote_copy, src_ref=src, dst_ref=dst, **sems)


ALL_PEERS = tuple(range(1, N_DEV))
SIBLING = 1
OTHER_CHIPS = (2, 4, 6)
SIBLING_AND_OTHER_CHIPS = (SIBLING,) + OTHER_CHIPS


def _gather_copies(n, rels=ALL_PEERS):
    def copies(srcs, lands, send_sems, recv_sems, local_sems):
        me = _me()[3]
        local, remote = [], []
        for k in range(n):
            local.append(functools.partial(pltpu.make_async_copy, srcs[k], lands[k].at[me], local_sems.at[k]))
            for rel in rels:
                dev, blk = _peer(rel)
                sems = dict(send_sem=send_sems.at[_sem_index(k, rel)], recv_sem=recv_sems.at[_sem_index(k, rel)],
                            device_id=dev, device_id_type=MESH)
                remote.append((_remote_copy(srcs[k], lands[k].at[me], sems), _remote_copy(srcs[k], lands[k].at[blk], sems)))
        return local, remote

    copies.n_arrays = n
    return copies


def _forward_copies(n):
    def copies(srcs, lands, send_sems, recv_sems, local_sems):
        sibling = _peer(SIBLING)[0]
        remote = []
        for k in range(n):
            for rel in OTHER_CHIPS:
                have = _peer(rel)[1]
                comes = _peer(rel | SIBLING)[1]
                sems = dict(send_sem=send_sems.at[_sem_index(k, rel)], recv_sem=recv_sems.at[_sem_index(k, rel)],
                            device_id=sibling, device_id_type=MESH)
                remote.append((_remote_copy(lands[k].at[have], lands[k].at[have], sems),
                               _remote_copy(lands[k].at[have], lands[k].at[comes], sems)))
        return [], remote

    copies.n_arrays = n
    return copies


def _scatter_copies(n, layer):
    def copies(srcs, lands, send_sems, recv_sems, local_sems):
        me = _me()[3]
        local, remote = [], []
        for k in range(n):
            local.append(functools.partial(pltpu.make_async_copy, srcs[k].at[me], lands[k].at[layer, me],
                                           local_sems.at[k]))
            for rel in range(1, N_DEV):
                dev, blk = _peer(rel)
                sems = dict(send_sem=send_sems.at[_sem_index(k, rel)], recv_sem=recv_sems.at[_sem_index(k, rel)],
                            device_id=dev, device_id_type=MESH)
                remote.append((_remote_copy(srcs[k].at[blk], lands[k].at[layer, me], sems),
                               _remote_copy(srcs[k].at[blk], lands[k].at[layer, blk], sems)))
        return local, remote

    copies.n_arrays = n
    return copies


def _sem_shapes(n):
    return [pltpu.SemaphoreType.DMA((n * (N_DEV - 1),)), pltpu.SemaphoreType.DMA((n * (N_DEV - 1),)),
            pltpu.SemaphoreType.DMA((n,))]


def _sem_index(k, rel):
    return k * (N_DEV - 1) + rel - 1


def _exchange(copies, name, srcs, lands):
    n_src, n_land = len(srcs), len(lands)

    def body(*refs):
        src_refs = refs[:n_src]
        land_refs = refs[n_src + n_land:n_src + 2 * n_land]
        local, remote = copies(src_refs, land_refs, *refs[n_src + 2 * n_land:])
        local = [cp() for cp in local]
        sends = [send() for send, _ in remote]
        for cp in local + sends:
            cp.start()
        for send, (_, landing) in zip(sends, remote):
            send.wait_send()
            landing().wait_recv()
        for cp in local:
            cp.wait()

    return pl.pallas_call(
        body, name=name, in_specs=[HBM_SPEC] * (n_src + n_land), out_specs=[HBM_SPEC] * n_land,
        out_shape=[jax.ShapeDtypeStruct(b.shape, b.dtype) for b in lands],
        scratch_shapes=_sem_shapes(copies.n_arrays),
        input_output_aliases={n_src + k: k for k in range(n_land)},
        compiler_params=pltpu.CompilerParams(has_side_effects=True),
    )(*srcs, *lands)


def _hbm(arrays):
    return [pltpu.with_memory_space_constraint(a, pltpu.HBM) for a in arrays]


def _exchange_start(copies, name, srcs, lands, after):
    n_src, n_land, n_after = len(srcs), len(lands), len(after)
    n_data = n_src + n_land

    def body(*refs):
        outs = refs[n_data + n_after:]
        local, remote = copies(refs[:n_src], refs[n_src:n_data], *outs[:3])
        for cp in local:
            cp().start()
        for send, _ in remote:
            send().start()
        outs[-1][...] = jnp.zeros_like(outs[-1])

    res = pl.pallas_call(
        body, name=name, in_specs=[HBM_SPEC] * n_data + [ANY_SPEC] * n_after,
        out_specs=[SEM_SPEC] * 3 + [HBM_SPEC] * n_data + [pl.BlockSpec(memory_space=pltpu.VMEM)],
        out_shape=_sem_shapes(copies.n_arrays) + [pltpu.HBM(a.shape, a.dtype) for a in list(srcs) + list(lands)]
        + [jax.ShapeDtypeStruct((SUBLANES, LANES), F32)],
        input_output_aliases={k: 3 + k for k in range(n_data)},
        compiler_params=pltpu.CompilerParams(has_side_effects=DATAFLOW),
    )(*_hbm(list(srcs) + list(lands)), *after)
    return res[:3], res[3:3 + n_src], res[3 + n_src:3 + n_data], res[-1]


def _exchange_wait(copies, name, sems, srcs, lands, after):
    n_src, n_land, n_after = len(srcs), len(lands), len(after)
    n_data = n_src + n_land

    def body(*refs):
        local, remote = copies(refs[:n_src], refs[n_src:n_data], *refs[n_data:n_data + 3])
        for send, landing in remote:
            send().wait_send()
            landing().wait_recv()
        for cp in local:
            cp().wait()

    res = pl.pallas_call(
        body, name=name, in_specs=[HBM_SPEC] * n_data + [SEM_SPEC] * 3 + [ANY_SPEC] * n_after,
        out_specs=[HBM_SPEC] * n_data,
        out_shape=[pltpu.HBM(a.shape, a.dtype) for a in list(srcs) + list(lands)],
        input_output_aliases={k: k for k in range(n_data)},
        compiler_params=pltpu.CompilerParams(has_side_effects=DATAFLOW),
    )(*srcs, *lands, *sems, *after)
    return res[n_src:]


def _block_diag(blocks, row_major):
    L, GB, g, P, N = blocks.shape
    eye = jnp.eye(g, dtype=blocks.dtype)
    if row_major:
        return jnp.einsum("lbgpn,gh->lbgphn", blocks, eye).reshape(L, GB, g * P, g * N)
    return jnp.einsum("lbgpn,gh->lbhngp", blocks, eye).reshape(L, GB, g * N, g * P)


def _diag_blocks(mat, g, P, N, row_major):
    GB = mat.shape[0]
    eye = jnp.eye(g, dtype=mat.dtype)
    if row_major:
        return jnp.einsum("bgphn,gh->bgpn", mat.reshape(GB, g, P, g, N), eye)
    return jnp.einsum("bhngp,gh->bgpn", mat.reshape(GB, g, N, g, P), eye)


def _pack(arrays, rows_multiple):
    flat = jnp.concatenate([a.reshape(-1).astype(F32) for a in arrays])
    pad = (-flat.shape[0]) % (rows_multiple * LANES)
    return jnp.pad(flat, (0, pad)).reshape(-1, LANES)


def _unpack(packed, shapes):
    flat = packed.reshape(-1)
    out, pos = [], 0
    for s in shapes:
        n = math.prod(s)
        out.append(flat[pos:pos + n].reshape(s))
        pos += n
    return out


SMALL = ["ln1_g", "ln1_b", "s5_lam_re", "s5_lam_im", "s5_log_dt", "s5_b_re", "s5_b_im", "s5_c_re", "s5_c_im", "s5_d",
         "conv_b", "g_s5", "g_conv", "ln2_g", "ln2_b", "ln3_g", "ln3_b"]
WEIGHTS = ["ffn1_gate", "ffn1_up", "ffn1_down", "ln1_g", "ln1_b", "w_in", "s5_lam_re", "s5_lam_im", "s5_log_dt",
           "s5_b_re", "s5_b_im", "s5_c_re", "s5_c_im", "s5_d", "s5_w_glu", "conv_w", "conv_b", "g_s5", "g_conv",
           "w_out", "ln2_g", "ln2_b", "ffn2_gate", "ffn2_up", "ffn2_down", "ln3_g", "ln3_b"]
TRANSPOSED = ["ffn1_gate", "ffn1_up", "w_in", "ffn2_gate", "ffn2_up"]
UPDATED_TRANSPOSED = ["ffn1_gate", "ffn1_up", "ffn2_gate", "ffn2_up"]
GROUPS = {"a": ["ffn1_gate", "ffn1_up", "ffn1_down"], "b": ["w_in", "s5_w_glu", "w_out"],
          "c": ["ffn2_gate", "ffn2_up", "ffn2_down"]}


def _train_step(x, target, w, m, v):
    T, D = x.shape
    L = w["ln1_g"].shape[0]
    alpha = (2.0 * L) ** 0.25
    G = w["s5_log_dt"].shape[1]
    d_s5 = G * S5_P
    d_conv = w["conv_b"].shape[1]
    GB = G // S5_GROUPS_PER_BLOCK
    me = _me()[3]

    def shard(n, l):
        return (jnp.swapaxes(w[n][l], 0, 1) if n in TRANSPOSED else w[n][l]).astype(BF16)

    conv_w_rows = jnp.pad(w["conv_w"], ((0, 0), (0, SUBLANES - CONV_W), (0, 0)))
    parts = [(l, grp) for l in range(L) for grp in GROUPS]
    step_one, step_two, token = {}, {}, []
    for l, grp in parts:
        srcs = [shard(n, l) for n in GROUPS[grp]] + ([conv_w_rows[l]] if grp == "b" else [])
        lands = [lax.empty((N_DEV,) + a.shape, a.dtype) for a in srcs]
        sems, srcs, lands, tok = _exchange_start(_gather_copies(len(srcs), SIBLING_AND_OTHER_CHIPS),
                                                 f"gather_start_{l}{grp}", srcs, lands, token)
        step_one[l, grp] = (sems, srcs, lands)
        token = [tok]

    def forward_on(i, after):
        if i >= len(parts):
            return []
        l, grp = parts[i]
        sems, srcs, lands = step_one[l, grp]
        copies = _gather_copies(len(srcs), SIBLING_AND_OTHER_CHIPS)
        lands = _exchange_wait(copies, f"gather_wait_{l}{grp}", sems, srcs, lands, after)
        sems, _, lands, tok = _exchange_start(_forward_copies(len(lands)), f"forward_start_{l}{grp}", [], lands, [])
        step_two[l, grp] = (sems, lands)
        return [tok]

    def gathered(i, after):
        l, grp = parts[i]
        sems, lands = step_two[l, grp]
        lands = _exchange_wait(_forward_copies(len(lands)), f"forward_wait_{l}{grp}", sems, [], lands, after)
        full = {n: p.reshape(-1, p.shape[-1]) for n, p in zip(GROUPS[grp], lands)}
        if grp == "b":
            full["conv_w"] = jnp.swapaxes(lands[-1][:, :CONV_W, :], 0, 1).reshape(CONV_W, d_conv)
        return full

    lre = w["s5_lam_re"].reshape(L * G, S5_N)
    lim = w["s5_lam_im"].reshape(L * G, S5_N)
    ldt = w["s5_log_dt"].reshape(L * G, 1)
    b_re = jnp.transpose(w["s5_b_re"], (3, 0, 1, 2)).reshape(S5_P, L * G, S5_N)
    b_im = jnp.transpose(w["s5_b_im"], (3, 0, 1, 2)).reshape(S5_P, L * G, S5_N)
    ab_re, ab_im, bb_re, bb_im = _s5_params_fwd(lre, lim, ldt, b_re, b_im)

    def groups(bb):
        return jnp.transpose(bb.reshape(S5_P, L, GB, S5_GROUPS_PER_BLOCK, S5_N), (1, 2, 3, 0, 4))

    bd_re = _block_diag(groups(bb_re), True).astype(BF16)
    bd_im = _block_diag(groups(bb_im), True).astype(BF16)
    c_shape = (L, GB, S5_GROUPS_PER_BLOCK, S5_P, S5_N)
    cd_re = _block_diag(w["s5_c_re"].reshape(c_shape), False).astype(BF16)
    cd_im = _block_diag(w["s5_c_im"].reshape(c_shape), False).astype(BF16)
    a_re = ab_re.reshape(L, 1, G * S5_N)
    a_im = ab_im.reshape(L, 1, G * S5_N)
    d_vec = w["s5_d"].reshape(L, 1, d_s5)

    def vec(name, l):
        return w[name][l].reshape(1, -1)

    saved, weights = [], []
    x_in, x_in_b = x, x.astype(BF16)
    token = forward_on(0, token)
    for l in range(L):
        i = len(GROUPS) * l
        gw = gathered(i, [x_in] if l else token)
        s = {"x0b": x_in_b}
        s["g1"], s["u1"], s["h1"] = _ffn_up(x_in_b, gw["ffn1_gate"], gw["ffn1_up"])
        x1, s["x1b"], s["xh1"], s["rstd1"] = _mm_res_ln(s["h1"], gw["ffn1_down"], x_in, vec("ln1_g", l),
                                                         vec("ln1_b", l), 0.5, alpha, forward_on(i + 1, [s["h1"]]))
        gw.update(gathered(i + 1, [s["x1b"]]))
        s["proj"] = _mm_nt(s["x1b"], gw["w_in"])
        s["y"], s["sr"], s["si"] = _s5_fwd(s["proj"], bd_re[l], bd_im[l], cd_re[l], cd_im[l], a_re[l], a_im[l],
                                           d_vec[l])
        s["z"] = _conv_fwd(s["proj"], gw["conv_w"], vec("conv_b", l), d_s5, d_conv)
        s["mcat"], s["gl"] = _mix_post(s["y"], s["z"], gw["s5_w_glu"], vec("g_s5", l), vec("g_conv", l))
        x2, s["x2b"], s["xh2"], s["rstd2"] = _mm_res_ln(s["mcat"], gw["w_out"], x1, vec("ln2_g", l),
                                                         vec("ln2_b", l), 1.0, alpha, forward_on(i + 2, [s["mcat"]]))
        gw.update(gathered(i + 2, [s["x2b"]]))
        s["g2"], s["u2"], s["h2"] = _ffn_up(s["x2b"], gw["ffn2_gate"], gw["ffn2_up"])
        x3, x3b, s["xh3"], s["rstd3"] = _mm_res_ln(s["h2"], gw["ffn2_down"], x2, vec("ln3_g", l), vec("ln3_b", l),
                                                   0.5, alpha, forward_on(i + 3, [s["h2"]]))
        saved.append(s)
        weights.append(gw)
        x_in, x_in_b = x3, x3b

    last = saved[L - 1]
    dr, drb, dg, db, loss = _loss_ln_bwd(x_in, target, last["xh3"], last["rstd3"], vec("ln3_g", L - 1))
    small = [dict() for _ in range(L)]
    small[L - 1]["ln3_g"], small[L - 1]["ln3_b"] = dg, db
    bufs = {grp: [lax.empty((L, N_DEV) + shard(n, 0).shape, BF16) for n in names] for grp, names in GROUPS.items()}
    scatters = {grp: [] for grp in GROUPS}
    grad_x = None
    for l in reversed(range(L)):
        gw, s, sm = weights[l], saved[l], small[l]
        full = {}

        def scatter_start(grp):
            fulls = [full[n].reshape((N_DEV, -1) + full[n].shape[1:]) for n in GROUPS[grp]]
            sems, fulls, bufs[grp], tok = _exchange_start(_scatter_copies(len(fulls), l), f"scatter_start_{l}{grp}",
                                                          fulls, bufs[grp], [])
            scatters[grp].append((l, sems, fulls))
            return [tok]

        def ffn_bwd(dr, drb, tag, grp, xb_in, ln):
            dgp, dup = _ffn_down_bwd(drb, gw[f"ffn{tag}_down"], s[f"g{tag}"], s[f"u{tag}"])
            full[f"ffn{tag}_down"] = _mm_tn(s[f"h{tag}"], drb, 0.5, BF16)
            full[f"ffn{tag}_gate"] = _mm_tn(dgp, xb_in, 1.0, BF16)
            full[f"ffn{tag}_up"] = _mm_tn(dup, xb_in, 1.0, BF16)
            return _mm_dx([(dgp, gw[f"ffn{tag}_gate"]), (dup, gw[f"ffn{tag}_up"])], dr, alpha, ln, scatter_start(grp))

        dr, drb, sm["ln2_g"], sm["ln2_b"] = ffn_bwd(dr, drb, 2, "c", s["x2b"], (s["xh2"], s["rstd2"], vec("ln2_g", l)))
        dm = _mm_nt(drb, gw["w_out"])
        full["w_out"] = _mm_tn(s["mcat"], drb, 1.0, BF16)
        dy, dz, dwglu, sm["g_s5"], sm["g_conv"] = _mix_post_bwd(dm, s["y"], s["gl"], s["z"], gw["s5_w_glu"],
                                                                vec("g_s5", l), vec("g_conv", l))
        full["s5_w_glu"] = dwglu.astype(BF16)
        du, dbd_re, dbd_im, dcd_re, dcd_im, sm["d_ab_re"], sm["d_ab_im"], sm["s5_d"] = _s5_bwd(
            dy, s["proj"], s["sr"], s["si"], bd_re[l], bd_im[l], cd_re[l], cd_im[l], a_re[l], a_im[l], d_vec[l])
        gsz = (S5_GROUPS_PER_BLOCK, S5_P, S5_N)
        sm["d_bb_re"] = _diag_blocks(dbd_re, *gsz, True)
        sm["d_bb_im"] = _diag_blocks(dbd_im, *gsz, True)
        sm["s5_c_re"] = _diag_blocks(dcd_re, *gsz, False).reshape(G, S5_P, S5_N)
        sm["s5_c_im"] = _diag_blocks(dcd_im, *gsz, False).reshape(G, S5_P, S5_N)
        dgb, dgc, dhh, sm["conv_w"], sm["conv_b"] = _conv_bwd(dz, s["proj"], gw["conv_w"], vec("conv_b", l),
                                                              d_s5, d_conv)
        dproj = jnp.concatenate([du, dgb, dgc, dhh], axis=1)
        full["w_in"] = _mm_tn(dproj, s["x1b"], 1.0, BF16)
        dr, drb, sm["ln1_g"], sm["ln1_b"] = _mm_dx([(dproj, gw["w_in"])], dr, alpha,
                                                   (s["xh1"], s["rstd1"], vec("ln1_g", l)), scatter_start("b"))
        if l > 0:
            prev = saved[l - 1]
            dr, drb, small[l - 1]["ln3_g"], small[l - 1]["ln3_b"] = ffn_bwd(
                dr, drb, 1, "a", s["x0b"], (prev["xh3"], prev["rstd3"], vec("ln3_g", l - 1)))
        else:
            (grad_x,) = ffn_bwd(dr, drb, 1, "a", s["x0b"], None)

    def stack(key):
        return jnp.stack([small[l][key] for l in range(L)])

    d_bb_re = jnp.transpose(stack("d_bb_re").reshape(L * G, S5_P, S5_N), (1, 0, 2))
    d_bb_im = jnp.transpose(stack("d_bb_im").reshape(L * G, S5_P, S5_N), (1, 0, 2))
    g_lre, g_lim, g_ldt, g_bre, g_bim = _s5_params_bwd(
        lre, lim, ldt, b_re, b_im, stack("d_ab_re").reshape(L * G, S5_N), stack("d_ab_im").reshape(L * G, S5_N),
        d_bb_re, d_bb_im)
    part = {n: stack(n) for n in ["ln1_g", "ln1_b", "s5_c_re", "s5_c_im", "s5_d", "conv_b", "g_s5", "g_conv", "ln2_g",
                                  "ln2_b", "ln3_g", "ln3_b", "conv_w"]}
    part["s5_lam_re"], part["s5_lam_im"], part["s5_log_dt"] = g_lre, g_lim, g_ldt
    part["s5_b_re"] = jnp.transpose(g_bre, (1, 2, 0))
    part["s5_b_im"] = jnp.transpose(g_bim, (1, 2, 0))

    small_names = SMALL + ["conv_w"]
    small_shapes = [w[n].shape for n in SMALL] + [(L, CONV_W, d_conv)]
    packed = _pack([part[n] for n in small_names], N_DEV * PACK_ROWS)
    rows = packed.shape[0] // N_DEV
    (landed,) = _exchange(_scatter_copies(1, 0), "scatter_small", [packed.reshape(N_DEV, rows, LANES)],
                          [lax.empty((1, N_DEV, rows, LANES), F32)])
    mine = _reduce_parts(landed)
    (summed,) = _exchange(_gather_copies(1), "gather_small", [mine[0]], [lax.empty((N_DEV, rows, LANES), F32)])
    small_grads = dict(zip(small_names, _unpack(summed, small_shapes)))

    out = {}

    def update(name, w3, m3, v3, shape, **grad):
        res = _adamw_update(w3, m3, v3, **grad)
        out[name] = [r.reshape(shape) for r in res]

    for grp in ("c", "b", "a"):
        for l, sems, fulls in scatters[grp]:
            bufs[grp] = _exchange_wait(_scatter_copies(len(fulls), l), f"scatter_wait_{l}{grp}", sems, fulls, bufs[grp],
                                       [grad_x])
        for n, parts in zip(GROUPS[grp], bufs[grp]):
            if n in UPDATED_TRANSPOSED:
                res = _adamw_update(*(jnp.swapaxes(a[n], 1, 2) for a in (w, m, v)), parts=parts)
                out[n] = [jnp.swapaxes(r, 1, 2) for r in res]
            elif n in TRANSPOSED:
                update(n, w[n], m[n], v[n], w[n].shape, grad=jnp.swapaxes(_reduce_parts(parts), 1, 2))
            else:
                update(n, w[n], m[n], v[n], w[n].shape, parts=parts)
    cw_shape = w["conv_w"].shape
    g_cw = lax.dynamic_slice_in_dim(small_grads["conv_w"], me * cw_shape[2], cw_shape[2], axis=2)
    update("conv_w", w["conv_w"], m["conv_w"], v["conv_w"], cw_shape, grad=g_cw)
    sizes = [w[n].shape for n in SMALL]
    pw, pm, pv, pg = (_pack([d[n] for n in SMALL], PACK_ROWS)[None] for d in (w, m, v, small_grads))
    for name, res in zip(SMALL, zip(*[_unpack(r, sizes) for r in _adamw_update(pw, pm, pv, grad=pg)])):
        out[name] = list(res)

    loss = lax.psum(loss[0, 0], AXES)
    return loss, grad_x, out


def kernel(x, ffn1_gate, ffn1_up, ffn1_down, ln1_g, ln1_b, w_in, s5_lam_re, s5_lam_im, s5_log_dt, s5_b_re, s5_b_im, s5_c_re, s5_c_im, s5_d, s5_w_glu, conv_w, conv_b, g_s5, g_conv, w_out, ln2_g, ln2_b, ffn2_gate, ffn2_up, ffn2_down, ln3_g, ln3_b, loss_target, m_ffn1_gate, m_ffn1_up, m_ffn1_down, m_ln1_g, m_ln1_b, m_w_in, m_s5_lam_re, m_s5_lam_im, m_s5_log_dt, m_s5_b_re, m_s5_b_im, m_s5_c_re, m_s5_c_im, m_s5_d, m_s5_w_glu, m_conv_w, m_conv_b, m_g_s5, m_g_conv, m_w_out, m_ln2_g, m_ln2_b, m_ffn2_gate, m_ffn2_up, m_ffn2_down, m_ln3_g, m_ln3_b, v_ffn1_gate, v_ffn1_up, v_ffn1_down, v_ln1_g, v_ln1_b, v_w_in, v_s5_lam_re, v_s5_lam_im, v_s5_log_dt, v_s5_b_re, v_s5_b_im, v_s5_c_re, v_s5_c_im, v_s5_d, v_s5_w_glu, v_conv_w, v_conv_b, v_g_s5, v_g_conv, v_w_out, v_ln2_g, v_ln2_b, v_ffn2_gate, v_ffn2_up, v_ffn2_down, v_ln3_g, v_ln3_b):
    given = dict(locals())
    w = {n: given[n] for n in WEIGHTS}
    m = {n: given["m_" + n] for n in WEIGHTS}
    v = {n: given["v_" + n] for n in WEIGHTS}
    T, D = x.shape[-2:]
    loss, grad_x, out = _train_step(x.reshape(T, D), loss_target.reshape(T, D), w, m, v)
    results = [loss, grad_x.reshape(x.shape)]
    for i in range(4):
        results += [out[n][i] for n in WEIGHTS]
    return tuple(results)
```

```python
import functools
import math

import jax
import jax.numpy as jnp
from jax import lax
from jax.experimental import pallas as pl
from jax.experimental.pallas import tpu as pltpu

F32 = jnp.float32
BF16 = jnp.bfloat16
MESH = pl.DeviceIdType.MESH
AXES = ("x", "y", "c")
N_DEV = 8
N_CHIPS = 4

S5_P = 16
S5_N = 64
CONV_W = 3
LN_EPS = 1e-5
RMS_EPS = 1e-6
ADAM_LR = 0.001
ADAM_B1 = 0.9
ADAM_B2 = 0.999
ADAM_EPS = 1e-08
ADAM_WD = 0.01
ADAM_STEP = 10

V7X_VMEM_BYTES = 64 * 1024 * 1024
VMEM_LIMIT = V7X_VMEM_BYTES * 7 // 8
LANES = 128
SUBLANES = 8
BF16_ROWS = 16
MXU_COLS = 256
ROWS_RESIDENT = 2048
PACK_ROWS = 512
S5_GROUPS_PER_BLOCK = LANES // S5_P
S5_STATE_BLOCK = S5_GROUPS_PER_BLOCK * S5_N

HBM_SPEC = pl.BlockSpec(memory_space=pltpu.HBM)
SEM_SPEC = pl.BlockSpec(memory_space=pltpu.SEMAPHORE)
ANY_SPEC = pl.BlockSpec(memory_space=pl.ANY)


def _tile(n, pref, align):
    best = None
    d = align
    while d <= min(n, pref):
        if n % d == 0:
            best = d
        d += align
    return best if best is not None else n


def _params(*sem):
    return pltpu.CompilerParams(dimension_semantics=sem, vmem_limit_bytes=VMEM_LIMIT)


def _dot_nn(a, b):
    return lax.dot_general(a, b, (((1,), (0,)), ((), ())), preferred_element_type=F32)


def _dot_nt(a, b):
    return lax.dot_general(a, b, (((1,), (1,)), ((), ())), preferred_element_type=F32)


def _dot_tn(a, b):
    return lax.dot_general(a, b, (((0,), (0,)), ((), ())), preferred_element_type=F32)


def _colsum(v):
    return jnp.sum(v, axis=0, keepdims=True)


def _rowmean(v):
    return jnp.mean(v, axis=-1, keepdims=True)


def _ffn_up(xb, wg, wu):
    T, D = xb.shape
    F = wg.shape[0]
    tm = _tile(T, ROWS_RESIDENT, 16)
    tn = _tile(F, MXU_COLS, LANES)

    def body(x_ref, wg_ref, wu_ref, g_ref, u_ref, h_ref):
        x = x_ref[...]
        g = _dot_nt(x, wg_ref[...])
        u = _dot_nt(x, wu_ref[...])
        g_ref[...] = g.astype(BF16)
        u_ref[...] = u.astype(BF16)
        h_ref[...] = (g * jax.nn.sigmoid(g) * u).astype(BF16)

    w_spec = pl.BlockSpec((tn, D), lambda j, i: (j, 0))
    o_spec = pl.BlockSpec((tm, tn), lambda j, i: (i, j))
    return pl.pallas_call(
        body, name="ffn_up", grid=(F // tn, T // tm),
        in_specs=[pl.BlockSpec((tm, D), lambda j, i: (i, 0)), w_spec, w_spec],
        out_specs=[o_spec, o_spec, o_spec],
        out_shape=[jax.ShapeDtypeStruct((T, F), BF16)] * 3,
        compiler_params=_params("arbitrary", "arbitrary"),
    )(xb, wg, wu)


def _mm_acc(pairs, after=()):
    T, K = pairs[0][0].shape
    D = pairs[0][1].shape[1]
    n = len(pairs)
    tk = _tile(K, 512, LANES)
    tm = _tile(T, 512, 16)

    def body(*refs):
        o_ref = refs[-1]

        @pl.when(pl.program_id(0) == 0)
        def _():
            o_ref[...] = jnp.zeros_like(o_ref)

        for r in range(0, T, tm):
            part = _dot_nn(refs[0][r:r + tm, :], refs[1][...])
            for a_ref, w_ref in zip(refs[2:2 * n:2], refs[3:2 * n:2]):
                part += _dot_nn(a_ref[r:r + tm, :], w_ref[...])
            o_ref[r:r + tm, :] += part

    in_specs, operands = [], []
    for a, w in pairs:
        in_specs += [pl.BlockSpec((T, tk), lambda k: (0, k)), pl.BlockSpec((tk, D), lambda k: (k, 0))]
        operands += [a, w]
    return pl.pallas_call(
        body, name="mm_acc", grid=(K // tk,),
        in_specs=in_specs + [ANY_SPEC] * len(after),
        out_specs=pl.BlockSpec((T, D), lambda k: (0, 0)),
        out_shape=jax.ShapeDtypeStruct((T, D), F32),
        compiler_params=_params("arbitrary"),
    )(*operands, *after)


def _mm_res_ln(a, w, res, g, b, scale, alpha, after=()):
    acc = _mm_acc([(a, w)], after)
    T, D = acc.shape
    tm = _tile(T, 256, 16)

    def body(acc_ref, res_ref, g_ref, b_ref, xo_ref, xb_ref, xh_ref, rstd_ref):
        r = alpha * res_ref[...] + scale * acc_ref[...]
        xc = r - _rowmean(r)
        rstd = lax.rsqrt(_rowmean(xc * xc) + LN_EPS)
        xh = xc * rstd
        xo = xh * g_ref[...] + b_ref[...]
        xo_ref[...] = xo
        xb_ref[...] = xo.astype(BF16)
        xh_ref[...] = xh
        rstd_ref[...] = rstd

    row = pl.BlockSpec((tm, D), lambda i: (i, 0))
    vec = pl.BlockSpec((1, D), lambda i: (0, 0))
    return pl.pallas_call(
        body, name="res_ln", grid=(T // tm,),
        in_specs=[row, row, vec, vec],
        out_specs=[row, row, row, pl.BlockSpec((tm, 1), lambda i: (i, 0))],
        out_shape=[jax.ShapeDtypeStruct((T, D), F32), jax.ShapeDtypeStruct((T, D), BF16),
                   jax.ShapeDtypeStruct((T, D), F32), jax.ShapeDtypeStruct((T, 1), F32)],
        compiler_params=_params("arbitrary"),
    )(acc, res, g, b)


def _mm_nt(a, w):
    M, K = a.shape
    N = w.shape[0]
    tm = _tile(M, ROWS_RESIDENT, 16)
    tn = _tile(N, MXU_COLS, LANES)

    def body(a_ref, w_ref, o_ref):
        o_ref[...] = _dot_nt(a_ref[...], w_ref[...])

    return pl.pallas_call(
        body, name="mm_nt", grid=(N // tn, M // tm),
        in_specs=[pl.BlockSpec((tm, K), lambda j, i: (i, 0)), pl.BlockSpec((tn, K), lambda j, i: (j, 0))],
        out_specs=pl.BlockSpec((tm, tn), lambda j, i: (i, j)),
        out_shape=jax.ShapeDtypeStruct((M, N), F32),
        compiler_params=_params("arbitrary", "arbitrary"),
    )(a, w)


def _mm_tn(a, b, scale, out_dtype):
    T, M = a.shape
    N = b.shape[1]
    tm = _tile(M, 512, LANES)
    tn = _tile(N, ROWS_RESIDENT, LANES)

    def body(a_ref, b_ref, o_ref):
        o_ref[...] = (scale * _dot_tn(a_ref[...], b_ref[...])).astype(out_dtype)

    return pl.pallas_call(
        body, name="mm_tn", grid=(M // tm, N // tn),
        in_specs=[pl.BlockSpec((T, tm), lambda i, j: (0, i)), pl.BlockSpec((T, tn), lambda i, j: (0, j))],
        out_specs=pl.BlockSpec((tm, tn), lambda i, j: (i, j)),
        out_shape=jax.ShapeDtypeStruct((M, N), out_dtype),
        compiler_params=_params("arbitrary", "arbitrary"),
    )(a, b)


def _ln_bwd(dy, xh, rstd, g):
    dxh = dy * g
    dr = rstd * (dxh - _rowmean(dxh) - xh * _rowmean(dxh * xh))
    return dr, _colsum(dy * xh), _colsum(dy)


def _loss_ln_bwd(y, target, xh, rstd, g):
    T, D = y.shape
    tm = _tile(T, 256, 16)

    def body(y_ref, t_ref, xh_ref, rstd_ref, g_ref, dr_ref, drb_ref, dg_ref, db_ref, loss_ref):
        i = pl.program_id(0)

        @pl.when(i == 0)
        def _():
            dg_ref[...] = jnp.zeros_like(dg_ref)
            db_ref[...] = jnp.zeros_like(db_ref)
            loss_ref[...] = jnp.zeros_like(loss_ref)

        err = y_ref[...] - t_ref[...]
        loss_ref[...] += (0.5 / D) * _colsum(jnp.sum(err * err, axis=1, keepdims=True))
        dr, dg, db = _ln_bwd(err * (1.0 / D), xh_ref[...], rstd_ref[...], g_ref[...])
        dr_ref[...] = dr
        drb_ref[...] = dr.astype(BF16)
        dg_ref[...] += dg
        db_ref[...] += db

    row = pl.BlockSpec((tm, D), lambda i: (i, 0))
    vec = pl.BlockSpec((1, D), lambda i: (0, 0))
    return pl.pallas_call(
        body, name="loss_ln_bwd", grid=(T // tm,),
        in_specs=[row, row, row, pl.BlockSpec((tm, 1), lambda i: (i, 0)), vec],
        out_specs=[row, row, vec, vec, pl.BlockSpec((1, 1), lambda i: (0, 0))],
        out_shape=[jax.ShapeDtypeStruct((T, D), F32), jax.ShapeDtypeStruct((T, D), BF16),
                   jax.ShapeDtypeStruct((1, D), F32), jax.ShapeDtypeStruct((1, D), F32),
                   jax.ShapeDtypeStruct((1, 1), F32)],
        compiler_params=_params("arbitrary"),
    )(y, target, xh, rstd, g)


def _ffn_down_bwd(drb, wd, gpre, upre):
    T, D = drb.shape
    F = wd.shape[0]
    tm = _tile(T, ROWS_RESIDENT, 16)
    tn = _tile(F, MXU_COLS, LANES)

    def body(dr_ref, wd_ref, g_ref, u_ref, dg_ref, du_ref):
        dh = 0.5 * _dot_nt(dr_ref[...], wd_ref[...])
        g = g_ref[...].astype(F32)
        u = u_ref[...].astype(F32)
        sg = jax.nn.sigmoid(g)
        du_ref[...] = (dh * (g * sg)).astype(BF16)
        dg_ref[...] = (dh * u * (sg * (1.0 + g * (1.0 - sg)))).astype(BF16)

    t_spec = pl.BlockSpec((tm, tn), lambda j, i: (i, j))
    return pl.pallas_call(
        body, name="ffn_down_bwd", grid=(F // tn, T // tm),
        in_specs=[pl.BlockSpec((tm, D), lambda j, i: (i, 0)), pl.BlockSpec((tn, D), lambda j, i: (j, 0)),
                  t_spec, t_spec],
        out_specs=[t_spec, t_spec],
        out_shape=[jax.ShapeDtypeStruct((T, F), BF16), jax.ShapeDtypeStruct((T, F), BF16)],
        compiler_params=_params("arbitrary", "arbitrary"),
    )(drb, wd, gpre, upre)


def _mm_dx(pairs, res, alpha, ln=None, after=()):
    acc = _mm_acc(pairs, after)
    T, D = acc.shape
    tm = _tile(T, 256, 16)
    with_ln = ln is not None

    def body(acc_ref, res_ref, *refs):
        dx = alpha * res_ref[...] + acc_ref[...]
        if with_ln:
            xh_ref, rstd_ref, g_ref, dr_ref, drb_ref, dg_ref, db_ref = refs

            @pl.when(pl.program_id(0) == 0)
            def _():
                dg_ref[...] = jnp.zeros_like(dg_ref)
                db_ref[...] = jnp.zeros_like(db_ref)

            dr, dg, db = _ln_bwd(dx, xh_ref[...], rstd_ref[...], g_ref[...])
            dr_ref[...] = dr
            drb_ref[...] = dr.astype(BF16)
            dg_ref[...] += dg
            db_ref[...] += db
        else:
            refs[0][...] = dx

    row = pl.BlockSpec((tm, D), lambda i: (i, 0))
    vec = pl.BlockSpec((1, D), lambda i: (0, 0))
    in_specs, operands = [row, row], [acc, res]
    if with_ln:
        in_specs += [row, pl.BlockSpec((tm, 1), lambda i: (i, 0)), vec]
        operands += list(ln)
        out_specs = [row, row, vec, vec]
        out_shape = [jax.ShapeDtypeStruct((T, D), F32), jax.ShapeDtypeStruct((T, D), BF16),
                     jax.ShapeDtypeStruct((1, D), F32), jax.ShapeDtypeStruct((1, D), F32)]
    else:
        out_specs = [row]
        out_shape = [jax.ShapeDtypeStruct((T, D), F32)]
    return pl.pallas_call(
        body, name="dx_ln_bwd" if with_ln else "dx_res", grid=(T // tm,),
        in_specs=in_specs, out_specs=out_specs, out_shape=out_shape,
        compiler_params=_params("arbitrary"),
    )(*operands)


def _s5_discretize(lre, lim, ldt, br, bi):
    dt = jnp.exp(ldt)
    mag = jnp.exp(lre * dt)
    ang = lim * dt
    ar = mag * jnp.cos(ang)
    ai = mag * jnp.sin(ang)
    den = lre * lre + lim * lim
    nr = ar - 1.0
    qr = (nr * lre + ai * lim) / den
    qi = (ai * lre - nr * lim) / den
    bbr = qr[None] * br - qi[None] * bi
    bbi = qr[None] * bi + qi[None] * br
    return ar, ai, bbr, bbi


def _s5_params_fwd(lre, lim, ldt, br, bi):
    def body(lre_ref, lim_ref, ldt_ref, br_ref, bi_ref, ar_ref, ai_ref, bbr_ref, bbi_ref):
        ar, ai, bbr, bbi = _s5_discretize(lre_ref[...], lim_ref[...], ldt_ref[...], br_ref[...], bi_ref[...])
        ar_ref[...] = ar
        ai_ref[...] = ai
        bbr_ref[...] = bbr
        bbi_ref[...] = bbi

    sds = jax.ShapeDtypeStruct
    return pl.pallas_call(
        body, name="s5_params_fwd",
        out_shape=[sds(lre.shape, F32), sds(lre.shape, F32), sds(br.shape, F32), sds(br.shape, F32)],
        compiler_params=pltpu.CompilerParams(vmem_limit_bytes=VMEM_LIMIT),
    )(lre, lim, ldt, br, bi)


def _s5_params_bwd(lre, lim, ldt, br, bi, dar, dai, dbbr, dbbi):
    def body(lre_ref, lim_ref, ldt_ref, br_ref, bi_ref, dar_ref, dai_ref, dbbr_ref, dbbi_ref,
             o_lre, o_lim, o_ldt, o_br, o_bi):
        _, vjp = jax.vjp(_s5_discretize, lre_ref[...], lim_ref[...], ldt_ref[...], br_ref[...], bi_ref[...])
        g = vjp((dar_ref[...], dai_ref[...], dbbr_ref[...], dbbi_ref[...]))
        o_lre[...] = g[0]
        o_lim[...] = g[1]
        o_ldt[...] = g[2]
        o_br[...] = g[3]
        o_bi[...] = g[4]

    sds = jax.ShapeDtypeStruct
    return pl.pallas_call(
        body, name="s5_params_bwd",
        out_shape=[sds(lre.shape, F32), sds(lre.shape, F32), sds(ldt.shape, F32), sds(br.shape, F32),
                   sds(br.shape, F32)],
        compiler_params=pltpu.CompilerParams(vmem_limit_bytes=VMEM_LIMIT),
    )(lre, lim, ldt, br, bi, dar, dai, dbbr, dbbi)


def _s5_fwd(proj, bdr, bdi, cdr, cdi, ar, ai, dvec):
    T = proj.shape[0]
    GB, UB, SB = bdr.shape
    tc = _tile(T, 256, SUBLANES)

    def body(u_ref, bdr_ref, bdi_ref, cdr_ref, cdi_ref, ar_ref, ai_ref, d_ref, y_ref, sr_ref, si_ref,
             cr_ref, ci_ref):
        @pl.when(pl.program_id(1) == 0)
        def _():
            cr_ref[...] = jnp.zeros_like(cr_ref)
            ci_ref[...] = jnp.zeros_like(ci_ref)

        u = u_ref[...]
        ub = u.astype(BF16)
        sr_ref[...] = _dot_nn(ub, bdr_ref[...])
        si_ref[...] = _dot_nn(ub, bdi_ref[...])
        a_re = ar_ref[...]
        a_im = ai_ref[...]

        def step(t, carry):
            p_re, p_im = carry
            row = pl.ds(t, 1)
            n_re = a_re * p_re - a_im * p_im + sr_ref[row, :]
            n_im = a_re * p_im + a_im * p_re + si_ref[row, :]
            sr_ref[row, :] = n_re
            si_ref[row, :] = n_im
            return n_re, n_im

        p_re, p_im = lax.fori_loop(0, tc, step, (cr_ref[...], ci_ref[...]), unroll=8)
        cr_ref[...] = p_re
        ci_ref[...] = p_im
        y_ref[...] = (_dot_nn(sr_ref[...].astype(BF16), cdr_ref[...])
                      - _dot_nn(si_ref[...].astype(BF16), cdi_ref[...]) + d_ref[...] * u)

    return pl.pallas_call(
        body, name="s5_fwd", grid=(GB, T // tc),
        in_specs=[pl.BlockSpec((tc, UB), lambda j, t: (t, j)),
                  pl.BlockSpec((None, UB, SB), lambda j, t: (j, 0, 0)),
                  pl.BlockSpec((None, UB, SB), lambda j, t: (j, 0, 0)),
                  pl.BlockSpec((None, SB, UB), lambda j, t: (j, 0, 0)),
                  pl.BlockSpec((None, SB, UB), lambda j, t: (j, 0, 0)),
                  pl.BlockSpec((1, SB), lambda j, t: (0, j)),
                  pl.BlockSpec((1, SB), lambda j, t: (0, j)),
                  pl.BlockSpec((1, UB), lambda j, t: (0, j))],
        out_specs=[pl.BlockSpec((tc, UB), lambda j, t: (t, j)),
                   pl.BlockSpec((tc, SB), lambda j, t: (t, j)),
                   pl.BlockSpec((tc, SB), lambda j, t: (t, j))],
        out_shape=[jax.ShapeDtypeStruct((T, GB * UB), F32), jax.ShapeDtypeStruct((T, GB * SB), F32),
                   jax.ShapeDtypeStruct((T, GB * SB), F32)],
        scratch_shapes=[pltpu.VMEM((1, SB), F32), pltpu.VMEM((1, SB), F32)],
        compiler_params=_params("arbitrary", "arbitrary"),
    )(proj, bdr, bdi, cdr, cdi, ar, ai, dvec)


def _s5_bwd(dy, proj, sr, si, bdr, bdi, cdr, cdi, ar, ai, dvec):
    T = dy.shape[0]
    GB, UB, SB = bdr.shape
    tc = _tile(T, 256, SUBLANES)
    nt = T // tc
    halo_blocks = tc // SUBLANES

    def body(dy_ref, u_ref, sr_ref, si_ref, hr_ref, hi_ref, bdr_ref, bdi_ref, cdr_ref, cdi_ref, ar_ref, ai_ref,
             d_ref, du_ref, dbdr_ref, dbdi_ref, dcdr_ref, dcdi_ref, dar_ref, dai_ref, dd_ref,
             gr_ref, gi_ref, pr_ref, pi_ref, cr_ref, ci_ref):
        step_no = pl.program_id(1)
        first_chunk = step_no == nt - 1

        @pl.when(step_no == 0)
        def _():
            for ref in (cr_ref, ci_ref, dbdr_ref, dbdi_ref, dcdr_ref, dcdi_ref, dar_ref, dai_ref, dd_ref):
                ref[...] = jnp.zeros_like(ref)

        dy = dy_ref[...]
        dyb = dy.astype(BF16)
        u = u_ref[...]
        s_re = sr_ref[...]
        s_im = si_ref[...]
        gr_ref[...] = _dot_nt(dyb, cdr_ref[...])
        gi_ref[...] = -_dot_nt(dyb, cdi_ref[...])
        dcdr_ref[...] += _dot_tn(s_re.astype(BF16), dyb)
        dcdi_ref[...] -= _dot_tn(s_im.astype(BF16), dyb)
        keep = jnp.where(first_chunk, 0.0, 1.0)
        pr_ref[0:SUBLANES, :] = hr_ref[...] * keep
        pi_ref[0:SUBLANES, :] = hi_ref[...] * keep
        pr_ref[SUBLANES:, :] = s_re
        pi_ref[SUBLANES:, :] = s_im
        a_re = ar_ref[...]
        a_im = ai_ref[...]

        def step(n, carry):
            c_re, c_im, acc_re, acc_im = carry
            t = tc - 1 - n
            row = pl.ds(t, 1)
            g_re = gr_ref[row, :] + a_re * c_re + a_im * c_im
            g_im = gi_ref[row, :] + a_re * c_im - a_im * c_re
            gr_ref[row, :] = g_re
            gi_ref[row, :] = g_im
            prev = pl.ds(t + SUBLANES - 1, 1)
            p_re = pr_ref[prev, :]
            p_im = pi_ref[prev, :]
            acc_re = acc_re + p_re * g_re + p_im * g_im
            acc_im = acc_im + p_re * g_im - p_im * g_re
            return g_re, g_im, acc_re, acc_im

        zero = jnp.zeros((1, SB), F32)
        c_re, c_im, acc_re, acc_im = lax.fori_loop(0, tc, step, (cr_ref[...], ci_ref[...], zero, zero), unroll=8)
        cr_ref[...] = c_re
        ci_ref[...] = c_im
        dar_ref[...] += acc_re
        dai_ref[...] += acc_im
        gsr = gr_ref[...].astype(BF16)
        gsi = gi_ref[...].astype(BF16)
        ub = u.astype(BF16)
        dbdr_ref[...] += _dot_tn(ub, gsr)
        dbdi_ref[...] += _dot_tn(ub, gsi)
        du_ref[...] = (_dot_nt(gsr, bdr_ref[...]) + _dot_nt(gsi, bdi_ref[...]) + d_ref[...] * dy).astype(BF16)
        dd_ref[...] += _colsum(dy * u)

    def rev(t):
        return nt - 1 - t

    def halo(j, t):
        return (jnp.maximum(rev(t) * halo_blocks - 1, 0), j)

    ublk = pl.BlockSpec((tc, UB), lambda j, t: (rev(t), j))
    sblk = pl.BlockSpec((tc, SB), lambda j, t: (rev(t), j))
    bd_spec = pl.BlockSpec((None, UB, SB), lambda j, t: (j, 0, 0))
    cd_spec = pl.BlockSpec((None, SB, UB), lambda j, t: (j, 0, 0))
    svec = pl.BlockSpec((1, SB), lambda j, t: (0, j))
    uvec = pl.BlockSpec((1, UB), lambda j, t: (0, j))
    sds = jax.ShapeDtypeStruct
    return pl.pallas_call(
        body, name="s5_bwd", grid=(GB, nt),
        in_specs=[ublk, ublk, sblk, sblk, pl.BlockSpec((SUBLANES, SB), halo), pl.BlockSpec((SUBLANES, SB), halo),
                  bd_spec, bd_spec, cd_spec, cd_spec, svec, svec, uvec],
        out_specs=[ublk, bd_spec, bd_spec, cd_spec, cd_spec, svec, svec, uvec],
        out_shape=[sds((T, GB * UB), BF16), sds((GB, UB, SB), F32), sds((GB, UB, SB), F32),
                   sds((GB, SB, UB), F32), sds((GB, SB, UB), F32), sds((1, GB * SB), F32),
                   sds((1, GB * SB), F32), sds((1, GB * UB), F32)],
        scratch_shapes=[pltpu.VMEM((tc, SB), F32), pltpu.VMEM((tc, SB), F32),
                        pltpu.VMEM((tc + SUBLANES, SB), F32), pltpu.VMEM((tc + SUBLANES, SB), F32),
                        pltpu.VMEM((1, SB), F32), pltpu.VMEM((1, SB), F32)],
        compiler_params=_params("arbitrary", "arbitrary"),
    )(dy, proj, sr, si, sr, si, bdr, bdi, cdr, cdi, ar, ai, dvec)


def _shift_down(v, k):
    rows = lax.broadcasted_iota(jnp.int32, v.shape, 0)
    return jnp.where(rows >= k, pltpu.roll(v, k, 0), 0.0)


def _shift_up(v, k):
    n = v.shape[0]
    rows = lax.broadcasted_iota(jnp.int32, v.shape, 0)
    return jnp.where(rows < n - k, pltpu.roll(v, n - k, 0), 0.0)


def _conv_specs(T, cb, n_s5_blocks, n_conv_blocks):
    gb = pl.BlockSpec((T, cb), lambda j: (0, n_s5_blocks + j))
    gc = pl.BlockSpec((T, cb), lambda j: (0, n_s5_blocks + n_conv_blocks + j))
    hh = pl.BlockSpec((T, cb), lambda j: (0, n_s5_blocks + 2 * n_conv_blocks + j))
    return gb, gc, hh


def _conv_fwd(proj, cw, cbias, d_s5, d_conv):
    T = proj.shape[0]
    cb = _tile(d_conv, 256, LANES)

    def body(gb_ref, gc_ref, hh_ref, w_ref, b_ref, z_ref):
        v = gc_ref[...] * hh_ref[...]
        w = w_ref[...]
        cv = b_ref[...] + w[0:1, :] * _shift_down(v, 2) + w[1:2, :] * _shift_down(v, 1) + w[2:3, :] * v
        z_ref[...] = gb_ref[...] * cv

    gb, gc, hh = _conv_specs(T, cb, d_s5 // cb, d_conv // cb)
    col = pl.BlockSpec((T, cb), lambda j: (0, j))
    return pl.pallas_call(
        body, name="conv_fwd", grid=(d_conv // cb,),
        in_specs=[gb, gc, hh, pl.BlockSpec((CONV_W, cb), lambda j: (0, j)), pl.BlockSpec((1, cb), lambda j: (0, j))],
        out_specs=col, out_shape=jax.ShapeDtypeStruct((T, d_conv), F32),
        compiler_params=_params("arbitrary"),
    )(proj, proj, proj, cw, cbias)


def _conv_bwd(dz, proj, cw, cbias, d_s5, d_conv):
    T = proj.shape[0]
    cb = _tile(d_conv, 256, LANES)

    def body(dz_ref, gb_ref, gc_ref, hh_ref, w_ref, b_ref, dgb_ref, dgc_ref, dhh_ref, dw_ref, db_ref):
        gc = gc_ref[...]
        hh = hh_ref[...]
        dz = dz_ref[...]
        w = w_ref[...]
        v = gc * hh
        v1 = _shift_down(v, 1)
        v2 = _shift_down(v, 2)
        cv = b_ref[...] + w[0:1, :] * v2 + w[1:2, :] * v1 + w[2:3, :] * v
        dgb_ref[...] = (dz * cv).astype(BF16)
        dcv = dz * gb_ref[...]
        dv = w[2:3, :] * dcv + w[1:2, :] * _shift_up(dcv, 1) + w[0:1, :] * _shift_up(dcv, 2)
        dgc_ref[...] = (dv * hh).astype(BF16)
        dhh_ref[...] = (dv * gc).astype(BF16)
        dw_ref[0:1, :] = _colsum(dcv * v2)
        dw_ref[1:2, :] = _colsum(dcv * v1)
        dw_ref[2:3, :] = _colsum(dcv * v)
        db_ref[...] = _colsum(dcv)

    gb, gc, hh = _conv_specs(T, cb, d_s5 // cb, d_conv // cb)
    col = pl.BlockSpec((T, cb), lambda j: (0, j))
    wspec = pl.BlockSpec((CONV_W, cb), lambda j: (0, j))
    bspec = pl.BlockSpec((1, cb), lambda j: (0, j))
    sds = jax.ShapeDtypeStruct
    return pl.pallas_call(
        body, name="conv_bwd", grid=(d_conv // cb,),
        in_specs=[col, gb, gc, hh, wspec, bspec],
        out_specs=[col, col, col, wspec, bspec],
        out_shape=[sds((T, d_conv), BF16), sds((T, d_conv), BF16), sds((T, d_conv), BF16),
                   sds((CONV_W, d_conv), F32), sds((1, d_conv), F32)],
        compiler_params=_params("arbitrary"),
    )(dz, proj, proj, proj, cw, cbias)


def _rms(v, g):
    rstd = lax.rsqrt(_rowmean(v * v) + RMS_EPS)
    return v * rstd * g, rstd


def _rms_bwd(dyn, v, rstd, g):
    w = dyn * g
    return rstd * w - v * (rstd * rstd * rstd) * _rowmean(w * v), _colsum(dyn * v * rstd)


def _mix_post(y, z, wglu, g_s5, g_conv):
    T, C = y.shape
    tm = _tile(T, 256, 16)

    def body(y_ref, z_ref, w_ref, gs_ref, gc_ref, m_ref, gl_ref):
        ge = jax.nn.gelu(y_ref[...])
        gl = _dot_nn(ge.astype(BF16), w_ref[...])
        gl_ref[...] = gl
        yn, _ = _rms(ge * jax.nn.sigmoid(gl), gs_ref[...])
        zn, _ = _rms(z_ref[...], gc_ref[...])
        m_ref[:, 0:C] = yn.astype(BF16)
        m_ref[:, C:2 * C] = zn.astype(BF16)

    row = pl.BlockSpec((tm, C), lambda i: (i, 0))
    vec = pl.BlockSpec((1, C), lambda i: (0, 0))
    return pl.pallas_call(
        body, name="mix_post", grid=(T // tm,),
        in_specs=[row, row, pl.BlockSpec((C, C), lambda i: (0, 0)), vec, vec],
        out_specs=[pl.BlockSpec((tm, 2 * C), lambda i: (i, 0)), row],
        out_shape=[jax.ShapeDtypeStruct((T, 2 * C), BF16), jax.ShapeDtypeStruct((T, C), F32)],
        compiler_params=_params("arbitrary"),
    )(y, z, wglu, g_s5, g_conv)


def _mix_post_bwd(dm, y, gl, z, wglu, g_s5, g_conv):
    T, C = y.shape
    tm = _tile(T, 256, 16)

    def body(dm_ref, y_ref, gl_ref, z_ref, w_ref, gs_ref, gc_ref, dy_ref, dz_ref, dw_ref, dgs_ref, dgc_ref):
        @pl.when(pl.program_id(0) == 0)
        def _():
            dw_ref[...] = jnp.zeros_like(dw_ref)
            dgs_ref[...] = jnp.zeros_like(dgs_ref)
            dgc_ref[...] = jnp.zeros_like(dgc_ref)

        yv = y_ref[...]
        ge, gelu_vjp = jax.vjp(jax.nn.gelu, yv)
        gl = gl_ref[...]
        sg = jax.nn.sigmoid(gl)
        y2 = ge * sg
        _, rstd_y = _rms(y2, gs_ref[...])
        dy2, dgs = _rms_bwd(dm_ref[:, 0:C], y2, rstd_y, gs_ref[...])
        dgs_ref[...] += dgs
        dgl = (dy2 * ge * sg * (1.0 - sg)).astype(BF16)
        dge = dy2 * sg + _dot_nt(dgl, w_ref[...])
        dw_ref[...] += _dot_tn(ge.astype(BF16), dgl)
        dy_ref[...] = gelu_vjp(dge)[0]
        zv = z_ref[...]
        _, rstd_z = _rms(zv, gc_ref[...])
        dz, dgc = _rms_bwd(dm_ref[:, C:2 * C], zv, rstd_z, gc_ref[...])
        dz_ref[...] = dz
        dgc_ref[...] += dgc

    row = pl.BlockSpec((tm, C), lambda i: (i, 0))
    vec = pl.BlockSpec((1, C), lambda i: (0, 0))
    full = pl.BlockSpec((C, C), lambda i: (0, 0))
    sds = jax.ShapeDtypeStruct
    return pl.pallas_call(
        body, name="mix_post_bwd", grid=(T // tm,),
        in_specs=[pl.BlockSpec((tm, 2 * C), lambda i: (i, 0)), row, row, row, full, vec, vec],
        out_specs=[row, row, full, vec, vec],
        out_shape=[sds((T, C), F32), sds((T, C), F32), sds((C, C), F32), sds((1, C), F32), sds((1, C), F32)],
        compiler_params=_params("arbitrary"),
    )(dm, y, gl, z, wglu, g_s5, g_conv)


def _adamw(w, g, m, v):
    m = ADAM_B1 * m + (1.0 - ADAM_B1) * g
    v = ADAM_B2 * v + (1.0 - ADAM_B2) * (g * g)
    m_hat = m / (1.0 - ADAM_B1 ** ADAM_STEP)
    v_hat = v / (1.0 - ADAM_B2 ** ADAM_STEP)
    return -ADAM_LR * (m_hat / (jnp.sqrt(v_hat) + ADAM_EPS) + ADAM_WD * w), m, v


def _sum_parts(p_ref):
    total = p_ref[0].astype(F32)
    for d in range(1, p_ref.shape[0]):
        total = total + p_ref[d].astype(F32)
    return total


def _row_tile(R, C, n_streams):
    budget = VMEM_LIMIT // 3 // (n_streams * C * 4)
    return _tile(R, max(BF16_ROWS, budget), BF16_ROWS)


def _reduce_parts(parts):
    L, P, R, C = parts.shape
    tr = _row_tile(R, C, P + 1)

    def body(p_ref, o_ref):
        o_ref[...] = _sum_parts(p_ref)

    return pl.pallas_call(
        body, name="reduce_parts", grid=(L, R // tr),
        in_specs=[pl.BlockSpec((None, P, tr, C), lambda l, i: (l, 0, i, 0))],
        out_specs=pl.BlockSpec((None, tr, C), lambda l, i: (l, i, 0)),
        out_shape=jax.ShapeDtypeStruct((L, R, C), F32),
        compiler_params=_params("arbitrary", "arbitrary"),
    )(parts)


def _adamw_update(w, m, v, grad=None, parts=None):
    L, R, C = w.shape
    from_parts = parts is not None
    P = parts.shape[1] if from_parts else 1
    tr = _row_tile(R, C, P + 7)

    def body(g_in_ref, w_ref, m_ref, v_ref, g_ref, d_ref, nm_ref, nv_ref):
        g = _sum_parts(g_in_ref) if from_parts else g_in_ref[...]
        delta, nm, nv = _adamw(w_ref[...], g, m_ref[...], v_ref[...])
        g_ref[...] = g
        d_ref[...] = delta
        nm_ref[...] = nm
        nv_ref[...] = nv

    blk = pl.BlockSpec((None, tr, C), lambda l, i: (l, i, 0))
    g_spec = pl.BlockSpec((None, P, tr, C), lambda l, i: (l, 0, i, 0)) if from_parts else blk
    out = jax.ShapeDtypeStruct((L, R, C), F32)
    return pl.pallas_call(
        body, name="adamw_parts" if from_parts else "adamw", grid=(L, R // tr),
        in_specs=[g_spec, blk, blk, blk], out_specs=[blk, blk, blk, blk], out_shape=[out, out, out, out],
        compiler_params=_params("arbitrary", "arbitrary"),
    )(parts if from_parts else grad, w, m, v)


def _pair_sum(full, stage):
    _, R, C = full.shape
    tr = _row_tile(R, C, 4)

    def body(f_ref, s_ref, o_ref):
        mine = f_ref[lax.axis_index("c")]
        o_ref[...] = (mine.astype(F32) + s_ref[...].astype(F32)).astype(BF16)

    blk = pl.BlockSpec((None, tr, C), lambda q, i: (q, i, 0))
    return pl.pallas_call(
        body, name="pair_sum", grid=(N_CHIPS, R // tr),
        in_specs=[pl.BlockSpec((None, 2, tr, C), lambda q, i: (q, 0, i, 0)), blk],
        out_specs=blk, out_shape=jax.ShapeDtypeStruct((N_CHIPS, R, C), BF16),
        compiler_params=_params("arbitrary", "arbitrary"),
    )(full.reshape(N_CHIPS, 2, R, C), stage)


def _me():
    x, y, c = (lax.axis_index(a) for a in AXES)
    return x, y, c, 4 * x + 2 * y + c


def _peer(rel):
    x, y, c, _ = _me()
    px = 1 - x if rel & 4 else x
    py = 1 - y if rel & 2 else y
    pc = 1 - c if rel & 1 else c
    return (px, py, pc), 4 * px + 2 * py + pc


DATAFLOW = pltpu.SideEffectType.DATAFLOW_SIDE_EFFECTING


def _remote_copy(src, dst, sems):
    return functools.partial(pltpu.make_async_remote_copy, src_ref=src, dst_ref=dst, **sems)


ALL_PEERS = tuple(range(1, N_DEV))
SIBLING = 1
OTHER_CHIPS = (2, 4, 6)
SIBLING_AND_OTHER_CHIPS = (SIBLING,) + OTHER_CHIPS


def _gather_copies(n, rels=ALL_PEERS):
    def copies(srcs, lands, send_sems, recv_sems, local_sems):
        me = _me()[3]
        local, remote = [], []
        for k in range(n):
            local.append(functools.partial(pltpu.make_async_copy, srcs[k], lands[k].at[me], local_sems.at[k]))
            for rel in rels:
                dev, blk = _peer(rel)
                sems = dict(send_sem=send_sems.at[_sem_index(k, rel)], recv_sem=recv_sems.at[_sem_index(k, rel)],
                            device_id=dev, device_id_type=MESH)
                remote.append((_remote_copy(srcs[k], lands[k].at[me], sems), _remote_copy(srcs[k], lands[k].at[blk], sems)))
        return local, remote

    copies.n_arrays = n
    return copies


def _forward_copies(n):
    def copies(srcs, lands, send_sems, recv_sems, local_sems):
        sibling = _peer(SIBLING)[0]
        remote = []
        for k in range(n):
            for rel in OTHER_CHIPS:
                have = _peer(rel)[1]
                comes = _peer(rel | SIBLING)[1]
                sems = dict(send_sem=send_sems.at[_sem_index(k, rel)], recv_sem=recv_sems.at[_sem_index(k, rel)],
                            device_id=sibling, device_id_type=MESH)
                remote.append((_remote_copy(lands[k].at[have], lands[k].at[have], sems),
                               _remote_copy(lands[k].at[have], lands[k].at[comes], sems)))
        return [], remote

    copies.n_arrays = n
    return copies


def _scatter_copies(n, layer):
    def copies(srcs, lands, send_sems, recv_sems, local_sems):
        me = _me()[3]
        local, remote = [], []
        for k in range(n):
            local.append(functools.partial(pltpu.make_async_copy, srcs[k].at[me], lands[k].at[layer, me],
                                           local_sems.at[k]))
            for rel in range(1, N_DEV):
                dev, blk = _peer(rel)
                sems = dict(send_sem=send_sems.at[_sem_index(k, rel)], recv_sem=recv_sems.at[_sem_index(k, rel)],
                            device_id=dev, device_id_type=MESH)
                remote.append((_remote_copy(srcs[k].at[blk], lands[k].at[layer, me], sems),
                               _remote_copy(srcs[k].at[blk], lands[k].at[layer, blk], sems)))
        return local, remote

    copies.n_arrays = n
    return copies


def _pair_copies(n):
    def copies(srcs, lands, send_sems, recv_sems, local_sems):
        c = _me()[2]
        sibling = _peer(SIBLING)[0]
        remote = []
        for k in range(n):
            for chip in range(N_CHIPS):
                sems = dict(send_sem=send_sems.at[_sem_index(k, chip + 1)], recv_sem=recv_sems.at[_sem_index(k, chip + 1)],
                            device_id=sibling, device_id_type=MESH)
                block = srcs[k].at[2 * chip + 1 - c]
                remote.append((_remote_copy(block, lands[k].at[chip], sems), _remote_copy(block, lands[k].at[chip], sems)))
        return [], remote

    copies.n_arrays = n
    return copies


def _chip_scatter_copies(n, layer):
    def copies(srcs, lands, send_sems, recv_sems, local_sems):
        x, y, _, _ = _me()
        my_chip = 2 * x + y
        local, remote = [], []
        for k in range(n):
            local.append(functools.partial(pltpu.make_async_copy, srcs[k].at[my_chip], lands[k].at[layer, my_chip],
                                           local_sems.at[k]))
            for rel in OTHER_CHIPS:
                dev = _peer(rel)[0]
                chip = 2 * dev[0] + dev[1]
                sems = dict(send_sem=send_sems.at[_sem_index(k, rel)], recv_sem=recv_sems.at[_sem_index(k, rel)],
                            device_id=dev, device_id_type=MESH)
                remote.append((_remote_copy(srcs[k].at[chip], lands[k].at[layer, my_chip], sems),
                               _remote_copy(srcs[k].at[chip], lands[k].at[layer, chip], sems)))
        return local, remote

    copies.n_arrays = n
    return copies


def _sem_shapes(n):
    return [pltpu.SemaphoreType.DMA((n * (N_DEV - 1),)), pltpu.SemaphoreType.DMA((n * (N_DEV - 1),)),
            pltpu.SemaphoreType.DMA((n,))]


def _sem_index(k, rel):
    return k * (N_DEV - 1) + rel - 1


def _exchange(copies, name, srcs, lands):
    n_src, n_land = len(srcs), len(lands)

    def body(*refs):
        src_refs = refs[:n_src]
        land_refs = refs[n_src + n_land:n_src + 2 * n_land]
        local, remote = copies(src_refs, land_refs, *refs[n_src + 2 * n_land:])
        local = [cp() for cp in local]
        sends = [send() for send, _ in remote]
        for cp in local + sends:
            cp.start()
        for send, (_, landing) in zip(sends, remote):
            send.wait_send()
            landing().wait_recv()
        for cp in local:
            cp.wait()

    return pl.pallas_call(
        body, name=name, in_specs=[HBM_SPEC] * (n_src + n_land), out_specs=[HBM_SPEC] * n_land,
        out_shape=[jax.ShapeDtypeStruct(b.shape, b.dtype) for b in lands],
        scratch_shapes=_sem_shapes(copies.n_arrays),
        input_output_aliases={n_src + k: k for k in range(n_land)},
        compiler_params=pltpu.CompilerParams(has_side_effects=True),
    )(*srcs, *lands)


def _hbm(arrays):
    return [pltpu.with_memory_space_constraint(a, pltpu.HBM) for a in arrays]


def _exchange_start(copies, name, srcs, lands, after):
    n_src, n_land, n_after = len(srcs), len(lands), len(after)
    n_data = n_src + n_land

    def body(*refs):
        outs = refs[n_data + n_after:]
        local, remote = copies(refs[:n_src], refs[n_src:n_data], *outs[:3])
        for cp in local:
            cp().start()
        for send, _ in remote:
            send().start()
        outs[-1][...] = jnp.zeros_like(outs[-1])

    res = pl.pallas_call(
        body, name=name, in_specs=[HBM_SPEC] * n_data + [ANY_SPEC] * n_after,
        out_specs=[SEM_SPEC] * 3 + [HBM_SPEC] * n_data + [pl.BlockSpec(memory_space=pltpu.VMEM)],
        out_shape=_sem_shapes(copies.n_arrays) + [pltpu.HBM(a.shape, a.dtype) for a in list(srcs) + list(lands)]
        + [jax.ShapeDtypeStruct((SUBLANES, LANES), F32)],
        input_output_aliases={k: 3 + k for k in range(n_data)},
        compiler_params=pltpu.CompilerParams(has_side_effects=DATAFLOW),
    )(*_hbm(list(srcs) + list(lands)), *after)
    return res[:3], res[3:3 + n_src], res[3 + n_src:3 + n_data], res[-1]


def _exchange_wait(copies, name, sems, srcs, lands, after):
    n_src, n_land, n_after = len(srcs), len(lands), len(after)
    n_data = n_src + n_land

    def body(*refs):
        local, remote = copies(refs[:n_src], refs[n_src:n_data], *refs[n_data:n_data + 3])
        for send, landing in remote:
            send().wait_send()
            landing().wait_recv()
        for cp in local:
            cp().wait()

    res = pl.pallas_call(
        body, name=name, in_specs=[HBM_SPEC] * n_data + [SEM_SPEC] * 3 + [ANY_SPEC] * n_after,
        out_specs=[HBM_SPEC] * n_data,
        out_shape=[pltpu.HBM(a.shape, a.dtype) for a in list(srcs) + list(lands)],
        input_output_aliases={k: k for k in range(n_data)},
        compiler_params=pltpu.CompilerParams(has_side_effects=DATAFLOW),
    )(*srcs, *lands, *sems, *after)
    return res[:n_src], res[n_src:]


def _block_diag(blocks, row_major):
    L, GB, g, P, N = blocks.shape
    eye = jnp.eye(g, dtype=blocks.dtype)
    if row_major:
        return jnp.einsum("lbgpn,gh->lbgphn", blocks, eye).reshape(L, GB, g * P, g * N)
    return jnp.einsum("lbgpn,gh->lbhngp", blocks, eye).reshape(L, GB, g * N, g * P)


def _diag_blocks(mat, g, P, N, row_major):
    GB = mat.shape[0]
    eye = jnp.eye(g, dtype=mat.dtype)
    if row_major:
        return jnp.einsum("bgphn,gh->bgpn", mat.reshape(GB, g, P, g, N), eye)
    return jnp.einsum("bhngp,gh->bgpn", mat.reshape(GB, g, N, g, P), eye)


def _pack(arrays, rows_multiple):
    flat = jnp.concatenate([a.reshape(-1).astype(F32) for a in arrays])
    pad = (-flat.shape[0]) % (rows_multiple * LANES)
    return jnp.pad(flat, (0, pad)).reshape(-1, LANES)


def _unpack(packed, shapes):
    flat = packed.reshape(-1)
    out, pos = [], 0
    for s in shapes:
        n = math.prod(s)
        out.append(flat[pos:pos + n].reshape(s))
        pos += n
    return out


SMALL = ["ln1_g", "ln1_b", "s5_lam_re", "s5_lam_im", "s5_log_dt", "s5_b_re", "s5_b_im", "s5_c_re", "s5_c_im", "s5_d",
         "conv_b", "g_s5", "g_conv", "ln2_g", "ln2_b", "ln3_g", "ln3_b"]
WEIGHTS = ["ffn1_gate", "ffn1_up", "ffn1_down", "ln1_g", "ln1_b", "w_in", "s5_lam_re", "s5_lam_im", "s5_log_dt",
           "s5_b_re", "s5_b_im", "s5_c_re", "s5_c_im", "s5_d", "s5_w_glu", "conv_w", "conv_b", "g_s5", "g_conv",
           "w_out", "ln2_g", "ln2_b", "ffn2_gate", "ffn2_up", "ffn2_down", "ln3_g", "ln3_b"]
TRANSPOSED = ["ffn1_gate", "ffn1_up", "w_in", "ffn2_gate", "ffn2_up"]
UPDATED_TRANSPOSED = ["ffn1_gate", "ffn1_up", "ffn2_gate", "ffn2_up"]
GROUPS = {"a": ["ffn1_gate", "ffn1_up", "ffn1_down"], "b": ["w_in", "s5_w_glu", "w_out"],
          "c": ["ffn2_gate", "ffn2_up", "ffn2_down"]}


def _train_step(x, target, w, m, v):
    T, D = x.shape
    L = w["ln1_g"].shape[0]
    alpha = (2.0 * L) ** 0.25
    G = w["s5_log_dt"].shape[1]
    d_s5 = G * S5_P
    d_conv = w["conv_b"].shape[1]
    GB = G // S5_GROUPS_PER_BLOCK
    me = _me()[3]

    def shard(n, l):
        return (jnp.swapaxes(w[n][l], 0, 1) if n in TRANSPOSED else w[n][l]).astype(BF16)

    conv_w_rows = jnp.pad(w["conv_w"], ((0, 0), (0, SUBLANES - CONV_W), (0, 0)))
    parts = [(l, grp) for l in range(L) for grp in GROUPS]
    step_one, step_two, token = {}, {}, []
    for l, grp in parts:
        srcs = [shard(n, l) for n in GROUPS[grp]] + ([conv_w_rows[l]] if grp == "b" else [])
        lands = [lax.empty((N_DEV,) + a.shape, a.dtype) for a in srcs]
        sems, srcs, lands, tok = _exchange_start(_gather_copies(len(srcs), SIBLING_AND_OTHER_CHIPS),
                                                 f"gather_start_{l}{grp}", srcs, lands, token)
        step_one[l, grp] = (sems, srcs, lands)
        token = [tok]

    def forward_on(i, after):
        if i >= len(parts):
            return []
        l, grp = parts[i]
        sems, srcs, lands = step_one[l, grp]
        copies = _gather_copies(len(srcs), SIBLING_AND_OTHER_CHIPS)
        _, lands = _exchange_wait(copies, f"gather_wait_{l}{grp}", sems, srcs, lands, after)
        sems, _, lands, tok = _exchange_start(_forward_copies(len(lands)), f"forward_start_{l}{grp}", [], lands, [])
        step_two[l, grp] = (sems, lands)
        return [tok]

    def gathered(i, after):
        l, grp = parts[i]
        sems, lands = step_two[l, grp]
        _, lands = _exchange_wait(_forward_copies(len(lands)), f"forward_wait_{l}{grp}", sems, [], lands, after)
        full = {n: p.reshape(-1, p.shape[-1]) for n, p in zip(GROUPS[grp], lands)}
        if grp == "b":
            full["conv_w"] = jnp.swapaxes(lands[-1][:, :CONV_W, :], 0, 1).reshape(CONV_W, d_conv)
        return full

    lre = w["s5_lam_re"].reshape(L * G, S5_N)
    lim = w["s5_lam_im"].reshape(L * G, S5_N)
    ldt = w["s5_log_dt"].reshape(L * G, 1)
    b_re = jnp.transpose(w["s5_b_re"], (3, 0, 1, 2)).reshape(S5_P, L * G, S5_N)
    b_im = jnp.transpose(w["s5_b_im"], (3, 0, 1, 2)).reshape(S5_P, L * G, S5_N)
    ab_re, ab_im, bb_re, bb_im = _s5_params_fwd(lre, lim, ldt, b_re, b_im)

    def groups(bb):
        return jnp.transpose(bb.reshape(S5_P, L, GB, S5_GROUPS_PER_BLOCK, S5_N), (1, 2, 3, 0, 4))

    bd_re = _block_diag(groups(bb_re), True).astype(BF16)
    bd_im = _block_diag(groups(bb_im), True).astype(BF16)
    c_shape = (L, GB, S5_GROUPS_PER_BLOCK, S5_P, S5_N)
    cd_re = _block_diag(w["s5_c_re"].reshape(c_shape), False).astype(BF16)
    cd_im = _block_diag(w["s5_c_im"].reshape(c_shape), False).astype(BF16)
    a_re = ab_re.reshape(L, 1, G * S5_N)
    a_im = ab_im.reshape(L, 1, G * S5_N)
    d_vec = w["s5_d"].reshape(L, 1, d_s5)

    def vec(name, l):
        return w[name][l].reshape(1, -1)

    saved, weights = [], []
    x_in, x_in_b = x, x.astype(BF16)
    token = forward_on(0, token)
    for l in range(L):
        i = len(GROUPS) * l
        gw = gathered(i, [x_in] if l else token)
        s = {"x0b": x_in_b}
        s["g1"], s["u1"], s["h1"] = _ffn_up(x_in_b, gw["ffn1_gate"], gw["ffn1_up"])
        x1, s["x1b"], s["xh1"], s["rstd1"] = _mm_res_ln(s["h1"], gw["ffn1_down"], x_in, vec("ln1_g", l),
                                                         vec("ln1_b", l), 0.5, alpha, forward_on(i + 1, [s["h1"]]))
        gw.update(gathered(i + 1, [s["x1b"]]))
        s["proj"] = _mm_nt(s["x1b"], gw["w_in"])
        s["y"], s["sr"], s["si"] = _s5_fwd(s["proj"], bd_re[l], bd_im[l], cd_re[l], cd_im[l], a_re[l], a_im[l],
                                           d_vec[l])
        s["z"] = _conv_fwd(s["proj"], gw["conv_w"], vec("conv_b", l), d_s5, d_conv)
        s["mcat"], s["gl"] = _mix_post(s["y"], s["z"], gw["s5_w_glu"], vec("g_s5", l), vec("g_conv", l))
        x2, s["x2b"], s["xh2"], s["rstd2"] = _mm_res_ln(s["mcat"], gw["w_out"], x1, vec("ln2_g", l),
                                                         vec("ln2_b", l), 1.0, alpha, forward_on(i + 2, [s["mcat"]]))
        gw.update(gathered(i + 2, [s["x2b"]]))
        s["g2"], s["u2"], s["h2"] = _ffn_up(s["x2b"], gw["ffn2_gate"], gw["ffn2_up"])
        x3, x3b, s["xh3"], s["rstd3"] = _mm_res_ln(s["h2"], gw["ffn2_down"], x2, vec("ln3_g", l), vec("ln3_b", l),
                                                   0.5, alpha, forward_on(i + 3, [s["h2"]]))
        saved.append(s)
        weights.append(gw)
        x_in, x_in_b = x3, x3b

    last = saved[L - 1]
    dr, drb, dg, db, loss = _loss_ln_bwd(x_in, target, last["xh3"], last["rstd3"], vec("ln3_g", L - 1))
    small = [dict() for _ in range(L)]
    small[L - 1]["ln3_g"], small[L - 1]["ln3_b"] = dg, db
    bufs = {grp: [lax.empty((L, N_CHIPS) + shard(n, 0).shape, BF16) for n in names] for grp, names in GROUPS.items()}
    scatters = {grp: [] for grp in GROUPS}
    paired = []
    grad_x = None

    def scatter_on(after):
        if not paired:
            return []
        l_, grp, sems, fulls, stages = paired.pop()
        n = len(fulls)
        fulls, stages = _exchange_wait(_pair_copies(n), f"pair_wait_{l_}{grp}", sems, fulls, stages, after)
        sums = [_pair_sum(f, s) for f, s in zip(fulls, stages)]
        sems, sums, bufs[grp], tok = _exchange_start(_chip_scatter_copies(n, l_), f"scatter_start_{l_}{grp}", sums,
                                                     bufs[grp], [])
        scatters[grp].append((l_, sems, sums))
        return [tok]

    small_names = SMALL + ["conv_w"]
    small_shapes = [w[n].shape for n in SMALL] + [(L, CONV_W, d_conv)]

    def small_scatter_start(small):
        def stack(key):
            return jnp.stack([small[l][key] for l in range(L)])

        d_bb_re = jnp.transpose(stack("d_bb_re").reshape(L * G, S5_P, S5_N), (1, 0, 2))
        d_bb_im = jnp.transpose(stack("d_bb_im").reshape(L * G, S5_P, S5_N), (1, 0, 2))
        g_lre, g_lim, g_ldt, g_bre, g_bim = _s5_params_bwd(
            lre, lim, ldt, b_re, b_im, stack("d_ab_re").reshape(L * G, S5_N), stack("d_ab_im").reshape(L * G, S5_N),
            d_bb_re, d_bb_im)
        part = {n: stack(n) for n in ["ln1_g", "ln1_b", "s5_c_re", "s5_c_im", "s5_d", "conv_b", "g_s5", "g_conv",
                                      "ln2_g", "ln2_b", "ln3_g", "ln3_b", "conv_w"]}
        part["s5_lam_re"], part["s5_lam_im"], part["s5_log_dt"] = g_lre, g_lim, g_ldt
        part["s5_b_re"] = jnp.transpose(g_bre, (1, 2, 0))
        part["s5_b_im"] = jnp.transpose(g_bim, (1, 2, 0))
        packed = _pack([part[n] for n in small_names], N_DEV * PACK_ROWS)
        rows = packed.shape[0] // N_DEV
        return _exchange_start(_scatter_copies(1, 0), "scatter_small_start", [packed.reshape(N_DEV, rows, LANES)],
                               [lax.empty((1, N_DEV, rows, LANES), F32)], [])

    for l in reversed(range(L)):
        gw, s, sm = weights[l], saved[l], small[l]
        full = {}

        def scatter_start(grp):
            fulls = [full[n].reshape((N_DEV, -1) + full[n].shape[1:]) for n in GROUPS[grp]]
            stages = [lax.empty((N_CHIPS,) + f.shape[1:], BF16) for f in fulls]
            sems, fulls, stages, tok = _exchange_start(_pair_copies(len(fulls)), f"pair_start_{l}{grp}", fulls, stages, [])
            tokens = [tok] + scatter_on([tok])
            paired.append((l, grp, sems, fulls, stages))
            return tokens

        def ffn_bwd(dr, drb, tag, grp, xb_in, ln, after=()):
            dgp, dup = _ffn_down_bwd(drb, gw[f"ffn{tag}_down"], s[f"g{tag}"], s[f"u{tag}"])
            full[f"ffn{tag}_down"] = _mm_tn(s[f"h{tag}"], drb, 0.5, BF16)
            full[f"ffn{tag}_gate"] = _mm_tn(dgp, xb_in, 1.0, BF16)
            full[f"ffn{tag}_up"] = _mm_tn(dup, xb_in, 1.0, BF16)
            return _mm_dx([(dgp, gw[f"ffn{tag}_gate"]), (dup, gw[f"ffn{tag}_up"])], dr, alpha, ln,
                          scatter_start(grp) + list(after))

        dr, drb, sm["ln2_g"], sm["ln2_b"] = ffn_bwd(dr, drb, 2, "c", s["x2b"], (s["xh2"], s["rstd2"], vec("ln2_g", l)))
        dm = _mm_nt(drb, gw["w_out"])
        full["w_out"] = _mm_tn(s["mcat"], drb, 1.0, BF16)
        dy, dz, dwglu, sm["g_s5"], sm["g_conv"] = _mix_post_bwd(dm, s["y"], s["gl"], s["z"], gw["s5_w_glu"],
                                                                vec("g_s5", l), vec("g_conv", l))
        full["s5_w_glu"] = dwglu.astype(BF16)
        du, dbd_re, dbd_im, dcd_re, dcd_im, sm["d_ab_re"], sm["d_ab_im"], sm["s5_d"] = _s5_bwd(
            dy, s["proj"], s["sr"], s["si"], bd_re[l], bd_im[l], cd_re[l], cd_im[l], a_re[l], a_im[l], d_vec[l])
        gsz = (S5_GROUPS_PER_BLOCK, S5_P, S5_N)
        sm["d_bb_re"] = _diag_blocks(dbd_re, *gsz, True)
        sm["d_bb_im"] = _diag_blocks(dbd_im, *gsz, True)
        sm["s5_c_re"] = _diag_blocks(dcd_re, *gsz, False).reshape(G, S5_P, S5_N)
        sm["s5_c_im"] = _diag_blocks(dcd_im, *gsz, False).reshape(G, S5_P, S5_N)
        dgb, dgc, dhh, sm["conv_w"], sm["conv_b"] = _conv_bwd(dz, s["proj"], gw["conv_w"], vec("conv_b", l),
                                                              d_s5, d_conv)
        dproj = jnp.concatenate([du, dgb, dgc, dhh], axis=1)
        full["w_in"] = _mm_tn(dproj, s["x1b"], 1.0, BF16)
        dr, drb, sm["ln1_g"], sm["ln1_b"] = _mm_dx([(dproj, gw["w_in"])], dr, alpha,
                                                   (s["xh1"], s["rstd1"], vec("ln1_g", l)), scatter_start("b"))
        if l > 0:
            prev = saved[l - 1]
            dr, drb, small[l - 1]["ln3_g"], small[l - 1]["ln3_b"] = ffn_bwd(
                dr, drb, 1, "a", s["x0b"], (prev["xh3"], prev["rstd3"], vec("ln3_g", l - 1)))
        else:
            small_sems, small_srcs, small_lands, tok = small_scatter_start(small)
            (grad_x,) = ffn_bwd(dr, drb, 1, "a", s["x0b"], None, [tok])
    scatter_on([grad_x])

    _, landed = _exchange_wait(_scatter_copies(1, 0), "scatter_small_wait", small_sems, small_srcs, small_lands,
                               [grad_x])
    mine = _reduce_parts(landed[0])
    (summed,) = _exchange(_gather_copies(1), "gather_small", [mine[0]], [lax.empty(landed[0].shape[1:], F32)])
    small_grads = dict(zip(small_names, _unpack(summed, small_shapes)))

    out = {}

    def update(name, w3, m3, v3, shape, **grad):
        res = _adamw_update(w3, m3, v3, **grad)
        out[name] = [r.reshape(shape) for r in res]

    for grp in ("c", "b", "a"):
        for l, sems, sums in scatters[grp]:
            _, bufs[grp] = _exchange_wait(_chip_scatter_copies(len(sums), l), f"scatter_wait_{l}{grp}", sems, sums,
                                          bufs[grp], [grad_x])
        for n, parts in zip(GROUPS[grp], bufs[grp]):
            if n in UPDATED_TRANSPOSED:
                res = _adamw_update(*(jnp.swapaxes(a[n], 1, 2) for a in (w, m, v)), parts=parts)
                out[n] = [jnp.swapaxes(r, 1, 2) for r in res]
            elif n in TRANSPOSED:
                update(n, w[n], m[n], v[n], w[n].shape, grad=jnp.swapaxes(_reduce_parts(parts), 1, 2))
            else:
                update(n, w[n], m[n], v[n], w[n].shape, parts=parts)
    cw_shape = w["conv_w"].shape
    g_cw = lax.dynamic_slice_in_dim(small_grads["conv_w"], me * cw_shape[2], cw_shape[2], axis=2)
    update("conv_w", w["conv_w"], m["conv_w"], v["conv_w"], cw_shape, grad=g_cw)
    sizes = [w[n].shape for n in SMALL]
    pw, pm, pv, pg = (_pack([d[n] for n in SMALL], PACK_ROWS)[None] for d in (w, m, v, small_grads))
    for name, res in zip(SMALL, zip(*[_unpack(r, sizes) for r in _adamw_update(pw, pm, pv, grad=pg)])):
        out[name] = list(res)

    loss = lax.psum(loss[0, 0], AXES)
    return loss, grad_x, out


def kernel(x, ffn1_gate, ffn1_up, ffn1_down, ln1_g, ln1_b, w_in, s5_lam_re, s5_lam_im, s5_log_dt, s5_b_re, s5_b_im, s5_c_re, s5_c_im, s5_d, s5_w_glu, conv_w, conv_b, g_s5, g_conv, w_out, ln2_g, ln2_b, ffn2_gate, ffn2_up, ffn2_down, ln3_g, ln3_b, loss_target, m_ffn1_gate, m_ffn1_up, m_ffn1_down, m_ln1_g, m_ln1_b, m_w_in, m_s5_lam_re, m_s5_lam_im, m_s5_log_dt, m_s5_b_re, m_s5_b_im, m_s5_c_re, m_s5_c_im, m_s5_d, m_s5_w_glu, m_conv_w, m_conv_b, m_g_s5, m_g_conv, m_w_out, m_ln2_g, m_ln2_b, m_ffn2_gate, m_ffn2_up, m_ffn2_down, m_ln3_g, m_ln3_b, v_ffn1_gate, v_ffn1_up, v_ffn1_down, v_ln1_g, v_ln1_b, v_w_in, v_s5_lam_re, v_s5_lam_im, v_s5_log_dt, v_s5_b_re, v_s5_b_im, v_s5_c_re, v_s5_c_im, v_s5_d, v_s5_w_glu, v_conv_w, v_conv_b, v_g_s5, v_g_conv, v_w_out, v_ln2_g, v_ln2_b, v_ffn2_gate, v_ffn2_up, v_ffn2_down, v_ln3_g, v_ln3_b):
    given = dict(locals())
    w = {n: given[n] for n in WEIGHTS}
    m = {n: given["m_" + n] for n in WEIGHTS}
    v = {n: given["v_" + n] for n in WEIGHTS}
    T, D = x.shape[-2:]
    loss, grad_x, out = _train_step(x.reshape(T, D), loss_target.reshape(T, D), w, m, v)
    results = [loss, grad_x.reshape(x.shape)]
    for i in range(4):
        results += [out[n][i] for n in WEIGHTS]
    return tuple(results)
```

```python
import functools
import math

import jax
import jax.numpy as jnp
from jax import lax
from jax.experimental import pallas as pl
from jax.experimental.pallas import tpu as pltpu

F32 = jnp.float32
BF16 = jnp.bfloat16
MESH = pl.DeviceIdType.MESH
AXES = ("x", "y", "c")
N_DEV = 8
N_CHIPS = 4

S5_P = 16
S5_N = 64
CONV_W = 3
LN_EPS = 1e-5
RMS_EPS = 1e-6
ADAM_LR = 0.001
ADAM_B1 = 0.9
ADAM_B2 = 0.999
ADAM_EPS = 1e-08
ADAM_WD = 0.01
ADAM_STEP = 10

V7X_VMEM_BYTES = 64 * 1024 * 1024
VMEM_LIMIT = V7X_VMEM_BYTES * 7 // 8
LANES = 128
SUBLANES = 8
BF16_ROWS = 16
MXU_COLS = 256
ROWS_RESIDENT = 2048
PACK_ROWS = 512
S5_GROUPS_PER_BLOCK = LANES // S5_P
S5_STATE_BLOCK = S5_GROUPS_PER_BLOCK * S5_N

HBM_SPEC = pl.BlockSpec(memory_space=pltpu.HBM)
SEM_SPEC = pl.BlockSpec(memory_space=pltpu.SEMAPHORE)
ANY_SPEC = pl.BlockSpec(memory_space=pl.ANY)


def _tile(n, pref, align):
    best = None
    d = align
    while d <= min(n, pref):
        if n % d == 0:
            best = d
        d += align
    return best if best is not None else n


def _params(*sem):
    return pltpu.CompilerParams(dimension_semantics=sem, vmem_limit_bytes=VMEM_LIMIT)


def _dot_nn(a, b):
    return lax.dot_general(a, b, (((1,), (0,)), ((), ())), preferred_element_type=F32)


def _dot_nt(a, b):
    return lax.dot_general(a, b, (((1,), (1,)), ((), ())), preferred_element_type=F32)


def _dot_tn(a, b):
    return lax.dot_general(a, b, (((0,), (0,)), ((), ())), preferred_element_type=F32)


def _colsum(v):
    return jnp.sum(v, axis=0, keepdims=True)


def _rowmean(v):
    return jnp.mean(v, axis=-1, keepdims=True)


def _ffn_up(xb, wg, wu):
    T, D = xb.shape
    F = wg.shape[0]
    tm = _tile(T, ROWS_RESIDENT, 16)
    tn = _tile(F, MXU_COLS, LANES)

    def body(x_ref, wg_ref, wu_ref, g_ref, u_ref, h_ref):
        x = x_ref[...]
        g = _dot_nt(x, wg_ref[...])
        u = _dot_nt(x, wu_ref[...])
        g_ref[...] = g.astype(BF16)
        u_ref[...] = u.astype(BF16)
        h_ref[...] = (g * jax.nn.sigmoid(g) * u).astype(BF16)

    w_spec = pl.BlockSpec((tn, D), lambda j, i: (j, 0))
    o_spec = pl.BlockSpec((tm, tn), lambda j, i: (i, j))
    return pl.pallas_call(
        body, name="ffn_up", grid=(F // tn, T // tm),
        in_specs=[pl.BlockSpec((tm, D), lambda j, i: (i, 0)), w_spec, w_spec],
        out_specs=[o_spec, o_spec, o_spec],
        out_shape=[jax.ShapeDtypeStruct((T, F), BF16)] * 3,
        compiler_params=_params("arbitrary", "arbitrary"),
    )(xb, wg, wu)


def _mm_acc(pairs, after=()):
    T, K = pairs[0][0].shape
    D = pairs[0][1].shape[1]
    n = len(pairs)
    tk = _tile(K, 512, LANES)
    tm = _tile(T, 512, 16)

    def body(*refs):
        o_ref = refs[-1]

        @pl.when(pl.program_id(0) == 0)
        def _():
            o_ref[...] = jnp.zeros_like(o_ref)

        for r in range(0, T, tm):
            part = _dot_nn(refs[0][r:r + tm, :], refs[1][...])
            for a_ref, w_ref in zip(refs[2:2 * n:2], refs[3:2 * n:2]):
                part += _dot_nn(a_ref[r:r + tm, :], w_ref[...])
            o_ref[r:r + tm, :] += part

    in_specs, operands = [], []
    for a, w in pairs:
        in_specs += [pl.BlockSpec((T, tk), lambda k: (0, k)), pl.BlockSpec((tk, D), lambda k: (k, 0))]
        operands += [a, w]
    return pl.pallas_call(
        body, name="mm_acc", grid=(K // tk,),
        in_specs=in_specs + [ANY_SPEC] * len(after),
        out_specs=pl.BlockSpec((T, D), lambda k: (0, 0)),
        out_shape=jax.ShapeDtypeStruct((T, D), F32),
        compiler_params=_params("arbitrary"),
    )(*operands, *after)


def _mm_res_ln(a, w, res, g, b, scale, alpha, after=()):
    acc = _mm_acc([(a, w)], after)
    T, D = acc.shape
    tm = _tile(T, 256, 16)

    def body(acc_ref, res_ref, g_ref, b_ref, xo_ref, xb_ref, xh_ref, rstd_ref):
        r = alpha * res_ref[...] + scale * acc_ref[...]
        xc = r - _rowmean(r)
        rstd = lax.rsqrt(_rowmean(xc * xc) + LN_EPS)
        xh = xc * rstd
        xo = xh * g_ref[...] + b_ref[...]
        xo_ref[...] = xo
        xb_ref[...] = xo.astype(BF16)
        xh_ref[...] = xh
        rstd_ref[...] = rstd

    row = pl.BlockSpec((tm, D), lambda i: (i, 0))
    vec = pl.BlockSpec((1, D), lambda i: (0, 0))
    return pl.pallas_call(
        body, name="res_ln", grid=(T // tm,),
        in_specs=[row, row, vec, vec],
        out_specs=[row, row, row, pl.BlockSpec((tm, 1), lambda i: (i, 0))],
        out_shape=[jax.ShapeDtypeStruct((T, D), F32), jax.ShapeDtypeStruct((T, D), BF16),
                   jax.ShapeDtypeStruct((T, D), F32), jax.ShapeDtypeStruct((T, 1), F32)],
        compiler_params=_params("arbitrary"),
    )(acc, res, g, b)


def _mm_nt(a, w):
    M, K = a.shape
    N = w.shape[0]
    tm = _tile(M, ROWS_RESIDENT, 16)
    tn = _tile(N, MXU_COLS, LANES)

    def body(a_ref, w_ref, o_ref):
        o_ref[...] = _dot_nt(a_ref[...], w_ref[...])

    return pl.pallas_call(
        body, name="mm_nt", grid=(N // tn, M // tm),
        in_specs=[pl.BlockSpec((tm, K), lambda j, i: (i, 0)), pl.BlockSpec((tn, K), lambda j, i: (j, 0))],
        out_specs=pl.BlockSpec((tm, tn), lambda j, i: (i, j)),
        out_shape=jax.ShapeDtypeStruct((M, N), F32),
        compiler_params=_params("arbitrary", "arbitrary"),
    )(a, w)


def _mm_tn(a, b, scale, out_dtype):
    T, M = a.shape
    N = b.shape[1]
    tm = _tile(M, 512, LANES)
    tn = _tile(N, ROWS_RESIDENT, LANES)

    def body(a_ref, b_ref, o_ref):
        o_ref[...] = (scale * _dot_tn(a_ref[...], b_ref[...])).astype(out_dtype)

    return pl.pallas_call(
        body, name="mm_tn", grid=(M // tm, N // tn),
        in_specs=[pl.BlockSpec((T, tm), lambda i, j: (0, i)), pl.BlockSpec((T, tn), lambda i, j: (0, j))],
        out_specs=pl.BlockSpec((tm, tn), lambda i, j: (i, j)),
        out_shape=jax.ShapeDtypeStruct((M, N), out_dtype),
        compiler_params=_params("arbitrary", "arbitrary"),
    )(a, b)


def _ln_bwd(dy, xh, rstd, g):
    dxh = dy * g
    dr = rstd * (dxh - _rowmean(dxh) - xh * _rowmean(dxh * xh))
    return dr, _colsum(dy * xh), _colsum(dy)


def _loss_ln_bwd(y, target, xh, rstd, g):
    T, D = y.shape
    tm = _tile(T, 256, 16)

    def body(y_ref, t_ref, xh_ref, rstd_ref, g_ref, dr_ref, drb_ref, dg_ref, db_ref, loss_ref):
        i = pl.program_id(0)

        @pl.when(i == 0)
        def _():
            dg_ref[...] = jnp.zeros_like(dg_ref)
            db_ref[...] = jnp.zeros_like(db_ref)
            loss_ref[...] = jnp.zeros_like(loss_ref)

        err = y_ref[...] - t_ref[...]
        loss_ref[...] += (0.5 / D) * _colsum(jnp.sum(err * err, axis=1, keepdims=True))
        dr, dg, db = _ln_bwd(err * (1.0 / D), xh_ref[...], rstd_ref[...], g_ref[...])
        dr_ref[...] = dr
        drb_ref[...] = dr.astype(BF16)
        dg_ref[...] += dg
        db_ref[...] += db

    row = pl.BlockSpec((tm, D), lambda i: (i, 0))
    vec = pl.BlockSpec((1, D), lambda i: (0, 0))
    return pl.pallas_call(
        body, name="loss_ln_bwd", grid=(T // tm,),
        in_specs=[row, row, row, pl.BlockSpec((tm, 1), lambda i: (i, 0)), vec],
        out_specs=[row, row, vec, vec, pl.BlockSpec((1, 1), lambda i: (0, 0))],
        out_shape=[jax.ShapeDtypeStruct((T, D), F32), jax.ShapeDtypeStruct((T, D), BF16),
                   jax.ShapeDtypeStruct((1, D), F32), jax.ShapeDtypeStruct((1, D), F32),
                   jax.ShapeDtypeStruct((1, 1), F32)],
        compiler_params=_params("arbitrary"),
    )(y, target, xh, rstd, g)


def _ffn_down_bwd(drb, wd, gpre, upre):
    T, D = drb.shape
    F = wd.shape[0]
    tm = _tile(T, ROWS_RESIDENT, 16)
    tn = _tile(F, MXU_COLS, LANES)

    def body(dr_ref, wd_ref, g_ref, u_ref, dg_ref, du_ref):
        dh = 0.5 * _dot_nt(dr_ref[...], wd_ref[...])
        g = g_ref[...].astype(F32)
        u = u_ref[...].astype(F32)
        sg = jax.nn.sigmoid(g)
        du_ref[...] = (dh * (g * sg)).astype(BF16)
        dg_ref[...] = (dh * u * (sg * (1.0 + g * (1.0 - sg)))).astype(BF16)

    t_spec = pl.BlockSpec((tm, tn), lambda j, i: (i, j))
    return pl.pallas_call(
        body, name="ffn_down_bwd", grid=(F // tn, T // tm),
        in_specs=[pl.BlockSpec((tm, D), lambda j, i: (i, 0)), pl.BlockSpec((tn, D), lambda j, i: (j, 0)),
                  t_spec, t_spec],
        out_specs=[t_spec, t_spec],
        out_shape=[jax.ShapeDtypeStruct((T, F), BF16), jax.ShapeDtypeStruct((T, F), BF16)],
        compiler_params=_params("arbitrary", "arbitrary"),
    )(drb, wd, gpre, upre)


def _mm_dx(pairs, res, alpha, ln=None, after=()):
    acc = _mm_acc(pairs, after)
    T, D = acc.shape
    tm = _tile(T, 256, 16)
    with_ln = ln is not None

    def body(acc_ref, res_ref, *refs):
        dx = alpha * res_ref[...] + acc_ref[...]
        if with_ln:
            xh_ref, rstd_ref, g_ref, dr_ref, drb_ref, dg_ref, db_ref = refs

            @pl.when(pl.program_id(0) == 0)
            def _():
                dg_ref[...] = jnp.zeros_like(dg_ref)
                db_ref[...] = jnp.zeros_like(db_ref)

            dr, dg, db = _ln_bwd(dx, xh_ref[...], rstd_ref[...], g_ref[...])
            dr_ref[...] = dr
            drb_ref[...] = dr.astype(BF16)
            dg_ref[...] += dg
            db_ref[...] += db
        else:
            refs[0][...] = dx

    row = pl.BlockSpec((tm, D), lambda i: (i, 0))
    vec = pl.BlockSpec((1, D), lambda i: (0, 0))
    in_specs, operands = [row, row], [acc, res]
    if with_ln:
        in_specs += [row, pl.BlockSpec((tm, 1), lambda i: (i, 0)), vec]
        operands += list(ln)
        out_specs = [row, row, vec, vec]
        out_shape = [jax.ShapeDtypeStruct((T, D), F32), jax.ShapeDtypeStruct((T, D), BF16),
                     jax.ShapeDtypeStruct((1, D), F32), jax.ShapeDtypeStruct((1, D), F32)]
    else:
        out_specs = [row]
        out_shape = [jax.ShapeDtypeStruct((T, D), F32)]
    return pl.pallas_call(
        body, name="dx_ln_bwd" if with_ln else "dx_res", grid=(T // tm,),
        in_specs=in_specs, out_specs=out_specs, out_shape=out_shape,
        compiler_params=_params("arbitrary"),
    )(*operands)


def _s5_discretize(lre, lim, ldt, br, bi):
    dt = jnp.exp(ldt)
    mag = jnp.exp(lre * dt)
    ang = lim * dt
    ar = mag * jnp.cos(ang)
    ai = mag * jnp.sin(ang)
    den = lre * lre + lim * lim
    nr = ar - 1.0
    qr = (nr * lre + ai * lim) / den
    qi = (ai * lre - nr * lim) / den
    bbr = qr[None] * br - qi[None] * bi
    bbi = qr[None] * bi + qi[None] * br
    return ar, ai, bbr, bbi


def _s5_params_fwd(lre, lim, ldt, br, bi):
    def body(lre_ref, lim_ref, ldt_ref, br_ref, bi_ref, ar_ref, ai_ref, bbr_ref, bbi_ref):
        ar, ai, bbr, bbi = _s5_discretize(lre_ref[...], lim_ref[...], ldt_ref[...], br_ref[...], bi_ref[...])
        ar_ref[...] = ar
        ai_ref[...] = ai
        bbr_ref[...] = bbr
        bbi_ref[...] = bbi

    sds = jax.ShapeDtypeStruct
    return pl.pallas_call(
        body, name="s5_params_fwd",
        out_shape=[sds(lre.shape, F32), sds(lre.shape, F32), sds(br.shape, F32), sds(br.shape, F32)],
        compiler_params=pltpu.CompilerParams(vmem_limit_bytes=VMEM_LIMIT),
    )(lre, lim, ldt, br, bi)


def _s5_params_bwd(lre, lim, ldt, br, bi, dar, dai, dbbr, dbbi):
    def body(lre_ref, lim_ref, ldt_ref, br_ref, bi_ref, dar_ref, dai_ref, dbbr_ref, dbbi_ref,
             o_lre, o_lim, o_ldt, o_br, o_bi):
        _, vjp = jax.vjp(_s5_discretize, lre_ref[...], lim_ref[...], ldt_ref[...], br_ref[...], bi_ref[...])
        g = vjp((dar_ref[...], dai_ref[...], dbbr_ref[...], dbbi_ref[...]))
        o_lre[...] = g[0]
        o_lim[...] = g[1]
        o_ldt[...] = g[2]
        o_br[...] = g[3]
        o_bi[...] = g[4]

    sds = jax.ShapeDtypeStruct
    return pl.pallas_call(
        body, name="s5_params_bwd",
        out_shape=[sds(lre.shape, F32), sds(lre.shape, F32), sds(ldt.shape, F32), sds(br.shape, F32),
                   sds(br.shape, F32)],
        compiler_params=pltpu.CompilerParams(vmem_limit_bytes=VMEM_LIMIT),
    )(lre, lim, ldt, br, bi, dar, dai, dbbr, dbbi)


def _cmul(a_re, a_im, b_re, b_im):
    return a_re * b_re - a_im * b_im, a_re * b_im + a_im * b_re


def _s5_tile(i):
    if isinstance(i, int):
        return pl.ds(i * SUBLANES, SUBLANES)
    return pl.ds(pl.multiple_of(i * SUBLANES, SUBLANES), SUBLANES)


def _s5_powers(a_re, a_im, pwr_ref, pwi_ref, seg):
    def step(i, carry):
        p_re, p_im = carry
        pwr_ref[pl.ds(i, 1), :] = p_re
        pwi_ref[pl.ds(i, 1), :] = p_im
        return _cmul(a_re, a_im, p_re, p_im)

    lax.fori_loop(0, seg, step, (a_re, a_im))


def _s5_fwd(proj, bdr, bdi, cdr, cdi, ar, ai, dvec):
    T = proj.shape[0]
    GB, UB, SB = bdr.shape
    tc = _tile(T, 256, SUBLANES * SUBLANES)
    seg = tc // SUBLANES

    def body(u_ref, bdr_ref, bdi_ref, cdr_ref, cdi_ref, ar_ref, ai_ref, d_ref, y_ref, sr_ref, si_ref,
             cr_ref, ci_ref, pwr_ref, pwi_ref, str_ref, sti_ref, up_ref, yp_ref):
        a_re = ar_ref[...]
        a_im = ai_ref[...]

        @pl.when(pl.program_id(1) == 0)
        def _():
            cr_ref[...] = jnp.zeros_like(cr_ref)
            ci_ref[...] = jnp.zeros_like(ci_ref)
            _s5_powers(a_re, a_im, pwr_ref, pwi_ref, seg)

        for i in range(seg):
            up_ref[_s5_tile(i), :] = u_ref[pl.ds(i, SUBLANES, stride=seg), :]
        u = up_ref[...]
        ub = u.astype(BF16)
        sr_ref[...] = _dot_nn(ub, bdr_ref[...])
        si_ref[...] = _dot_nn(ub, bdi_ref[...])

        def local(i, carry):
            p_re, p_im = carry
            rows = _s5_tile(i)
            n_re, n_im = _cmul(a_re, a_im, p_re, p_im)
            n_re, n_im = n_re + sr_ref[rows, :], n_im + si_ref[rows, :]
            sr_ref[rows, :] = n_re
            si_ref[rows, :] = n_im
            return n_re, n_im

        e_re, e_im = lax.fori_loop(1, seg, local, (sr_ref[_s5_tile(0), :], si_ref[_s5_tile(0), :]))
        s_re, s_im = cr_ref[...], ci_ref[...]
        top_re, top_im = pwr_ref[seg - 1:seg, :], pwi_ref[seg - 1:seg, :]
        for j in range(SUBLANES):
            str_ref[j:j + 1, :] = s_re
            sti_ref[j:j + 1, :] = s_im
            n_re, n_im = _cmul(top_re, top_im, s_re, s_im)
            s_re, s_im = n_re + e_re[j:j + 1, :], n_im + e_im[j:j + 1, :]
        cr_ref[...] = s_re
        ci_ref[...] = s_im
        b_re, b_im = str_ref[...], sti_ref[...]

        def fix(i, carry):
            rows = _s5_tile(i)
            f_re, f_im = _cmul(pwr_ref[pl.ds(i, 1), :], pwi_ref[pl.ds(i, 1), :], b_re, b_im)
            sr_ref[rows, :] += f_re
            si_ref[rows, :] += f_im
            return carry

        lax.fori_loop(0, seg, fix, 0)
        yp_ref[...] = (_dot_nn(sr_ref[...].astype(BF16), cdr_ref[...])
                       - _dot_nn(si_ref[...].astype(BF16), cdi_ref[...]) + d_ref[...] * u)
        for i in range(seg):
            y_ref[pl.ds(i, SUBLANES, stride=seg), :] = yp_ref[_s5_tile(i), :]

    return pl.pallas_call(
        body, name="s5_fwd", grid=(GB, T // tc),
        in_specs=[pl.BlockSpec((tc, UB), lambda j, t: (t, j)),
                  pl.BlockSpec((None, UB, SB), lambda j, t: (j, 0, 0)),
                  pl.BlockSpec((None, UB, SB), lambda j, t: (j, 0, 0)),
                  pl.BlockSpec((None, SB, UB), lambda j, t: (j, 0, 0)),
                  pl.BlockSpec((None, SB, UB), lambda j, t: (j, 0, 0)),
                  pl.BlockSpec((1, SB), lambda j, t: (0, j)),
                  pl.BlockSpec((1, SB), lambda j, t: (0, j)),
                  pl.BlockSpec((1, UB), lambda j, t: (0, j))],
        out_specs=[pl.BlockSpec((tc, UB), lambda j, t: (t, j)),
                   pl.BlockSpec((tc, SB), lambda j, t: (t, j)),
                   pl.BlockSpec((tc, SB), lambda j, t: (t, j))],
        out_shape=[jax.ShapeDtypeStruct((T, GB * UB), F32), jax.ShapeDtypeStruct((T, GB * SB), F32),
                   jax.ShapeDtypeStruct((T, GB * SB), F32)],
        scratch_shapes=[pltpu.VMEM((1, SB), F32), pltpu.VMEM((1, SB), F32),
                        pltpu.VMEM((seg, SB), F32), pltpu.VMEM((seg, SB), F32),
                        pltpu.VMEM((SUBLANES, SB), F32), pltpu.VMEM((SUBLANES, SB), F32),
                        pltpu.VMEM((tc, UB), F32), pltpu.VMEM((tc, UB), F32)],
        compiler_params=_params("arbitrary", "arbitrary"),
    )(proj, bdr, bdi, cdr, cdi, ar, ai, dvec)


def _s5_bwd(dy, proj, sr, si, bdr, bdi, cdr, cdi, ar, ai, dvec):
    T = dy.shape[0]
    GB, UB, SB = bdr.shape
    tc = _tile(T, 256, SUBLANES * SUBLANES)
    seg = tc // SUBLANES
    nt = T // tc
    halo_blocks = tc // SUBLANES

    def body(dy_ref, u_ref, sr_ref, si_ref, hr_ref, hi_ref, bdr_ref, bdi_ref, cdr_ref, cdi_ref, ar_ref, ai_ref,
             d_ref, du_ref, dbdr_ref, dbdi_ref, dcdr_ref, dcdi_ref, dar_ref, dai_ref, dd_ref,
             gr_ref, gi_ref, pwr_ref, pwi_ref, rtr_ref, rti_ref, cr_ref, ci_ref, dyp_ref, up_ref, dup_ref):
        step_no = pl.program_id(1)
        first_chunk = step_no == nt - 1
        a_re = ar_ref[...]
        a_im = ai_ref[...]

        @pl.when(step_no == 0)
        def _():
            for ref in (cr_ref, ci_ref, dbdr_ref, dbdi_ref, dcdr_ref, dcdi_ref, dar_ref, dai_ref, dd_ref):
                ref[...] = jnp.zeros_like(ref)
            _s5_powers(a_re, a_im, pwr_ref, pwi_ref, seg)

        for i in range(seg):
            dyp_ref[_s5_tile(i), :] = dy_ref[pl.ds(i, SUBLANES, stride=seg), :]
            up_ref[_s5_tile(i), :] = u_ref[pl.ds(i, SUBLANES, stride=seg), :]
        dy = dyp_ref[...]
        dyb = dy.astype(BF16)
        u = up_ref[...]
        gr_ref[...] = _dot_nt(dyb, cdr_ref[...])
        gi_ref[...] = -_dot_nt(dyb, cdi_ref[...])
        dcdr_ref[...] += _dot_tn(sr_ref[...].astype(BF16), dyb)
        dcdi_ref[...] -= _dot_tn(si_ref[...].astype(BF16), dyb)

        def local(n, carry):
            c_re, c_im = carry
            rows = _s5_tile(seg - 1 - n)
            g_re = gr_ref[rows, :] + a_re * c_re + a_im * c_im
            g_im = gi_ref[rows, :] + a_re * c_im - a_im * c_re
            gr_ref[rows, :] = g_re
            gi_ref[rows, :] = g_im
            return g_re, g_im

        last = _s5_tile(seg - 1)
        m_re, m_im = lax.fori_loop(1, seg, local, (gr_ref[last, :], gi_ref[last, :]))
        top_re, top_im = pwr_ref[seg - 1:seg, :], -pwi_ref[seg - 1:seg, :]
        r_re, r_im = cr_ref[...], ci_ref[...]
        for j in reversed(range(SUBLANES)):
            rtr_ref[j:j + 1, :] = r_re
            rti_ref[j:j + 1, :] = r_im
            n_re, n_im = _cmul(top_re, top_im, r_re, r_im)
            r_re, r_im = n_re + m_re[j:j + 1, :], n_im + m_im[j:j + 1, :]
        cr_ref[...] = r_re
        ci_ref[...] = r_im
        f_re, f_im = rtr_ref[...], rti_ref[...]
        keep = jnp.where(first_chunk, 0.0, 1.0)
        sub = lax.broadcasted_iota(jnp.int32, (SUBLANES, SB), 0)
        before_re = jnp.where(sub == 0, hr_ref[SUBLANES - 1:SUBLANES, :] * keep, pltpu.roll(sr_ref[last, :], 1, 0))
        before_im = jnp.where(sub == 0, hi_ref[SUBLANES - 1:SUBLANES, :] * keep, pltpu.roll(si_ref[last, :], 1, 0))

        def fix(i, p_re, p_im, acc):
            rows = _s5_tile(i)
            k = seg - 1 - i
            c_re, c_im = _cmul(pwr_ref[pl.ds(k, 1), :], -pwi_ref[pl.ds(k, 1), :], f_re, f_im)
            g_re = gr_ref[rows, :] + c_re
            g_im = gi_ref[rows, :] + c_im
            gr_ref[rows, :] = g_re
            gi_ref[rows, :] = g_im
            return acc[0] + p_re * g_re + p_im * g_im, acc[1] + p_re * g_im - p_im * g_re

        zero = jnp.zeros((SUBLANES, SB), F32)
        acc = fix(0, before_re, before_im, (zero, zero))
        acc = lax.fori_loop(
            1, seg, lambda i, acc: fix(i, sr_ref[_s5_tile(i - 1), :], si_ref[_s5_tile(i - 1), :], acc), acc)
        dar_ref[...] += _colsum(acc[0])
        dai_ref[...] += _colsum(acc[1])
        gsr = gr_ref[...].astype(BF16)
        gsi = gi_ref[...].astype(BF16)
        ub = u.astype(BF16)
        dbdr_ref[...] += _dot_tn(ub, gsr)
        dbdi_ref[...] += _dot_tn(ub, gsi)
        dup_ref[...] = _dot_nt(gsr, bdr_ref[...]) + _dot_nt(gsi, bdi_ref[...]) + d_ref[...] * dy
        for i in range(seg):
            du_ref[pl.ds(i, SUBLANES, stride=seg), :] = dup_ref[_s5_tile(i), :]
        dd_ref[...] += _colsum(dy * u)

    def rev(t):
        return nt - 1 - t

    def halo(j, t):
        return (jnp.maximum(rev(t) * halo_blocks - 1, 0), j)

    ublk = pl.BlockSpec((tc, UB), lambda j, t: (rev(t), j))
    sblk = pl.BlockSpec((tc, SB), lambda j, t: (rev(t), j))
    bd_spec = pl.BlockSpec((None, UB, SB), lambda j, t: (j, 0, 0))
    cd_spec = pl.BlockSpec((None, SB, UB), lambda j, t: (j, 0, 0))
    svec = pl.BlockSpec((1, SB), lambda j, t: (0, j))
    uvec = pl.BlockSpec((1, UB), lambda j, t: (0, j))
    sds = jax.ShapeDtypeStruct
    return pl.pallas_call(
        body, name="s5_bwd", grid=(GB, nt),
        in_specs=[ublk, ublk, sblk, sblk, pl.BlockSpec((SUBLANES, SB), halo), pl.BlockSpec((SUBLANES, SB), halo),
                  bd_spec, bd_spec, cd_spec, cd_spec, svec, svec, uvec],
        out_specs=[ublk, bd_spec, bd_spec, cd_spec, cd_spec, svec, svec, uvec],
        out_shape=[sds((T, GB * UB), F32), sds((GB, UB, SB), F32), sds((GB, UB, SB), F32),
                   sds((GB, SB, UB), F32), sds((GB, SB, UB), F32), sds((1, GB * SB), F32),
                   sds((1, GB * SB), F32), sds((1, GB * UB), F32)],
        scratch_shapes=[pltpu.VMEM((tc, SB), F32), pltpu.VMEM((tc, SB), F32),
                        pltpu.VMEM((seg, SB), F32), pltpu.VMEM((seg, SB), F32),
                        pltpu.VMEM((SUBLANES, SB), F32), pltpu.VMEM((SUBLANES, SB), F32),
                        pltpu.VMEM((1, SB), F32), pltpu.VMEM((1, SB), F32),
                        pltpu.VMEM((tc, UB), F32), pltpu.VMEM((tc, UB), F32), pltpu.VMEM((tc, UB), F32)],
        compiler_params=_params("arbitrary", "arbitrary"),
    )(dy, proj, sr, si, sr, si, bdr, bdi, cdr, cdi, ar, ai, dvec)


def _shift_down(v, k):
    rows = lax.broadcasted_iota(jnp.int32, v.shape, 0)
    return jnp.where(rows >= k, pltpu.roll(v, k, 0), 0.0)


def _shift_up(v, k):
    n = v.shape[0]
    rows = lax.broadcasted_iota(jnp.int32, v.shape, 0)
    return jnp.where(rows < n - k, pltpu.roll(v, n - k, 0), 0.0)


def _conv_specs(T, cb, n_s5_blocks, n_conv_blocks):
    gb = pl.BlockSpec((T, cb), lambda j: (0, n_s5_blocks + j))
    gc = pl.BlockSpec((T, cb), lambda j: (0, n_s5_blocks + n_conv_blocks + j))
    hh = pl.BlockSpec((T, cb), lambda j: (0, n_s5_blocks + 2 * n_conv_blocks + j))
    return gb, gc, hh


def _conv_fwd(proj, cw, cbias, d_s5, d_conv):
    T = proj.shape[0]
    cb = _tile(d_conv, 256, LANES)

    def body(gb_ref, gc_ref, hh_ref, w_ref, b_ref, z_ref):
        v = gc_ref[...] * hh_ref[...]
        w = w_ref[...]
        cv = b_ref[...] + w[0:1, :] * _shift_down(v, 2) + w[1:2, :] * _shift_down(v, 1) + w[2:3, :] * v
        z_ref[...] = gb_ref[...] * cv

    gb, gc, hh = _conv_specs(T, cb, d_s5 // cb, d_conv // cb)
    col = pl.BlockSpec((T, cb), lambda j: (0, j))
    return pl.pallas_call(
        body, name="conv_fwd", grid=(d_conv // cb,),
        in_specs=[gb, gc, hh, pl.BlockSpec((CONV_W, cb), lambda j: (0, j)), pl.BlockSpec((1, cb), lambda j: (0, j))],
        out_specs=col, out_shape=jax.ShapeDtypeStruct((T, d_conv), F32),
        compiler_params=_params("arbitrary"),
    )(proj, proj, proj, cw, cbias)


def _conv_bwd(dz, proj, cw, cbias, d_s5, d_conv):
    T = proj.shape[0]
    cb = _tile(d_conv, 256, LANES)

    def body(dz_ref, gb_ref, gc_ref, hh_ref, w_ref, b_ref, dgb_ref, dgc_ref, dhh_ref, dw_ref, db_ref):
        gc = gc_ref[...]
        hh = hh_ref[...]
        dz = dz_ref[...]
        w = w_ref[...]
        v = gc * hh
        v1 = _shift_down(v, 1)
        v2 = _shift_down(v, 2)
        cv = b_ref[...] + w[0:1, :] * v2 + w[1:2, :] * v1 + w[2:3, :] * v
        dgb_ref[...] = (dz * cv).astype(BF16)
        dcv = dz * gb_ref[...]
        dv = w[2:3, :] * dcv + w[1:2, :] * _shift_up(dcv, 1) + w[0:1, :] * _shift_up(dcv, 2)
        dgc_ref[...] = (dv * hh).astype(BF16)
        dhh_ref[...] = (dv * gc).astype(BF16)
        dw_ref[0:1, :] = _colsum(dcv * v2)
        dw_ref[1:2, :] = _colsum(dcv * v1)
        dw_ref[2:3, :] = _colsum(dcv * v)
        db_ref[...] = _colsum(dcv)

    gb, gc, hh = _conv_specs(T, cb, d_s5 // cb, d_conv // cb)
    col = pl.BlockSpec((T, cb), lambda j: (0, j))
    wspec = pl.BlockSpec((CONV_W, cb), lambda j: (0, j))
    bspec = pl.BlockSpec((1, cb), lambda j: (0, j))
    sds = jax.ShapeDtypeStruct
    return pl.pallas_call(
        body, name="conv_bwd", grid=(d_conv // cb,),
        in_specs=[col, gb, gc, hh, wspec, bspec],
        out_specs=[col, col, col, wspec, bspec],
        out_shape=[sds((T, d_conv), BF16), sds((T, d_conv), BF16), sds((T, d_conv), BF16),
                   sds((CONV_W, d_conv), F32), sds((1, d_conv), F32)],
        compiler_params=_params("arbitrary"),
    )(dz, proj, proj, proj, cw, cbias)


def _rms(v, g):
    rstd = lax.rsqrt(_rowmean(v * v) + RMS_EPS)
    return v * rstd * g, rstd


def _rms_bwd(dyn, v, rstd, g):
    w = dyn * g
    return rstd * w - v * (rstd * rstd * rstd) * _rowmean(w * v), _colsum(dyn * v * rstd)


def _mix_post(y, z, wglu, g_s5, g_conv):
    T, C = y.shape
    tm = _tile(T, 256, 16)

    def body(y_ref, z_ref, w_ref, gs_ref, gc_ref, m_ref, gl_ref):
        ge = jax.nn.gelu(y_ref[...])
        gl = _dot_nn(ge.astype(BF16), w_ref[...])
        gl_ref[...] = gl
        yn, _ = _rms(ge * jax.nn.sigmoid(gl), gs_ref[...])
        zn, _ = _rms(z_ref[...], gc_ref[...])
        m_ref[:, 0:C] = yn.astype(BF16)
        m_ref[:, C:2 * C] = zn.astype(BF16)

    row = pl.BlockSpec((tm, C), lambda i: (i, 0))
    vec = pl.BlockSpec((1, C), lambda i: (0, 0))
    return pl.pallas_call(
        body, name="mix_post", grid=(T // tm,),
        in_specs=[row, row, pl.BlockSpec((C, C), lambda i: (0, 0)), vec, vec],
        out_specs=[pl.BlockSpec((tm, 2 * C), lambda i: (i, 0)), row],
        out_shape=[jax.ShapeDtypeStruct((T, 2 * C), BF16), jax.ShapeDtypeStruct((T, C), F32)],
        compiler_params=_params("arbitrary"),
    )(y, z, wglu, g_s5, g_conv)


def _mix_post_bwd(dm, y, gl, z, wglu, g_s5, g_conv):
    T, C = y.shape
    tm = _tile(T, 256, 16)

    def body(dm_ref, y_ref, gl_ref, z_ref, w_ref, gs_ref, gc_ref, dy_ref, dz_ref, dw_ref, dgs_ref, dgc_ref):
        @pl.when(pl.program_id(0) == 0)
        def _():
            dw_ref[...] = jnp.zeros_like(dw_ref)
            dgs_ref[...] = jnp.zeros_like(dgs_ref)
            dgc_ref[...] = jnp.zeros_like(dgc_ref)

        yv = y_ref[...]
        ge, gelu_vjp = jax.vjp(jax.nn.gelu, yv)
        gl = gl_ref[...]
        sg = jax.nn.sigmoid(gl)
        y2 = ge * sg
        _, rstd_y = _rms(y2, gs_ref[...])
        dy2, dgs = _rms_bwd(dm_ref[:, 0:C], y2, rstd_y, gs_ref[...])
        dgs_ref[...] += dgs
        dgl = (dy2 * ge * sg * (1.0 - sg)).astype(BF16)
        dge = dy2 * sg + _dot_nt(dgl, w_ref[...])
        dw_ref[...] += _dot_tn(ge.astype(BF16), dgl)
        dy_ref[...] = gelu_vjp(dge)[0]
        zv = z_ref[...]
        _, rstd_z = _rms(zv, gc_ref[...])
        dz, dgc = _rms_bwd(dm_ref[:, C:2 * C], zv, rstd_z, gc_ref[...])
        dz_ref[...] = dz
        dgc_ref[...] += dgc

    row = pl.BlockSpec((tm, C), lambda i: (i, 0))
    vec = pl.BlockSpec((1, C), lambda i: (0, 0))
    full = pl.BlockSpec((C, C), lambda i: (0, 0))
    sds = jax.ShapeDtypeStruct
    return pl.pallas_call(
        body, name="mix_post_bwd", grid=(T // tm,),
        in_specs=[pl.BlockSpec((tm, 2 * C), lambda i: (i, 0)), row, row, row, full, vec, vec],
        out_specs=[row, row, full, vec, vec],
        out_shape=[sds((T, C), F32), sds((T, C), F32), sds((C, C), F32), sds((1, C), F32), sds((1, C), F32)],
        compiler_params=_params("arbitrary"),
    )(dm, y, gl, z, wglu, g_s5, g_conv)


def _adamw(w, g, m, v):
    m = ADAM_B1 * m + (1.0 - ADAM_B1) * g
    v = ADAM_B2 * v + (1.0 - ADAM_B2) * (g * g)
    m_hat = m / (1.0 - ADAM_B1 ** ADAM_STEP)
    v_hat = v / (1.0 - ADAM_B2 ** ADAM_STEP)
    return -ADAM_LR * (m_hat / (jnp.sqrt(v_hat) + ADAM_EPS) + ADAM_WD * w), m, v


def _sum_parts(p_ref):
    total = p_ref[0].astype(F32)
    for d in range(1, p_ref.shape[0]):
        total = total + p_ref[d].astype(F32)
    return total


def _row_tile(R, C, n_streams):
    budget = VMEM_LIMIT // 3 // (n_streams * C * 4)
    return _tile(R, max(BF16_ROWS, budget), BF16_ROWS)


def _reduce_parts(parts):
    L, P, R, C = parts.shape
    tr = _row_tile(R, C, P + 1)

    def body(p_ref, o_ref):
        o_ref[...] = _sum_parts(p_ref)

    return pl.pallas_call(
        body, name="reduce_parts", grid=(L, R // tr),
        in_specs=[pl.BlockSpec((None, P, tr, C), lambda l, i: (l, 0, i, 0))],
        out_specs=pl.BlockSpec((None, tr, C), lambda l, i: (l, i, 0)),
        out_shape=jax.ShapeDtypeStruct((L, R, C), F32),
        compiler_params=_params("arbitrary", "arbitrary"),
    )(parts)


def _adamw_update(w, m, v, grad=None, parts=None):
    L, R, C = w.shape
    from_parts = parts is not None
    P = parts.shape[1] if from_parts else 1
    tr = _row_tile(R, C, P + 7)

    def body(g_in_ref, w_ref, m_ref, v_ref, g_ref, d_ref, nm_ref, nv_ref):
        g = _sum_parts(g_in_ref) if from_parts else g_in_ref[...]
        delta, nm, nv = _adamw(w_ref[...], g, m_ref[...], v_ref[...])
        g_ref[...] = g
        d_ref[...] = delta
        nm_ref[...] = nm
        nv_ref[...] = nv

    blk = pl.BlockSpec((None, tr, C), lambda l, i: (l, i, 0))
    g_spec = pl.BlockSpec((None, P, tr, C), lambda l, i: (l, 0, i, 0)) if from_parts else blk
    out = jax.ShapeDtypeStruct((L, R, C), F32)
    return pl.pallas_call(
        body, name="adamw_parts" if from_parts else "adamw", grid=(L, R // tr),
        in_specs=[g_spec, blk, blk, blk], out_specs=[blk, blk, blk, blk], out_shape=[out, out, out, out],
        compiler_params=_params("arbitrary", "arbitrary"),
    )(parts if from_parts else grad, w, m, v)


def _pair_sum(full, stage):
    _, R, C = full.shape
    tr = _row_tile(R, C, 4)

    def body(f_ref, s_ref, o_ref):
        mine = f_ref[lax.axis_index("c")]
        o_ref[...] = (mine.astype(F32) + s_ref[...].astype(F32)).astype(BF16)

    blk = pl.BlockSpec((None, tr, C), lambda q, i: (q, i, 0))
    return pl.pallas_call(
        body, name="pair_sum", grid=(N_CHIPS, R // tr),
        in_specs=[pl.BlockSpec((None, 2, tr, C), lambda q, i: (q, 0, i, 0)), blk],
        out_specs=blk, out_shape=jax.ShapeDtypeStruct((N_CHIPS, R, C), BF16),
        compiler_params=_params("arbitrary", "arbitrary"),
    )(full.reshape(N_CHIPS, 2, R, C), stage)


def _me():
    x, y, c = (lax.axis_index(a) for a in AXES)
    return x, y, c, 4 * x + 2 * y + c


def _peer(rel):
    x, y, c, _ = _me()
    px = 1 - x if rel & 4 else x
    py = 1 - y if rel & 2 else y
    pc = 1 - c if rel & 1 else c
    return (px, py, pc), 4 * px + 2 * py + pc


DATAFLOW = pltpu.SideEffectType.DATAFLOW_SIDE_EFFECTING


def _remote_copy(src, dst, sems):
    return functools.partial(pltpu.make_async_remote_copy, src_ref=src, dst_ref=dst, **sems)


ALL_PEERS = tuple(range(1, N_DEV))
SIBLING = 1
OTHER_CHIPS = (2, 4, 6)
SIBLING_AND_OTHER_CHIPS = (SIBLING,) + OTHER_CHIPS


def _gather_copies(n, rels=ALL_PEERS):
    def copies(srcs, lands, send_sems, recv_sems, local_sems):
        me = _me()[3]
        local, remote = [], []
        for k in range(n):
            local.append(functools.partial(pltpu.make_async_copy, srcs[k], lands[k].at[me], local_sems.at[k]))
            for rel in rels:
                dev, blk = _peer(rel)
                sems = dict(send_sem=send_sems.at[_sem_index(k, rel)], recv_sem=recv_sems.at[_sem_index(k, rel)],
                            device_id=dev, device_id_type=MESH)
                remote.append((_remote_copy(srcs[k], lands[k].at[me], sems), _remote_copy(srcs[k], lands[k].at[blk], sems)))
        return local, remote

    copies.n_arrays = n
    return copies


def _forward_copies(n):
    def copies(srcs, lands, send_sems, recv_sems, local_sems):
        sibling = _peer(SIBLING)[0]
        remote = []
        for k in range(n):
            for rel in OTHER_CHIPS:
                have = _peer(rel)[1]
                comes = _peer(rel | SIBLING)[1]
                sems = dict(send_sem=send_sems.at[_sem_index(k, rel)], recv_sem=recv_sems.at[_sem_index(k, rel)],
                            device_id=sibling, device_id_type=MESH)
                remote.append((_remote_copy(lands[k].at[have], lands[k].at[have], sems),
                               _remote_copy(lands[k].at[have], lands[k].at[comes], sems)))
        return [], remote

    copies.n_arrays = n
    return copies


def _scatter_copies(n, layer):
    def copies(srcs, lands, send_sems, recv_sems, local_sems):
        me = _me()[3]
        local, remote = [], []
        for k in range(n):
            local.append(functools.partial(pltpu.make_async_copy, srcs[k].at[me], lands[k].at[layer, me],
                                           local_sems.at[k]))
            for rel in range(1, N_DEV):
                dev, blk = _peer(rel)
                sems = dict(send_sem=send_sems.at[_sem_index(k, rel)], recv_sem=recv_sems.at[_sem_index(k, rel)],
                            device_id=dev, device_id_type=MESH)
                remote.append((_remote_copy(srcs[k].at[blk], lands[k].at[layer, me], sems),
                               _remote_copy(srcs[k].at[blk], lands[k].at[layer, blk], sems)))
        return local, remote

    copies.n_arrays = n
    return copies


def _pair_copies(n):
    def copies(srcs, lands, send_sems, recv_sems, local_sems):
        c = _me()[2]
        sibling = _peer(SIBLING)[0]
        remote = []
        for k in range(n):
            for chip in range(N_CHIPS):
                sems = dict(send_sem=send_sems.at[_sem_index(k, chip + 1)], recv_sem=recv_sems.at[_sem_index(k, chip + 1)],
                            device_id=sibling, device_id_type=MESH)
                block = srcs[k].at[2 * chip + 1 - c]
                remote.append((_remote_copy(block, lands[k].at[chip], sems), _remote_copy(block, lands[k].at[chip], sems)))
        return [], remote

    copies.n_arrays = n
    return copies


def _chip_scatter_copies(n, layer):
    def copies(srcs, lands, send_sems, recv_sems, local_sems):
        x, y, _, _ = _me()
        my_chip = 2 * x + y
        local, remote = [], []
        for k in range(n):
            local.append(functools.partial(pltpu.make_async_copy, srcs[k].at[my_chip], lands[k].at[layer, my_chip],
                                           local_sems.at[k]))
            for rel in OTHER_CHIPS:
                dev = _peer(rel)[0]
                chip = 2 * dev[0] + dev[1]
                sems = dict(send_sem=send_sems.at[_sem_index(k, rel)], recv_sem=recv_sems.at[_sem_index(k, rel)],
                            device_id=dev, device_id_type=MESH)
                remote.append((_remote_copy(srcs[k].at[chip], lands[k].at[layer, my_chip], sems),
                               _remote_copy(srcs[k].at[chip], lands[k].at[layer, chip], sems)))
        return local, remote

    copies.n_arrays = n
    return copies


def _sem_shapes(n):
    return [pltpu.SemaphoreType.DMA((n * (N_DEV - 1),)), pltpu.SemaphoreType.DMA((n * (N_DEV - 1),)),
            pltpu.SemaphoreType.DMA((n,))]


def _sem_index(k, rel):
    return k * (N_DEV - 1) + rel - 1


def _exchange(copies, name, srcs, lands):
    n_src, n_land = len(srcs), len(lands)

    def body(*refs):
        src_refs = refs[:n_src]
        land_refs = refs[n_src + n_land:n_src + 2 * n_land]
        local, remote = copies(src_refs, land_refs, *refs[n_src + 2 * n_land:])
        local = [cp() for cp in local]
        sends = [send() for send, _ in remote]
        for cp in local + sends:
            cp.start()
        for send, (_, landing) in zip(sends, remote):
            send.wait_send()
            landing().wait_recv()
        for cp in local:
            cp.wait()

    return pl.pallas_call(
        body, name=name, in_specs=[HBM_SPEC] * (n_src + n_land), out_specs=[HBM_SPEC] * n_land,
        out_shape=[jax.ShapeDtypeStruct(b.shape, b.dtype) for b in lands],
        scratch_shapes=_sem_shapes(copies.n_arrays),
        input_output_aliases={n_src + k: k for k in range(n_land)},
        compiler_params=pltpu.CompilerParams(has_side_effects=True),
    )(*srcs, *lands)


def _hbm(arrays):
    return [pltpu.with_memory_space_constraint(a, pltpu.HBM) for a in arrays]


def _exchange_start(copies, name, srcs, lands, after):
    n_src, n_land, n_after = len(srcs), len(lands), len(after)
    n_data = n_src + n_land

    def body(*refs):
        outs = refs[n_data + n_after:]
        local, remote = copies(refs[:n_src], refs[n_src:n_data], *outs[:3])
        for cp in local:
            cp().start()
        for send, _ in remote:
            send().start()
        outs[-1][...] = jnp.zeros_like(outs[-1])

    res = pl.pallas_call(
        body, name=name, in_specs=[HBM_SPEC] * n_data + [ANY_SPEC] * n_after,
        out_specs=[SEM_SPEC] * 3 + [HBM_SPEC] * n_data + [pl.BlockSpec(memory_space=pltpu.VMEM)],
        out_shape=_sem_shapes(copies.n_arrays) + [pltpu.HBM(a.shape, a.dtype) for a in list(srcs) + list(lands)]
        + [jax.ShapeDtypeStruct((SUBLANES, LANES), F32)],
        input_output_aliases={k: 3 + k for k in range(n_data)},
        compiler_params=pltpu.CompilerParams(has_side_effects=DATAFLOW),
    )(*_hbm(list(srcs) + list(lands)), *after)
    return res[:3], res[3:3 + n_src], res[3 + n_src:3 + n_data], res[-1]


def _exchange_wait(copies, name, sems, srcs, lands, after):
    n_src, n_land, n_after = len(srcs), len(lands), len(after)
    n_data = n_src + n_land

    def body(*refs):
        local, remote = copies(refs[:n_src], refs[n_src:n_data], *refs[n_data:n_data + 3])
        for send, landing in remote:
            send().wait_send()
            landing().wait_recv()
        for cp in local:
            cp().wait()

    res = pl.pallas_call(
        body, name=name, in_specs=[HBM_SPEC] * n_data + [SEM_SPEC] * 3 + [ANY_SPEC] * n_after,
        out_specs=[HBM_SPEC] * n_data,
        out_shape=[pltpu.HBM(a.shape, a.dtype) for a in list(srcs) + list(lands)],
        input_output_aliases={k: k for k in range(n_data)},
        compiler_params=pltpu.CompilerParams(has_side_effects=DATAFLOW),
    )(*srcs, *lands, *sems, *after)
    return res[:n_src], res[n_src:]


def _block_diag(blocks, row_major):
    L, GB, g, P, N = blocks.shape
    eye = jnp.eye(g, dtype=blocks.dtype)
    if row_major:
        return jnp.einsum("lbgpn,gh->lbgphn", blocks, eye).reshape(L, GB, g * P, g * N)
    return jnp.einsum("lbgpn,gh->lbhngp", blocks, eye).reshape(L, GB, g * N, g * P)


def _diag_blocks(mat, g, P, N, row_major):
    GB = mat.shape[0]
    eye = jnp.eye(g, dtype=mat.dtype)
    if row_major:
        return jnp.einsum("bgphn,gh->bgpn", mat.reshape(GB, g, P, g, N), eye)
    return jnp.einsum("bhngp,gh->bgpn", mat.reshape(GB, g, N, g, P), eye)


def _pack(arrays, rows_multiple):
    flat = jnp.concatenate([a.reshape(-1).astype(F32) for a in arrays])
    pad = (-flat.shape[0]) % (rows_multiple * LANES)
    return jnp.pad(flat, (0, pad)).reshape(-1, LANES)


def _unpack(packed, shapes):
    flat = packed.reshape(-1)
    out, pos = [], 0
    for s in shapes:
        n = math.prod(s)
        out.append(flat[pos:pos + n].reshape(s))
        pos += n
    return out


SMALL = ["ln1_g", "ln1_b", "s5_lam_re", "s5_lam_im", "s5_log_dt", "s5_b_re", "s5_b_im", "s5_c_re", "s5_c_im", "s5_d",
         "conv_b", "g_s5", "g_conv", "ln2_g", "ln2_b", "ln3_g", "ln3_b"]
S5_B = ["s5_b_re", "s5_b_im"]
WEIGHTS = ["ffn1_gate", "ffn1_up", "ffn1_down", "ln1_g", "ln1_b", "w_in", "s5_lam_re", "s5_lam_im", "s5_log_dt",
           "s5_b_re", "s5_b_im", "s5_c_re", "s5_c_im", "s5_d", "s5_w_glu", "conv_w", "conv_b", "g_s5", "g_conv",
           "w_out", "ln2_g", "ln2_b", "ffn2_gate", "ffn2_up", "ffn2_down", "ln3_g", "ln3_b"]
TRANSPOSED = ["ffn1_gate", "ffn1_up", "w_in", "ffn2_gate", "ffn2_up"]
UPDATED_TRANSPOSED = ["ffn1_gate", "ffn1_up", "ffn2_gate", "ffn2_up"]
GROUPS = {"a": ["ffn1_gate", "ffn1_up", "ffn1_down"], "b": ["w_in", "s5_w_glu", "w_out"],
          "c": ["ffn2_gate", "ffn2_up", "ffn2_down"]}


def _train_step(x, target, w, m, v):
    T, D = x.shape
    L = w["ln1_g"].shape[0]
    alpha = (2.0 * L) ** 0.25
    G = w["s5_log_dt"].shape[1]
    d_s5 = G * S5_P
    d_conv = w["conv_b"].shape[1]
    GB = G // S5_GROUPS_PER_BLOCK
    me = _me()[3]

    def shard(n, l):
        return (jnp.swapaxes(w[n][l], 0, 1) if n in TRANSPOSED else w[n][l]).astype(BF16)

    conv_w_rows = jnp.pad(w["conv_w"], ((0, 0), (0, SUBLANES - CONV_W), (0, 0)))
    parts = [(l, grp) for l in range(L) for grp in GROUPS]
    step_one, step_two, token = {}, {}, []
    for l, grp in parts:
        srcs = [shard(n, l) for n in GROUPS[grp]] + ([conv_w_rows[l]] if grp == "b" else [])
        lands = [lax.empty((N_DEV,) + a.shape, a.dtype) for a in srcs]
        sems, srcs, lands, tok = _exchange_start(_gather_copies(len(srcs), SIBLING_AND_OTHER_CHIPS),
                                                 f"gather_start_{l}{grp}", srcs, lands, token)
        step_one[l, grp] = (sems, srcs, lands)
        token = [tok]

    def forward_on(i, after):
        if i >= len(parts):
            return []
        l, grp = parts[i]
        sems, srcs, lands = step_one[l, grp]
        copies = _gather_copies(len(srcs), SIBLING_AND_OTHER_CHIPS)
        _, lands = _exchange_wait(copies, f"gather_wait_{l}{grp}", sems, srcs, lands, after)
        sems, _, lands, tok = _exchange_start(_forward_copies(len(lands)), f"forward_start_{l}{grp}", [], lands, [])
        step_two[l, grp] = (sems, lands)
        return [tok]

    def gathered(i, after):
        l, grp = parts[i]
        sems, lands = step_two[l, grp]
        _, lands = _exchange_wait(_forward_copies(len(lands)), f"forward_wait_{l}{grp}", sems, [], lands, after)
        full = {n: p.reshape(-1, p.shape[-1]) for n, p in zip(GROUPS[grp], lands)}
        if grp == "b":
            full["conv_w"] = jnp.swapaxes(lands[-1][:, :CONV_W, :], 0, 1).reshape(CONV_W, d_conv)
        return full

    lre = w["s5_lam_re"].reshape(L * G, S5_N)
    lim = w["s5_lam_im"].reshape(L * G, S5_N)
    ldt = w["s5_log_dt"].reshape(L * G, 1)
    b_re = jnp.transpose(w["s5_b_re"], (3, 0, 1, 2)).reshape(S5_P, L * G, S5_N)
    b_im = jnp.transpose(w["s5_b_im"], (3, 0, 1, 2)).reshape(S5_P, L * G, S5_N)
    ab_re, ab_im, bb_re, bb_im = _s5_params_fwd(lre, lim, ldt, b_re, b_im)

    def groups(bb):
        return jnp.transpose(bb.reshape(S5_P, L, GB, S5_GROUPS_PER_BLOCK, S5_N), (1, 2, 3, 0, 4))

    bd_re = _block_diag(groups(bb_re), True).astype(BF16)
    bd_im = _block_diag(groups(bb_im), True).astype(BF16)
    c_shape = (L, GB, S5_GROUPS_PER_BLOCK, S5_P, S5_N)
    cd_re = _block_diag(w["s5_c_re"].reshape(c_shape), False).astype(BF16)
    cd_im = _block_diag(w["s5_c_im"].reshape(c_shape), False).astype(BF16)
    a_re = ab_re.reshape(L, 1, G * S5_N)
    a_im = ab_im.reshape(L, 1, G * S5_N)
    d_vec = w["s5_d"].reshape(L, 1, d_s5)

    def vec(name, l):
        return w[name][l].reshape(1, -1)

    saved, weights = [], []
    x_in, x_in_b = x, x.astype(BF16)
    token = forward_on(0, token)
    for l in range(L):
        i = len(GROUPS) * l
        gw = gathered(i, [x_in] if l else token)
        s = {"x0b": x_in_b}
        s["g1"], s["u1"], s["h1"] = _ffn_up(x_in_b, gw["ffn1_gate"], gw["ffn1_up"])
        x1, s["x1b"], s["xh1"], s["rstd1"] = _mm_res_ln(s["h1"], gw["ffn1_down"], x_in, vec("ln1_g", l),
                                                         vec("ln1_b", l), 0.5, alpha, forward_on(i + 1, [s["h1"]]))
        gw.update(gathered(i + 1, [s["x1b"]]))
        s["proj"] = _mm_nt(s["x1b"], gw["w_in"])
        s["y"], s["sr"], s["si"] = _s5_fwd(s["proj"], bd_re[l], bd_im[l], cd_re[l], cd_im[l], a_re[l], a_im[l],
                                           d_vec[l])
        s["z"] = _conv_fwd(s["proj"], gw["conv_w"], vec("conv_b", l), d_s5, d_conv)
        s["mcat"], s["gl"] = _mix_post(s["y"], s["z"], gw["s5_w_glu"], vec("g_s5", l), vec("g_conv", l))
        x2, s["x2b"], s["xh2"], s["rstd2"] = _mm_res_ln(s["mcat"], gw["w_out"], x1, vec("ln2_g", l),
                                                         vec("ln2_b", l), 1.0, alpha, forward_on(i + 2, [s["mcat"]]))
        gw.update(gathered(i + 2, [s["x2b"]]))
        s["g2"], s["u2"], s["h2"] = _ffn_up(s["x2b"], gw["ffn2_gate"], gw["ffn2_up"])
        x3, x3b, s["xh3"], s["rstd3"] = _mm_res_ln(s["h2"], gw["ffn2_down"], x2, vec("ln3_g", l), vec("ln3_b", l),
                                                   0.5, alpha, forward_on(i + 3, [s["h2"]]))
        saved.append(s)
        weights.append(gw)
        x_in, x_in_b = x3, x3b

    last = saved[L - 1]
    dr, drb, dg, db, loss = _loss_ln_bwd(x_in, target, last["xh3"], last["rstd3"], vec("ln3_g", L - 1))
    small = [dict() for _ in range(L)]
    small[L - 1]["ln3_g"], small[L - 1]["ln3_b"] = dg, db
    bufs = {grp: [lax.empty((L, N_CHIPS) + shard(n, 0).shape, BF16) for n in names] for grp, names in GROUPS.items()}
    scatters = {grp: [] for grp in GROUPS}
    paired = []
    grad_x = None

    def scatter_on(after):
        if not paired:
            return []
        l_, grp, sems, fulls, stages = paired.pop()
        n = len(fulls)
        fulls, stages = _exchange_wait(_pair_copies(n), f"pair_wait_{l_}{grp}", sems, fulls, stages, after)
        sums = [_pair_sum(f, s) for f, s in zip(fulls, stages)]
        sems, sums, bufs[grp], tok = _exchange_start(_chip_scatter_copies(n, l_), f"scatter_start_{l_}{grp}", sums,
                                                     bufs[grp], [])
        scatters[grp].append((l_, sems, sums))
        return [tok]

    small_names = SMALL + ["conv_w"]
    b_shape = (L, G, S5_P, S5_N)
    small_shapes = [b_shape if n in S5_B else w[n].shape for n in SMALL] + [(L, CONV_W, d_conv)]

    def small_scatter_start(small):
        def stack(key):
            return jnp.stack([small[l][key] for l in range(L)])

        d_bb_re = jnp.transpose(stack("d_bb_re").reshape(L * G, S5_P, S5_N), (1, 0, 2))
        d_bb_im = jnp.transpose(stack("d_bb_im").reshape(L * G, S5_P, S5_N), (1, 0, 2))
        g_lre, g_lim, g_ldt, g_bre, g_bim = _s5_params_bwd(
            lre, lim, ldt, b_re, b_im, stack("d_ab_re").reshape(L * G, S5_N), stack("d_ab_im").reshape(L * G, S5_N),
            d_bb_re, d_bb_im)
        part = {n: [small[l][n] for l in range(L)]
                for n in ["ln1_g", "ln1_b", "s5_c_re", "s5_c_im", "s5_d", "conv_b", "g_s5", "g_conv", "ln2_g", "ln2_b",
                          "ln3_g", "ln3_b", "conv_w"]}
        part["s5_lam_re"], part["s5_lam_im"], part["s5_log_dt"] = [g_lre], [g_lim], [g_ldt]
        part["s5_b_re"] = [jnp.transpose(g_bre, (1, 0, 2))]
        part["s5_b_im"] = [jnp.transpose(g_bim, (1, 0, 2))]
        packed = _pack([piece for n in small_names for piece in part[n]], N_DEV * PACK_ROWS)
        rows = packed.shape[0] // N_DEV
        return _exchange_start(_scatter_copies(1, 0), "scatter_small_start", [packed.reshape(N_DEV, rows, LANES)],
                               [lax.empty((1, N_DEV, rows, LANES), F32)], [])

    for l in reversed(range(L)):
        gw, s, sm = weights[l], saved[l], small[l]
        full = {}

        def scatter_start(grp):
            fulls = [full[n].reshape((N_DEV, -1) + full[n].shape[1:]) for n in GROUPS[grp]]
            stages = [lax.empty((N_CHIPS,) + f.shape[1:], BF16) for f in fulls]
            sems, fulls, stages, tok = _exchange_start(_pair_copies(len(fulls)), f"pair_start_{l}{grp}", fulls, stages, [])
            tokens = [tok] + scatter_on([tok])
            paired.append((l, grp, sems, fulls, stages))
            return tokens

        def ffn_bwd(dr, drb, tag, grp, xb_in, ln, after=()):
            dgp, dup = _ffn_down_bwd(drb, gw[f"ffn{tag}_down"], s[f"g{tag}"], s[f"u{tag}"])
            full[f"ffn{tag}_down"] = _mm_tn(s[f"h{tag}"], drb, 0.5, BF16)
            full[f"ffn{tag}_gate"] = _mm_tn(dgp, xb_in, 1.0, BF16)
            full[f"ffn{tag}_up"] = _mm_tn(dup, xb_in, 1.0, BF16)
            return _mm_dx([(dgp, gw[f"ffn{tag}_gate"]), (dup, gw[f"ffn{tag}_up"])], dr, alpha, ln,
                          scatter_start(grp) + list(after))

        dr, drb, sm["ln2_g"], sm["ln2_b"] = ffn_bwd(dr, drb, 2, "c", s["x2b"], (s["xh2"], s["rstd2"], vec("ln2_g", l)))
        dm = _mm_nt(drb, gw["w_out"])
        full["w_out"] = _mm_tn(s["mcat"], drb, 1.0, BF16)
        dy, dz, dwglu, sm["g_s5"], sm["g_conv"] = _mix_post_bwd(dm, s["y"], s["gl"], s["z"], gw["s5_w_glu"],
                                                                vec("g_s5", l), vec("g_conv", l))
        full["s5_w_glu"] = dwglu.astype(BF16)
        du, dbd_re, dbd_im, dcd_re, dcd_im, sm["d_ab_re"], sm["d_ab_im"], sm["s5_d"] = _s5_bwd(
            dy, s["proj"], s["sr"], s["si"], bd_re[l], bd_im[l], cd_re[l], cd_im[l], a_re[l], a_im[l], d_vec[l])
        gsz = (S5_GROUPS_PER_BLOCK, S5_P, S5_N)
        sm["d_bb_re"] = _diag_blocks(dbd_re, *gsz, True)
        sm["d_bb_im"] = _diag_blocks(dbd_im, *gsz, True)
        sm["s5_c_re"] = _diag_blocks(dcd_re, *gsz, False).reshape(G, S5_P, S5_N)
        sm["s5_c_im"] = _diag_blocks(dcd_im, *gsz, False).reshape(G, S5_P, S5_N)
        dgb, dgc, dhh, sm["conv_w"], sm["conv_b"] = _conv_bwd(dz, s["proj"], gw["conv_w"], vec("conv_b", l),
                                                              d_s5, d_conv)
        dproj = jnp.concatenate([du.astype(BF16), dgb, dgc, dhh], axis=1)
        full["w_in"] = _mm_tn(dproj, s["x1b"], 1.0, BF16)
        dr, drb, sm["ln1_g"], sm["ln1_b"] = _mm_dx([(dproj, gw["w_in"])], dr, alpha,
                                                   (s["xh1"], s["rstd1"], vec("ln1_g", l)), scatter_start("b"))
        if l > 0:
            prev = saved[l - 1]
            dr, drb, small[l - 1]["ln3_g"], small[l - 1]["ln3_b"] = ffn_bwd(
                dr, drb, 1, "a", s["x0b"], (prev["xh3"], prev["rstd3"], vec("ln3_g", l - 1)))
        else:
            small_sems, small_srcs, small_lands, tok = small_scatter_start(small)
            (grad_x,) = ffn_bwd(dr, drb, 1, "a", s["x0b"], None, [tok])
    scatter_on([grad_x])

    _, landed = _exchange_wait(_scatter_copies(1, 0), "scatter_small_wait", small_sems, small_srcs, small_lands,
                               [grad_x])
    mine = _reduce_parts(landed[0])
    (summed,) = _exchange(_gather_copies(1), "gather_small", [mine[0]], [lax.empty(landed[0].shape[1:], F32)])
    small_grads = dict(zip(small_names, _unpack(summed, small_shapes)))

    out = {}

    def update(name, w3, m3, v3, shape, **grad):
        res = _adamw_update(w3, m3, v3, **grad)
        out[name] = [r.reshape(shape) for r in res]

    for grp in ("c", "b", "a"):
        for l, sems, sums in scatters[grp]:
            _, bufs[grp] = _exchange_wait(_chip_scatter_copies(len(sums), l), f"scatter_wait_{l}{grp}", sems, sums,
                                          bufs[grp], [grad_x])
        for n, parts in zip(GROUPS[grp], bufs[grp]):
            if n in UPDATED_TRANSPOSED:
                res = _adamw_update(*(jnp.swapaxes(a[n], 1, 2) for a in (w, m, v)), parts=parts)
                out[n] = [jnp.swapaxes(r, 1, 2) for r in res]
            elif n in TRANSPOSED:
                update(n, w[n], m[n], v[n], w[n].shape, grad=jnp.swapaxes(_reduce_parts(parts), 1, 2))
            else:
                update(n, w[n], m[n], v[n], w[n].shape, parts=parts)
    cw_shape = w["conv_w"].shape
    g_cw = lax.dynamic_slice_in_dim(small_grads["conv_w"], me * cw_shape[2], cw_shape[2], axis=2)
    update("conv_w", w["conv_w"], m["conv_w"], v["conv_w"], cw_shape, grad=g_cw)
    for n in SMALL:
        if n in S5_B:
            def view(a):
                return jnp.swapaxes(a, 2, 3).reshape(1, -1, S5_N)
            res = _adamw_update(view(w[n]), view(m[n]), view(v[n]), grad=small_grads[n].reshape(1, -1, S5_N))
            out[n] = [jnp.swapaxes(r.reshape(b_shape), 2, 3) for r in res]
        else:
            update(n, *(a[n].reshape(1, -1, a[n].shape[-1]) for a in (w, m, v)), w[n].shape,
                   grad=small_grads[n].reshape(1, -1, w[n].shape[-1]))

    loss = lax.psum(loss[0, 0], AXES)
    return loss, grad_x, out


def kernel(x, ffn1_gate, ffn1_up, ffn1_down, ln1_g, ln1_b, w_in, s5_lam_re, s5_lam_im, s5_log_dt, s5_b_re, s5_b_im, s5_c_re, s5_c_im, s5_d, s5_w_glu, conv_w, conv_b, g_s5, g_conv, w_out, ln2_g, ln2_b, ffn2_gate, ffn2_up, ffn2_down, ln3_g, ln3_b, loss_target, m_ffn1_gate, m_ffn1_up, m_ffn1_down, m_ln1_g, m_ln1_b, m_w_in, m_s5_lam_re, m_s5_lam_im, m_s5_log_dt, m_s5_b_re, m_s5_b_im, m_s5_c_re, m_s5_c_im, m_s5_d, m_s5_w_glu, m_conv_w, m_conv_b, m_g_s5, m_g_conv, m_w_out, m_ln2_g, m_ln2_b, m_ffn2_gate, m_ffn2_up, m_ffn2_down, m_ln3_g, m_ln3_b, v_ffn1_gate, v_ffn1_up, v_ffn1_down, v_ln1_g, v_ln1_b, v_w_in, v_s5_lam_re, v_s5_lam_im, v_s5_log_dt, v_s5_b_re, v_s5_b_im, v_s5_c_re, v_s5_c_im, v_s5_d, v_s5_w_glu, v_conv_w, v_conv_b, v_g_s5, v_g_conv, v_w_out, v_ln2_g, v_ln2_b, v_ffn2_gate, v_ffn2_up, v_ffn2_down, v_ln3_g, v_ln3_b):
    given = dict(locals())
    w = {n: given[n] for n in WEIGHTS}
    m = {n: given["m_" + n] for n in WEIGHTS}
    v = {n: given["v_" + n] for n in WEIGHTS}
    T, D = x.shape[-2:]
    loss, grad_x, out = _train_step(x.reshape(T, D), loss_target.reshape(T, D), w, m, v)
    results = [loss, grad_x.reshape(x.shape)]
    for i in range(4):
        results += [out[n][i] for n in WEIGHTS]
    return tuple(results)
```

```python
import functools
import math

import jax
import jax.numpy as jnp
from jax import lax
from jax.experimental import pallas as pl
from jax.experimental.pallas import tpu as pltpu

F32 = jnp.float32
BF16 = jnp.bfloat16
MESH = pl.DeviceIdType.MESH
AXES = ("x", "y", "c")
N_DEV = 8
N_CHIPS = 4

S5_P = 16
S5_N = 64
CONV_W = 3
LN_EPS = 1e-5
RMS_EPS = 1e-6
ADAM_LR = 0.001
ADAM_B1 = 0.9
ADAM_B2 = 0.999
ADAM_EPS = 1e-08
ADAM_WD = 0.01
ADAM_STEP = 10

V7X_VMEM_BYTES = 64 * 1024 * 1024
VMEM_LIMIT = V7X_VMEM_BYTES * 7 // 8
LANES = 128
SUBLANES = 8
BF16_ROWS = 16
MXU_COLS = 256
ROWS_RESIDENT = 2048
PACK_ROWS = 512
S5_GROUPS_PER_BLOCK = LANES // S5_P
S5_STATE_BLOCK = S5_GROUPS_PER_BLOCK * S5_N

HBM_SPEC = pl.BlockSpec(memory_space=pltpu.HBM)
SEM_SPEC = pl.BlockSpec(memory_space=pltpu.SEMAPHORE)
ANY_SPEC = pl.BlockSpec(memory_space=pl.ANY)


def _tile(n, pref, align):
    best = None
    d = align
    while d <= min(n, pref):
        if n % d == 0:
            best = d
        d += align
    return best if best is not None else n


def _params(*sem):
    return pltpu.CompilerParams(dimension_semantics=sem, vmem_limit_bytes=VMEM_LIMIT)


def _dot_nn(a, b):
    return lax.dot_general(a, b, (((1,), (0,)), ((), ())), preferred_element_type=F32)


def _dot_nt(a, b):
    return lax.dot_general(a, b, (((1,), (1,)), ((), ())), preferred_element_type=F32)


def _dot_tn(a, b):
    return lax.dot_general(a, b, (((0,), (0,)), ((), ())), preferred_element_type=F32)


def _colsum(v):
    return jnp.sum(v, axis=0, keepdims=True)


def _rowmean(v):
    return jnp.mean(v, axis=-1, keepdims=True)


def _ffn_up(xb, wg, wu):
    T, D = xb.shape
    F = wg.shape[0]
    tm = _tile(T, ROWS_RESIDENT, 16)
    tn = _tile(F, MXU_COLS, LANES)

    def body(x_ref, wg_ref, wu_ref, g_ref, u_ref, h_ref):
        x = x_ref[...]
        g = _dot_nt(x, wg_ref[...])
        u = _dot_nt(x, wu_ref[...])
        g_ref[...] = g.astype(BF16)
        u_ref[...] = u.astype(BF16)
        h_ref[...] = (g * jax.nn.sigmoid(g) * u).astype(BF16)

    w_spec = pl.BlockSpec((tn, D), lambda j, i: (j, 0))
    o_spec = pl.BlockSpec((tm, tn), lambda j, i: (i, j))
    return pl.pallas_call(
        body, name="ffn_up", grid=(F // tn, T // tm),
        in_specs=[pl.BlockSpec((tm, D), lambda j, i: (i, 0)), w_spec, w_spec],
        out_specs=[o_spec, o_spec, o_spec],
        out_shape=[jax.ShapeDtypeStruct((T, F), BF16)] * 3,
        compiler_params=_params("arbitrary", "arbitrary"),
    )(xb, wg, wu)


def _mm_acc(pairs, after=()):
    T, K = pairs[0][0].shape
    D = pairs[0][1].shape[1]
    n = len(pairs)
    tk = _tile(K, 512, LANES)
    tm = _tile(T, 512, 16)

    def body(*refs):
        o_ref = refs[-1]

        @pl.when(pl.program_id(0) == 0)
        def _():
            o_ref[...] = jnp.zeros_like(o_ref)

        for r in range(0, T, tm):
            part = _dot_nn(refs[0][r:r + tm, :], refs[1][...])
            for a_ref, w_ref in zip(refs[2:2 * n:2], refs[3:2 * n:2]):
                part += _dot_nn(a_ref[r:r + tm, :], w_ref[...])
            o_ref[r:r + tm, :] += part

    in_specs, operands = [], []
    for a, w in pairs:
        in_specs += [pl.BlockSpec((T, tk), lambda k: (0, k)), pl.BlockSpec((tk, D), lambda k: (k, 0))]
        operands += [a, w]
    return pl.pallas_call(
        body, name="mm_acc", grid=(K // tk,),
        in_specs=in_specs + [ANY_SPEC] * len(after),
        out_specs=pl.BlockSpec((T, D), lambda k: (0, 0)),
        out_shape=jax.ShapeDtypeStruct((T, D), F32),
        compiler_params=_params("arbitrary"),
    )(*operands, *after)


def _mm_res_ln(a, w, res, g, b, scale, alpha, after=()):
    acc = _mm_acc([(a, w)], after)
    T, D = acc.shape
    tm = _tile(T, 256, 16)

    def body(acc_ref, res_ref, g_ref, b_ref, xo_ref, xb_ref, xh_ref, rstd_ref):
        r = alpha * res_ref[...] + scale * acc_ref[...]
        xc = r - _rowmean(r)
        rstd = lax.rsqrt(_rowmean(xc * xc) + LN_EPS)
        xh = xc * rstd
        xo = xh * g_ref[...] + b_ref[...]
        xo_ref[...] = xo
        xb_ref[...] = xo.astype(BF16)
        xh_ref[...] = xh
        rstd_ref[...] = rstd

    row = pl.BlockSpec((tm, D), lambda i: (i, 0))
    vec = pl.BlockSpec((1, D), lambda i: (0, 0))
    return pl.pallas_call(
        body, name="res_ln", grid=(T // tm,),
        in_specs=[row, row, vec, vec],
        out_specs=[row, row, row, pl.BlockSpec((tm, 1), lambda i: (i, 0))],
        out_shape=[jax.ShapeDtypeStruct((T, D), F32), jax.ShapeDtypeStruct((T, D), BF16),
                   jax.ShapeDtypeStruct((T, D), F32), jax.ShapeDtypeStruct((T, 1), F32)],
        compiler_params=_params("arbitrary"),
    )(acc, res, g, b)


def _mm_nt(a, w, after=()):
    M, K = a.shape
    N = w.shape[0]
    tm = _tile(M, ROWS_RESIDENT, 16)
    tn = _tile(N, MXU_COLS, LANES)

    def body(a_ref, w_ref, *rest):
        rest[-1][...] = _dot_nt(a_ref[...], w_ref[...])

    return pl.pallas_call(
        body, name="mm_nt", grid=(N // tn, M // tm),
        in_specs=[pl.BlockSpec((tm, K), lambda j, i: (i, 0)), pl.BlockSpec((tn, K), lambda j, i: (j, 0))]
        + [ANY_SPEC] * len(after),
        out_specs=pl.BlockSpec((tm, tn), lambda j, i: (i, j)),
        out_shape=jax.ShapeDtypeStruct((M, N), F32),
        compiler_params=_params("arbitrary", "arbitrary"),
    )(a, w, *after)


def _mm_tn(a, b, scale, out_dtype, half=None, addend=None):
    T, M = a.shape
    N = b.shape[1]
    rows = M if half is None else M // 2
    tm = _tile(rows, 512, LANES)
    tn = _tile(N, ROWS_RESIDENT, LANES)
    first = 0 if half is None else half * (rows // tm)

    def body(a_ref, b_ref, *rest):
        out = scale * _dot_tn(a_ref[...], b_ref[...])
        if addend is not None:
            out = out + rest[0][...].astype(F32)
        rest[-1][...] = out.astype(out_dtype)

    o_spec = pl.BlockSpec((tm, tn), lambda i, j: (i, j))
    return pl.pallas_call(
        body, name="mm_tn", grid=(rows // tm, N // tn),
        in_specs=[pl.BlockSpec((T, tm), lambda i, j: (0, first + i)), pl.BlockSpec((T, tn), lambda i, j: (0, j))]
        + ([] if addend is None else [o_spec]),
        out_specs=o_spec,
        out_shape=jax.ShapeDtypeStruct((rows, N), out_dtype),
        compiler_params=_params("arbitrary", "arbitrary"),
    )(a, b, *([] if addend is None else [addend]))


def _ln_bwd(dy, xh, rstd, g):
    dxh = dy * g
    dr = rstd * (dxh - _rowmean(dxh) - xh * _rowmean(dxh * xh))
    return dr, _colsum(dy * xh), _colsum(dy)


def _loss_ln_bwd(y, target, xh, rstd, g):
    T, D = y.shape
    tm = _tile(T, 256, 16)

    def body(y_ref, t_ref, xh_ref, rstd_ref, g_ref, dr_ref, drb_ref, dg_ref, db_ref, loss_ref):
        i = pl.program_id(0)

        @pl.when(i == 0)
        def _():
            dg_ref[...] = jnp.zeros_like(dg_ref)
            db_ref[...] = jnp.zeros_like(db_ref)
            loss_ref[...] = jnp.zeros_like(loss_ref)

        err = y_ref[...] - t_ref[...]
        loss_ref[...] += (0.5 / D) * _colsum(jnp.sum(err * err, axis=1, keepdims=True))
        dr, dg, db = _ln_bwd(err * (1.0 / D), xh_ref[...], rstd_ref[...], g_ref[...])
        dr_ref[...] = dr
        drb_ref[...] = dr.astype(BF16)
        dg_ref[...] += dg
        db_ref[...] += db

    row = pl.BlockSpec((tm, D), lambda i: (i, 0))
    vec = pl.BlockSpec((1, D), lambda i: (0, 0))
    return pl.pallas_call(
        body, name="loss_ln_bwd", grid=(T // tm,),
        in_specs=[row, row, row, pl.BlockSpec((tm, 1), lambda i: (i, 0)), vec],
        out_specs=[row, row, vec, vec, pl.BlockSpec((1, 1), lambda i: (0, 0))],
        out_shape=[jax.ShapeDtypeStruct((T, D), F32), jax.ShapeDtypeStruct((T, D), BF16),
                   jax.ShapeDtypeStruct((1, D), F32), jax.ShapeDtypeStruct((1, D), F32),
                   jax.ShapeDtypeStruct((1, 1), F32)],
        compiler_params=_params("arbitrary"),
    )(y, target, xh, rstd, g)


def _ffn_down_bwd(drb, wd, gpre, upre, after=()):
    T, D = drb.shape
    F = wd.shape[0]
    tm = _tile(T, ROWS_RESIDENT, 16)
    tn = _tile(F, MXU_COLS, LANES)

    def body(dr_ref, wd_ref, g_ref, u_ref, *rest):
        dg_ref, du_ref = rest[len(after):]
        dh = 0.5 * _dot_nt(dr_ref[...], wd_ref[...])
        g = g_ref[...].astype(F32)
        u = u_ref[...].astype(F32)
        sg = jax.nn.sigmoid(g)
        du_ref[...] = (dh * (g * sg)).astype(BF16)
        dg_ref[...] = (dh * u * (sg * (1.0 + g * (1.0 - sg)))).astype(BF16)

    t_spec = pl.BlockSpec((tm, tn), lambda j, i: (i, j))
    return pl.pallas_call(
        body, name="ffn_down_bwd", grid=(F // tn, T // tm),
        in_specs=[pl.BlockSpec((tm, D), lambda j, i: (i, 0)), pl.BlockSpec((tn, D), lambda j, i: (j, 0)),
                  t_spec, t_spec] + [ANY_SPEC] * len(after),
        out_specs=[t_spec, t_spec],
        out_shape=[jax.ShapeDtypeStruct((T, F), BF16), jax.ShapeDtypeStruct((T, F), BF16)],
        compiler_params=_params("arbitrary", "arbitrary"),
    )(drb, wd, gpre, upre, *after)


def _mm_dx(pairs, res, alpha, ln=None, after=()):
    acc = _mm_acc(pairs, after)
    T, D = acc.shape
    tm = _tile(T, 256, 16)
    with_ln = ln is not None

    def body(acc_ref, res_ref, *refs):
        dx = alpha * res_ref[...] + acc_ref[...]
        if with_ln:
            xh_ref, rstd_ref, g_ref, dr_ref, drb_ref, dg_ref, db_ref = refs

            @pl.when(pl.program_id(0) == 0)
            def _():
                dg_ref[...] = jnp.zeros_like(dg_ref)
                db_ref[...] = jnp.zeros_like(db_ref)

            dr, dg, db = _ln_bwd(dx, xh_ref[...], rstd_ref[...], g_ref[...])
            dr_ref[...] = dr
            drb_ref[...] = dr.astype(BF16)
            dg_ref[...] += dg
            db_ref[...] += db
        else:
            refs[0][...] = dx

    row = pl.BlockSpec((tm, D), lambda i: (i, 0))
    vec = pl.BlockSpec((1, D), lambda i: (0, 0))
    in_specs, operands = [row, row], [acc, res]
    if with_ln:
        in_specs += [row, pl.BlockSpec((tm, 1), lambda i: (i, 0)), vec]
        operands += list(ln)
        out_specs = [row, row, vec, vec]
        out_shape = [jax.ShapeDtypeStruct((T, D), F32), jax.ShapeDtypeStruct((T, D), BF16),
                     jax.ShapeDtypeStruct((1, D), F32), jax.ShapeDtypeStruct((1, D), F32)]
    else:
        out_specs = [row]
        out_shape = [jax.ShapeDtypeStruct((T, D), F32)]
    return pl.pallas_call(
        body, name="dx_ln_bwd" if with_ln else "dx_res", grid=(T // tm,),
        in_specs=in_specs, out_specs=out_specs, out_shape=out_shape,
        compiler_params=_params("arbitrary"),
    )(*operands)


def _s5_discretize(lre, lim, ldt, br, bi):
    dt = jnp.exp(ldt)
    mag = jnp.exp(lre * dt)
    ang = lim * dt
    ar = mag * jnp.cos(ang)
    ai = mag * jnp.sin(ang)
    den = lre * lre + lim * lim
    nr = ar - 1.0
    qr = (nr * lre + ai * lim) / den
    qi = (ai * lre - nr * lim) / den
    bbr = qr[None] * br - qi[None] * bi
    bbi = qr[None] * bi + qi[None] * br
    return ar, ai, bbr, bbi


def _s5_params_fwd(lre, lim, ldt, br, bi):
    def body(lre_ref, lim_ref, ldt_ref, br_ref, bi_ref, ar_ref, ai_ref, bbr_ref, bbi_ref):
        ar, ai, bbr, bbi = _s5_discretize(lre_ref[...], lim_ref[...], ldt_ref[...], br_ref[...], bi_ref[...])
        ar_ref[...] = ar
        ai_ref[...] = ai
        bbr_ref[...] = bbr
        bbi_ref[...] = bbi

    sds = jax.ShapeDtypeStruct
    return pl.pallas_call(
        body, name="s5_params_fwd",
        out_shape=[sds(lre.shape, F32), sds(lre.shape, F32), sds(br.shape, F32), sds(br.shape, F32)],
        compiler_params=pltpu.CompilerParams(vmem_limit_bytes=VMEM_LIMIT),
    )(lre, lim, ldt, br, bi)


def _s5_params_bwd(lre, lim, ldt, br, bi, dar, dai, dbbr, dbbi):
    def body(lre_ref, lim_ref, ldt_ref, br_ref, bi_ref, dar_ref, dai_ref, dbbr_ref, dbbi_ref,
             o_lre, o_lim, o_ldt, o_br, o_bi):
        _, vjp = jax.vjp(_s5_discretize, lre_ref[...], lim_ref[...], ldt_ref[...], br_ref[...], bi_ref[...])
        g = vjp((dar_ref[...], dai_ref[...], dbbr_ref[...], dbbi_ref[...]))
        o_lre[...] = g[0]
        o_lim[...] = g[1]
        o_ldt[...] = g[2]
        o_br[...] = g[3]
        o_bi[...] = g[4]

    sds = jax.ShapeDtypeStruct
    return pl.pallas_call(
        body, name="s5_params_bwd",
        out_shape=[sds(lre.shape, F32), sds(lre.shape, F32), sds(ldt.shape, F32), sds(br.shape, F32),
                   sds(br.shape, F32)],
        compiler_params=pltpu.CompilerParams(vmem_limit_bytes=VMEM_LIMIT),
    )(lre, lim, ldt, br, bi, dar, dai, dbbr, dbbi)


def _cmul(a_re, a_im, b_re, b_im):
    return a_re * b_re - a_im * b_im, a_re * b_im + a_im * b_re


def _s5_tile(i):
    if isinstance(i, int):
        return pl.ds(i * SUBLANES, SUBLANES)
    return pl.ds(pl.multiple_of(i * SUBLANES, SUBLANES), SUBLANES)


def _s5_powers(a_re, a_im, pwr_ref, pwi_ref, seg):
    def step(i, carry):
        p_re, p_im = carry
        pwr_ref[pl.ds(i, 1), :] = p_re
        pwi_ref[pl.ds(i, 1), :] = p_im
        return _cmul(a_re, a_im, p_re, p_im)

    lax.fori_loop(0, seg, step, (a_re, a_im))


def _s5_fwd(proj, bdr, bdi, cdr, cdi, ar, ai, dvec):
    T = proj.shape[0]
    GB, UB, SB = bdr.shape
    tc = _tile(T, 256, SUBLANES * SUBLANES)
    seg = tc // SUBLANES

    def body(u_ref, bdr_ref, bdi_ref, cdr_ref, cdi_ref, ar_ref, ai_ref, d_ref, y_ref, sr_ref, si_ref,
             cr_ref, ci_ref, pwr_ref, pwi_ref, str_ref, sti_ref, up_ref, yp_ref):
        a_re = ar_ref[...]
        a_im = ai_ref[...]

        @pl.when(pl.program_id(1) == 0)
        def _():
            cr_ref[...] = jnp.zeros_like(cr_ref)
            ci_ref[...] = jnp.zeros_like(ci_ref)
            _s5_powers(a_re, a_im, pwr_ref, pwi_ref, seg)

        for i in range(seg):
            up_ref[_s5_tile(i), :] = u_ref[pl.ds(i, SUBLANES, stride=seg), :]
        u = up_ref[...]
        ub = u.astype(BF16)
        sr_ref[...] = _dot_nn(ub, bdr_ref[...])
        si_ref[...] = _dot_nn(ub, bdi_ref[...])

        def local(i, carry):
            p_re, p_im = carry
            rows = _s5_tile(i)
            n_re, n_im = _cmul(a_re, a_im, p_re, p_im)
            n_re, n_im = n_re + sr_ref[rows, :], n_im + si_ref[rows, :]
            sr_ref[rows, :] = n_re
            si_ref[rows, :] = n_im
            return n_re, n_im

        e_re, e_im = lax.fori_loop(1, seg, local, (sr_ref[_s5_tile(0), :], si_ref[_s5_tile(0), :]))
        s_re, s_im = cr_ref[...], ci_ref[...]
        top_re, top_im = pwr_ref[seg - 1:seg, :], pwi_ref[seg - 1:seg, :]
        for j in range(SUBLANES):
            str_ref[j:j + 1, :] = s_re
            sti_ref[j:j + 1, :] = s_im
            n_re, n_im = _cmul(top_re, top_im, s_re, s_im)
            s_re, s_im = n_re + e_re[j:j + 1, :], n_im + e_im[j:j + 1, :]
        cr_ref[...] = s_re
        ci_ref[...] = s_im
        b_re, b_im = str_ref[...], sti_ref[...]

        def fix(i, carry):
            rows = _s5_tile(i)
            f_re, f_im = _cmul(pwr_ref[pl.ds(i, 1), :], pwi_ref[pl.ds(i, 1), :], b_re, b_im)
            sr_ref[rows, :] += f_re
            si_ref[rows, :] += f_im
            return carry

        lax.fori_loop(0, seg, fix, 0)
        yp_ref[...] = (_dot_nn(sr_ref[...].astype(BF16), cdr_ref[...])
                       - _dot_nn(si_ref[...].astype(BF16), cdi_ref[...]) + d_ref[...] * u)
        for i in range(seg):
            y_ref[pl.ds(i, SUBLANES, stride=seg), :] = yp_ref[_s5_tile(i), :]

    return pl.pallas_call(
        body, name="s5_fwd", grid=(GB, T // tc),
        in_specs=[pl.BlockSpec((tc, UB), lambda j, t: (t, j)),
                  pl.BlockSpec((None, UB, SB), lambda j, t: (j, 0, 0)),
                  pl.BlockSpec((None, UB, SB), lambda j, t: (j, 0, 0)),
                  pl.BlockSpec((None, SB, UB), lambda j, t: (j, 0, 0)),
                  pl.BlockSpec((None, SB, UB), lambda j, t: (j, 0, 0)),
                  pl.BlockSpec((1, SB), lambda j, t: (0, j)),
                  pl.BlockSpec((1, SB), lambda j, t: (0, j)),
                  pl.BlockSpec((1, UB), lambda j, t: (0, j))],
        out_specs=[pl.BlockSpec((tc, UB), lambda j, t: (t, j)),
                   pl.BlockSpec((tc, SB), lambda j, t: (t, j)),
                   pl.BlockSpec((tc, SB), lambda j, t: (t, j))],
        out_shape=[jax.ShapeDtypeStruct((T, GB * UB), F32), jax.ShapeDtypeStruct((T, GB * SB), F32),
                   jax.ShapeDtypeStruct((T, GB * SB), F32)],
        scratch_shapes=[pltpu.VMEM((1, SB), F32), pltpu.VMEM((1, SB), F32),
                        pltpu.VMEM((seg, SB), F32), pltpu.VMEM((seg, SB), F32),
                        pltpu.VMEM((SUBLANES, SB), F32), pltpu.VMEM((SUBLANES, SB), F32),
                        pltpu.VMEM((tc, UB), F32), pltpu.VMEM((tc, UB), F32)],
        compiler_params=_params("arbitrary", "arbitrary"),
    )(proj, bdr, bdi, cdr, cdi, ar, ai, dvec)


def _s5_bwd(dy, proj, sr, si, bdr, bdi, cdr, cdi, ar, ai, dvec):
    T = dy.shape[0]
    GB, UB, SB = bdr.shape
    tc = _tile(T, 256, SUBLANES * SUBLANES)
    seg = tc // SUBLANES
    nt = T // tc
    halo_blocks = tc // SUBLANES

    def body(dy_ref, u_ref, sr_ref, si_ref, hr_ref, hi_ref, bdr_ref, bdi_ref, cdr_ref, cdi_ref, ar_ref, ai_ref,
             d_ref, du_ref, dbdr_ref, dbdi_ref, dcdr_ref, dcdi_ref, dar_ref, dai_ref, dd_ref,
             gr_ref, gi_ref, pwr_ref, pwi_ref, rtr_ref, rti_ref, cr_ref, ci_ref, dyp_ref, up_ref, dup_ref):
        step_no = pl.program_id(1)
        first_chunk = step_no == nt - 1
        a_re = ar_ref[...]
        a_im = ai_ref[...]

        @pl.when(step_no == 0)
        def _():
            for ref in (cr_ref, ci_ref, dbdr_ref, dbdi_ref, dcdr_ref, dcdi_ref, dar_ref, dai_ref, dd_ref):
                ref[...] = jnp.zeros_like(ref)
            _s5_powers(a_re, a_im, pwr_ref, pwi_ref, seg)

        for i in range(seg):
            dyp_ref[_s5_tile(i), :] = dy_ref[pl.ds(i, SUBLANES, stride=seg), :]
            up_ref[_s5_tile(i), :] = u_ref[pl.ds(i, SUBLANES, stride=seg), :]
        dy = dyp_ref[...]
        dyb = dy.astype(BF16)
        u = up_ref[...]
        gr_ref[...] = _dot_nt(dyb, cdr_ref[...])
        gi_ref[...] = -_dot_nt(dyb, cdi_ref[...])
        dcdr_ref[...] += _dot_tn(sr_ref[...].astype(BF16), dyb)
        dcdi_ref[...] -= _dot_tn(si_ref[...].astype(BF16), dyb)

        def local(n, carry):
            c_re, c_im = carry
            rows = _s5_tile(seg - 1 - n)
            g_re = gr_ref[rows, :] + a_re * c_re + a_im * c_im
            g_im = gi_ref[rows, :] + a_re * c_im - a_im * c_re
            gr_ref[rows, :] = g_re
            gi_ref[rows, :] = g_im
            return g_re, g_im

        last = _s5_tile(seg - 1)
        m_re, m_im = lax.fori_loop(1, seg, local, (gr_ref[last, :], gi_ref[last, :]))
        top_re, top_im = pwr_ref[seg - 1:seg, :], -pwi_ref[seg - 1:seg, :]
        r_re, r_im = cr_ref[...], ci_ref[...]
        for j in reversed(range(SUBLANES)):
            rtr_ref[j:j + 1, :] = r_re
            rti_ref[j:j + 1, :] = r_im
            n_re, n_im = _cmul(top_re, top_im, r_re, r_im)
            r_re, r_im = n_re + m_re[j:j + 1, :], n_im + m_im[j:j + 1, :]
        cr_ref[...] = r_re
        ci_ref[...] = r_im
        f_re, f_im = rtr_ref[...], rti_ref[...]
        keep = jnp.where(first_chunk, 0.0, 1.0)
        sub = lax.broadcasted_iota(jnp.int32, (SUBLANES, SB), 0)
        before_re = jnp.where(sub == 0, hr_ref[SUBLANES - 1:SUBLANES, :] * keep, pltpu.roll(sr_ref[last, :], 1, 0))
        before_im = jnp.where(sub == 0, hi_ref[SUBLANES - 1:SUBLANES, :] * keep, pltpu.roll(si_ref[last, :], 1, 0))

        def fix(i, p_re, p_im, acc):
            rows = _s5_tile(i)
            k = seg - 1 - i
            c_re, c_im = _cmul(pwr_ref[pl.ds(k, 1), :], -pwi_ref[pl.ds(k, 1), :], f_re, f_im)
            g_re = gr_ref[rows, :] + c_re
            g_im = gi_ref[rows, :] + c_im
            gr_ref[rows, :] = g_re
            gi_ref[rows, :] = g_im
            return acc[0] + p_re * g_re + p_im * g_im, acc[1] + p_re * g_im - p_im * g_re

        zero = jnp.zeros((SUBLANES, SB), F32)
        acc = fix(0, before_re, before_im, (zero, zero))
        acc = lax.fori_loop(
            1, seg, lambda i, acc: fix(i, sr_ref[_s5_tile(i - 1), :], si_ref[_s5_tile(i - 1), :], acc), acc)
        dar_ref[...] += _colsum(acc[0])
        dai_ref[...] += _colsum(acc[1])
        gsr = gr_ref[...].astype(BF16)
        gsi = gi_ref[...].astype(BF16)
        ub = u.astype(BF16)
        dbdr_ref[...] += _dot_tn(ub, gsr)
        dbdi_ref[...] += _dot_tn(ub, gsi)
        dup_ref[...] = _dot_nt(gsr, bdr_ref[...]) + _dot_nt(gsi, bdi_ref[...]) + d_ref[...] * dy
        for i in range(seg):
            du_ref[pl.ds(i, SUBLANES, stride=seg), :] = dup_ref[_s5_tile(i), :]
        dd_ref[...] += _colsum(dy * u)

    def rev(t):
        return nt - 1 - t

    def halo(j, t):
        return (jnp.maximum(rev(t) * halo_blocks - 1, 0), j)

    ublk = pl.BlockSpec((tc, UB), lambda j, t: (rev(t), j))
    sblk = pl.BlockSpec((tc, SB), lambda j, t: (rev(t), j))
    bd_spec = pl.BlockSpec((None, UB, SB), lambda j, t: (j, 0, 0))
    cd_spec = pl.BlockSpec((None, SB, UB), lambda j, t: (j, 0, 0))
    svec = pl.BlockSpec((1, SB), lambda j, t: (0, j))
    uvec = pl.BlockSpec((1, UB), lambda j, t: (0, j))
    sds = jax.ShapeDtypeStruct
    return pl.pallas_call(
        body, name="s5_bwd", grid=(GB, nt),
        in_specs=[ublk, ublk, sblk, sblk, pl.BlockSpec((SUBLANES, SB), halo), pl.BlockSpec((SUBLANES, SB), halo),
                  bd_spec, bd_spec, cd_spec, cd_spec, svec, svec, uvec],
        out_specs=[ublk, bd_spec, bd_spec, cd_spec, cd_spec, svec, svec, uvec],
        out_shape=[sds((T, GB * UB), F32), sds((GB, UB, SB), F32), sds((GB, UB, SB), F32),
                   sds((GB, SB, UB), F32), sds((GB, SB, UB), F32), sds((1, GB * SB), F32),
                   sds((1, GB * SB), F32), sds((1, GB * UB), F32)],
        scratch_shapes=[pltpu.VMEM((tc, SB), F32), pltpu.VMEM((tc, SB), F32),
                        pltpu.VMEM((seg, SB), F32), pltpu.VMEM((seg, SB), F32),
                        pltpu.VMEM((SUBLANES, SB), F32), pltpu.VMEM((SUBLANES, SB), F32),
                        pltpu.VMEM((1, SB), F32), pltpu.VMEM((1, SB), F32),
                        pltpu.VMEM((tc, UB), F32), pltpu.VMEM((tc, UB), F32), pltpu.VMEM((tc, UB), F32)],
        compiler_params=_params("arbitrary", "arbitrary"),
    )(dy, proj, sr, si, sr, si, bdr, bdi, cdr, cdi, ar, ai, dvec)


def _shift_down(v, k):
    rows = lax.broadcasted_iota(jnp.int32, v.shape, 0)
    return jnp.where(rows >= k, pltpu.roll(v, k, 0), 0.0)


def _shift_up(v, k):
    n = v.shape[0]
    rows = lax.broadcasted_iota(jnp.int32, v.shape, 0)
    return jnp.where(rows < n - k, pltpu.roll(v, n - k, 0), 0.0)


def _conv_specs(T, cb, n_s5_blocks, n_conv_blocks):
    gb = pl.BlockSpec((T, cb), lambda j: (0, n_s5_blocks + j))
    gc = pl.BlockSpec((T, cb), lambda j: (0, n_s5_blocks + n_conv_blocks + j))
    hh = pl.BlockSpec((T, cb), lambda j: (0, n_s5_blocks + 2 * n_conv_blocks + j))
    return gb, gc, hh


def _conv_fwd(proj, cw, cbias, d_s5, d_conv):
    T = proj.shape[0]
    cb = _tile(d_conv, 256, LANES)

    def body(gb_ref, gc_ref, hh_ref, w_ref, b_ref, z_ref):
        v = gc_ref[...] * hh_ref[...]
        w = w_ref[...]
        cv = b_ref[...] + w[0:1, :] * _shift_down(v, 2) + w[1:2, :] * _shift_down(v, 1) + w[2:3, :] * v
        z_ref[...] = gb_ref[...] * cv

    gb, gc, hh = _conv_specs(T, cb, d_s5 // cb, d_conv // cb)
    col = pl.BlockSpec((T, cb), lambda j: (0, j))
    return pl.pallas_call(
        body, name="conv_fwd", grid=(d_conv // cb,),
        in_specs=[gb, gc, hh, pl.BlockSpec((CONV_W, cb), lambda j: (0, j)), pl.BlockSpec((1, cb), lambda j: (0, j))],
        out_specs=col, out_shape=jax.ShapeDtypeStruct((T, d_conv), F32),
        compiler_params=_params("arbitrary"),
    )(proj, proj, proj, cw, cbias)


def _conv_bwd(dz, proj, cw, cbias, d_s5, d_conv):
    T = proj.shape[0]
    cb = _tile(d_conv, 256, LANES)

    def body(dz_ref, gb_ref, gc_ref, hh_ref, w_ref, b_ref, dgb_ref, dgc_ref, dhh_ref, dw_ref, db_ref):
        gc = gc_ref[...]
        hh = hh_ref[...]
        dz = dz_ref[...]
        w = w_ref[...]
        v = gc * hh
        v1 = _shift_down(v, 1)
        v2 = _shift_down(v, 2)
        cv = b_ref[...] + w[0:1, :] * v2 + w[1:2, :] * v1 + w[2:3, :] * v
        dgb_ref[...] = (dz * cv).astype(BF16)
        dcv = dz * gb_ref[...]
        dv = w[2:3, :] * dcv + w[1:2, :] * _shift_up(dcv, 1) + w[0:1, :] * _shift_up(dcv, 2)
        dgc_ref[...] = (dv * hh).astype(BF16)
        dhh_ref[...] = (dv * gc).astype(BF16)
        dw_ref[0:1, :] = _colsum(dcv * v2)
        dw_ref[1:2, :] = _colsum(dcv * v1)
        dw_ref[2:3, :] = _colsum(dcv * v)
        db_ref[...] = _colsum(dcv)

    gb, gc, hh = _conv_specs(T, cb, d_s5 // cb, d_conv // cb)
    col = pl.BlockSpec((T, cb), lambda j: (0, j))
    wspec = pl.BlockSpec((CONV_W, cb), lambda j: (0, j))
    bspec = pl.BlockSpec((1, cb), lambda j: (0, j))
    sds = jax.ShapeDtypeStruct
    return pl.pallas_call(
        body, name="conv_bwd", grid=(d_conv // cb,),
        in_specs=[col, gb, gc, hh, wspec, bspec],
        out_specs=[col, col, col, wspec, bspec],
        out_shape=[sds((T, d_conv), BF16), sds((T, d_conv), BF16), sds((T, d_conv), BF16),
                   sds((CONV_W, d_conv), F32), sds((1, d_conv), F32)],
        compiler_params=_params("arbitrary"),
    )(dz, proj, proj, proj, cw, cbias)


def _rms(v, g):
    rstd = lax.rsqrt(_rowmean(v * v) + RMS_EPS)
    return v * rstd * g, rstd


def _rms_bwd(dyn, v, rstd, g):
    w = dyn * g
    return rstd * w - v * (rstd * rstd * rstd) * _rowmean(w * v), _colsum(dyn * v * rstd)


def _mix_post(y, z, wglu, g_s5, g_conv):
    T, C = y.shape
    tm = _tile(T, 256, 16)

    def body(y_ref, z_ref, w_ref, gs_ref, gc_ref, m_ref, gl_ref):
        ge = jax.nn.gelu(y_ref[...])
        gl = _dot_nn(ge.astype(BF16), w_ref[...])
        gl_ref[...] = gl
        yn, _ = _rms(ge * jax.nn.sigmoid(gl), gs_ref[...])
        zn, _ = _rms(z_ref[...], gc_ref[...])
        m_ref[:, 0:C] = yn.astype(BF16)
        m_ref[:, C:2 * C] = zn.astype(BF16)

    row = pl.BlockSpec((tm, C), lambda i: (i, 0))
    vec = pl.BlockSpec((1, C), lambda i: (0, 0))
    return pl.pallas_call(
        body, name="mix_post", grid=(T // tm,),
        in_specs=[row, row, pl.BlockSpec((C, C), lambda i: (0, 0)), vec, vec],
        out_specs=[pl.BlockSpec((tm, 2 * C), lambda i: (i, 0)), row],
        out_shape=[jax.ShapeDtypeStruct((T, 2 * C), BF16), jax.ShapeDtypeStruct((T, C), F32)],
        compiler_params=_params("arbitrary"),
    )(y, z, wglu, g_s5, g_conv)


def _mix_post_bwd(dm, y, gl, z, wglu, g_s5, g_conv):
    T, C = y.shape
    tm = _tile(T, 256, 16)

    def body(dm_ref, y_ref, gl_ref, z_ref, w_ref, gs_ref, gc_ref, dy_ref, dz_ref, dw_ref, dgs_ref, dgc_ref):
        @pl.when(pl.program_id(0) == 0)
        def _():
            dw_ref[...] = jnp.zeros_like(dw_ref)
            dgs_ref[...] = jnp.zeros_like(dgs_ref)
            dgc_ref[...] = jnp.zeros_like(dgc_ref)

        yv = y_ref[...]
        ge, gelu_vjp = jax.vjp(jax.nn.gelu, yv)
        gl = gl_ref[...]
        sg = jax.nn.sigmoid(gl)
        y2 = ge * sg
        _, rstd_y = _rms(y2, gs_ref[...])
        dy2, dgs = _rms_bwd(dm_ref[:, 0:C], y2, rstd_y, gs_ref[...])
        dgs_ref[...] += dgs
        dgl = (dy2 * ge * sg * (1.0 - sg)).astype(BF16)
        dge = dy2 * sg + _dot_nt(dgl, w_ref[...])
        dw_ref[...] += _dot_tn(ge.astype(BF16), dgl)
        dy_ref[...] = gelu_vjp(dge)[0]
        zv = z_ref[...]
        _, rstd_z = _rms(zv, gc_ref[...])
        dz, dgc = _rms_bwd(dm_ref[:, C:2 * C], zv, rstd_z, gc_ref[...])
        dz_ref[...] = dz
        dgc_ref[...] += dgc

    row = pl.BlockSpec((tm, C), lambda i: (i, 0))
    vec = pl.BlockSpec((1, C), lambda i: (0, 0))
    full = pl.BlockSpec((C, C), lambda i: (0, 0))
    sds = jax.ShapeDtypeStruct
    return pl.pallas_call(
        body, name="mix_post_bwd", grid=(T // tm,),
        in_specs=[pl.BlockSpec((tm, 2 * C), lambda i: (i, 0)), row, row, row, full, vec, vec],
        out_specs=[row, row, full, vec, vec],
        out_shape=[sds((T, C), F32), sds((T, C), F32), sds((C, C), F32), sds((1, C), F32), sds((1, C), F32)],
        compiler_params=_params("arbitrary"),
    )(dm, y, gl, z, wglu, g_s5, g_conv)


def _adamw(w, g, m, v):
    m = ADAM_B1 * m + (1.0 - ADAM_B1) * g
    v = ADAM_B2 * v + (1.0 - ADAM_B2) * (g * g)
    m_hat = m / (1.0 - ADAM_B1 ** ADAM_STEP)
    v_hat = v / (1.0 - ADAM_B2 ** ADAM_STEP)
    return -ADAM_LR * (m_hat / (jnp.sqrt(v_hat) + ADAM_EPS) + ADAM_WD * w), m, v


def _sum_parts(p_ref):
    total = p_ref[0].astype(F32)
    for d in range(1, p_ref.shape[0]):
        total = total + p_ref[d].astype(F32)
    return total


def _row_tile(R, C, n_streams):
    budget = VMEM_LIMIT // 3 // (n_streams * C * 4)
    return _tile(R, max(BF16_ROWS, budget), BF16_ROWS)


def _reduce_parts(parts):
    L, P, R, C = parts.shape
    tr = _row_tile(R, C, P + 1)

    def body(p_ref, o_ref):
        o_ref[...] = _sum_parts(p_ref)

    return pl.pallas_call(
        body, name="reduce_parts", grid=(L, R // tr),
        in_specs=[pl.BlockSpec((None, P, tr, C), lambda l, i: (l, 0, i, 0))],
        out_specs=pl.BlockSpec((None, tr, C), lambda l, i: (l, i, 0)),
        out_shape=jax.ShapeDtypeStruct((L, R, C), F32),
        compiler_params=_params("arbitrary", "arbitrary"),
    )(parts)


def _adamw_update(w, m, v, grad=None, parts=None):
    L, R, C = w.shape
    from_parts = parts is not None
    P = parts.shape[1] if from_parts else 1
    tr = _row_tile(R, C, P + 7)

    def body(g_in_ref, w_ref, m_ref, v_ref, g_ref, d_ref, nm_ref, nv_ref):
        g = _sum_parts(g_in_ref) if from_parts else g_in_ref[...]
        delta, nm, nv = _adamw(w_ref[...], g, m_ref[...], v_ref[...])
        g_ref[...] = g
        d_ref[...] = delta
        nm_ref[...] = nm
        nv_ref[...] = nv

    blk = pl.BlockSpec((None, tr, C), lambda l, i: (l, i, 0))
    g_spec = pl.BlockSpec((None, P, tr, C), lambda l, i: (l, 0, i, 0)) if from_parts else blk
    out = jax.ShapeDtypeStruct((L, R, C), F32)
    return pl.pallas_call(
        body, name="adamw_parts" if from_parts else "adamw", grid=(L, R // tr),
        in_specs=[g_spec, blk, blk, blk], out_specs=[blk, blk, blk, blk], out_shape=[out, out, out, out],
        compiler_params=_params("arbitrary", "arbitrary"),
    )(parts if from_parts else grad, w, m, v)


def _pair_sum(full, stage):
    _, R, C = full.shape
    tr = _row_tile(R, C, 4)

    def body(f_ref, s_ref, o_ref):
        mine = f_ref[lax.axis_index("c")]
        o_ref[...] = (mine.astype(F32) + s_ref[...].astype(F32)).astype(BF16)

    blk = pl.BlockSpec((None, tr, C), lambda q, i: (q, i, 0))
    return pl.pallas_call(
        body, name="pair_sum", grid=(N_CHIPS, R // tr),
        in_specs=[pl.BlockSpec((None, 2, tr, C), lambda q, i: (q, 0, i, 0)), blk],
        out_specs=blk, out_shape=jax.ShapeDtypeStruct((N_CHIPS, R, C), BF16),
        compiler_params=_params("arbitrary", "arbitrary"),
    )(full.reshape(N_CHIPS, 2, R, C), stage)


def _me():
    x, y, c = (lax.axis_index(a) for a in AXES)
    return x, y, c, 4 * x + 2 * y + c


def _peer(rel):
    x, y, c, _ = _me()
    px = 1 - x if rel & 4 else x
    py = 1 - y if rel & 2 else y
    pc = 1 - c if rel & 1 else c
    return (px, py, pc), 4 * px + 2 * py + pc


DATAFLOW = pltpu.SideEffectType.DATAFLOW_SIDE_EFFECTING


def _remote_copy(src, dst, sems):
    return functools.partial(pltpu.make_async_remote_copy, src_ref=src, dst_ref=dst, **sems)


ALL_PEERS = tuple(range(1, N_DEV))
SIBLING = 1
OTHER_CHIPS = (2, 4, 6)
SIBLING_AND_OTHER_CHIPS = (SIBLING,) + OTHER_CHIPS


def _slot(dev, blk, same_core, own_core_last):
    if not own_core_last:
        return blk
    return (N_CHIPS if same_core else 0) + 2 * dev[0] + dev[1]


def _gather_copies(n, rels=ALL_PEERS, own_core_last=False):
    def copies(srcs, lands, send_sems, recv_sems, local_sems):
        x, y, c, me = _me()
        local, remote = [], []
        for k in range(n):
            local.append(functools.partial(pltpu.make_async_copy, srcs[k],
                                           lands[k].at[_slot((x, y, c), me, True, own_core_last)], local_sems.at[k]))
            for rel in rels:
                dev, blk = _peer(rel)
                same_core = not rel & SIBLING
                sems = dict(send_sem=send_sems.at[_sem_index(k, rel)], recv_sem=recv_sems.at[_sem_index(k, rel)],
                            device_id=dev, device_id_type=MESH)
                remote.append((_remote_copy(srcs[k], lands[k].at[_slot((x, y, c), me, same_core, own_core_last)], sems),
                               _remote_copy(srcs[k], lands[k].at[_slot(dev, blk, same_core, own_core_last)], sems)))
        return local, remote

    copies.n_arrays = n
    return copies


def _forward_copies(n, own_core_last=False):
    def copies(srcs, lands, send_sems, recv_sems, local_sems):
        sibling = _peer(SIBLING)[0]
        remote = []
        for k in range(n):
            for rel in OTHER_CHIPS:
                dev, blk = _peer(rel)
                have = _slot(dev, blk, True, own_core_last)
                there = _slot(dev, blk, False, own_core_last)
                other = _peer(rel | SIBLING)
                comes = _slot(other[0], other[1], False, own_core_last)
                sems = dict(send_sem=send_sems.at[_sem_index(k, rel)], recv_sem=recv_sems.at[_sem_index(k, rel)],
                            device_id=sibling, device_id_type=MESH)
                remote.append((_remote_copy(lands[k].at[have], lands[k].at[there], sems),
                               _remote_copy(lands[k].at[have], lands[k].at[comes], sems)))
        return [], remote

    copies.n_arrays = n
    return copies


def _scatter_copies(n, layer):
    def copies(srcs, lands, send_sems, recv_sems, local_sems):
        me = _me()[3]
        local, remote = [], []
        for k in range(n):
            local.append(functools.partial(pltpu.make_async_copy, srcs[k].at[me], lands[k].at[layer, me],
                                           local_sems.at[k]))
            for rel in range(1, N_DEV):
                dev, blk = _peer(rel)
                sems = dict(send_sem=send_sems.at[_sem_index(k, rel)], recv_sem=recv_sems.at[_sem_index(k, rel)],
                            device_id=dev, device_id_type=MESH)
                remote.append((_remote_copy(srcs[k].at[blk], lands[k].at[layer, me], sems),
                               _remote_copy(srcs[k].at[blk], lands[k].at[layer, blk], sems)))
        return local, remote

    copies.n_arrays = n
    return copies


def _swap_copies(n):
    def copies(srcs, lands, send_sems, recv_sems, local_sems):
        sibling = _peer(SIBLING)[0]
        remote = []
        for k in range(n):
            sems = dict(send_sem=send_sems.at[_sem_index(k, SIBLING)], recv_sem=recv_sems.at[_sem_index(k, SIBLING)],
                        device_id=sibling, device_id_type=MESH)
            remote.append((_remote_copy(srcs[k], lands[k], sems), _remote_copy(srcs[k], lands[k], sems)))
        return [], remote

    copies.n_arrays = n
    return copies


def _pair_copies(n):
    def copies(srcs, lands, send_sems, recv_sems, local_sems):
        c = _me()[2]
        sibling = _peer(SIBLING)[0]
        remote = []
        for k in range(n):
            for chip in range(N_CHIPS):
                sems = dict(send_sem=send_sems.at[_sem_index(k, chip + 1)], recv_sem=recv_sems.at[_sem_index(k, chip + 1)],
                            device_id=sibling, device_id_type=MESH)
                block = srcs[k].at[2 * chip + 1 - c]
                remote.append((_remote_copy(block, lands[k].at[chip], sems), _remote_copy(block, lands[k].at[chip], sems)))
        return [], remote

    copies.n_arrays = n
    return copies


def _chip_scatter_copies(n, layer):
    def copies(srcs, lands, send_sems, recv_sems, local_sems):
        x, y, _, _ = _me()
        my_chip = 2 * x + y
        local, remote = [], []
        for k in range(n):
            local.append(functools.partial(pltpu.make_async_copy, srcs[k].at[my_chip], lands[k].at[layer, my_chip],
                                           local_sems.at[k]))
            for rel in OTHER_CHIPS:
                dev = _peer(rel)[0]
                chip = 2 * dev[0] + dev[1]
                sems = dict(send_sem=send_sems.at[_sem_index(k, rel)], recv_sem=recv_sems.at[_sem_index(k, rel)],
                            device_id=dev, device_id_type=MESH)
                remote.append((_remote_copy(srcs[k].at[chip], lands[k].at[layer, my_chip], sems),
                               _remote_copy(srcs[k].at[chip], lands[k].at[layer, chip], sems)))
        return local, remote

    copies.n_arrays = n
    return copies


def _sem_shapes(n):
    return [pltpu.SemaphoreType.DMA((n * (N_DEV - 1),)), pltpu.SemaphoreType.DMA((n * (N_DEV - 1),)),
            pltpu.SemaphoreType.DMA((n,))]


def _sem_index(k, rel):
    return k * (N_DEV - 1) + rel - 1


def _exchange(copies, name, srcs, lands):
    n_src, n_land = len(srcs), len(lands)

    def body(*refs):
        src_refs = refs[:n_src]
        land_refs = refs[n_src + n_land:n_src + 2 * n_land]
        local, remote = copies(src_refs, land_refs, *refs[n_src + 2 * n_land:])
        local = [cp() for cp in local]
        sends = [send() for send, _ in remote]
        for cp in local + sends:
            cp.start()
        for send, (_, landing) in zip(sends, remote):
            send.wait_send()
            landing().wait_recv()
        for cp in local:
            cp.wait()

    return pl.pallas_call(
        body, name=name, in_specs=[HBM_SPEC] * (n_src + n_land), out_specs=[HBM_SPEC] * n_land,
        out_shape=[jax.ShapeDtypeStruct(b.shape, b.dtype) for b in lands],
        scratch_shapes=_sem_shapes(copies.n_arrays),
        input_output_aliases={n_src + k: k for k in range(n_land)},
        compiler_params=pltpu.CompilerParams(has_side_effects=True),
    )(*srcs, *lands)


def _hbm(arrays):
    return [pltpu.with_memory_space_constraint(a, pltpu.HBM) for a in arrays]


def _exchange_start(copies, name, srcs, lands, after):
    n_src, n_land, n_after = len(srcs), len(lands), len(after)
    n_data = n_src + n_land

    def body(*refs):
        outs = refs[n_data + n_after:]
        local, remote = copies(refs[:n_src], refs[n_src:n_data], *outs[:3])
        for cp in local:
            cp().start()
        for send, _ in remote:
            send().start()
        outs[-1][...] = jnp.zeros_like(outs[-1])

    res = pl.pallas_call(
        body, name=name, in_specs=[HBM_SPEC] * n_data + [ANY_SPEC] * n_after,
        out_specs=[SEM_SPEC] * 3 + [HBM_SPEC] * n_data + [pl.BlockSpec(memory_space=pltpu.VMEM)],
        out_shape=_sem_shapes(copies.n_arrays) + [pltpu.HBM(a.shape, a.dtype) for a in list(srcs) + list(lands)]
        + [jax.ShapeDtypeStruct((SUBLANES, LANES), F32)],
        input_output_aliases={k: 3 + k for k in range(n_data)},
        compiler_params=pltpu.CompilerParams(has_side_effects=DATAFLOW),
    )(*_hbm(list(srcs) + list(lands)), *after)
    return res[:3], res[3:3 + n_src], res[3 + n_src:3 + n_data], res[-1]


def _exchange_wait(copies, name, sems, srcs, lands, after):
    n_src, n_land, n_after = len(srcs), len(lands), len(after)
    n_data = n_src + n_land

    def body(*refs):
        local, remote = copies(refs[:n_src], refs[n_src:n_data], *refs[n_data:n_data + 3])
        for send, landing in remote:
            send().wait_send()
            landing().wait_recv()
        for cp in local:
            cp().wait()

    res = pl.pallas_call(
        body, name=name, in_specs=[HBM_SPEC] * n_data + [SEM_SPEC] * 3 + [ANY_SPEC] * n_after,
        out_specs=[HBM_SPEC] * n_data,
        out_shape=[pltpu.HBM(a.shape, a.dtype) for a in list(srcs) + list(lands)],
        input_output_aliases={k: k for k in range(n_data)},
        compiler_params=pltpu.CompilerParams(has_side_effects=DATAFLOW),
    )(*srcs, *lands, *sems, *after)
    return res[:n_src], res[n_src:]


def _block_diag(blocks, row_major):
    L, GB, g, P, N = blocks.shape
    eye = jnp.eye(g, dtype=blocks.dtype)
    if row_major:
        return jnp.einsum("lbgpn,gh->lbgphn", blocks, eye).reshape(L, GB, g * P, g * N)
    return jnp.einsum("lbgpn,gh->lbhngp", blocks, eye).reshape(L, GB, g * N, g * P)


def _diag_blocks(mat, g, P, N, row_major):
    GB = mat.shape[0]
    eye = jnp.eye(g, dtype=mat.dtype)
    if row_major:
        return jnp.einsum("bgphn,gh->bgpn", mat.reshape(GB, g, P, g, N), eye)
    return jnp.einsum("bhngp,gh->bgpn", mat.reshape(GB, g, N, g, P), eye)


def _pack(arrays, rows_multiple):
    flat = jnp.concatenate([a.reshape(-1).astype(F32) for a in arrays])
    pad = (-flat.shape[0]) % (rows_multiple * LANES)
    return jnp.pad(flat, (0, pad)).reshape(-1, LANES)


def _unpack(packed, shapes):
    flat = packed.reshape(-1)
    out, pos = [], 0
    for s in shapes:
        n = math.prod(s)
        out.append(flat[pos:pos + n].reshape(s))
        pos += n
    return out


SMALL = ["ln1_g", "ln1_b", "s5_lam_re", "s5_lam_im", "s5_log_dt", "s5_b_re", "s5_b_im", "s5_c_re", "s5_c_im", "s5_d",
         "conv_b", "g_s5", "g_conv", "ln2_g", "ln2_b", "ln3_g", "ln3_b"]
S5_B = ["s5_b_re", "s5_b_im"]
WEIGHTS = ["ffn1_gate", "ffn1_up", "ffn1_down", "ln1_g", "ln1_b", "w_in", "s5_lam_re", "s5_lam_im", "s5_log_dt",
           "s5_b_re", "s5_b_im", "s5_c_re", "s5_c_im", "s5_d", "s5_w_glu", "conv_w", "conv_b", "g_s5", "g_conv",
           "w_out", "ln2_g", "ln2_b", "ffn2_gate", "ffn2_up", "ffn2_down", "ln3_g", "ln3_b"]
TRANSPOSED = ["ffn1_gate", "ffn1_up", "w_in", "ffn2_gate", "ffn2_up"]
UPDATED_TRANSPOSED = ["ffn1_gate", "ffn1_up", "ffn2_gate", "ffn2_up"]
GROUPS = {"a": ["ffn1_gate", "ffn1_up", "ffn1_down"], "b": ["w_in", "s5_w_glu", "w_out"],
          "c": ["ffn2_gate", "ffn2_up", "ffn2_down"]}
FFN_GROUPS = ("a", "c")


def _train_step(x, target, w, m, v):
    T, D = x.shape
    L = w["ln1_g"].shape[0]
    alpha = (2.0 * L) ** 0.25
    G = w["s5_log_dt"].shape[1]
    d_s5 = G * S5_P
    d_conv = w["conv_b"].shape[1]
    GB = G // S5_GROUPS_PER_BLOCK
    me = _me()[3]

    def shard(n, l):
        return (jnp.swapaxes(w[n][l], 0, 1) if n in TRANSPOSED else w[n][l]).astype(BF16)

    conv_w_rows = jnp.pad(w["conv_w"], ((0, 0), (0, SUBLANES - CONV_W), (0, 0)))
    parts = [(l, grp) for l in range(L) for grp in GROUPS]
    step_one, step_two, token = {}, {}, []
    for l, grp in parts:
        srcs = [shard(n, l) for n in GROUPS[grp]] + ([conv_w_rows[l]] if grp == "b" else [])
        lands = [lax.empty((N_DEV,) + a.shape, a.dtype) for a in srcs]
        sems, srcs, lands, tok = _exchange_start(_gather_copies(len(srcs), SIBLING_AND_OTHER_CHIPS, grp in FFN_GROUPS),
                                                 f"gather_start_{l}{grp}", srcs, lands, token)
        step_one[l, grp] = (sems, srcs, lands)
        token = [tok]

    def forward_on(i, after):
        if i >= len(parts):
            return []
        l, grp = parts[i]
        sems, srcs, lands = step_one[l, grp]
        copies = _gather_copies(len(srcs), SIBLING_AND_OTHER_CHIPS, grp in FFN_GROUPS)
        _, lands = _exchange_wait(copies, f"gather_wait_{l}{grp}", sems, srcs, lands, after)
        sems, _, lands, tok = _exchange_start(_forward_copies(len(lands), grp in FFN_GROUPS), f"forward_start_{l}{grp}",
                                              [], lands, [])
        step_two[l, grp] = (sems, lands)
        return [tok]

    def gathered(i, after):
        l, grp = parts[i]
        sems, lands = step_two[l, grp]
        _, lands = _exchange_wait(_forward_copies(len(lands), grp in FFN_GROUPS), f"forward_wait_{l}{grp}", sems, [],
                                  lands, after)
        full = {n: p.reshape(-1, p.shape[-1]) for n, p in zip(GROUPS[grp], lands)}
        if grp == "b":
            full["conv_w"] = jnp.swapaxes(lands[-1][:, :CONV_W, :], 0, 1).reshape(CONV_W, d_conv)
        return full

    lre = w["s5_lam_re"].reshape(L * G, S5_N)
    lim = w["s5_lam_im"].reshape(L * G, S5_N)
    ldt = w["s5_log_dt"].reshape(L * G, 1)
    b_re = jnp.transpose(w["s5_b_re"], (3, 0, 1, 2)).reshape(S5_P, L * G, S5_N)
    b_im = jnp.transpose(w["s5_b_im"], (3, 0, 1, 2)).reshape(S5_P, L * G, S5_N)
    ab_re, ab_im, bb_re, bb_im = _s5_params_fwd(lre, lim, ldt, b_re, b_im)

    def groups(bb):
        return jnp.transpose(bb.reshape(S5_P, L, GB, S5_GROUPS_PER_BLOCK, S5_N), (1, 2, 3, 0, 4))

    bd_re = _block_diag(groups(bb_re), True).astype(BF16)
    bd_im = _block_diag(groups(bb_im), True).astype(BF16)
    c_shape = (L, GB, S5_GROUPS_PER_BLOCK, S5_P, S5_N)
    cd_re = _block_diag(w["s5_c_re"].reshape(c_shape), False).astype(BF16)
    cd_im = _block_diag(w["s5_c_im"].reshape(c_shape), False).astype(BF16)
    a_re = ab_re.reshape(L, 1, G * S5_N)
    a_im = ab_im.reshape(L, 1, G * S5_N)
    d_vec = w["s5_d"].reshape(L, 1, d_s5)

    def vec(name, l):
        return w[name][l].reshape(1, -1)

    saved, weights = [], []
    x_in, x_in_b = x, x.astype(BF16)
    token = forward_on(0, token)
    for l in range(L):
        i = len(GROUPS) * l
        gw = gathered(i, [x_in] if l else token)
        s = {"x0b": x_in_b}
        s["g1"], s["u1"], s["h1"] = _ffn_up(x_in_b, gw["ffn1_gate"], gw["ffn1_up"])
        x1, s["x1b"], s["xh1"], s["rstd1"] = _mm_res_ln(s["h1"], gw["ffn1_down"], x_in, vec("ln1_g", l),
                                                         vec("ln1_b", l), 0.5, alpha, forward_on(i + 1, [s["h1"]]))
        gw.update(gathered(i + 1, [s["x1b"]]))
        s["proj"] = _mm_nt(s["x1b"], gw["w_in"])
        s["y"], s["sr"], s["si"] = _s5_fwd(s["proj"], bd_re[l], bd_im[l], cd_re[l], cd_im[l], a_re[l], a_im[l],
                                           d_vec[l])
        s["z"] = _conv_fwd(s["proj"], gw["conv_w"], vec("conv_b", l), d_s5, d_conv)
        s["mcat"], s["gl"] = _mix_post(s["y"], s["z"], gw["s5_w_glu"], vec("g_s5", l), vec("g_conv", l))
        x2, s["x2b"], s["xh2"], s["rstd2"] = _mm_res_ln(s["mcat"], gw["w_out"], x1, vec("ln2_g", l),
                                                         vec("ln2_b", l), 1.0, alpha, forward_on(i + 2, [s["mcat"]]))
        gw.update(gathered(i + 2, [s["x2b"]]))
        s["g2"], s["u2"], s["h2"] = _ffn_up(s["x2b"], gw["ffn2_gate"], gw["ffn2_up"])
        x3, x3b, s["xh3"], s["rstd3"] = _mm_res_ln(s["h2"], gw["ffn2_down"], x2, vec("ln3_g", l), vec("ln3_b", l),
                                                   0.5, alpha, forward_on(i + 3, [s["h2"]]))
        saved.append(s)
        weights.append(gw)
        x_in, x_in_b = x3, x3b

    last = saved[L - 1]
    dr, drb, dg, db, loss = _loss_ln_bwd(x_in, target, last["xh3"], last["rstd3"], vec("ln3_g", L - 1))
    small = [dict() for _ in range(L)]
    small[L - 1]["ln3_g"], small[L - 1]["ln3_b"] = dg, db
    bufs = {grp: [lax.empty((L, N_CHIPS) + shard(n, 0).shape, BF16) for n in names] for grp, names in GROUPS.items()}
    scatters = {grp: [] for grp in GROUPS}
    grad_x = None

    def chip_scatter(l, grp, sums):
        sems, sums, bufs[grp], tok = _exchange_start(_chip_scatter_copies(len(sums), l), f"scatter_start_{l}{grp}", sums,
                                                     bufs[grp], [])
        scatters[grp].append((l, sems, sums))
        return [tok]

    small_names = SMALL + ["conv_w"]
    b_shape = (L, G, S5_P, S5_N)
    small_shapes = [b_shape if n in S5_B else w[n].shape for n in SMALL] + [(L, CONV_W, d_conv)]

    def small_scatter_start(small):
        def stack(key):
            return jnp.stack([small[l][key] for l in range(L)])

        d_bb_re = jnp.transpose(stack("d_bb_re").reshape(L * G, S5_P, S5_N), (1, 0, 2))
        d_bb_im = jnp.transpose(stack("d_bb_im").reshape(L * G, S5_P, S5_N), (1, 0, 2))
        g_lre, g_lim, g_ldt, g_bre, g_bim = _s5_params_bwd(
            lre, lim, ldt, b_re, b_im, stack("d_ab_re").reshape(L * G, S5_N), stack("d_ab_im").reshape(L * G, S5_N),
            d_bb_re, d_bb_im)
        part = {n: [small[l][n] for l in range(L)]
                for n in ["ln1_g", "ln1_b", "s5_c_re", "s5_c_im", "s5_d", "conv_b", "g_s5", "g_conv", "ln2_g", "ln2_b",
                          "ln3_g", "ln3_b", "conv_w"]}
        part["s5_lam_re"], part["s5_lam_im"], part["s5_log_dt"] = [g_lre], [g_lim], [g_ldt]
        part["s5_b_re"] = [jnp.transpose(g_bre, (1, 0, 2))]
        part["s5_b_im"] = [jnp.transpose(g_bim, (1, 0, 2))]
        packed = _pack([piece for n in small_names for piece in part[n]], N_DEV * PACK_ROWS)
        rows = packed.shape[0] // N_DEV
        return _exchange_start(_scatter_copies(1, 0), "scatter_small_start", [packed.reshape(N_DEV, rows, LANES)],
                               [lax.empty((1, N_DEV, rows, LANES), F32)], [])

    token = []
    for l in reversed(range(L)):
        gw, s, sm = weights[l], saved[l], small[l]
        full = {}

        def ffn_bwd(dr, drb, tag, grp, xb_in, ln, after):
            dgp, dup = _ffn_down_bwd(drb, gw[f"ffn{tag}_down"], s[f"g{tag}"], s[f"u{tag}"], after)
            terms = {"gate": (dgp, xb_in, 1.0), "up": (dup, xb_in, 1.0), "down": (s[f"h{tag}"], drb, 0.5)}
            terms = [terms[n.split("_")[1]] for n in GROUPS[grp]]
            away = [_mm_tn(a, b, scale, BF16, half=0).reshape((N_CHIPS, -1, b.shape[1])) for a, b, scale in terms]
            stages = [lax.empty(a.shape, BF16) for a in away]
            sems, away, stages, tok = _exchange_start(_swap_copies(len(away)), f"pair_start_{l}{grp}", away, stages, [])
            res = _mm_dx([(dgp, gw[f"ffn{tag}_gate"]), (dup, gw[f"ffn{tag}_up"])], dr, alpha, ln, [tok])
            _, stages = _exchange_wait(_swap_copies(len(away)), f"pair_wait_{l}{grp}", sems, away, stages, [res[0]])
            sums = [_mm_tn(a, b, scale, BF16, half=1, addend=st.reshape(-1, b.shape[1])).reshape(st.shape)
                    for (a, b, scale), st in zip(terms, stages)]
            return res, chip_scatter(l, grp, sums)

        (dr, drb, sm["ln2_g"], sm["ln2_b"]), token = ffn_bwd(dr, drb, 2, "c", s["x2b"],
                                                              (s["xh2"], s["rstd2"], vec("ln2_g", l)), token)
        dm = _mm_nt(drb, gw["w_out"], token)
        full["w_out"] = _mm_tn(s["mcat"], drb, 1.0, BF16)
        dy, dz, dwglu, sm["g_s5"], sm["g_conv"] = _mix_post_bwd(dm, s["y"], s["gl"], s["z"], gw["s5_w_glu"],
                                                                vec("g_s5", l), vec("g_conv", l))
        full["s5_w_glu"] = dwglu.astype(BF16)
        du, dbd_re, dbd_im, dcd_re, dcd_im, sm["d_ab_re"], sm["d_ab_im"], sm["s5_d"] = _s5_bwd(
            dy, s["proj"], s["sr"], s["si"], bd_re[l], bd_im[l], cd_re[l], cd_im[l], a_re[l], a_im[l], d_vec[l])
        gsz = (S5_GROUPS_PER_BLOCK, S5_P, S5_N)
        sm["d_bb_re"] = _diag_blocks(dbd_re, *gsz, True)
        sm["d_bb_im"] = _diag_blocks(dbd_im, *gsz, True)
        sm["s5_c_re"] = _diag_blocks(dcd_re, *gsz, False).reshape(G, S5_P, S5_N)
        sm["s5_c_im"] = _diag_blocks(dcd_im, *gsz, False).reshape(G, S5_P, S5_N)
        dgb, dgc, dhh, sm["conv_w"], sm["conv_b"] = _conv_bwd(dz, s["proj"], gw["conv_w"], vec("conv_b", l),
                                                              d_s5, d_conv)
        dproj = jnp.concatenate([du.astype(BF16), dgb, dgc, dhh], axis=1)
        full["w_in"] = _mm_tn(dproj, s["x1b"], 1.0, BF16)
        fulls = [full[n].reshape((N_DEV, -1) + full[n].shape[1:]) for n in GROUPS["b"]]
        stages = [lax.empty((N_CHIPS,) + f.shape[1:], BF16) for f in fulls]
        sems, fulls, stages, tok = _exchange_start(_pair_copies(len(fulls)), f"pair_start_{l}b", fulls, stages, [])
        dr, drb, sm["ln1_g"], sm["ln1_b"] = _mm_dx([(dproj, gw["w_in"])], dr, alpha,
                                                   (s["xh1"], s["rstd1"], vec("ln1_g", l)), [tok])
        fulls, stages = _exchange_wait(_pair_copies(len(fulls)), f"pair_wait_{l}b", sems, fulls, stages, [dr])
        token = chip_scatter(l, "b", [_pair_sum(f, st) for f, st in zip(fulls, stages)])
        if l > 0:
            prev = saved[l - 1]
            (dr, drb, small[l - 1]["ln3_g"], small[l - 1]["ln3_b"]), token = ffn_bwd(
                dr, drb, 1, "a", s["x0b"], (prev["xh3"], prev["rstd3"], vec("ln3_g", l - 1)), token)
        else:
            small_sems, small_srcs, small_lands, tok = small_scatter_start(small)
            ((grad_x,), _) = ffn_bwd(dr, drb, 1, "a", s["x0b"], None, token + [tok])

    _, landed = _exchange_wait(_scatter_copies(1, 0), "scatter_small_wait", small_sems, small_srcs, small_lands,
                               [grad_x])
    mine = _reduce_parts(landed[0])
    (summed,) = _exchange(_gather_copies(1), "gather_small", [mine[0]], [lax.empty(landed[0].shape[1:], F32)])
    small_grads = dict(zip(small_names, _unpack(summed, small_shapes)))

    out = {}

    def update(name, w3, m3, v3, shape, **grad):
        res = _adamw_update(w3, m3, v3, **grad)
        out[name] = [r.reshape(shape) for r in res]

    for grp in ("c", "b", "a"):
        for l, sems, sums in scatters[grp]:
            _, bufs[grp] = _exchange_wait(_chip_scatter_copies(len(sums), l), f"scatter_wait_{l}{grp}", sems, sums,
                                          bufs[grp], [grad_x])
        for n, parts in zip(GROUPS[grp], bufs[grp]):
            if n in UPDATED_TRANSPOSED:
                res = _adamw_update(*(jnp.swapaxes(a[n], 1, 2) for a in (w, m, v)), parts=parts)
                out[n] = [jnp.swapaxes(r, 1, 2) for r in res]
            elif n in TRANSPOSED:
                update(n, w[n], m[n], v[n], w[n].shape, grad=jnp.swapaxes(_reduce_parts(parts), 1, 2))
            else:
                update(n, w[n], m[n], v[n], w[n].shape, parts=parts)
    cw_shape = w["conv_w"].shape
    g_cw = lax.dynamic_slice_in_dim(small_grads["conv_w"], me * cw_shape[2], cw_shape[2], axis=2)
    update("conv_w", w["conv_w"], m["conv_w"], v["conv_w"], cw_shape, grad=g_cw)
    for n in SMALL:
        if n in S5_B:
            def view(a):
                return jnp.swapaxes(a, 2, 3).reshape(1, -1, S5_N)
            res = _adamw_update(view(w[n]), view(m[n]), view(v[n]), grad=small_grads[n].reshape(1, -1, S5_N))
            out[n] = [jnp.swapaxes(r.reshape(b_shape), 2, 3) for r in res]
        else:
            update(n, *(a[n].reshape(1, -1, a[n].shape[-1]) for a in (w, m, v)), w[n].shape,
                   grad=small_grads[n].reshape(1, -1, w[n].shape[-1]))

    loss = lax.psum(loss[0, 0], AXES)
    return loss, grad_x, out


def kernel(x, ffn1_gate, ffn1_up, ffn1_down, ln1_g, ln1_b, w_in, s5_lam_re, s5_lam_im, s5_log_dt, s5_b_re, s5_b_im, s5_c_re, s5_c_im, s5_d, s5_w_glu, conv_w, conv_b, g_s5, g_conv, w_out, ln2_g, ln2_b, ffn2_gate, ffn2_up, ffn2_down, ln3_g, ln3_b, loss_target, m_ffn1_gate, m_ffn1_up, m_ffn1_down, m_ln1_g, m_ln1_b, m_w_in, m_s5_lam_re, m_s5_lam_im, m_s5_log_dt, m_s5_b_re, m_s5_b_im, m_s5_c_re, m_s5_c_im, m_s5_d, m_s5_w_glu, m_conv_w, m_conv_b, m_g_s5, m_g_conv, m_w_out, m_ln2_g, m_ln2_b, m_ffn2_gate, m_ffn2_up, m_ffn2_down, m_ln3_g, m_ln3_b, v_ffn1_gate, v_ffn1_up, v_ffn1_down, v_ln1_g, v_ln1_b, v_w_in, v_s5_lam_re, v_s5_lam_im, v_s5_log_dt, v_s5_b_re, v_s5_b_im, v_s5_c_re, v_s5_c_im, v_s5_d, v_s5_w_glu, v_conv_w, v_conv_b, v_g_s5, v_g_conv, v_w_out, v_ln2_g, v_ln2_b, v_ffn2_gate, v_ffn2_up, v_ffn2_down, v_ln3_g, v_ln3_b):
    given = dict(locals())
    w = {n: given[n] for n in WEIGHTS}
    m = {n: given["m_" + n] for n in WEIGHTS}
    v = {n: given["v_" + n] for n in WEIGHTS}
    T, D = x.shape[-2:]
    loss, grad_x, out = _train_step(x.reshape(T, D), loss_target.reshape(T, D), w, m, v)
    results = [loss, grad_x.reshape(x.shape)]
    for i in range(4):
        results += [out[n][i] for n in WEIGHTS]
    return tuple(results)
```

```python
import functools
import math

import jax
import jax.numpy as jnp
from jax import lax
from jax.experimental import pallas as pl
from jax.experimental.pallas import tpu as pltpu

F32 = jnp.float32
BF16 = jnp.bfloat16
MESH = pl.DeviceIdType.MESH
AXES = ("x", "y", "c")
N_DEV = 8
N_CHIPS = 4

S5_P = 16
S5_N = 64
CONV_W = 3
LN_EPS = 1e-5
RMS_EPS = 1e-6
ADAM_LR = 0.001
ADAM_B1 = 0.9
ADAM_B2 = 0.999
ADAM_EPS = 1e-08
ADAM_WD = 0.01
ADAM_STEP = 10

V7X_VMEM_BYTES = 64 * 1024 * 1024
VMEM_LIMIT = V7X_VMEM_BYTES * 7 // 8
LANES = 128
SUBLANES = 8
BF16_ROWS = 16
MXU_COLS = 256
ROWS_RESIDENT = 2048
PACK_ROWS = 512
S5_GROUPS_PER_BLOCK = LANES // S5_P
S5_STATE_BLOCK = S5_GROUPS_PER_BLOCK * S5_N

HBM_SPEC = pl.BlockSpec(memory_space=pltpu.HBM)
SEM_SPEC = pl.BlockSpec(memory_space=pltpu.SEMAPHORE)
ANY_SPEC = pl.BlockSpec(memory_space=pl.ANY)


def _tile(n, pref, align):
    best = None
    d = align
    while d <= min(n, pref):
        if n % d == 0:
            best = d
        d += align
    return best if best is not None else n


def _params(*sem):
    return pltpu.CompilerParams(dimension_semantics=sem, vmem_limit_bytes=VMEM_LIMIT)


def _dot_nn(a, b):
    return lax.dot_general(a, b, (((1,), (0,)), ((), ())), preferred_element_type=F32)


def _dot_nt(a, b):
    return lax.dot_general(a, b, (((1,), (1,)), ((), ())), preferred_element_type=F32)


def _dot_tn(a, b):
    return lax.dot_general(a, b, (((0,), (0,)), ((), ())), preferred_element_type=F32)


def _colsum(v):
    return jnp.sum(v, axis=0, keepdims=True)


def _rowmean(v):
    return jnp.mean(v, axis=-1, keepdims=True)


def _ffn_up(xb, wg, wu):
    T, D = xb.shape
    F = wg.shape[0]
    tm = _tile(T, ROWS_RESIDENT, 16)
    tn = _tile(F, MXU_COLS, LANES)

    def body(x_ref, wg_ref, wu_ref, g_ref, u_ref, h_ref):
        x = x_ref[...]
        g = _dot_nt(x, wg_ref[...])
        u = _dot_nt(x, wu_ref[...])
        g_ref[...] = g.astype(BF16)
        u_ref[...] = u.astype(BF16)
        h_ref[...] = (g * jax.nn.sigmoid(g) * u).astype(BF16)

    w_spec = pl.BlockSpec((tn, D), lambda j, i: (j, 0))
    o_spec = pl.BlockSpec((tm, tn), lambda j, i: (i, j))
    return pl.pallas_call(
        body, name="ffn_up", grid=(F // tn, T // tm),
        in_specs=[pl.BlockSpec((tm, D), lambda j, i: (i, 0)), w_spec, w_spec],
        out_specs=[o_spec, o_spec, o_spec],
        out_shape=[jax.ShapeDtypeStruct((T, F), BF16)] * 3,
        compiler_params=_params("arbitrary", "arbitrary"),
    )(xb, wg, wu)


def _mm_acc(pairs, after=()):
    T, K = pairs[0][0].shape
    D = pairs[0][1].shape[1]
    n = len(pairs)
    tk = _tile(K, 512, LANES)
    tm = _tile(T, 512, 16)

    def body(*refs):
        o_ref = refs[-1]

        @pl.when(pl.program_id(0) == 0)
        def _():
            o_ref[...] = jnp.zeros_like(o_ref)

        for r in range(0, T, tm):
            part = _dot_nn(refs[0][r:r + tm, :], refs[1][...])
            for a_ref, w_ref in zip(refs[2:2 * n:2], refs[3:2 * n:2]):
                part += _dot_nn(a_ref[r:r + tm, :], w_ref[...])
            o_ref[r:r + tm, :] += part

    in_specs, operands = [], []
    for a, w in pairs:
        in_specs += [pl.BlockSpec((T, tk), lambda k: (0, k)), pl.BlockSpec((tk, D), lambda k: (k, 0))]
        operands += [a, w]
    return pl.pallas_call(
        body, name="mm_acc", grid=(K // tk,),
        in_specs=in_specs + [ANY_SPEC] * len(after),
        out_specs=pl.BlockSpec((T, D), lambda k: (0, 0)),
        out_shape=jax.ShapeDtypeStruct((T, D), F32),
        compiler_params=_params("arbitrary"),
    )(*operands, *after)


def _mm_res_ln(a, w, res, g, b, scale, alpha, after=()):
    acc = _mm_acc([(a, w)], after)
    T, D = acc.shape
    tm = _tile(T, 256, 16)

    def body(acc_ref, res_ref, g_ref, b_ref, xo_ref, xb_ref, xh_ref, rstd_ref):
        r = alpha * res_ref[...] + scale * acc_ref[...]
        xc = r - _rowmean(r)
        rstd = lax.rsqrt(_rowmean(xc * xc) + LN_EPS)
        xh = xc * rstd
        xo = xh * g_ref[...] + b_ref[...]
        xo_ref[...] = xo
        xb_ref[...] = xo.astype(BF16)
        xh_ref[...] = xh
        rstd_ref[...] = rstd

    row = pl.BlockSpec((tm, D), lambda i: (i, 0))
    vec = pl.BlockSpec((1, D), lambda i: (0, 0))
    return pl.pallas_call(
        body, name="res_ln", grid=(T // tm,),
        in_specs=[row, row, vec, vec],
        out_specs=[row, row, row, pl.BlockSpec((tm, 1), lambda i: (i, 0))],
        out_shape=[jax.ShapeDtypeStruct((T, D), F32), jax.ShapeDtypeStruct((T, D), BF16),
                   jax.ShapeDtypeStruct((T, D), F32), jax.ShapeDtypeStruct((T, 1), F32)],
        compiler_params=_params("arbitrary"),
    )(acc, res, g, b)


def _mm_nt(a, w, after=()):
    M, K = a.shape
    N = w.shape[0]
    tm = _tile(M, ROWS_RESIDENT, 16)
    tn = _tile(N, MXU_COLS, LANES)

    def body(a_ref, w_ref, *rest):
        rest[-1][...] = _dot_nt(a_ref[...], w_ref[...])

    return pl.pallas_call(
        body, name="mm_nt", grid=(N // tn, M // tm),
        in_specs=[pl.BlockSpec((tm, K), lambda j, i: (i, 0)), pl.BlockSpec((tn, K), lambda j, i: (j, 0))]
        + [ANY_SPEC] * len(after),
        out_specs=pl.BlockSpec((tm, tn), lambda j, i: (i, j)),
        out_shape=jax.ShapeDtypeStruct((M, N), F32),
        compiler_params=_params("arbitrary", "arbitrary"),
    )(a, w, *after)


def _mm_tn(a, b, scale, out_dtype, half=None, addend=None):
    T, M = a.shape
    N = b.shape[1]
    rows = M if half is None else M // 2
    tm = _tile(rows, 512, LANES)
    tn = _tile(N, ROWS_RESIDENT, LANES)
    first = 0 if half is None else half * (rows // tm)

    def body(a_ref, b_ref, *rest):
        out = scale * _dot_tn(a_ref[...], b_ref[...])
        if addend is not None:
            out = out + rest[0][...].astype(F32)
        rest[-1][...] = out.astype(out_dtype)

    o_spec = pl.BlockSpec((tm, tn), lambda i, j: (i, j))
    return pl.pallas_call(
        body, name="mm_tn", grid=(rows // tm, N // tn),
        in_specs=[pl.BlockSpec((T, tm), lambda i, j: (0, first + i)), pl.BlockSpec((T, tn), lambda i, j: (0, j))]
        + ([] if addend is None else [o_spec]),
        out_specs=o_spec,
        out_shape=jax.ShapeDtypeStruct((rows, N), out_dtype),
        compiler_params=_params("arbitrary", "arbitrary"),
    )(a, b, *([] if addend is None else [addend]))


def _ln_bwd(dy, xh, rstd, g):
    dxh = dy * g
    dr = rstd * (dxh - _rowmean(dxh) - xh * _rowmean(dxh * xh))
    return dr, _colsum(dy * xh), _colsum(dy)


def _loss_ln_bwd(y, target, xh, rstd, g):
    T, D = y.shape
    tm = _tile(T, 256, 16)

    def body(y_ref, t_ref, xh_ref, rstd_ref, g_ref, dr_ref, drb_ref, dg_ref, db_ref, loss_ref):
        i = pl.program_id(0)

        @pl.when(i == 0)
        def _():
            dg_ref[...] = jnp.zeros_like(dg_ref)
            db_ref[...] = jnp.zeros_like(db_ref)
            loss_ref[...] = jnp.zeros_like(loss_ref)

        err = y_ref[...] - t_ref[...]
        loss_ref[...] += (0.5 / D) * _colsum(jnp.sum(err * err, axis=1, keepdims=True))
        dr, dg, db = _ln_bwd(err * (1.0 / D), xh_ref[...], rstd_ref[...], g_ref[...])
        dr_ref[...] = dr
        drb_ref[...] = dr.astype(BF16)
        dg_ref[...] += dg
        db_ref[...] += db

    row = pl.BlockSpec((tm, D), lambda i: (i, 0))
    vec = pl.BlockSpec((1, D), lambda i: (0, 0))
    return pl.pallas_call(
        body, name="loss_ln_bwd", grid=(T // tm,),
        in_specs=[row, row, row, pl.BlockSpec((tm, 1), lambda i: (i, 0)), vec],
        out_specs=[row, row, vec, vec, pl.BlockSpec((1, 1), lambda i: (0, 0))],
        out_shape=[jax.ShapeDtypeStruct((T, D), F32), jax.ShapeDtypeStruct((T, D), BF16),
                   jax.ShapeDtypeStruct((1, D), F32), jax.ShapeDtypeStruct((1, D), F32),
                   jax.ShapeDtypeStruct((1, 1), F32)],
        compiler_params=_params("arbitrary"),
    )(y, target, xh, rstd, g)


def _ffn_down_bwd(drb, wd, gpre, upre, after=()):
    T, D = drb.shape
    F = wd.shape[0]
    tm = _tile(T, ROWS_RESIDENT, 16)
    tn = _tile(F, MXU_COLS, LANES)

    def body(dr_ref, wd_ref, g_ref, u_ref, *rest):
        dg_ref, du_ref = rest[len(after):]
        dh = 0.5 * _dot_nt(dr_ref[...], wd_ref[...])
        g = g_ref[...].astype(F32)
        u = u_ref[...].astype(F32)
        sg = jax.nn.sigmoid(g)
        du_ref[...] = (dh * (g * sg)).astype(BF16)
        dg_ref[...] = (dh * u * (sg * (1.0 + g * (1.0 - sg)))).astype(BF16)

    t_spec = pl.BlockSpec((tm, tn), lambda j, i: (i, j))
    return pl.pallas_call(
        body, name="ffn_down_bwd", grid=(F // tn, T // tm),
        in_specs=[pl.BlockSpec((tm, D), lambda j, i: (i, 0)), pl.BlockSpec((tn, D), lambda j, i: (j, 0)),
                  t_spec, t_spec] + [ANY_SPEC] * len(after),
        out_specs=[t_spec, t_spec],
        out_shape=[jax.ShapeDtypeStruct((T, F), BF16), jax.ShapeDtypeStruct((T, F), BF16)],
        compiler_params=_params("arbitrary", "arbitrary"),
    )(drb, wd, gpre, upre, *after)


def _mm_dx(pairs, res, alpha, ln=None, after=()):
    acc = _mm_acc(pairs, after)
    T, D = acc.shape
    tm = _tile(T, 256, 16)
    with_ln = ln is not None

    def body(acc_ref, res_ref, *refs):
        dx = alpha * res_ref[...] + acc_ref[...]
        if with_ln:
            xh_ref, rstd_ref, g_ref, dr_ref, drb_ref, dg_ref, db_ref = refs

            @pl.when(pl.program_id(0) == 0)
            def _():
                dg_ref[...] = jnp.zeros_like(dg_ref)
                db_ref[...] = jnp.zeros_like(db_ref)

            dr, dg, db = _ln_bwd(dx, xh_ref[...], rstd_ref[...], g_ref[...])
            dr_ref[...] = dr
            drb_ref[...] = dr.astype(BF16)
            dg_ref[...] += dg
            db_ref[...] += db
        else:
            refs[0][...] = dx

    row = pl.BlockSpec((tm, D), lambda i: (i, 0))
    vec = pl.BlockSpec((1, D), lambda i: (0, 0))
    in_specs, operands = [row, row], [acc, res]
    if with_ln:
        in_specs += [row, pl.BlockSpec((tm, 1), lambda i: (i, 0)), vec]
        operands += list(ln)
        out_specs = [row, row, vec, vec]
        out_shape = [jax.ShapeDtypeStruct((T, D), F32), jax.ShapeDtypeStruct((T, D), BF16),
                     jax.ShapeDtypeStruct((1, D), F32), jax.ShapeDtypeStruct((1, D), F32)]
    else:
        out_specs = [row]
        out_shape = [jax.ShapeDtypeStruct((T, D), F32)]
    return pl.pallas_call(
        body, name="dx_ln_bwd" if with_ln else "dx_res", grid=(T // tm,),
        in_specs=in_specs, out_specs=out_specs, out_shape=out_shape,
        compiler_params=_params("arbitrary"),
    )(*operands)


def _s5_discretize(lre, lim, ldt, br, bi):
    dt = jnp.exp(ldt)
    mag = jnp.exp(lre * dt)
    ang = lim * dt
    ar = mag * jnp.cos(ang)
    ai = mag * jnp.sin(ang)
    den = lre * lre + lim * lim
    nr = ar - 1.0
    qr = (nr * lre + ai * lim) / den
    qi = (ai * lre - nr * lim) / den
    bbr = qr[None] * br - qi[None] * bi
    bbi = qr[None] * bi + qi[None] * br
    return ar, ai, bbr, bbi


def _s5_params_fwd(lre, lim, ldt, br, bi):
    def body(lre_ref, lim_ref, ldt_ref, br_ref, bi_ref, ar_ref, ai_ref, bbr_ref, bbi_ref):
        ar, ai, bbr, bbi = _s5_discretize(lre_ref[...], lim_ref[...], ldt_ref[...], br_ref[...], bi_ref[...])
        ar_ref[...] = ar
        ai_ref[...] = ai
        bbr_ref[...] = bbr
        bbi_ref[...] = bbi

    sds = jax.ShapeDtypeStruct
    return pl.pallas_call(
        body, name="s5_params_fwd",
        out_shape=[sds(lre.shape, F32), sds(lre.shape, F32), sds(br.shape, F32), sds(br.shape, F32)],
        compiler_params=pltpu.CompilerParams(vmem_limit_bytes=VMEM_LIMIT),
    )(lre, lim, ldt, br, bi)


def _s5_params_bwd(lre, lim, ldt, br, bi, dar, dai, dbbr, dbbi):
    def body(lre_ref, lim_ref, ldt_ref, br_ref, bi_ref, dar_ref, dai_ref, dbbr_ref, dbbi_ref,
             o_lre, o_lim, o_ldt, o_br, o_bi):
        _, vjp = jax.vjp(_s5_discretize, lre_ref[...], lim_ref[...], ldt_ref[...], br_ref[...], bi_ref[...])
        g = vjp((dar_ref[...], dai_ref[...], dbbr_ref[...], dbbi_ref[...]))
        o_lre[...] = g[0]
        o_lim[...] = g[1]
        o_ldt[...] = g[2]
        o_br[...] = g[3]
        o_bi[...] = g[4]

    sds = jax.ShapeDtypeStruct
    return pl.pallas_call(
        body, name="s5_params_bwd",
        out_shape=[sds(lre.shape, F32), sds(lre.shape, F32), sds(ldt.shape, F32), sds(br.shape, F32),
                   sds(br.shape, F32)],
        compiler_params=pltpu.CompilerParams(vmem_limit_bytes=VMEM_LIMIT),
    )(lre, lim, ldt, br, bi, dar, dai, dbbr, dbbi)


def _cmul(a_re, a_im, b_re, b_im):
    return a_re * b_re - a_im * b_im, a_re * b_im + a_im * b_re


def _s5_tile(i):
    if isinstance(i, int):
        return pl.ds(i * SUBLANES, SUBLANES)
    return pl.ds(pl.multiple_of(i * SUBLANES, SUBLANES), SUBLANES)


def _s5_powers(a_re, a_im, pwr_ref, pwi_ref, seg):
    def step(i, carry):
        p_re, p_im = carry
        pwr_ref[pl.ds(i, 1), :] = p_re
        pwi_ref[pl.ds(i, 1), :] = p_im
        return _cmul(a_re, a_im, p_re, p_im)

    lax.fori_loop(0, seg, step, (a_re, a_im))


def _s5_fwd(proj, bdr, bdi, cdr, cdi, ar, ai, dvec):
    T = proj.shape[0]
    GB, UB, SB = bdr.shape
    tc = _tile(T, 256, SUBLANES * SUBLANES)
    seg = tc // SUBLANES

    def body(u_ref, bdr_ref, bdi_ref, cdr_ref, cdi_ref, ar_ref, ai_ref, d_ref, y_ref, sr_ref, si_ref,
             cr_ref, ci_ref, pwr_ref, pwi_ref, str_ref, sti_ref, up_ref, yp_ref):
        a_re = ar_ref[...]
        a_im = ai_ref[...]

        @pl.when(pl.program_id(1) == 0)
        def _():
            cr_ref[...] = jnp.zeros_like(cr_ref)
            ci_ref[...] = jnp.zeros_like(ci_ref)
            _s5_powers(a_re, a_im, pwr_ref, pwi_ref, seg)

        for i in range(seg):
            up_ref[_s5_tile(i), :] = u_ref[pl.ds(i, SUBLANES, stride=seg), :]
        u = up_ref[...]
        ub = u.astype(BF16)
        sr_ref[...] = _dot_nn(ub, bdr_ref[...])
        si_ref[...] = _dot_nn(ub, bdi_ref[...])

        def local(i, carry):
            p_re, p_im = carry
            rows = _s5_tile(i)
            n_re, n_im = _cmul(a_re, a_im, p_re, p_im)
            n_re, n_im = n_re + sr_ref[rows, :], n_im + si_ref[rows, :]
            sr_ref[rows, :] = n_re
            si_ref[rows, :] = n_im
            return n_re, n_im

        e_re, e_im = lax.fori_loop(1, seg, local, (sr_ref[_s5_tile(0), :], si_ref[_s5_tile(0), :]))
        s_re, s_im = cr_ref[...], ci_ref[...]
        top_re, top_im = pwr_ref[seg - 1:seg, :], pwi_ref[seg - 1:seg, :]
        for j in range(SUBLANES):
            str_ref[j:j + 1, :] = s_re
            sti_ref[j:j + 1, :] = s_im
            n_re, n_im = _cmul(top_re, top_im, s_re, s_im)
            s_re, s_im = n_re + e_re[j:j + 1, :], n_im + e_im[j:j + 1, :]
        cr_ref[...] = s_re
        ci_ref[...] = s_im
        b_re, b_im = str_ref[...], sti_ref[...]

        def fix(i, carry):
            rows = _s5_tile(i)
            f_re, f_im = _cmul(pwr_ref[pl.ds(i, 1), :], pwi_ref[pl.ds(i, 1), :], b_re, b_im)
            sr_ref[rows, :] += f_re
            si_ref[rows, :] += f_im
            return carry

        lax.fori_loop(0, seg, fix, 0)
        yp_ref[...] = (_dot_nn(sr_ref[...].astype(BF16), cdr_ref[...])
                       - _dot_nn(si_ref[...].astype(BF16), cdi_ref[...]) + d_ref[...] * u)
        for i in range(seg):
            y_ref[pl.ds(i, SUBLANES, stride=seg), :] = yp_ref[_s5_tile(i), :]

    return pl.pallas_call(
        body, name="s5_fwd", grid=(GB, T // tc),
        in_specs=[pl.BlockSpec((tc, UB), lambda j, t: (t, j)),
                  pl.BlockSpec((None, UB, SB), lambda j, t: (j, 0, 0)),
                  pl.BlockSpec((None, UB, SB), lambda j, t: (j, 0, 0)),
                  pl.BlockSpec((None, SB, UB), lambda j, t: (j, 0, 0)),
                  pl.BlockSpec((None, SB, UB), lambda j, t: (j, 0, 0)),
                  pl.BlockSpec((1, SB), lambda j, t: (0, j)),
                  pl.BlockSpec((1, SB), lambda j, t: (0, j)),
                  pl.BlockSpec((1, UB), lambda j, t: (0, j))],
        out_specs=[pl.BlockSpec((tc, UB), lambda j, t: (t, j)),
                   pl.BlockSpec((tc, SB), lambda j, t: (t, j)),
                   pl.BlockSpec((tc, SB), lambda j, t: (t, j))],
        out_shape=[jax.ShapeDtypeStruct((T, GB * UB), F32), jax.ShapeDtypeStruct((T, GB * SB), F32),
                   jax.ShapeDtypeStruct((T, GB * SB), F32)],
        scratch_shapes=[pltpu.VMEM((1, SB), F32), pltpu.VMEM((1, SB), F32),
                        pltpu.VMEM((seg, SB), F32), pltpu.VMEM((seg, SB), F32),
                        pltpu.VMEM((SUBLANES, SB), F32), pltpu.VMEM((SUBLANES, SB), F32),
                        pltpu.VMEM((tc, UB), F32), pltpu.VMEM((tc, UB), F32)],
        compiler_params=_params("arbitrary", "arbitrary"),
    )(proj, bdr, bdi, cdr, cdi, ar, ai, dvec)


def _s5_bwd(dy, proj, sr, si, bdr, bdi, cdr, cdi, ar, ai, dvec):
    T = dy.shape[0]
    GB, UB, SB = bdr.shape
    tc = _tile(T, 256, SUBLANES * SUBLANES)
    seg = tc // SUBLANES
    nt = T // tc
    halo_blocks = tc // SUBLANES

    def body(dy_ref, u_ref, sr_ref, si_ref, hr_ref, hi_ref, bdr_ref, bdi_ref, cdr_ref, cdi_ref, ar_ref, ai_ref,
             d_ref, du_ref, dbdr_ref, dbdi_ref, dcdr_ref, dcdi_ref, dar_ref, dai_ref, dd_ref,
             gr_ref, gi_ref, pwr_ref, pwi_ref, rtr_ref, rti_ref, cr_ref, ci_ref, dyp_ref, up_ref, dup_ref):
        step_no = pl.program_id(1)
        first_chunk = step_no == nt - 1
        a_re = ar_ref[...]
        a_im = ai_ref[...]

        @pl.when(step_no == 0)
        def _():
            for ref in (cr_ref, ci_ref, dbdr_ref, dbdi_ref, dcdr_ref, dcdi_ref, dar_ref, dai_ref, dd_ref):
                ref[...] = jnp.zeros_like(ref)
            _s5_powers(a_re, a_im, pwr_ref, pwi_ref, seg)

        for i in range(seg):
            dyp_ref[_s5_tile(i), :] = dy_ref[pl.ds(i, SUBLANES, stride=seg), :]
            up_ref[_s5_tile(i), :] = u_ref[pl.ds(i, SUBLANES, stride=seg), :]
        dy = dyp_ref[...]
        dyb = dy.astype(BF16)
        u = up_ref[...]
        gr_ref[...] = _dot_nt(dyb, cdr_ref[...])
        gi_ref[...] = -_dot_nt(dyb, cdi_ref[...])
        dcdr_ref[...] += _dot_tn(sr_ref[...].astype(BF16), dyb)
        dcdi_ref[...] -= _dot_tn(si_ref[...].astype(BF16), dyb)

        def local(n, carry):
            c_re, c_im = carry
            rows = _s5_tile(seg - 1 - n)
            g_re = gr_ref[rows, :] + a_re * c_re + a_im * c_im
            g_im = gi_ref[rows, :] + a_re * c_im - a_im * c_re
            gr_ref[rows, :] = g_re
            gi_ref[rows, :] = g_im
            return g_re, g_im

        last = _s5_tile(seg - 1)
        m_re, m_im = lax.fori_loop(1, seg, local, (gr_ref[last, :], gi_ref[last, :]))
        top_re, top_im = pwr_ref[seg - 1:seg, :], -pwi_ref[seg - 1:seg, :]
        r_re, r_im = cr_ref[...], ci_ref[...]
        for j in reversed(range(SUBLANES)):
            rtr_ref[j:j + 1, :] = r_re
            rti_ref[j:j + 1, :] = r_im
            n_re, n_im = _cmul(top_re, top_im, r_re, r_im)
            r_re, r_im = n_re + m_re[j:j + 1, :], n_im + m_im[j:j + 1, :]
        cr_ref[...] = r_re
        ci_ref[...] = r_im
        f_re, f_im = rtr_ref[...], rti_ref[...]
        keep = jnp.where(first_chunk, 0.0, 1.0)
        sub = lax.broadcasted_iota(jnp.int32, (SUBLANES, SB), 0)
        before_re = jnp.where(sub == 0, hr_ref[SUBLANES - 1:SUBLANES, :] * keep, pltpu.roll(sr_ref[last, :], 1, 0))
        before_im = jnp.where(sub == 0, hi_ref[SUBLANES - 1:SUBLANES, :] * keep, pltpu.roll(si_ref[last, :], 1, 0))

        def fix(i, p_re, p_im, acc):
            rows = _s5_tile(i)
            k = seg - 1 - i
            c_re, c_im = _cmul(pwr_ref[pl.ds(k, 1), :], -pwi_ref[pl.ds(k, 1), :], f_re, f_im)
            g_re = gr_ref[rows, :] + c_re
            g_im = gi_ref[rows, :] + c_im
            gr_ref[rows, :] = g_re
            gi_ref[rows, :] = g_im
            return acc[0] + p_re * g_re + p_im * g_im, acc[1] + p_re * g_im - p_im * g_re

        zero = jnp.zeros((SUBLANES, SB), F32)
        acc = fix(0, before_re, before_im, (zero, zero))
        acc = lax.fori_loop(
            1, seg, lambda i, acc: fix(i, sr_ref[_s5_tile(i - 1), :], si_ref[_s5_tile(i - 1), :], acc), acc)
        dar_ref[...] += _colsum(acc[0])
        dai_ref[...] += _colsum(acc[1])
        gsr = gr_ref[...].astype(BF16)
        gsi = gi_ref[...].astype(BF16)
        ub = u.astype(BF16)
        dbdr_ref[...] += _dot_tn(ub, gsr)
        dbdi_ref[...] += _dot_tn(ub, gsi)
        dup_ref[...] = _dot_nt(gsr, bdr_ref[...]) + _dot_nt(gsi, bdi_ref[...]) + d_ref[...] * dy
        for i in range(seg):
            du_ref[pl.ds(i, SUBLANES, stride=seg), :] = dup_ref[_s5_tile(i), :]
        dd_ref[...] += _colsum(dy * u)

    def rev(t):
        return nt - 1 - t

    def halo(j, t):
        return (jnp.maximum(rev(t) * halo_blocks - 1, 0), j)

    ublk = pl.BlockSpec((tc, UB), lambda j, t: (rev(t), j))
    sblk = pl.BlockSpec((tc, SB), lambda j, t: (rev(t), j))
    bd_spec = pl.BlockSpec((None, UB, SB), lambda j, t: (j, 0, 0))
    cd_spec = pl.BlockSpec((None, SB, UB), lambda j, t: (j, 0, 0))
    svec = pl.BlockSpec((1, SB), lambda j, t: (0, j))
    uvec = pl.BlockSpec((1, UB), lambda j, t: (0, j))
    sds = jax.ShapeDtypeStruct
    return pl.pallas_call(
        body, name="s5_bwd", grid=(GB, nt),
        in_specs=[ublk, ublk, sblk, sblk, pl.BlockSpec((SUBLANES, SB), halo), pl.BlockSpec((SUBLANES, SB), halo),
                  bd_spec, bd_spec, cd_spec, cd_spec, svec, svec, uvec],
        out_specs=[ublk, bd_spec, bd_spec, cd_spec, cd_spec, svec, svec, uvec],
        out_shape=[sds((T, GB * UB), F32), sds((GB, UB, SB), F32), sds((GB, UB, SB), F32),
                   sds((GB, SB, UB), F32), sds((GB, SB, UB), F32), sds((1, GB * SB), F32),
                   sds((1, GB * SB), F32), sds((1, GB * UB), F32)],
        scratch_shapes=[pltpu.VMEM((tc, SB), F32), pltpu.VMEM((tc, SB), F32),
                        pltpu.VMEM((seg, SB), F32), pltpu.VMEM((seg, SB), F32),
                        pltpu.VMEM((SUBLANES, SB), F32), pltpu.VMEM((SUBLANES, SB), F32),
                        pltpu.VMEM((1, SB), F32), pltpu.VMEM((1, SB), F32),
                        pltpu.VMEM((tc, UB), F32), pltpu.VMEM((tc, UB), F32), pltpu.VMEM((tc, UB), F32)],
        compiler_params=_params("arbitrary", "arbitrary"),
    )(dy, proj, sr, si, sr, si, bdr, bdi, cdr, cdi, ar, ai, dvec)


def _shift_down(v, k):
    rows = lax.broadcasted_iota(jnp.int32, v.shape, 0)
    return jnp.where(rows >= k, pltpu.roll(v, k, 0), 0.0)


def _shift_up(v, k):
    n = v.shape[0]
    rows = lax.broadcasted_iota(jnp.int32, v.shape, 0)
    return jnp.where(rows < n - k, pltpu.roll(v, n - k, 0), 0.0)


def _conv_specs(T, cb, n_s5_blocks, n_conv_blocks):
    gb = pl.BlockSpec((T, cb), lambda j: (0, n_s5_blocks + j))
    gc = pl.BlockSpec((T, cb), lambda j: (0, n_s5_blocks + n_conv_blocks + j))
    hh = pl.BlockSpec((T, cb), lambda j: (0, n_s5_blocks + 2 * n_conv_blocks + j))
    return gb, gc, hh


def _conv_fwd(proj, cw, cbias, d_s5, d_conv):
    T = proj.shape[0]
    cb = _tile(d_conv, 256, LANES)

    def body(gb_ref, gc_ref, hh_ref, w_ref, b_ref, z_ref):
        v = gc_ref[...] * hh_ref[...]
        w = w_ref[...]
        cv = b_ref[...] + w[0:1, :] * _shift_down(v, 2) + w[1:2, :] * _shift_down(v, 1) + w[2:3, :] * v
        z_ref[...] = gb_ref[...] * cv

    gb, gc, hh = _conv_specs(T, cb, d_s5 // cb, d_conv // cb)
    col = pl.BlockSpec((T, cb), lambda j: (0, j))
    return pl.pallas_call(
        body, name="conv_fwd", grid=(d_conv // cb,),
        in_specs=[gb, gc, hh, pl.BlockSpec((CONV_W, cb), lambda j: (0, j)), pl.BlockSpec((1, cb), lambda j: (0, j))],
        out_specs=col, out_shape=jax.ShapeDtypeStruct((T, d_conv), F32),
        compiler_params=_params("arbitrary"),
    )(proj, proj, proj, cw, cbias)


def _conv_bwd(dz, proj, cw, cbias, d_s5, d_conv):
    T = proj.shape[0]
    cb = _tile(d_conv, 256, LANES)

    def body(dz_ref, gb_ref, gc_ref, hh_ref, w_ref, b_ref, dgb_ref, dgc_ref, dhh_ref, dw_ref, db_ref):
        gc = gc_ref[...]
        hh = hh_ref[...]
        dz = dz_ref[...]
        w = w_ref[...]
        v = gc * hh
        v1 = _shift_down(v, 1)
        v2 = _shift_down(v, 2)
        cv = b_ref[...] + w[0:1, :] * v2 + w[1:2, :] * v1 + w[2:3, :] * v
        dgb_ref[...] = (dz * cv).astype(BF16)
        dcv = dz * gb_ref[...]
        dv = w[2:3, :] * dcv + w[1:2, :] * _shift_up(dcv, 1) + w[0:1, :] * _shift_up(dcv, 2)
        dgc_ref[...] = (dv * hh).astype(BF16)
        dhh_ref[...] = (dv * gc).astype(BF16)
        dw_ref[0:1, :] = _colsum(dcv * v2)
        dw_ref[1:2, :] = _colsum(dcv * v1)
        dw_ref[2:3, :] = _colsum(dcv * v)
        db_ref[...] = _colsum(dcv)

    gb, gc, hh = _conv_specs(T, cb, d_s5 // cb, d_conv // cb)
    col = pl.BlockSpec((T, cb), lambda j: (0, j))
    wspec = pl.BlockSpec((CONV_W, cb), lambda j: (0, j))
    bspec = pl.BlockSpec((1, cb), lambda j: (0, j))
    sds = jax.ShapeDtypeStruct
    return pl.pallas_call(
        body, name="conv_bwd", grid=(d_conv // cb,),
        in_specs=[col, gb, gc, hh, wspec, bspec],
        out_specs=[col, col, col, wspec, bspec],
        out_shape=[sds((T, d_conv), BF16), sds((T, d_conv), BF16), sds((T, d_conv), BF16),
                   sds((CONV_W, d_conv), F32), sds((1, d_conv), F32)],
        compiler_params=_params("arbitrary"),
    )(dz, proj, proj, proj, cw, cbias)


def _rms(v, g):
    rstd = lax.rsqrt(_rowmean(v * v) + RMS_EPS)
    return v * rstd * g, rstd


def _rms_bwd(dyn, v, rstd, g):
    w = dyn * g
    return rstd * w - v * (rstd * rstd * rstd) * _rowmean(w * v), _colsum(dyn * v * rstd)


def _mix_post(y, z, wglu, g_s5, g_conv):
    T, C = y.shape
    tm = _tile(T, 256, 16)

    def body(y_ref, z_ref, w_ref, gs_ref, gc_ref, m_ref, gl_ref):
        ge = jax.nn.gelu(y_ref[...])
        gl = _dot_nn(ge.astype(BF16), w_ref[...])
        gl_ref[...] = gl
        yn, _ = _rms(ge * jax.nn.sigmoid(gl), gs_ref[...])
        zn, _ = _rms(z_ref[...], gc_ref[...])
        m_ref[:, 0:C] = yn.astype(BF16)
        m_ref[:, C:2 * C] = zn.astype(BF16)

    row = pl.BlockSpec((tm, C), lambda i: (i, 0))
    vec = pl.BlockSpec((1, C), lambda i: (0, 0))
    return pl.pallas_call(
        body, name="mix_post", grid=(T // tm,),
        in_specs=[row, row, pl.BlockSpec((C, C), lambda i: (0, 0)), vec, vec],
        out_specs=[pl.BlockSpec((tm, 2 * C), lambda i: (i, 0)), row],
        out_shape=[jax.ShapeDtypeStruct((T, 2 * C), BF16), jax.ShapeDtypeStruct((T, C), F32)],
        compiler_params=_params("arbitrary"),
    )(y, z, wglu, g_s5, g_conv)


def _mix_post_bwd(dm, y, gl, z, wglu, g_s5, g_conv):
    T, C = y.shape
    tm = _tile(T, 256, 16)

    def body(dm_ref, y_ref, gl_ref, z_ref, w_ref, gs_ref, gc_ref, dy_ref, dz_ref, dw_ref, dgs_ref, dgc_ref):
        @pl.when(pl.program_id(0) == 0)
        def _():
            dw_ref[...] = jnp.zeros_like(dw_ref)
            dgs_ref[...] = jnp.zeros_like(dgs_ref)
            dgc_ref[...] = jnp.zeros_like(dgc_ref)

        yv = y_ref[...]
        ge, gelu_vjp = jax.vjp(jax.nn.gelu, yv)
        gl = gl_ref[...]
        sg = jax.nn.sigmoid(gl)
        y2 = ge * sg
        _, rstd_y = _rms(y2, gs_ref[...])
        dy2, dgs = _rms_bwd(dm_ref[:, 0:C], y2, rstd_y, gs_ref[...])
        dgs_ref[...] += dgs
        dgl = (dy2 * ge * sg * (1.0 - sg)).astype(BF16)
        dge = dy2 * sg + _dot_nt(dgl, w_ref[...])
        dw_ref[...] += _dot_tn(ge.astype(BF16), dgl)
        dy_ref[...] = gelu_vjp(dge)[0]
        zv = z_ref[...]
        _, rstd_z = _rms(zv, gc_ref[...])
        dz, dgc = _rms_bwd(dm_ref[:, C:2 * C], zv, rstd_z, gc_ref[...])
        dz_ref[...] = dz
        dgc_ref[...] += dgc

    row = pl.BlockSpec((tm, C), lambda i: (i, 0))
    vec = pl.BlockSpec((1, C), lambda i: (0, 0))
    full = pl.BlockSpec((C, C), lambda i: (0, 0))
    sds = jax.ShapeDtypeStruct
    return pl.pallas_call(
        body, name="mix_post_bwd", grid=(T // tm,),
        in_specs=[pl.BlockSpec((tm, 2 * C), lambda i: (i, 0)), row, row, row, full, vec, vec],
        out_specs=[row, row, full, vec, vec],
        out_shape=[sds((T, C), F32), sds((T, C), F32), sds((C, C), F32), sds((1, C), F32), sds((1, C), F32)],
        compiler_params=_params("arbitrary"),
    )(dm, y, gl, z, wglu, g_s5, g_conv)


def _adamw(w, g, m, v):
    m = ADAM_B1 * m + (1.0 - ADAM_B1) * g
    v = ADAM_B2 * v + (1.0 - ADAM_B2) * (g * g)
    m_hat = m / (1.0 - ADAM_B1 ** ADAM_STEP)
    v_hat = v / (1.0 - ADAM_B2 ** ADAM_STEP)
    return -ADAM_LR * (m_hat / (jnp.sqrt(v_hat) + ADAM_EPS) + ADAM_WD * w), m, v


def _sum_parts(p_ref):
    total = p_ref[0].astype(F32)
    for d in range(1, p_ref.shape[0]):
        total = total + p_ref[d].astype(F32)
    return total


def _row_tile(R, C, n_streams):
    budget = VMEM_LIMIT // 3 // (n_streams * C * 4)
    return _tile(R, max(BF16_ROWS, budget), BF16_ROWS)


def _reduce_parts(parts):
    L, P, R, C = parts.shape
    tr = _row_tile(R, C, P + 1)

    def body(p_ref, o_ref):
        o_ref[...] = _sum_parts(p_ref)

    return pl.pallas_call(
        body, name="reduce_parts", grid=(L, R // tr),
        in_specs=[pl.BlockSpec((None, P, tr, C), lambda l, i: (l, 0, i, 0))],
        out_specs=pl.BlockSpec((None, tr, C), lambda l, i: (l, i, 0)),
        out_shape=jax.ShapeDtypeStruct((L, R, C), F32),
        compiler_params=_params("arbitrary", "arbitrary"),
    )(parts)


def _adamw_update(w, m, v, grad=None, parts=None):
    L, R, C = w.shape
    from_parts = parts is not None
    P = parts.shape[1] if from_parts else 1
    tr = _row_tile(R, C, P + 7)

    def body(g_in_ref, w_ref, m_ref, v_ref, g_ref, d_ref, nm_ref, nv_ref):
        g = _sum_parts(g_in_ref) if from_parts else g_in_ref[...]
        delta, nm, nv = _adamw(w_ref[...], g, m_ref[...], v_ref[...])
        g_ref[...] = g
        d_ref[...] = delta
        nm_ref[...] = nm
        nv_ref[...] = nv

    blk = pl.BlockSpec((None, tr, C), lambda l, i: (l, i, 0))
    g_spec = pl.BlockSpec((None, P, tr, C), lambda l, i: (l, 0, i, 0)) if from_parts else blk
    out = jax.ShapeDtypeStruct((L, R, C), F32)
    return pl.pallas_call(
        body, name="adamw_parts" if from_parts else "adamw", grid=(L, R // tr),
        in_specs=[g_spec, blk, blk, blk], out_specs=[blk, blk, blk, blk], out_shape=[out, out, out, out],
        compiler_params=_params("arbitrary", "arbitrary"),
    )(parts if from_parts else grad, w, m, v)


def _pair_sum(full, stage):
    _, R, C = full.shape
    tr = _row_tile(R, C, 4)

    def body(f_ref, s_ref, o_ref):
        mine = f_ref[lax.axis_index("c")]
        o_ref[...] = (mine.astype(F32) + s_ref[...].astype(F32)).astype(BF16)

    blk = pl.BlockSpec((None, tr, C), lambda q, i: (q, i, 0))
    return pl.pallas_call(
        body, name="pair_sum", grid=(N_CHIPS, R // tr),
        in_specs=[pl.BlockSpec((None, 2, tr, C), lambda q, i: (q, 0, i, 0)), blk],
        out_specs=blk, out_shape=jax.ShapeDtypeStruct((N_CHIPS, R, C), BF16),
        compiler_params=_params("arbitrary", "arbitrary"),
    )(full.reshape(N_CHIPS, 2, R, C), stage)


def _me():
    x, y, c = (lax.axis_index(a) for a in AXES)
    return x, y, c, 4 * x + 2 * y + c


def _peer(rel):
    x, y, c, _ = _me()
    px = 1 - x if rel & 4 else x
    py = 1 - y if rel & 2 else y
    pc = 1 - c if rel & 1 else c
    return (px, py, pc), 4 * px + 2 * py + pc


DATAFLOW = pltpu.SideEffectType.DATAFLOW_SIDE_EFFECTING


def _remote_copy(src, dst, sems):
    return functools.partial(pltpu.make_async_remote_copy, src_ref=src, dst_ref=dst, **sems)


ALL_PEERS = tuple(range(1, N_DEV))
SIBLING = 1
OTHER_CHIPS = (2, 4, 6)
SIBLING_AND_OTHER_CHIPS = (SIBLING,) + OTHER_CHIPS


def _slot(dev, blk, same_core, own_core_last):
    if not own_core_last:
        return blk
    return (N_CHIPS if same_core else 0) + 2 * dev[0] + dev[1]


def _gather_copies(n, rels=ALL_PEERS, own_core_last=False):
    def copies(srcs, lands, send_sems, recv_sems, local_sems):
        x, y, c, me = _me()
        local, remote = [], []
        for k in range(n):
            local.append(functools.partial(pltpu.make_async_copy, srcs[k],
                                           lands[k].at[_slot((x, y, c), me, True, own_core_last)], local_sems.at[k]))
            for rel in rels:
                dev, blk = _peer(rel)
                same_core = not rel & SIBLING
                sems = dict(send_sem=send_sems.at[_sem_index(k, rel)], recv_sem=recv_sems.at[_sem_index(k, rel)],
                            device_id=dev, device_id_type=MESH)
                remote.append((_remote_copy(srcs[k], lands[k].at[_slot((x, y, c), me, same_core, own_core_last)], sems),
                               _remote_copy(srcs[k], lands[k].at[_slot(dev, blk, same_core, own_core_last)], sems)))
        return local, remote

    copies.n_arrays = n
    return copies


def _forward_copies(n, own_core_last=False):
    def copies(srcs, lands, send_sems, recv_sems, local_sems):
        sibling = _peer(SIBLING)[0]
        remote = []
        for k in range(n):
            for rel in OTHER_CHIPS:
                dev, blk = _peer(rel)
                have = _slot(dev, blk, True, own_core_last)
                there = _slot(dev, blk, False, own_core_last)
                other = _peer(rel | SIBLING)
                comes = _slot(other[0], other[1], False, own_core_last)
                sems = dict(send_sem=send_sems.at[_sem_index(k, rel)], recv_sem=recv_sems.at[_sem_index(k, rel)],
                            device_id=sibling, device_id_type=MESH)
                remote.append((_remote_copy(lands[k].at[have], lands[k].at[there], sems),
                               _remote_copy(lands[k].at[have], lands[k].at[comes], sems)))
        return [], remote

    copies.n_arrays = n
    return copies


def _scatter_copies(n, layer):
    def copies(srcs, lands, send_sems, recv_sems, local_sems):
        me = _me()[3]
        local, remote = [], []
        for k in range(n):
            local.append(functools.partial(pltpu.make_async_copy, srcs[k].at[me], lands[k].at[layer, me],
                                           local_sems.at[k]))
            for rel in range(1, N_DEV):
                dev, blk = _peer(rel)
                sems = dict(send_sem=send_sems.at[_sem_index(k, rel)], recv_sem=recv_sems.at[_sem_index(k, rel)],
                            device_id=dev, device_id_type=MESH)
                remote.append((_remote_copy(srcs[k].at[blk], lands[k].at[layer, me], sems),
                               _remote_copy(srcs[k].at[blk], lands[k].at[layer, blk], sems)))
        return local, remote

    copies.n_arrays = n
    return copies


def _swap_copies(n):
    def copies(srcs, lands, send_sems, recv_sems, local_sems):
        sibling = _peer(SIBLING)[0]
        remote = []
        for k in range(n):
            sems = dict(send_sem=send_sems.at[_sem_index(k, SIBLING)], recv_sem=recv_sems.at[_sem_index(k, SIBLING)],
                        device_id=sibling, device_id_type=MESH)
            remote.append((_remote_copy(srcs[k], lands[k], sems), _remote_copy(srcs[k], lands[k], sems)))
        return [], remote

    copies.n_arrays = n
    return copies


def _pair_copies(n):
    def copies(srcs, lands, send_sems, recv_sems, local_sems):
        c = _me()[2]
        sibling = _peer(SIBLING)[0]
        remote = []
        for k in range(n):
            for chip in range(N_CHIPS):
                sems = dict(send_sem=send_sems.at[_sem_index(k, chip + 1)], recv_sem=recv_sems.at[_sem_index(k, chip + 1)],
                            device_id=sibling, device_id_type=MESH)
                block = srcs[k].at[2 * chip + 1 - c]
                remote.append((_remote_copy(block, lands[k].at[chip], sems), _remote_copy(block, lands[k].at[chip], sems)))
        return [], remote

    copies.n_arrays = n
    return copies


def _chip_scatter_copies(n, layer):
    def copies(srcs, lands, send_sems, recv_sems, local_sems):
        x, y, _, _ = _me()
        my_chip = 2 * x + y
        local, remote = [], []
        for k in range(n):
            local.append(functools.partial(pltpu.make_async_copy, srcs[k].at[my_chip], lands[k].at[layer, my_chip],
                                           local_sems.at[k]))
            for rel in OTHER_CHIPS:
                dev = _peer(rel)[0]
                chip = 2 * dev[0] + dev[1]
                sems = dict(send_sem=send_sems.at[_sem_index(k, rel)], recv_sem=recv_sems.at[_sem_index(k, rel)],
                            device_id=dev, device_id_type=MESH)
                remote.append((_remote_copy(srcs[k].at[chip], lands[k].at[layer, my_chip], sems),
                               _remote_copy(srcs[k].at[chip], lands[k].at[layer, chip], sems)))
        return local, remote

    copies.n_arrays = n
    return copies


def _sem_shapes(n):
    return [pltpu.SemaphoreType.DMA((n * (N_DEV - 1),)), pltpu.SemaphoreType.DMA((n * (N_DEV - 1),)),
            pltpu.SemaphoreType.DMA((n,))]


def _sem_index(k, rel):
    return k * (N_DEV - 1) + rel - 1


def _exchange(copies, name, srcs, lands):
    n_src, n_land = len(srcs), len(lands)

    def body(*refs):
        src_refs = refs[:n_src]
        land_refs = refs[n_src + n_land:n_src + 2 * n_land]
        local, remote = copies(src_refs, land_refs, *refs[n_src + 2 * n_land:])
        local = [cp() for cp in local]
        sends = [send() for send, _ in remote]
        for cp in local + sends:
            cp.start()
        for send, (_, landing) in zip(sends, remote):
            send.wait_send()
            landing().wait_recv()
        for cp in local:
            cp.wait()

    return pl.pallas_call(
        body, name=name, in_specs=[HBM_SPEC] * (n_src + n_land), out_specs=[HBM_SPEC] * n_land,
        out_shape=[jax.ShapeDtypeStruct(b.shape, b.dtype) for b in lands],
        scratch_shapes=_sem_shapes(copies.n_arrays),
        input_output_aliases={n_src + k: k for k in range(n_land)},
        compiler_params=pltpu.CompilerParams(has_side_effects=True),
    )(*srcs, *lands)


def _hbm(arrays):
    return [pltpu.with_memory_space_constraint(a, pltpu.HBM) for a in arrays]


def _exchange_start(copies, name, srcs, lands, after):
    n_src, n_land, n_after = len(srcs), len(lands), len(after)
    n_data = n_src + n_land

    def body(*refs):
        outs = refs[n_data + n_after:]
        local, remote = copies(refs[:n_src], refs[n_src:n_data], *outs[:3])
        for cp in local:
            cp().start()
        for send, _ in remote:
            send().start()
        outs[-1][...] = jnp.zeros_like(outs[-1])

    res = pl.pallas_call(
        body, name=name, in_specs=[HBM_SPEC] * n_data + [ANY_SPEC] * n_after,
        out_specs=[SEM_SPEC] * 3 + [HBM_SPEC] * n_data + [pl.BlockSpec(memory_space=pltpu.VMEM)],
        out_shape=_sem_shapes(copies.n_arrays) + [pltpu.HBM(a.shape, a.dtype) for a in list(srcs) + list(lands)]
        + [jax.ShapeDtypeStruct((SUBLANES, LANES), F32)],
        input_output_aliases={k: 3 + k for k in range(n_data)},
        compiler_params=pltpu.CompilerParams(has_side_effects=DATAFLOW),
    )(*_hbm(list(srcs) + list(lands)), *after)
    return res[:3], res[3:3 + n_src], res[3 + n_src:3 + n_data], res[-1]


def _exchange_wait(copies, name, sems, srcs, lands, after):
    n_src, n_land, n_after = len(srcs), len(lands), len(after)
    n_data = n_src + n_land

    def body(*refs):
        local, remote = copies(refs[:n_src], refs[n_src:n_data], *refs[n_data:n_data + 3])
        for send, landing in remote:
            send().wait_send()
            landing().wait_recv()
        for cp in local:
            cp().wait()

    res = pl.pallas_call(
        body, name=name, in_specs=[HBM_SPEC] * n_data + [SEM_SPEC] * 3 + [ANY_SPEC] * n_after,
        out_specs=[HBM_SPEC] * n_data,
        out_shape=[pltpu.HBM(a.shape, a.dtype) for a in list(srcs) + list(lands)],
        input_output_aliases={k: k for k in range(n_data)},
        compiler_params=pltpu.CompilerParams(has_side_effects=DATAFLOW),
    )(*srcs, *lands, *sems, *after)
    return res[:n_src], res[n_src:]


def _block_diag(blocks, row_major):
    L, GB, g, P, N = blocks.shape
    eye = jnp.eye(g, dtype=blocks.dtype)
    if row_major:
        return jnp.einsum("lbgpn,gh->lbgphn", blocks, eye).reshape(L, GB, g * P, g * N)
    return jnp.einsum("lbgpn,gh->lbhngp", blocks, eye).reshape(L, GB, g * N, g * P)


def _diag_blocks(mat, g, P, N, row_major):
    GB = mat.shape[0]
    eye = jnp.eye(g, dtype=mat.dtype)
    if row_major:
        return jnp.einsum("bgphn,gh->bgpn", mat.reshape(GB, g, P, g, N), eye)
    return jnp.einsum("bhngp,gh->bgpn", mat.reshape(GB, g, N, g, P), eye)


def _pack(arrays, rows_multiple):
    flat = jnp.concatenate([a.reshape(-1).astype(F32) for a in arrays])
    pad = (-flat.shape[0]) % (rows_multiple * LANES)
    return jnp.pad(flat, (0, pad)).reshape(-1, LANES)


def _unpack(packed, shapes):
    flat = packed.reshape(-1)
    out, pos = [], 0
    for s in shapes:
        n = math.prod(s)
        out.append(flat[pos:pos + n].reshape(s))
        pos += n
    return out


SMALL = ["ln1_g", "ln1_b", "s5_lam_re", "s5_lam_im", "s5_log_dt", "s5_b_re", "s5_b_im", "s5_c_re", "s5_c_im", "s5_d",
         "conv_b", "g_s5", "g_conv", "ln2_g", "ln2_b", "ln3_g", "ln3_b"]
S5_B = ["s5_b_re", "s5_b_im"]
WEIGHTS = ["ffn1_gate", "ffn1_up", "ffn1_down", "ln1_g", "ln1_b", "w_in", "s5_lam_re", "s5_lam_im", "s5_log_dt",
           "s5_b_re", "s5_b_im", "s5_c_re", "s5_c_im", "s5_d", "s5_w_glu", "conv_w", "conv_b", "g_s5", "g_conv",
           "w_out", "ln2_g", "ln2_b", "ffn2_gate", "ffn2_up", "ffn2_down", "ln3_g", "ln3_b"]
TRANSPOSED = ["ffn1_gate", "ffn1_up", "w_in", "ffn2_gate", "ffn2_up"]
UPDATED_TRANSPOSED = ["ffn1_gate", "ffn1_up", "ffn2_gate", "ffn2_up"]
GROUPS = {"a": ["ffn1_gate", "ffn1_up", "ffn1_down"], "b": ["w_in", "s5_w_glu", "w_out"],
          "c": ["ffn2_gate", "ffn2_up", "ffn2_down"]}
FFN_GROUPS = ("a", "c")


def _train_step(x, target, w, m, v):
    T, D = x.shape
    L = w["ln1_g"].shape[0]
    alpha = (2.0 * L) ** 0.25
    G = w["s5_log_dt"].shape[1]
    d_s5 = G * S5_P
    d_conv = w["conv_b"].shape[1]
    GB = G // S5_GROUPS_PER_BLOCK
    me = _me()[3]

    def shard(n, l):
        return (jnp.swapaxes(w[n][l], 0, 1) if n in TRANSPOSED else w[n][l]).astype(BF16)

    conv_w_rows = jnp.pad(w["conv_w"], ((0, 0), (0, SUBLANES - CONV_W), (0, 0)))
    parts = [(l, grp) for l in range(L) for grp in GROUPS]
    step_one, step_two, token = {}, {}, []
    for l, grp in parts:
        srcs = [shard(n, l) for n in GROUPS[grp]] + ([conv_w_rows[l]] if grp == "b" else [])
        lands = [lax.empty((N_DEV,) + a.shape, a.dtype) for a in srcs]
        sems, srcs, lands, tok = _exchange_start(_gather_copies(len(srcs), SIBLING_AND_OTHER_CHIPS, grp in FFN_GROUPS),
                                                 f"gather_start_{l}{grp}", srcs, lands, token)
        step_one[l, grp] = (sems, srcs, lands)
        token = [tok]

    def forward_on(i, after):
        if i >= len(parts):
            return []
        l, grp = parts[i]
        sems, srcs, lands = step_one[l, grp]
        copies = _gather_copies(len(srcs), SIBLING_AND_OTHER_CHIPS, grp in FFN_GROUPS)
        _, lands = _exchange_wait(copies, f"gather_wait_{l}{grp}", sems, srcs, lands, after)
        sems, _, lands, tok = _exchange_start(_forward_copies(len(lands), grp in FFN_GROUPS), f"forward_start_{l}{grp}",
                                              [], lands, [])
        step_two[l, grp] = (sems, lands)
        return [tok]

    def gathered(i, after):
        l, grp = parts[i]
        sems, lands = step_two[l, grp]
        _, lands = _exchange_wait(_forward_copies(len(lands), grp in FFN_GROUPS), f"forward_wait_{l}{grp}", sems, [],
                                  lands, after)
        full = {n: p.reshape(-1, p.shape[-1]) for n, p in zip(GROUPS[grp], lands)}
        if grp == "b":
            full["conv_w"] = jnp.swapaxes(lands[-1][:, :CONV_W, :], 0, 1).reshape(CONV_W, d_conv)
        return full

    lre = w["s5_lam_re"].reshape(L * G, S5_N)
    lim = w["s5_lam_im"].reshape(L * G, S5_N)
    ldt = w["s5_log_dt"].reshape(L * G, 1)
    b_re = jnp.transpose(w["s5_b_re"], (3, 0, 1, 2)).reshape(S5_P, L * G, S5_N)
    b_im = jnp.transpose(w["s5_b_im"], (3, 0, 1, 2)).reshape(S5_P, L * G, S5_N)
    ab_re, ab_im, bb_re, bb_im = _s5_params_fwd(lre, lim, ldt, b_re, b_im)

    def groups(bb):
        return jnp.transpose(bb.reshape(S5_P, L, GB, S5_GROUPS_PER_BLOCK, S5_N), (1, 2, 3, 0, 4))

    bd_re = _block_diag(groups(bb_re), True).astype(BF16)
    bd_im = _block_diag(groups(bb_im), True).astype(BF16)
    c_shape = (L, GB, S5_GROUPS_PER_BLOCK, S5_P, S5_N)
    cd_re = _block_diag(w["s5_c_re"].reshape(c_shape), False).astype(BF16)
    cd_im = _block_diag(w["s5_c_im"].reshape(c_shape), False).astype(BF16)
    a_re = ab_re.reshape(L, 1, G * S5_N)
    a_im = ab_im.reshape(L, 1, G * S5_N)
    d_vec = w["s5_d"].reshape(L, 1, d_s5)

    def vec(name, l):
        return w[name][l].reshape(1, -1)

    saved, weights = [], []
    x_in, x_in_b = x, x.astype(BF16)
    token = forward_on(0, token)
    for l in range(L):
        i = len(GROUPS) * l
        gw = gathered(i, [x_in] if l else token)
        s = {"x0b": x_in_b}
        s["g1"], s["u1"], s["h1"] = _ffn_up(x_in_b, gw["ffn1_gate"], gw["ffn1_up"])
        x1, s["x1b"], s["xh1"], s["rstd1"] = _mm_res_ln(s["h1"], gw["ffn1_down"], x_in, vec("ln1_g", l),
                                                         vec("ln1_b", l), 0.5, alpha, forward_on(i + 1, [s["h1"]]))
        gw.update(gathered(i + 1, [s["x1b"]]))
        s["proj"] = _mm_nt(s["x1b"], gw["w_in"])
        s["y"], s["sr"], s["si"] = _s5_fwd(s["proj"], bd_re[l], bd_im[l], cd_re[l], cd_im[l], a_re[l], a_im[l],
                                           d_vec[l])
        s["z"] = _conv_fwd(s["proj"], gw["conv_w"], vec("conv_b", l), d_s5, d_conv)
        s["mcat"], s["gl"] = _mix_post(s["y"], s["z"], gw["s5_w_glu"], vec("g_s5", l), vec("g_conv", l))
        x2, s["x2b"], s["xh2"], s["rstd2"] = _mm_res_ln(s["mcat"], gw["w_out"], x1, vec("ln2_g", l),
                                                         vec("ln2_b", l), 1.0, alpha, forward_on(i + 2, [s["mcat"]]))
        gw.update(gathered(i + 2, [s["x2b"]]))
        s["g2"], s["u2"], s["h2"] = _ffn_up(s["x2b"], gw["ffn2_gate"], gw["ffn2_up"])
        x3, x3b, s["xh3"], s["rstd3"] = _mm_res_ln(s["h2"], gw["ffn2_down"], x2, vec("ln3_g", l), vec("ln3_b", l),
                                                   0.5, alpha, forward_on(i + 3, [s["h2"]]))
        saved.append(s)
        weights.append(gw)
        x_in, x_in_b = x3, x3b

    last = saved[L - 1]
    dr, drb, dg, db, loss = _loss_ln_bwd(x_in, target, last["xh3"], last["rstd3"], vec("ln3_g", L - 1))
    small = [dict() for _ in range(L)]
    small[L - 1]["ln3_g"], small[L - 1]["ln3_b"] = dg, db
    bufs = {grp: [lax.empty((L, N_CHIPS) + shard(n, 0).shape, BF16) for n in names] for grp, names in GROUPS.items()}
    scatters = {grp: [] for grp in GROUPS}
    grad_x = None

    def chip_scatter(l, grp, sums):
        sems, sums, bufs[grp], tok = _exchange_start(_chip_scatter_copies(len(sums), l), f"scatter_start_{l}{grp}", sums,
                                                     bufs[grp], [])
        scatters[grp].append((l, sems, sums))
        return [tok]

    small_names = SMALL + ["conv_w"]
    b_shape = (L, G, S5_P, S5_N)
    small_shapes = [b_shape if n in S5_B else w[n].shape for n in SMALL] + [(L, CONV_W, d_conv)]

    def small_scatter_start(small):
        def stack(key):
            return jnp.stack([small[l][key] for l in range(L)])

        d_bb_re = jnp.transpose(stack("d_bb_re").reshape(L * G, S5_P, S5_N), (1, 0, 2))
        d_bb_im = jnp.transpose(stack("d_bb_im").reshape(L * G, S5_P, S5_N), (1, 0, 2))
        g_lre, g_lim, g_ldt, g_bre, g_bim = _s5_params_bwd(
            lre, lim, ldt, b_re, b_im, stack("d_ab_re").reshape(L * G, S5_N), stack("d_ab_im").reshape(L * G, S5_N),
            d_bb_re, d_bb_im)
        part = {n: [small[l][n] for l in range(L)]
                for n in ["ln1_g", "ln1_b", "s5_c_re", "s5_c_im", "s5_d", "conv_b", "g_s5", "g_conv", "ln2_g", "ln2_b",
                          "ln3_g", "ln3_b", "conv_w"]}
        part["s5_lam_re"], part["s5_lam_im"], part["s5_log_dt"] = [g_lre], [g_lim], [g_ldt]
        part["s5_b_re"] = [jnp.transpose(g_bre, (1, 0, 2))]
        part["s5_b_im"] = [jnp.transpose(g_bim, (1, 0, 2))]
        packed = _pack([piece for n in small_names for piece in part[n]], N_DEV * PACK_ROWS)
        rows = packed.shape[0] // N_DEV
        return _exchange_start(_scatter_copies(1, 0), "scatter_small_start", [packed.reshape(N_DEV, rows, LANES)],
                               [lax.empty((1, N_DEV, rows, LANES), F32)], [])

    token = []
    for l in reversed(range(L)):
        gw, s, sm = weights[l], saved[l], small[l]
        full = {}

        def ffn_bwd(dr, drb, tag, grp, xb_in, ln, after):
            dgp, dup = _ffn_down_bwd(drb, gw[f"ffn{tag}_down"], s[f"g{tag}"], s[f"u{tag}"], after)
            terms = {"gate": (dgp, xb_in, 1.0), "up": (dup, xb_in, 1.0), "down": (s[f"h{tag}"], drb, 0.5)}
            terms = [terms[n.split("_")[1]] for n in GROUPS[grp]]
            away = [_mm_tn(a, b, scale, BF16, half=0).reshape((N_CHIPS, -1, b.shape[1])) for a, b, scale in terms]
            stages = [lax.empty(a.shape, BF16) for a in away]
            sems, away, stages, tok = _exchange_start(_swap_copies(len(away)), f"pair_start_{l}{grp}", away, stages, [])
            res = _mm_dx([(dgp, gw[f"ffn{tag}_gate"]), (dup, gw[f"ffn{tag}_up"])], dr, alpha, ln, [tok])
            _, stages = _exchange_wait(_swap_copies(len(away)), f"pair_wait_{l}{grp}", sems, away, stages, [res[0]])
            sums = [_mm_tn(a, b, scale, BF16, half=1, addend=st.reshape(-1, b.shape[1])).reshape(st.shape)
                    for (a, b, scale), st in zip(terms, stages)]
            return res, chip_scatter(l, grp, sums)

        (dr, drb, sm["ln2_g"], sm["ln2_b"]), token = ffn_bwd(dr, drb, 2, "c", s["x2b"],
                                                              (s["xh2"], s["rstd2"], vec("ln2_g", l)), token)
        dm = _mm_nt(drb, gw["w_out"], token)
        full["w_out"] = _mm_tn(s["mcat"], drb, 1.0, BF16)
        dy, dz, dwglu, sm["g_s5"], sm["g_conv"] = _mix_post_bwd(dm, s["y"], s["gl"], s["z"], gw["s5_w_glu"],
                                                                vec("g_s5", l), vec("g_conv", l))
        full["s5_w_glu"] = dwglu.astype(BF16)
        du, dbd_re, dbd_im, dcd_re, dcd_im, sm["d_ab_re"], sm["d_ab_im"], sm["s5_d"] = _s5_bwd(
            dy, s["proj"], s["sr"], s["si"], bd_re[l], bd_im[l], cd_re[l], cd_im[l], a_re[l], a_im[l], d_vec[l])
        gsz = (S5_GROUPS_PER_BLOCK, S5_P, S5_N)
        sm["d_bb_re"] = _diag_blocks(dbd_re, *gsz, True)
        sm["d_bb_im"] = _diag_blocks(dbd_im, *gsz, True)
        sm["s5_c_re"] = _diag_blocks(dcd_re, *gsz, False).reshape(G, S5_P, S5_N)
        sm["s5_c_im"] = _diag_blocks(dcd_im, *gsz, False).reshape(G, S5_P, S5_N)
        dgb, dgc, dhh, sm["conv_w"], sm["conv_b"] = _conv_bwd(dz, s["proj"], gw["conv_w"], vec("conv_b", l),
                                                              d_s5, d_conv)
        dproj = jnp.concatenate([du.astype(BF16), dgb, dgc, dhh], axis=1)
        full["w_in"] = _mm_tn(dproj, s["x1b"], 1.0, BF16)
        fulls = [full[n].reshape((N_DEV, -1) + full[n].shape[1:]) for n in GROUPS["b"]]
        stages = [lax.empty((N_CHIPS,) + f.shape[1:], BF16) for f in fulls]
        sems, fulls, stages, tok = _exchange_start(_pair_copies(len(fulls)), f"pair_start_{l}b", fulls, stages, [])
        dr, drb, sm["ln1_g"], sm["ln1_b"] = _mm_dx([(dproj, gw["w_in"])], dr, alpha,
                                                   (s["xh1"], s["rstd1"], vec("ln1_g", l)), [tok])
        fulls, stages = _exchange_wait(_pair_copies(len(fulls)), f"pair_wait_{l}b", sems, fulls, stages, [dr])
        token = chip_scatter(l, "b", [_pair_sum(f, st) for f, st in zip(fulls, stages)])
        if l > 0:
            prev = saved[l - 1]
            (dr, drb, small[l - 1]["ln3_g"], small[l - 1]["ln3_b"]), token = ffn_bwd(
                dr, drb, 1, "a", s["x0b"], (prev["xh3"], prev["rstd3"], vec("ln3_g", l - 1)), token)
        else:
            small_sems, small_srcs, small_lands, tok = small_scatter_start(small)
            ((grad_x,), token) = ffn_bwd(dr, drb, 1, "a", s["x0b"], None, token + [tok])

    _, landed = _exchange_wait(_scatter_copies(1, 0), "scatter_small_wait", small_sems, small_srcs, small_lands,
                               [grad_x])
    mine = _reduce_parts(landed[0])
    (summed,) = _exchange(_gather_copies(1), "gather_small", [mine[0]], [lax.empty(landed[0].shape[1:], F32)])
    small_grads = dict(zip(small_names, _unpack(summed, small_shapes)))

    out = {}

    def update(name, w3, m3, v3, shape, **grad):
        res = _adamw_update(w3, m3, v3, **grad)
        out[name] = [r.reshape(shape) for r in res]
        return res

    cw_shape = w["conv_w"].shape
    g_cw = lax.dynamic_slice_in_dim(small_grads["conv_w"], me * cw_shape[2], cw_shape[2], axis=2)
    done = [update("conv_w", w["conv_w"], m["conv_w"], v["conv_w"], cw_shape, grad=g_cw)[3]]
    for n in SMALL:
        if n in S5_B:
            def view(a):
                return jnp.swapaxes(a, 2, 3).reshape(1, -1, S5_N)
            res = _adamw_update(view(w[n]), view(m[n]), view(v[n]), grad=small_grads[n].reshape(1, -1, S5_N))
            out[n] = [jnp.swapaxes(r.reshape(b_shape), 2, 3) for r in res]
        else:
            res = update(n, *(a[n].reshape(1, -1, a[n].shape[-1]) for a in (w, m, v)), w[n].shape,
                         grad=small_grads[n].reshape(1, -1, w[n].shape[-1]))
        done.append(res[3])
    for grp in ("c", "b", "a"):
        for l, sems, sums in scatters[grp]:
            _, bufs[grp] = _exchange_wait(_chip_scatter_copies(len(sums), l), f"scatter_wait_{l}{grp}", sems, sums,
                                          bufs[grp], [grad_x] + token + done)
        done = []
        for n, parts in zip(GROUPS[grp], bufs[grp]):
            if n in UPDATED_TRANSPOSED:
                res = _adamw_update(*(jnp.swapaxes(a[n], 1, 2) for a in (w, m, v)), parts=parts)
                out[n] = [jnp.swapaxes(r, 1, 2) for r in res]
            elif n in TRANSPOSED:
                res = update(n, w[n], m[n], v[n], w[n].shape, grad=jnp.swapaxes(_reduce_parts(parts), 1, 2))
            else:
                res = update(n, w[n], m[n], v[n], w[n].shape, parts=parts)
            done.append(res[3])

    loss = lax.psum(loss[0, 0], AXES)
    return loss, grad_x, out


def kernel(x, ffn1_gate, ffn1_up, ffn1_down, ln1_g, ln1_b, w_in, s5_lam_re, s5_lam_im, s5_log_dt, s5_b_re, s5_b_im, s5_c_re, s5_c_im, s5_d, s5_w_glu, conv_w, conv_b, g_s5, g_conv, w_out, ln2_g, ln2_b, ffn2_gate, ffn2_up, ffn2_down, ln3_g, ln3_b, loss_target, m_ffn1_gate, m_ffn1_up, m_ffn1_down, m_ln1_g, m_ln1_b, m_w_in, m_s5_lam_re, m_s5_lam_im, m_s5_log_dt, m_s5_b_re, m_s5_b_im, m_s5_c_re, m_s5_c_im, m_s5_d, m_s5_w_glu, m_conv_w, m_conv_b, m_g_s5, m_g_conv, m_w_out, m_ln2_g, m_ln2_b, m_ffn2_gate, m_ffn2_up, m_ffn2_down, m_ln3_g, m_ln3_b, v_ffn1_gate, v_ffn1_up, v_ffn1_down, v_ln1_g, v_ln1_b, v_w_in, v_s5_lam_re, v_s5_lam_im, v_s5_log_dt, v_s5_b_re, v_s5_b_im, v_s5_c_re, v_s5_c_im, v_s5_d, v_s5_w_glu, v_conv_w, v_conv_b, v_g_s5, v_g_conv, v_w_out, v_ln2_g, v_ln2_b, v_ffn2_gate, v_ffn2_up, v_ffn2_down, v_ln3_g, v_ln3_b):
    given = dict(locals())
    w = {n: given[n] for n in WEIGHTS}
    m = {n: given["m_" + n] for n in WEIGHTS}
    v = {n: given["v_" + n] for n in WEIGHTS}
    T, D = x.shape[-2:]
    loss, grad_x, out = _train_step(x.reshape(T, D), loss_target.reshape(T, D), w, m, v)
    results = [loss, grad_x.reshape(x.shape)]
    for i in range(4):
        results += [out[n][i] for n in WEIGHTS]
    return tuple(results)
```

```python
import functools
import math

import jax
import jax.numpy as jnp
from jax import lax
from jax.experimental import pallas as pl
from jax.experimental.pallas import tpu as pltpu

F32 = jnp.float32
BF16 = jnp.bfloat16
MESH = pl.DeviceIdType.MESH
AXES = ("x", "y", "c")
N_DEV = 8
N_CHIPS = 4

S5_P = 16
S5_N = 64
CONV_W = 3
LN_EPS = 1e-5
RMS_EPS = 1e-6
ADAM_LR = 0.001
ADAM_B1 = 0.9
ADAM_B2 = 0.999
ADAM_EPS = 1e-08
ADAM_WD = 0.01
ADAM_STEP = 10

V7X_VMEM_BYTES = 64 * 1024 * 1024
VMEM_LIMIT = V7X_VMEM_BYTES * 7 // 8
LANES = 128
SUBLANES = 8
BF16_ROWS = 16
MXU_COLS = 256
ROWS_RESIDENT = 2048
PACK_ROWS = 512
S5_GROUPS_PER_BLOCK = LANES // S5_P
S5_STATE_BLOCK = S5_GROUPS_PER_BLOCK * S5_N

HBM_SPEC = pl.BlockSpec(memory_space=pltpu.HBM)
SEM_SPEC = pl.BlockSpec(memory_space=pltpu.SEMAPHORE)
ANY_SPEC = pl.BlockSpec(memory_space=pl.ANY)


def _tile(n, pref, align):
    best = None
    d = align
    while d <= min(n, pref):
        if n % d == 0:
            best = d
        d += align
    return best if best is not None else n


def _params(*sem):
    return pltpu.CompilerParams(dimension_semantics=sem, vmem_limit_bytes=VMEM_LIMIT)


def _dot_nn(a, b):
    return lax.dot_general(a, b, (((1,), (0,)), ((), ())), preferred_element_type=F32)


def _dot_nt(a, b):
    return lax.dot_general(a, b, (((1,), (1,)), ((), ())), preferred_element_type=F32)


def _dot_tn(a, b):
    return lax.dot_general(a, b, (((0,), (0,)), ((), ())), preferred_element_type=F32)


def _colsum(v):
    return jnp.sum(v, axis=0, keepdims=True)


def _rowmean(v):
    return jnp.mean(v, axis=-1, keepdims=True)


def _ffn_up(xb, wg, wu):
    T, D = xb.shape
    F = wg.shape[0]
    tm = _tile(T, ROWS_RESIDENT, 16)
    tn = _tile(F, MXU_COLS, LANES)

    def body(x_ref, wg_ref, wu_ref, cg_ref, cu_ref, h_ref):
        x = x_ref[...]
        g = _dot_nt(x, wg_ref[...])
        u = _dot_nt(x, wu_ref[...])
        sg = jax.nn.sigmoid(g)
        silu = g * sg
        cu_ref[...] = silu.astype(BF16)
        cg_ref[...] = (u * (sg * (1.0 + g * (1.0 - sg)))).astype(BF16)
        h_ref[...] = (silu * u).astype(BF16)

    w_spec = pl.BlockSpec((tn, D), lambda j, i: (j, 0))
    o_spec = pl.BlockSpec((tm, tn), lambda j, i: (i, j))
    return pl.pallas_call(
        body, name="ffn_up", grid=(F // tn, T // tm),
        in_specs=[pl.BlockSpec((tm, D), lambda j, i: (i, 0)), w_spec, w_spec],
        out_specs=[o_spec, o_spec, o_spec],
        out_shape=[jax.ShapeDtypeStruct((T, F), BF16)] * 3,
        compiler_params=_params("arbitrary", "arbitrary"),
    )(xb, wg, wu)


def _mm_acc(pairs, after=()):
    T, K = pairs[0][0].shape
    D = pairs[0][1].shape[1]
    n = len(pairs)
    tk = _tile(K, 512, LANES)
    tm = _tile(T, 512, 16)

    def body(*refs):
        o_ref = refs[-1]

        @pl.when(pl.program_id(0) == 0)
        def _():
            o_ref[...] = jnp.zeros_like(o_ref)

        for r in range(0, T, tm):
            part = _dot_nn(refs[0][r:r + tm, :], refs[1][...])
            for a_ref, w_ref in zip(refs[2:2 * n:2], refs[3:2 * n:2]):
                part += _dot_nn(a_ref[r:r + tm, :], w_ref[...])
            o_ref[r:r + tm, :] += part

    in_specs, operands = [], []
    for a, w in pairs:
        in_specs += [pl.BlockSpec((T, tk), lambda k: (0, k)), pl.BlockSpec((tk, D), lambda k: (k, 0))]
        operands += [a, w]
    return pl.pallas_call(
        body, name="mm_acc", grid=(K // tk,),
        in_specs=in_specs + [ANY_SPEC] * len(after),
        out_specs=pl.BlockSpec((T, D), lambda k: (0, 0)),
        out_shape=jax.ShapeDtypeStruct((T, D), F32),
        compiler_params=_params("arbitrary"),
    )(*operands, *after)


def _mm_res_ln(a, w, res, g, b, scale, alpha, after=()):
    acc = _mm_acc([(a, w)], after)
    T, D = acc.shape
    tm = _tile(T, 256, 16)

    def body(acc_ref, res_ref, g_ref, b_ref, xo_ref, xb_ref, xh_ref, rstd_ref):
        r = alpha * res_ref[...] + scale * acc_ref[...]
        xc = r - _rowmean(r)
        rstd = lax.rsqrt(_rowmean(xc * xc) + LN_EPS)
        xh = xc * rstd
        xo = xh * g_ref[...] + b_ref[...]
        xo_ref[...] = xo
        xb_ref[...] = xo.astype(BF16)
        xh_ref[...] = xh
        rstd_ref[...] = rstd

    row = pl.BlockSpec((tm, D), lambda i: (i, 0))
    vec = pl.BlockSpec((1, D), lambda i: (0, 0))
    return pl.pallas_call(
        body, name="res_ln", grid=(T // tm,),
        in_specs=[row, row, vec, vec],
        out_specs=[row, row, row, pl.BlockSpec((tm, 1), lambda i: (i, 0))],
        out_shape=[jax.ShapeDtypeStruct((T, D), F32), jax.ShapeDtypeStruct((T, D), BF16),
                   jax.ShapeDtypeStruct((T, D), F32), jax.ShapeDtypeStruct((T, 1), F32)],
        compiler_params=_params("arbitrary"),
    )(acc, res, g, b)


def _mm_nt(a, w, after=()):
    M, K = a.shape
    N = w.shape[0]
    tm = _tile(M, ROWS_RESIDENT, 16)
    tn = _tile(N, MXU_COLS, LANES)

    def body(a_ref, w_ref, *rest):
        rest[-1][...] = _dot_nt(a_ref[...], w_ref[...])

    return pl.pallas_call(
        body, name="mm_nt", grid=(N // tn, M // tm),
        in_specs=[pl.BlockSpec((tm, K), lambda j, i: (i, 0)), pl.BlockSpec((tn, K), lambda j, i: (j, 0))]
        + [ANY_SPEC] * len(after),
        out_specs=pl.BlockSpec((tm, tn), lambda j, i: (i, j)),
        out_shape=jax.ShapeDtypeStruct((M, N), F32),
        compiler_params=_params("arbitrary", "arbitrary"),
    )(a, w, *after)


def _mm_tn(a, b, scale, out_dtype, half=None, addend=None):
    T, M = a.shape
    N = b.shape[1]
    rows = M if half is None else M // 2
    tm = _tile(rows, 512, LANES)
    tn = _tile(N, ROWS_RESIDENT, LANES)
    if tm < 512:
        tm = _tile(rows, 1408, LANES)
        tn = _tile(N, ROWS_RESIDENT // 2, LANES)
    first = 0 if half is None else half * (rows // tm)

    def body(a_ref, b_ref, *rest):
        out = scale * _dot_tn(a_ref[...], b_ref[...])
        if addend is not None:
            out = out + rest[0][...].astype(F32)
        rest[-1][...] = out.astype(out_dtype)

    o_spec = pl.BlockSpec((tm, tn), lambda i, j: (i, j))
    return pl.pallas_call(
        body, name="mm_tn", grid=(rows // tm, N // tn),
        in_specs=[pl.BlockSpec((T, tm), lambda i, j: (0, first + i)), pl.BlockSpec((T, tn), lambda i, j: (0, j))]
        + ([] if addend is None else [o_spec]),
        out_specs=o_spec,
        out_shape=jax.ShapeDtypeStruct((rows, N), out_dtype),
        compiler_params=_params("arbitrary", "arbitrary"),
    )(a, b, *([] if addend is None else [addend]))


def _ln_bwd(dy, xh, rstd, g):
    dxh = dy * g
    dr = rstd * (dxh - _rowmean(dxh) - xh * _rowmean(dxh * xh))
    return dr, _colsum(dy * xh), _colsum(dy)


def _loss_ln_bwd(y, target, xh, rstd, g):
    T, D = y.shape
    tm = _tile(T, 256, 16)

    def body(y_ref, t_ref, xh_ref, rstd_ref, g_ref, dr_ref, drb_ref, dg_ref, db_ref, loss_ref):
        i = pl.program_id(0)

        @pl.when(i == 0)
        def _():
            dg_ref[...] = jnp.zeros_like(dg_ref)
            db_ref[...] = jnp.zeros_like(db_ref)
            loss_ref[...] = jnp.zeros_like(loss_ref)

        err = y_ref[...] - t_ref[...]
        loss_ref[...] += (0.5 / D) * _colsum(jnp.sum(err * err, axis=1, keepdims=True))
        dr, dg, db = _ln_bwd(err * (1.0 / D), xh_ref[...], rstd_ref[...], g_ref[...])
        dr_ref[...] = dr
        drb_ref[...] = dr.astype(BF16)
        dg_ref[...] += dg
        db_ref[...] += db

    row = pl.BlockSpec((tm, D), lambda i: (i, 0))
    vec = pl.BlockSpec((1, D), lambda i: (0, 0))
    return pl.pallas_call(
        body, name="loss_ln_bwd", grid=(T // tm,),
        in_specs=[row, row, row, pl.BlockSpec((tm, 1), lambda i: (i, 0)), vec],
        out_specs=[row, row, vec, vec, pl.BlockSpec((1, 1), lambda i: (0, 0))],
        out_shape=[jax.ShapeDtypeStruct((T, D), F32), jax.ShapeDtypeStruct((T, D), BF16),
                   jax.ShapeDtypeStruct((1, D), F32), jax.ShapeDtypeStruct((1, D), F32),
                   jax.ShapeDtypeStruct((1, 1), F32)],
        compiler_params=_params("arbitrary"),
    )(y, target, xh, rstd, g)


def _ffn_down_bwd(drb, wd, cg, cu, after=()):
    T, D = drb.shape
    F = wd.shape[0]
    tm = _tile(T, ROWS_RESIDENT, 16)
    tn = _tile(F, MXU_COLS, LANES)

    def body(dr_ref, wd_ref, cg_ref, cu_ref, *rest):
        dg_ref, du_ref = rest[len(after):]
        dh = 0.5 * _dot_nt(dr_ref[...], wd_ref[...])
        du_ref[...] = (dh * cu_ref[...].astype(F32)).astype(BF16)
        dg_ref[...] = (dh * cg_ref[...].astype(F32)).astype(BF16)

    t_spec = pl.BlockSpec((tm, tn), lambda j, i: (i, j))
    return pl.pallas_call(
        body, name="ffn_down_bwd", grid=(F // tn, T // tm),
        in_specs=[pl.BlockSpec((tm, D), lambda j, i: (i, 0)), pl.BlockSpec((tn, D), lambda j, i: (j, 0)),
                  t_spec, t_spec] + [ANY_SPEC] * len(after),
        out_specs=[t_spec, t_spec],
        out_shape=[jax.ShapeDtypeStruct((T, F), BF16), jax.ShapeDtypeStruct((T, F), BF16)],
        compiler_params=_params("arbitrary", "arbitrary"),
    )(drb, wd, cg, cu, *after)


def _mm_dx(pairs, res, alpha, ln=None, after=()):
    acc = _mm_acc(pairs, after)
    T, D = acc.shape
    tm = _tile(T, 256, 16)
    with_ln = ln is not None

    def body(acc_ref, res_ref, *refs):
        dx = alpha * res_ref[...] + acc_ref[...]
        if with_ln:
            xh_ref, rstd_ref, g_ref, dr_ref, drb_ref, dg_ref, db_ref = refs

            @pl.when(pl.program_id(0) == 0)
            def _():
                dg_ref[...] = jnp.zeros_like(dg_ref)
                db_ref[...] = jnp.zeros_like(db_ref)

            dr, dg, db = _ln_bwd(dx, xh_ref[...], rstd_ref[...], g_ref[...])
            dr_ref[...] = dr
            drb_ref[...] = dr.astype(BF16)
            dg_ref[...] += dg
            db_ref[...] += db
        else:
            refs[0][...] = dx

    row = pl.BlockSpec((tm, D), lambda i: (i, 0))
    vec = pl.BlockSpec((1, D), lambda i: (0, 0))
    in_specs, operands = [row, row], [acc, res]
    if with_ln:
        in_specs += [row, pl.BlockSpec((tm, 1), lambda i: (i, 0)), vec]
        operands += list(ln)
        out_specs = [row, row, vec, vec]
        out_shape = [jax.ShapeDtypeStruct((T, D), F32), jax.ShapeDtypeStruct((T, D), BF16),
                     jax.ShapeDtypeStruct((1, D), F32), jax.ShapeDtypeStruct((1, D), F32)]
    else:
        out_specs = [row]
        out_shape = [jax.ShapeDtypeStruct((T, D), F32)]
    return pl.pallas_call(
        body, name="dx_ln_bwd" if with_ln else "dx_res", grid=(T // tm,),
        in_specs=in_specs, out_specs=out_specs, out_shape=out_shape,
        compiler_params=_params("arbitrary"),
    )(*operands)


def _s5_discretize(lre, lim, ldt, br, bi):
    dt = jnp.exp(ldt)
    mag = jnp.exp(lre * dt)
    ang = lim * dt
    ar = mag * jnp.cos(ang)
    ai = mag * jnp.sin(ang)
    den = lre * lre + lim * lim
    nr = ar - 1.0
    qr = (nr * lre + ai * lim) / den
    qi = (ai * lre - nr * lim) / den
    bbr = qr[None] * br - qi[None] * bi
    bbi = qr[None] * bi + qi[None] * br
    return ar, ai, bbr, bbi


def _s5_params_fwd(lre, lim, ldt, br, bi):
    def body(lre_ref, lim_ref, ldt_ref, br_ref, bi_ref, ar_ref, ai_ref, bbr_ref, bbi_ref):
        ar, ai, bbr, bbi = _s5_discretize(lre_ref[...], lim_ref[...], ldt_ref[...], br_ref[...], bi_ref[...])
        ar_ref[...] = ar
        ai_ref[...] = ai
        bbr_ref[...] = bbr
        bbi_ref[...] = bbi

    sds = jax.ShapeDtypeStruct
    return pl.pallas_call(
        body, name="s5_params_fwd",
        out_shape=[sds(lre.shape, F32), sds(lre.shape, F32), sds(br.shape, F32), sds(br.shape, F32)],
        compiler_params=pltpu.CompilerParams(vmem_limit_bytes=VMEM_LIMIT),
    )(lre, lim, ldt, br, bi)


def _s5_params_bwd(lre, lim, ldt, br, bi, dar, dai, dbbr, dbbi):
    def body(lre_ref, lim_ref, ldt_ref, br_ref, bi_ref, dar_ref, dai_ref, dbbr_ref, dbbi_ref,
             o_lre, o_lim, o_ldt, o_br, o_bi):
        _, vjp = jax.vjp(_s5_discretize, lre_ref[...], lim_ref[...], ldt_ref[...], br_ref[...], bi_ref[...])
        g = vjp((dar_ref[...], dai_ref[...], dbbr_ref[...], dbbi_ref[...]))
        o_lre[...] = g[0]
        o_lim[...] = g[1]
        o_ldt[...] = g[2]
        o_br[...] = g[3]
        o_bi[...] = g[4]

    sds = jax.ShapeDtypeStruct
    return pl.pallas_call(
        body, name="s5_params_bwd",
        out_shape=[sds(lre.shape, F32), sds(lre.shape, F32), sds(ldt.shape, F32), sds(br.shape, F32),
                   sds(br.shape, F32)],
        compiler_params=pltpu.CompilerParams(vmem_limit_bytes=VMEM_LIMIT),
    )(lre, lim, ldt, br, bi, dar, dai, dbbr, dbbi)


def _cmul(a_re, a_im, b_re, b_im):
    return a_re * b_re - a_im * b_im, a_re * b_im + a_im * b_re


def _s5_tile(i):
    if isinstance(i, int):
        return pl.ds(i * SUBLANES, SUBLANES)
    return pl.ds(pl.multiple_of(i * SUBLANES, SUBLANES), SUBLANES)


def _s5_powers(a_re, a_im, pwr_ref, pwi_ref, seg):
    def step(i, carry):
        p_re, p_im = carry
        pwr_ref[pl.ds(i, 1), :] = p_re
        pwi_ref[pl.ds(i, 1), :] = p_im
        return _cmul(a_re, a_im, p_re, p_im)

    lax.fori_loop(0, seg, step, (a_re, a_im))


def _s5_fwd(proj, bdr, bdi, cdr, cdi, ar, ai, dvec):
    T = proj.shape[0]
    GB, UB, SB = bdr.shape
    tc = _tile(T, 256, SUBLANES * SUBLANES)
    seg = tc // SUBLANES

    def body(u_ref, bdr_ref, bdi_ref, cdr_ref, cdi_ref, ar_ref, ai_ref, d_ref, y_ref, sr_ref, si_ref,
             cr_ref, ci_ref, pwr_ref, pwi_ref, str_ref, sti_ref, up_ref, yp_ref):
        a_re = ar_ref[...]
        a_im = ai_ref[...]

        @pl.when(pl.program_id(1) == 0)
        def _():
            cr_ref[...] = jnp.zeros_like(cr_ref)
            ci_ref[...] = jnp.zeros_like(ci_ref)
            _s5_powers(a_re, a_im, pwr_ref, pwi_ref, seg)

        for i in range(seg):
            up_ref[_s5_tile(i), :] = u_ref[pl.ds(i, SUBLANES, stride=seg), :]
        u = up_ref[...]
        ub = u.astype(BF16)
        sr_ref[...] = _dot_nn(ub, bdr_ref[...])
        si_ref[...] = _dot_nn(ub, bdi_ref[...])

        def local(i, carry):
            p_re, p_im = carry
            rows = _s5_tile(i)
            n_re, n_im = _cmul(a_re, a_im, p_re, p_im)
            n_re, n_im = n_re + sr_ref[rows, :], n_im + si_ref[rows, :]
            sr_ref[rows, :] = n_re
            si_ref[rows, :] = n_im
            return n_re, n_im

        e_re, e_im = lax.fori_loop(1, seg, local, (sr_ref[_s5_tile(0), :], si_ref[_s5_tile(0), :]))
        s_re, s_im = cr_ref[...], ci_ref[...]
        top_re, top_im = pwr_ref[seg - 1:seg, :], pwi_ref[seg - 1:seg, :]
        for j in range(SUBLANES):
            str_ref[j:j + 1, :] = s_re
            sti_ref[j:j + 1, :] = s_im
            n_re, n_im = _cmul(top_re, top_im, s_re, s_im)
            s_re, s_im = n_re + e_re[j:j + 1, :], n_im + e_im[j:j + 1, :]
        cr_ref[...] = s_re
        ci_ref[...] = s_im
        b_re, b_im = str_ref[...], sti_ref[...]

        def fix(i, carry):
            rows = _s5_tile(i)
            f_re, f_im = _cmul(pwr_ref[pl.ds(i, 1), :], pwi_ref[pl.ds(i, 1), :], b_re, b_im)
            sr_ref[rows, :] += f_re
            si_ref[rows, :] += f_im
            return carry

        lax.fori_loop(0, seg, fix, 0)
        yp_ref[...] = (_dot_nn(sr_ref[...].astype(BF16), cdr_ref[...])
                       - _dot_nn(si_ref[...].astype(BF16), cdi_ref[...]) + d_ref[...] * u)
        for i in range(seg):
            y_ref[pl.ds(i, SUBLANES, stride=seg), :] = yp_ref[_s5_tile(i), :]

    return pl.pallas_call(
        body, name="s5_fwd", grid=(GB, T // tc),
        in_specs=[pl.BlockSpec((tc, UB), lambda j, t: (t, j)),
                  pl.BlockSpec((None, UB, SB), lambda j, t: (j, 0, 0)),
                  pl.BlockSpec((None, UB, SB), lambda j, t: (j, 0, 0)),
                  pl.BlockSpec((None, SB, UB), lambda j, t: (j, 0, 0)),
                  pl.BlockSpec((None, SB, UB), lambda j, t: (j, 0, 0)),
                  pl.BlockSpec((1, SB), lambda j, t: (0, j)),
                  pl.BlockSpec((1, SB), lambda j, t: (0, j)),
                  pl.BlockSpec((1, UB), lambda j, t: (0, j))],
        out_specs=[pl.BlockSpec((tc, UB), lambda j, t: (t, j)),
                   pl.BlockSpec((tc, SB), lambda j, t: (t, j)),
                   pl.BlockSpec((tc, SB), lambda j, t: (t, j))],
        out_shape=[jax.ShapeDtypeStruct((T, GB * UB), F32), jax.ShapeDtypeStruct((T, GB * SB), F32),
                   jax.ShapeDtypeStruct((T, GB * SB), F32)],
        scratch_shapes=[pltpu.VMEM((1, SB), F32), pltpu.VMEM((1, SB), F32),
                        pltpu.VMEM((seg, SB), F32), pltpu.VMEM((seg, SB), F32),
                        pltpu.VMEM((SUBLANES, SB), F32), pltpu.VMEM((SUBLANES, SB), F32),
                        pltpu.VMEM((tc, UB), F32), pltpu.VMEM((tc, UB), F32)],
        compiler_params=_params("arbitrary", "arbitrary"),
    )(proj, bdr, bdi, cdr, cdi, ar, ai, dvec)


def _s5_bwd(dy, proj, sr, si, bdr, bdi, cdr, cdi, ar, ai, dvec):
    T = dy.shape[0]
    GB, UB, SB = bdr.shape
    tc = _tile(T, 256, SUBLANES * SUBLANES)
    seg = tc // SUBLANES
    nt = T // tc
    halo_blocks = tc // SUBLANES

    def body(dy_ref, u_ref, sr_ref, si_ref, hr_ref, hi_ref, bdr_ref, bdi_ref, cdr_ref, cdi_ref, ar_ref, ai_ref,
             d_ref, du_ref, dbdr_ref, dbdi_ref, dcdr_ref, dcdi_ref, dar_ref, dai_ref, dd_ref,
             gr_ref, gi_ref, pwr_ref, pwi_ref, rtr_ref, rti_ref, cr_ref, ci_ref, dyp_ref, up_ref, dup_ref):
        step_no = pl.program_id(1)
        first_chunk = step_no == nt - 1
        a_re = ar_ref[...]
        a_im = ai_ref[...]

        @pl.when(step_no == 0)
        def _():
            for ref in (cr_ref, ci_ref, dbdr_ref, dbdi_ref, dcdr_ref, dcdi_ref, dar_ref, dai_ref, dd_ref):
                ref[...] = jnp.zeros_like(ref)
            _s5_powers(a_re, a_im, pwr_ref, pwi_ref, seg)

        for i in range(seg):
            dyp_ref[_s5_tile(i), :] = dy_ref[pl.ds(i, SUBLANES, stride=seg), :]
            up_ref[_s5_tile(i), :] = u_ref[pl.ds(i, SUBLANES, stride=seg), :]
        dy = dyp_ref[...]
        dyb = dy.astype(BF16)
        u = up_ref[...]
        gr_ref[...] = _dot_nt(dyb, cdr_ref[...])
        gi_ref[...] = -_dot_nt(dyb, cdi_ref[...])
        dcdr_ref[...] += _dot_tn(sr_ref[...].astype(BF16), dyb)
        dcdi_ref[...] -= _dot_tn(si_ref[...].astype(BF16), dyb)

        def local(n, carry):
            c_re, c_im = carry
            rows = _s5_tile(seg - 1 - n)
            g_re = gr_ref[rows, :] + a_re * c_re + a_im * c_im
            g_im = gi_ref[rows, :] + a_re * c_im - a_im * c_re
            gr_ref[rows, :] = g_re
            gi_ref[rows, :] = g_im
            return g_re, g_im

        last = _s5_tile(seg - 1)
        m_re, m_im = lax.fori_loop(1, seg, local, (gr_ref[last, :], gi_ref[last, :]))
        top_re, top_im = pwr_ref[seg - 1:seg, :], -pwi_ref[seg - 1:seg, :]
        r_re, r_im = cr_ref[...], ci_ref[...]
        for j in reversed(range(SUBLANES)):
            rtr_ref[j:j + 1, :] = r_re
            rti_ref[j:j + 1, :] = r_im
            n_re, n_im = _cmul(top_re, top_im, r_re, r_im)
            r_re, r_im = n_re + m_re[j:j + 1, :], n_im + m_im[j:j + 1, :]
        cr_ref[...] = r_re
        ci_ref[...] = r_im
        f_re, f_im = rtr_ref[...], rti_ref[...]
        keep = jnp.where(first_chunk, 0.0, 1.0)
        sub = lax.broadcasted_iota(jnp.int32, (SUBLANES, SB), 0)
        before_re = jnp.where(sub == 0, hr_ref[SUBLANES - 1:SUBLANES, :] * keep, pltpu.roll(sr_ref[last, :], 1, 0))
        before_im = jnp.where(sub == 0, hi_ref[SUBLANES - 1:SUBLANES, :] * keep, pltpu.roll(si_ref[last, :], 1, 0))

        def fix(i, p_re, p_im, acc):
            rows = _s5_tile(i)
            k = seg - 1 - i
            c_re, c_im = _cmul(pwr_ref[pl.ds(k, 1), :], -pwi_ref[pl.ds(k, 1), :], f_re, f_im)
            g_re = gr_ref[rows, :] + c_re
            g_im = gi_ref[rows, :] + c_im
            gr_ref[rows, :] = g_re
            gi_ref[rows, :] = g_im
            return acc[0] + p_re * g_re + p_im * g_im, acc[1] + p_re * g_im - p_im * g_re

        zero = jnp.zeros((SUBLANES, SB), F32)
        acc = fix(0, before_re, before_im, (zero, zero))
        acc = lax.fori_loop(
            1, seg, lambda i, acc: fix(i, sr_ref[_s5_tile(i - 1), :], si_ref[_s5_tile(i - 1), :], acc), acc)
        dar_ref[...] += _colsum(acc[0])
        dai_ref[...] += _colsum(acc[1])
        gsr = gr_ref[...].astype(BF16)
        gsi = gi_ref[...].astype(BF16)
        ub = u.astype(BF16)
        dbdr_ref[...] += _dot_tn(ub, gsr)
        dbdi_ref[...] += _dot_tn(ub, gsi)
        dup_ref[...] = _dot_nt(gsr, bdr_ref[...]) + _dot_nt(gsi, bdi_ref[...]) + d_ref[...] * dy
        for i in range(seg):
            du_ref[pl.ds(i, SUBLANES, stride=seg), :] = dup_ref[_s5_tile(i), :]
        dd_ref[...] += _colsum(dy * u)

    def rev(t):
        return nt - 1 - t

    def halo(j, t):
        return (jnp.maximum(rev(t) * halo_blocks - 1, 0), j)

    ublk = pl.BlockSpec((tc, UB), lambda j, t: (rev(t), j))
    sblk = pl.BlockSpec((tc, SB), lambda j, t: (rev(t), j))
    bd_spec = pl.BlockSpec((None, UB, SB), lambda j, t: (j, 0, 0))
    cd_spec = pl.BlockSpec((None, SB, UB), lambda j, t: (j, 0, 0))
    svec = pl.BlockSpec((1, SB), lambda j, t: (0, j))
    uvec = pl.BlockSpec((1, UB), lambda j, t: (0, j))
    sds = jax.ShapeDtypeStruct
    return pl.pallas_call(
        body, name="s5_bwd", grid=(GB, nt),
        in_specs=[ublk, ublk, sblk, sblk, pl.BlockSpec((SUBLANES, SB), halo), pl.BlockSpec((SUBLANES, SB), halo),
                  bd_spec, bd_spec, cd_spec, cd_spec, svec, svec, uvec],
        out_specs=[ublk, bd_spec, bd_spec, cd_spec, cd_spec, svec, svec, uvec],
        out_shape=[sds((T, GB * UB), F32), sds((GB, UB, SB), F32), sds((GB, UB, SB), F32),
                   sds((GB, SB, UB), F32), sds((GB, SB, UB), F32), sds((1, GB * SB), F32),
                   sds((1, GB * SB), F32), sds((1, GB * UB), F32)],
        scratch_shapes=[pltpu.VMEM((tc, SB), F32), pltpu.VMEM((tc, SB), F32),
                        pltpu.VMEM((seg, SB), F32), pltpu.VMEM((seg, SB), F32),
                        pltpu.VMEM((SUBLANES, SB), F32), pltpu.VMEM((SUBLANES, SB), F32),
                        pltpu.VMEM((1, SB), F32), pltpu.VMEM((1, SB), F32),
                        pltpu.VMEM((tc, UB), F32), pltpu.VMEM((tc, UB), F32), pltpu.VMEM((tc, UB), F32)],
        compiler_params=_params("arbitrary", "arbitrary"),
    )(dy, proj, sr, si, sr, si, bdr, bdi, cdr, cdi, ar, ai, dvec)


def _shift_down(v, k):
    rows = lax.broadcasted_iota(jnp.int32, v.shape, 0)
    return jnp.where(rows >= k, pltpu.roll(v, k, 0), 0.0)


def _shift_up(v, k):
    n = v.shape[0]
    rows = lax.broadcasted_iota(jnp.int32, v.shape, 0)
    return jnp.where(rows < n - k, pltpu.roll(v, n - k, 0), 0.0)


def _conv_specs(T, cb, n_s5_blocks, n_conv_blocks):
    gb = pl.BlockSpec((T, cb), lambda j: (0, n_s5_blocks + j))
    gc = pl.BlockSpec((T, cb), lambda j: (0, n_s5_blocks + n_conv_blocks + j))
    hh = pl.BlockSpec((T, cb), lambda j: (0, n_s5_blocks + 2 * n_conv_blocks + j))
    return gb, gc, hh


def _conv_fwd(proj, cw, cbias, d_s5, d_conv):
    T = proj.shape[0]
    cb = _tile(d_conv, 256, LANES)

    def body(gb_ref, gc_ref, hh_ref, w_ref, b_ref, z_ref):
        v = gc_ref[...] * hh_ref[...]
        w = w_ref[...]
        cv = b_ref[...] + w[0:1, :] * _shift_down(v, 2) + w[1:2, :] * _shift_down(v, 1) + w[2:3, :] * v
        z_ref[...] = gb_ref[...] * cv

    gb, gc, hh = _conv_specs(T, cb, d_s5 // cb, d_conv // cb)
    col = pl.BlockSpec((T, cb), lambda j: (0, j))
    return pl.pallas_call(
        body, name="conv_fwd", grid=(d_conv // cb,),
        in_specs=[gb, gc, hh, pl.BlockSpec((CONV_W, cb), lambda j: (0, j)), pl.BlockSpec((1, cb), lambda j: (0, j))],
        out_specs=col, out_shape=jax.ShapeDtypeStruct((T, d_conv), F32),
        compiler_params=_params("arbitrary"),
    )(proj, proj, proj, cw, cbias)


def _conv_bwd(dz, proj, cw, cbias, d_s5, d_conv):
    T = proj.shape[0]
    cb = _tile(d_conv, 256, LANES)

    def body(dz_ref, gb_ref, gc_ref, hh_ref, w_ref, b_ref, dgb_ref, dgc_ref, dhh_ref, dw_ref, db_ref):
        gc = gc_ref[...]
        hh = hh_ref[...]
        dz = dz_ref[...]
        w = w_ref[...]
        v = gc * hh
        v1 = _shift_down(v, 1)
        v2 = _shift_down(v, 2)
        cv = b_ref[...] + w[0:1, :] * v2 + w[1:2, :] * v1 + w[2:3, :] * v
        dgb_ref[...] = (dz * cv).astype(BF16)
        dcv = dz * gb_ref[...]
        dv = w[2:3, :] * dcv + w[1:2, :] * _shift_up(dcv, 1) + w[0:1, :] * _shift_up(dcv, 2)
        dgc_ref[...] = (dv * hh).astype(BF16)
        dhh_ref[...] = (dv * gc).astype(BF16)
        dw_ref[0:1, :] = _colsum(dcv * v2)
        dw_ref[1:2, :] = _colsum(dcv * v1)
        dw_ref[2:3, :] = _colsum(dcv * v)
        db_ref[...] = _colsum(dcv)

    gb, gc, hh = _conv_specs(T, cb, d_s5 // cb, d_conv // cb)
    col = pl.BlockSpec((T, cb), lambda j: (0, j))
    wspec = pl.BlockSpec((CONV_W, cb), lambda j: (0, j))
    bspec = pl.BlockSpec((1, cb), lambda j: (0, j))
    sds = jax.ShapeDtypeStruct
    return pl.pallas_call(
        body, name="conv_bwd", grid=(d_conv // cb,),
        in_specs=[col, gb, gc, hh, wspec, bspec],
        out_specs=[col, col, col, wspec, bspec],
        out_shape=[sds((T, d_conv), BF16), sds((T, d_conv), BF16), sds((T, d_conv), BF16),
                   sds((CONV_W, d_conv), F32), sds((1, d_conv), F32)],
        compiler_params=_params("arbitrary"),
    )(dz, proj, proj, proj, cw, cbias)


def _rms(v, g):
    rstd = lax.rsqrt(_rowmean(v * v) + RMS_EPS)
    return v * rstd * g, rstd


def _rms_bwd(dyn, v, rstd, g):
    w = dyn * g
    return rstd * w - v * (rstd * rstd * rstd) * _rowmean(w * v), _colsum(dyn * v * rstd)


def _mix_post(y, z, wglu, g_s5, g_conv):
    T, C = y.shape
    tm = _tile(T, 256, 16)

    def body(y_ref, z_ref, w_ref, gs_ref, gc_ref, m_ref, gl_ref):
        ge = jax.nn.gelu(y_ref[...])
        gl = _dot_nn(ge.astype(BF16), w_ref[...])
        gl_ref[...] = gl
        yn, _ = _rms(ge * jax.nn.sigmoid(gl), gs_ref[...])
        zn, _ = _rms(z_ref[...], gc_ref[...])
        m_ref[:, 0:C] = yn.astype(BF16)
        m_ref[:, C:2 * C] = zn.astype(BF16)

    row = pl.BlockSpec((tm, C), lambda i: (i, 0))
    vec = pl.BlockSpec((1, C), lambda i: (0, 0))
    return pl.pallas_call(
        body, name="mix_post", grid=(T // tm,),
        in_specs=[row, row, pl.BlockSpec((C, C), lambda i: (0, 0)), vec, vec],
        out_specs=[pl.BlockSpec((tm, 2 * C), lambda i: (i, 0)), row],
        out_shape=[jax.ShapeDtypeStruct((T, 2 * C), BF16), jax.ShapeDtypeStruct((T, C), F32)],
        compiler_params=_params("arbitrary"),
    )(y, z, wglu, g_s5, g_conv)


def _mix_post_bwd(dm, y, gl, z, wglu, g_s5, g_conv):
    T, C = y.shape
    tm = _tile(T, 256, 16)

    def body(dm_ref, y_ref, gl_ref, z_ref, w_ref, gs_ref, gc_ref, dy_ref, dz_ref, dw_ref, dgs_ref, dgc_ref):
        @pl.when(pl.program_id(0) == 0)
        def _():
            dw_ref[...] = jnp.zeros_like(dw_ref)
            dgs_ref[...] = jnp.zeros_like(dgs_ref)
            dgc_ref[...] = jnp.zeros_like(dgc_ref)

        yv = y_ref[...]
        ge, gelu_vjp = jax.vjp(jax.nn.gelu, yv)
        gl = gl_ref[...]
        sg = jax.nn.sigmoid(gl)
        y2 = ge * sg
        _, rstd_y = _rms(y2, gs_ref[...])
        dy2, dgs = _rms_bwd(dm_ref[:, 0:C], y2, rstd_y, gs_ref[...])
        dgs_ref[...] += dgs
        dgl = (dy2 * ge * sg * (1.0 - sg)).astype(BF16)
        dge = dy2 * sg + _dot_nt(dgl, w_ref[...])
        dw_ref[...] += _dot_tn(ge.astype(BF16), dgl)
        dy_ref[...] = gelu_vjp(dge)[0]
        zv = z_ref[...]
        _, rstd_z = _rms(zv, gc_ref[...])
        dz, dgc = _rms_bwd(dm_ref[:, C:2 * C], zv, rstd_z, gc_ref[...])
        dz_ref[...] = dz
        dgc_ref[...] += dgc

    row = pl.BlockSpec((tm, C), lambda i: (i, 0))
    vec = pl.BlockSpec((1, C), lambda i: (0, 0))
    full = pl.BlockSpec((C, C), lambda i: (0, 0))
    sds = jax.ShapeDtypeStruct
    return pl.pallas_call(
        body, name="mix_post_bwd", grid=(T // tm,),
        in_specs=[pl.BlockSpec((tm, 2 * C), lambda i: (i, 0)), row, row, row, full, vec, vec],
        out_specs=[row, row, full, vec, vec],
        out_shape=[sds((T, C), F32), sds((T, C), F32), sds((C, C), F32), sds((1, C), F32), sds((1, C), F32)],
        compiler_params=_params("arbitrary"),
    )(dm, y, gl, z, wglu, g_s5, g_conv)


def _adamw(w, g, m, v):
    m = ADAM_B1 * m + (1.0 - ADAM_B1) * g
    v = ADAM_B2 * v + (1.0 - ADAM_B2) * (g * g)
    m_hat = m / (1.0 - ADAM_B1 ** ADAM_STEP)
    v_hat = v / (1.0 - ADAM_B2 ** ADAM_STEP)
    return -ADAM_LR * (m_hat / (jnp.sqrt(v_hat) + ADAM_EPS) + ADAM_WD * w), m, v


def _sum_parts(p_ref):
    total = p_ref[0].astype(F32)
    for d in range(1, p_ref.shape[0]):
        total = total + p_ref[d].astype(F32)
    return total


def _row_tile(R, C, n_streams):
    budget = VMEM_LIMIT // 3 // (n_streams * C * 4)
    return _tile(R, max(BF16_ROWS, budget), BF16_ROWS)


def _reduce_parts(parts):
    L, P, R, C = parts.shape
    tr = _row_tile(R, C, P + 1)

    def body(p_ref, o_ref):
        o_ref[...] = _sum_parts(p_ref)

    return pl.pallas_call(
        body, name="reduce_parts", grid=(L, R // tr),
        in_specs=[pl.BlockSpec((None, P, tr, C), lambda l, i: (l, 0, i, 0))],
        out_specs=pl.BlockSpec((None, tr, C), lambda l, i: (l, i, 0)),
        out_shape=jax.ShapeDtypeStruct((L, R, C), F32),
        compiler_params=_params("arbitrary", "arbitrary"),
    )(parts)


def _adamw_update(w, m, v, grad=None, parts=None):
    L, R, C = w.shape
    from_parts = parts is not None
    P = parts.shape[1] if from_parts else 1
    tr = _row_tile(R, C, P + 7)

    def body(g_in_ref, w_ref, m_ref, v_ref, g_ref, d_ref, nm_ref, nv_ref):
        g = _sum_parts(g_in_ref) if from_parts else g_in_ref[...]
        delta, nm, nv = _adamw(w_ref[...], g, m_ref[...], v_ref[...])
        g_ref[...] = g
        d_ref[...] = delta
        nm_ref[...] = nm
        nv_ref[...] = nv

    blk = pl.BlockSpec((None, tr, C), lambda l, i: (l, i, 0))
    g_spec = pl.BlockSpec((None, P, tr, C), lambda l, i: (l, 0, i, 0)) if from_parts else blk
    out = jax.ShapeDtypeStruct((L, R, C), F32)
    return pl.pallas_call(
        body, name="adamw_parts" if from_parts else "adamw", grid=(L, R // tr),
        in_specs=[g_spec, blk, blk, blk], out_specs=[blk, blk, blk, blk], out_shape=[out, out, out, out],
        compiler_params=_params("arbitrary", "arbitrary"),
    )(parts if from_parts else grad, w, m, v)


def _pair_sum(full, stage):
    _, R, C = full.shape
    tr = _row_tile(R, C, 4)

    def body(f_ref, s_ref, o_ref):
        mine = f_ref[lax.axis_index("c")]
        o_ref[...] = (mine.astype(F32) + s_ref[...].astype(F32)).astype(BF16)

    blk = pl.BlockSpec((None, tr, C), lambda q, i: (q, i, 0))
    return pl.pallas_call(
        body, name="pair_sum", grid=(N_CHIPS, R // tr),
        in_specs=[pl.BlockSpec((None, 2, tr, C), lambda q, i: (q, 0, i, 0)), blk],
        out_specs=blk, out_shape=jax.ShapeDtypeStruct((N_CHIPS, R, C), BF16),
        compiler_params=_params("arbitrary", "arbitrary"),
    )(full.reshape(N_CHIPS, 2, R, C), stage)


def _me():
    x, y, c = (lax.axis_index(a) for a in AXES)
    return x, y, c, 4 * x + 2 * y + c


def _peer(rel):
    x, y, c, _ = _me()
    px = 1 - x if rel & 4 else x
    py = 1 - y if rel & 2 else y
    pc = 1 - c if rel & 1 else c
    return (px, py, pc), 4 * px + 2 * py + pc


DATAFLOW = pltpu.SideEffectType.DATAFLOW_SIDE_EFFECTING


def _remote_copy(src, dst, sems):
    return functools.partial(pltpu.make_async_remote_copy, src_ref=src, dst_ref=dst, **sems)


ALL_PEERS = tuple(range(1, N_DEV))
SIBLING = 1
OTHER_CHIPS = (2, 4, 6)
SIBLING_AND_OTHER_CHIPS = (SIBLING,) + OTHER_CHIPS


def _slot(dev, blk, same_core, own_core_last):
    if not own_core_last:
        return blk
    return (N_CHIPS if same_core else 0) + 2 * dev[0] + dev[1]


def _gather_copies(n, rels=ALL_PEERS, own_core_last=False):
    def copies(srcs, lands, send_sems, recv_sems, local_sems):
        x, y, c, me = _me()
        local, remote = [], []
        for k in range(n):
            local.append(functools.partial(pltpu.make_async_copy, srcs[k],
                                           lands[k].at[_slot((x, y, c), me, True, own_core_last)], local_sems.at[k]))
            for rel in rels:
                dev, blk = _peer(rel)
                same_core = not rel & SIBLING
                sems = dict(send_sem=send_sems.at[_sem_index(k, rel)], recv_sem=recv_sems.at[_sem_index(k, rel)],
                            device_id=dev, device_id_type=MESH)
                remote.append((_remote_copy(srcs[k], lands[k].at[_slot((x, y, c), me, same_core, own_core_last)], sems),
                               _remote_copy(srcs[k], lands[k].at[_slot(dev, blk, same_core, own_core_last)], sems)))
        return local, remote

    copies.n_arrays = n
    return copies


def _forward_copies(n, own_core_last=False):
    def copies(srcs, lands, send_sems, recv_sems, local_sems):
        sibling = _peer(SIBLING)[0]
        remote = []
        for k in range(n):
            for rel in OTHER_CHIPS:
                dev, blk = _peer(rel)
                have = _slot(dev, blk, True, own_core_last)
                there = _slot(dev, blk, False, own_core_last)
                other = _peer(rel | SIBLING)
                comes = _slot(other[0], other[1], False, own_core_last)
                sems = dict(send_sem=send_sems.at[_sem_index(k, rel)], recv_sem=recv_sems.at[_sem_index(k, rel)],
                            device_id=sibling, device_id_type=MESH)
                remote.append((_remote_copy(lands[k].at[have], lands[k].at[there], sems),
                               _remote_copy(lands[k].at[have], lands[k].at[comes], sems)))
        return [], remote

    copies.n_arrays = n
    return copies


def _scatter_copies(n, layer):
    def copies(srcs, lands, send_sems, recv_sems, local_sems):
        me = _me()[3]
        local, remote = [], []
        for k in range(n):
            local.append(functools.partial(pltpu.make_async_copy, srcs[k].at[me], lands[k].at[layer, me],
                                           local_sems.at[k]))
            for rel in range(1, N_DEV):
                dev, blk = _peer(rel)
                sems = dict(send_sem=send_sems.at[_sem_index(k, rel)], recv_sem=recv_sems.at[_sem_index(k, rel)],
                            device_id=dev, device_id_type=MESH)
                remote.append((_remote_copy(srcs[k].at[blk], lands[k].at[layer, me], sems),
                               _remote_copy(srcs[k].at[blk], lands[k].at[layer, blk], sems)))
        return local, remote

    copies.n_arrays = n
    return copies


def _swap_copies(n):
    def copies(srcs, lands, send_sems, recv_sems, local_sems):
        sibling = _peer(SIBLING)[0]
        remote = []
        for k in range(n):
            sems = dict(send_sem=send_sems.at[_sem_index(k, SIBLING)], recv_sem=recv_sems.at[_sem_index(k, SIBLING)],
                        device_id=sibling, device_id_type=MESH)
            remote.append((_remote_copy(srcs[k], lands[k], sems), _remote_copy(srcs[k], lands[k], sems)))
        return [], remote

    copies.n_arrays = n
    return copies


def _pair_copies(n):
    def copies(srcs, lands, send_sems, recv_sems, local_sems):
        c = _me()[2]
        sibling = _peer(SIBLING)[0]
        remote = []
        for k in range(n):
            for chip in range(N_CHIPS):
                sems = dict(send_sem=send_sems.at[_sem_index(k, chip + 1)], recv_sem=recv_sems.at[_sem_index(k, chip + 1)],
                            device_id=sibling, device_id_type=MESH)
                block = srcs[k].at[2 * chip + 1 - c]
                remote.append((_remote_copy(block, lands[k].at[chip], sems), _remote_copy(block, lands[k].at[chip], sems)))
        return [], remote

    copies.n_arrays = n
    return copies


def _chip_scatter_copies(n, layer):
    def copies(srcs, lands, send_sems, recv_sems, local_sems):
        x, y, _, _ = _me()
        my_chip = 2 * x + y
        local, remote = [], []
        for k in range(n):
            local.append(functools.partial(pltpu.make_async_copy, srcs[k].at[my_chip], lands[k].at[layer, my_chip],
                                           local_sems.at[k]))
            for rel in OTHER_CHIPS:
                dev = _peer(rel)[0]
                chip = 2 * dev[0] + dev[1]
                sems = dict(send_sem=send_sems.at[_sem_index(k, rel)], recv_sem=recv_sems.at[_sem_index(k, rel)],
                            device_id=dev, device_id_type=MESH)
                remote.append((_remote_copy(srcs[k].at[chip], lands[k].at[layer, my_chip], sems),
                               _remote_copy(srcs[k].at[chip], lands[k].at[layer, chip], sems)))
        return local, remote

    copies.n_arrays = n
    return copies


def _sem_shapes(n):
    return [pltpu.SemaphoreType.DMA((n * (N_DEV - 1),)), pltpu.SemaphoreType.DMA((n * (N_DEV - 1),)),
            pltpu.SemaphoreType.DMA((n,))]


def _sem_index(k, rel):
    return k * (N_DEV - 1) + rel - 1


def _exchange(copies, name, srcs, lands):
    n_src, n_land = len(srcs), len(lands)

    def body(*refs):
        src_refs = refs[:n_src]
        land_refs = refs[n_src + n_land:n_src + 2 * n_land]
        local, remote = copies(src_refs, land_refs, *refs[n_src + 2 * n_land:])
        local = [cp() for cp in local]
        sends = [send() for send, _ in remote]
        for cp in local + sends:
            cp.start()
        for send, (_, landing) in zip(sends, remote):
            send.wait_send()
            landing().wait_recv()
        for cp in local:
            cp.wait()

    return pl.pallas_call(
        body, name=name, in_specs=[HBM_SPEC] * (n_src + n_land), out_specs=[HBM_SPEC] * n_land,
        out_shape=[jax.ShapeDtypeStruct(b.shape, b.dtype) for b in lands],
        scratch_shapes=_sem_shapes(copies.n_arrays),
        input_output_aliases={n_src + k: k for k in range(n_land)},
        compiler_params=pltpu.CompilerParams(has_side_effects=True),
    )(*srcs, *lands)


def _hbm(arrays):
    return [pltpu.with_memory_space_constraint(a, pltpu.HBM) for a in arrays]


def _exchange_start(copies, name, srcs, lands, after):
    n_src, n_land, n_after = len(srcs), len(lands), len(after)
    n_data = n_src + n_land

    def body(*refs):
        outs = refs[n_data + n_after:]
        local, remote = copies(refs[:n_src], refs[n_src:n_data], *outs[:3])
        for cp in local:
            cp().start()
        for send, _ in remote:
            send().start()
        outs[-1][...] = jnp.zeros_like(outs[-1])

    res = pl.pallas_call(
        body, name=name, in_specs=[HBM_SPEC] * n_data + [ANY_SPEC] * n_after,
        out_specs=[SEM_SPEC] * 3 + [HBM_SPEC] * n_data + [pl.BlockSpec(memory_space=pltpu.VMEM)],
        out_shape=_sem_shapes(copies.n_arrays) + [pltpu.HBM(a.shape, a.dtype) for a in list(srcs) + list(lands)]
        + [jax.ShapeDtypeStruct((SUBLANES, LANES), F32)],
        input_output_aliases={k: 3 + k for k in range(n_data)},
        compiler_params=pltpu.CompilerParams(has_side_effects=DATAFLOW),
    )(*_hbm(list(srcs) + list(lands)), *after)
    return res[:3], res[3:3 + n_src], res[3 + n_src:3 + n_data], res[-1]


def _exchange_wait(copies, name, sems, srcs, lands, after):
    n_src, n_land, n_after = len(srcs), len(lands), len(after)
    n_data = n_src + n_land

    def body(*refs):
        local, remote = copies(refs[:n_src], refs[n_src:n_data], *refs[n_data:n_data + 3])
        for send, landing in remote:
            send().wait_send()
            landing().wait_recv()
        for cp in local:
            cp().wait()

    res = pl.pallas_call(
        body, name=name, in_specs=[HBM_SPEC] * n_data + [SEM_SPEC] * 3 + [ANY_SPEC] * n_after,
        out_specs=[HBM_SPEC] * n_data,
        out_shape=[pltpu.HBM(a.shape, a.dtype) for a in list(srcs) + list(lands)],
        input_output_aliases={k: k for k in range(n_data)},
        compiler_params=pltpu.CompilerParams(has_side_effects=DATAFLOW),
    )(*srcs, *lands, *sems, *after)
    return res[:n_src], res[n_src:]


def _block_diag(blocks, row_major):
    L, GB, g, P, N = blocks.shape
    eye = jnp.eye(g, dtype=blocks.dtype)
    if row_major:
        return jnp.einsum("lbgpn,gh->lbgphn", blocks, eye).reshape(L, GB, g * P, g * N)
    return jnp.einsum("lbgpn,gh->lbhngp", blocks, eye).reshape(L, GB, g * N, g * P)


def _diag_blocks(mat, g, P, N, row_major):
    GB = mat.shape[0]
    eye = jnp.eye(g, dtype=mat.dtype)
    if row_major:
        return jnp.einsum("bgphn,gh->bgpn", mat.reshape(GB, g, P, g, N), eye)
    return jnp.einsum("bhngp,gh->bgpn", mat.reshape(GB, g, N, g, P), eye)


def _pack(arrays, rows_multiple):
    flat = jnp.concatenate([a.reshape(-1).astype(F32) for a in arrays])
    pad = (-flat.shape[0]) % (rows_multiple * LANES)
    return jnp.pad(flat, (0, pad)).reshape(-1, LANES)


def _unpack(packed, shapes):
    flat = packed.reshape(-1)
    out, pos = [], 0
    for s in shapes:
        n = math.prod(s)
        out.append(flat[pos:pos + n].reshape(s))
        pos += n
    return out


SMALL = ["ln1_g", "ln1_b", "s5_lam_re", "s5_lam_im", "s5_log_dt", "s5_b_re", "s5_b_im", "s5_c_re", "s5_c_im", "s5_d",
         "conv_b", "g_s5", "g_conv", "ln2_g", "ln2_b", "ln3_g", "ln3_b"]
S5_B = ["s5_b_re", "s5_b_im"]
WEIGHTS = ["ffn1_gate", "ffn1_up", "ffn1_down", "ln1_g", "ln1_b", "w_in", "s5_lam_re", "s5_lam_im", "s5_log_dt",
           "s5_b_re", "s5_b_im", "s5_c_re", "s5_c_im", "s5_d", "s5_w_glu", "conv_w", "conv_b", "g_s5", "g_conv",
           "w_out", "ln2_g", "ln2_b", "ffn2_gate", "ffn2_up", "ffn2_down", "ln3_g", "ln3_b"]
TRANSPOSED = ["ffn1_gate", "ffn1_up", "w_in", "ffn2_gate", "ffn2_up"]
UPDATED_TRANSPOSED = ["ffn1_gate", "ffn1_up", "ffn2_gate", "ffn2_up"]
GROUPS = {"a": ["ffn1_gate", "ffn1_up", "ffn1_down"], "b": ["w_in", "s5_w_glu", "w_out"],
          "c": ["ffn2_gate", "ffn2_up", "ffn2_down"]}
FFN_GROUPS = ("a", "c")


def _train_step(x, target, w, m, v):
    T, D = x.shape
    L = w["ln1_g"].shape[0]
    alpha = (2.0 * L) ** 0.25
    G = w["s5_log_dt"].shape[1]
    d_s5 = G * S5_P
    d_conv = w["conv_b"].shape[1]
    GB = G // S5_GROUPS_PER_BLOCK
    me = _me()[3]

    def shard(n, l):
        return (jnp.swapaxes(w[n][l], 0, 1) if n in TRANSPOSED else w[n][l]).astype(BF16)

    conv_w_rows = jnp.pad(w["conv_w"], ((0, 0), (0, SUBLANES - CONV_W), (0, 0)))
    parts = [(l, grp) for l in range(L) for grp in GROUPS]
    step_one, step_two, token = {}, {}, []
    for l, grp in parts:
        srcs = [shard(n, l) for n in GROUPS[grp]] + ([conv_w_rows[l]] if grp == "b" else [])
        lands = [lax.empty((N_DEV,) + a.shape, a.dtype) for a in srcs]
        sems, srcs, lands, tok = _exchange_start(_gather_copies(len(srcs), SIBLING_AND_OTHER_CHIPS, grp in FFN_GROUPS),
                                                 f"gather_start_{l}{grp}", srcs, lands, token)
        step_one[l, grp] = (sems, srcs, lands)
        token = [tok]

    def forward_on(i, after):
        if i >= len(parts):
            return []
        l, grp = parts[i]
        sems, srcs, lands = step_one[l, grp]
        copies = _gather_copies(len(srcs), SIBLING_AND_OTHER_CHIPS, grp in FFN_GROUPS)
        _, lands = _exchange_wait(copies, f"gather_wait_{l}{grp}", sems, srcs, lands, after)
        sems, _, lands, tok = _exchange_start(_forward_copies(len(lands), grp in FFN_GROUPS), f"forward_start_{l}{grp}",
                                              [], lands, [])
        step_two[l, grp] = (sems, lands)
        return [tok]

    def gathered(i, after):
        l, grp = parts[i]
        sems, lands = step_two[l, grp]
        _, lands = _exchange_wait(_forward_copies(len(lands), grp in FFN_GROUPS), f"forward_wait_{l}{grp}", sems, [],
                                  lands, after)
        full = {n: p.reshape(-1, p.shape[-1]) for n, p in zip(GROUPS[grp], lands)}
        if grp == "b":
            full["conv_w"] = jnp.swapaxes(lands[-1][:, :CONV_W, :], 0, 1).reshape(CONV_W, d_conv)
        return full

    lre = w["s5_lam_re"].reshape(L * G, S5_N)
    lim = w["s5_lam_im"].reshape(L * G, S5_N)
    ldt = w["s5_log_dt"].reshape(L * G, 1)
    b_re = jnp.transpose(w["s5_b_re"], (3, 0, 1, 2)).reshape(S5_P, L * G, S5_N)
    b_im = jnp.transpose(w["s5_b_im"], (3, 0, 1, 2)).reshape(S5_P, L * G, S5_N)
    ab_re, ab_im, bb_re, bb_im = _s5_params_fwd(lre, lim, ldt, b_re, b_im)

    def groups(bb):
        return jnp.transpose(bb.reshape(S5_P, L, GB, S5_GROUPS_PER_BLOCK, S5_N), (1, 2, 3, 0, 4))

    bd_re = _block_diag(groups(bb_re), True).astype(BF16)
    bd_im = _block_diag(groups(bb_im), True).astype(BF16)
    c_shape = (L, GB, S5_GROUPS_PER_BLOCK, S5_P, S5_N)
    cd_re = _block_diag(w["s5_c_re"].reshape(c_shape), False).astype(BF16)
    cd_im = _block_diag(w["s5_c_im"].reshape(c_shape), False).astype(BF16)
    a_re = ab_re.reshape(L, 1, G * S5_N)
    a_im = ab_im.reshape(L, 1, G * S5_N)
    d_vec = w["s5_d"].reshape(L, 1, d_s5)

    def vec(name, l):
        return w[name][l].reshape(1, -1)

    saved, weights = [], []
    x_in, x_in_b = x, x.astype(BF16)
    token = forward_on(0, token)
    for l in range(L):
        i = len(GROUPS) * l
        gw = gathered(i, [x_in] if l else token)
        s = {"x0b": x_in_b}
        s["g1"], s["u1"], s["h1"] = _ffn_up(x_in_b, gw["ffn1_gate"], gw["ffn1_up"])
        x1, s["x1b"], s["xh1"], s["rstd1"] = _mm_res_ln(s["h1"], gw["ffn1_down"], x_in, vec("ln1_g", l),
                                                         vec("ln1_b", l), 0.5, alpha, forward_on(i + 1, [s["h1"]]))
        gw.update(gathered(i + 1, [s["x1b"]]))
        s["proj"] = _mm_nt(s["x1b"], gw["w_in"])
        s["y"], s["sr"], s["si"] = _s5_fwd(s["proj"], bd_re[l], bd_im[l], cd_re[l], cd_im[l], a_re[l], a_im[l],
                                           d_vec[l])
        s["z"] = _conv_fwd(s["proj"], gw["conv_w"], vec("conv_b", l), d_s5, d_conv)
        s["mcat"], s["gl"] = _mix_post(s["y"], s["z"], gw["s5_w_glu"], vec("g_s5", l), vec("g_conv", l))
        x2, s["x2b"], s["xh2"], s["rstd2"] = _mm_res_ln(s["mcat"], gw["w_out"], x1, vec("ln2_g", l),
                                                         vec("ln2_b", l), 1.0, alpha, forward_on(i + 2, [s["mcat"]]))
        gw.update(gathered(i + 2, [s["x2b"]]))
        s["g2"], s["u2"], s["h2"] = _ffn_up(s["x2b"], gw["ffn2_gate"], gw["ffn2_up"])
        x3, x3b, s["xh3"], s["rstd3"] = _mm_res_ln(s["h2"], gw["ffn2_down"], x2, vec("ln3_g", l), vec("ln3_b", l),
                                                   0.5, alpha, forward_on(i + 3, [s["h2"]]))
        saved.append(s)
        weights.append(gw)
        x_in, x_in_b = x3, x3b

    last = saved[L - 1]
    dr, drb, dg, db, loss = _loss_ln_bwd(x_in, target, last["xh3"], last["rstd3"], vec("ln3_g", L - 1))
    small = [dict() for _ in range(L)]
    small[L - 1]["ln3_g"], small[L - 1]["ln3_b"] = dg, db
    bufs = {grp: [lax.empty((L, N_CHIPS) + shard(n, 0).shape, BF16) for n in names] for grp, names in GROUPS.items()}
    scatters = {grp: [] for grp in GROUPS}
    grad_x = None

    def chip_scatter(l, grp, sums):
        sems, sums, bufs[grp], tok = _exchange_start(_chip_scatter_copies(len(sums), l), f"scatter_start_{l}{grp}", sums,
                                                     bufs[grp], [])
        scatters[grp].append((l, sems, sums))
        return [tok]

    small_names = SMALL + ["conv_w"]
    b_shape = (L, G, S5_P, S5_N)
    small_shapes = [b_shape if n in S5_B else w[n].shape for n in SMALL] + [(L, CONV_W, d_conv)]

    def small_scatter_start(small):
        def stack(key):
            return jnp.stack([small[l][key] for l in range(L)])

        d_bb_re = jnp.transpose(stack("d_bb_re").reshape(L * G, S5_P, S5_N), (1, 0, 2))
        d_bb_im = jnp.transpose(stack("d_bb_im").reshape(L * G, S5_P, S5_N), (1, 0, 2))
        g_lre, g_lim, g_ldt, g_bre, g_bim = _s5_params_bwd(
            lre, lim, ldt, b_re, b_im, stack("d_ab_re").reshape(L * G, S5_N), stack("d_ab_im").reshape(L * G, S5_N),
            d_bb_re, d_bb_im)
        part = {n: [small[l][n] for l in range(L)]
                for n in ["ln1_g", "ln1_b", "s5_c_re", "s5_c_im", "s5_d", "conv_b", "g_s5", "g_conv", "ln2_g", "ln2_b",
                          "ln3_g", "ln3_b", "conv_w"]}
        part["s5_lam_re"], part["s5_lam_im"], part["s5_log_dt"] = [g_lre], [g_lim], [g_ldt]
        part["s5_b_re"] = [jnp.transpose(g_bre, (1, 0, 2))]
        part["s5_b_im"] = [jnp.transpose(g_bim, (1, 0, 2))]
        packed = _pack([piece for n in small_names for piece in part[n]], N_DEV * PACK_ROWS)
        rows = packed.shape[0] // N_DEV
        return _exchange_start(_scatter_copies(1, 0), "scatter_small_start", [packed.reshape(N_DEV, rows, LANES)],
                               [lax.empty((1, N_DEV, rows, LANES), F32)], [])

    token = []
    for l in reversed(range(L)):
        gw, s, sm = weights[l], saved[l], small[l]
        full = {}

        def ffn_bwd(dr, drb, tag, grp, xb_in, ln, after):
            dgp, dup = _ffn_down_bwd(drb, gw[f"ffn{tag}_down"], s[f"g{tag}"], s[f"u{tag}"], after)
            terms = {"gate": (dgp, xb_in, 1.0), "up": (dup, xb_in, 1.0), "down": (s[f"h{tag}"], drb, 0.5)}
            terms = [terms[n.split("_")[1]] for n in GROUPS[grp]]
            away = [_mm_tn(a, b, scale, BF16, half=0).reshape((N_CHIPS, -1, b.shape[1])) for a, b, scale in terms]
            stages = [lax.empty(a.shape, BF16) for a in away]
            sems, away, stages, tok = _exchange_start(_swap_copies(len(away)), f"pair_start_{l}{grp}", away, stages, [])
            res = _mm_dx([(dgp, gw[f"ffn{tag}_gate"]), (dup, gw[f"ffn{tag}_up"])], dr, alpha, ln, [tok])
            _, stages = _exchange_wait(_swap_copies(len(away)), f"pair_wait_{l}{grp}", sems, away, stages, [res[0]])
            sums = [_mm_tn(a, b, scale, BF16, half=1, addend=st.reshape(-1, b.shape[1])).reshape(st.shape)
                    for (a, b, scale), st in zip(terms, stages)]
            return res, chip_scatter(l, grp, sums)

        (dr, drb, sm["ln2_g"], sm["ln2_b"]), token = ffn_bwd(dr, drb, 2, "c", s["x2b"],
                                                              (s["xh2"], s["rstd2"], vec("ln2_g", l)), token)
        dm = _mm_nt(drb, gw["w_out"], token)
        full["w_out"] = _mm_tn(s["mcat"], drb, 1.0, BF16)
        dy, dz, dwglu, sm["g_s5"], sm["g_conv"] = _mix_post_bwd(dm, s["y"], s["gl"], s["z"], gw["s5_w_glu"],
                                                                vec("g_s5", l), vec("g_conv", l))
        full["s5_w_glu"] = dwglu.astype(BF16)
        du, dbd_re, dbd_im, dcd_re, dcd_im, sm["d_ab_re"], sm["d_ab_im"], sm["s5_d"] = _s5_bwd(
            dy, s["proj"], s["sr"], s["si"], bd_re[l], bd_im[l], cd_re[l], cd_im[l], a_re[l], a_im[l], d_vec[l])
        gsz = (S5_GROUPS_PER_BLOCK, S5_P, S5_N)
        sm["d_bb_re"] = _diag_blocks(dbd_re, *gsz, True)
        sm["d_bb_im"] = _diag_blocks(dbd_im, *gsz, True)
        sm["s5_c_re"] = _diag_blocks(dcd_re, *gsz, False).reshape(G, S5_P, S5_N)
        sm["s5_c_im"] = _diag_blocks(dcd_im, *gsz, False).reshape(G, S5_P, S5_N)
        dgb, dgc, dhh, sm["conv_w"], sm["conv_b"] = _conv_bwd(dz, s["proj"], gw["conv_w"], vec("conv_b", l),
                                                              d_s5, d_conv)
        dproj = jnp.concatenate([du.astype(BF16), dgb, dgc, dhh], axis=1)
        full["w_in"] = _mm_tn(dproj, s["x1b"], 1.0, BF16)
        fulls = [full[n].reshape((N_DEV, -1) + full[n].shape[1:]) for n in GROUPS["b"]]
        stages = [lax.empty((N_CHIPS,) + f.shape[1:], BF16) for f in fulls]
        sems, fulls, stages, tok = _exchange_start(_pair_copies(len(fulls)), f"pair_start_{l}b", fulls, stages, [])
        dr, drb, sm["ln1_g"], sm["ln1_b"] = _mm_dx([(dproj, gw["w_in"])], dr, alpha,
                                                   (s["xh1"], s["rstd1"], vec("ln1_g", l)), [tok])
        fulls, stages = _exchange_wait(_pair_copies(len(fulls)), f"pair_wait_{l}b", sems, fulls, stages, [dr])
        token = chip_scatter(l, "b", [_pair_sum(f, st) for f, st in zip(fulls, stages)])
        if l > 0:
            prev = saved[l - 1]
            (dr, drb, small[l - 1]["ln3_g"], small[l - 1]["ln3_b"]), token = ffn_bwd(
                dr, drb, 1, "a", s["x0b"], (prev["xh3"], prev["rstd3"], vec("ln3_g", l - 1)), token)
        else:
            small_sems, small_srcs, small_lands, tok = small_scatter_start(small)
            ((grad_x,), token) = ffn_bwd(dr, drb, 1, "a", s["x0b"], None, token + [tok])

    _, landed = _exchange_wait(_scatter_copies(1, 0), "scatter_small_wait", small_sems, small_srcs, small_lands,
                               [grad_x])
    mine = _reduce_parts(landed[0])
    (summed,) = _exchange(_gather_copies(1), "gather_small", [mine[0]], [lax.empty(landed[0].shape[1:], F32)])
    small_grads = dict(zip(small_names, _unpack(summed, small_shapes)))

    out = {}

    def update(name, w3, m3, v3, shape, **grad):
        res = _adamw_update(w3, m3, v3, **grad)
        out[name] = [r.reshape(shape) for r in res]
        return res

    cw_shape = w["conv_w"].shape
    g_cw = lax.dynamic_slice_in_dim(small_grads["conv_w"], me * cw_shape[2], cw_shape[2], axis=2)
    done = [update("conv_w", w["conv_w"], m["conv_w"], v["conv_w"], cw_shape, grad=g_cw)[3]]
    for n in SMALL:
        if n in S5_B:
            def view(a):
                return jnp.swapaxes(a, 2, 3).reshape(1, -1, S5_N)
            res = _adamw_update(view(w[n]), view(m[n]), view(v[n]), grad=small_grads[n].reshape(1, -1, S5_N))
            out[n] = [jnp.swapaxes(r.reshape(b_shape), 2, 3) for r in res]
        else:
            res = update(n, *(a[n].reshape(1, -1, a[n].shape[-1]) for a in (w, m, v)), w[n].shape,
                         grad=small_grads[n].reshape(1, -1, w[n].shape[-1]))
        done.append(res[3])
    for grp in ("c", "b", "a"):
        for l, sems, sums in scatters[grp]:
            _, bufs[grp] = _exchange_wait(_chip_scatter_copies(len(sums), l), f"scatter_wait_{l}{grp}", sems, sums,
                                          bufs[grp], [grad_x] + token + done)
        done = []
        for n, parts in zip(GROUPS[grp], bufs[grp]):
            if n in UPDATED_TRANSPOSED:
                res = _adamw_update(*(jnp.swapaxes(a[n], 1, 2) for a in (w, m, v)), parts=parts)
                out[n] = [jnp.swapaxes(r, 1, 2) for r in res]
            elif n in TRANSPOSED:
                res = update(n, w[n], m[n], v[n], w[n].shape, grad=jnp.swapaxes(_reduce_parts(parts), 1, 2))
            else:
                res = update(n, w[n], m[n], v[n], w[n].shape, parts=parts)
            done.append(res[3])

    loss = lax.psum(loss[0, 0], AXES)
    return loss, grad_x, out


def kernel(x, ffn1_gate, ffn1_up, ffn1_down, ln1_g, ln1_b, w_in, s5_lam_re, s5_lam_im, s5_log_dt, s5_b_re, s5_b_im, s5_c_re, s5_c_im, s5_d, s5_w_glu, conv_w, conv_b, g_s5, g_conv, w_out, ln2_g, ln2_b, ffn2_gate, ffn2_up, ffn2_down, ln3_g, ln3_b, loss_target, m_ffn1_gate, m_ffn1_up, m_ffn1_down, m_ln1_g, m_ln1_b, m_w_in, m_s5_lam_re, m_s5_lam_im, m_s5_log_dt, m_s5_b_re, m_s5_b_im, m_s5_c_re, m_s5_c_im, m_s5_d, m_s5_w_glu, m_conv_w, m_conv_b, m_g_s5, m_g_conv, m_w_out, m_ln2_g, m_ln2_b, m_ffn2_gate, m_ffn2_up, m_ffn2_down, m_ln3_g, m_ln3_b, v_ffn1_gate, v_ffn1_up, v_ffn1_down, v_ln1_g, v_ln1_b, v_w_in, v_s5_lam_re, v_s5_lam_im, v_s5_log_dt, v_s5_b_re, v_s5_b_im, v_s5_c_re, v_s5_c_im, v_s5_d, v_s5_w_glu, v_conv_w, v_conv_b, v_g_s5, v_g_conv, v_w_out, v_ln2_g, v_ln2_b, v_ffn2_gate, v_ffn2_up, v_ffn2_down, v_ln3_g, v_ln3_b):
    given = dict(locals())
    w = {n: given[n] for n in WEIGHTS}
    m = {n: given["m_" + n] for n in WEIGHTS}
    v = {n: given["v_" + n] for n in WEIGHTS}
    T, D = x.shape[-2:]
    loss, grad_x, out = _train_step(x.reshape(T, D), loss_target.reshape(T, D), w, m, v)
    results = [loss, grad_x.reshape(x.shape)]
    for i in range(4):
        results += [out[n][i] for n in WEIGHTS]
    return tuple(results)
```

```python
import functools
import math

import jax
import jax.numpy as jnp
from jax import lax
from jax.experimental import pallas as pl
from jax.experimental.pallas import tpu as pltpu

F32 = jnp.float32
BF16 = jnp.bfloat16
MESH = pl.DeviceIdType.MESH
AXES = ("x", "y", "c")
N_DEV = 8
N_CHIPS = 4

S5_P = 16
S5_N = 64
CONV_W = 3
LN_EPS = 1e-5
RMS_EPS = 1e-6
ADAM_LR = 0.001
ADAM_B1 = 0.9
ADAM_B2 = 0.999
ADAM_EPS = 1e-08
ADAM_WD = 0.01
ADAM_STEP = 10

V7X_VMEM_BYTES = 64 * 1024 * 1024
VMEM_LIMIT = V7X_VMEM_BYTES * 7 // 8
LANES = 128
SUBLANES = 8
BF16_ROWS = 16
MXU_COLS = 256
ROWS_RESIDENT = 2048
PACK_ROWS = 512
S5_GROUPS_PER_BLOCK = LANES // S5_P
S5_STATE_BLOCK = S5_GROUPS_PER_BLOCK * S5_N

HBM_SPEC = pl.BlockSpec(memory_space=pltpu.HBM)
SEM_SPEC = pl.BlockSpec(memory_space=pltpu.SEMAPHORE)
ANY_SPEC = pl.BlockSpec(memory_space=pl.ANY)


def _tile(n, pref, align):
    best = None
    d = align
    while d <= min(n, pref):
        if n % d == 0:
            best = d
        d += align
    return best if best is not None else n


def _params(*sem):
    return pltpu.CompilerParams(dimension_semantics=sem, vmem_limit_bytes=VMEM_LIMIT)


def _dot_nn(a, b):
    return lax.dot_general(a, b, (((1,), (0,)), ((), ())), preferred_element_type=F32)


def _dot_nt(a, b):
    return lax.dot_general(a, b, (((1,), (1,)), ((), ())), preferred_element_type=F32)


def _dot_tn(a, b):
    return lax.dot_general(a, b, (((0,), (0,)), ((), ())), preferred_element_type=F32)


def _colsum(v):
    return jnp.sum(v, axis=0, keepdims=True)


def _rowmean(v):
    return jnp.mean(v, axis=-1, keepdims=True)


def _ffn_up(xb, wg, wu):
    T, D = xb.shape
    F = wg.shape[0]
    tm = _tile(T, ROWS_RESIDENT, 16)
    tn = _tile(F, MXU_COLS, LANES)

    def body(x_ref, wg_ref, wu_ref, cg_ref, cu_ref, h_ref):
        x = x_ref[...]
        g = _dot_nt(x, wg_ref[...])
        u = _dot_nt(x, wu_ref[...])
        sg = jax.nn.sigmoid(g)
        silu = g * sg
        cu_ref[...] = silu.astype(BF16)
        cg_ref[...] = (u * (sg * (1.0 + g * (1.0 - sg)))).astype(BF16)
        h_ref[...] = (silu * u).astype(BF16)

    w_spec = pl.BlockSpec((tn, D), lambda j, i: (j, 0))
    o_spec = pl.BlockSpec((tm, tn), lambda j, i: (i, j))
    return pl.pallas_call(
        body, name="ffn_up", grid=(F // tn, T // tm),
        in_specs=[pl.BlockSpec((tm, D), lambda j, i: (i, 0)), w_spec, w_spec],
        out_specs=[o_spec, o_spec, o_spec],
        out_shape=[jax.ShapeDtypeStruct((T, F), BF16)] * 3,
        compiler_params=_params("arbitrary", "arbitrary"),
    )(xb, wg, wu)


def _mm_acc(pairs, after=()):
    T, K = pairs[0][0].shape
    D = pairs[0][1].shape[1]
    n = len(pairs)
    tk = _tile(K, 512, LANES)
    tm = _tile(T, 512, 16)

    def body(*refs):
        o_ref = refs[-1]

        @pl.when(pl.program_id(0) == 0)
        def _():
            o_ref[...] = jnp.zeros_like(o_ref)

        for r in range(0, T, tm):
            part = _dot_nn(refs[0][r:r + tm, :], refs[1][...])
            for a_ref, w_ref in zip(refs[2:2 * n:2], refs[3:2 * n:2]):
                part += _dot_nn(a_ref[r:r + tm, :], w_ref[...])
            o_ref[r:r + tm, :] += part

    in_specs, operands = [], []
    for a, w in pairs:
        in_specs += [pl.BlockSpec((T, tk), lambda k: (0, k)), pl.BlockSpec((tk, D), lambda k: (k, 0))]
        operands += [a, w]
    return pl.pallas_call(
        body, name="mm_acc", grid=(K // tk,),
        in_specs=in_specs + [ANY_SPEC] * len(after),
        out_specs=pl.BlockSpec((T, D), lambda k: (0, 0)),
        out_shape=jax.ShapeDtypeStruct((T, D), F32),
        compiler_params=_params("arbitrary"),
    )(*operands, *after)


def _mm_res_ln(a, w, res, g, b, scale, alpha, after=()):
    acc = _mm_acc([(a, w)], after)
    T, D = acc.shape
    tm = _tile(T, 256, 16)

    def body(acc_ref, res_ref, g_ref, b_ref, xo_ref, xb_ref, xh_ref, rstd_ref):
        r = alpha * res_ref[...] + scale * acc_ref[...]
        xc = r - _rowmean(r)
        rstd = lax.rsqrt(_rowmean(xc * xc) + LN_EPS)
        xh = xc * rstd
        xo = xh * g_ref[...] + b_ref[...]
        xo_ref[...] = xo
        xb_ref[...] = xo.astype(BF16)
        xh_ref[...] = xh
        rstd_ref[...] = rstd

    row = pl.BlockSpec((tm, D), lambda i: (i, 0))
    vec = pl.BlockSpec((1, D), lambda i: (0, 0))
    return pl.pallas_call(
        body, name="res_ln", grid=(T // tm,),
        in_specs=[row, row, vec, vec],
        out_specs=[row, row, row, pl.BlockSpec((tm, 1), lambda i: (i, 0))],
        out_shape=[jax.ShapeDtypeStruct((T, D), F32), jax.ShapeDtypeStruct((T, D), BF16),
                   jax.ShapeDtypeStruct((T, D), F32), jax.ShapeDtypeStruct((T, 1), F32)],
        compiler_params=_params("arbitrary"),
    )(acc, res, g, b)


def _mm_nt(a, w, after=()):
    M, K = a.shape
    N = w.shape[0]
    tm = _tile(M, ROWS_RESIDENT, 16)
    tn = _tile(N, MXU_COLS, LANES)

    def body(a_ref, w_ref, *rest):
        rest[-1][...] = _dot_nt(a_ref[...], w_ref[...])

    return pl.pallas_call(
        body, name="mm_nt", grid=(N // tn, M // tm),
        in_specs=[pl.BlockSpec((tm, K), lambda j, i: (i, 0)), pl.BlockSpec((tn, K), lambda j, i: (j, 0))]
        + [ANY_SPEC] * len(after),
        out_specs=pl.BlockSpec((tm, tn), lambda j, i: (i, j)),
        out_shape=jax.ShapeDtypeStruct((M, N), F32),
        compiler_params=_params("arbitrary", "arbitrary"),
    )(a, w, *after)


def _mm_tn(a, b, scale, out_dtype, half=None, addend=None):
    T, M = a.shape
    N = b.shape[1]
    rows = M if half is None else M // 2
    tm = _tile(rows, 512, LANES)
    tn = _tile(N, ROWS_RESIDENT, LANES)
    if tm < 512:
        tm = _tile(rows, 1408, LANES)
        tn = _tile(N, ROWS_RESIDENT // 2, LANES)
    first = 0 if half is None else half * (rows // tm)

    def body(a_ref, b_ref, *rest):
        out = scale * _dot_tn(a_ref[...], b_ref[...])
        if addend is not None:
            out = out + rest[0][...].astype(F32)
        rest[-1][...] = out.astype(out_dtype)

    o_spec = pl.BlockSpec((tm, tn), lambda i, j: (i, j))
    return pl.pallas_call(
        body, name="mm_tn", grid=(rows // tm, N // tn),
        in_specs=[pl.BlockSpec((T, tm), lambda i, j: (0, first + i)), pl.BlockSpec((T, tn), lambda i, j: (0, j))]
        + ([] if addend is None else [o_spec]),
        out_specs=o_spec,
        out_shape=jax.ShapeDtypeStruct((rows, N), out_dtype),
        compiler_params=_params("arbitrary", "arbitrary"),
    )(a, b, *([] if addend is None else [addend]))


def _ln_bwd(dy, xh, rstd, g):
    dxh = dy * g
    dr = rstd * (dxh - _rowmean(dxh) - xh * _rowmean(dxh * xh))
    return dr, _colsum(dy * xh), _colsum(dy)


def _loss_ln_bwd(y, target, xh, rstd, g):
    T, D = y.shape
    tm = _tile(T, 256, 16)

    def body(y_ref, t_ref, xh_ref, rstd_ref, g_ref, dr_ref, drb_ref, dg_ref, db_ref, loss_ref):
        i = pl.program_id(0)

        @pl.when(i == 0)
        def _():
            dg_ref[...] = jnp.zeros_like(dg_ref)
            db_ref[...] = jnp.zeros_like(db_ref)
            loss_ref[...] = jnp.zeros_like(loss_ref)

        err = y_ref[...] - t_ref[...]
        loss_ref[...] += (0.5 / D) * _colsum(jnp.sum(err * err, axis=1, keepdims=True))
        dr, dg, db = _ln_bwd(err * (1.0 / D), xh_ref[...], rstd_ref[...], g_ref[...])
        dr_ref[...] = dr
        drb_ref[...] = dr.astype(BF16)
        dg_ref[...] += dg
        db_ref[...] += db

    row = pl.BlockSpec((tm, D), lambda i: (i, 0))
    vec = pl.BlockSpec((1, D), lambda i: (0, 0))
    return pl.pallas_call(
        body, name="loss_ln_bwd", grid=(T // tm,),
        in_specs=[row, row, row, pl.BlockSpec((tm, 1), lambda i: (i, 0)), vec],
        out_specs=[row, row, vec, vec, pl.BlockSpec((1, 1), lambda i: (0, 0))],
        out_shape=[jax.ShapeDtypeStruct((T, D), F32), jax.ShapeDtypeStruct((T, D), BF16),
                   jax.ShapeDtypeStruct((1, D), F32), jax.ShapeDtypeStruct((1, D), F32),
                   jax.ShapeDtypeStruct((1, 1), F32)],
        compiler_params=_params("arbitrary"),
    )(y, target, xh, rstd, g)


def _ffn_down_bwd(drb, wd, cg, cu, after=()):
    T, D = drb.shape
    F = wd.shape[0]
    tm = _tile(T, ROWS_RESIDENT, 16)
    tn = _tile(F, MXU_COLS, LANES)

    def body(dr_ref, wd_ref, cg_ref, cu_ref, *rest):
        dg_ref, du_ref = rest[len(after):]
        dh = 0.5 * _dot_nt(dr_ref[...], wd_ref[...])
        du_ref[...] = (dh * cu_ref[...].astype(F32)).astype(BF16)
        dg_ref[...] = (dh * cg_ref[...].astype(F32)).astype(BF16)

    t_spec = pl.BlockSpec((tm, tn), lambda j, i: (i, j))
    return pl.pallas_call(
        body, name="ffn_down_bwd", grid=(F // tn, T // tm),
        in_specs=[pl.BlockSpec((tm, D), lambda j, i: (i, 0)), pl.BlockSpec((tn, D), lambda j, i: (j, 0)),
                  t_spec, t_spec] + [ANY_SPEC] * len(after),
        out_specs=[t_spec, t_spec],
        out_shape=[jax.ShapeDtypeStruct((T, F), BF16), jax.ShapeDtypeStruct((T, F), BF16)],
        compiler_params=_params("arbitrary", "arbitrary"),
    )(drb, wd, cg, cu, *after)


def _mm_dx(pairs, res, alpha, ln=None, after=()):
    acc = _mm_acc(pairs, after)
    T, D = acc.shape
    tm = _tile(T, 256, 16)
    with_ln = ln is not None

    def body(acc_ref, res_ref, *refs):
        dx = alpha * res_ref[...] + acc_ref[...]
        if with_ln:
            xh_ref, rstd_ref, g_ref, dr_ref, drb_ref, dg_ref, db_ref = refs

            @pl.when(pl.program_id(0) == 0)
            def _():
                dg_ref[...] = jnp.zeros_like(dg_ref)
                db_ref[...] = jnp.zeros_like(db_ref)

            dr, dg, db = _ln_bwd(dx, xh_ref[...], rstd_ref[...], g_ref[...])
            dr_ref[...] = dr
            drb_ref[...] = dr.astype(BF16)
            dg_ref[...] += dg
            db_ref[...] += db
        else:
            refs[0][...] = dx

    row = pl.BlockSpec((tm, D), lambda i: (i, 0))
    vec = pl.BlockSpec((1, D), lambda i: (0, 0))
    in_specs, operands = [row, row], [acc, res]
    if with_ln:
        in_specs += [row, pl.BlockSpec((tm, 1), lambda i: (i, 0)), vec]
        operands += list(ln)
        out_specs = [row, row, vec, vec]
        out_shape = [jax.ShapeDtypeStruct((T, D), F32), jax.ShapeDtypeStruct((T, D), BF16),
                     jax.ShapeDtypeStruct((1, D), F32), jax.ShapeDtypeStruct((1, D), F32)]
    else:
        out_specs = [row]
        out_shape = [jax.ShapeDtypeStruct((T, D), F32)]
    return pl.pallas_call(
        body, name="dx_ln_bwd" if with_ln else "dx_res", grid=(T // tm,),
        in_specs=in_specs, out_specs=out_specs, out_shape=out_shape,
        compiler_params=_params("arbitrary"),
    )(*operands)


def _s5_discretize(lre, lim, ldt, br, bi):
    dt = jnp.exp(ldt)
    mag = jnp.exp(lre * dt)
    ang = lim * dt
    ar = mag * jnp.cos(ang)
    ai = mag * jnp.sin(ang)
    den = lre * lre + lim * lim
    nr = ar - 1.0
    qr = (nr * lre + ai * lim) / den
    qi = (ai * lre - nr * lim) / den
    bbr = qr[None] * br - qi[None] * bi
    bbi = qr[None] * bi + qi[None] * br
    return ar, ai, bbr, bbi


def _s5_params_fwd(lre, lim, ldt, br, bi):
    def body(lre_ref, lim_ref, ldt_ref, br_ref, bi_ref, ar_ref, ai_ref, bbr_ref, bbi_ref):
        ar, ai, bbr, bbi = _s5_discretize(lre_ref[...], lim_ref[...], ldt_ref[...], br_ref[...], bi_ref[...])
        ar_ref[...] = ar
        ai_ref[...] = ai
        bbr_ref[...] = bbr
        bbi_ref[...] = bbi

    sds = jax.ShapeDtypeStruct
    return pl.pallas_call(
        body, name="s5_params_fwd",
        out_shape=[sds(lre.shape, F32), sds(lre.shape, F32), sds(br.shape, F32), sds(br.shape, F32)],
        compiler_params=pltpu.CompilerParams(vmem_limit_bytes=VMEM_LIMIT),
    )(lre, lim, ldt, br, bi)


def _s5_params_bwd(lre, lim, ldt, br, bi, dar, dai, dbbr, dbbi):
    def body(lre_ref, lim_ref, ldt_ref, br_ref, bi_ref, dar_ref, dai_ref, dbbr_ref, dbbi_ref,
             o_lre, o_lim, o_ldt, o_br, o_bi):
        _, vjp = jax.vjp(_s5_discretize, lre_ref[...], lim_ref[...], ldt_ref[...], br_ref[...], bi_ref[...])
        g = vjp((dar_ref[...], dai_ref[...], dbbr_ref[...], dbbi_ref[...]))
        o_lre[...] = g[0]
        o_lim[...] = g[1]
        o_ldt[...] = g[2]
        o_br[...] = g[3]
        o_bi[...] = g[4]

    sds = jax.ShapeDtypeStruct
    return pl.pallas_call(
        body, name="s5_params_bwd",
        out_shape=[sds(lre.shape, F32), sds(lre.shape, F32), sds(ldt.shape, F32), sds(br.shape, F32),
                   sds(br.shape, F32)],
        compiler_params=pltpu.CompilerParams(vmem_limit_bytes=VMEM_LIMIT),
    )(lre, lim, ldt, br, bi, dar, dai, dbbr, dbbi)


def _cmul(a_re, a_im, b_re, b_im):
    return a_re * b_re - a_im * b_im, a_re * b_im + a_im * b_re


def _s5_tile(i):
    if isinstance(i, int):
        return pl.ds(i * SUBLANES, SUBLANES)
    return pl.ds(pl.multiple_of(i * SUBLANES, SUBLANES), SUBLANES)


def _s5_powers(a_re, a_im, pwr_ref, pwi_ref, seg):
    def step(i, carry):
        p_re, p_im = carry
        pwr_ref[pl.ds(i, 1), :] = p_re
        pwi_ref[pl.ds(i, 1), :] = p_im
        return _cmul(a_re, a_im, p_re, p_im)

    lax.fori_loop(0, seg, step, (a_re, a_im))


def _s5_fwd(proj, bdr, bdi, cdr, cdi, ar, ai, dvec):
    T = proj.shape[0]
    GB, UB, SB = bdr.shape
    tc = _tile(T, 256, SUBLANES * SUBLANES)
    seg = tc // SUBLANES

    def body(u_ref, bdr_ref, bdi_ref, cdr_ref, cdi_ref, ar_ref, ai_ref, d_ref, y_ref, sr_ref, si_ref,
             cr_ref, ci_ref, pwr_ref, pwi_ref, str_ref, sti_ref, up_ref, yp_ref):
        a_re = ar_ref[...]
        a_im = ai_ref[...]

        @pl.when(pl.program_id(1) == 0)
        def _():
            cr_ref[...] = jnp.zeros_like(cr_ref)
            ci_ref[...] = jnp.zeros_like(ci_ref)
            _s5_powers(a_re, a_im, pwr_ref, pwi_ref, seg)

        for i in range(seg):
            up_ref[_s5_tile(i), :] = u_ref[pl.ds(i, SUBLANES, stride=seg), :]
        u = up_ref[...]
        ub = u.astype(BF16)
        sr_ref[...] = _dot_nn(ub, bdr_ref[...])
        si_ref[...] = _dot_nn(ub, bdi_ref[...])

        def local(i, carry):
            p_re, p_im = carry
            rows = _s5_tile(i)
            n_re, n_im = _cmul(a_re, a_im, p_re, p_im)
            n_re, n_im = n_re + sr_ref[rows, :], n_im + si_ref[rows, :]
            sr_ref[rows, :] = n_re
            si_ref[rows, :] = n_im
            return n_re, n_im

        e_re, e_im = lax.fori_loop(1, seg, local, (sr_ref[_s5_tile(0), :], si_ref[_s5_tile(0), :]))
        s_re, s_im = cr_ref[...], ci_ref[...]
        top_re, top_im = pwr_ref[seg - 1:seg, :], pwi_ref[seg - 1:seg, :]
        for j in range(SUBLANES):
            str_ref[j:j + 1, :] = s_re
            sti_ref[j:j + 1, :] = s_im
            n_re, n_im = _cmul(top_re, top_im, s_re, s_im)
            s_re, s_im = n_re + e_re[j:j + 1, :], n_im + e_im[j:j + 1, :]
        cr_ref[...] = s_re
        ci_ref[...] = s_im
        b_re, b_im = str_ref[...], sti_ref[...]

        def fix(i, carry):
            rows = _s5_tile(i)
            f_re, f_im = _cmul(pwr_ref[pl.ds(i, 1), :], pwi_ref[pl.ds(i, 1), :], b_re, b_im)
            sr_ref[rows, :] += f_re
            si_ref[rows, :] += f_im
            return carry

        lax.fori_loop(0, seg, fix, 0)
        yp_ref[...] = (_dot_nn(sr_ref[...].astype(BF16), cdr_ref[...])
                       - _dot_nn(si_ref[...].astype(BF16), cdi_ref[...]) + d_ref[...] * u)
        for i in range(seg):
            y_ref[pl.ds(i, SUBLANES, stride=seg), :] = yp_ref[_s5_tile(i), :]

    return pl.pallas_call(
        body, name="s5_fwd", grid=(GB, T // tc),
        in_specs=[pl.BlockSpec((tc, UB), lambda j, t: (t, j)),
                  pl.BlockSpec((None, UB, SB), lambda j, t: (j, 0, 0)),
                  pl.BlockSpec((None, UB, SB), lambda j, t: (j, 0, 0)),
                  pl.BlockSpec((None, SB, UB), lambda j, t: (j, 0, 0)),
                  pl.BlockSpec((None, SB, UB), lambda j, t: (j, 0, 0)),
                  pl.BlockSpec((1, SB), lambda j, t: (0, j)),
                  pl.BlockSpec((1, SB), lambda j, t: (0, j)),
                  pl.BlockSpec((1, UB), lambda j, t: (0, j))],
        out_specs=[pl.BlockSpec((tc, UB), lambda j, t: (t, j)),
                   pl.BlockSpec((tc, SB), lambda j, t: (t, j)),
                   pl.BlockSpec((tc, SB), lambda j, t: (t, j))],
        out_shape=[jax.ShapeDtypeStruct((T, GB * UB), F32), jax.ShapeDtypeStruct((T, GB * SB), F32),
                   jax.ShapeDtypeStruct((T, GB * SB), F32)],
        scratch_shapes=[pltpu.VMEM((1, SB), F32), pltpu.VMEM((1, SB), F32),
                        pltpu.VMEM((seg, SB), F32), pltpu.VMEM((seg, SB), F32),
                        pltpu.VMEM((SUBLANES, SB), F32), pltpu.VMEM((SUBLANES, SB), F32),
                        pltpu.VMEM((tc, UB), F32), pltpu.VMEM((tc, UB), F32)],
        compiler_params=_params("arbitrary", "arbitrary"),
    )(proj, bdr, bdi, cdr, cdi, ar, ai, dvec)


def _s5_bwd(dy, proj, sr, si, bdr, bdi, cdr, cdi, ar, ai, dvec):
    T = dy.shape[0]
    GB, UB, SB = bdr.shape
    tc = _tile(T, 256, SUBLANES * SUBLANES)
    seg = tc // SUBLANES
    nt = T // tc
    halo_blocks = tc // SUBLANES

    def body(dy_ref, u_ref, sr_ref, si_ref, hr_ref, hi_ref, bdr_ref, bdi_ref, cdr_ref, cdi_ref, ar_ref, ai_ref,
             d_ref, du_ref, dbdr_ref, dbdi_ref, dcdr_ref, dcdi_ref, dar_ref, dai_ref, dd_ref,
             gr_ref, gi_ref, pwr_ref, pwi_ref, rtr_ref, rti_ref, cr_ref, ci_ref, dyp_ref, up_ref, dup_ref):
        step_no = pl.program_id(1)
        first_chunk = step_no == nt - 1
        a_re = ar_ref[...]
        a_im = ai_ref[...]

        @pl.when(step_no == 0)
        def _():
            for ref in (cr_ref, ci_ref, dbdr_ref, dbdi_ref, dcdr_ref, dcdi_ref, dar_ref, dai_ref, dd_ref):
                ref[...] = jnp.zeros_like(ref)
            _s5_powers(a_re, a_im, pwr_ref, pwi_ref, seg)

        for i in range(seg):
            dyp_ref[_s5_tile(i), :] = dy_ref[pl.ds(i, SUBLANES, stride=seg), :]
            up_ref[_s5_tile(i), :] = u_ref[pl.ds(i, SUBLANES, stride=seg), :]
        dy = dyp_ref[...]
        dyb = dy.astype(BF16)
        u = up_ref[...]
        gr_ref[...] = _dot_nt(dyb, cdr_ref[...])
        gi_ref[...] = -_dot_nt(dyb, cdi_ref[...])
        dcdr_ref[...] += _dot_tn(sr_ref[...].astype(BF16), dyb)
        dcdi_ref[...] -= _dot_tn(si_ref[...].astype(BF16), dyb)

        def local(n, carry):
            c_re, c_im = carry
            rows = _s5_tile(seg - 1 - n)
            g_re = gr_ref[rows, :] + a_re * c_re + a_im * c_im
            g_im = gi_ref[rows, :] + a_re * c_im - a_im * c_re
            gr_ref[rows, :] = g_re
            gi_ref[rows, :] = g_im
            return g_re, g_im

        last = _s5_tile(seg - 1)
        m_re, m_im = lax.fori_loop(1, seg, local, (gr_ref[last, :], gi_ref[last, :]))
        top_re, top_im = pwr_ref[seg - 1:seg, :], -pwi_ref[seg - 1:seg, :]
        r_re, r_im = cr_ref[...], ci_ref[...]
        for j in reversed(range(SUBLANES)):
            rtr_ref[j:j + 1, :] = r_re
            rti_ref[j:j + 1, :] = r_im
            n_re, n_im = _cmul(top_re, top_im, r_re, r_im)
            r_re, r_im = n_re + m_re[j:j + 1, :], n_im + m_im[j:j + 1, :]
        cr_ref[...] = r_re
        ci_ref[...] = r_im
        f_re, f_im = rtr_ref[...], rti_ref[...]
        keep = jnp.where(first_chunk, 0.0, 1.0)
        sub = lax.broadcasted_iota(jnp.int32, (SUBLANES, SB), 0)
        before_re = jnp.where(sub == 0, hr_ref[SUBLANES - 1:SUBLANES, :] * keep, pltpu.roll(sr_ref[last, :], 1, 0))
        before_im = jnp.where(sub == 0, hi_ref[SUBLANES - 1:SUBLANES, :] * keep, pltpu.roll(si_ref[last, :], 1, 0))

        def fix(i, p_re, p_im, acc):
            rows = _s5_tile(i)
            k = seg - 1 - i
            c_re, c_im = _cmul(pwr_ref[pl.ds(k, 1), :], -pwi_ref[pl.ds(k, 1), :], f_re, f_im)
            g_re = gr_ref[rows, :] + c_re
            g_im = gi_ref[rows, :] + c_im
            gr_ref[rows, :] = g_re
            gi_ref[rows, :] = g_im
            return acc[0] + p_re * g_re + p_im * g_im, acc[1] + p_re * g_im - p_im * g_re

        zero = jnp.zeros((SUBLANES, SB), F32)
        acc = fix(0, before_re, before_im, (zero, zero))
        acc = lax.fori_loop(
            1, seg, lambda i, acc: fix(i, sr_ref[_s5_tile(i - 1), :], si_ref[_s5_tile(i - 1), :], acc), acc)
        dar_ref[...] += _colsum(acc[0])
        dai_ref[...] += _colsum(acc[1])
        gsr = gr_ref[...].astype(BF16)
        gsi = gi_ref[...].astype(BF16)
        ub = u.astype(BF16)
        dbdr_ref[...] += _dot_tn(ub, gsr)
        dbdi_ref[...] += _dot_tn(ub, gsi)
        dup_ref[...] = _dot_nt(gsr, bdr_ref[...]) + _dot_nt(gsi, bdi_ref[...]) + d_ref[...] * dy
        for i in range(seg):
            du_ref[pl.ds(i, SUBLANES, stride=seg), :] = dup_ref[_s5_tile(i), :]
        dd_ref[...] += _colsum(dy * u)

    def rev(t):
        return nt - 1 - t

    def halo(j, t):
        return (jnp.maximum(rev(t) * halo_blocks - 1, 0), j)

    ublk = pl.BlockSpec((tc, UB), lambda j, t: (rev(t), j))
    sblk = pl.BlockSpec((tc, SB), lambda j, t: (rev(t), j))
    bd_spec = pl.BlockSpec((None, UB, SB), lambda j, t: (j, 0, 0))
    cd_spec = pl.BlockSpec((None, SB, UB), lambda j, t: (j, 0, 0))
    svec = pl.BlockSpec((1, SB), lambda j, t: (0, j))
    uvec = pl.BlockSpec((1, UB), lambda j, t: (0, j))
    sds = jax.ShapeDtypeStruct
    return pl.pallas_call(
        body, name="s5_bwd", grid=(GB, nt),
        in_specs=[ublk, ublk, sblk, sblk, pl.BlockSpec((SUBLANES, SB), halo), pl.BlockSpec((SUBLANES, SB), halo),
                  bd_spec, bd_spec, cd_spec, cd_spec, svec, svec, uvec],
        out_specs=[ublk, bd_spec, bd_spec, cd_spec, cd_spec, svec, svec, uvec],
        out_shape=[sds((T, GB * UB), F32), sds((GB, UB, SB), F32), sds((GB, UB, SB), F32),
                   sds((GB, SB, UB), F32), sds((GB, SB, UB), F32), sds((1, GB * SB), F32),
                   sds((1, GB * SB), F32), sds((1, GB * UB), F32)],
        scratch_shapes=[pltpu.VMEM((tc, SB), F32), pltpu.VMEM((tc, SB), F32),
                        pltpu.VMEM((seg, SB), F32), pltpu.VMEM((seg, SB), F32),
                        pltpu.VMEM((SUBLANES, SB), F32), pltpu.VMEM((SUBLANES, SB), F32),
                        pltpu.VMEM((1, SB), F32), pltpu.VMEM((1, SB), F32),
                        pltpu.VMEM((tc, UB), F32), pltpu.VMEM((tc, UB), F32), pltpu.VMEM((tc, UB), F32)],
        compiler_params=_params("arbitrary", "arbitrary"),
    )(dy, proj, sr, si, sr, si, bdr, bdi, cdr, cdi, ar, ai, dvec)


def _shift_down(v, k):
    rows = lax.broadcasted_iota(jnp.int32, v.shape, 0)
    return jnp.where(rows >= k, pltpu.roll(v, k, 0), 0.0)


def _shift_up(v, k):
    n = v.shape[0]
    rows = lax.broadcasted_iota(jnp.int32, v.shape, 0)
    return jnp.where(rows < n - k, pltpu.roll(v, n - k, 0), 0.0)


def _conv_specs(T, cb, n_s5_blocks, n_conv_blocks):
    gb = pl.BlockSpec((T, cb), lambda j: (0, n_s5_blocks + j))
    gc = pl.BlockSpec((T, cb), lambda j: (0, n_s5_blocks + n_conv_blocks + j))
    hh = pl.BlockSpec((T, cb), lambda j: (0, n_s5_blocks + 2 * n_conv_blocks + j))
    return gb, gc, hh


def _conv_fwd(proj, cw, cbias, d_s5, d_conv):
    T = proj.shape[0]
    cb = _tile(d_conv, 256, LANES)

    def body(gb_ref, gc_ref, hh_ref, w_ref, b_ref, z_ref):
        v = gc_ref[...] * hh_ref[...]
        w = w_ref[...]
        cv = b_ref[...] + w[0:1, :] * _shift_down(v, 2) + w[1:2, :] * _shift_down(v, 1) + w[2:3, :] * v
        z_ref[...] = gb_ref[...] * cv

    gb, gc, hh = _conv_specs(T, cb, d_s5 // cb, d_conv // cb)
    col = pl.BlockSpec((T, cb), lambda j: (0, j))
    return pl.pallas_call(
        body, name="conv_fwd", grid=(d_conv // cb,),
        in_specs=[gb, gc, hh, pl.BlockSpec((CONV_W, cb), lambda j: (0, j)), pl.BlockSpec((1, cb), lambda j: (0, j))],
        out_specs=col, out_shape=jax.ShapeDtypeStruct((T, d_conv), F32),
        compiler_params=_params("arbitrary"),
    )(proj, proj, proj, cw, cbias)


def _conv_bwd(dz, proj, cw, cbias, d_s5, d_conv):
    T = proj.shape[0]
    cb = _tile(d_conv, 256, LANES)

    def body(dz_ref, gb_ref, gc_ref, hh_ref, w_ref, b_ref, dgb_ref, dgc_ref, dhh_ref, dw_ref, db_ref):
        gc = gc_ref[...]
        hh = hh_ref[...]
        dz = dz_ref[...]
        w = w_ref[...]
        v = gc * hh
        v1 = _shift_down(v, 1)
        v2 = _shift_down(v, 2)
        cv = b_ref[...] + w[0:1, :] * v2 + w[1:2, :] * v1 + w[2:3, :] * v
        dgb_ref[...] = (dz * cv).astype(BF16)
        dcv = dz * gb_ref[...]
        dv = w[2:3, :] * dcv + w[1:2, :] * _shift_up(dcv, 1) + w[0:1, :] * _shift_up(dcv, 2)
        dgc_ref[...] = (dv * hh).astype(BF16)
        dhh_ref[...] = (dv * gc).astype(BF16)
        dw_ref[0:1, :] = _colsum(dcv * v2)
        dw_ref[1:2, :] = _colsum(dcv * v1)
        dw_ref[2:3, :] = _colsum(dcv * v)
        db_ref[...] = _colsum(dcv)

    gb, gc, hh = _conv_specs(T, cb, d_s5 // cb, d_conv // cb)
    col = pl.BlockSpec((T, cb), lambda j: (0, j))
    wspec = pl.BlockSpec((CONV_W, cb), lambda j: (0, j))
    bspec = pl.BlockSpec((1, cb), lambda j: (0, j))
    sds = jax.ShapeDtypeStruct
    return pl.pallas_call(
        body, name="conv_bwd", grid=(d_conv // cb,),
        in_specs=[col, gb, gc, hh, wspec, bspec],
        out_specs=[col, col, col, wspec, bspec],
        out_shape=[sds((T, d_conv), BF16), sds((T, d_conv), BF16), sds((T, d_conv), BF16),
                   sds((CONV_W, d_conv), F32), sds((1, d_conv), F32)],
        compiler_params=_params("arbitrary"),
    )(dz, proj, proj, proj, cw, cbias)


def _rms(v, g):
    rstd = lax.rsqrt(_rowmean(v * v) + RMS_EPS)
    return v * rstd * g, rstd


def _rms_bwd(dyn, v, rstd, g):
    w = dyn * g
    return rstd * w - v * (rstd * rstd * rstd) * _rowmean(w * v), _colsum(dyn * v * rstd)


def _mix_post(y, z, wglu, g_s5, g_conv):
    T, C = y.shape
    tm = _tile(T, 256, 16)

    def body(y_ref, z_ref, w_ref, gs_ref, gc_ref, m_ref, gl_ref):
        ge = jax.nn.gelu(y_ref[...])
        gl = _dot_nn(ge.astype(BF16), w_ref[...])
        gl_ref[...] = gl
        yn, _ = _rms(ge * jax.nn.sigmoid(gl), gs_ref[...])
        zn, _ = _rms(z_ref[...], gc_ref[...])
        m_ref[:, 0:C] = yn.astype(BF16)
        m_ref[:, C:2 * C] = zn.astype(BF16)

    row = pl.BlockSpec((tm, C), lambda i: (i, 0))
    vec = pl.BlockSpec((1, C), lambda i: (0, 0))
    return pl.pallas_call(
        body, name="mix_post", grid=(T // tm,),
        in_specs=[row, row, pl.BlockSpec((C, C), lambda i: (0, 0)), vec, vec],
        out_specs=[pl.BlockSpec((tm, 2 * C), lambda i: (i, 0)), row],
        out_shape=[jax.ShapeDtypeStruct((T, 2 * C), BF16), jax.ShapeDtypeStruct((T, C), F32)],
        compiler_params=_params("arbitrary"),
    )(y, z, wglu, g_s5, g_conv)


def _mix_post_bwd(dm, y, gl, z, wglu, g_s5, g_conv):
    T, C = y.shape
    tm = _tile(T, 256, 16)

    def body(dm_ref, y_ref, gl_ref, z_ref, w_ref, gs_ref, gc_ref, dy_ref, dz_ref, dw_ref, dgs_ref, dgc_ref):
        @pl.when(pl.program_id(0) == 0)
        def _():
            dw_ref[...] = jnp.zeros_like(dw_ref)
            dgs_ref[...] = jnp.zeros_like(dgs_ref)
            dgc_ref[...] = jnp.zeros_like(dgc_ref)

        yv = y_ref[...]
        ge, gelu_vjp = jax.vjp(jax.nn.gelu, yv)
        gl = gl_ref[...]
        sg = jax.nn.sigmoid(gl)
        y2 = ge * sg
        _, rstd_y = _rms(y2, gs_ref[...])
        dy2, dgs = _rms_bwd(dm_ref[:, 0:C], y2, rstd_y, gs_ref[...])
        dgs_ref[...] += dgs
        dgl = (dy2 * ge * sg * (1.0 - sg)).astype(BF16)
        dge = dy2 * sg + _dot_nt(dgl, w_ref[...])
        dw_ref[...] += _dot_tn(ge.astype(BF16), dgl)
        dy_ref[...] = gelu_vjp(dge)[0]
        zv = z_ref[...]
        _, rstd_z = _rms(zv, gc_ref[...])
        dz, dgc = _rms_bwd(dm_ref[:, C:2 * C], zv, rstd_z, gc_ref[...])
        dz_ref[...] = dz
        dgc_ref[...] += dgc

    row = pl.BlockSpec((tm, C), lambda i: (i, 0))
    vec = pl.BlockSpec((1, C), lambda i: (0, 0))
    full = pl.BlockSpec((C, C), lambda i: (0, 0))
    sds = jax.ShapeDtypeStruct
    return pl.pallas_call(
        body, name="mix_post_bwd", grid=(T // tm,),
        in_specs=[pl.BlockSpec((tm, 2 * C), lambda i: (i, 0)), row, row, row, full, vec, vec],
        out_specs=[row, row, full, vec, vec],
        out_shape=[sds((T, C), F32), sds((T, C), F32), sds((C, C), F32), sds((1, C), F32), sds((1, C), F32)],
        compiler_params=_params("arbitrary"),
    )(dm, y, gl, z, wglu, g_s5, g_conv)


def _adamw(w, g, m, v):
    m = ADAM_B1 * m + (1.0 - ADAM_B1) * g
    v = ADAM_B2 * v + (1.0 - ADAM_B2) * (g * g)
    m_hat = m / (1.0 - ADAM_B1 ** ADAM_STEP)
    v_hat = v / (1.0 - ADAM_B2 ** ADAM_STEP)
    return -ADAM_LR * (m_hat / (jnp.sqrt(v_hat) + ADAM_EPS) + ADAM_WD * w), m, v


def _sum_parts(p_ref):
    total = p_ref[0].astype(F32)
    for d in range(1, p_ref.shape[0]):
        total = total + p_ref[d].astype(F32)
    return total


def _row_tile(R, C, n_streams):
    budget = VMEM_LIMIT // 3 // (n_streams * C * 4)
    return _tile(R, max(BF16_ROWS, budget), BF16_ROWS)


def _reduce_parts(parts):
    L, P, R, C = parts.shape
    tr = _row_tile(R, C, P + 1)

    def body(p_ref, o_ref):
        o_ref[...] = _sum_parts(p_ref)

    return pl.pallas_call(
        body, name="reduce_parts", grid=(L, R // tr),
        in_specs=[pl.BlockSpec((None, P, tr, C), lambda l, i: (l, 0, i, 0))],
        out_specs=pl.BlockSpec((None, tr, C), lambda l, i: (l, i, 0)),
        out_shape=jax.ShapeDtypeStruct((L, R, C), F32),
        compiler_params=_params("arbitrary", "arbitrary"),
    )(parts)


def _adamw_update(w, m, v, grad=None, parts=None):
    L, R, C = w.shape
    from_parts = parts is not None
    P = parts.shape[1] if from_parts else 1
    tr = _row_tile(R, C, P + 7)

    def body(g_in_ref, w_ref, m_ref, v_ref, g_ref, d_ref, nm_ref, nv_ref):
        g = _sum_parts(g_in_ref) if from_parts else g_in_ref[...]
        delta, nm, nv = _adamw(w_ref[...], g, m_ref[...], v_ref[...])
        g_ref[...] = g
        d_ref[...] = delta
        nm_ref[...] = nm
        nv_ref[...] = nv

    blk = pl.BlockSpec((None, tr, C), lambda l, i: (l, i, 0))
    g_spec = pl.BlockSpec((None, P, tr, C), lambda l, i: (l, 0, i, 0)) if from_parts else blk
    out = jax.ShapeDtypeStruct((L, R, C), F32)
    return pl.pallas_call(
        body, name="adamw_parts" if from_parts else "adamw", grid=(L, R // tr),
        in_specs=[g_spec, blk, blk, blk], out_specs=[blk, blk, blk, blk], out_shape=[out, out, out, out],
        compiler_params=_params("arbitrary", "arbitrary"),
    )(parts if from_parts else grad, w, m, v)


def _pair_sum(full, stage):
    _, R, C = full.shape
    tr = _row_tile(R, C, 4)

    def body(f_ref, s_ref, o_ref):
        mine = f_ref[lax.axis_index("c")]
        o_ref[...] = (mine.astype(F32) + s_ref[...].astype(F32)).astype(BF16)

    blk = pl.BlockSpec((None, tr, C), lambda q, i: (q, i, 0))
    return pl.pallas_call(
        body, name="pair_sum", grid=(N_CHIPS, R // tr),
        in_specs=[pl.BlockSpec((None, 2, tr, C), lambda q, i: (q, 0, i, 0)), blk],
        out_specs=blk, out_shape=jax.ShapeDtypeStruct((N_CHIPS, R, C), BF16),
        compiler_params=_params("arbitrary", "arbitrary"),
    )(full.reshape(N_CHIPS, 2, R, C), stage)


def _me():
    x, y, c = (lax.axis_index(a) for a in AXES)
    return x, y, c, 4 * x + 2 * y + c


def _peer(rel):
    x, y, c, _ = _me()
    px = 1 - x if rel & 4 else x
    py = 1 - y if rel & 2 else y
    pc = 1 - c if rel & 1 else c
    return (px, py, pc), 4 * px + 2 * py + pc


DATAFLOW = pltpu.SideEffectType.DATAFLOW_SIDE_EFFECTING


def _remote_copy(src, dst, sems):
    return functools.partial(pltpu.make_async_remote_copy, src_ref=src, dst_ref=dst, **sems)


ALL_PEERS = tuple(range(1, N_DEV))
SIBLING = 1
OTHER_CHIPS = (2, 4, 6)
SIBLING_AND_OTHER_CHIPS = (SIBLING,) + OTHER_CHIPS


def _slot(dev, blk, same_core, own_core_last):
    if not own_core_last:
        return blk
    return (N_CHIPS if same_core else 0) + 2 * dev[0] + dev[1]


def _gather_copies(n, rels=ALL_PEERS, own_core_last=False):
    def copies(srcs, lands, send_sems, recv_sems, local_sems):
        x, y, c, me = _me()
        local, remote = [], []
        for k in range(n):
            local.append(functools.partial(pltpu.make_async_copy, srcs[k],
                                           lands[k].at[_slot((x, y, c), me, True, own_core_last)], local_sems.at[k]))
            for rel in rels:
                dev, blk = _peer(rel)
                same_core = not rel & SIBLING
                sems = dict(send_sem=send_sems.at[_sem_index(k, rel)], recv_sem=recv_sems.at[_sem_index(k, rel)],
                            device_id=dev, device_id_type=MESH)
                remote.append((_remote_copy(srcs[k], lands[k].at[_slot((x, y, c), me, same_core, own_core_last)], sems),
                               _remote_copy(srcs[k], lands[k].at[_slot(dev, blk, same_core, own_core_last)], sems)))
        return local, remote

    copies.n_arrays = n
    return copies


Y_NEIGHBOUR, X_NEIGHBOUR, DIAGONAL = 2, 4, 6
SIBLING_AND_NEIGHBOURS = (SIBLING, Y_NEIGHBOUR, X_NEIGHBOUR)


def _relay_copies(n, own_core_last=False):
    def copies(srcs, lands, send_sems, recv_sems, local_sems):
        diagonal = _slot(*_peer(DIAGONAL), True, own_core_last)
        remote = []
        for k in range(n):
            half = lands[k].shape[1] // 2
            for to, held, rows in ((X_NEIGHBOUR, Y_NEIGHBOUR, pl.ds(0, half)), (Y_NEIGHBOUR, X_NEIGHBOUR, pl.ds(half, half))):
                block = lands[k].at[_slot(*_peer(held), True, own_core_last), rows]
                sems = dict(send_sem=send_sems.at[_sem_index(k, to)], recv_sem=recv_sems.at[_sem_index(k, to)],
                            device_id=_peer(to)[0], device_id_type=MESH)
                remote.append((_remote_copy(block, block, sems), _remote_copy(block, lands[k].at[diagonal, rows], sems)))
        return [], remote

    copies.n_arrays = n
    return copies


def _forward_copies(n, own_core_last=False):
    def copies(srcs, lands, send_sems, recv_sems, local_sems):
        sibling = _peer(SIBLING)[0]
        remote = []
        for k in range(n):
            for rel in OTHER_CHIPS:
                dev, blk = _peer(rel)
                have = _slot(dev, blk, True, own_core_last)
                there = _slot(dev, blk, False, own_core_last)
                other = _peer(rel | SIBLING)
                comes = _slot(other[0], other[1], False, own_core_last)
                sems = dict(send_sem=send_sems.at[_sem_index(k, rel)], recv_sem=recv_sems.at[_sem_index(k, rel)],
                            device_id=sibling, device_id_type=MESH)
                remote.append((_remote_copy(lands[k].at[have], lands[k].at[there], sems),
                               _remote_copy(lands[k].at[have], lands[k].at[comes], sems)))
        return [], remote

    copies.n_arrays = n
    return copies


def _scatter_copies(n, layer):
    def copies(srcs, lands, send_sems, recv_sems, local_sems):
        me = _me()[3]
        local, remote = [], []
        for k in range(n):
            local.append(functools.partial(pltpu.make_async_copy, srcs[k].at[me], lands[k].at[layer, me],
                                           local_sems.at[k]))
            for rel in range(1, N_DEV):
                dev, blk = _peer(rel)
                sems = dict(send_sem=send_sems.at[_sem_index(k, rel)], recv_sem=recv_sems.at[_sem_index(k, rel)],
                            device_id=dev, device_id_type=MESH)
                remote.append((_remote_copy(srcs[k].at[blk], lands[k].at[layer, me], sems),
                               _remote_copy(srcs[k].at[blk], lands[k].at[layer, blk], sems)))
        return local, remote

    copies.n_arrays = n
    return copies


def _swap_copies(n):
    def copies(srcs, lands, send_sems, recv_sems, local_sems):
        sibling = _peer(SIBLING)[0]
        remote = []
        for k in range(n):
            sems = dict(send_sem=send_sems.at[_sem_index(k, SIBLING)], recv_sem=recv_sems.at[_sem_index(k, SIBLING)],
                        device_id=sibling, device_id_type=MESH)
            remote.append((_remote_copy(srcs[k], lands[k], sems), _remote_copy(srcs[k], lands[k], sems)))
        return [], remote

    copies.n_arrays = n
    return copies


def _pair_copies(n):
    def copies(srcs, lands, send_sems, recv_sems, local_sems):
        c = _me()[2]
        sibling = _peer(SIBLING)[0]
        remote = []
        for k in range(n):
            for chip in range(N_CHIPS):
                sems = dict(send_sem=send_sems.at[_sem_index(k, chip + 1)], recv_sem=recv_sems.at[_sem_index(k, chip + 1)],
                            device_id=sibling, device_id_type=MESH)
                block = srcs[k].at[2 * chip + 1 - c]
                remote.append((_remote_copy(block, lands[k].at[chip], sems), _remote_copy(block, lands[k].at[chip], sems)))
        return [], remote

    copies.n_arrays = n
    return copies


def _chip_scatter_copies(n, layer):
    def copies(srcs, lands, send_sems, recv_sems, local_sems):
        x, y, _, _ = _me()
        my_chip = 2 * x + y
        local, remote = [], []
        for k in range(n):
            local.append(functools.partial(pltpu.make_async_copy, srcs[k].at[my_chip], lands[k].at[layer, my_chip],
                                           local_sems.at[k]))
            for rel in OTHER_CHIPS:
                dev = _peer(rel)[0]
                chip = 2 * dev[0] + dev[1]
                sems = dict(send_sem=send_sems.at[_sem_index(k, rel)], recv_sem=recv_sems.at[_sem_index(k, rel)],
                            device_id=dev, device_id_type=MESH)
                remote.append((_remote_copy(srcs[k].at[chip], lands[k].at[layer, my_chip], sems),
                               _remote_copy(srcs[k].at[chip], lands[k].at[layer, chip], sems)))
        return local, remote

    copies.n_arrays = n
    return copies


def _sem_shapes(n):
    return [pltpu.SemaphoreType.DMA((n * (N_DEV - 1),)), pltpu.SemaphoreType.DMA((n * (N_DEV - 1),)),
            pltpu.SemaphoreType.DMA((n,))]


def _sem_index(k, rel):
    return k * (N_DEV - 1) + rel - 1


def _exchange(copies, name, srcs, lands):
    n_src, n_land = len(srcs), len(lands)

    def body(*refs):
        src_refs = refs[:n_src]
        land_refs = refs[n_src + n_land:n_src + 2 * n_land]
        local, remote = copies(src_refs, land_refs, *refs[n_src + 2 * n_land:])
        local = [cp() for cp in local]
        sends = [send() for send, _ in remote]
        for cp in local + sends:
            cp.start()
        for send, (_, landing) in zip(sends, remote):
            send.wait_send()
            landing().wait_recv()
        for cp in local:
            cp.wait()

    return pl.pallas_call(
        body, name=name, in_specs=[HBM_SPEC] * (n_src + n_land), out_specs=[HBM_SPEC] * n_land,
        out_shape=[jax.ShapeDtypeStruct(b.shape, b.dtype) for b in lands],
        scratch_shapes=_sem_shapes(copies.n_arrays),
        input_output_aliases={n_src + k: k for k in range(n_land)},
        compiler_params=pltpu.CompilerParams(has_side_effects=True),
    )(*srcs, *lands)


def _hbm(arrays):
    return [pltpu.with_memory_space_constraint(a, pltpu.HBM) for a in arrays]


def _exchange_start(copies, name, srcs, lands, after):
    n_src, n_land, n_after = len(srcs), len(lands), len(after)
    n_data = n_src + n_land

    def body(*refs):
        outs = refs[n_data + n_after:]
        local, remote = copies(refs[:n_src], refs[n_src:n_data], *outs[:3])
        for cp in local:
            cp().start()
        for send, _ in remote:
            send().start()
        outs[-1][...] = jnp.zeros_like(outs[-1])

    res = pl.pallas_call(
        body, name=name, in_specs=[HBM_SPEC] * n_data + [ANY_SPEC] * n_after,
        out_specs=[SEM_SPEC] * 3 + [HBM_SPEC] * n_data + [pl.BlockSpec(memory_space=pltpu.VMEM)],
        out_shape=_sem_shapes(copies.n_arrays) + [pltpu.HBM(a.shape, a.dtype) for a in list(srcs) + list(lands)]
        + [jax.ShapeDtypeStruct((SUBLANES, LANES), F32)],
        input_output_aliases={k: 3 + k for k in range(n_data)},
        compiler_params=pltpu.CompilerParams(has_side_effects=DATAFLOW),
    )(*_hbm(list(srcs) + list(lands)), *after)
    return res[:3], res[3:3 + n_src], res[3 + n_src:3 + n_data], res[-1]


def _exchange_wait(copies, name, sems, srcs, lands, after):
    n_src, n_land, n_after = len(srcs), len(lands), len(after)
    n_data = n_src + n_land

    def body(*refs):
        local, remote = copies(refs[:n_src], refs[n_src:n_data], *refs[n_data:n_data + 3])
        for send, landing in remote:
            send().wait_send()
            landing().wait_recv()
        for cp in local:
            cp().wait()

    res = pl.pallas_call(
        body, name=name, in_specs=[HBM_SPEC] * n_data + [SEM_SPEC] * 3 + [ANY_SPEC] * n_after,
        out_specs=[HBM_SPEC] * n_data,
        out_shape=[pltpu.HBM(a.shape, a.dtype) for a in list(srcs) + list(lands)],
        input_output_aliases={k: k for k in range(n_data)},
        compiler_params=pltpu.CompilerParams(has_side_effects=DATAFLOW),
    )(*srcs, *lands, *sems, *after)
    return res[:n_src], res[n_src:]


def _block_diag(blocks, row_major):
    L, GB, g, P, N = blocks.shape
    eye = jnp.eye(g, dtype=blocks.dtype)
    if row_major:
        return jnp.einsum("lbgpn,gh->lbgphn", blocks, eye).reshape(L, GB, g * P, g * N)
    return jnp.einsum("lbgpn,gh->lbhngp", blocks, eye).reshape(L, GB, g * N, g * P)


def _diag_blocks(mat, g, P, N, row_major):
    GB = mat.shape[0]
    eye = jnp.eye(g, dtype=mat.dtype)
    if row_major:
        return jnp.einsum("bgphn,gh->bgpn", mat.reshape(GB, g, P, g, N), eye)
    return jnp.einsum("bhngp,gh->bgpn", mat.reshape(GB, g, N, g, P), eye)


def _pack(arrays, rows_multiple):
    flat = jnp.concatenate([a.reshape(-1).astype(F32) for a in arrays])
    pad = (-flat.shape[0]) % (rows_multiple * LANES)
    return jnp.pad(flat, (0, pad)).reshape(-1, LANES)


def _unpack(packed, shapes):
    flat = packed.reshape(-1)
    out, pos = [], 0
    for s in shapes:
        n = math.prod(s)
        out.append(flat[pos:pos + n].reshape(s))
        pos += n
    return out


SMALL = ["ln1_g", "ln1_b", "s5_lam_re", "s5_lam_im", "s5_log_dt", "s5_b_re", "s5_b_im", "s5_c_re", "s5_c_im", "s5_d",
         "conv_b", "g_s5", "g_conv", "ln2_g", "ln2_b", "ln3_g", "ln3_b"]
S5_B = ["s5_b_re", "s5_b_im"]
WEIGHTS = ["ffn1_gate", "ffn1_up", "ffn1_down", "ln1_g", "ln1_b", "w_in", "s5_lam_re", "s5_lam_im", "s5_log_dt",
           "s5_b_re", "s5_b_im", "s5_c_re", "s5_c_im", "s5_d", "s5_w_glu", "conv_w", "conv_b", "g_s5", "g_conv",
           "w_out", "ln2_g", "ln2_b", "ffn2_gate", "ffn2_up", "ffn2_down", "ln3_g", "ln3_b"]
TRANSPOSED = ["ffn1_gate", "ffn1_up", "w_in", "ffn2_gate", "ffn2_up"]
UPDATED_TRANSPOSED = ["ffn1_gate", "ffn1_up", "ffn2_gate", "ffn2_up"]
GROUPS = {"a": ["ffn1_gate", "ffn1_up", "ffn1_down"], "b": ["w_in", "s5_w_glu", "w_out"],
          "c": ["ffn2_gate", "ffn2_up", "ffn2_down"]}
FFN_GROUPS = ("a", "c")


def _train_step(x, target, w, m, v):
    T, D = x.shape
    L = w["ln1_g"].shape[0]
    alpha = (2.0 * L) ** 0.25
    G = w["s5_log_dt"].shape[1]
    d_s5 = G * S5_P
    d_conv = w["conv_b"].shape[1]
    GB = G // S5_GROUPS_PER_BLOCK
    me = _me()[3]

    def shard(n, l):
        return (jnp.swapaxes(w[n][l], 0, 1) if n in TRANSPOSED else w[n][l]).astype(BF16)

    conv_w_rows = jnp.pad(w["conv_w"], ((0, 0), (0, SUBLANES - CONV_W), (0, 0)))
    (conv_w_all,) = _exchange(_gather_copies(1), "gather_conv_w", [conv_w_rows],
                              [lax.empty((N_DEV,) + conv_w_rows.shape, F32)])
    parts = [(l, grp) for l in range(L) for grp in GROUPS]
    step_one, relayed, step_two = {}, {}, {}

    def start_one(i, after):
        if i >= len(parts):
            return []
        l, grp = parts[i]
        srcs = [shard(n, l) for n in GROUPS[grp]]
        lands = [lax.empty((N_DEV,) + a.shape, a.dtype) for a in srcs]
        sems, srcs, lands, tok = _exchange_start(_gather_copies(len(srcs), SIBLING_AND_NEIGHBOURS, grp in FFN_GROUPS),
                                                 f"gather_start_{l}{grp}", srcs, lands, after)
        step_one[l, grp] = (sems, srcs, lands)
        return [tok]

    def relay_on(i, after):
        if i >= len(parts):
            return []
        l, grp = parts[i]
        sems, srcs, lands = step_one[l, grp]
        copies = _gather_copies(len(srcs), SIBLING_AND_NEIGHBOURS, grp in FFN_GROUPS)
        _, lands = _exchange_wait(copies, f"gather_wait_{l}{grp}", sems, srcs, lands, after)
        sems, _, lands, tok = _exchange_start(_relay_copies(len(lands), grp in FFN_GROUPS), f"relay_start_{l}{grp}",
                                              [], lands, [])
        relayed[l, grp] = (sems, lands)
        return [tok]

    def forward_on(i, after):
        if i >= len(parts):
            return []
        l, grp = parts[i]
        sems, lands = relayed[l, grp]
        _, lands = _exchange_wait(_relay_copies(len(lands), grp in FFN_GROUPS), f"relay_wait_{l}{grp}", sems, [], lands,
                                  after)
        sems, _, lands, tok = _exchange_start(_forward_copies(len(lands), grp in FFN_GROUPS), f"forward_start_{l}{grp}",
                                              [], lands, [])
        step_two[l, grp] = (sems, lands)
        return [tok]

    def gathered(i, after):
        l, grp = parts[i]
        sems, lands = step_two[l, grp]
        _, lands = _exchange_wait(_forward_copies(len(lands), grp in FFN_GROUPS), f"forward_wait_{l}{grp}", sems, [],
                                  lands, after)
        full = {n: p.reshape(-1, p.shape[-1]) for n, p in zip(GROUPS[grp], lands)}
        if grp == "b":
            full["conv_w"] = jnp.swapaxes(conv_w_all[:, l, :CONV_W, :], 0, 1).reshape(CONV_W, d_conv)
        return full

    lre = w["s5_lam_re"].reshape(L * G, S5_N)
    lim = w["s5_lam_im"].reshape(L * G, S5_N)
    ldt = w["s5_log_dt"].reshape(L * G, 1)
    b_re = jnp.transpose(w["s5_b_re"], (3, 0, 1, 2)).reshape(S5_P, L * G, S5_N)
    b_im = jnp.transpose(w["s5_b_im"], (3, 0, 1, 2)).reshape(S5_P, L * G, S5_N)
    ab_re, ab_im, bb_re, bb_im = _s5_params_fwd(lre, lim, ldt, b_re, b_im)

    def groups(bb):
        return jnp.transpose(bb.reshape(S5_P, L, GB, S5_GROUPS_PER_BLOCK, S5_N), (1, 2, 3, 0, 4))

    bd_re = _block_diag(groups(bb_re), True).astype(BF16)
    bd_im = _block_diag(groups(bb_im), True).astype(BF16)
    c_shape = (L, GB, S5_GROUPS_PER_BLOCK, S5_P, S5_N)
    cd_re = _block_diag(w["s5_c_re"].reshape(c_shape), False).astype(BF16)
    cd_im = _block_diag(w["s5_c_im"].reshape(c_shape), False).astype(BF16)
    a_re = ab_re.reshape(L, 1, G * S5_N)
    a_im = ab_im.reshape(L, 1, G * S5_N)
    d_vec = w["s5_d"].reshape(L, 1, d_s5)

    def vec(name, l):
        return w[name][l].reshape(1, -1)

    saved, weights = [], []
    x_in, x_in_b = x, x.astype(BF16)
    def advance(i, after):
        tokens = forward_on(i + 1, after)
        tokens = tokens + relay_on(i + 2, tokens or after)
        return tokens + start_one(i + 3, tokens or after)

    token = start_one(0, [])
    token = relay_on(0, token)
    token = start_one(1, token)
    token = start_one(2, token)
    token = relay_on(1, token)
    token = forward_on(0, token)
    for l in range(L):
        i = len(GROUPS) * l
        gw = gathered(i, [x_in] if l else token)
        s = {"x0b": x_in_b}
        s["g1"], s["u1"], s["h1"] = _ffn_up(x_in_b, gw["ffn1_gate"], gw["ffn1_up"])
        x1, s["x1b"], s["xh1"], s["rstd1"] = _mm_res_ln(s["h1"], gw["ffn1_down"], x_in, vec("ln1_g", l),
                                                         vec("ln1_b", l), 0.5, alpha, advance(i, [s["h1"]]))
        gw.update(gathered(i + 1, [s["x1b"]]))
        s["proj"] = _mm_nt(s["x1b"], gw["w_in"])
        s["y"], s["sr"], s["si"] = _s5_fwd(s["proj"], bd_re[l], bd_im[l], cd_re[l], cd_im[l], a_re[l], a_im[l],
                                           d_vec[l])
        s["z"] = _conv_fwd(s["proj"], gw["conv_w"], vec("conv_b", l), d_s5, d_conv)
        s["mcat"], s["gl"] = _mix_post(s["y"], s["z"], gw["s5_w_glu"], vec("g_s5", l), vec("g_conv", l))
        x2, s["x2b"], s["xh2"], s["rstd2"] = _mm_res_ln(s["mcat"], gw["w_out"], x1, vec("ln2_g", l),
                                                         vec("ln2_b", l), 1.0, alpha, advance(i + 1, [s["mcat"]]))
        gw.update(gathered(i + 2, [s["x2b"]]))
        s["g2"], s["u2"], s["h2"] = _ffn_up(s["x2b"], gw["ffn2_gate"], gw["ffn2_up"])
        x3, x3b, s["xh3"], s["rstd3"] = _mm_res_ln(s["h2"], gw["ffn2_down"], x2, vec("ln3_g", l), vec("ln3_b", l),
                                                   0.5, alpha, advance(i + 2, [s["h2"]]))
        saved.append(s)
        weights.append(gw)
        x_in, x_in_b = x3, x3b

    last = saved[L - 1]
    dr, drb, dg, db, loss = _loss_ln_bwd(x_in, target, last["xh3"], last["rstd3"], vec("ln3_g", L - 1))
    small = [dict() for _ in range(L)]
    small[L - 1]["ln3_g"], small[L - 1]["ln3_b"] = dg, db
    bufs = {grp: [lax.empty((L, N_CHIPS) + shard(n, 0).shape, BF16) for n in names] for grp, names in GROUPS.items()}
    scatters = {grp: [] for grp in GROUPS}
    grad_x = None

    def chip_scatter(l, grp, sums):
        sems, sums, bufs[grp], tok = _exchange_start(_chip_scatter_copies(len(sums), l), f"scatter_start_{l}{grp}", sums,
                                                     bufs[grp], [])
        scatters[grp].append((l, sems, sums))
        return [tok]

    small_names = SMALL + ["conv_w"]
    b_shape = (L, G, S5_P, S5_N)
    small_shapes = [b_shape if n in S5_B else w[n].shape for n in SMALL] + [(L, CONV_W, d_conv)]

    def small_scatter_start(small):
        def stack(key):
            return jnp.stack([small[l][key] for l in range(L)])

        d_bb_re = jnp.transpose(stack("d_bb_re").reshape(L * G, S5_P, S5_N), (1, 0, 2))
        d_bb_im = jnp.transpose(stack("d_bb_im").reshape(L * G, S5_P, S5_N), (1, 0, 2))
        g_lre, g_lim, g_ldt, g_bre, g_bim = _s5_params_bwd(
            lre, lim, ldt, b_re, b_im, stack("d_ab_re").reshape(L * G, S5_N), stack("d_ab_im").reshape(L * G, S5_N),
            d_bb_re, d_bb_im)
        part = {n: [small[l][n] for l in range(L)]
                for n in ["ln1_g", "ln1_b", "s5_c_re", "s5_c_im", "s5_d", "conv_b", "g_s5", "g_conv", "ln2_g", "ln2_b",
                          "ln3_g", "ln3_b", "conv_w"]}
        part["s5_lam_re"], part["s5_lam_im"], part["s5_log_dt"] = [g_lre], [g_lim], [g_ldt]
        part["s5_b_re"] = [jnp.transpose(g_bre, (1, 0, 2))]
        part["s5_b_im"] = [jnp.transpose(g_bim, (1, 0, 2))]
        packed = _pack([piece for n in small_names for piece in part[n]], N_DEV * PACK_ROWS)
        rows = packed.shape[0] // N_DEV
        return _exchange_start(_scatter_copies(1, 0), "scatter_small_start", [packed.reshape(N_DEV, rows, LANES)],
                               [lax.empty((1, N_DEV, rows, LANES), F32)], [])

    token = []
    for l in reversed(range(L)):
        gw, s, sm = weights[l], saved[l], small[l]
        full = {}

        def ffn_bwd(dr, drb, tag, grp, xb_in, ln, after):
            dgp, dup = _ffn_down_bwd(drb, gw[f"ffn{tag}_down"], s[f"g{tag}"], s[f"u{tag}"], after)
            terms = {"gate": (dgp, xb_in, 1.0), "up": (dup, xb_in, 1.0), "down": (s[f"h{tag}"], drb, 0.5)}
            terms = [terms[n.split("_")[1]] for n in GROUPS[grp]]
            away = [_mm_tn(a, b, scale, BF16, half=0).reshape((N_CHIPS, -1, b.shape[1])) for a, b, scale in terms]
            stages = [lax.empty(a.shape, BF16) for a in away]
            sems, away, stages, tok = _exchange_start(_swap_copies(len(away)), f"pair_start_{l}{grp}", away, stages, [])
            res = _mm_dx([(dgp, gw[f"ffn{tag}_gate"]), (dup, gw[f"ffn{tag}_up"])], dr, alpha, ln, [tok])
            _, stages = _exchange_wait(_swap_copies(len(away)), f"pair_wait_{l}{grp}", sems, away, stages, [res[0]])
            sums = [_mm_tn(a, b, scale, BF16, half=1, addend=st.reshape(-1, b.shape[1])).reshape(st.shape)
                    for (a, b, scale), st in zip(terms, stages)]
            return res, chip_scatter(l, grp, sums)

        (dr, drb, sm["ln2_g"], sm["ln2_b"]), token = ffn_bwd(dr, drb, 2, "c", s["x2b"],
                                                              (s["xh2"], s["rstd2"], vec("ln2_g", l)), token)
        dm = _mm_nt(drb, gw["w_out"], token)
        full["w_out"] = _mm_tn(s["mcat"], drb, 1.0, BF16)
        dy, dz, dwglu, sm["g_s5"], sm["g_conv"] = _mix_post_bwd(dm, s["y"], s["gl"], s["z"], gw["s5_w_glu"],
                                                                vec("g_s5", l), vec("g_conv", l))
        full["s5_w_glu"] = dwglu.astype(BF16)
        du, dbd_re, dbd_im, dcd_re, dcd_im, sm["d_ab_re"], sm["d_ab_im"], sm["s5_d"] = _s5_bwd(
            dy, s["proj"], s["sr"], s["si"], bd_re[l], bd_im[l], cd_re[l], cd_im[l], a_re[l], a_im[l], d_vec[l])
        gsz = (S5_GROUPS_PER_BLOCK, S5_P, S5_N)
        sm["d_bb_re"] = _diag_blocks(dbd_re, *gsz, True)
        sm["d_bb_im"] = _diag_blocks(dbd_im, *gsz, True)
        sm["s5_c_re"] = _diag_blocks(dcd_re, *gsz, False).reshape(G, S5_P, S5_N)
        sm["s5_c_im"] = _diag_blocks(dcd_im, *gsz, False).reshape(G, S5_P, S5_N)
        dgb, dgc, dhh, sm["conv_w"], sm["conv_b"] = _conv_bwd(dz, s["proj"], gw["conv_w"], vec("conv_b", l),
                                                              d_s5, d_conv)
        dproj = jnp.concatenate([du.astype(BF16), dgb, dgc, dhh], axis=1)
        full["w_in"] = _mm_tn(dproj, s["x1b"], 1.0, BF16)
        fulls = [full[n].reshape((N_DEV, -1) + full[n].shape[1:]) for n in GROUPS["b"]]
        stages = [lax.empty((N_CHIPS,) + f.shape[1:], BF16) for f in fulls]
        sems, fulls, stages, tok = _exchange_start(_pair_copies(len(fulls)), f"pair_start_{l}b", fulls, stages, [])
        dr, drb, sm["ln1_g"], sm["ln1_b"] = _mm_dx([(dproj, gw["w_in"])], dr, alpha,
                                                   (s["xh1"], s["rstd1"], vec("ln1_g", l)), [tok])
        fulls, stages = _exchange_wait(_pair_copies(len(fulls)), f"pair_wait_{l}b", sems, fulls, stages, [dr])
        token = chip_scatter(l, "b", [_pair_sum(f, st) for f, st in zip(fulls, stages)])
        if l > 0:
            prev = saved[l - 1]
            (dr, drb, small[l - 1]["ln3_g"], small[l - 1]["ln3_b"]), token = ffn_bwd(
                dr, drb, 1, "a", s["x0b"], (prev["xh3"], prev["rstd3"], vec("ln3_g", l - 1)), token)
        else:
            small_sems, small_srcs, small_lands, tok = small_scatter_start(small)
            ((grad_x,), token) = ffn_bwd(dr, drb, 1, "a", s["x0b"], None, token + [tok])

    _, landed = _exchange_wait(_scatter_copies(1, 0), "scatter_small_wait", small_sems, small_srcs, small_lands,
                               [grad_x])
    mine = _reduce_parts(landed[0])
    (summed,) = _exchange(_gather_copies(1), "gather_small", [mine[0]], [lax.empty(landed[0].shape[1:], F32)])
    small_grads = dict(zip(small_names, _unpack(summed, small_shapes)))

    out = {}

    def update(name, w3, m3, v3, shape, **grad):
        res = _adamw_update(w3, m3, v3, **grad)
        out[name] = [r.reshape(shape) for r in res]
        return res

    cw_shape = w["conv_w"].shape
    g_cw = lax.dynamic_slice_in_dim(small_grads["conv_w"], me * cw_shape[2], cw_shape[2], axis=2)
    done = [update("conv_w", w["conv_w"], m["conv_w"], v["conv_w"], cw_shape, grad=g_cw)[3]]
    for n in SMALL:
        if n in S5_B:
            def view(a):
                return jnp.swapaxes(a, 2, 3).reshape(1, -1, S5_N)
            res = _adamw_update(view(w[n]), view(m[n]), view(v[n]), grad=small_grads[n].reshape(1, -1, S5_N))
            out[n] = [jnp.swapaxes(r.reshape(b_shape), 2, 3) for r in res]
        else:
            res = update(n, *(a[n].reshape(1, -1, a[n].shape[-1]) for a in (w, m, v)), w[n].shape,
                         grad=small_grads[n].reshape(1, -1, w[n].shape[-1]))
        done.append(res[3])
    for grp in ("c", "b", "a"):
        for l, sems, sums in scatters[grp]:
            _, bufs[grp] = _exchange_wait(_chip_scatter_copies(len(sums), l), f"scatter_wait_{l}{grp}", sems, sums,
                                          bufs[grp], [grad_x] + token + done)
        done = []
        for n, parts in zip(GROUPS[grp], bufs[grp]):
            if n in UPDATED_TRANSPOSED:
                res = _adamw_update(*(jnp.swapaxes(a[n], 1, 2) for a in (w, m, v)), parts=parts)
                out[n] = [jnp.swapaxes(r, 1, 2) for r in res]
            elif n in TRANSPOSED:
                res = update(n, w[n], m[n], v[n], w[n].shape, grad=jnp.swapaxes(_reduce_parts(parts), 1, 2))
            else:
                res = update(n, w[n], m[n], v[n], w[n].shape, parts=parts)
            done.append(res[3])

    loss = lax.psum(loss[0, 0], AXES)
    return loss, grad_x, out


def kernel(x, ffn1_gate, ffn1_up, ffn1_down, ln1_g, ln1_b, w_in, s5_lam_re, s5_lam_im, s5_log_dt, s5_b_re, s5_b_im, s5_c_re, s5_c_im, s5_d, s5_w_glu, conv_w, conv_b, g_s5, g_conv, w_out, ln2_g, ln2_b, ffn2_gate, ffn2_up, ffn2_down, ln3_g, ln3_b, loss_target, m_ffn1_gate, m_ffn1_up, m_ffn1_down, m_ln1_g, m_ln1_b, m_w_in, m_s5_lam_re, m_s5_lam_im, m_s5_log_dt, m_s5_b_re, m_s5_b_im, m_s5_c_re, m_s5_c_im, m_s5_d, m_s5_w_glu, m_conv_w, m_conv_b, m_g_s5, m_g_conv, m_w_out, m_ln2_g, m_ln2_b, m_ffn2_gate, m_ffn2_up, m_ffn2_down, m_ln3_g, m_ln3_b, v_ffn1_gate, v_ffn1_up, v_ffn1_down, v_ln1_g, v_ln1_b, v_w_in, v_s5_lam_re, v_s5_lam_im, v_s5_log_dt, v_s5_b_re, v_s5_b_im, v_s5_c_re, v_s5_c_im, v_s5_d, v_s5_w_glu, v_conv_w, v_conv_b, v_g_s5, v_g_conv, v_w_out, v_ln2_g, v_ln2_b, v_ffn2_gate, v_ffn2_up, v_ffn2_down, v_ln3_g, v_ln3_b):
    given = dict(locals())
    w = {n: given[n] for n in WEIGHTS}
    m = {n: given["m_" + n] for n in WEIGHTS}
    v = {n: given["v_" + n] for n in WEIGHTS}
    T, D = x.shape[-2:]
    loss, grad_x, out = _train_step(x.reshape(T, D), loss_target.reshape(T, D), w, m, v)
    results = [loss, grad_x.reshape(x.shape)]
    for i in range(4):
        results += [out[n][i] for n in WEIGHTS]
    return tuple(results)
```

```python
import functools
import math

import jax
import jax.numpy as jnp
from jax import lax
from jax.experimental import pallas as pl
from jax.experimental.pallas import tpu as pltpu

F32 = jnp.float32
BF16 = jnp.bfloat16
MESH = pl.DeviceIdType.MESH
AXES = ("x", "y", "c")
N_DEV = 8
N_CHIPS = 4

S5_P = 16
S5_N = 64
CONV_W = 3
LN_EPS = 1e-5
RMS_EPS = 1e-6
ADAM_LR = 0.001
ADAM_B1 = 0.9
ADAM_B2 = 0.999
ADAM_EPS = 1e-08
ADAM_WD = 0.01
ADAM_STEP = 10

V7X_VMEM_BYTES = 64 * 1024 * 1024
VMEM_LIMIT = V7X_VMEM_BYTES * 7 // 8
LANES = 128
SUBLANES = 8
BF16_ROWS = 16
MXU_COLS = 256
ROWS_RESIDENT = 2048
PACK_ROWS = 512
S5_GROUPS_PER_BLOCK = LANES // S5_P
S5_STATE_BLOCK = S5_GROUPS_PER_BLOCK * S5_N

HBM_SPEC = pl.BlockSpec(memory_space=pltpu.HBM)
SEM_SPEC = pl.BlockSpec(memory_space=pltpu.SEMAPHORE)
ANY_SPEC = pl.BlockSpec(memory_space=pl.ANY)


def _tile(n, pref, align):
    best = None
    d = align
    while d <= min(n, pref):
        if n % d == 0:
            best = d
        d += align
    return best if best is not None else n


def _params(*sem):
    return pltpu.CompilerParams(dimension_semantics=sem, vmem_limit_bytes=VMEM_LIMIT)


def _dot_nn(a, b):
    return lax.dot_general(a, b, (((1,), (0,)), ((), ())), preferred_element_type=F32)


def _dot_nt(a, b):
    return lax.dot_general(a, b, (((1,), (1,)), ((), ())), preferred_element_type=F32)


def _dot_tn(a, b):
    return lax.dot_general(a, b, (((0,), (0,)), ((), ())), preferred_element_type=F32)


def _colsum(v):
    return jnp.sum(v, axis=0, keepdims=True)


def _rowmean(v):
    return jnp.mean(v, axis=-1, keepdims=True)


def _ffn_up(xb, wg, wu):
    T, D = xb.shape
    F = wg.shape[0]
    tm = _tile(T, ROWS_RESIDENT, 16)
    tn = _tile(F, MXU_COLS, LANES)

    def body(x_ref, wg_ref, wu_ref, cg_ref, cu_ref, h_ref):
        x = x_ref[...]
        g = _dot_nt(x, wg_ref[...])
        u = _dot_nt(x, wu_ref[...])
        sg = jax.nn.sigmoid(g)
        silu = g * sg
        cu_ref[...] = silu.astype(BF16)
        cg_ref[...] = (u * (sg * (1.0 + g * (1.0 - sg)))).astype(BF16)
        h_ref[...] = (silu * u).astype(BF16)

    w_spec = pl.BlockSpec((tn, D), lambda j, i: (j, 0))
    o_spec = pl.BlockSpec((tm, tn), lambda j, i: (i, j))
    return pl.pallas_call(
        body, name="ffn_up", grid=(F // tn, T // tm),
        in_specs=[pl.BlockSpec((tm, D), lambda j, i: (i, 0)), w_spec, w_spec],
        out_specs=[o_spec, o_spec, o_spec],
        out_shape=[jax.ShapeDtypeStruct((T, F), BF16)] * 3,
        compiler_params=_params("arbitrary", "arbitrary"),
    )(xb, wg, wu)


def _mm_acc(pairs, after=()):
    T, K = pairs[0][0].shape
    D = pairs[0][1].shape[1]
    n = len(pairs)
    tk = _tile(K, 512, LANES)
    tm = _tile(T, 512, 16)

    def body(*refs):
        o_ref = refs[-1]

        @pl.when(pl.program_id(0) == 0)
        def _():
            o_ref[...] = jnp.zeros_like(o_ref)

        for r in range(0, T, tm):
            part = _dot_nn(refs[0][r:r + tm, :], refs[1][...])
            for a_ref, w_ref in zip(refs[2:2 * n:2], refs[3:2 * n:2]):
                part += _dot_nn(a_ref[r:r + tm, :], w_ref[...])
            o_ref[r:r + tm, :] += part

    in_specs, operands = [], []
    for a, w in pairs:
        in_specs += [pl.BlockSpec((T, tk), lambda k: (0, k)), pl.BlockSpec((tk, D), lambda k: (k, 0))]
        operands += [a, w]
    return pl.pallas_call(
        body, name="mm_acc", grid=(K // tk,),
        in_specs=in_specs + [ANY_SPEC] * len(after),
        out_specs=pl.BlockSpec((T, D), lambda k: (0, 0)),
        out_shape=jax.ShapeDtypeStruct((T, D), F32),
        compiler_params=_params("arbitrary"),
    )(*operands, *after)


def _mm_res_ln(a, w, res, g, b, scale, alpha, after=()):
    acc = _mm_acc([(a, w)], after)
    T, D = acc.shape
    tm = _tile(T, 256, 16)

    def body(acc_ref, res_ref, g_ref, b_ref, xo_ref, xb_ref, xh_ref, rstd_ref):
        r = alpha * res_ref[...] + scale * acc_ref[...]
        xc = r - _rowmean(r)
        rstd = lax.rsqrt(_rowmean(xc * xc) + LN_EPS)
        xh = xc * rstd
        xo = xh * g_ref[...] + b_ref[...]
        xo_ref[...] = xo
        xb_ref[...] = xo.astype(BF16)
        xh_ref[...] = xh
        rstd_ref[...] = rstd

    row = pl.BlockSpec((tm, D), lambda i: (i, 0))
    vec = pl.BlockSpec((1, D), lambda i: (0, 0))
    return pl.pallas_call(
        body, name="res_ln", grid=(T // tm,),
        in_specs=[row, row, vec, vec],
        out_specs=[row, row, row, pl.BlockSpec((tm, 1), lambda i: (i, 0))],
        out_shape=[jax.ShapeDtypeStruct((T, D), F32), jax.ShapeDtypeStruct((T, D), BF16),
                   jax.ShapeDtypeStruct((T, D), F32), jax.ShapeDtypeStruct((T, 1), F32)],
        compiler_params=_params("arbitrary"),
    )(acc, res, g, b)


def _mm_nt(a, w, after=()):
    M, K = a.shape
    N = w.shape[0]
    tm = _tile(M, ROWS_RESIDENT, 16)
    tn = _tile(N, MXU_COLS, LANES)

    def body(a_ref, w_ref, *rest):
        rest[-1][...] = _dot_nt(a_ref[...], w_ref[...])

    return pl.pallas_call(
        body, name="mm_nt", grid=(N // tn, M // tm),
        in_specs=[pl.BlockSpec((tm, K), lambda j, i: (i, 0)), pl.BlockSpec((tn, K), lambda j, i: (j, 0))]
        + [ANY_SPEC] * len(after),
        out_specs=pl.BlockSpec((tm, tn), lambda j, i: (i, j)),
        out_shape=jax.ShapeDtypeStruct((M, N), F32),
        compiler_params=_params("arbitrary", "arbitrary"),
    )(a, w, *after)


def _mm_tn(a, b, scale, out_dtype, half=None, addend=None):
    T, M = a.shape
    N = b.shape[1]
    rows = M if half is None else M // 2
    tm = _tile(rows, 512, LANES)
    tn = _tile(N, ROWS_RESIDENT, LANES)
    if tm < 512:
        tm = _tile(rows, 1408, LANES)
        tn = _tile(N, ROWS_RESIDENT // 2, LANES)
    first = 0 if half is None else half * (rows // tm)

    def body(a_ref, b_ref, *rest):
        out = scale * _dot_tn(a_ref[...], b_ref[...])
        if addend is not None:
            out = out + rest[0][...].astype(F32)
        rest[-1][...] = out.astype(out_dtype)

    o_spec = pl.BlockSpec((tm, tn), lambda i, j: (i, j))
    return pl.pallas_call(
        body, name="mm_tn", grid=(rows // tm, N // tn),
        in_specs=[pl.BlockSpec((T, tm), lambda i, j: (0, first + i)), pl.BlockSpec((T, tn), lambda i, j: (0, j))]
        + ([] if addend is None else [o_spec]),
        out_specs=o_spec,
        out_shape=jax.ShapeDtypeStruct((rows, N), out_dtype),
        compiler_params=_params("arbitrary", "arbitrary"),
    )(a, b, *([] if addend is None else [addend]))


def _ln_bwd(dy, xh, rstd, g):
    dxh = dy * g
    dr = rstd * (dxh - _rowmean(dxh) - xh * _rowmean(dxh * xh))
    return dr, _colsum(dy * xh), _colsum(dy)


def _loss_ln_bwd(y, target, xh, rstd, g):
    T, D = y.shape
    tm = _tile(T, 256, 16)

    def body(y_ref, t_ref, xh_ref, rstd_ref, g_ref, dr_ref, drb_ref, dg_ref, db_ref, loss_ref):
        i = pl.program_id(0)

        @pl.when(i == 0)
        def _():
            dg_ref[...] = jnp.zeros_like(dg_ref)
            db_ref[...] = jnp.zeros_like(db_ref)
            loss_ref[...] = jnp.zeros_like(loss_ref)

        err = y_ref[...] - t_ref[...]
        loss_ref[...] += (0.5 / D) * _colsum(jnp.sum(err * err, axis=1, keepdims=True))
        dr, dg, db = _ln_bwd(err * (1.0 / D), xh_ref[...], rstd_ref[...], g_ref[...])
        dr_ref[...] = dr
        drb_ref[...] = dr.astype(BF16)
        dg_ref[...] += dg
        db_ref[...] += db

    row = pl.BlockSpec((tm, D), lambda i: (i, 0))
    vec = pl.BlockSpec((1, D), lambda i: (0, 0))
    return pl.pallas_call(
        body, name="loss_ln_bwd", grid=(T // tm,),
        in_specs=[row, row, row, pl.BlockSpec((tm, 1), lambda i: (i, 0)), vec],
        out_specs=[row, row, vec, vec, pl.BlockSpec((1, 1), lambda i: (0, 0))],
        out_shape=[jax.ShapeDtypeStruct((T, D), F32), jax.ShapeDtypeStruct((T, D), BF16),
                   jax.ShapeDtypeStruct((1, D), F32), jax.ShapeDtypeStruct((1, D), F32),
                   jax.ShapeDtypeStruct((1, 1), F32)],
        compiler_params=_params("arbitrary"),
    )(y, target, xh, rstd, g)


def _ffn_down_bwd(drb, wd, cg, cu, after=()):
    T, D = drb.shape
    F = wd.shape[0]
    tm = _tile(T, ROWS_RESIDENT, 16)
    tn = _tile(F, MXU_COLS, LANES)

    def body(dr_ref, wd_ref, cg_ref, cu_ref, *rest):
        dg_ref, du_ref = rest[len(after):]
        dh = 0.5 * _dot_nt(dr_ref[...], wd_ref[...])
        du_ref[...] = (dh * cu_ref[...].astype(F32)).astype(BF16)
        dg_ref[...] = (dh * cg_ref[...].astype(F32)).astype(BF16)

    t_spec = pl.BlockSpec((tm, tn), lambda j, i: (i, j))
    return pl.pallas_call(
        body, name="ffn_down_bwd", grid=(F // tn, T // tm),
        in_specs=[pl.BlockSpec((tm, D), lambda j, i: (i, 0)), pl.BlockSpec((tn, D), lambda j, i: (j, 0)),
                  t_spec, t_spec] + [ANY_SPEC] * len(after),
        out_specs=[t_spec, t_spec],
        out_shape=[jax.ShapeDtypeStruct((T, F), BF16), jax.ShapeDtypeStruct((T, F), BF16)],
        compiler_params=_params("arbitrary", "arbitrary"),
    )(drb, wd, cg, cu, *after)


def _mm_dx(pairs, res, alpha, ln=None, after=()):
    acc = _mm_acc(pairs, after)
    T, D = acc.shape
    tm = _tile(T, 256, 16)
    with_ln = ln is not None

    def body(acc_ref, res_ref, *refs):
        dx = alpha * res_ref[...] + acc_ref[...]
        if with_ln:
            xh_ref, rstd_ref, g_ref, dr_ref, drb_ref, dg_ref, db_ref = refs

            @pl.when(pl.program_id(0) == 0)
            def _():
                dg_ref[...] = jnp.zeros_like(dg_ref)
                db_ref[...] = jnp.zeros_like(db_ref)

            dr, dg, db = _ln_bwd(dx, xh_ref[...], rstd_ref[...], g_ref[...])
            dr_ref[...] = dr
            drb_ref[...] = dr.astype(BF16)
            dg_ref[...] += dg
            db_ref[...] += db
        else:
            refs[0][...] = dx

    row = pl.BlockSpec((tm, D), lambda i: (i, 0))
    vec = pl.BlockSpec((1, D), lambda i: (0, 0))
    in_specs, operands = [row, row], [acc, res]
    if with_ln:
        in_specs += [row, pl.BlockSpec((tm, 1), lambda i: (i, 0)), vec]
        operands += list(ln)
        out_specs = [row, row, vec, vec]
        out_shape = [jax.ShapeDtypeStruct((T, D), F32), jax.ShapeDtypeStruct((T, D), BF16),
                     jax.ShapeDtypeStruct((1, D), F32), jax.ShapeDtypeStruct((1, D), F32)]
    else:
        out_specs = [row]
        out_shape = [jax.ShapeDtypeStruct((T, D), F32)]
    return pl.pallas_call(
        body, name="dx_ln_bwd" if with_ln else "dx_res", grid=(T // tm,),
        in_specs=in_specs, out_specs=out_specs, out_shape=out_shape,
        compiler_params=_params("arbitrary"),
    )(*operands)


def _s5_discretize(lre, lim, ldt, br, bi):
    dt = jnp.exp(ldt)
    mag = jnp.exp(lre * dt)
    ang = lim * dt
    ar = mag * jnp.cos(ang)
    ai = mag * jnp.sin(ang)
    den = lre * lre + lim * lim
    nr = ar - 1.0
    qr = (nr * lre + ai * lim) / den
    qi = (ai * lre - nr * lim) / den
    bbr = qr[None] * br - qi[None] * bi
    bbi = qr[None] * bi + qi[None] * br
    return ar, ai, bbr, bbi


def _s5_params_fwd(lre, lim, ldt, br, bi):
    def body(lre_ref, lim_ref, ldt_ref, br_ref, bi_ref, ar_ref, ai_ref, bbr_ref, bbi_ref):
        ar, ai, bbr, bbi = _s5_discretize(lre_ref[...], lim_ref[...], ldt_ref[...], br_ref[...], bi_ref[...])
        ar_ref[...] = ar
        ai_ref[...] = ai
        bbr_ref[...] = bbr
        bbi_ref[...] = bbi

    sds = jax.ShapeDtypeStruct
    return pl.pallas_call(
        body, name="s5_params_fwd",
        out_shape=[sds(lre.shape, F32), sds(lre.shape, F32), sds(br.shape, F32), sds(br.shape, F32)],
        compiler_params=pltpu.CompilerParams(vmem_limit_bytes=VMEM_LIMIT),
    )(lre, lim, ldt, br, bi)


def _s5_params_bwd(lre, lim, ldt, br, bi, dar, dai, dbbr, dbbi):
    def body(lre_ref, lim_ref, ldt_ref, br_ref, bi_ref, dar_ref, dai_ref, dbbr_ref, dbbi_ref,
             o_lre, o_lim, o_ldt, o_br, o_bi):
        _, vjp = jax.vjp(_s5_discretize, lre_ref[...], lim_ref[...], ldt_ref[...], br_ref[...], bi_ref[...])
        g = vjp((dar_ref[...], dai_ref[...], dbbr_ref[...], dbbi_ref[...]))
        o_lre[...] = g[0]
        o_lim[...] = g[1]
        o_ldt[...] = g[2]
        o_br[...] = g[3]
        o_bi[...] = g[4]

    sds = jax.ShapeDtypeStruct
    return pl.pallas_call(
        body, name="s5_params_bwd",
        out_shape=[sds(lre.shape, F32), sds(lre.shape, F32), sds(ldt.shape, F32), sds(br.shape, F32),
                   sds(br.shape, F32)],
        compiler_params=pltpu.CompilerParams(vmem_limit_bytes=VMEM_LIMIT),
    )(lre, lim, ldt, br, bi, dar, dai, dbbr, dbbi)


def _cmul(a_re, a_im, b_re, b_im):
    return a_re * b_re - a_im * b_im, a_re * b_im + a_im * b_re


def _s5_tile(i):
    if isinstance(i, int):
        return pl.ds(i * SUBLANES, SUBLANES)
    return pl.ds(pl.multiple_of(i * SUBLANES, SUBLANES), SUBLANES)


def _s5_powers(a_re, a_im, pwr_ref, pwi_ref, seg):
    def step(i, carry):
        p_re, p_im = carry
        pwr_ref[pl.ds(i, 1), :] = p_re
        pwi_ref[pl.ds(i, 1), :] = p_im
        return _cmul(a_re, a_im, p_re, p_im)

    lax.fori_loop(0, seg, step, (a_re, a_im))


def _s5_fwd(proj, bdr, bdi, cdr, cdi, ar, ai, dvec):
    T = proj.shape[0]
    GB, UB, SB = bdr.shape
    tc = _tile(T, 256, SUBLANES * SUBLANES)
    seg = tc // SUBLANES

    def body(u_ref, bdr_ref, bdi_ref, cdr_ref, cdi_ref, ar_ref, ai_ref, d_ref, y_ref, sr_ref, si_ref,
             cr_ref, ci_ref, pwr_ref, pwi_ref, str_ref, sti_ref, up_ref, yp_ref):
        a_re = ar_ref[...]
        a_im = ai_ref[...]

        @pl.when(pl.program_id(1) == 0)
        def _():
            cr_ref[...] = jnp.zeros_like(cr_ref)
            ci_ref[...] = jnp.zeros_like(ci_ref)
            _s5_powers(a_re, a_im, pwr_ref, pwi_ref, seg)

        for i in range(seg):
            up_ref[_s5_tile(i), :] = u_ref[pl.ds(i, SUBLANES, stride=seg), :]
        u = up_ref[...]
        ub = u.astype(BF16)
        sr_ref[...] = _dot_nn(ub, bdr_ref[...])
        si_ref[...] = _dot_nn(ub, bdi_ref[...])

        def local(i, carry):
            p_re, p_im = carry
            rows = _s5_tile(i)
            n_re, n_im = _cmul(a_re, a_im, p_re, p_im)
            n_re, n_im = n_re + sr_ref[rows, :], n_im + si_ref[rows, :]
            sr_ref[rows, :] = n_re
            si_ref[rows, :] = n_im
            return n_re, n_im

        e_re, e_im = lax.fori_loop(1, seg, local, (sr_ref[_s5_tile(0), :], si_ref[_s5_tile(0), :]))
        s_re, s_im = cr_ref[...], ci_ref[...]
        top_re, top_im = pwr_ref[seg - 1:seg, :], pwi_ref[seg - 1:seg, :]
        for j in range(SUBLANES):
            str_ref[j:j + 1, :] = s_re
            sti_ref[j:j + 1, :] = s_im
            n_re, n_im = _cmul(top_re, top_im, s_re, s_im)
            s_re, s_im = n_re + e_re[j:j + 1, :], n_im + e_im[j:j + 1, :]
        cr_ref[...] = s_re
        ci_ref[...] = s_im
        b_re, b_im = str_ref[...], sti_ref[...]

        def fix(i, carry):
            rows = _s5_tile(i)
            f_re, f_im = _cmul(pwr_ref[pl.ds(i, 1), :], pwi_ref[pl.ds(i, 1), :], b_re, b_im)
            sr_ref[rows, :] += f_re
            si_ref[rows, :] += f_im
            return carry

        lax.fori_loop(0, seg, fix, 0)
        yp_ref[...] = (_dot_nn(sr_ref[...].astype(BF16), cdr_ref[...])
                       - _dot_nn(si_ref[...].astype(BF16), cdi_ref[...]) + d_ref[...] * u)
        for i in range(seg):
            y_ref[pl.ds(i, SUBLANES, stride=seg), :] = yp_ref[_s5_tile(i), :]

    return pl.pallas_call(
        body, name="s5_fwd", grid=(GB, T // tc),
        in_specs=[pl.BlockSpec((tc, UB), lambda j, t: (t, j)),
                  pl.BlockSpec((None, UB, SB), lambda j, t: (j, 0, 0)),
                  pl.BlockSpec((None, UB, SB), lambda j, t: (j, 0, 0)),
                  pl.BlockSpec((None, SB, UB), lambda j, t: (j, 0, 0)),
                  pl.BlockSpec((None, SB, UB), lambda j, t: (j, 0, 0)),
                  pl.BlockSpec((1, SB), lambda j, t: (0, j)),
                  pl.BlockSpec((1, SB), lambda j, t: (0, j)),
                  pl.BlockSpec((1, UB), lambda j, t: (0, j))],
        out_specs=[pl.BlockSpec((tc, UB), lambda j, t: (t, j)),
                   pl.BlockSpec((tc, SB), lambda j, t: (t, j)),
                   pl.BlockSpec((tc, SB), lambda j, t: (t, j))],
        out_shape=[jax.ShapeDtypeStruct((T, GB * UB), F32), jax.ShapeDtypeStruct((T, GB * SB), F32),
                   jax.ShapeDtypeStruct((T, GB * SB), F32)],
        scratch_shapes=[pltpu.VMEM((1, SB), F32), pltpu.VMEM((1, SB), F32),
                        pltpu.VMEM((seg, SB), F32), pltpu.VMEM((seg, SB), F32),
                        pltpu.VMEM((SUBLANES, SB), F32), pltpu.VMEM((SUBLANES, SB), F32),
                        pltpu.VMEM((tc, UB), F32), pltpu.VMEM((tc, UB), F32)],
        compiler_params=_params("arbitrary", "arbitrary"),
    )(proj, bdr, bdi, cdr, cdi, ar, ai, dvec)


def _s5_bwd(dy, proj, sr, si, bdr, bdi, cdr, cdi, ar, ai, dvec):
    T = dy.shape[0]
    GB, UB, SB = bdr.shape
    tc = _tile(T, 256, SUBLANES * SUBLANES)
    seg = tc // SUBLANES
    nt = T // tc
    halo_blocks = tc // SUBLANES

    def body(dy_ref, u_ref, sr_ref, si_ref, hr_ref, hi_ref, bdr_ref, bdi_ref, cdr_ref, cdi_ref, ar_ref, ai_ref,
             d_ref, du_ref, dbdr_ref, dbdi_ref, dcdr_ref, dcdi_ref, dar_ref, dai_ref, dd_ref,
             gr_ref, gi_ref, pwr_ref, pwi_ref, rtr_ref, rti_ref, cr_ref, ci_ref, dyp_ref, up_ref, dup_ref):
        step_no = pl.program_id(1)
        first_chunk = step_no == nt - 1
        a_re = ar_ref[...]
        a_im = ai_ref[...]

        @pl.when(step_no == 0)
        def _():
            for ref in (cr_ref, ci_ref, dbdr_ref, dbdi_ref, dcdr_ref, dcdi_ref, dar_ref, dai_ref, dd_ref):
                ref[...] = jnp.zeros_like(ref)
            _s5_powers(a_re, a_im, pwr_ref, pwi_ref, seg)

        for i in range(seg):
            dyp_ref[_s5_tile(i), :] = dy_ref[pl.ds(i, SUBLANES, stride=seg), :]
            up_ref[_s5_tile(i), :] = u_ref[pl.ds(i, SUBLANES, stride=seg), :]
        dy = dyp_ref[...]
        dyb = dy.astype(BF16)
        u = up_ref[...]
        gr_ref[...] = _dot_nt(dyb, cdr_ref[...])
        gi_ref[...] = -_dot_nt(dyb, cdi_ref[...])
        dcdr_ref[...] += _dot_tn(sr_ref[...].astype(BF16), dyb)
        dcdi_ref[...] -= _dot_tn(si_ref[...].astype(BF16), dyb)

        def local(n, carry):
            c_re, c_im = carry
            rows = _s5_tile(seg - 1 - n)
            g_re = gr_ref[rows, :] + a_re * c_re + a_im * c_im
            g_im = gi_ref[rows, :] + a_re * c_im - a_im * c_re
            gr_ref[rows, :] = g_re
            gi_ref[rows, :] = g_im
            return g_re, g_im

        last = _s5_tile(seg - 1)
        m_re, m_im = lax.fori_loop(1, seg, local, (gr_ref[last, :], gi_ref[last, :]))
        top_re, top_im = pwr_ref[seg - 1:seg, :], -pwi_ref[seg - 1:seg, :]
        r_re, r_im = cr_ref[...], ci_ref[...]
        for j in reversed(range(SUBLANES)):
            rtr_ref[j:j + 1, :] = r_re
            rti_ref[j:j + 1, :] = r_im
            n_re, n_im = _cmul(top_re, top_im, r_re, r_im)
            r_re, r_im = n_re + m_re[j:j + 1, :], n_im + m_im[j:j + 1, :]
        cr_ref[...] = r_re
        ci_ref[...] = r_im
        f_re, f_im = rtr_ref[...], rti_ref[...]
        keep = jnp.where(first_chunk, 0.0, 1.0)
        sub = lax.broadcasted_iota(jnp.int32, (SUBLANES, SB), 0)
        before_re = jnp.where(sub == 0, hr_ref[SUBLANES - 1:SUBLANES, :] * keep, pltpu.roll(sr_ref[last, :], 1, 0))
        before_im = jnp.where(sub == 0, hi_ref[SUBLANES - 1:SUBLANES, :] * keep, pltpu.roll(si_ref[last, :], 1, 0))

        def fix(i, p_re, p_im, acc):
            rows = _s5_tile(i)
            k = seg - 1 - i
            c_re, c_im = _cmul(pwr_ref[pl.ds(k, 1), :], -pwi_ref[pl.ds(k, 1), :], f_re, f_im)
            g_re = gr_ref[rows, :] + c_re
            g_im = gi_ref[rows, :] + c_im
            gr_ref[rows, :] = g_re
            gi_ref[rows, :] = g_im
            return acc[0] + p_re * g_re + p_im * g_im, acc[1] + p_re * g_im - p_im * g_re

        zero = jnp.zeros((SUBLANES, SB), F32)
        acc = fix(0, before_re, before_im, (zero, zero))
        acc = lax.fori_loop(
            1, seg, lambda i, acc: fix(i, sr_ref[_s5_tile(i - 1), :], si_ref[_s5_tile(i - 1), :], acc), acc)
        dar_ref[...] += _colsum(acc[0])
        dai_ref[...] += _colsum(acc[1])
        gsr = gr_ref[...].astype(BF16)
        gsi = gi_ref[...].astype(BF16)
        ub = u.astype(BF16)
        dbdr_ref[...] += _dot_tn(ub, gsr)
        dbdi_ref[...] += _dot_tn(ub, gsi)
        dup_ref[...] = _dot_nt(gsr, bdr_ref[...]) + _dot_nt(gsi, bdi_ref[...]) + d_ref[...] * dy
        for i in range(seg):
            du_ref[pl.ds(i, SUBLANES, stride=seg), :] = dup_ref[_s5_tile(i), :]
        dd_ref[...] += _colsum(dy * u)

    def rev(t):
        return nt - 1 - t

    def halo(j, t):
        return (jnp.maximum(rev(t) * halo_blocks - 1, 0), j)

    ublk = pl.BlockSpec((tc, UB), lambda j, t: (rev(t), j))
    sblk = pl.BlockSpec((tc, SB), lambda j, t: (rev(t), j))
    bd_spec = pl.BlockSpec((None, UB, SB), lambda j, t: (j, 0, 0))
    cd_spec = pl.BlockSpec((None, SB, UB), lambda j, t: (j, 0, 0))
    svec = pl.BlockSpec((1, SB), lambda j, t: (0, j))
    uvec = pl.BlockSpec((1, UB), lambda j, t: (0, j))
    sds = jax.ShapeDtypeStruct
    return pl.pallas_call(
        body, name="s5_bwd", grid=(GB, nt),
        in_specs=[ublk, ublk, sblk, sblk, pl.BlockSpec((SUBLANES, SB), halo), pl.BlockSpec((SUBLANES, SB), halo),
                  bd_spec, bd_spec, cd_spec, cd_spec, svec, svec, uvec],
        out_specs=[ublk, bd_spec, bd_spec, cd_spec, cd_spec, svec, svec, uvec],
        out_shape=[sds((T, GB * UB), F32), sds((GB, UB, SB), F32), sds((GB, UB, SB), F32),
                   sds((GB, SB, UB), F32), sds((GB, SB, UB), F32), sds((1, GB * SB), F32),
                   sds((1, GB * SB), F32), sds((1, GB * UB), F32)],
        scratch_shapes=[pltpu.VMEM((tc, SB), F32), pltpu.VMEM((tc, SB), F32),
                        pltpu.VMEM((seg, SB), F32), pltpu.VMEM((seg, SB), F32),
                        pltpu.VMEM((SUBLANES, SB), F32), pltpu.VMEM((SUBLANES, SB), F32),
                        pltpu.VMEM((1, SB), F32), pltpu.VMEM((1, SB), F32),
                        pltpu.VMEM((tc, UB), F32), pltpu.VMEM((tc, UB), F32), pltpu.VMEM((tc, UB), F32)],
        compiler_params=_params("arbitrary", "arbitrary"),
    )(dy, proj, sr, si, sr, si, bdr, bdi, cdr, cdi, ar, ai, dvec)


def _shift_down(v, k):
    rows = lax.broadcasted_iota(jnp.int32, v.shape, 0)
    return jnp.where(rows >= k, pltpu.roll(v, k, 0), 0.0)


def _shift_up(v, k):
    n = v.shape[0]
    rows = lax.broadcasted_iota(jnp.int32, v.shape, 0)
    return jnp.where(rows < n - k, pltpu.roll(v, n - k, 0), 0.0)


def _conv_specs(T, cb, n_s5_blocks, n_conv_blocks):
    gb = pl.BlockSpec((T, cb), lambda j: (0, n_s5_blocks + j))
    gc = pl.BlockSpec((T, cb), lambda j: (0, n_s5_blocks + n_conv_blocks + j))
    hh = pl.BlockSpec((T, cb), lambda j: (0, n_s5_blocks + 2 * n_conv_blocks + j))
    return gb, gc, hh


def _conv_fwd(proj, cw, cbias, d_s5, d_conv):
    T = proj.shape[0]
    cb = _tile(d_conv, 256, LANES)

    def body(gb_ref, gc_ref, hh_ref, w_ref, b_ref, z_ref):
        v = gc_ref[...] * hh_ref[...]
        w = w_ref[...]
        cv = b_ref[...] + w[0:1, :] * _shift_down(v, 2) + w[1:2, :] * _shift_down(v, 1) + w[2:3, :] * v
        z_ref[...] = gb_ref[...] * cv

    gb, gc, hh = _conv_specs(T, cb, d_s5 // cb, d_conv // cb)
    col = pl.BlockSpec((T, cb), lambda j: (0, j))
    return pl.pallas_call(
        body, name="conv_fwd", grid=(d_conv // cb,),
        in_specs=[gb, gc, hh, pl.BlockSpec((CONV_W, cb), lambda j: (0, j)), pl.BlockSpec((1, cb), lambda j: (0, j))],
        out_specs=col, out_shape=jax.ShapeDtypeStruct((T, d_conv), F32),
        compiler_params=_params("arbitrary"),
    )(proj, proj, proj, cw, cbias)


def _conv_bwd(dz, proj, cw, cbias, d_s5, d_conv):
    T = proj.shape[0]
    cb = _tile(d_conv, 256, LANES)

    def body(dz_ref, gb_ref, gc_ref, hh_ref, w_ref, b_ref, dgb_ref, dgc_ref, dhh_ref, dw_ref, db_ref):
        gc = gc_ref[...]
        hh = hh_ref[...]
        dz = dz_ref[...]
        w = w_ref[...]
        v = gc * hh
        v1 = _shift_down(v, 1)
        v2 = _shift_down(v, 2)
        cv = b_ref[...] + w[0:1, :] * v2 + w[1:2, :] * v1 + w[2:3, :] * v
        dgb_ref[...] = (dz * cv).astype(BF16)
        dcv = dz * gb_ref[...]
        dv = w[2:3, :] * dcv + w[1:2, :] * _shift_up(dcv, 1) + w[0:1, :] * _shift_up(dcv, 2)
        dgc_ref[...] = (dv * hh).astype(BF16)
        dhh_ref[...] = (dv * gc).astype(BF16)
        dw_ref[0:1, :] = _colsum(dcv * v2)
        dw_ref[1:2, :] = _colsum(dcv * v1)
        dw_ref[2:3, :] = _colsum(dcv * v)
        db_ref[...] = _colsum(dcv)

    gb, gc, hh = _conv_specs(T, cb, d_s5 // cb, d_conv // cb)
    col = pl.BlockSpec((T, cb), lambda j: (0, j))
    wspec = pl.BlockSpec((CONV_W, cb), lambda j: (0, j))
    bspec = pl.BlockSpec((1, cb), lambda j: (0, j))
    sds = jax.ShapeDtypeStruct
    return pl.pallas_call(
        body, name="conv_bwd", grid=(d_conv // cb,),
        in_specs=[col, gb, gc, hh, wspec, bspec],
        out_specs=[col, col, col, wspec, bspec],
        out_shape=[sds((T, d_conv), BF16), sds((T, d_conv), BF16), sds((T, d_conv), BF16),
                   sds((CONV_W, d_conv), F32), sds((1, d_conv), F32)],
        compiler_params=_params("arbitrary"),
    )(dz, proj, proj, proj, cw, cbias)


def _rms(v, g):
    rstd = lax.rsqrt(_rowmean(v * v) + RMS_EPS)
    return v * rstd * g, rstd


def _rms_bwd(dyn, v, rstd, g):
    w = dyn * g
    return rstd * w - v * (rstd * rstd * rstd) * _rowmean(w * v), _colsum(dyn * v * rstd)


def _mix_post(y, z, wglu, g_s5, g_conv):
    T, C = y.shape
    tm = _tile(T, 256, 16)

    def body(y_ref, z_ref, w_ref, gs_ref, gc_ref, m_ref, gl_ref):
        ge = jax.nn.gelu(y_ref[...])
        gl = _dot_nn(ge.astype(BF16), w_ref[...])
        gl_ref[...] = gl
        yn, _ = _rms(ge * jax.nn.sigmoid(gl), gs_ref[...])
        zn, _ = _rms(z_ref[...], gc_ref[...])
        m_ref[:, 0:C] = yn.astype(BF16)
        m_ref[:, C:2 * C] = zn.astype(BF16)

    row = pl.BlockSpec((tm, C), lambda i: (i, 0))
    vec = pl.BlockSpec((1, C), lambda i: (0, 0))
    return pl.pallas_call(
        body, name="mix_post", grid=(T // tm,),
        in_specs=[row, row, pl.BlockSpec((C, C), lambda i: (0, 0)), vec, vec],
        out_specs=[pl.BlockSpec((tm, 2 * C), lambda i: (i, 0)), row],
        out_shape=[jax.ShapeDtypeStruct((T, 2 * C), BF16), jax.ShapeDtypeStruct((T, C), F32)],
        compiler_params=_params("arbitrary"),
    )(y, z, wglu, g_s5, g_conv)


def _mix_post_bwd(dm, y, gl, z, wglu, g_s5, g_conv):
    T, C = y.shape
    tm = _tile(T, 256, 16)

    def body(dm_ref, y_ref, gl_ref, z_ref, w_ref, gs_ref, gc_ref, dy_ref, dz_ref, dw_ref, dgs_ref, dgc_ref):
        @pl.when(pl.program_id(0) == 0)
        def _():
            dw_ref[...] = jnp.zeros_like(dw_ref)
            dgs_ref[...] = jnp.zeros_like(dgs_ref)
            dgc_ref[...] = jnp.zeros_like(dgc_ref)

        yv = y_ref[...]
        ge, gelu_vjp = jax.vjp(jax.nn.gelu, yv)
        gl = gl_ref[...]
        sg = jax.nn.sigmoid(gl)
        y2 = ge * sg
        _, rstd_y = _rms(y2, gs_ref[...])
        dy2, dgs = _rms_bwd(dm_ref[:, 0:C], y2, rstd_y, gs_ref[...])
        dgs_ref[...] += dgs
        dgl = (dy2 * ge * sg * (1.0 - sg)).astype(BF16)
        dge = dy2 * sg + _dot_nt(dgl, w_ref[...])
        dw_ref[...] += _dot_tn(ge.astype(BF16), dgl)
        dy_ref[...] = gelu_vjp(dge)[0]
        zv = z_ref[...]
        _, rstd_z = _rms(zv, gc_ref[...])
        dz, dgc = _rms_bwd(dm_ref[:, C:2 * C], zv, rstd_z, gc_ref[...])
        dz_ref[...] = dz
        dgc_ref[...] += dgc

    row = pl.BlockSpec((tm, C), lambda i: (i, 0))
    vec = pl.BlockSpec((1, C), lambda i: (0, 0))
    full = pl.BlockSpec((C, C), lambda i: (0, 0))
    sds = jax.ShapeDtypeStruct
    return pl.pallas_call(
        body, name="mix_post_bwd", grid=(T // tm,),
        in_specs=[pl.BlockSpec((tm, 2 * C), lambda i: (i, 0)), row, row, row, full, vec, vec],
        out_specs=[row, row, full, vec, vec],
        out_shape=[sds((T, C), F32), sds((T, C), F32), sds((C, C), F32), sds((1, C), F32), sds((1, C), F32)],
        compiler_params=_params("arbitrary"),
    )(dm, y, gl, z, wglu, g_s5, g_conv)


def _adamw(w, g, m, v):
    m = ADAM_B1 * m + (1.0 - ADAM_B1) * g
    v = ADAM_B2 * v + (1.0 - ADAM_B2) * (g * g)
    m_hat = m / (1.0 - ADAM_B1 ** ADAM_STEP)
    v_hat = v / (1.0 - ADAM_B2 ** ADAM_STEP)
    return -ADAM_LR * (m_hat / (jnp.sqrt(v_hat) + ADAM_EPS) + ADAM_WD * w), m, v


def _sum_parts(p_ref):
    total = p_ref[0].astype(F32)
    for d in range(1, p_ref.shape[0]):
        total = total + p_ref[d].astype(F32)
    return total


def _row_tile(R, C, n_streams):
    budget = VMEM_LIMIT // 3 // (n_streams * C * 4)
    return _tile(R, max(BF16_ROWS, budget), BF16_ROWS)


def _reduce_parts(parts):
    L, P, R, C = parts.shape
    tr = _row_tile(R, C, P + 1)

    def body(p_ref, o_ref):
        o_ref[...] = _sum_parts(p_ref)

    return pl.pallas_call(
        body, name="reduce_parts", grid=(L, R // tr),
        in_specs=[pl.BlockSpec((None, P, tr, C), lambda l, i: (l, 0, i, 0))],
        out_specs=pl.BlockSpec((None, tr, C), lambda l, i: (l, i, 0)),
        out_shape=jax.ShapeDtypeStruct((L, R, C), F32),
        compiler_params=_params("arbitrary", "arbitrary"),
    )(parts)


def _adamw_update(w, m, v, grad=None, parts=None):
    L, R, C = w.shape
    from_parts = parts is not None
    P = parts.shape[1] if from_parts else 1
    tr = _row_tile(R, C, P + 7)

    def body(g_in_ref, w_ref, m_ref, v_ref, g_ref, d_ref, nm_ref, nv_ref):
        g = _sum_parts(g_in_ref) if from_parts else g_in_ref[...]
        delta, nm, nv = _adamw(w_ref[...], g, m_ref[...], v_ref[...])
        g_ref[...] = g
        d_ref[...] = delta
        nm_ref[...] = nm
        nv_ref[...] = nv

    blk = pl.BlockSpec((None, tr, C), lambda l, i: (l, i, 0))
    g_spec = pl.BlockSpec((None, P, tr, C), lambda l, i: (l, 0, i, 0)) if from_parts else blk
    out = jax.ShapeDtypeStruct((L, R, C), F32)
    return pl.pallas_call(
        body, name="adamw_parts" if from_parts else "adamw", grid=(L, R // tr),
        in_specs=[g_spec, blk, blk, blk], out_specs=[blk, blk, blk, blk], out_shape=[out, out, out, out],
        compiler_params=_params("arbitrary", "arbitrary"),
    )(parts if from_parts else grad, w, m, v)


def _pair_sum(full, stage):
    _, R, C = full.shape
    tr = _row_tile(R, C, 4)

    def body(f_ref, s_ref, o_ref):
        mine = f_ref[lax.axis_index("c")]
        o_ref[...] = (mine.astype(F32) + s_ref[...].astype(F32)).astype(BF16)

    blk = pl.BlockSpec((None, tr, C), lambda q, i: (q, i, 0))
    return pl.pallas_call(
        body, name="pair_sum", grid=(N_CHIPS, R // tr),
        in_specs=[pl.BlockSpec((None, 2, tr, C), lambda q, i: (q, 0, i, 0)), blk],
        out_specs=blk, out_shape=jax.ShapeDtypeStruct((N_CHIPS, R, C), BF16),
        compiler_params=_params("arbitrary", "arbitrary"),
    )(full.reshape(N_CHIPS, 2, R, C), stage)


def _me():
    x, y, c = (lax.axis_index(a) for a in AXES)
    return x, y, c, 4 * x + 2 * y + c


def _peer(rel):
    x, y, c, _ = _me()
    px = 1 - x if rel & 4 else x
    py = 1 - y if rel & 2 else y
    pc = 1 - c if rel & 1 else c
    return (px, py, pc), 4 * px + 2 * py + pc


DATAFLOW = pltpu.SideEffectType.DATAFLOW_SIDE_EFFECTING


def _remote_copy(src, dst, sems):
    return functools.partial(pltpu.make_async_remote_copy, src_ref=src, dst_ref=dst, **sems)


ALL_PEERS = tuple(range(1, N_DEV))
SIBLING = 1
OTHER_CHIPS = (2, 4, 6)
SIBLING_AND_OTHER_CHIPS = (SIBLING,) + OTHER_CHIPS


def _slot(dev, blk, same_core, own_core_last):
    if not own_core_last:
        return blk
    return (N_CHIPS if same_core else 0) + 2 * dev[0] + dev[1]


def _gather_copies(n, rels=ALL_PEERS, own_core_last=False):
    def copies(srcs, lands, send_sems, recv_sems, local_sems):
        x, y, c, me = _me()
        local, remote = [], []
        for k in range(n):
            local.append(functools.partial(pltpu.make_async_copy, srcs[k],
                                           lands[k].at[_slot((x, y, c), me, True, own_core_last)], local_sems.at[k]))
            for rel in rels:
                dev, blk = _peer(rel)
                same_core = not rel & SIBLING
                sems = dict(send_sem=send_sems.at[_sem_index(k, rel)], recv_sem=recv_sems.at[_sem_index(k, rel)],
                            device_id=dev, device_id_type=MESH)
                remote.append((_remote_copy(srcs[k], lands[k].at[_slot((x, y, c), me, same_core, own_core_last)], sems),
                               _remote_copy(srcs[k], lands[k].at[_slot(dev, blk, same_core, own_core_last)], sems)))
        return local, remote

    copies.n_arrays = n
    return copies


Y_NEIGHBOUR, X_NEIGHBOUR, DIAGONAL = 2, 4, 6
SIBLING_AND_NEIGHBOURS = (SIBLING, Y_NEIGHBOUR, X_NEIGHBOUR)


def _relay_copies(n, own_core_last=False):
    def copies(srcs, lands, send_sems, recv_sems, local_sems):
        diagonal = _slot(*_peer(DIAGONAL), True, own_core_last)
        remote = []
        for k in range(n):
            half = lands[k].shape[1] // 2
            for to, held, rows in ((X_NEIGHBOUR, Y_NEIGHBOUR, pl.ds(0, half)), (Y_NEIGHBOUR, X_NEIGHBOUR, pl.ds(half, half))):
                block = lands[k].at[_slot(*_peer(held), True, own_core_last), rows]
                sems = dict(send_sem=send_sems.at[_sem_index(k, to)], recv_sem=recv_sems.at[_sem_index(k, to)],
                            device_id=_peer(to)[0], device_id_type=MESH)
                remote.append((_remote_copy(block, block, sems), _remote_copy(block, lands[k].at[diagonal, rows], sems)))
        return [], remote

    copies.n_arrays = n
    return copies


def _forward_copies(n, own_core_last=False):
    def copies(srcs, lands, send_sems, recv_sems, local_sems):
        sibling = _peer(SIBLING)[0]
        remote = []
        for k in range(n):
            for rel in OTHER_CHIPS:
                dev, blk = _peer(rel)
                have = _slot(dev, blk, True, own_core_last)
                there = _slot(dev, blk, False, own_core_last)
                other = _peer(rel | SIBLING)
                comes = _slot(other[0], other[1], False, own_core_last)
                sems = dict(send_sem=send_sems.at[_sem_index(k, rel)], recv_sem=recv_sems.at[_sem_index(k, rel)],
                            device_id=sibling, device_id_type=MESH)
                remote.append((_remote_copy(lands[k].at[have], lands[k].at[there], sems),
                               _remote_copy(lands[k].at[have], lands[k].at[comes], sems)))
        return [], remote

    copies.n_arrays = n
    return copies


def _scatter_copies(n, layer):
    def copies(srcs, lands, send_sems, recv_sems, local_sems):
        me = _me()[3]
        local, remote = [], []
        for k in range(n):
            local.append(functools.partial(pltpu.make_async_copy, srcs[k].at[me], lands[k].at[layer, me],
                                           local_sems.at[k]))
            for rel in range(1, N_DEV):
                dev, blk = _peer(rel)
                sems = dict(send_sem=send_sems.at[_sem_index(k, rel)], recv_sem=recv_sems.at[_sem_index(k, rel)],
                            device_id=dev, device_id_type=MESH)
                remote.append((_remote_copy(srcs[k].at[blk], lands[k].at[layer, me], sems),
                               _remote_copy(srcs[k].at[blk], lands[k].at[layer, blk], sems)))
        return local, remote

    copies.n_arrays = n
    return copies


def _swap_copies(n):
    def copies(srcs, lands, send_sems, recv_sems, local_sems):
        sibling = _peer(SIBLING)[0]
        remote = []
        for k in range(n):
            sems = dict(send_sem=send_sems.at[_sem_index(k, SIBLING)], recv_sem=recv_sems.at[_sem_index(k, SIBLING)],
                        device_id=sibling, device_id_type=MESH)
            remote.append((_remote_copy(srcs[k], lands[k], sems), _remote_copy(srcs[k], lands[k], sems)))
        return [], remote

    copies.n_arrays = n
    return copies


def _pair_copies(n):
    def copies(srcs, lands, send_sems, recv_sems, local_sems):
        c = _me()[2]
        sibling = _peer(SIBLING)[0]
        remote = []
        for k in range(n):
            for chip in range(N_CHIPS):
                sems = dict(send_sem=send_sems.at[_sem_index(k, chip + 1)], recv_sem=recv_sems.at[_sem_index(k, chip + 1)],
                            device_id=sibling, device_id_type=MESH)
                block = srcs[k].at[2 * chip + 1 - c]
                remote.append((_remote_copy(block, lands[k].at[chip], sems), _remote_copy(block, lands[k].at[chip], sems)))
        return [], remote

    copies.n_arrays = n
    return copies


def _chip_scatter_copies(n, layer):
    def copies(srcs, lands, send_sems, recv_sems, local_sems):
        x, y, _, _ = _me()
        my_chip = 2 * x + y
        local, remote = [], []
        for k in range(n):
            local.append(functools.partial(pltpu.make_async_copy, srcs[k].at[my_chip], lands[k].at[layer, my_chip],
                                           local_sems.at[k]))
            for rel in OTHER_CHIPS:
                dev = _peer(rel)[0]
                chip = 2 * dev[0] + dev[1]
                sems = dict(send_sem=send_sems.at[_sem_index(k, rel)], recv_sem=recv_sems.at[_sem_index(k, rel)],
                            device_id=dev, device_id_type=MESH)
                remote.append((_remote_copy(srcs[k].at[chip], lands[k].at[layer, my_chip], sems),
                               _remote_copy(srcs[k].at[chip], lands[k].at[layer, chip], sems)))
        return local, remote

    copies.n_arrays = n
    return copies


def _sem_shapes(n):
    return [pltpu.SemaphoreType.DMA((n * (N_DEV - 1),)), pltpu.SemaphoreType.DMA((n * (N_DEV - 1),)),
            pltpu.SemaphoreType.DMA((n,))]


def _sem_index(k, rel):
    return k * (N_DEV - 1) + rel - 1


def _exchange(copies, name, srcs, lands):
    n_src, n_land = len(srcs), len(lands)

    def body(*refs):
        src_refs = refs[:n_src]
        land_refs = refs[n_src + n_land:n_src + 2 * n_land]
        local, remote = copies(src_refs, land_refs, *refs[n_src + 2 * n_land:])
        local = [cp() for cp in local]
        sends = [send() for send, _ in remote]
        for cp in local + sends:
            cp.start()
        for send, (_, landing) in zip(sends, remote):
            send.wait_send()
            landing().wait_recv()
        for cp in local:
            cp.wait()

    return pl.pallas_call(
        body, name=name, in_specs=[HBM_SPEC] * (n_src + n_land), out_specs=[HBM_SPEC] * n_land,
        out_shape=[jax.ShapeDtypeStruct(b.shape, b.dtype) for b in lands],
        scratch_shapes=_sem_shapes(copies.n_arrays),
        input_output_aliases={n_src + k: k for k in range(n_land)},
        compiler_params=pltpu.CompilerParams(has_side_effects=True),
    )(*srcs, *lands)


def _hbm(arrays):
    return [pltpu.with_memory_space_constraint(a, pltpu.HBM) for a in arrays]


def _exchange_start(copies, name, srcs, lands, after):
    n_src, n_land, n_after = len(srcs), len(lands), len(after)
    n_data = n_src + n_land

    def body(*refs):
        outs = refs[n_data + n_after:]
        local, remote = copies(refs[:n_src], refs[n_src:n_data], *outs[:3])
        for cp in local:
            cp().start()
        for send, _ in remote:
            send().start()
        outs[-1][...] = jnp.zeros_like(outs[-1])

    res = pl.pallas_call(
        body, name=name, in_specs=[HBM_SPEC] * n_data + [ANY_SPEC] * n_after,
        out_specs=[SEM_SPEC] * 3 + [HBM_SPEC] * n_data + [pl.BlockSpec(memory_space=pltpu.VMEM)],
        out_shape=_sem_shapes(copies.n_arrays) + [pltpu.HBM(a.shape, a.dtype) for a in list(srcs) + list(lands)]
        + [jax.ShapeDtypeStruct((SUBLANES, LANES), F32)],
        input_output_aliases={k: 3 + k for k in range(n_data)},
        compiler_params=pltpu.CompilerParams(has_side_effects=DATAFLOW),
    )(*_hbm(list(srcs) + list(lands)), *after)
    return res[:3], res[3:3 + n_src], res[3 + n_src:3 + n_data], res[-1]


def _exchange_wait(copies, name, sems, srcs, lands, after):
    n_src, n_land, n_after = len(srcs), len(lands), len(after)
    n_data = n_src + n_land

    def body(*refs):
        local, remote = copies(refs[:n_src], refs[n_src:n_data], *refs[n_data:n_data + 3])
        for send, landing in remote:
            send().wait_send()
            landing().wait_recv()
        for cp in local:
            cp().wait()

    res = pl.pallas_call(
        body, name=name, in_specs=[HBM_SPEC] * n_data + [SEM_SPEC] * 3 + [ANY_SPEC] * n_after,
        out_specs=[HBM_SPEC] * n_data,
        out_shape=[pltpu.HBM(a.shape, a.dtype) for a in list(srcs) + list(lands)],
        input_output_aliases={k: k for k in range(n_data)},
        compiler_params=pltpu.CompilerParams(has_side_effects=DATAFLOW),
    )(*srcs, *lands, *sems, *after)
    return res[:n_src], res[n_src:]


def _block_diag(blocks, row_major):
    L, GB, g, P, N = blocks.shape
    eye = jnp.eye(g, dtype=blocks.dtype)
    if row_major:
        return jnp.einsum("lbgpn,gh->lbgphn", blocks, eye).reshape(L, GB, g * P, g * N)
    return jnp.einsum("lbgpn,gh->lbhngp", blocks, eye).reshape(L, GB, g * N, g * P)


def _diag_blocks(mat, g, P, N, row_major):
    GB = mat.shape[0]
    eye = jnp.eye(g, dtype=mat.dtype)
    if row_major:
        return jnp.einsum("bgphn,gh->bgpn", mat.reshape(GB, g, P, g, N), eye)
    return jnp.einsum("bhngp,gh->bgpn", mat.reshape(GB, g, N, g, P), eye)


def _pack(arrays, rows_multiple):
    flat = jnp.concatenate([a.reshape(-1).astype(F32) for a in arrays])
    pad = (-flat.shape[0]) % (rows_multiple * LANES)
    return jnp.pad(flat, (0, pad)).reshape(-1, LANES)


def _unpack(packed, shapes):
    flat = packed.reshape(-1)
    out, pos = [], 0
    for s in shapes:
        n = math.prod(s)
        out.append(flat[pos:pos + n].reshape(s))
        pos += n
    return out


SMALL = ["ln1_g", "ln1_b", "s5_lam_re", "s5_lam_im", "s5_log_dt", "s5_b_re", "s5_b_im", "s5_c_re", "s5_c_im", "s5_d",
         "conv_b", "g_s5", "g_conv", "ln2_g", "ln2_b", "ln3_g", "ln3_b"]
S5_B = ["s5_b_re", "s5_b_im"]
WEIGHTS = ["ffn1_gate", "ffn1_up", "ffn1_down", "ln1_g", "ln1_b", "w_in", "s5_lam_re", "s5_lam_im", "s5_log_dt",
           "s5_b_re", "s5_b_im", "s5_c_re", "s5_c_im", "s5_d", "s5_w_glu", "conv_w", "conv_b", "g_s5", "g_conv",
           "w_out", "ln2_g", "ln2_b", "ffn2_gate", "ffn2_up", "ffn2_down", "ln3_g", "ln3_b"]
TRANSPOSED = ["ffn1_gate", "ffn1_up", "w_in", "ffn2_gate", "ffn2_up"]
UPDATED_TRANSPOSED = ["ffn1_gate", "ffn1_up", "ffn2_gate", "ffn2_up"]
GROUPS = {"a": ["ffn1_gate", "ffn1_up", "ffn1_down"], "b": ["w_in", "s5_w_glu", "w_out"],
          "c": ["ffn2_gate", "ffn2_up", "ffn2_down"]}
FFN_GROUPS = ("a", "c")


def _train_step(x, target, w, m, v):
    T, D = x.shape
    L = w["ln1_g"].shape[0]
    alpha = (2.0 * L) ** 0.25
    G = w["s5_log_dt"].shape[1]
    d_s5 = G * S5_P
    d_conv = w["conv_b"].shape[1]
    GB = G // S5_GROUPS_PER_BLOCK
    me = _me()[3]

    def shard(n, l):
        return (jnp.swapaxes(w[n][l], 0, 1) if n in TRANSPOSED else w[n][l]).astype(BF16)

    conv_w_rows = jnp.pad(w["conv_w"], ((0, 0), (0, SUBLANES - CONV_W), (0, 0)))
    (conv_w_all,) = _exchange(_gather_copies(1), "gather_conv_w", [conv_w_rows],
                              [lax.empty((N_DEV,) + conv_w_rows.shape, F32)])
    parts = [(l, grp) for l in range(L) for grp in GROUPS]
    step_one, relayed, step_two = {}, {}, {}

    def start_one(i, after):
        if i >= len(parts):
            return []
        l, grp = parts[i]
        srcs = [shard(n, l) for n in GROUPS[grp]]
        lands = [lax.empty((N_DEV,) + a.shape, a.dtype) for a in srcs]
        sems, srcs, lands, tok = _exchange_start(_gather_copies(len(srcs), SIBLING_AND_NEIGHBOURS, grp in FFN_GROUPS),
                                                 f"gather_start_{l}{grp}", srcs, lands, after)
        step_one[l, grp] = (sems, srcs, lands)
        return [tok]

    def relay_on(i, after):
        if i >= len(parts):
            return []
        l, grp = parts[i]
        sems, srcs, lands = step_one[l, grp]
        copies = _gather_copies(len(srcs), SIBLING_AND_NEIGHBOURS, grp in FFN_GROUPS)
        _, lands = _exchange_wait(copies, f"gather_wait_{l}{grp}", sems, srcs, lands, after)
        sems, _, lands, tok = _exchange_start(_relay_copies(len(lands), grp in FFN_GROUPS), f"relay_start_{l}{grp}",
                                              [], lands, [])
        relayed[l, grp] = (sems, lands)
        return [tok]

    def forward_on(i, after):
        if i >= len(parts):
            return []
        l, grp = parts[i]
        sems, lands = relayed[l, grp]
        _, lands = _exchange_wait(_relay_copies(len(lands), grp in FFN_GROUPS), f"relay_wait_{l}{grp}", sems, [], lands,
                                  after)
        sems, _, lands, tok = _exchange_start(_forward_copies(len(lands), grp in FFN_GROUPS), f"forward_start_{l}{grp}",
                                              [], lands, [])
        step_two[l, grp] = (sems, lands)
        return [tok]

    def gathered(i, after):
        l, grp = parts[i]
        sems, lands = step_two[l, grp]
        _, lands = _exchange_wait(_forward_copies(len(lands), grp in FFN_GROUPS), f"forward_wait_{l}{grp}", sems, [],
                                  lands, after)
        full = {n: p.reshape(-1, p.shape[-1]) for n, p in zip(GROUPS[grp], lands)}
        if grp == "b":
            full["conv_w"] = jnp.swapaxes(conv_w_all[:, l, :CONV_W, :], 0, 1).reshape(CONV_W, d_conv)
        return full

    lre = w["s5_lam_re"].reshape(L * G, S5_N)
    lim = w["s5_lam_im"].reshape(L * G, S5_N)
    ldt = w["s5_log_dt"].reshape(L * G, 1)
    b_re = jnp.transpose(w["s5_b_re"], (3, 0, 1, 2)).reshape(S5_P, L * G, S5_N)
    b_im = jnp.transpose(w["s5_b_im"], (3, 0, 1, 2)).reshape(S5_P, L * G, S5_N)
    ab_re, ab_im, bb_re, bb_im = _s5_params_fwd(lre, lim, ldt, b_re, b_im)

    def groups(bb):
        return jnp.transpose(bb.reshape(S5_P, L, GB, S5_GROUPS_PER_BLOCK, S5_N), (1, 2, 3, 0, 4))

    bd_re = _block_diag(groups(bb_re), True).astype(BF16)
    bd_im = _block_diag(groups(bb_im), True).astype(BF16)
    c_shape = (L, GB, S5_GROUPS_PER_BLOCK, S5_P, S5_N)
    cd_re = _block_diag(w["s5_c_re"].reshape(c_shape), False).astype(BF16)
    cd_im = _block_diag(w["s5_c_im"].reshape(c_shape), False).astype(BF16)
    a_re = ab_re.reshape(L, 1, G * S5_N)
    a_im = ab_im.reshape(L, 1, G * S5_N)
    d_vec = w["s5_d"].reshape(L, 1, d_s5)

    def vec(name, l):
        return w[name][l].reshape(1, -1)

    saved, weights = [], []
    x_in, x_in_b = x, x.astype(BF16)
    def advance(i, after):
        if i == 0:
            tokens = relay_on(1, after)
            return tokens + start_one(3, tokens)
        tokens = forward_on(i + 1, after)
        tokens = tokens + relay_on(i + 2, tokens or after)
        return tokens + start_one(i + 3, tokens or after)

    token = start_one(0, [])
    token = relay_on(0, token)
    token = start_one(1, token)
    token = start_one(2, token)
    token = forward_on(0, token)
    for l in range(L):
        i = len(GROUPS) * l
        gw = gathered(i, [x_in] if l else token)
        s = {"x0b": x_in_b}
        s["g1"], s["u1"], s["h1"] = _ffn_up(x_in_b, gw["ffn1_gate"], gw["ffn1_up"])
        x1, s["x1b"], s["xh1"], s["rstd1"] = _mm_res_ln(s["h1"], gw["ffn1_down"], x_in, vec("ln1_g", l),
                                                         vec("ln1_b", l), 0.5, alpha, advance(i, [s["h1"]]))
        behind = [s["x1b"]]
        if l == 0:
            behind = forward_on(1, behind)
            behind = behind + relay_on(2, behind)
        gw.update(gathered(i + 1, behind))
        s["proj"] = _mm_nt(s["x1b"], gw["w_in"])
        s["y"], s["sr"], s["si"] = _s5_fwd(s["proj"], bd_re[l], bd_im[l], cd_re[l], cd_im[l], a_re[l], a_im[l],
                                           d_vec[l])
        s["z"] = _conv_fwd(s["proj"], gw["conv_w"], vec("conv_b", l), d_s5, d_conv)
        s["mcat"], s["gl"] = _mix_post(s["y"], s["z"], gw["s5_w_glu"], vec("g_s5", l), vec("g_conv", l))
        x2, s["x2b"], s["xh2"], s["rstd2"] = _mm_res_ln(s["mcat"], gw["w_out"], x1, vec("ln2_g", l),
                                                         vec("ln2_b", l), 1.0, alpha, advance(i + 1, [s["mcat"]]))
        gw.update(gathered(i + 2, [s["x2b"]]))
        s["g2"], s["u2"], s["h2"] = _ffn_up(s["x2b"], gw["ffn2_gate"], gw["ffn2_up"])
        x3, x3b, s["xh3"], s["rstd3"] = _mm_res_ln(s["h2"], gw["ffn2_down"], x2, vec("ln3_g", l), vec("ln3_b", l),
                                                   0.5, alpha, advance(i + 2, [s["h2"]]))
        saved.append(s)
        weights.append(gw)
        x_in, x_in_b = x3, x3b

    last = saved[L - 1]
    dr, drb, dg, db, loss = _loss_ln_bwd(x_in, target, last["xh3"], last["rstd3"], vec("ln3_g", L - 1))
    small = [dict() for _ in range(L)]
    small[L - 1]["ln3_g"], small[L - 1]["ln3_b"] = dg, db
    bufs = {grp: [lax.empty((L, N_CHIPS) + shard(n, 0).shape, BF16) for n in names] for grp, names in GROUPS.items()}
    scatters = {grp: [] for grp in GROUPS}
    grad_x = None

    def chip_scatter(l, grp, sums):
        sems, sums, bufs[grp], tok = _exchange_start(_chip_scatter_copies(len(sums), l), f"scatter_start_{l}{grp}", sums,
                                                     bufs[grp], [])
        scatters[grp].append((l, sems, sums))
        return [tok]

    small_names = SMALL + ["conv_w"]
    b_shape = (L, G, S5_P, S5_N)
    small_shapes = [b_shape if n in S5_B else w[n].shape for n in SMALL] + [(L, CONV_W, d_conv)]

    def small_scatter_start(small):
        def stack(key):
            return jnp.stack([small[l][key] for l in range(L)])

        d_bb_re = jnp.transpose(stack("d_bb_re").reshape(L * G, S5_P, S5_N), (1, 0, 2))
        d_bb_im = jnp.transpose(stack("d_bb_im").reshape(L * G, S5_P, S5_N), (1, 0, 2))
        g_lre, g_lim, g_ldt, g_bre, g_bim = _s5_params_bwd(
            lre, lim, ldt, b_re, b_im, stack("d_ab_re").reshape(L * G, S5_N), stack("d_ab_im").reshape(L * G, S5_N),
            d_bb_re, d_bb_im)
        part = {n: [small[l][n] for l in range(L)]
                for n in ["ln1_g", "ln1_b", "s5_c_re", "s5_c_im", "s5_d", "conv_b", "g_s5", "g_conv", "ln2_g", "ln2_b",
                          "ln3_g", "ln3_b", "conv_w"]}
        part["s5_lam_re"], part["s5_lam_im"], part["s5_log_dt"] = [g_lre], [g_lim], [g_ldt]
        part["s5_b_re"] = [jnp.transpose(g_bre, (1, 0, 2))]
        part["s5_b_im"] = [jnp.transpose(g_bim, (1, 0, 2))]
        packed = _pack([piece for n in small_names for piece in part[n]], N_DEV * PACK_ROWS)
        rows = packed.shape[0] // N_DEV
        return _exchange_start(_scatter_copies(1, 0), "scatter_small_start", [packed.reshape(N_DEV, rows, LANES)],
                               [lax.empty((1, N_DEV, rows, LANES), F32)], [])

    token = []
    for l in reversed(range(L)):
        gw, s, sm = weights[l], saved[l], small[l]
        full = {}

        def ffn_bwd(dr, drb, tag, grp, xb_in, ln, after):
            dgp, dup = _ffn_down_bwd(drb, gw[f"ffn{tag}_down"], s[f"g{tag}"], s[f"u{tag}"], after)
            terms = {"gate": (dgp, xb_in, 1.0), "up": (dup, xb_in, 1.0), "down": (s[f"h{tag}"], drb, 0.5)}
            terms = [terms[n.split("_")[1]] for n in GROUPS[grp]]
            away = [_mm_tn(a, b, scale, BF16, half=0).reshape((N_CHIPS, -1, b.shape[1])) for a, b, scale in terms]
            stages = [lax.empty(a.shape, BF16) for a in away]
            sems, away, stages, tok = _exchange_start(_swap_copies(len(away)), f"pair_start_{l}{grp}", away, stages, [])
            res = _mm_dx([(dgp, gw[f"ffn{tag}_gate"]), (dup, gw[f"ffn{tag}_up"])], dr, alpha, ln, [tok])
            _, stages = _exchange_wait(_swap_copies(len(away)), f"pair_wait_{l}{grp}", sems, away, stages, [res[0]])
            sums = [_mm_tn(a, b, scale, BF16, half=1, addend=st.reshape(-1, b.shape[1])).reshape(st.shape)
                    for (a, b, scale), st in zip(terms, stages)]
            return res, chip_scatter(l, grp, sums)

        (dr, drb, sm["ln2_g"], sm["ln2_b"]), token = ffn_bwd(dr, drb, 2, "c", s["x2b"],
                                                              (s["xh2"], s["rstd2"], vec("ln2_g", l)), token)
        dm = _mm_nt(drb, gw["w_out"], token)
        full["w_out"] = _mm_tn(s["mcat"], drb, 1.0, BF16)
        dy, dz, dwglu, sm["g_s5"], sm["g_conv"] = _mix_post_bwd(dm, s["y"], s["gl"], s["z"], gw["s5_w_glu"],
                                                                vec("g_s5", l), vec("g_conv", l))
        full["s5_w_glu"] = dwglu.astype(BF16)
        du, dbd_re, dbd_im, dcd_re, dcd_im, sm["d_ab_re"], sm["d_ab_im"], sm["s5_d"] = _s5_bwd(
            dy, s["proj"], s["sr"], s["si"], bd_re[l], bd_im[l], cd_re[l], cd_im[l], a_re[l], a_im[l], d_vec[l])
        gsz = (S5_GROUPS_PER_BLOCK, S5_P, S5_N)
        sm["d_bb_re"] = _diag_blocks(dbd_re, *gsz, True)
        sm["d_bb_im"] = _diag_blocks(dbd_im, *gsz, True)
        sm["s5_c_re"] = _diag_blocks(dcd_re, *gsz, False).reshape(G, S5_P, S5_N)
        sm["s5_c_im"] = _diag_blocks(dcd_im, *gsz, False).reshape(G, S5_P, S5_N)
        dgb, dgc, dhh, sm["conv_w"], sm["conv_b"] = _conv_bwd(dz, s["proj"], gw["conv_w"], vec("conv_b", l),
                                                              d_s5, d_conv)
        dproj = jnp.concatenate([du.astype(BF16), dgb, dgc, dhh], axis=1)
        full["w_in"] = _mm_tn(dproj, s["x1b"], 1.0, BF16)
        fulls = [full[n].reshape((N_DEV, -1) + full[n].shape[1:]) for n in GROUPS["b"]]
        stages = [lax.empty((N_CHIPS,) + f.shape[1:], BF16) for f in fulls]
        sems, fulls, stages, tok = _exchange_start(_pair_copies(len(fulls)), f"pair_start_{l}b", fulls, stages, [])
        dr, drb, sm["ln1_g"], sm["ln1_b"] = _mm_dx([(dproj, gw["w_in"])], dr, alpha,
                                                   (s["xh1"], s["rstd1"], vec("ln1_g", l)), [tok])
        fulls, stages = _exchange_wait(_pair_copies(len(fulls)), f"pair_wait_{l}b", sems, fulls, stages, [dr])
        token = chip_scatter(l, "b", [_pair_sum(f, st) for f, st in zip(fulls, stages)])
        if l > 0:
            prev = saved[l - 1]
            (dr, drb, small[l - 1]["ln3_g"], small[l - 1]["ln3_b"]), token = ffn_bwd(
                dr, drb, 1, "a", s["x0b"], (prev["xh3"], prev["rstd3"], vec("ln3_g", l - 1)), token)
        else:
            small_sems, small_srcs, small_lands, tok = small_scatter_start(small)
            ((grad_x,), token) = ffn_bwd(dr, drb, 1, "a", s["x0b"], None, token + [tok])

    _, landed = _exchange_wait(_scatter_copies(1, 0), "scatter_small_wait", small_sems, small_srcs, small_lands,
                               [grad_x])
    mine = _reduce_parts(landed[0])
    (summed,) = _exchange(_gather_copies(1), "gather_small", [mine[0]], [lax.empty(landed[0].shape[1:], F32)])
    small_grads = dict(zip(small_names, _unpack(summed, small_shapes)))

    out = {}

    def update(name, w3, m3, v3, shape, **grad):
        res = _adamw_update(w3, m3, v3, **grad)
        out[name] = [r.reshape(shape) for r in res]
        return res

    cw_shape = w["conv_w"].shape
    g_cw = lax.dynamic_slice_in_dim(small_grads["conv_w"], me * cw_shape[2], cw_shape[2], axis=2)
    done = [update("conv_w", w["conv_w"], m["conv_w"], v["conv_w"], cw_shape, grad=g_cw)[3]]
    for n in SMALL:
        if n in S5_B:
            def view(a):
                return jnp.swapaxes(a, 2, 3).reshape(1, -1, S5_N)
            res = _adamw_update(view(w[n]), view(m[n]), view(v[n]), grad=small_grads[n].reshape(1, -1, S5_N))
            out[n] = [jnp.swapaxes(r.reshape(b_shape), 2, 3) for r in res]
        else:
            res = update(n, *(a[n].reshape(1, -1, a[n].shape[-1]) for a in (w, m, v)), w[n].shape,
                         grad=small_grads[n].reshape(1, -1, w[n].shape[-1]))
        done.append(res[3])
    for grp in ("c", "b", "a"):
        for l, sems, sums in scatters[grp]:
            _, bufs[grp] = _exchange_wait(_chip_scatter_copies(len(sums), l), f"scatter_wait_{l}{grp}", sems, sums,
                                          bufs[grp], [grad_x] + token + done)
        done = []
        for n, parts in zip(GROUPS[grp], bufs[grp]):
            if n in UPDATED_TRANSPOSED:
                res = _adamw_update(*(jnp.swapaxes(a[n], 1, 2) for a in (w, m, v)), parts=parts)
                out[n] = [jnp.swapaxes(r, 1, 2) for r in res]
            elif n in TRANSPOSED:
                res = update(n, w[n], m[n], v[n], w[n].shape, grad=jnp.swapaxes(_reduce_parts(parts), 1, 2))
            else:
                res = update(n, w[n], m[n], v[n], w[n].shape, parts=parts)
            done.append(res[3])

    loss = lax.psum(loss[0, 0], AXES)
    return loss, grad_x, out


def kernel(x, ffn1_gate, ffn1_up, ffn1_down, ln1_g, ln1_b, w_in, s5_lam_re, s5_lam_im, s5_log_dt, s5_b_re, s5_b_im, s5_c_re, s5_c_im, s5_d, s5_w_glu, conv_w, conv_b, g_s5, g_conv, w_out, ln2_g, ln2_b, ffn2_gate, ffn2_up, ffn2_down, ln3_g, ln3_b, loss_target, m_ffn1_gate, m_ffn1_up, m_ffn1_down, m_ln1_g, m_ln1_b, m_w_in, m_s5_lam_re, m_s5_lam_im, m_s5_log_dt, m_s5_b_re, m_s5_b_im, m_s5_c_re, m_s5_c_im, m_s5_d, m_s5_w_glu, m_conv_w, m_conv_b, m_g_s5, m_g_conv, m_w_out, m_ln2_g, m_ln2_b, m_ffn2_gate, m_ffn2_up, m_ffn2_down, m_ln3_g, m_ln3_b, v_ffn1_gate, v_ffn1_up, v_ffn1_down, v_ln1_g, v_ln1_b, v_w_in, v_s5_lam_re, v_s5_lam_im, v_s5_log_dt, v_s5_b_re, v_s5_b_im, v_s5_c_re, v_s5_c_im, v_s5_d, v_s5_w_glu, v_conv_w, v_conv_b, v_g_s5, v_g_conv, v_w_out, v_ln2_g, v_ln2_b, v_ffn2_gate, v_ffn2_up, v_ffn2_down, v_ln3_g, v_ln3_b):
    given = dict(locals())
    w = {n: given[n] for n in WEIGHTS}
    m = {n: given["m_" + n] for n in WEIGHTS}
    v = {n: given["v_" + n] for n in WEIGHTS}
    T, D = x.shape[-2:]
    loss, grad_x, out = _train_step(x.reshape(T, D), loss_target.reshape(T, D), w, m, v)
    results = [loss, grad_x.reshape(x.shape)]
    for i in range(4):
        results += [out[n][i] for n in WEIGHTS]
    return tuple(results)
```

```python
import functools
import math

import jax
import jax.numpy as jnp
from jax import lax
from jax.experimental import pallas as pl
from jax.experimental.pallas import tpu as pltpu

F32 = jnp.float32
BF16 = jnp.bfloat16
MESH = pl.DeviceIdType.MESH
AXES = ("x", "y", "c")
N_DEV = 8
N_CHIPS = 4

S5_P = 16
S5_N = 64
CONV_W = 3
LN_EPS = 1e-5
RMS_EPS = 1e-6
ADAM_LR = 0.001
ADAM_B1 = 0.9
ADAM_B2 = 0.999
ADAM_EPS = 1e-08
ADAM_WD = 0.01
ADAM_STEP = 10

V7X_VMEM_BYTES = 64 * 1024 * 1024
VMEM_LIMIT = V7X_VMEM_BYTES * 7 // 8
LANES = 128
SUBLANES = 8
BF16_ROWS = 16
MXU_COLS = 256
ROWS_RESIDENT = 2048
PACK_ROWS = 512
S5_GROUPS_PER_BLOCK = LANES // S5_P
S5_STATE_BLOCK = S5_GROUPS_PER_BLOCK * S5_N

HBM_SPEC = pl.BlockSpec(memory_space=pltpu.HBM)
SEM_SPEC = pl.BlockSpec(memory_space=pltpu.SEMAPHORE)
ANY_SPEC = pl.BlockSpec(memory_space=pl.ANY)


def _tile(n, pref, align):
    best = None
    d = align
    while d <= min(n, pref):
        if n % d == 0:
            best = d
        d += align
    return best if best is not None else n


def _params(*sem):
    return pltpu.CompilerParams(dimension_semantics=sem, vmem_limit_bytes=VMEM_LIMIT)


def _dot_nn(a, b):
    return lax.dot_general(a, b, (((1,), (0,)), ((), ())), preferred_element_type=F32)


def _dot_nt(a, b):
    return lax.dot_general(a, b, (((1,), (1,)), ((), ())), preferred_element_type=F32)


def _dot_tn(a, b):
    return lax.dot_general(a, b, (((0,), (0,)), ((), ())), preferred_element_type=F32)


def _colsum(v):
    return jnp.sum(v, axis=0, keepdims=True)


def _rowmean(v):
    return jnp.mean(v, axis=-1, keepdims=True)


def _ffn_up(xb, wg, wu):
    T, D = xb.shape
    F = wg.shape[0]
    tm = _tile(T, ROWS_RESIDENT, 16)
    tn = _tile(F, MXU_COLS, LANES)

    def body(x_ref, wg_ref, wu_ref, cg_ref, cu_ref, h_ref):
        x = x_ref[...]
        g = _dot_nt(x, wg_ref[...])
        u = _dot_nt(x, wu_ref[...])
        sg = jax.nn.sigmoid(g)
        silu = g * sg
        cu_ref[...] = silu.astype(BF16)
        cg_ref[...] = (u * (sg * (1.0 + g * (1.0 - sg)))).astype(BF16)
        h_ref[...] = (silu * u).astype(BF16)

    w_spec = pl.BlockSpec((tn, D), lambda j, i: (j, 0))
    o_spec = pl.BlockSpec((tm, tn), lambda j, i: (i, j))
    return pl.pallas_call(
        body, name="ffn_up", grid=(F // tn, T // tm),
        in_specs=[pl.BlockSpec((tm, D), lambda j, i: (i, 0)), w_spec, w_spec],
        out_specs=[o_spec, o_spec, o_spec],
        out_shape=[jax.ShapeDtypeStruct((T, F), BF16)] * 3,
        compiler_params=_params("arbitrary", "arbitrary"),
    )(xb, wg, wu)


def _mm_acc(pairs, after=()):
    T, K = pairs[0][0].shape
    D = pairs[0][1].shape[1]
    n = len(pairs)
    tk = _tile(K, 512, LANES)
    tm = _tile(T, 512, 16)

    def body(*refs):
        o_ref = refs[-1]

        @pl.when(pl.program_id(0) == 0)
        def _():
            o_ref[...] = jnp.zeros_like(o_ref)

        for r in range(0, T, tm):
            part = _dot_nn(refs[0][r:r + tm, :], refs[1][...])
            for a_ref, w_ref in zip(refs[2:2 * n:2], refs[3:2 * n:2]):
                part += _dot_nn(a_ref[r:r + tm, :], w_ref[...])
            o_ref[r:r + tm, :] += part

    in_specs, operands = [], []
    for a, w in pairs:
        in_specs += [pl.BlockSpec((T, tk), lambda k: (0, k)), pl.BlockSpec((tk, D), lambda k: (k, 0))]
        operands += [a, w]
    return pl.pallas_call(
        body, name="mm_acc", grid=(K // tk,),
        in_specs=in_specs + [ANY_SPEC] * len(after),
        out_specs=pl.BlockSpec((T, D), lambda k: (0, 0)),
        out_shape=jax.ShapeDtypeStruct((T, D), F32),
        compiler_params=_params("arbitrary"),
    )(*operands, *after)


def _mm_res_ln(a, w, res, g, b, scale, alpha, after=()):
    acc = _mm_acc([(a, w)], after)
    T, D = acc.shape
    tm = _tile(T, 256, 16)

    def body(acc_ref, res_ref, g_ref, b_ref, xo_ref, xb_ref, xh_ref, rstd_ref):
        r = alpha * res_ref[...] + scale * acc_ref[...]
        xc = r - _rowmean(r)
        rstd = lax.rsqrt(_rowmean(xc * xc) + LN_EPS)
        xh = xc * rstd
        xo = xh * g_ref[...] + b_ref[...]
        xo_ref[...] = xo
        xb_ref[...] = xo.astype(BF16)
        xh_ref[...] = xh
        rstd_ref[...] = rstd

    row = pl.BlockSpec((tm, D), lambda i: (i, 0))
    vec = pl.BlockSpec((1, D), lambda i: (0, 0))
    return pl.pallas_call(
        body, name="res_ln", grid=(T // tm,),
        in_specs=[row, row, vec, vec],
        out_specs=[row, row, row, pl.BlockSpec((tm, 1), lambda i: (i, 0))],
        out_shape=[jax.ShapeDtypeStruct((T, D), F32), jax.ShapeDtypeStruct((T, D), BF16),
                   jax.ShapeDtypeStruct((T, D), F32), jax.ShapeDtypeStruct((T, 1), F32)],
        compiler_params=_params("arbitrary"),
    )(acc, res, g, b)


def _mm_nt(a, w, after=()):
    M, K = a.shape
    N = w.shape[0]
    tm = _tile(M, ROWS_RESIDENT, 16)
    tn = _tile(N, MXU_COLS, LANES)

    def body(a_ref, w_ref, *rest):
        rest[-1][...] = _dot_nt(a_ref[...], w_ref[...])

    return pl.pallas_call(
        body, name="mm_nt", grid=(N // tn, M // tm),
        in_specs=[pl.BlockSpec((tm, K), lambda j, i: (i, 0)), pl.BlockSpec((tn, K), lambda j, i: (j, 0))]
        + [ANY_SPEC] * len(after),
        out_specs=pl.BlockSpec((tm, tn), lambda j, i: (i, j)),
        out_shape=jax.ShapeDtypeStruct((M, N), F32),
        compiler_params=_params("arbitrary", "arbitrary"),
    )(a, w, *after)


def _mm_tn(a, b, scale, out_dtype, half=None, addend=None):
    T, M = a.shape
    N = b.shape[1]
    rows = M if half is None else M // 2
    tm = _tile(rows, 512, LANES)
    tn = _tile(N, ROWS_RESIDENT, LANES)
    if tm < 512:
        tm = _tile(rows, 1408, LANES)
        tn = _tile(N, ROWS_RESIDENT // 2, LANES)
    first = 0 if half is None else half * (rows // tm)

    def body(a_ref, b_ref, *rest):
        out = scale * _dot_tn(a_ref[...], b_ref[...])
        if addend is not None:
            out = out + rest[0][...].astype(F32)
        rest[-1][...] = out.astype(out_dtype)

    o_spec = pl.BlockSpec((tm, tn), lambda i, j: (i, j))
    return pl.pallas_call(
        body, name="mm_tn", grid=(rows // tm, N // tn),
        in_specs=[pl.BlockSpec((T, tm), lambda i, j: (0, first + i)), pl.BlockSpec((T, tn), lambda i, j: (0, j))]
        + ([] if addend is None else [o_spec]),
        out_specs=o_spec,
        out_shape=jax.ShapeDtypeStruct((rows, N), out_dtype),
        compiler_params=_params("arbitrary", "arbitrary"),
    )(a, b, *([] if addend is None else [addend]))


def _ln_bwd(dy, xh, rstd, g):
    dxh = dy * g
    dr = rstd * (dxh - _rowmean(dxh) - xh * _rowmean(dxh * xh))
    return dr, _colsum(dy * xh), _colsum(dy)


def _loss_ln_bwd(y, target, xh, rstd, g):
    T, D = y.shape
    tm = _tile(T, 256, 16)

    def body(y_ref, t_ref, xh_ref, rstd_ref, g_ref, dr_ref, drb_ref, dg_ref, db_ref, loss_ref):
        i = pl.program_id(0)

        @pl.when(i == 0)
        def _():
            dg_ref[...] = jnp.zeros_like(dg_ref)
            db_ref[...] = jnp.zeros_like(db_ref)
            loss_ref[...] = jnp.zeros_like(loss_ref)

        err = y_ref[...] - t_ref[...]
        loss_ref[...] += (0.5 / D) * _colsum(jnp.sum(err * err, axis=1, keepdims=True))
        dr, dg, db = _ln_bwd(err * (1.0 / D), xh_ref[...], rstd_ref[...], g_ref[...])
        dr_ref[...] = dr
        drb_ref[...] = dr.astype(BF16)
        dg_ref[...] += dg
        db_ref[...] += db

    row = pl.BlockSpec((tm, D), lambda i: (i, 0))
    vec = pl.BlockSpec((1, D), lambda i: (0, 0))
    return pl.pallas_call(
        body, name="loss_ln_bwd", grid=(T // tm,),
        in_specs=[row, row, row, pl.BlockSpec((tm, 1), lambda i: (i, 0)), vec],
        out_specs=[row, row, vec, vec, pl.BlockSpec((1, 1), lambda i: (0, 0))],
        out_shape=[jax.ShapeDtypeStruct((T, D), F32), jax.ShapeDtypeStruct((T, D), BF16),
                   jax.ShapeDtypeStruct((1, D), F32), jax.ShapeDtypeStruct((1, D), F32),
                   jax.ShapeDtypeStruct((1, 1), F32)],
        compiler_params=_params("arbitrary"),
    )(y, target, xh, rstd, g)


def _ffn_down_bwd(drb, wd, cg, cu, after=()):
    T, D = drb.shape
    F = wd.shape[0]
    tm = _tile(T, ROWS_RESIDENT, 16)
    tn = _tile(F, MXU_COLS, LANES)

    def body(dr_ref, wd_ref, cg_ref, cu_ref, *rest):
        dg_ref, du_ref = rest[len(after):]
        dh = 0.5 * _dot_nt(dr_ref[...], wd_ref[...])
        du_ref[...] = (dh * cu_ref[...].astype(F32)).astype(BF16)
        dg_ref[...] = (dh * cg_ref[...].astype(F32)).astype(BF16)

    t_spec = pl.BlockSpec((tm, tn), lambda j, i: (i, j))
    return pl.pallas_call(
        body, name="ffn_down_bwd", grid=(F // tn, T // tm),
        in_specs=[pl.BlockSpec((tm, D), lambda j, i: (i, 0)), pl.BlockSpec((tn, D), lambda j, i: (j, 0)),
                  t_spec, t_spec] + [ANY_SPEC] * len(after),
        out_specs=[t_spec, t_spec],
        out_shape=[jax.ShapeDtypeStruct((T, F), BF16), jax.ShapeDtypeStruct((T, F), BF16)],
        compiler_params=_params("arbitrary", "arbitrary"),
    )(drb, wd, cg, cu, *after)


def _mm_dx(pairs, res, alpha, ln=None, after=()):
    acc = _mm_acc(pairs, after)
    T, D = acc.shape
    tm = _tile(T, 256, 16)
    with_ln = ln is not None

    def body(acc_ref, res_ref, *refs):
        dx = alpha * res_ref[...] + acc_ref[...]
        if with_ln:
            xh_ref, rstd_ref, g_ref, dr_ref, drb_ref, dg_ref, db_ref = refs

            @pl.when(pl.program_id(0) == 0)
            def _():
                dg_ref[...] = jnp.zeros_like(dg_ref)
                db_ref[...] = jnp.zeros_like(db_ref)

            dr, dg, db = _ln_bwd(dx, xh_ref[...], rstd_ref[...], g_ref[...])
            dr_ref[...] = dr
            drb_ref[...] = dr.astype(BF16)
            dg_ref[...] += dg
            db_ref[...] += db
        else:
            refs[0][...] = dx

    row = pl.BlockSpec((tm, D), lambda i: (i, 0))
    vec = pl.BlockSpec((1, D), lambda i: (0, 0))
    in_specs, operands = [row, row], [acc, res]
    if with_ln:
        in_specs += [row, pl.BlockSpec((tm, 1), lambda i: (i, 0)), vec]
        operands += list(ln)
        out_specs = [row, row, vec, vec]
        out_shape = [jax.ShapeDtypeStruct((T, D), F32), jax.ShapeDtypeStruct((T, D), BF16),
                     jax.ShapeDtypeStruct((1, D), F32), jax.ShapeDtypeStruct((1, D), F32)]
    else:
        out_specs = [row]
        out_shape = [jax.ShapeDtypeStruct((T, D), F32)]
    return pl.pallas_call(
        body, name="dx_ln_bwd" if with_ln else "dx_res", grid=(T // tm,),
        in_specs=in_specs, out_specs=out_specs, out_shape=out_shape,
        compiler_params=_params("arbitrary"),
    )(*operands)


def _s5_discretize(lre, lim, ldt, br, bi):
    dt = jnp.exp(ldt)
    mag = jnp.exp(lre * dt)
    ang = lim * dt
    ar = mag * jnp.cos(ang)
    ai = mag * jnp.sin(ang)
    den = lre * lre + lim * lim
    nr = ar - 1.0
    qr = (nr * lre + ai * lim) / den
    qi = (ai * lre - nr * lim) / den
    bbr = qr[None] * br - qi[None] * bi
    bbi = qr[None] * bi + qi[None] * br
    return ar, ai, bbr, bbi


def _s5_params_fwd(lre, lim, ldt, br, bi):
    def body(lre_ref, lim_ref, ldt_ref, br_ref, bi_ref, ar_ref, ai_ref, bbr_ref, bbi_ref):
        ar, ai, bbr, bbi = _s5_discretize(lre_ref[...], lim_ref[...], ldt_ref[...], br_ref[...], bi_ref[...])
        ar_ref[...] = ar
        ai_ref[...] = ai
        bbr_ref[...] = bbr
        bbi_ref[...] = bbi

    sds = jax.ShapeDtypeStruct
    return pl.pallas_call(
        body, name="s5_params_fwd",
        out_shape=[sds(lre.shape, F32), sds(lre.shape, F32), sds(br.shape, F32), sds(br.shape, F32)],
        compiler_params=pltpu.CompilerParams(vmem_limit_bytes=VMEM_LIMIT),
    )(lre, lim, ldt, br, bi)


def _s5_params_bwd(lre, lim, ldt, br, bi, dar, dai, dbbr, dbbi):
    def body(lre_ref, lim_ref, ldt_ref, br_ref, bi_ref, dar_ref, dai_ref, dbbr_ref, dbbi_ref,
             o_lre, o_lim, o_ldt, o_br, o_bi):
        _, vjp = jax.vjp(_s5_discretize, lre_ref[...], lim_ref[...], ldt_ref[...], br_ref[...], bi_ref[...])
        g = vjp((dar_ref[...], dai_ref[...], dbbr_ref[...], dbbi_ref[...]))
        o_lre[...] = g[0]
        o_lim[...] = g[1]
        o_ldt[...] = g[2]
        o_br[...] = g[3]
        o_bi[...] = g[4]

    sds = jax.ShapeDtypeStruct
    return pl.pallas_call(
        body, name="s5_params_bwd",
        out_shape=[sds(lre.shape, F32), sds(lre.shape, F32), sds(ldt.shape, F32), sds(br.shape, F32),
                   sds(br.shape, F32)],
        compiler_params=pltpu.CompilerParams(vmem_limit_bytes=VMEM_LIMIT),
    )(lre, lim, ldt, br, bi, dar, dai, dbbr, dbbi)


def _cmul(a_re, a_im, b_re, b_im):
    return a_re * b_re - a_im * b_im, a_re * b_im + a_im * b_re


def _s5_tile(i):
    if isinstance(i, int):
        return pl.ds(i * SUBLANES, SUBLANES)
    return pl.ds(pl.multiple_of(i * SUBLANES, SUBLANES), SUBLANES)


def _s5_powers(a_re, a_im, pwr_ref, pwi_ref, seg):
    def step(i, carry):
        p_re, p_im = carry
        pwr_ref[pl.ds(i, 1), :] = p_re
        pwi_ref[pl.ds(i, 1), :] = p_im
        return _cmul(a_re, a_im, p_re, p_im)

    lax.fori_loop(0, seg, step, (a_re, a_im))


def _s5_fwd(proj, bdr, bdi, cdr, cdi, ar, ai, dvec):
    T = proj.shape[0]
    GB, UB, SB = bdr.shape
    tc = _tile(T, 256, SUBLANES * SUBLANES)
    seg = tc // SUBLANES

    def body(u_ref, bdr_ref, bdi_ref, cdr_ref, cdi_ref, ar_ref, ai_ref, d_ref, y_ref, sr_ref, si_ref,
             cr_ref, ci_ref, pwr_ref, pwi_ref, str_ref, sti_ref, up_ref, yp_ref):
        a_re = ar_ref[...]
        a_im = ai_ref[...]

        @pl.when(pl.program_id(1) == 0)
        def _():
            cr_ref[...] = jnp.zeros_like(cr_ref)
            ci_ref[...] = jnp.zeros_like(ci_ref)
            _s5_powers(a_re, a_im, pwr_ref, pwi_ref, seg)

        for i in range(seg):
            up_ref[_s5_tile(i), :] = u_ref[pl.ds(i, SUBLANES, stride=seg), :]
        u = up_ref[...]
        ub = u.astype(BF16)
        sr_ref[...] = _dot_nn(ub, bdr_ref[...])
        si_ref[...] = _dot_nn(ub, bdi_ref[...])

        def local(i, carry):
            p_re, p_im = carry
            rows = _s5_tile(i)
            n_re, n_im = _cmul(a_re, a_im, p_re, p_im)
            n_re, n_im = n_re + sr_ref[rows, :], n_im + si_ref[rows, :]
            sr_ref[rows, :] = n_re
            si_ref[rows, :] = n_im
            return n_re, n_im

        e_re, e_im = lax.fori_loop(1, seg, local, (sr_ref[_s5_tile(0), :], si_ref[_s5_tile(0), :]))
        s_re, s_im = cr_ref[...], ci_ref[...]
        top_re, top_im = pwr_ref[seg - 1:seg, :], pwi_ref[seg - 1:seg, :]
        for j in range(SUBLANES):
            str_ref[j:j + 1, :] = s_re
            sti_ref[j:j + 1, :] = s_im
            n_re, n_im = _cmul(top_re, top_im, s_re, s_im)
            s_re, s_im = n_re + e_re[j:j + 1, :], n_im + e_im[j:j + 1, :]
        cr_ref[...] = s_re
        ci_ref[...] = s_im
        b_re, b_im = str_ref[...], sti_ref[...]

        def fix(i, carry):
            rows = _s5_tile(i)
            f_re, f_im = _cmul(pwr_ref[pl.ds(i, 1), :], pwi_ref[pl.ds(i, 1), :], b_re, b_im)
            sr_ref[rows, :] += f_re
            si_ref[rows, :] += f_im
            return carry

        lax.fori_loop(0, seg, fix, 0)
        yp_ref[...] = (_dot_nn(sr_ref[...].astype(BF16), cdr_ref[...])
                       - _dot_nn(si_ref[...].astype(BF16), cdi_ref[...]) + d_ref[...] * u)
        for i in range(seg):
            y_ref[pl.ds(i, SUBLANES, stride=seg), :] = yp_ref[_s5_tile(i), :]

    return pl.pallas_call(
        body, name="s5_fwd", grid=(GB, T // tc),
        in_specs=[pl.BlockSpec((tc, UB), lambda j, t: (t, j)),
                  pl.BlockSpec((None, UB, SB), lambda j, t: (j, 0, 0)),
                  pl.BlockSpec((None, UB, SB), lambda j, t: (j, 0, 0)),
                  pl.BlockSpec((None, SB, UB), lambda j, t: (j, 0, 0)),
                  pl.BlockSpec((None, SB, UB), lambda j, t: (j, 0, 0)),
                  pl.BlockSpec((1, SB), lambda j, t: (0, j)),
                  pl.BlockSpec((1, SB), lambda j, t: (0, j)),
                  pl.BlockSpec((1, UB), lambda j, t: (0, j))],
        out_specs=[pl.BlockSpec((tc, UB), lambda j, t: (t, j)),
                   pl.BlockSpec((tc, SB), lambda j, t: (t, j)),
                   pl.BlockSpec((tc, SB), lambda j, t: (t, j))],
        out_shape=[jax.ShapeDtypeStruct((T, GB * UB), F32), jax.ShapeDtypeStruct((T, GB * SB), F32),
                   jax.ShapeDtypeStruct((T, GB * SB), F32)],
        scratch_shapes=[pltpu.VMEM((1, SB), F32), pltpu.VMEM((1, SB), F32),
                        pltpu.VMEM((seg, SB), F32), pltpu.VMEM((seg, SB), F32),
                        pltpu.VMEM((SUBLANES, SB), F32), pltpu.VMEM((SUBLANES, SB), F32),
                        pltpu.VMEM((tc, UB), F32), pltpu.VMEM((tc, UB), F32)],
        compiler_params=_params("arbitrary", "arbitrary"),
    )(proj, bdr, bdi, cdr, cdi, ar, ai, dvec)


def _s5_bwd(dy, proj, sr, si, bdr, bdi, cdr, cdi, ar, ai, dvec):
    T = dy.shape[0]
    GB, UB, SB = bdr.shape
    tc = _tile(T, 256, SUBLANES * SUBLANES)
    seg = tc // SUBLANES
    nt = T // tc
    halo_blocks = tc // SUBLANES

    def body(dy_ref, u_ref, sr_ref, si_ref, hr_ref, hi_ref, bdr_ref, bdi_ref, cdr_ref, cdi_ref, ar_ref, ai_ref,
             d_ref, du_ref, dbdr_ref, dbdi_ref, dcdr_ref, dcdi_ref, dar_ref, dai_ref, dd_ref,
             gr_ref, gi_ref, pwr_ref, pwi_ref, rtr_ref, rti_ref, cr_ref, ci_ref, dyp_ref, up_ref, dup_ref):
        step_no = pl.program_id(1)
        first_chunk = step_no == nt - 1
        a_re = ar_ref[...]
        a_im = ai_ref[...]

        @pl.when(step_no == 0)
        def _():
            for ref in (cr_ref, ci_ref, dbdr_ref, dbdi_ref, dcdr_ref, dcdi_ref, dar_ref, dai_ref, dd_ref):
                ref[...] = jnp.zeros_like(ref)
            _s5_powers(a_re, a_im, pwr_ref, pwi_ref, seg)

        for i in range(seg):
            dyp_ref[_s5_tile(i), :] = dy_ref[pl.ds(i, SUBLANES, stride=seg), :]
            up_ref[_s5_tile(i), :] = u_ref[pl.ds(i, SUBLANES, stride=seg), :]
        dy = dyp_ref[...]
        dyb = dy.astype(BF16)
        u = up_ref[...]
        gr_ref[...] = _dot_nt(dyb, cdr_ref[...])
        gi_ref[...] = -_dot_nt(dyb, cdi_ref[...])
        dcdr_ref[...] += _dot_tn(sr_ref[...].astype(BF16), dyb)
        dcdi_ref[...] -= _dot_tn(si_ref[...].astype(BF16), dyb)

        def local(n, carry):
            c_re, c_im = carry
            rows = _s5_tile(seg - 1 - n)
            g_re = gr_ref[rows, :] + a_re * c_re + a_im * c_im
            g_im = gi_ref[rows, :] + a_re * c_im - a_im * c_re
            gr_ref[rows, :] = g_re
            gi_ref[rows, :] = g_im
            return g_re, g_im

        last = _s5_tile(seg - 1)
        m_re, m_im = lax.fori_loop(1, seg, local, (gr_ref[last, :], gi_ref[last, :]))
        top_re, top_im = pwr_ref[seg - 1:seg, :], -pwi_ref[seg - 1:seg, :]
        r_re, r_im = cr_ref[...], ci_ref[...]
        for j in reversed(range(SUBLANES)):
            rtr_ref[j:j + 1, :] = r_re
            rti_ref[j:j + 1, :] = r_im
            n_re, n_im = _cmul(top_re, top_im, r_re, r_im)
            r_re, r_im = n_re + m_re[j:j + 1, :], n_im + m_im[j:j + 1, :]
        cr_ref[...] = r_re
        ci_ref[...] = r_im
        f_re, f_im = rtr_ref[...], rti_ref[...]
        keep = jnp.where(first_chunk, 0.0, 1.0)
        sub = lax.broadcasted_iota(jnp.int32, (SUBLANES, SB), 0)
        before_re = jnp.where(sub == 0, hr_ref[SUBLANES - 1:SUBLANES, :] * keep, pltpu.roll(sr_ref[last, :], 1, 0))
        before_im = jnp.where(sub == 0, hi_ref[SUBLANES - 1:SUBLANES, :] * keep, pltpu.roll(si_ref[last, :], 1, 0))

        def fix(i, p_re, p_im, acc):
            rows = _s5_tile(i)
            k = seg - 1 - i
            c_re, c_im = _cmul(pwr_ref[pl.ds(k, 1), :], -pwi_ref[pl.ds(k, 1), :], f_re, f_im)
            g_re = gr_ref[rows, :] + c_re
            g_im = gi_ref[rows, :] + c_im
            gr_ref[rows, :] = g_re
            gi_ref[rows, :] = g_im
            return acc[0] + p_re * g_re + p_im * g_im, acc[1] + p_re * g_im - p_im * g_re

        zero = jnp.zeros((SUBLANES, SB), F32)
        acc = fix(0, before_re, before_im, (zero, zero))
        acc = lax.fori_loop(
            1, seg, lambda i, acc: fix(i, sr_ref[_s5_tile(i - 1), :], si_ref[_s5_tile(i - 1), :], acc), acc)
        dar_ref[...] += _colsum(acc[0])
        dai_ref[...] += _colsum(acc[1])
        gsr = gr_ref[...].astype(BF16)
        gsi = gi_ref[...].astype(BF16)
        ub = u.astype(BF16)
        dbdr_ref[...] += _dot_tn(ub, gsr)
        dbdi_ref[...] += _dot_tn(ub, gsi)
        dup_ref[...] = _dot_nt(gsr, bdr_ref[...]) + _dot_nt(gsi, bdi_ref[...]) + d_ref[...] * dy
        for i in range(seg):
            du_ref[pl.ds(i, SUBLANES, stride=seg), :] = dup_ref[_s5_tile(i), :]
        dd_ref[...] += _colsum(dy * u)

    def rev(t):
        return nt - 1 - t

    def halo(j, t):
        return (jnp.maximum(rev(t) * halo_blocks - 1, 0), j)

    ublk = pl.BlockSpec((tc, UB), lambda j, t: (rev(t), j))
    sblk = pl.BlockSpec((tc, SB), lambda j, t: (rev(t), j))
    bd_spec = pl.BlockSpec((None, UB, SB), lambda j, t: (j, 0, 0))
    cd_spec = pl.BlockSpec((None, SB, UB), lambda j, t: (j, 0, 0))
    svec = pl.BlockSpec((1, SB), lambda j, t: (0, j))
    uvec = pl.BlockSpec((1, UB), lambda j, t: (0, j))
    sds = jax.ShapeDtypeStruct
    return pl.pallas_call(
        body, name="s5_bwd", grid=(GB, nt),
        in_specs=[ublk, ublk, sblk, sblk, pl.BlockSpec((SUBLANES, SB), halo), pl.BlockSpec((SUBLANES, SB), halo),
                  bd_spec, bd_spec, cd_spec, cd_spec, svec, svec, uvec],
        out_specs=[ublk, bd_spec, bd_spec, cd_spec, cd_spec, svec, svec, uvec],
        out_shape=[sds((T, GB * UB), F32), sds((GB, UB, SB), F32), sds((GB, UB, SB), F32),
                   sds((GB, SB, UB), F32), sds((GB, SB, UB), F32), sds((1, GB * SB), F32),
                   sds((1, GB * SB), F32), sds((1, GB * UB), F32)],
        scratch_shapes=[pltpu.VMEM((tc, SB), F32), pltpu.VMEM((tc, SB), F32),
                        pltpu.VMEM((seg, SB), F32), pltpu.VMEM((seg, SB), F32),
                        pltpu.VMEM((SUBLANES, SB), F32), pltpu.VMEM((SUBLANES, SB), F32),
                        pltpu.VMEM((1, SB), F32), pltpu.VMEM((1, SB), F32),
                        pltpu.VMEM((tc, UB), F32), pltpu.VMEM((tc, UB), F32), pltpu.VMEM((tc, UB), F32)],
        compiler_params=_params("arbitrary", "arbitrary"),
    )(dy, proj, sr, si, sr, si, bdr, bdi, cdr, cdi, ar, ai, dvec)


def _shift_down(v, k):
    rows = lax.broadcasted_iota(jnp.int32, v.shape, 0)
    return jnp.where(rows >= k, pltpu.roll(v, k, 0), 0.0)


def _shift_up(v, k):
    n = v.shape[0]
    rows = lax.broadcasted_iota(jnp.int32, v.shape, 0)
    return jnp.where(rows < n - k, pltpu.roll(v, n - k, 0), 0.0)


def _conv_specs(T, cb, n_s5_blocks, n_conv_blocks):
    gb = pl.BlockSpec((T, cb), lambda j: (0, n_s5_blocks + j))
    gc = pl.BlockSpec((T, cb), lambda j: (0, n_s5_blocks + n_conv_blocks + j))
    hh = pl.BlockSpec((T, cb), lambda j: (0, n_s5_blocks + 2 * n_conv_blocks + j))
    return gb, gc, hh


def _conv_fwd(proj, cw, cbias, d_s5, d_conv):
    T = proj.shape[0]
    cb = _tile(d_conv, 256, LANES)

    def body(gb_ref, gc_ref, hh_ref, w_ref, b_ref, z_ref):
        v = gc_ref[...] * hh_ref[...]
        w = w_ref[...]
        cv = b_ref[...] + w[0:1, :] * _shift_down(v, 2) + w[1:2, :] * _shift_down(v, 1) + w[2:3, :] * v
        z_ref[...] = gb_ref[...] * cv

    gb, gc, hh = _conv_specs(T, cb, d_s5 // cb, d_conv // cb)
    col = pl.BlockSpec((T, cb), lambda j: (0, j))
    return pl.pallas_call(
        body, name="conv_fwd", grid=(d_conv // cb,),
        in_specs=[gb, gc, hh, pl.BlockSpec((CONV_W, cb), lambda j: (0, j)), pl.BlockSpec((1, cb), lambda j: (0, j))],
        out_specs=col, out_shape=jax.ShapeDtypeStruct((T, d_conv), F32),
        compiler_params=_params("arbitrary"),
    )(proj, proj, proj, cw, cbias)


def _conv_bwd(dz, proj, cw, cbias, d_s5, d_conv):
    T = proj.shape[0]
    cb = _tile(d_conv, 256, LANES)

    def body(dz_ref, gb_ref, gc_ref, hh_ref, w_ref, b_ref, dgb_ref, dgc_ref, dhh_ref, dw_ref, db_ref):
        gc = gc_ref[...]
        hh = hh_ref[...]
        dz = dz_ref[...]
        w = w_ref[...]
        v = gc * hh
        v1 = _shift_down(v, 1)
        v2 = _shift_down(v, 2)
        cv = b_ref[...] + w[0:1, :] * v2 + w[1:2, :] * v1 + w[2:3, :] * v
        dgb_ref[...] = (dz * cv).astype(BF16)
        dcv = dz * gb_ref[...]
        dv = w[2:3, :] * dcv + w[1:2, :] * _shift_up(dcv, 1) + w[0:1, :] * _shift_up(dcv, 2)
        dgc_ref[...] = (dv * hh).astype(BF16)
        dhh_ref[...] = (dv * gc).astype(BF16)
        dw_ref[0:1, :] = _colsum(dcv * v2)
        dw_ref[1:2, :] = _colsum(dcv * v1)
        dw_ref[2:3, :] = _colsum(dcv * v)
        db_ref[...] = _colsum(dcv)

    gb, gc, hh = _conv_specs(T, cb, d_s5 // cb, d_conv // cb)
    col = pl.BlockSpec((T, cb), lambda j: (0, j))
    wspec = pl.BlockSpec((CONV_W, cb), lambda j: (0, j))
    bspec = pl.BlockSpec((1, cb), lambda j: (0, j))
    sds = jax.ShapeDtypeStruct
    return pl.pallas_call(
        body, name="conv_bwd", grid=(d_conv // cb,),
        in_specs=[col, gb, gc, hh, wspec, bspec],
        out_specs=[col, col, col, wspec, bspec],
        out_shape=[sds((T, d_conv), BF16), sds((T, d_conv), BF16), sds((T, d_conv), BF16),
                   sds((CONV_W, d_conv), F32), sds((1, d_conv), F32)],
        compiler_params=_params("arbitrary"),
    )(dz, proj, proj, proj, cw, cbias)


def _rms(v, g):
    rstd = lax.rsqrt(_rowmean(v * v) + RMS_EPS)
    return v * rstd * g, rstd


def _rms_bwd(dyn, v, rstd, g):
    w = dyn * g
    return rstd * w - v * (rstd * rstd * rstd) * _rowmean(w * v), _colsum(dyn * v * rstd)


def _mix_post(y, z, wglu, g_s5, g_conv):
    T, C = y.shape
    tm = _tile(T, 256, 16)

    def body(y_ref, z_ref, w_ref, gs_ref, gc_ref, m_ref, gl_ref):
        ge = jax.nn.gelu(y_ref[...])
        gl = _dot_nn(ge.astype(BF16), w_ref[...])
        gl_ref[...] = gl
        yn, _ = _rms(ge * jax.nn.sigmoid(gl), gs_ref[...])
        zn, _ = _rms(z_ref[...], gc_ref[...])
        m_ref[:, 0:C] = yn.astype(BF16)
        m_ref[:, C:2 * C] = zn.astype(BF16)

    row = pl.BlockSpec((tm, C), lambda i: (i, 0))
    vec = pl.BlockSpec((1, C), lambda i: (0, 0))
    return pl.pallas_call(
        body, name="mix_post", grid=(T // tm,),
        in_specs=[row, row, pl.BlockSpec((C, C), lambda i: (0, 0)), vec, vec],
        out_specs=[pl.BlockSpec((tm, 2 * C), lambda i: (i, 0)), row],
        out_shape=[jax.ShapeDtypeStruct((T, 2 * C), BF16), jax.ShapeDtypeStruct((T, C), F32)],
        compiler_params=_params("arbitrary"),
    )(y, z, wglu, g_s5, g_conv)


def _mix_post_bwd(dm, y, gl, z, wglu, g_s5, g_conv):
    T, C = y.shape
    tm = _tile(T, 256, 16)

    def body(dm_ref, y_ref, gl_ref, z_ref, w_ref, gs_ref, gc_ref, dy_ref, dz_ref, dw_ref, dgs_ref, dgc_ref):
        @pl.when(pl.program_id(0) == 0)
        def _():
            dw_ref[...] = jnp.zeros_like(dw_ref)
            dgs_ref[...] = jnp.zeros_like(dgs_ref)
            dgc_ref[...] = jnp.zeros_like(dgc_ref)

        yv = y_ref[...]
        ge, gelu_vjp = jax.vjp(jax.nn.gelu, yv)
        gl = gl_ref[...]
        sg = jax.nn.sigmoid(gl)
        y2 = ge * sg
        _, rstd_y = _rms(y2, gs_ref[...])
        dy2, dgs = _rms_bwd(dm_ref[:, 0:C], y2, rstd_y, gs_ref[...])
        dgs_ref[...] += dgs
        dgl = (dy2 * ge * sg * (1.0 - sg)).astype(BF16)
        dge = dy2 * sg + _dot_nt(dgl, w_ref[...])
        dw_ref[...] += _dot_tn(ge.astype(BF16), dgl)
        dy_ref[...] = gelu_vjp(dge)[0]
        zv = z_ref[...]
        _, rstd_z = _rms(zv, gc_ref[...])
        dz, dgc = _rms_bwd(dm_ref[:, C:2 * C], zv, rstd_z, gc_ref[...])
        dz_ref[...] = dz
        dgc_ref[...] += dgc

    row = pl.BlockSpec((tm, C), lambda i: (i, 0))
    vec = pl.BlockSpec((1, C), lambda i: (0, 0))
    full = pl.BlockSpec((C, C), lambda i: (0, 0))
    sds = jax.ShapeDtypeStruct
    return pl.pallas_call(
        body, name="mix_post_bwd", grid=(T // tm,),
        in_specs=[pl.BlockSpec((tm, 2 * C), lambda i: (i, 0)), row, row, row, full, vec, vec],
        out_specs=[row, row, full, vec, vec],
        out_shape=[sds((T, C), F32), sds((T, C), F32), sds((C, C), F32), sds((1, C), F32), sds((1, C), F32)],
        compiler_params=_params("arbitrary"),
    )(dm, y, gl, z, wglu, g_s5, g_conv)


def _adamw(w, g, m, v):
    m = ADAM_B1 * m + (1.0 - ADAM_B1) * g
    v = ADAM_B2 * v + (1.0 - ADAM_B2) * (g * g)
    m_hat = m / (1.0 - ADAM_B1 ** ADAM_STEP)
    v_hat = v / (1.0 - ADAM_B2 ** ADAM_STEP)
    return -ADAM_LR * (m_hat / (jnp.sqrt(v_hat) + ADAM_EPS) + ADAM_WD * w), m, v


def _sum_parts(p_ref):
    total = p_ref[0].astype(F32)
    for d in range(1, p_ref.shape[0]):
        total = total + p_ref[d].astype(F32)
    return total


def _row_tile(R, C, n_streams):
    budget = VMEM_LIMIT // 3 // (n_streams * C * 4)
    return _tile(R, max(BF16_ROWS, budget), BF16_ROWS)


def _reduce_parts(parts):
    L, P, R, C = parts.shape
    tr = _row_tile(R, C, P + 1)

    def body(p_ref, o_ref):
        o_ref[...] = _sum_parts(p_ref)

    return pl.pallas_call(
        body, name="reduce_parts", grid=(L, R // tr),
        in_specs=[pl.BlockSpec((None, P, tr, C), lambda l, i: (l, 0, i, 0))],
        out_specs=pl.BlockSpec((None, tr, C), lambda l, i: (l, i, 0)),
        out_shape=jax.ShapeDtypeStruct((L, R, C), F32),
        compiler_params=_params("arbitrary", "arbitrary"),
    )(parts)


def _adamw_update(w, m, v, grad=None, parts=None):
    L, R, C = w.shape
    from_parts = parts is not None
    P = parts.shape[1] if from_parts else 1
    tr = _row_tile(R, C, P + 7)

    def body(g_in_ref, w_ref, m_ref, v_ref, g_ref, d_ref, nm_ref, nv_ref):
        g = _sum_parts(g_in_ref) if from_parts else g_in_ref[...]
        delta, nm, nv = _adamw(w_ref[...], g, m_ref[...], v_ref[...])
        g_ref[...] = g
        d_ref[...] = delta
        nm_ref[...] = nm
        nv_ref[...] = nv

    blk = pl.BlockSpec((None, tr, C), lambda l, i: (l, i, 0))
    g_spec = pl.BlockSpec((None, P, tr, C), lambda l, i: (l, 0, i, 0)) if from_parts else blk
    out = jax.ShapeDtypeStruct((L, R, C), F32)
    return pl.pallas_call(
        body, name="adamw_parts" if from_parts else "adamw", grid=(L, R // tr),
        in_specs=[g_spec, blk, blk, blk], out_specs=[blk, blk, blk, blk], out_shape=[out, out, out, out],
        compiler_params=_params("arbitrary", "arbitrary"),
    )(parts if from_parts else grad, w, m, v)


def _pair_sum(full, stage):
    _, R, C = full.shape
    tr = _row_tile(R, C, 4)

    def body(f_ref, s_ref, o_ref):
        mine = f_ref[lax.axis_index("c")]
        o_ref[...] = (mine.astype(F32) + s_ref[...].astype(F32)).astype(BF16)

    blk = pl.BlockSpec((None, tr, C), lambda q, i: (q, i, 0))
    return pl.pallas_call(
        body, name="pair_sum", grid=(N_CHIPS, R // tr),
        in_specs=[pl.BlockSpec((None, 2, tr, C), lambda q, i: (q, 0, i, 0)), blk],
        out_specs=blk, out_shape=jax.ShapeDtypeStruct((N_CHIPS, R, C), BF16),
        compiler_params=_params("arbitrary", "arbitrary"),
    )(full.reshape(N_CHIPS, 2, R, C), stage)


def _me():
    x, y, c = (lax.axis_index(a) for a in AXES)
    return x, y, c, 4 * x + 2 * y + c


def _peer(rel):
    x, y, c, _ = _me()
    px = 1 - x if rel & 4 else x
    py = 1 - y if rel & 2 else y
    pc = 1 - c if rel & 1 else c
    return (px, py, pc), 4 * px + 2 * py + pc


DATAFLOW = pltpu.SideEffectType.DATAFLOW_SIDE_EFFECTING


def _remote_copy(src, dst, sems):
    return functools.partial(pltpu.make_async_remote_copy, src_ref=src, dst_ref=dst, **sems)


ALL_PEERS = tuple(range(1, N_DEV))
SIBLING = 1
OTHER_CHIPS = (2, 4, 6)
SIBLING_AND_OTHER_CHIPS = (SIBLING,) + OTHER_CHIPS


def _slot(dev, blk, same_core, own_core_last):
    if not own_core_last:
        return blk
    return (N_CHIPS if same_core else 0) + 2 * dev[0] + dev[1]


def _gather_copies(n, rels=ALL_PEERS, own_core_last=False):
    def copies(srcs, lands, send_sems, recv_sems, local_sems):
        x, y, c, me = _me()
        local, remote = [], []
        for k in range(n):
            local.append(functools.partial(pltpu.make_async_copy, srcs[k],
                                           lands[k].at[_slot((x, y, c), me, True, own_core_last)], local_sems.at[k]))
            for rel in rels:
                dev, blk = _peer(rel)
                same_core = not rel & SIBLING
                sems = dict(send_sem=send_sems.at[_sem_index(k, rel)], recv_sem=recv_sems.at[_sem_index(k, rel)],
                            device_id=dev, device_id_type=MESH)
                remote.append((_remote_copy(srcs[k], lands[k].at[_slot((x, y, c), me, same_core, own_core_last)], sems),
                               _remote_copy(srcs[k], lands[k].at[_slot(dev, blk, same_core, own_core_last)], sems)))
        return local, remote

    copies.n_arrays = n
    return copies


Y_NEIGHBOUR, X_NEIGHBOUR, DIAGONAL = 2, 4, 6
SIBLING_AND_NEIGHBOURS = (SIBLING, Y_NEIGHBOUR, X_NEIGHBOUR)


def _relay_copies(n, own_core_last=False):
    def copies(srcs, lands, send_sems, recv_sems, local_sems):
        diagonal = _slot(*_peer(DIAGONAL), True, own_core_last)
        remote = []
        for k in range(n):
            half = lands[k].shape[1] // 2
            for to, held, rows in ((X_NEIGHBOUR, Y_NEIGHBOUR, pl.ds(0, half)), (Y_NEIGHBOUR, X_NEIGHBOUR, pl.ds(half, half))):
                block = lands[k].at[_slot(*_peer(held), True, own_core_last), rows]
                sems = dict(send_sem=send_sems.at[_sem_index(k, to)], recv_sem=recv_sems.at[_sem_index(k, to)],
                            device_id=_peer(to)[0], device_id_type=MESH)
                remote.append((_remote_copy(block, block, sems), _remote_copy(block, lands[k].at[diagonal, rows], sems)))
        return [], remote

    copies.n_arrays = n
    return copies


def _forward_copies(n, own_core_last=False):
    def copies(srcs, lands, send_sems, recv_sems, local_sems):
        sibling = _peer(SIBLING)[0]
        remote = []
        for k in range(n):
            for rel in OTHER_CHIPS:
                dev, blk = _peer(rel)
                have = _slot(dev, blk, True, own_core_last)
                there = _slot(dev, blk, False, own_core_last)
                other = _peer(rel | SIBLING)
                comes = _slot(other[0], other[1], False, own_core_last)
                sems = dict(send_sem=send_sems.at[_sem_index(k, rel)], recv_sem=recv_sems.at[_sem_index(k, rel)],
                            device_id=sibling, device_id_type=MESH)
                remote.append((_remote_copy(lands[k].at[have], lands[k].at[there], sems),
                               _remote_copy(lands[k].at[have], lands[k].at[comes], sems)))
        return [], remote

    copies.n_arrays = n
    return copies


def _scatter_copies(n, layer):
    def copies(srcs, lands, send_sems, recv_sems, local_sems):
        me = _me()[3]
        local, remote = [], []
        for k in range(n):
            local.append(functools.partial(pltpu.make_async_copy, srcs[k].at[me], lands[k].at[layer, me],
                                           local_sems.at[k]))
            for rel in range(1, N_DEV):
                dev, blk = _peer(rel)
                sems = dict(send_sem=send_sems.at[_sem_index(k, rel)], recv_sem=recv_sems.at[_sem_index(k, rel)],
                            device_id=dev, device_id_type=MESH)
                remote.append((_remote_copy(srcs[k].at[blk], lands[k].at[layer, me], sems),
                               _remote_copy(srcs[k].at[blk], lands[k].at[layer, blk], sems)))
        return local, remote

    copies.n_arrays = n
    return copies


def _swap_copies(n):
    def copies(srcs, lands, send_sems, recv_sems, local_sems):
        sibling = _peer(SIBLING)[0]
        remote = []
        for k in range(n):
            sems = dict(send_sem=send_sems.at[_sem_index(k, SIBLING)], recv_sem=recv_sems.at[_sem_index(k, SIBLING)],
                        device_id=sibling, device_id_type=MESH)
            remote.append((_remote_copy(srcs[k], lands[k], sems), _remote_copy(srcs[k], lands[k], sems)))
        return [], remote

    copies.n_arrays = n
    return copies


def _pair_copies(n):
    def copies(srcs, lands, send_sems, recv_sems, local_sems):
        c = _me()[2]
        sibling = _peer(SIBLING)[0]
        remote = []
        for k in range(n):
            for chip in range(N_CHIPS):
                sems = dict(send_sem=send_sems.at[_sem_index(k, chip + 1)], recv_sem=recv_sems.at[_sem_index(k, chip + 1)],
                            device_id=sibling, device_id_type=MESH)
                block = srcs[k].at[2 * chip + 1 - c]
                remote.append((_remote_copy(block, lands[k].at[chip], sems), _remote_copy(block, lands[k].at[chip], sems)))
        return [], remote

    copies.n_arrays = n
    return copies


def _chip_scatter_copies(n, layer):
    def copies(srcs, lands, send_sems, recv_sems, local_sems):
        x, y, _, _ = _me()
        my_chip = 2 * x + y
        local, remote = [], []
        for k in range(n):
            local.append(functools.partial(pltpu.make_async_copy, srcs[k].at[my_chip], lands[k].at[layer, my_chip],
                                           local_sems.at[k]))
            for rel in OTHER_CHIPS:
                dev = _peer(rel)[0]
                chip = 2 * dev[0] + dev[1]
                sems = dict(send_sem=send_sems.at[_sem_index(k, rel)], recv_sem=recv_sems.at[_sem_index(k, rel)],
                            device_id=dev, device_id_type=MESH)
                remote.append((_remote_copy(srcs[k].at[chip], lands[k].at[layer, my_chip], sems),
                               _remote_copy(srcs[k].at[chip], lands[k].at[layer, chip], sems)))
        return local, remote

    copies.n_arrays = n
    return copies


def _sem_shapes(n):
    return [pltpu.SemaphoreType.DMA((n * (N_DEV - 1),)), pltpu.SemaphoreType.DMA((n * (N_DEV - 1),)),
            pltpu.SemaphoreType.DMA((n,))]


def _sem_index(k, rel):
    return k * (N_DEV - 1) + rel - 1


def _exchange(copies, name, srcs, lands):
    n_src, n_land = len(srcs), len(lands)

    def body(*refs):
        src_refs = refs[:n_src]
        land_refs = refs[n_src + n_land:n_src + 2 * n_land]
        local, remote = copies(src_refs, land_refs, *refs[n_src + 2 * n_land:])
        local = [cp() for cp in local]
        sends = [send() for send, _ in remote]
        for cp in local + sends:
            cp.start()
        for send, (_, landing) in zip(sends, remote):
            send.wait_send()
            landing().wait_recv()
        for cp in local:
            cp.wait()

    return pl.pallas_call(
        body, name=name, in_specs=[HBM_SPEC] * (n_src + n_land), out_specs=[HBM_SPEC] * n_land,
        out_shape=[jax.ShapeDtypeStruct(b.shape, b.dtype) for b in lands],
        scratch_shapes=_sem_shapes(copies.n_arrays),
        input_output_aliases={n_src + k: k for k in range(n_land)},
        compiler_params=pltpu.CompilerParams(has_side_effects=True),
    )(*srcs, *lands)


def _hbm(arrays):
    return [pltpu.with_memory_space_constraint(a, pltpu.HBM) for a in arrays]


def _exchange_start(copies, name, srcs, lands, after):
    n_src, n_land, n_after = len(srcs), len(lands), len(after)
    n_data = n_src + n_land

    def body(*refs):
        outs = refs[n_data + n_after:]
        local, remote = copies(refs[:n_src], refs[n_src:n_data], *outs[:3])
        for cp in local:
            cp().start()
        for send, _ in remote:
            send().start()
        outs[-1][...] = jnp.zeros_like(outs[-1])

    res = pl.pallas_call(
        body, name=name, in_specs=[HBM_SPEC] * n_data + [ANY_SPEC] * n_after,
        out_specs=[SEM_SPEC] * 3 + [HBM_SPEC] * n_data + [pl.BlockSpec(memory_space=pltpu.VMEM)],
        out_shape=_sem_shapes(copies.n_arrays) + [pltpu.HBM(a.shape, a.dtype) for a in list(srcs) + list(lands)]
        + [jax.ShapeDtypeStruct((SUBLANES, LANES), F32)],
        input_output_aliases={k: 3 + k for k in range(n_data)},
        compiler_params=pltpu.CompilerParams(has_side_effects=DATAFLOW),
    )(*_hbm(list(srcs) + list(lands)), *after)
    return res[:3], res[3:3 + n_src], res[3 + n_src:3 + n_data], res[-1]


def _exchange_wait(copies, name, sems, srcs, lands, after):
    n_src, n_land, n_after = len(srcs), len(lands), len(after)
    n_data = n_src + n_land

    def body(*refs):
        local, remote = copies(refs[:n_src], refs[n_src:n_data], *refs[n_data:n_data + 3])
        for send, landing in remote:
            send().wait_send()
            landing().wait_recv()
        for cp in local:
            cp().wait()

    res = pl.pallas_call(
        body, name=name, in_specs=[HBM_SPEC] * n_data + [SEM_SPEC] * 3 + [ANY_SPEC] * n_after,
        out_specs=[HBM_SPEC] * n_data,
        out_shape=[pltpu.HBM(a.shape, a.dtype) for a in list(srcs) + list(lands)],
        input_output_aliases={k: k for k in range(n_data)},
        compiler_params=pltpu.CompilerParams(has_side_effects=DATAFLOW),
    )(*srcs, *lands, *sems, *after)
    return res[:n_src], res[n_src:]


def _block_diag(blocks, row_major):
    L, GB, g, P, N = blocks.shape
    eye = jnp.eye(g, dtype=blocks.dtype)
    if row_major:
        return jnp.einsum("lbgpn,gh->lbgphn", blocks, eye).reshape(L, GB, g * P, g * N)
    return jnp.einsum("lbgpn,gh->lbhngp", blocks, eye).reshape(L, GB, g * N, g * P)


def _diag_blocks(mat, g, P, N, row_major):
    GB = mat.shape[0]
    eye = jnp.eye(g, dtype=mat.dtype)
    if row_major:
        return jnp.einsum("bgphn,gh->bgpn", mat.reshape(GB, g, P, g, N), eye)
    return jnp.einsum("bhngp,gh->bgpn", mat.reshape(GB, g, N, g, P), eye)


def _pack(arrays, rows_multiple):
    flat = jnp.concatenate([a.reshape(-1).astype(F32) for a in arrays])
    pad = (-flat.shape[0]) % (rows_multiple * LANES)
    return jnp.pad(flat, (0, pad)).reshape(-1, LANES)


def _unpack(packed, shapes):
    flat = packed.reshape(-1)
    out, pos = [], 0
    for s in shapes:
        n = math.prod(s)
        out.append(flat[pos:pos + n].reshape(s))
        pos += n
    return out


SMALL = ["ln1_g", "ln1_b", "s5_lam_re", "s5_lam_im", "s5_log_dt", "s5_b_re", "s5_b_im", "s5_c_re", "s5_c_im", "s5_d",
         "conv_b", "g_s5", "g_conv", "ln2_g", "ln2_b", "ln3_g", "ln3_b"]
S5_B = ["s5_b_re", "s5_b_im"]
WEIGHTS = ["ffn1_gate", "ffn1_up", "ffn1_down", "ln1_g", "ln1_b", "w_in", "s5_lam_re", "s5_lam_im", "s5_log_dt",
           "s5_b_re", "s5_b_im", "s5_c_re", "s5_c_im", "s5_d", "s5_w_glu", "conv_w", "conv_b", "g_s5", "g_conv",
           "w_out", "ln2_g", "ln2_b", "ffn2_gate", "ffn2_up", "ffn2_down", "ln3_g", "ln3_b"]
TRANSPOSED = ["ffn1_gate", "ffn1_up", "w_in", "ffn2_gate", "ffn2_up"]
UPDATED_TRANSPOSED = ["ffn1_gate", "ffn1_up", "ffn2_gate", "ffn2_up"]
GROUPS = {"a": ["ffn1_gate", "ffn1_up", "ffn1_down"], "b": ["w_in", "s5_w_glu", "w_out"],
          "c": ["ffn2_gate", "ffn2_up", "ffn2_down"]}
FFN_GROUPS = ("a", "c")
FIRST_DOWN = "d"
OWN_CORE_LAST = FFN_GROUPS + (FIRST_DOWN,)


def _train_step(x, target, w, m, v):
    T, D = x.shape
    L = w["ln1_g"].shape[0]
    alpha = (2.0 * L) ** 0.25
    G = w["s5_log_dt"].shape[1]
    d_s5 = G * S5_P
    d_conv = w["conv_b"].shape[1]
    GB = G // S5_GROUPS_PER_BLOCK
    me = _me()[3]

    def shard(n, l):
        return (jnp.swapaxes(w[n][l], 0, 1) if n in TRANSPOSED else w[n][l]).astype(BF16)

    conv_w_rows = jnp.pad(w["conv_w"], ((0, 0), (0, SUBLANES - CONV_W), (0, 0)))
    (conv_w_all,) = _exchange(_gather_copies(1), "gather_conv_w", [conv_w_rows],
                              [lax.empty((N_DEV,) + conv_w_rows.shape, F32)])
    parts = [(l, grp) for l in range(L) for grp in GROUPS]
    parts.insert(1, (0, FIRST_DOWN))

    def names_of(l, grp):
        if l == 0 and grp in ("a", FIRST_DOWN):
            return GROUPS["a"][:2] if grp == "a" else GROUPS["a"][2:]
        return GROUPS[grp]
    step_one, relayed, step_two = {}, {}, {}

    def start_one(i, after):
        if i >= len(parts):
            return []
        l, grp = parts[i]
        srcs = [shard(n, l) for n in names_of(l, grp)]
        lands = [lax.empty((N_DEV,) + a.shape, a.dtype) for a in srcs]
        sems, srcs, lands, tok = _exchange_start(_gather_copies(len(srcs), SIBLING_AND_NEIGHBOURS, grp in OWN_CORE_LAST),
                                                 f"gather_start_{l}{grp}", srcs, lands, after)
        step_one[l, grp] = (sems, srcs, lands)
        return [tok]

    def relay_on(i, after):
        if i >= len(parts):
            return []
        l, grp = parts[i]
        sems, srcs, lands = step_one[l, grp]
        copies = _gather_copies(len(srcs), SIBLING_AND_NEIGHBOURS, grp in OWN_CORE_LAST)
        _, lands = _exchange_wait(copies, f"gather_wait_{l}{grp}", sems, srcs, lands, after)
        sems, _, lands, tok = _exchange_start(_relay_copies(len(lands), grp in OWN_CORE_LAST), f"relay_start_{l}{grp}",
                                              [], lands, [])
        relayed[l, grp] = (sems, lands)
        return [tok]

    def forward_on(i, after):
        if i >= len(parts):
            return []
        l, grp = parts[i]
        sems, lands = relayed[l, grp]
        _, lands = _exchange_wait(_relay_copies(len(lands), grp in OWN_CORE_LAST), f"relay_wait_{l}{grp}", sems, [], lands,
                                  after)
        sems, _, lands, tok = _exchange_start(_forward_copies(len(lands), grp in OWN_CORE_LAST), f"forward_start_{l}{grp}",
                                              [], lands, [])
        step_two[l, grp] = (sems, lands)
        return [tok]

    def gathered(i, after):
        l, grp = parts[i]
        sems, lands = step_two[l, grp]
        _, lands = _exchange_wait(_forward_copies(len(lands), grp in OWN_CORE_LAST), f"forward_wait_{l}{grp}", sems, [],
                                  lands, after)
        full = {n: p.reshape(-1, p.shape[-1]) for n, p in zip(names_of(l, grp), lands)}
        if grp == "b":
            full["conv_w"] = jnp.swapaxes(conv_w_all[:, l, :CONV_W, :], 0, 1).reshape(CONV_W, d_conv)
        return full

    lre = w["s5_lam_re"].reshape(L * G, S5_N)
    lim = w["s5_lam_im"].reshape(L * G, S5_N)
    ldt = w["s5_log_dt"].reshape(L * G, 1)
    b_re = jnp.transpose(w["s5_b_re"], (3, 0, 1, 2)).reshape(S5_P, L * G, S5_N)
    b_im = jnp.transpose(w["s5_b_im"], (3, 0, 1, 2)).reshape(S5_P, L * G, S5_N)
    ab_re, ab_im, bb_re, bb_im = _s5_params_fwd(lre, lim, ldt, b_re, b_im)

    def groups(bb):
        return jnp.transpose(bb.reshape(S5_P, L, GB, S5_GROUPS_PER_BLOCK, S5_N), (1, 2, 3, 0, 4))

    bd_re = _block_diag(groups(bb_re), True).astype(BF16)
    bd_im = _block_diag(groups(bb_im), True).astype(BF16)
    c_shape = (L, GB, S5_GROUPS_PER_BLOCK, S5_P, S5_N)
    cd_re = _block_diag(w["s5_c_re"].reshape(c_shape), False).astype(BF16)
    cd_im = _block_diag(w["s5_c_im"].reshape(c_shape), False).astype(BF16)
    a_re = ab_re.reshape(L, 1, G * S5_N)
    a_im = ab_im.reshape(L, 1, G * S5_N)
    d_vec = w["s5_d"].reshape(L, 1, d_s5)

    def vec(name, l):
        return w[name][l].reshape(1, -1)

    saved, weights = [], []
    x_in, x_in_b = x, x.astype(BF16)
    def advance(i, after):
        if i == 0:
            tokens = relay_on(1, after)
            return tokens + start_one(3, tokens)
        tokens = forward_on(i + 1, after)
        tokens = tokens + relay_on(i + 2, tokens or after)
        return tokens + start_one(i + 3, tokens or after)

    token = start_one(0, [])
    token = relay_on(0, token)
    token = start_one(1, token)
    token = start_one(2, token)
    token = forward_on(0, token)
    for l in range(L):
        i = len(GROUPS) * l + (1 if l else 0)
        gw = gathered(i, [x_in] if l else token)
        s = {"x0b": x_in_b}
        if l == 0:
            advance(0, token)
        s["g1"], s["u1"], s["h1"] = _ffn_up(x_in_b, gw["ffn1_gate"], gw["ffn1_up"])
        if l == 0:
            behind = forward_on(1, [s["h1"]])
            behind = behind + relay_on(2, behind)
            gw.update(gathered(1, behind))
            i = 1
        x1, s["x1b"], s["xh1"], s["rstd1"] = _mm_res_ln(s["h1"], gw["ffn1_down"], x_in, vec("ln1_g", l),
                                                         vec("ln1_b", l), 0.5, alpha, advance(i, [s["h1"]]))
        gw.update(gathered(i + 1, [s["x1b"]]))
        s["proj"] = _mm_nt(s["x1b"], gw["w_in"])
        s["y"], s["sr"], s["si"] = _s5_fwd(s["proj"], bd_re[l], bd_im[l], cd_re[l], cd_im[l], a_re[l], a_im[l],
                                           d_vec[l])
        s["z"] = _conv_fwd(s["proj"], gw["conv_w"], vec("conv_b", l), d_s5, d_conv)
        s["mcat"], s["gl"] = _mix_post(s["y"], s["z"], gw["s5_w_glu"], vec("g_s5", l), vec("g_conv", l))
        x2, s["x2b"], s["xh2"], s["rstd2"] = _mm_res_ln(s["mcat"], gw["w_out"], x1, vec("ln2_g", l),
                                                         vec("ln2_b", l), 1.0, alpha, advance(i + 1, [s["mcat"]]))
        gw.update(gathered(i + 2, [s["x2b"]]))
        s["g2"], s["u2"], s["h2"] = _ffn_up(s["x2b"], gw["ffn2_gate"], gw["ffn2_up"])
        x3, x3b, s["xh3"], s["rstd3"] = _mm_res_ln(s["h2"], gw["ffn2_down"], x2, vec("ln3_g", l), vec("ln3_b", l),
                                                   0.5, alpha, advance(i + 2, [s["h2"]]))
        saved.append(s)
        weights.append(gw)
        x_in, x_in_b = x3, x3b

    last = saved[L - 1]
    dr, drb, dg, db, loss = _loss_ln_bwd(x_in, target, last["xh3"], last["rstd3"], vec("ln3_g", L - 1))
    small = [dict() for _ in range(L)]
    small[L - 1]["ln3_g"], small[L - 1]["ln3_b"] = dg, db
    bufs = {grp: [lax.empty((L, N_CHIPS) + shard(n, 0).shape, BF16) for n in names] for grp, names in GROUPS.items()}
    scatters = {grp: [] for grp in GROUPS}
    grad_x = None

    def chip_scatter(l, grp, sums):
        sems, sums, bufs[grp], tok = _exchange_start(_chip_scatter_copies(len(sums), l), f"scatter_start_{l}{grp}", sums,
                                                     bufs[grp], [])
        scatters[grp].append((l, sems, sums))
        return [tok]

    small_names = SMALL + ["conv_w"]
    b_shape = (L, G, S5_P, S5_N)
    small_shapes = [b_shape if n in S5_B else w[n].shape for n in SMALL] + [(L, CONV_W, d_conv)]

    def small_scatter_start(small):
        def stack(key):
            return jnp.stack([small[l][key] for l in range(L)])

        d_bb_re = jnp.transpose(stack("d_bb_re").reshape(L * G, S5_P, S5_N), (1, 0, 2))
        d_bb_im = jnp.transpose(stack("d_bb_im").reshape(L * G, S5_P, S5_N), (1, 0, 2))
        g_lre, g_lim, g_ldt, g_bre, g_bim = _s5_params_bwd(
            lre, lim, ldt, b_re, b_im, stack("d_ab_re").reshape(L * G, S5_N), stack("d_ab_im").reshape(L * G, S5_N),
            d_bb_re, d_bb_im)
        part = {n: [small[l][n] for l in range(L)]
                for n in ["ln1_g", "ln1_b", "s5_c_re", "s5_c_im", "s5_d", "conv_b", "g_s5", "g_conv", "ln2_g", "ln2_b",
                          "ln3_g", "ln3_b", "conv_w"]}
        part["s5_lam_re"], part["s5_lam_im"], part["s5_log_dt"] = [g_lre], [g_lim], [g_ldt]
        part["s5_b_re"] = [jnp.transpose(g_bre, (1, 0, 2))]
        part["s5_b_im"] = [jnp.transpose(g_bim, (1, 0, 2))]
        packed = _pack([piece for n in small_names for piece in part[n]], N_DEV * PACK_ROWS)
        rows = packed.shape[0] // N_DEV
        return _exchange_start(_scatter_copies(1, 0), "scatter_small_start", [packed.reshape(N_DEV, rows, LANES)],
                               [lax.empty((1, N_DEV, rows, LANES), F32)], [])

    token = []
    for l in reversed(range(L)):
        gw, s, sm = weights[l], saved[l], small[l]
        full = {}

        def ffn_bwd(dr, drb, tag, grp, xb_in, ln, after):
            dgp, dup = _ffn_down_bwd(drb, gw[f"ffn{tag}_down"], s[f"g{tag}"], s[f"u{tag}"], after)
            terms = {"gate": (dgp, xb_in, 1.0), "up": (dup, xb_in, 1.0), "down": (s[f"h{tag}"], drb, 0.5)}
            terms = [terms[n.split("_")[1]] for n in GROUPS[grp]]
            away = [_mm_tn(a, b, scale, BF16, half=0).reshape((N_CHIPS, -1, b.shape[1])) for a, b, scale in terms]
            stages = [lax.empty(a.shape, BF16) for a in away]
            sems, away, stages, tok = _exchange_start(_swap_copies(len(away)), f"pair_start_{l}{grp}", away, stages, [])
            res = _mm_dx([(dgp, gw[f"ffn{tag}_gate"]), (dup, gw[f"ffn{tag}_up"])], dr, alpha, ln, [tok])
            _, stages = _exchange_wait(_swap_copies(len(away)), f"pair_wait_{l}{grp}", sems, away, stages, [res[0]])
            sums = [_mm_tn(a, b, scale, BF16, half=1, addend=st.reshape(-1, b.shape[1])).reshape(st.shape)
                    for (a, b, scale), st in zip(terms, stages)]
            return res, chip_scatter(l, grp, sums)

        (dr, drb, sm["ln2_g"], sm["ln2_b"]), token = ffn_bwd(dr, drb, 2, "c", s["x2b"],
                                                              (s["xh2"], s["rstd2"], vec("ln2_g", l)), token)
        dm = _mm_nt(drb, gw["w_out"], token)
        full["w_out"] = _mm_tn(s["mcat"], drb, 1.0, BF16)
        dy, dz, dwglu, sm["g_s5"], sm["g_conv"] = _mix_post_bwd(dm, s["y"], s["gl"], s["z"], gw["s5_w_glu"],
                                                                vec("g_s5", l), vec("g_conv", l))
        full["s5_w_glu"] = dwglu.astype(BF16)
        du, dbd_re, dbd_im, dcd_re, dcd_im, sm["d_ab_re"], sm["d_ab_im"], sm["s5_d"] = _s5_bwd(
            dy, s["proj"], s["sr"], s["si"], bd_re[l], bd_im[l], cd_re[l], cd_im[l], a_re[l], a_im[l], d_vec[l])
        gsz = (S5_GROUPS_PER_BLOCK, S5_P, S5_N)
        sm["d_bb_re"] = _diag_blocks(dbd_re, *gsz, True)
        sm["d_bb_im"] = _diag_blocks(dbd_im, *gsz, True)
        sm["s5_c_re"] = _diag_blocks(dcd_re, *gsz, False).reshape(G, S5_P, S5_N)
        sm["s5_c_im"] = _diag_blocks(dcd_im, *gsz, False).reshape(G, S5_P, S5_N)
        dgb, dgc, dhh, sm["conv_w"], sm["conv_b"] = _conv_bwd(dz, s["proj"], gw["conv_w"], vec("conv_b", l),
                                                              d_s5, d_conv)
        dproj = jnp.concatenate([du.astype(BF16), dgb, dgc, dhh], axis=1)
        full["w_in"] = _mm_tn(dproj, s["x1b"], 1.0, BF16)
        fulls = [full[n].reshape((N_DEV, -1) + full[n].shape[1:]) for n in GROUPS["b"]]
        stages = [lax.empty((N_CHIPS,) + f.shape[1:], BF16) for f in fulls]
        sems, fulls, stages, tok = _exchange_start(_pair_copies(len(fulls)), f"pair_start_{l}b", fulls, stages, [])
        dr, drb, sm["ln1_g"], sm["ln1_b"] = _mm_dx([(dproj, gw["w_in"])], dr, alpha,
                                                   (s["xh1"], s["rstd1"], vec("ln1_g", l)), [tok])
        fulls, stages = _exchange_wait(_pair_copies(len(fulls)), f"pair_wait_{l}b", sems, fulls, stages, [dr])
        token = chip_scatter(l, "b", [_pair_sum(f, st) for f, st in zip(fulls, stages)])
        if l > 0:
            prev = saved[l - 1]
            (dr, drb, small[l - 1]["ln3_g"], small[l - 1]["ln3_b"]), token = ffn_bwd(
                dr, drb, 1, "a", s["x0b"], (prev["xh3"], prev["rstd3"], vec("ln3_g", l - 1)), token)
        else:
            small_sems, small_srcs, small_lands, tok = small_scatter_start(small)
            ((grad_x,), token) = ffn_bwd(dr, drb, 1, "a", s["x0b"], None, token + [tok])

    _, landed = _exchange_wait(_scatter_copies(1, 0), "scatter_small_wait", small_sems, small_srcs, small_lands,
                               [grad_x])
    mine = _reduce_parts(landed[0])
    (summed,) = _exchange(_gather_copies(1), "gather_small", [mine[0]], [lax.empty(landed[0].shape[1:], F32)])
    small_grads = dict(zip(small_names, _unpack(summed, small_shapes)))

    out = {}

    def update(name, w3, m3, v3, shape, **grad):
        res = _adamw_update(w3, m3, v3, **grad)
        out[name] = [r.reshape(shape) for r in res]
        return res

    cw_shape = w["conv_w"].shape
    g_cw = lax.dynamic_slice_in_dim(small_grads["conv_w"], me * cw_shape[2], cw_shape[2], axis=2)
    done = [update("conv_w", w["conv_w"], m["conv_w"], v["conv_w"], cw_shape, grad=g_cw)[3]]
    for n in SMALL:
        if n in S5_B:
            def view(a):
                return jnp.swapaxes(a, 2, 3).reshape(1, -1, S5_N)
            res = _adamw_update(view(w[n]), view(m[n]), view(v[n]), grad=small_grads[n].reshape(1, -1, S5_N))
            out[n] = [jnp.swapaxes(r.reshape(b_shape), 2, 3) for r in res]
        else:
            res = update(n, *(a[n].reshape(1, -1, a[n].shape[-1]) for a in (w, m, v)), w[n].shape,
                         grad=small_grads[n].reshape(1, -1, w[n].shape[-1]))
        done.append(res[3])
    for grp in ("c", "b", "a"):
        for l, sems, sums in scatters[grp]:
            _, bufs[grp] = _exchange_wait(_chip_scatter_copies(len(sums), l), f"scatter_wait_{l}{grp}", sems, sums,
                                          bufs[grp], [grad_x] + token + done)
        done = []
        for n, parts in zip(GROUPS[grp], bufs[grp]):
            if n in UPDATED_TRANSPOSED:
                res = _adamw_update(*(jnp.swapaxes(a[n], 1, 2) for a in (w, m, v)), parts=parts)
                out[n] = [jnp.swapaxes(r, 1, 2) for r in res]
            elif n in TRANSPOSED:
                res = update(n, w[n], m[n], v[n], w[n].shape, grad=jnp.swapaxes(_reduce_parts(parts), 1, 2))
            else:
                res = update(n, w[n], m[n], v[n], w[n].shape, parts=parts)
            done.append(res[3])

    loss = lax.psum(loss[0, 0], AXES)
    return loss, grad_x, out


def kernel(x, ffn1_gate, ffn1_up, ffn1_down, ln1_g, ln1_b, w_in, s5_lam_re, s5_lam_im, s5_log_dt, s5_b_re, s5_b_im, s5_c_re, s5_c_im, s5_d, s5_w_glu, conv_w, conv_b, g_s5, g_conv, w_out, ln2_g, ln2_b, ffn2_gate, ffn2_up, ffn2_down, ln3_g, ln3_b, loss_target, m_ffn1_gate, m_ffn1_up, m_ffn1_down, m_ln1_g, m_ln1_b, m_w_in, m_s5_lam_re, m_s5_lam_im, m_s5_log_dt, m_s5_b_re, m_s5_b_im, m_s5_c_re, m_s5_c_im, m_s5_d, m_s5_w_glu, m_conv_w, m_conv_b, m_g_s5, m_g_conv, m_w_out, m_ln2_g, m_ln2_b, m_ffn2_gate, m_ffn2_up, m_ffn2_down, m_ln3_g, m_ln3_b, v_ffn1_gate, v_ffn1_up, v_ffn1_down, v_ln1_g, v_ln1_b, v_w_in, v_s5_lam_re, v_s5_lam_im, v_s5_log_dt, v_s5_b_re, v_s5_b_im, v_s5_c_re, v_s5_c_im, v_s5_d, v_s5_w_glu, v_conv_w, v_conv_b, v_g_s5, v_g_conv, v_w_out, v_ln2_g, v_ln2_b, v_ffn2_gate, v_ffn2_up, v_ffn2_down, v_ln3_g, v_ln3_b):
    given = dict(locals())
    w = {n: given[n] for n in WEIGHTS}
    m = {n: given["m_" + n] for n in WEIGHTS}
    v = {n: given["v_" + n] for n in WEIGHTS}
    T, D = x.shape[-2:]
    loss, grad_x, out = _train_step(x.reshape(T, D), loss_target.reshape(T, D), w, m, v)
    results = [loss, grad_x.reshape(x.shape)]
    for i in range(4):
        results += [out[n][i] for n in WEIGHTS]
    return tuple(results)
```

```python
import functools
import math

import jax
import jax.numpy as jnp
from jax import lax
from jax.experimental import pallas as pl
from jax.experimental.pallas import tpu as pltpu

F32 = jnp.float32
BF16 = jnp.bfloat16
MESH = pl.DeviceIdType.MESH
AXES = ("x", "y", "c")
N_DEV = 8
N_CHIPS = 4

S5_P = 16
S5_N = 64
CONV_W = 3
LN_EPS = 1e-5
RMS_EPS = 1e-6
ADAM_LR = 0.001
ADAM_B1 = 0.9
ADAM_B2 = 0.999
ADAM_EPS = 1e-08
ADAM_WD = 0.01
ADAM_STEP = 10

V7X_VMEM_BYTES = 64 * 1024 * 1024
VMEM_LIMIT = V7X_VMEM_BYTES * 7 // 8
LANES = 128
SUBLANES = 8
BF16_ROWS = 16
MXU_COLS = 256
ROWS_RESIDENT = 2048
PACK_ROWS = 512
S5_GROUPS_PER_BLOCK = LANES // S5_P
S5_STATE_BLOCK = S5_GROUPS_PER_BLOCK * S5_N

HBM_SPEC = pl.BlockSpec(memory_space=pltpu.HBM)
SEM_SPEC = pl.BlockSpec(memory_space=pltpu.SEMAPHORE)
ANY_SPEC = pl.BlockSpec(memory_space=pl.ANY)


def _tile(n, pref, align):
    best = None
    d = align
    while d <= min(n, pref):
        if n % d == 0:
            best = d
        d += align
    return best if best is not None else n


def _params(*sem):
    return pltpu.CompilerParams(dimension_semantics=sem, vmem_limit_bytes=VMEM_LIMIT)


def _dot_nn(a, b):
    return lax.dot_general(a, b, (((1,), (0,)), ((), ())), preferred_element_type=F32)


def _dot_nt(a, b):
    return lax.dot_general(a, b, (((1,), (1,)), ((), ())), preferred_element_type=F32)


def _dot_tn(a, b):
    return lax.dot_general(a, b, (((0,), (0,)), ((), ())), preferred_element_type=F32)


def _colsum(v):
    return jnp.sum(v, axis=0, keepdims=True)


def _rowmean(v):
    return jnp.mean(v, axis=-1, keepdims=True)


def _ffn_up(xb, wg, wu):
    T, D = xb.shape
    F = wg.shape[0]
    tm = _tile(T, ROWS_RESIDENT, 16)
    tn = _tile(F, MXU_COLS, LANES)

    def body(x_ref, wg_ref, wu_ref, cg_ref, cu_ref, h_ref):
        x = x_ref[...]
        g = _dot_nt(x, wg_ref[...])
        u = _dot_nt(x, wu_ref[...])
        sg = jax.nn.sigmoid(g)
        silu = g * sg
        cu_ref[...] = silu.astype(BF16)
        cg_ref[...] = (u * (sg * (1.0 + g * (1.0 - sg)))).astype(BF16)
        h_ref[...] = (silu * u).astype(BF16)

    w_spec = pl.BlockSpec((tn, D), lambda j, i: (j, 0))
    o_spec = pl.BlockSpec((tm, tn), lambda j, i: (i, j))
    return pl.pallas_call(
        body, name="ffn_up", grid=(F // tn, T // tm),
        in_specs=[pl.BlockSpec((tm, D), lambda j, i: (i, 0)), w_spec, w_spec],
        out_specs=[o_spec, o_spec, o_spec],
        out_shape=[jax.ShapeDtypeStruct((T, F), BF16)] * 3,
        compiler_params=_params("arbitrary", "arbitrary"),
    )(xb, wg, wu)


def _mm_acc(pairs, after=()):
    T, K = pairs[0][0].shape
    D = pairs[0][1].shape[1]
    n = len(pairs)
    tk = _tile(K, 512, LANES)
    tm = _tile(T, 512, 16)

    def body(*refs):
        o_ref = refs[-1]

        @pl.when(pl.program_id(0) == 0)
        def _():
            o_ref[...] = jnp.zeros_like(o_ref)

        for r in range(0, T, tm):
            part = _dot_nn(refs[0][r:r + tm, :], refs[1][...])
            for a_ref, w_ref in zip(refs[2:2 * n:2], refs[3:2 * n:2]):
                part += _dot_nn(a_ref[r:r + tm, :], w_ref[...])
            o_ref[r:r + tm, :] += part

    in_specs, operands = [], []
    for a, w in pairs:
        in_specs += [pl.BlockSpec((T, tk), lambda k: (0, k)), pl.BlockSpec((tk, D), lambda k: (k, 0))]
        operands += [a, w]
    return pl.pallas_call(
        body, name="mm_acc", grid=(K // tk,),
        in_specs=in_specs + [ANY_SPEC] * len(after),
        out_specs=pl.BlockSpec((T, D), lambda k: (0, 0)),
        out_shape=jax.ShapeDtypeStruct((T, D), F32),
        compiler_params=_params("arbitrary"),
    )(*operands, *after)


def _mm_res_ln(a, w, res, g, b, scale, alpha, after=()):
    acc = _mm_acc([(a, w)], after)
    T, D = acc.shape
    tm = _tile(T, 256, 16)

    def body(acc_ref, res_ref, g_ref, b_ref, xo_ref, xb_ref, xh_ref, rstd_ref):
        r = alpha * res_ref[...] + scale * acc_ref[...]
        xc = r - _rowmean(r)
        rstd = lax.rsqrt(_rowmean(xc * xc) + LN_EPS)
        xh = xc * rstd
        xo = xh * g_ref[...] + b_ref[...]
        xo_ref[...] = xo
        xb_ref[...] = xo.astype(BF16)
        xh_ref[...] = xh.astype(BF16)
        rstd_ref[...] = rstd

    row = pl.BlockSpec((tm, D), lambda i: (i, 0))
    vec = pl.BlockSpec((1, D), lambda i: (0, 0))
    return pl.pallas_call(
        body, name="res_ln", grid=(T // tm,),
        in_specs=[row, row, vec, vec],
        out_specs=[row, row, row, pl.BlockSpec((tm, 1), lambda i: (i, 0))],
        out_shape=[jax.ShapeDtypeStruct((T, D), F32), jax.ShapeDtypeStruct((T, D), BF16),
                   jax.ShapeDtypeStruct((T, D), BF16), jax.ShapeDtypeStruct((T, 1), F32)],
        compiler_params=_params("arbitrary"),
    )(acc, res, g, b)


def _mm_nt(a, w, after=()):
    M, K = a.shape
    N = w.shape[0]
    tm = _tile(M, ROWS_RESIDENT, 16)
    tn = _tile(N, MXU_COLS, LANES)

    def body(a_ref, w_ref, *rest):
        rest[-1][...] = _dot_nt(a_ref[...], w_ref[...])

    return pl.pallas_call(
        body, name="mm_nt", grid=(N // tn, M // tm),
        in_specs=[pl.BlockSpec((tm, K), lambda j, i: (i, 0)), pl.BlockSpec((tn, K), lambda j, i: (j, 0))]
        + [ANY_SPEC] * len(after),
        out_specs=pl.BlockSpec((tm, tn), lambda j, i: (i, j)),
        out_shape=jax.ShapeDtypeStruct((M, N), F32),
        compiler_params=_params("arbitrary", "arbitrary"),
    )(a, w, *after)


def _mm_tn(a, b, scale, out_dtype, half=None, addend=None):
    T, M = a.shape
    N = b.shape[1]
    rows = M if half is None else M // 2
    tm = _tile(rows, 512, LANES)
    tn = _tile(N, ROWS_RESIDENT, LANES)
    if tm < 512:
        tm = _tile(rows, 1408, LANES)
        tn = _tile(N, ROWS_RESIDENT // 2, LANES)
    first = 0 if half is None else half * (rows // tm)

    def body(a_ref, b_ref, *rest):
        out = scale * _dot_tn(a_ref[...], b_ref[...])
        if addend is not None:
            out = out + rest[0][...].astype(F32)
        rest[-1][...] = out.astype(out_dtype)

    o_spec = pl.BlockSpec((tm, tn), lambda i, j: (i, j))
    return pl.pallas_call(
        body, name="mm_tn", grid=(rows // tm, N // tn),
        in_specs=[pl.BlockSpec((T, tm), lambda i, j: (0, first + i)), pl.BlockSpec((T, tn), lambda i, j: (0, j))]
        + ([] if addend is None else [o_spec]),
        out_specs=o_spec,
        out_shape=jax.ShapeDtypeStruct((rows, N), out_dtype),
        compiler_params=_params("arbitrary", "arbitrary"),
    )(a, b, *([] if addend is None else [addend]))


def _ln_bwd(dy, xh, rstd, g):
    dxh = dy * g
    dr = rstd * (dxh - _rowmean(dxh) - xh * _rowmean(dxh * xh))
    return dr, _colsum(dy * xh), _colsum(dy)


def _loss_ln_bwd(y, target, xh, rstd, g):
    T, D = y.shape
    tm = _tile(T, 256, 16)

    def body(y_ref, t_ref, xh_ref, rstd_ref, g_ref, dr_ref, drb_ref, dg_ref, db_ref, loss_ref):
        i = pl.program_id(0)

        @pl.when(i == 0)
        def _():
            dg_ref[...] = jnp.zeros_like(dg_ref)
            db_ref[...] = jnp.zeros_like(db_ref)
            loss_ref[...] = jnp.zeros_like(loss_ref)

        err = y_ref[...] - t_ref[...]
        loss_ref[...] += (0.5 / D) * _colsum(jnp.sum(err * err, axis=1, keepdims=True))
        dr, dg, db = _ln_bwd(err * (1.0 / D), xh_ref[...].astype(F32), rstd_ref[...], g_ref[...])
        dr_ref[...] = dr
        drb_ref[...] = dr.astype(BF16)
        dg_ref[...] += dg
        db_ref[...] += db

    row = pl.BlockSpec((tm, D), lambda i: (i, 0))
    vec = pl.BlockSpec((1, D), lambda i: (0, 0))
    return pl.pallas_call(
        body, name="loss_ln_bwd", grid=(T // tm,),
        in_specs=[row, row, row, pl.BlockSpec((tm, 1), lambda i: (i, 0)), vec],
        out_specs=[row, row, vec, vec, pl.BlockSpec((1, 1), lambda i: (0, 0))],
        out_shape=[jax.ShapeDtypeStruct((T, D), F32), jax.ShapeDtypeStruct((T, D), BF16),
                   jax.ShapeDtypeStruct((1, D), F32), jax.ShapeDtypeStruct((1, D), F32),
                   jax.ShapeDtypeStruct((1, 1), F32)],
        compiler_params=_params("arbitrary"),
    )(y, target, xh, rstd, g)


def _ffn_down_bwd(drb, wd, cg, cu, after=()):
    T, D = drb.shape
    F = wd.shape[0]
    tm = _tile(T, ROWS_RESIDENT, 16)
    tn = _tile(F, MXU_COLS, LANES)

    def body(dr_ref, wd_ref, cg_ref, cu_ref, *rest):
        dg_ref, du_ref = rest[len(after):]
        dh = 0.5 * _dot_nt(dr_ref[...], wd_ref[...])
        du_ref[...] = (dh * cu_ref[...].astype(F32)).astype(BF16)
        dg_ref[...] = (dh * cg_ref[...].astype(F32)).astype(BF16)

    t_spec = pl.BlockSpec((tm, tn), lambda j, i: (i, j))
    return pl.pallas_call(
        body, name="ffn_down_bwd", grid=(F // tn, T // tm),
        in_specs=[pl.BlockSpec((tm, D), lambda j, i: (i, 0)), pl.BlockSpec((tn, D), lambda j, i: (j, 0)),
                  t_spec, t_spec] + [ANY_SPEC] * len(after),
        out_specs=[t_spec, t_spec],
        out_shape=[jax.ShapeDtypeStruct((T, F), BF16), jax.ShapeDtypeStruct((T, F), BF16)],
        compiler_params=_params("arbitrary", "arbitrary"),
    )(drb, wd, cg, cu, *after)


def _mm_dx(pairs, res, alpha, ln=None, after=()):
    acc = _mm_acc(pairs, after)
    T, D = acc.shape
    tm = _tile(T, 256, 16)
    with_ln = ln is not None

    def body(acc_ref, res_ref, *refs):
        dx = alpha * res_ref[...] + acc_ref[...]
        if with_ln:
            xh_ref, rstd_ref, g_ref, dr_ref, drb_ref, dg_ref, db_ref = refs

            @pl.when(pl.program_id(0) == 0)
            def _():
                dg_ref[...] = jnp.zeros_like(dg_ref)
                db_ref[...] = jnp.zeros_like(db_ref)

            dr, dg, db = _ln_bwd(dx, xh_ref[...].astype(F32), rstd_ref[...], g_ref[...])
            dr_ref[...] = dr
            drb_ref[...] = dr.astype(BF16)
            dg_ref[...] += dg
            db_ref[...] += db
        else:
            refs[0][...] = dx

    row = pl.BlockSpec((tm, D), lambda i: (i, 0))
    vec = pl.BlockSpec((1, D), lambda i: (0, 0))
    in_specs, operands = [row, row], [acc, res]
    if with_ln:
        in_specs += [row, pl.BlockSpec((tm, 1), lambda i: (i, 0)), vec]
        operands += list(ln)
        out_specs = [row, row, vec, vec]
        out_shape = [jax.ShapeDtypeStruct((T, D), F32), jax.ShapeDtypeStruct((T, D), BF16),
                     jax.ShapeDtypeStruct((1, D), F32), jax.ShapeDtypeStruct((1, D), F32)]
    else:
        out_specs = [row]
        out_shape = [jax.ShapeDtypeStruct((T, D), F32)]
    return pl.pallas_call(
        body, name="dx_ln_bwd" if with_ln else "dx_res", grid=(T // tm,),
        in_specs=in_specs, out_specs=out_specs, out_shape=out_shape,
        compiler_params=_params("arbitrary"),
    )(*operands)


def _s5_discretize(lre, lim, ldt, br, bi):
    dt = jnp.exp(ldt)
    mag = jnp.exp(lre * dt)
    ang = lim * dt
    ar = mag * jnp.cos(ang)
    ai = mag * jnp.sin(ang)
    den = lre * lre + lim * lim
    nr = ar - 1.0
    qr = (nr * lre + ai * lim) / den
    qi = (ai * lre - nr * lim) / den
    bbr = qr[None] * br - qi[None] * bi
    bbi = qr[None] * bi + qi[None] * br
    return ar, ai, bbr, bbi


def _s5_params_fwd(lre, lim, ldt, br, bi):
    def body(lre_ref, lim_ref, ldt_ref, br_ref, bi_ref, ar_ref, ai_ref, bbr_ref, bbi_ref):
        ar, ai, bbr, bbi = _s5_discretize(lre_ref[...], lim_ref[...], ldt_ref[...], br_ref[...], bi_ref[...])
        ar_ref[...] = ar
        ai_ref[...] = ai
        bbr_ref[...] = bbr
        bbi_ref[...] = bbi

    sds = jax.ShapeDtypeStruct
    return pl.pallas_call(
        body, name="s5_params_fwd",
        out_shape=[sds(lre.shape, F32), sds(lre.shape, F32), sds(br.shape, F32), sds(br.shape, F32)],
        compiler_params=pltpu.CompilerParams(vmem_limit_bytes=VMEM_LIMIT),
    )(lre, lim, ldt, br, bi)


def _s5_params_bwd(lre, lim, ldt, br, bi, dar, dai, dbbr, dbbi):
    def body(lre_ref, lim_ref, ldt_ref, br_ref, bi_ref, dar_ref, dai_ref, dbbr_ref, dbbi_ref,
             o_lre, o_lim, o_ldt, o_br, o_bi):
        _, vjp = jax.vjp(_s5_discretize, lre_ref[...], lim_ref[...], ldt_ref[...], br_ref[...], bi_ref[...])
        g = vjp((dar_ref[...], dai_ref[...], dbbr_ref[...], dbbi_ref[...]))
        o_lre[...] = g[0]
        o_lim[...] = g[1]
        o_ldt[...] = g[2]
        o_br[...] = g[3]
        o_bi[...] = g[4]

    sds = jax.ShapeDtypeStruct
    return pl.pallas_call(
        body, name="s5_params_bwd",
        out_shape=[sds(lre.shape, F32), sds(lre.shape, F32), sds(ldt.shape, F32), sds(br.shape, F32),
                   sds(br.shape, F32)],
        compiler_params=pltpu.CompilerParams(vmem_limit_bytes=VMEM_LIMIT),
    )(lre, lim, ldt, br, bi, dar, dai, dbbr, dbbi)


def _cmul(a_re, a_im, b_re, b_im):
    return a_re * b_re - a_im * b_im, a_re * b_im + a_im * b_re


def _s5_tile(i):
    if isinstance(i, int):
        return pl.ds(i * SUBLANES, SUBLANES)
    return pl.ds(pl.multiple_of(i * SUBLANES, SUBLANES), SUBLANES)


def _s5_powers(a_re, a_im, pwr_ref, pwi_ref, seg):
    def step(i, carry):
        p_re, p_im = carry
        pwr_ref[pl.ds(i, 1), :] = p_re
        pwi_ref[pl.ds(i, 1), :] = p_im
        return _cmul(a_re, a_im, p_re, p_im)

    lax.fori_loop(0, seg, step, (a_re, a_im))


def _s5_fwd(proj, bdr, bdi, cdr, cdi, ar, ai, dvec):
    T = proj.shape[0]
    GB, UB, SB = bdr.shape
    tc = _tile(T, 256, SUBLANES * SUBLANES)
    seg = tc // SUBLANES

    def body(u_ref, bdr_ref, bdi_ref, cdr_ref, cdi_ref, ar_ref, ai_ref, d_ref, y_ref, sr_ref, si_ref,
             cr_ref, ci_ref, pwr_ref, pwi_ref, str_ref, sti_ref, up_ref, yp_ref):
        a_re = ar_ref[...]
        a_im = ai_ref[...]

        @pl.when(pl.program_id(1) == 0)
        def _():
            cr_ref[...] = jnp.zeros_like(cr_ref)
            ci_ref[...] = jnp.zeros_like(ci_ref)
            _s5_powers(a_re, a_im, pwr_ref, pwi_ref, seg)

        for i in range(seg):
            up_ref[_s5_tile(i), :] = u_ref[pl.ds(i, SUBLANES, stride=seg), :]
        u = up_ref[...]
        ub = u.astype(BF16)
        sr_ref[...] = _dot_nn(ub, bdr_ref[...])
        si_ref[...] = _dot_nn(ub, bdi_ref[...])

        def local(i, carry):
            p_re, p_im = carry
            rows = _s5_tile(i)
            n_re, n_im = _cmul(a_re, a_im, p_re, p_im)
            n_re, n_im = n_re + sr_ref[rows, :], n_im + si_ref[rows, :]
            sr_ref[rows, :] = n_re
            si_ref[rows, :] = n_im
            return n_re, n_im

        e_re, e_im = lax.fori_loop(1, seg, local, (sr_ref[_s5_tile(0), :], si_ref[_s5_tile(0), :]))
        s_re, s_im = cr_ref[...], ci_ref[...]
        top_re, top_im = pwr_ref[seg - 1:seg, :], pwi_ref[seg - 1:seg, :]
        for j in range(SUBLANES):
            str_ref[j:j + 1, :] = s_re
            sti_ref[j:j + 1, :] = s_im
            n_re, n_im = _cmul(top_re, top_im, s_re, s_im)
            s_re, s_im = n_re + e_re[j:j + 1, :], n_im + e_im[j:j + 1, :]
        cr_ref[...] = s_re
        ci_ref[...] = s_im
        b_re, b_im = str_ref[...], sti_ref[...]

        def fix(i, carry):
            rows = _s5_tile(i)
            f_re, f_im = _cmul(pwr_ref[pl.ds(i, 1), :], pwi_ref[pl.ds(i, 1), :], b_re, b_im)
            sr_ref[rows, :] += f_re
            si_ref[rows, :] += f_im
            return carry

        lax.fori_loop(0, seg, fix, 0)
        yp_ref[...] = (_dot_nn(sr_ref[...].astype(BF16), cdr_ref[...])
                       - _dot_nn(si_ref[...].astype(BF16), cdi_ref[...]) + d_ref[...] * u)
        for i in range(seg):
            y_ref[pl.ds(i, SUBLANES, stride=seg), :] = yp_ref[_s5_tile(i), :]

    return pl.pallas_call(
        body, name="s5_fwd", grid=(GB, T // tc),
        in_specs=[pl.BlockSpec((tc, UB), lambda j, t: (t, j)),
                  pl.BlockSpec((None, UB, SB), lambda j, t: (j, 0, 0)),
                  pl.BlockSpec((None, UB, SB), lambda j, t: (j, 0, 0)),
                  pl.BlockSpec((None, SB, UB), lambda j, t: (j, 0, 0)),
                  pl.BlockSpec((None, SB, UB), lambda j, t: (j, 0, 0)),
                  pl.BlockSpec((1, SB), lambda j, t: (0, j)),
                  pl.BlockSpec((1, SB), lambda j, t: (0, j)),
                  pl.BlockSpec((1, UB), lambda j, t: (0, j))],
        out_specs=[pl.BlockSpec((tc, UB), lambda j, t: (t, j)),
                   pl.BlockSpec((tc, SB), lambda j, t: (t, j)),
                   pl.BlockSpec((tc, SB), lambda j, t: (t, j))],
        out_shape=[jax.ShapeDtypeStruct((T, GB * UB), F32), jax.ShapeDtypeStruct((T, GB * SB), F32),
                   jax.ShapeDtypeStruct((T, GB * SB), F32)],
        scratch_shapes=[pltpu.VMEM((1, SB), F32), pltpu.VMEM((1, SB), F32),
                        pltpu.VMEM((seg, SB), F32), pltpu.VMEM((seg, SB), F32),
                        pltpu.VMEM((SUBLANES, SB), F32), pltpu.VMEM((SUBLANES, SB), F32),
                        pltpu.VMEM((tc, UB), F32), pltpu.VMEM((tc, UB), F32)],
        compiler_params=_params("arbitrary", "arbitrary"),
    )(proj, bdr, bdi, cdr, cdi, ar, ai, dvec)


def _s5_bwd(dy, proj, sr, si, bdr, bdi, cdr, cdi, ar, ai, dvec):
    T = dy.shape[0]
    GB, UB, SB = bdr.shape
    tc = _tile(T, 256, SUBLANES * SUBLANES)
    seg = tc // SUBLANES
    nt = T // tc
    halo_blocks = tc // SUBLANES

    def body(dy_ref, u_ref, sr_ref, si_ref, hr_ref, hi_ref, bdr_ref, bdi_ref, cdr_ref, cdi_ref, ar_ref, ai_ref,
             d_ref, du_ref, dbdr_ref, dbdi_ref, dcdr_ref, dcdi_ref, dar_ref, dai_ref, dd_ref,
             gr_ref, gi_ref, pwr_ref, pwi_ref, rtr_ref, rti_ref, cr_ref, ci_ref, dyp_ref, up_ref, dup_ref):
        step_no = pl.program_id(1)
        first_chunk = step_no == nt - 1
        a_re = ar_ref[...]
        a_im = ai_ref[...]

        @pl.when(step_no == 0)
        def _():
            for ref in (cr_ref, ci_ref, dbdr_ref, dbdi_ref, dcdr_ref, dcdi_ref, dar_ref, dai_ref, dd_ref):
                ref[...] = jnp.zeros_like(ref)
            _s5_powers(a_re, a_im, pwr_ref, pwi_ref, seg)

        for i in range(seg):
            dyp_ref[_s5_tile(i), :] = dy_ref[pl.ds(i, SUBLANES, stride=seg), :]
            up_ref[_s5_tile(i), :] = u_ref[pl.ds(i, SUBLANES, stride=seg), :]
        dy = dyp_ref[...]
        dyb = dy.astype(BF16)
        u = up_ref[...]
        gr_ref[...] = _dot_nt(dyb, cdr_ref[...])
        gi_ref[...] = -_dot_nt(dyb, cdi_ref[...])
        dcdr_ref[...] += _dot_tn(sr_ref[...].astype(BF16), dyb)
        dcdi_ref[...] -= _dot_tn(si_ref[...].astype(BF16), dyb)

        def local(n, carry):
            c_re, c_im = carry
            rows = _s5_tile(seg - 1 - n)
            g_re = gr_ref[rows, :] + a_re * c_re + a_im * c_im
            g_im = gi_ref[rows, :] + a_re * c_im - a_im * c_re
            gr_ref[rows, :] = g_re
            gi_ref[rows, :] = g_im
            return g_re, g_im

        last = _s5_tile(seg - 1)
        m_re, m_im = lax.fori_loop(1, seg, local, (gr_ref[last, :], gi_ref[last, :]))
        top_re, top_im = pwr_ref[seg - 1:seg, :], -pwi_ref[seg - 1:seg, :]
        r_re, r_im = cr_ref[...], ci_ref[...]
        for j in reversed(range(SUBLANES)):
            rtr_ref[j:j + 1, :] = r_re
            rti_ref[j:j + 1, :] = r_im
            n_re, n_im = _cmul(top_re, top_im, r_re, r_im)
            r_re, r_im = n_re + m_re[j:j + 1, :], n_im + m_im[j:j + 1, :]
        cr_ref[...] = r_re
        ci_ref[...] = r_im
        f_re, f_im = rtr_ref[...], rti_ref[...]
        keep = jnp.where(first_chunk, 0.0, 1.0)
        sub = lax.broadcasted_iota(jnp.int32, (SUBLANES, SB), 0)
        before_re = jnp.where(sub == 0, hr_ref[SUBLANES - 1:SUBLANES, :] * keep, pltpu.roll(sr_ref[last, :], 1, 0))
        before_im = jnp.where(sub == 0, hi_ref[SUBLANES - 1:SUBLANES, :] * keep, pltpu.roll(si_ref[last, :], 1, 0))

        def fix(i, p_re, p_im, acc):
            rows = _s5_tile(i)
            k = seg - 1 - i
            c_re, c_im = _cmul(pwr_ref[pl.ds(k, 1), :], -pwi_ref[pl.ds(k, 1), :], f_re, f_im)
            g_re = gr_ref[rows, :] + c_re
            g_im = gi_ref[rows, :] + c_im
            gr_ref[rows, :] = g_re
            gi_ref[rows, :] = g_im
            return acc[0] + p_re * g_re + p_im * g_im, acc[1] + p_re * g_im - p_im * g_re

        zero = jnp.zeros((SUBLANES, SB), F32)
        acc = fix(0, before_re, before_im, (zero, zero))
        acc = lax.fori_loop(
            1, seg, lambda i, acc: fix(i, sr_ref[_s5_tile(i - 1), :], si_ref[_s5_tile(i - 1), :], acc), acc)
        dar_ref[...] += _colsum(acc[0])
        dai_ref[...] += _colsum(acc[1])
        gsr = gr_ref[...].astype(BF16)
        gsi = gi_ref[...].astype(BF16)
        ub = u.astype(BF16)
        dbdr_ref[...] += _dot_tn(ub, gsr)
        dbdi_ref[...] += _dot_tn(ub, gsi)
        dup_ref[...] = _dot_nt(gsr, bdr_ref[...]) + _dot_nt(gsi, bdi_ref[...]) + d_ref[...] * dy
        for i in range(seg):
            du_ref[pl.ds(i, SUBLANES, stride=seg), :] = dup_ref[_s5_tile(i), :]
        dd_ref[...] += _colsum(dy * u)

    def rev(t):
        return nt - 1 - t

    def halo(j, t):
        return (jnp.maximum(rev(t) * halo_blocks - 1, 0), j)

    ublk = pl.BlockSpec((tc, UB), lambda j, t: (rev(t), j))
    sblk = pl.BlockSpec((tc, SB), lambda j, t: (rev(t), j))
    bd_spec = pl.BlockSpec((None, UB, SB), lambda j, t: (j, 0, 0))
    cd_spec = pl.BlockSpec((None, SB, UB), lambda j, t: (j, 0, 0))
    svec = pl.BlockSpec((1, SB), lambda j, t: (0, j))
    uvec = pl.BlockSpec((1, UB), lambda j, t: (0, j))
    sds = jax.ShapeDtypeStruct
    return pl.pallas_call(
        body, name="s5_bwd", grid=(GB, nt),
        in_specs=[ublk, ublk, sblk, sblk, pl.BlockSpec((SUBLANES, SB), halo), pl.BlockSpec((SUBLANES, SB), halo),
                  bd_spec, bd_spec, cd_spec, cd_spec, svec, svec, uvec],
        out_specs=[ublk, bd_spec, bd_spec, cd_spec, cd_spec, svec, svec, uvec],
        out_shape=[sds((T, GB * UB), F32), sds((GB, UB, SB), F32), sds((GB, UB, SB), F32),
                   sds((GB, SB, UB), F32), sds((GB, SB, UB), F32), sds((1, GB * SB), F32),
                   sds((1, GB * SB), F32), sds((1, GB * UB), F32)],
        scratch_shapes=[pltpu.VMEM((tc, SB), F32), pltpu.VMEM((tc, SB), F32),
                        pltpu.VMEM((seg, SB), F32), pltpu.VMEM((seg, SB), F32),
                        pltpu.VMEM((SUBLANES, SB), F32), pltpu.VMEM((SUBLANES, SB), F32),
                        pltpu.VMEM((1, SB), F32), pltpu.VMEM((1, SB), F32),
                        pltpu.VMEM((tc, UB), F32), pltpu.VMEM((tc, UB), F32), pltpu.VMEM((tc, UB), F32)],
        compiler_params=_params("arbitrary", "arbitrary"),
    )(dy, proj, sr, si, sr, si, bdr, bdi, cdr, cdi, ar, ai, dvec)


def _shift_down(v, k):
    rows = lax.broadcasted_iota(jnp.int32, v.shape, 0)
    return jnp.where(rows >= k, pltpu.roll(v, k, 0), 0.0)


def _shift_up(v, k):
    n = v.shape[0]
    rows = lax.broadcasted_iota(jnp.int32, v.shape, 0)
    return jnp.where(rows < n - k, pltpu.roll(v, n - k, 0), 0.0)


def _conv_specs(T, cb, n_s5_blocks, n_conv_blocks):
    gb = pl.BlockSpec((T, cb), lambda j: (0, n_s5_blocks + j))
    gc = pl.BlockSpec((T, cb), lambda j: (0, n_s5_blocks + n_conv_blocks + j))
    hh = pl.BlockSpec((T, cb), lambda j: (0, n_s5_blocks + 2 * n_conv_blocks + j))
    return gb, gc, hh


def _conv_fwd(proj, cw, cbias, d_s5, d_conv):
    T = proj.shape[0]
    cb = _tile(d_conv, 256, LANES)

    def body(gb_ref, gc_ref, hh_ref, w_ref, b_ref, z_ref):
        v = gc_ref[...] * hh_ref[...]
        w = w_ref[...]
        cv = b_ref[...] + w[0:1, :] * _shift_down(v, 2) + w[1:2, :] * _shift_down(v, 1) + w[2:3, :] * v
        z_ref[...] = gb_ref[...] * cv

    gb, gc, hh = _conv_specs(T, cb, d_s5 // cb, d_conv // cb)
    col = pl.BlockSpec((T, cb), lambda j: (0, j))
    return pl.pallas_call(
        body, name="conv_fwd", grid=(d_conv // cb,),
        in_specs=[gb, gc, hh, pl.BlockSpec((CONV_W, cb), lambda j: (0, j)), pl.BlockSpec((1, cb), lambda j: (0, j))],
        out_specs=col, out_shape=jax.ShapeDtypeStruct((T, d_conv), F32),
        compiler_params=_params("arbitrary"),
    )(proj, proj, proj, cw, cbias)


def _conv_bwd(dz, proj, cw, cbias, d_s5, d_conv):
    T = proj.shape[0]
    cb = _tile(d_conv, 256, LANES)

    def body(dz_ref, gb_ref, gc_ref, hh_ref, w_ref, b_ref, dgb_ref, dgc_ref, dhh_ref, dw_ref, db_ref):
        gc = gc_ref[...]
        hh = hh_ref[...]
        dz = dz_ref[...]
        w = w_ref[...]
        v = gc * hh
        v1 = _shift_down(v, 1)
        v2 = _shift_down(v, 2)
        cv = b_ref[...] + w[0:1, :] * v2 + w[1:2, :] * v1 + w[2:3, :] * v
        dgb_ref[...] = (dz * cv).astype(BF16)
        dcv = dz * gb_ref[...]
        dv = w[2:3, :] * dcv + w[1:2, :] * _shift_up(dcv, 1) + w[0:1, :] * _shift_up(dcv, 2)
        dgc_ref[...] = (dv * hh).astype(BF16)
        dhh_ref[...] = (dv * gc).astype(BF16)
        dw_ref[0:1, :] = _colsum(dcv * v2)
        dw_ref[1:2, :] = _colsum(dcv * v1)
        dw_ref[2:3, :] = _colsum(dcv * v)
        db_ref[...] = _colsum(dcv)

    gb, gc, hh = _conv_specs(T, cb, d_s5 // cb, d_conv // cb)
    col = pl.BlockSpec((T, cb), lambda j: (0, j))
    wspec = pl.BlockSpec((CONV_W, cb), lambda j: (0, j))
    bspec = pl.BlockSpec((1, cb), lambda j: (0, j))
    sds = jax.ShapeDtypeStruct
    return pl.pallas_call(
        body, name="conv_bwd", grid=(d_conv // cb,),
        in_specs=[col, gb, gc, hh, wspec, bspec],
        out_specs=[col, col, col, wspec, bspec],
        out_shape=[sds((T, d_conv), BF16), sds((T, d_conv), BF16), sds((T, d_conv), BF16),
                   sds((CONV_W, d_conv), F32), sds((1, d_conv), F32)],
        compiler_params=_params("arbitrary"),
    )(dz, proj, proj, proj, cw, cbias)


def _rms(v, g):
    rstd = lax.rsqrt(_rowmean(v * v) + RMS_EPS)
    return v * rstd * g, rstd


def _rms_bwd(dyn, v, rstd, g):
    w = dyn * g
    return rstd * w - v * (rstd * rstd * rstd) * _rowmean(w * v), _colsum(dyn * v * rstd)


def _mix_post(y, z, wglu, g_s5, g_conv):
    T, C = y.shape
    tm = _tile(T, 256, 16)

    def body(y_ref, z_ref, w_ref, gs_ref, gc_ref, m_ref, gl_ref):
        ge = jax.nn.gelu(y_ref[...])
        gl = _dot_nn(ge.astype(BF16), w_ref[...])
        gl_ref[...] = gl
        yn, _ = _rms(ge * jax.nn.sigmoid(gl), gs_ref[...])
        zn, _ = _rms(z_ref[...], gc_ref[...])
        m_ref[:, 0:C] = yn.astype(BF16)
        m_ref[:, C:2 * C] = zn.astype(BF16)

    row = pl.BlockSpec((tm, C), lambda i: (i, 0))
    vec = pl.BlockSpec((1, C), lambda i: (0, 0))
    return pl.pallas_call(
        body, name="mix_post", grid=(T // tm,),
        in_specs=[row, row, pl.BlockSpec((C, C), lambda i: (0, 0)), vec, vec],
        out_specs=[pl.BlockSpec((tm, 2 * C), lambda i: (i, 0)), row],
        out_shape=[jax.ShapeDtypeStruct((T, 2 * C), BF16), jax.ShapeDtypeStruct((T, C), F32)],
        compiler_params=_params("arbitrary"),
    )(y, z, wglu, g_s5, g_conv)


def _mix_post_bwd(dm, y, gl, z, wglu, g_s5, g_conv):
    T, C = y.shape
    tm = _tile(T, 256, 16)

    def body(dm_ref, y_ref, gl_ref, z_ref, w_ref, gs_ref, gc_ref, dy_ref, dz_ref, dw_ref, dgs_ref, dgc_ref):
        @pl.when(pl.program_id(0) == 0)
        def _():
            dw_ref[...] = jnp.zeros_like(dw_ref)
            dgs_ref[...] = jnp.zeros_like(dgs_ref)
            dgc_ref[...] = jnp.zeros_like(dgc_ref)

        yv = y_ref[...]
        ge, gelu_vjp = jax.vjp(jax.nn.gelu, yv)
        gl = gl_ref[...]
        sg = jax.nn.sigmoid(gl)
        y2 = ge * sg
        _, rstd_y = _rms(y2, gs_ref[...])
        dy2, dgs = _rms_bwd(dm_ref[:, 0:C], y2, rstd_y, gs_ref[...])
        dgs_ref[...] += dgs
        dgl = (dy2 * ge * sg * (1.0 - sg)).astype(BF16)
        dge = dy2 * sg + _dot_nt(dgl, w_ref[...])
        dw_ref[...] += _dot_tn(ge.astype(BF16), dgl)
        dy_ref[...] = gelu_vjp(dge)[0]
        zv = z_ref[...]
        _, rstd_z = _rms(zv, gc_ref[...])
        dz, dgc = _rms_bwd(dm_ref[:, C:2 * C], zv, rstd_z, gc_ref[...])
        dz_ref[...] = dz
        dgc_ref[...] += dgc

    row = pl.BlockSpec((tm, C), lambda i: (i, 0))
    vec = pl.BlockSpec((1, C), lambda i: (0, 0))
    full = pl.BlockSpec((C, C), lambda i: (0, 0))
    sds = jax.ShapeDtypeStruct
    return pl.pallas_call(
        body, name="mix_post_bwd", grid=(T // tm,),
        in_specs=[pl.BlockSpec((tm, 2 * C), lambda i: (i, 0)), row, row, row, full, vec, vec],
        out_specs=[row, row, full, vec, vec],
        out_shape=[sds((T, C), F32), sds((T, C), F32), sds((C, C), F32), sds((1, C), F32), sds((1, C), F32)],
        compiler_params=_params("arbitrary"),
    )(dm, y, gl, z, wglu, g_s5, g_conv)


def _adamw(w, g, m, v):
    m = ADAM_B1 * m + (1.0 - ADAM_B1) * g
    v = ADAM_B2 * v + (1.0 - ADAM_B2) * (g * g)
    m_hat = m / (1.0 - ADAM_B1 ** ADAM_STEP)
    v_hat = v / (1.0 - ADAM_B2 ** ADAM_STEP)
    return -ADAM_LR * (m_hat / (jnp.sqrt(v_hat) + ADAM_EPS) + ADAM_WD * w), m, v


def _sum_parts(p_ref):
    total = p_ref[0].astype(F32)
    for d in range(1, p_ref.shape[0]):
        total = total + p_ref[d].astype(F32)
    return total


def _row_tile(R, C, n_streams):
    budget = VMEM_LIMIT // 3 // (n_streams * C * 4)
    return _tile(R, max(BF16_ROWS, budget), BF16_ROWS)


def _reduce_parts(parts):
    L, P, R, C = parts.shape
    tr = _row_tile(R, C, P + 1)

    def body(p_ref, o_ref):
        o_ref[...] = _sum_parts(p_ref)

    return pl.pallas_call(
        body, name="reduce_parts", grid=(L, R // tr),
        in_specs=[pl.BlockSpec((None, P, tr, C), lambda l, i: (l, 0, i, 0))],
        out_specs=pl.BlockSpec((None, tr, C), lambda l, i: (l, i, 0)),
        out_shape=jax.ShapeDtypeStruct((L, R, C), F32),
        compiler_params=_params("arbitrary", "arbitrary"),
    )(parts)


def _adamw_update(w, m, v, grad=None, parts=None):
    L, R, C = w.shape
    from_parts = parts is not None
    P = parts.shape[1] if from_parts else 1
    tr = _row_tile(R, C, P + 7)

    def body(g_in_ref, w_ref, m_ref, v_ref, g_ref, d_ref, nm_ref, nv_ref):
        g = _sum_parts(g_in_ref) if from_parts else g_in_ref[...]
        delta, nm, nv = _adamw(w_ref[...], g, m_ref[...], v_ref[...])
        g_ref[...] = g
        d_ref[...] = delta
        nm_ref[...] = nm
        nv_ref[...] = nv

    blk = pl.BlockSpec((None, tr, C), lambda l, i: (l, i, 0))
    g_spec = pl.BlockSpec((None, P, tr, C), lambda l, i: (l, 0, i, 0)) if from_parts else blk
    out = jax.ShapeDtypeStruct((L, R, C), F32)
    return pl.pallas_call(
        body, name="adamw_parts" if from_parts else "adamw", grid=(L, R // tr),
        in_specs=[g_spec, blk, blk, blk], out_specs=[blk, blk, blk, blk], out_shape=[out, out, out, out],
        compiler_params=_params("arbitrary", "arbitrary"),
    )(parts if from_parts else grad, w, m, v)


def _pair_sum(full, stage):
    _, R, C = full.shape
    tr = _row_tile(R, C, 4)

    def body(f_ref, s_ref, o_ref):
        mine = f_ref[lax.axis_index("c")]
        o_ref[...] = (mine.astype(F32) + s_ref[...].astype(F32)).astype(BF16)

    blk = pl.BlockSpec((None, tr, C), lambda q, i: (q, i, 0))
    return pl.pallas_call(
        body, name="pair_sum", grid=(N_CHIPS, R // tr),
        in_specs=[pl.BlockSpec((None, 2, tr, C), lambda q, i: (q, 0, i, 0)), blk],
        out_specs=blk, out_shape=jax.ShapeDtypeStruct((N_CHIPS, R, C), BF16),
        compiler_params=_params("arbitrary", "arbitrary"),
    )(full.reshape(N_CHIPS, 2, R, C), stage)


def _me():
    x, y, c = (lax.axis_index(a) for a in AXES)
    return x, y, c, 4 * x + 2 * y + c


def _peer(rel):
    x, y, c, _ = _me()
    px = 1 - x if rel & 4 else x
    py = 1 - y if rel & 2 else y
    pc = 1 - c if rel & 1 else c
    return (px, py, pc), 4 * px + 2 * py + pc


DATAFLOW = pltpu.SideEffectType.DATAFLOW_SIDE_EFFECTING


def _remote_copy(src, dst, sems):
    return functools.partial(pltpu.make_async_remote_copy, src_ref=src, dst_ref=dst, **sems)


ALL_PEERS = tuple(range(1, N_DEV))
SIBLING = 1
OTHER_CHIPS = (2, 4, 6)
SIBLING_AND_OTHER_CHIPS = (SIBLING,) + OTHER_CHIPS


def _slot(dev, blk, same_core, own_core_last):
    if not own_core_last:
        return blk
    return (N_CHIPS if same_core else 0) + 2 * dev[0] + dev[1]


def _gather_copies(n, rels=ALL_PEERS, own_core_last=False):
    def copies(srcs, lands, send_sems, recv_sems, local_sems):
        x, y, c, me = _me()
        local, remote = [], []
        for k in range(n):
            local.append(functools.partial(pltpu.make_async_copy, srcs[k],
                                           lands[k].at[_slot((x, y, c), me, True, own_core_last)], local_sems.at[k]))
            for rel in rels:
                dev, blk = _peer(rel)
                same_core = not rel & SIBLING
                sems = dict(send_sem=send_sems.at[_sem_index(k, rel)], recv_sem=recv_sems.at[_sem_index(k, rel)],
                            device_id=dev, device_id_type=MESH)
                remote.append((_remote_copy(srcs[k], lands[k].at[_slot((x, y, c), me, same_core, own_core_last)], sems),
                               _remote_copy(srcs[k], lands[k].at[_slot(dev, blk, same_core, own_core_last)], sems)))
        return local, remote

    copies.n_arrays = n
    return copies


Y_NEIGHBOUR, X_NEIGHBOUR, DIAGONAL = 2, 4, 6
SIBLING_AND_NEIGHBOURS = (SIBLING, Y_NEIGHBOUR, X_NEIGHBOUR)


def _relay_copies(n, own_core_last=False):
    def copies(srcs, lands, send_sems, recv_sems, local_sems):
        diagonal = _slot(*_peer(DIAGONAL), True, own_core_last)
        remote = []
        for k in range(n):
            half = lands[k].shape[1] // 2
            for to, held, rows in ((X_NEIGHBOUR, Y_NEIGHBOUR, pl.ds(0, half)), (Y_NEIGHBOUR, X_NEIGHBOUR, pl.ds(half, half))):
                block = lands[k].at[_slot(*_peer(held), True, own_core_last), rows]
                sems = dict(send_sem=send_sems.at[_sem_index(k, to)], recv_sem=recv_sems.at[_sem_index(k, to)],
                            device_id=_peer(to)[0], device_id_type=MESH)
                remote.append((_remote_copy(block, block, sems), _remote_copy(block, lands[k].at[diagonal, rows], sems)))
        return [], remote

    copies.n_arrays = n
    return copies


def _forward_copies(n, own_core_last=False):
    def copies(srcs, lands, send_sems, recv_sems, local_sems):
        sibling = _peer(SIBLING)[0]
        remote = []
        for k in range(n):
            for rel in OTHER_CHIPS:
                dev, blk = _peer(rel)
                have = _slot(dev, blk, True, own_core_last)
                there = _slot(dev, blk, False, own_core_last)
                other = _peer(rel | SIBLING)
                comes = _slot(other[0], other[1], False, own_core_last)
                sems = dict(send_sem=send_sems.at[_sem_index(k, rel)], recv_sem=recv_sems.at[_sem_index(k, rel)],
                            device_id=sibling, device_id_type=MESH)
                remote.append((_remote_copy(lands[k].at[have], lands[k].at[there], sems),
                               _remote_copy(lands[k].at[have], lands[k].at[comes], sems)))
        return [], remote

    copies.n_arrays = n
    return copies


def _scatter_copies(n, layer):
    def copies(srcs, lands, send_sems, recv_sems, local_sems):
        me = _me()[3]
        local, remote = [], []
        for k in range(n):
            local.append(functools.partial(pltpu.make_async_copy, srcs[k].at[me], lands[k].at[layer, me],
                                           local_sems.at[k]))
            for rel in range(1, N_DEV):
                dev, blk = _peer(rel)
                sems = dict(send_sem=send_sems.at[_sem_index(k, rel)], recv_sem=recv_sems.at[_sem_index(k, rel)],
                            device_id=dev, device_id_type=MESH)
                remote.append((_remote_copy(srcs[k].at[blk], lands[k].at[layer, me], sems),
                               _remote_copy(srcs[k].at[blk], lands[k].at[layer, blk], sems)))
        return local, remote

    copies.n_arrays = n
    return copies


def _swap_copies(n):
    def copies(srcs, lands, send_sems, recv_sems, local_sems):
        sibling = _peer(SIBLING)[0]
        remote = []
        for k in range(n):
            sems = dict(send_sem=send_sems.at[_sem_index(k, SIBLING)], recv_sem=recv_sems.at[_sem_index(k, SIBLING)],
                        device_id=sibling, device_id_type=MESH)
            remote.append((_remote_copy(srcs[k], lands[k], sems), _remote_copy(srcs[k], lands[k], sems)))
        return [], remote

    copies.n_arrays = n
    return copies


def _pair_copies(n):
    def copies(srcs, lands, send_sems, recv_sems, local_sems):
        c = _me()[2]
        sibling = _peer(SIBLING)[0]
        remote = []
        for k in range(n):
            for chip in range(N_CHIPS):
                sems = dict(send_sem=send_sems.at[_sem_index(k, chip + 1)], recv_sem=recv_sems.at[_sem_index(k, chip + 1)],
                            device_id=sibling, device_id_type=MESH)
                block = srcs[k].at[2 * chip + 1 - c]
                remote.append((_remote_copy(block, lands[k].at[chip], sems), _remote_copy(block, lands[k].at[chip], sems)))
        return [], remote

    copies.n_arrays = n
    return copies


def _chip_scatter_copies(n, layer):
    def copies(srcs, lands, send_sems, recv_sems, local_sems):
        x, y, _, _ = _me()
        my_chip = 2 * x + y
        local, remote = [], []
        for k in range(n):
            local.append(functools.partial(pltpu.make_async_copy, srcs[k].at[my_chip], lands[k].at[layer, my_chip],
                                           local_sems.at[k]))
            for rel in OTHER_CHIPS:
                dev = _peer(rel)[0]
                chip = 2 * dev[0] + dev[1]
                sems = dict(send_sem=send_sems.at[_sem_index(k, rel)], recv_sem=recv_sems.at[_sem_index(k, rel)],
                            device_id=dev, device_id_type=MESH)
                remote.append((_remote_copy(srcs[k].at[chip], lands[k].at[layer, my_chip], sems),
                               _remote_copy(srcs[k].at[chip], lands[k].at[layer, chip], sems)))
        return local, remote

    copies.n_arrays = n
    return copies


def _sem_shapes(n):
    return [pltpu.SemaphoreType.DMA((n * (N_DEV - 1),)), pltpu.SemaphoreType.DMA((n * (N_DEV - 1),)),
            pltpu.SemaphoreType.DMA((n,))]


def _sem_index(k, rel):
    return k * (N_DEV - 1) + rel - 1


def _exchange(copies, name, srcs, lands):
    n_src, n_land = len(srcs), len(lands)

    def body(*refs):
        src_refs = refs[:n_src]
        land_refs = refs[n_src + n_land:n_src + 2 * n_land]
        local, remote = copies(src_refs, land_refs, *refs[n_src + 2 * n_land:])
        local = [cp() for cp in local]
        sends = [send() for send, _ in remote]
        for cp in local + sends:
            cp.start()
        for send, (_, landing) in zip(sends, remote):
            send.wait_send()
            landing().wait_recv()
        for cp in local:
            cp.wait()

    return pl.pallas_call(
        body, name=name, in_specs=[HBM_SPEC] * (n_src + n_land), out_specs=[HBM_SPEC] * n_land,
        out_shape=[jax.ShapeDtypeStruct(b.shape, b.dtype) for b in lands],
        scratch_shapes=_sem_shapes(copies.n_arrays),
        input_output_aliases={n_src + k: k for k in range(n_land)},
        compiler_params=pltpu.CompilerParams(has_side_effects=True),
    )(*srcs, *lands)


def _hbm(arrays):
    return [pltpu.with_memory_space_constraint(a, pltpu.HBM) for a in arrays]


def _exchange_start(copies, name, srcs, lands, after):
    n_src, n_land, n_after = len(srcs), len(lands), len(after)
    n_data = n_src + n_land

    def body(*refs):
        outs = refs[n_data + n_after:]
        local, remote = copies(refs[:n_src], refs[n_src:n_data], *outs[:3])
        for cp in local:
            cp().start()
        for send, _ in remote:
            send().start()
        outs[-1][...] = jnp.zeros_like(outs[-1])

    res = pl.pallas_call(
        body, name=name, in_specs=[HBM_SPEC] * n_data + [ANY_SPEC] * n_after,
        out_specs=[SEM_SPEC] * 3 + [HBM_SPEC] * n_data + [pl.BlockSpec(memory_space=pltpu.VMEM)],
        out_shape=_sem_shapes(copies.n_arrays) + [pltpu.HBM(a.shape, a.dtype) for a in list(srcs) + list(lands)]
        + [jax.ShapeDtypeStruct((SUBLANES, LANES), F32)],
        input_output_aliases={k: 3 + k for k in range(n_data)},
        compiler_params=pltpu.CompilerParams(has_side_effects=DATAFLOW),
    )(*_hbm(list(srcs) + list(lands)), *after)
    return res[:3], res[3:3 + n_src], res[3 + n_src:3 + n_data], res[-1]


def _exchange_wait(copies, name, sems, srcs, lands, after):
    n_src, n_land, n_after = len(srcs), len(lands), len(after)
    n_data = n_src + n_land

    def body(*refs):
        local, remote = copies(refs[:n_src], refs[n_src:n_data], *refs[n_data:n_data + 3])
        for send, landing in remote:
            send().wait_send()
            landing().wait_recv()
        for cp in local:
            cp().wait()

    res = pl.pallas_call(
        body, name=name, in_specs=[HBM_SPEC] * n_data + [SEM_SPEC] * 3 + [ANY_SPEC] * n_after,
        out_specs=[HBM_SPEC] * n_data,
        out_shape=[pltpu.HBM(a.shape, a.dtype) for a in list(srcs) + list(lands)],
        input_output_aliases={k: k for k in range(n_data)},
        compiler_params=pltpu.CompilerParams(has_side_effects=DATAFLOW),
    )(*srcs, *lands, *sems, *after)
    return res[:n_src], res[n_src:]


def _block_diag(blocks, row_major):
    L, GB, g, P, N = blocks.shape
    eye = jnp.eye(g, dtype=blocks.dtype)
    if row_major:
        return jnp.einsum("lbgpn,gh->lbgphn", blocks, eye).reshape(L, GB, g * P, g * N)
    return jnp.einsum("lbgpn,gh->lbhngp", blocks, eye).reshape(L, GB, g * N, g * P)


def _diag_blocks(mat, g, P, N, row_major):
    GB = mat.shape[0]
    eye = jnp.eye(g, dtype=mat.dtype)
    if row_major:
        return jnp.einsum("bgphn,gh->bgpn", mat.reshape(GB, g, P, g, N), eye)
    return jnp.einsum("bhngp,gh->bgpn", mat.reshape(GB, g, N, g, P), eye)


def _pack(arrays, rows_multiple):
    flat = jnp.concatenate([a.reshape(-1).astype(F32) for a in arrays])
    pad = (-flat.shape[0]) % (rows_multiple * LANES)
    return jnp.pad(flat, (0, pad)).reshape(-1, LANES)


def _unpack(packed, shapes):
    flat = packed.reshape(-1)
    out, pos = [], 0
    for s in shapes:
        n = math.prod(s)
        out.append(flat[pos:pos + n].reshape(s))
        pos += n
    return out


SMALL = ["ln1_g", "ln1_b", "s5_lam_re", "s5_lam_im", "s5_log_dt", "s5_b_re", "s5_b_im", "s5_c_re", "s5_c_im", "s5_d",
         "conv_b", "g_s5", "g_conv", "ln2_g", "ln2_b", "ln3_g", "ln3_b"]
S5_B = ["s5_b_re", "s5_b_im"]
WEIGHTS = ["ffn1_gate", "ffn1_up", "ffn1_down", "ln1_g", "ln1_b", "w_in", "s5_lam_re", "s5_lam_im", "s5_log_dt",
           "s5_b_re", "s5_b_im", "s5_c_re", "s5_c_im", "s5_d", "s5_w_glu", "conv_w", "conv_b", "g_s5", "g_conv",
           "w_out", "ln2_g", "ln2_b", "ffn2_gate", "ffn2_up", "ffn2_down", "ln3_g", "ln3_b"]
TRANSPOSED = ["ffn1_gate", "ffn1_up", "w_in", "ffn2_gate", "ffn2_up"]
UPDATED_TRANSPOSED = ["ffn1_gate", "ffn1_up", "ffn2_gate", "ffn2_up"]
GROUPS = {"a": ["ffn1_gate", "ffn1_up", "ffn1_down"], "b": ["w_in", "s5_w_glu", "w_out"],
          "c": ["ffn2_gate", "ffn2_up", "ffn2_down"]}
FFN_GROUPS = ("a", "c")


def _train_step(x, target, w, m, v):
    T, D = x.shape
    L = w["ln1_g"].shape[0]
    alpha = (2.0 * L) ** 0.25
    G = w["s5_log_dt"].shape[1]
    d_s5 = G * S5_P
    d_conv = w["conv_b"].shape[1]
    GB = G // S5_GROUPS_PER_BLOCK
    me = _me()[3]

    def shard(n, l):
        return (jnp.swapaxes(w[n][l], 0, 1) if n in TRANSPOSED else w[n][l]).astype(BF16)

    conv_w_rows = jnp.pad(w["conv_w"], ((0, 0), (0, SUBLANES - CONV_W), (0, 0)))
    (conv_w_all,) = _exchange(_gather_copies(1), "gather_conv_w", [conv_w_rows],
                              [lax.empty((N_DEV,) + conv_w_rows.shape, F32)])
    parts = [(l, grp) for l in range(L) for grp in GROUPS]
    step_one, relayed, step_two = {}, {}, {}

    def start_one(i, after):
        if i >= len(parts):
            return []
        l, grp = parts[i]
        srcs = [shard(n, l) for n in GROUPS[grp]]
        lands = [lax.empty((N_DEV,) + a.shape, a.dtype) for a in srcs]
        sems, srcs, lands, tok = _exchange_start(_gather_copies(len(srcs), SIBLING_AND_NEIGHBOURS, grp in FFN_GROUPS),
                                                 f"gather_start_{l}{grp}", srcs, lands, after)
        step_one[l, grp] = (sems, srcs, lands)
        return [tok]

    def relay_on(i, after):
        if i >= len(parts):
            return []
        l, grp = parts[i]
        sems, srcs, lands = step_one[l, grp]
        copies = _gather_copies(len(srcs), SIBLING_AND_NEIGHBOURS, grp in FFN_GROUPS)
        _, lands = _exchange_wait(copies, f"gather_wait_{l}{grp}", sems, srcs, lands, after)
        sems, _, lands, tok = _exchange_start(_relay_copies(len(lands), grp in FFN_GROUPS), f"relay_start_{l}{grp}",
                                              [], lands, [])
        relayed[l, grp] = (sems, lands)
        return [tok]

    def forward_on(i, after):
        if i >= len(parts):
            return []
        l, grp = parts[i]
        sems, lands = relayed[l, grp]
        _, lands = _exchange_wait(_relay_copies(len(lands), grp in FFN_GROUPS), f"relay_wait_{l}{grp}", sems, [], lands,
                                  after)
        sems, _, lands, tok = _exchange_start(_forward_copies(len(lands), grp in FFN_GROUPS), f"forward_start_{l}{grp}",
                                              [], lands, [])
        step_two[l, grp] = (sems, lands)
        return [tok]

    def gathered(i, after):
        l, grp = parts[i]
        sems, lands = step_two[l, grp]
        _, lands = _exchange_wait(_forward_copies(len(lands), grp in FFN_GROUPS), f"forward_wait_{l}{grp}", sems, [],
                                  lands, after)
        full = {n: p.reshape(-1, p.shape[-1]) for n, p in zip(GROUPS[grp], lands)}
        if grp == "b":
            full["conv_w"] = jnp.swapaxes(conv_w_all[:, l, :CONV_W, :], 0, 1).reshape(CONV_W, d_conv)
        return full

    lre = w["s5_lam_re"].reshape(L * G, S5_N)
    lim = w["s5_lam_im"].reshape(L * G, S5_N)
    ldt = w["s5_log_dt"].reshape(L * G, 1)
    b_re = jnp.transpose(w["s5_b_re"], (3, 0, 1, 2)).reshape(S5_P, L * G, S5_N)
    b_im = jnp.transpose(w["s5_b_im"], (3, 0, 1, 2)).reshape(S5_P, L * G, S5_N)
    ab_re, ab_im, bb_re, bb_im = _s5_params_fwd(lre, lim, ldt, b_re, b_im)

    def groups(bb):
        return jnp.transpose(bb.reshape(S5_P, L, GB, S5_GROUPS_PER_BLOCK, S5_N), (1, 2, 3, 0, 4))

    bd_re = _block_diag(groups(bb_re), True).astype(BF16)
    bd_im = _block_diag(groups(bb_im), True).astype(BF16)
    c_shape = (L, GB, S5_GROUPS_PER_BLOCK, S5_P, S5_N)
    cd_re = _block_diag(w["s5_c_re"].reshape(c_shape), False).astype(BF16)
    cd_im = _block_diag(w["s5_c_im"].reshape(c_shape), False).astype(BF16)
    a_re = ab_re.reshape(L, 1, G * S5_N)
    a_im = ab_im.reshape(L, 1, G * S5_N)
    d_vec = w["s5_d"].reshape(L, 1, d_s5)

    def vec(name, l):
        return w[name][l].reshape(1, -1)

    saved, weights = [], []
    x_in, x_in_b = x, x.astype(BF16)
    def advance(i, after):
        if i == 0:
            tokens = relay_on(1, after)
            return tokens + start_one(3, tokens)
        tokens = forward_on(i + 1, after)
        tokens = tokens + relay_on(i + 2, tokens or after)
        return tokens + start_one(i + 3, tokens or after)

    token = start_one(0, [])
    token = relay_on(0, token)
    token = start_one(1, token)
    token = start_one(2, token)
    token = forward_on(0, token)
    for l in range(L):
        i = len(GROUPS) * l
        gw = gathered(i, [x_in] if l else token)
        s = {"x0b": x_in_b}
        s["g1"], s["u1"], s["h1"] = _ffn_up(x_in_b, gw["ffn1_gate"], gw["ffn1_up"])
        x1, s["x1b"], s["xh1"], s["rstd1"] = _mm_res_ln(s["h1"], gw["ffn1_down"], x_in, vec("ln1_g", l),
                                                         vec("ln1_b", l), 0.5, alpha, advance(i, [s["h1"]]))
        behind = [s["x1b"]]
        if l == 0:
            behind = forward_on(1, behind)
            behind = behind + relay_on(2, behind)
        gw.update(gathered(i + 1, behind))
        s["proj"] = _mm_nt(s["x1b"], gw["w_in"])
        s["y"], s["sr"], s["si"] = _s5_fwd(s["proj"], bd_re[l], bd_im[l], cd_re[l], cd_im[l], a_re[l], a_im[l],
                                           d_vec[l])
        s["z"] = _conv_fwd(s["proj"], gw["conv_w"], vec("conv_b", l), d_s5, d_conv)
        s["mcat"], s["gl"] = _mix_post(s["y"], s["z"], gw["s5_w_glu"], vec("g_s5", l), vec("g_conv", l))
        x2, s["x2b"], s["xh2"], s["rstd2"] = _mm_res_ln(s["mcat"], gw["w_out"], x1, vec("ln2_g", l),
                                                         vec("ln2_b", l), 1.0, alpha, advance(i + 1, [s["mcat"]]))
        gw.update(gathered(i + 2, [s["x2b"]]))
        s["g2"], s["u2"], s["h2"] = _ffn_up(s["x2b"], gw["ffn2_gate"], gw["ffn2_up"])
        x3, x3b, s["xh3"], s["rstd3"] = _mm_res_ln(s["h2"], gw["ffn2_down"], x2, vec("ln3_g", l), vec("ln3_b", l),
                                                   0.5, alpha, advance(i + 2, [s["h2"]]))
        saved.append(s)
        weights.append(gw)
        x_in, x_in_b = x3, x3b

    last = saved[L - 1]
    dr, drb, dg, db, loss = _loss_ln_bwd(x_in, target, last["xh3"], last["rstd3"], vec("ln3_g", L - 1))
    small = [dict() for _ in range(L)]
    small[L - 1]["ln3_g"], small[L - 1]["ln3_b"] = dg, db
    bufs = {grp: [lax.empty((L, N_CHIPS) + shard(n, 0).shape, BF16) for n in names] for grp, names in GROUPS.items()}
    scatters = {grp: [] for grp in GROUPS}
    grad_x = None

    def chip_scatter(l, grp, sums):
        sems, sums, bufs[grp], tok = _exchange_start(_chip_scatter_copies(len(sums), l), f"scatter_start_{l}{grp}", sums,
                                                     bufs[grp], [])
        scatters[grp].append((l, sems, sums))
        return [tok]

    small_names = SMALL + ["conv_w"]
    b_shape = (L, G, S5_P, S5_N)
    small_shapes = [b_shape if n in S5_B else w[n].shape for n in SMALL] + [(L, CONV_W, d_conv)]

    def small_scatter_start(small):
        def stack(key):
            return jnp.stack([small[l][key] for l in range(L)])

        d_bb_re = jnp.transpose(stack("d_bb_re").reshape(L * G, S5_P, S5_N), (1, 0, 2))
        d_bb_im = jnp.transpose(stack("d_bb_im").reshape(L * G, S5_P, S5_N), (1, 0, 2))
        g_lre, g_lim, g_ldt, g_bre, g_bim = _s5_params_bwd(
            lre, lim, ldt, b_re, b_im, stack("d_ab_re").reshape(L * G, S5_N), stack("d_ab_im").reshape(L * G, S5_N),
            d_bb_re, d_bb_im)
        part = {n: [small[l][n] for l in range(L)]
                for n in ["ln1_g", "ln1_b", "s5_c_re", "s5_c_im", "s5_d", "conv_b", "g_s5", "g_conv", "ln2_g", "ln2_b",
                          "ln3_g", "ln3_b", "conv_w"]}
        part["s5_lam_re"], part["s5_lam_im"], part["s5_log_dt"] = [g_lre], [g_lim], [g_ldt]
        part["s5_b_re"] = [jnp.transpose(g_bre, (1, 0, 2))]
        part["s5_b_im"] = [jnp.transpose(g_bim, (1, 0, 2))]
        packed = _pack([piece for n in small_names for piece in part[n]], N_DEV * PACK_ROWS)
        rows = packed.shape[0] // N_DEV
        return _exchange_start(_scatter_copies(1, 0), "scatter_small_start", [packed.reshape(N_DEV, rows, LANES)],
                               [lax.empty((1, N_DEV, rows, LANES), F32)], [])

    token = []
    for l in reversed(range(L)):
        gw, s, sm = weights[l], saved[l], small[l]
        full = {}

        def ffn_bwd(dr, drb, tag, grp, xb_in, ln, after):
            dgp, dup = _ffn_down_bwd(drb, gw[f"ffn{tag}_down"], s[f"g{tag}"], s[f"u{tag}"], after)
            terms = {"gate": (dgp, xb_in, 1.0), "up": (dup, xb_in, 1.0), "down": (s[f"h{tag}"], drb, 0.5)}
            terms = [terms[n.split("_")[1]] for n in GROUPS[grp]]
            away = [_mm_tn(a, b, scale, BF16, half=0).reshape((N_CHIPS, -1, b.shape[1])) for a, b, scale in terms]
            stages = [lax.empty(a.shape, BF16) for a in away]
            sems, away, stages, tok = _exchange_start(_swap_copies(len(away)), f"pair_start_{l}{grp}", away, stages, [])
            res = _mm_dx([(dgp, gw[f"ffn{tag}_gate"]), (dup, gw[f"ffn{tag}_up"])], dr, alpha, ln, [tok])
            _, stages = _exchange_wait(_swap_copies(len(away)), f"pair_wait_{l}{grp}", sems, away, stages, [res[0]])
            sums = [_mm_tn(a, b, scale, BF16, half=1, addend=st.reshape(-1, b.shape[1])).reshape(st.shape)
                    for (a, b, scale), st in zip(terms, stages)]
            return res, chip_scatter(l, grp, sums)

        (dr, drb, sm["ln2_g"], sm["ln2_b"]), token = ffn_bwd(dr, drb, 2, "c", s["x2b"],
                                                              (s["xh2"], s["rstd2"], vec("ln2_g", l)), token)
        dm = _mm_nt(drb, gw["w_out"], token)
        full["w_out"] = _mm_tn(s["mcat"], drb, 1.0, BF16)
        dy, dz, dwglu, sm["g_s5"], sm["g_conv"] = _mix_post_bwd(dm, s["y"], s["gl"], s["z"], gw["s5_w_glu"],
                                                                vec("g_s5", l), vec("g_conv", l))
        full["s5_w_glu"] = dwglu.astype(BF16)
        du, dbd_re, dbd_im, dcd_re, dcd_im, sm["d_ab_re"], sm["d_ab_im"], sm["s5_d"] = _s5_bwd(
            dy, s["proj"], s["sr"], s["si"], bd_re[l], bd_im[l], cd_re[l], cd_im[l], a_re[l], a_im[l], d_vec[l])
        gsz = (S5_GROUPS_PER_BLOCK, S5_P, S5_N)
        sm["d_bb_re"] = _diag_blocks(dbd_re, *gsz, True)
        sm["d_bb_im"] = _diag_blocks(dbd_im, *gsz, True)
        sm["s5_c_re"] = _diag_blocks(dcd_re, *gsz, False).reshape(G, S5_P, S5_N)
        sm["s5_c_im"] = _diag_blocks(dcd_im, *gsz, False).reshape(G, S5_P, S5_N)
        dgb, dgc, dhh, sm["conv_w"], sm["conv_b"] = _conv_bwd(dz, s["proj"], gw["conv_w"], vec("conv_b", l),
                                                              d_s5, d_conv)
        dproj = jnp.concatenate([du.astype(BF16), dgb, dgc, dhh], axis=1)
        full["w_in"] = _mm_tn(dproj, s["x1b"], 1.0, BF16)
        fulls = [full[n].reshape((N_DEV, -1) + full[n].shape[1:]) for n in GROUPS["b"]]
        stages = [lax.empty((N_CHIPS,) + f.shape[1:], BF16) for f in fulls]
        sems, fulls, stages, tok = _exchange_start(_pair_copies(len(fulls)), f"pair_start_{l}b", fulls, stages, [])
        dr, drb, sm["ln1_g"], sm["ln1_b"] = _mm_dx([(dproj, gw["w_in"])], dr, alpha,
                                                   (s["xh1"], s["rstd1"], vec("ln1_g", l)), [tok])
        fulls, stages = _exchange_wait(_pair_copies(len(fulls)), f"pair_wait_{l}b", sems, fulls, stages, [dr])
        token = chip_scatter(l, "b", [_pair_sum(f, st) for f, st in zip(fulls, stages)])
        if l > 0:
            prev = saved[l - 1]
            (dr, drb, small[l - 1]["ln3_g"], small[l - 1]["ln3_b"]), token = ffn_bwd(
                dr, drb, 1, "a", s["x0b"], (prev["xh3"], prev["rstd3"], vec("ln3_g", l - 1)), token)
        else:
            small_sems, small_srcs, small_lands, tok = small_scatter_start(small)
            ((grad_x,), token) = ffn_bwd(dr, drb, 1, "a", s["x0b"], None, token + [tok])

    _, landed = _exchange_wait(_scatter_copies(1, 0), "scatter_small_wait", small_sems, small_srcs, small_lands,
                               [grad_x])
    mine = _reduce_parts(landed[0])
    (summed,) = _exchange(_gather_copies(1), "gather_small", [mine[0]], [lax.empty(landed[0].shape[1:], F32)])
    small_grads = dict(zip(small_names, _unpack(summed, small_shapes)))

    out = {}

    def update(name, w3, m3, v3, shape, **grad):
        res = _adamw_update(w3, m3, v3, **grad)
        out[name] = [r.reshape(shape) for r in res]
        return res

    cw_shape = w["conv_w"].shape
    g_cw = lax.dynamic_slice_in_dim(small_grads["conv_w"], me * cw_shape[2], cw_shape[2], axis=2)
    done = [update("conv_w", w["conv_w"], m["conv_w"], v["conv_w"], cw_shape, grad=g_cw)[3]]
    for n in SMALL:
        if n in S5_B:
            def view(a):
                return jnp.swapaxes(a, 2, 3).reshape(1, -1, S5_N)
            res = _adamw_update(view(w[n]), view(m[n]), view(v[n]), grad=small_grads[n].reshape(1, -1, S5_N))
            out[n] = [jnp.swapaxes(r.reshape(b_shape), 2, 3) for r in res]
        else:
            res = update(n, *(a[n].reshape(1, -1, a[n].shape[-1]) for a in (w, m, v)), w[n].shape,
                         grad=small_grads[n].reshape(1, -1, w[n].shape[-1]))
        done.append(res[3])
    for grp in ("c", "b", "a"):
        for l, sems, sums in scatters[grp]:
            _, bufs[grp] = _exchange_wait(_chip_scatter_copies(len(sums), l), f"scatter_wait_{l}{grp}", sems, sums,
                                          bufs[grp], [grad_x] + token + done)
        done = []
        for n, parts in zip(GROUPS[grp], bufs[grp]):
            if n in UPDATED_TRANSPOSED:
                res = _adamw_update(*(jnp.swapaxes(a[n], 1, 2) for a in (w, m, v)), parts=parts)
                out[n] = [jnp.swapaxes(r, 1, 2) for r in res]
            elif n in TRANSPOSED:
                res = update(n, w[n], m[n], v[n], w[n].shape, grad=jnp.swapaxes(_reduce_parts(parts), 1, 2))
            else:
                res = update(n, w[n], m[n], v[n], w[n].shape, parts=parts)
            done.append(res[3])

    loss = lax.psum(loss[0, 0], AXES)
    return loss, grad_x, out


def kernel(x, ffn1_gate, ffn1_up, ffn1_down, ln1_g, ln1_b, w_in, s5_lam_re, s5_lam_im, s5_log_dt, s5_b_re, s5_b_im, s5_c_re, s5_c_im, s5_d, s5_w_glu, conv_w, conv_b, g_s5, g_conv, w_out, ln2_g, ln2_b, ffn2_gate, ffn2_up, ffn2_down, ln3_g, ln3_b, loss_target, m_ffn1_gate, m_ffn1_up, m_ffn1_down, m_ln1_g, m_ln1_b, m_w_in, m_s5_lam_re, m_s5_lam_im, m_s5_log_dt, m_s5_b_re, m_s5_b_im, m_s5_c_re, m_s5_c_im, m_s5_d, m_s5_w_glu, m_conv_w, m_conv_b, m_g_s5, m_g_conv, m_w_out, m_ln2_g, m_ln2_b, m_ffn2_gate, m_ffn2_up, m_ffn2_down, m_ln3_g, m_ln3_b, v_ffn1_gate, v_ffn1_up, v_ffn1_down, v_ln1_g, v_ln1_b, v_w_in, v_s5_lam_re, v_s5_lam_im, v_s5_log_dt, v_s5_b_re, v_s5_b_im, v_s5_c_re, v_s5_c_im, v_s5_d, v_s5_w_glu, v_conv_w, v_conv_b, v_g_s5, v_g_conv, v_w_out, v_ln2_g, v_ln2_b, v_ffn2_gate, v_ffn2_up, v_ffn2_down, v_ln3_g, v_ln3_b):
    given = dict(locals())
    w = {n: given[n] for n in WEIGHTS}
    m = {n: given["m_" + n] for n in WEIGHTS}
    v = {n: given["v_" + n] for n in WEIGHTS}
    T, D = x.shape[-2:]
    loss, grad_x, out = _train_step(x.reshape(T, D), loss_target.reshape(T, D), w, m, v)
    results = [loss, grad_x.reshape(x.shape)]
    for i in range(4):
        results += [out[n][i] for n in WEIGHTS]
    return tuple(results)
```
